```python
import jax, jax.numpy as jnp
from jax import lax
import numpy as np


D_MODEL = 1024
BATCH = 8
SEQ = 8192
DEPTH = 2

GRID_W = 64
CTX_LEN = 256
N_MOD = 9
NA_HEADS = 8
HEAD_DIM = 64
NA_WIDTH = NA_HEADS * HEAD_DIM
NA_KH = 8
NA_KW = 16
POOL_GROUPS = 4
POOL_CH = 128
POOL_WIDTH = POOL_GROUPS * POOL_CH
POOL_WINDOWS = (2, 4, 8, 16)
MIX_WIDTH = NA_WIDTH + POOL_WIDTH
IN_WIDTH = 3 * NA_WIDTH + POOL_WIDTH
D_FF = 2816
ROPE_THETA = 10000.0
ROPE_PAIRS = HEAD_DIM // 4
RMS_EPS = 1e-6
NEG_INF = -1e30

kernel_name = 'hybrid_na_pool_macaron_dit_block'


def rms_norm(x, g):
    xf = x.astype(jnp.float32)
    y = xf * lax.rsqrt(jnp.mean(xf * xf, axis=-1, keepdims=True) + RMS_EPS)
    return (y * g.astype(jnp.float32)).astype(x.dtype)


def modulate(h, shift, scale):
    return h * (1.0 + scale) + shift


def mod_vectors(cvec, w_mod, b_mod):
    m = jax.nn.silu(cvec) @ w_mod + b_mod
    return jnp.split(m, N_MOD, axis=-1)


def swiglu(h, w_gate_up, w_down):
    gate, up = jnp.split(h @ w_gate_up, 2, axis=-1)
    return (jax.nn.silu(gate) * up) @ w_down


def sandwich_ffn(x, w_gate_up, w_down, g_pre, g_post, shift, scale, gate):
    h = modulate(rms_norm(x, g_pre), shift, scale)
    return x + 0.5 * gate * rms_norm(swiglu(h, w_gate_up, w_down), g_post)


def axial_rope(x, rows):
    seq = rows * GRID_W
    t = jnp.arange(seq)
    inv = ROPE_THETA ** (-jnp.arange(ROPE_PAIRS, dtype=jnp.float32) / ROPE_PAIRS)

    def rot(xa, pos):
        ang = pos.astype(jnp.float32)[:, None] * inv
        cos = jnp.cos(ang)[:, None, :]
        sin = jnp.sin(ang)[:, None, :]
        x1, x2 = jnp.split(xa, 2, axis=-1)
        return jnp.concatenate([x1 * cos - x2 * sin, x2 * cos + x1 * sin], axis=-1)

    xr, xc = jnp.split(x.astype(jnp.float32), 2, axis=-1)
    return jnp.concatenate([rot(xr, t // GRID_W), rot(xc, t % GRID_W)], axis=-1).astype(x.dtype)


def neighbourhood_attention(q, k, v, kc, vc, rpb, rows):
    b, s, h, dh = q.shape
    kh = min(NA_KH, rows)
    scale = dh ** -0.5
    q = q.reshape(b, rows, GRID_W, h, dh)
    k = k.reshape(b, rows, GRID_W, h, dh)
    v = v.reshape(b, rows, GRID_W, h, dh)
    r = np.arange(rows)
    row_start = np.clip(r - kh // 2, 0, rows - kh)
    row_idx = row_start[:, None] + np.arange(kh)[None, :]
    k_blk = k[:, row_idx]
    v_blk = v[:, row_idx]
    j = np.arange(GRID_W)
    col_start = np.clip(j - NA_KW // 2, 0, GRID_W - NA_KW)
    col_valid = (j[None, :] >= col_start[:, None]) & (j[None, :] < col_start[:, None] + NA_KW)
    dr = row_idx - r[:, None]
    dc = np.clip(j[None, :] - j[:, None] + NA_KW - 1, 0, 2 * NA_KW - 2)
    bias = rpb[:, dr + NA_KH - 1]
    bias = bias[..., dc]
    bias = jnp.transpose(bias, (0, 1, 3, 2, 4)).astype(jnp.float32)
    s_loc = jnp.einsum('brqhd,brkchd->bhrqkc', q, k_blk, preferred_element_type=jnp.float32) * scale
    s_loc = jnp.where(col_valid[:, None, :], s_loc + bias, NEG_INF)
    s_ctx = jnp.einsum('brqhd,bkhd->bhrqk', q, kc, preferred_element_type=jnp.float32) * scale
    n_loc = kh * GRID_W
    scores = jnp.concatenate([s_loc.reshape(b, h, rows, GRID_W, n_loc), s_ctx], axis=-1)
    p = jax.nn.softmax(scores, axis=-1).astype(v.dtype)
    p_loc = p[..., :n_loc].reshape(b, h, rows, GRID_W, kh, GRID_W)
    p_ctx = p[..., n_loc:]
    o = jnp.einsum('bhrqkc,brkchd->brqhd', p_loc, v_blk) + jnp.einsum('bhrqk,bkhd->brqhd', p_ctx, vc)
    return o.reshape(b, s, h * dh)


def context_attention(q, k, v):
    b, l, h, dh = q.shape
    sc = jnp.einsum('bqhd,bkhd->bhqk', q, k, preferred_element_type=jnp.float32) * dh ** -0.5
    p = jax.nn.softmax(sc, axis=-1).astype(v.dtype)
    return jnp.einsum('bhqk,bkhd->bqhd', p, v).reshape(b, l, h * dh)


def pool_mix(u, w_pool, pool_scale):
    length = u.shape[-2]
    ug = u.reshape(u.shape[:-1] + (POOL_GROUPS, POOL_CH))
    uf = ug.astype(jnp.float32)
    cs = jnp.cumsum(uf, axis=-3)
    pad = [(0, 0)] * cs.ndim
    pad[-3] = (1, 0)
    cs = jnp.pad(cs, pad)
    t = np.arange(length)
    outs = []
    for g, w in enumerate(POOL_WINDOWS):
        lo = np.clip(t - w // 2, 0, length)
        hi = np.clip(t - w // 2 + w, 0, length)
        cnt = (hi - lo).astype(np.float32)[:, None]
        csg = cs[..., g, :]
        outs.append((jnp.take(csg, hi, axis=-2) - jnp.take(csg, lo, axis=-2)) / cnt)
    pooled = jnp.stack(outs, axis=-2)
    d = (pooled - uf).astype(u.dtype)
    y = jnp.einsum('...gc,gcd->...gd', d, w_pool) * pool_scale.reshape(POOL_GROUPS, POOL_CH)
    return y.reshape(u.shape)


def token_mix(hx, hc, w_in, w_out, rpb, w_pool, pool_scale, with_ctx_out):
    b, s, _ = hx.shape
    rows = s // GRID_W
    l = hc.shape[1]
    qx, kx, vx, ux = jnp.split(hx @ w_in, [NA_WIDTH, 2 * NA_WIDTH, 3 * NA_WIDTH], axis=-1)
    if with_ctx_out:
        qc, kc, vc, uc = jnp.split(hc @ w_in, [NA_WIDTH, 2 * NA_WIDTH, 3 * NA_WIDTH], axis=-1)
    else:
        kc, vc = jnp.split(hc @ w_in[:, NA_WIDTH:3 * NA_WIDTH], 2, axis=-1)
    heads = lambda t, n: t.reshape(b, n, NA_HEADS, HEAD_DIM)
    qx = axial_rope(heads(qx, s), rows)
    kx = axial_rope(heads(kx, s), rows)
    kc_h, vc_h = heads(kc, l), heads(vc, l)
    na_x = neighbourhood_attention(qx, kx, heads(vx, s), kc_h, vc_h, rpb, rows)
    pool_x = pool_mix(ux.reshape(b, rows, GRID_W, POOL_WIDTH), w_pool, pool_scale).reshape(b, s, POOL_WIDTH)
    out_x = jnp.concatenate([na_x, pool_x], axis=-1) @ w_out
    if with_ctx_out:
        na_c = context_attention(heads(qc, l), kc_h, vc_h)
        pool_c = pool_mix(uc, w_pool, pool_scale)
        out_c = jnp.concatenate([na_c, pool_c], axis=-1) @ w_out
        return out_x, out_c
    return out_x, None


def _fwd_setup_inputs(seed: int = 0) -> dict:
    key = jax.random.key(seed)
    ks = jax.random.split(key, 14)
    nrm = jax.random.normal
    f32 = jnp.float32
    return {
        'x': nrm(ks[0], (BATCH, SEQ, D_MODEL), f32),
        'c': nrm(ks[1], (BATCH, D_MODEL), f32),
        'ctx': nrm(ks[2], (BATCH, CTX_LEN, D_MODEL), f32),
        'c_ctx': nrm(ks[3], (D_MODEL,), f32),
        'w_mod': nrm(ks[4], (DEPTH, D_MODEL, N_MOD * D_MODEL), f32) * (0.5 * D_MODEL ** -0.5),
        'b_mod': nrm(ks[5], (DEPTH, N_MOD * D_MODEL), f32) * 0.01,
        'norm_g': 1.0 + 0.05 * nrm(ks[6], (DEPTH, 6, D_MODEL), f32),
        'w_ffn_gate_up': nrm(ks[7], (DEPTH, 2, D_MODEL, 2 * D_FF), f32) * D_MODEL ** -0.5,
        'w_ffn_down': nrm(ks[8], (DEPTH, 2, D_FF, D_MODEL), f32) * D_FF ** -0.5,
        'w_in': nrm(ks[9], (DEPTH, D_MODEL, IN_WIDTH), f32) * D_MODEL ** -0.5,
        'w_out': nrm(ks[10], (DEPTH, MIX_WIDTH, D_MODEL), f32) * MIX_WIDTH ** -0.5,
        'na_rpb': nrm(ks[11], (DEPTH, NA_HEADS, 2 * NA_KH - 1, 2 * NA_KW - 1), f32) * 0.1,
        'w_pool': nrm(ks[12], (DEPTH, POOL_GROUPS, POOL_CH, POOL_CH), f32) * POOL_CH ** -0.5,
        'pool_scale': 1.0 + 0.05 * nrm(ks[13], (DEPTH, POOL_WIDTH), f32),
    }


def _fwd_reference(x, c, ctx, c_ctx, w_mod, b_mod, norm_g, w_ffn_gate_up, w_ffn_down, w_in, w_out, na_rpb, w_pool, pool_scale):
    for l in range(DEPTH):
        last = l == DEPTH - 1
        mx = [m[:, None, :] for m in mod_vectors(c, w_mod[l], b_mod[l])]
        mc = mod_vectors(c_ctx, w_mod[l], b_mod[l])
        g = norm_g[l]
        x = sandwich_ffn(x, w_ffn_gate_up[l, 0], w_ffn_down[l, 0], g[0], g[1], mx[0], mx[1], mx[2])
        ctx = sandwich_ffn(ctx, w_ffn_gate_up[l, 0], w_ffn_down[l, 0], g[0], g[1], mc[0], mc[1], mc[2])
        hx = modulate(rms_norm(x, g[2]), mx[3], mx[4])
        hc = modulate(rms_norm(ctx, g[2]), mc[3], mc[4])
        out_x, out_c = token_mix(hx, hc, w_in[l], w_out[l], na_rpb[l], w_pool[l], pool_scale[l], not last)
        x = x + mx[5] * rms_norm(out_x, g[3])
        x = sandwich_ffn(x, w_ffn_gate_up[l, 1], w_ffn_down[l, 1], g[4], g[5], mx[6], mx[7], mx[8])
        if not last:
            ctx = ctx + mc[5] * rms_norm(out_c, g[3])
            ctx = sandwich_ffn(ctx, w_ffn_gate_up[l, 1], w_ffn_down[l, 1], g[4], g[5], mc[6], mc[7], mc[8])
    return x


import jax as _jax
import jax.numpy as _jnp

TWIN_FORMAT = 'train_step'
FWD_PARAMS = ['x', 'c', 'ctx', 'c_ctx', 'w_mod', 'b_mod', 'norm_g', 'w_ffn_gate_up', 'w_ffn_down', 'w_in', 'w_out', 'na_rpb', 'w_pool', 'pool_scale']
TWIN_WEIGHTS = ['c_ctx', 'w_mod', 'b_mod', 'norm_g', 'w_ffn_gate_up', 'w_ffn_down', 'w_in', 'w_out', 'na_rpb', 'w_pool', 'pool_scale']
TWIN_DIFF_INPUT = 'x'
TWIN_INPUTS = ['x', 'c', 'ctx', 'c_ctx', 'w_mod', 'b_mod', 'norm_g', 'w_ffn_gate_up', 'w_ffn_down', 'w_in', 'w_out', 'na_rpb', 'w_pool', 'pool_scale', 'loss_target', 'm_c_ctx', 'm_w_mod', 'm_b_mod', 'm_norm_g', 'm_w_ffn_gate_up', 'm_w_ffn_down', 'm_w_in', 'm_w_out', 'm_na_rpb', 'm_w_pool', 'm_pool_scale', 'v_c_ctx', 'v_w_mod', 'v_b_mod', 'v_norm_g', 'v_w_ffn_gate_up', 'v_w_ffn_down', 'v_w_in', 'v_w_out', 'v_na_rpb', 'v_w_pool', 'v_pool_scale']
TWIN_OUTPUTS = ['loss', 'grad_x', 'grad_c_ctx', 'grad_w_mod', 'grad_b_mod', 'grad_norm_g', 'grad_w_ffn_gate_up', 'grad_w_ffn_down', 'grad_w_in', 'grad_w_out', 'grad_na_rpb', 'grad_w_pool', 'grad_pool_scale', 'delta_c_ctx', 'delta_w_mod', 'delta_b_mod', 'delta_norm_g', 'delta_w_ffn_gate_up', 'delta_w_ffn_down', 'delta_w_in', 'delta_w_out', 'delta_na_rpb', 'delta_w_pool', 'delta_pool_scale', 'new_m_c_ctx', 'new_m_w_mod', 'new_m_b_mod', 'new_m_norm_g', 'new_m_w_ffn_gate_up', 'new_m_w_ffn_down', 'new_m_w_in', 'new_m_w_out', 'new_m_na_rpb', 'new_m_w_pool', 'new_m_pool_scale', 'new_v_c_ctx', 'new_v_w_mod', 'new_v_b_mod', 'new_v_norm_g', 'new_v_w_ffn_gate_up', 'new_v_w_ffn_down', 'new_v_w_in', 'new_v_w_out', 'new_v_na_rpb', 'new_v_w_pool', 'new_v_pool_scale']
TWIN_LEAF_KINDS = {'loss': 'loss', 'grad_x': 'grad_x', 'grad_c_ctx': 'grad_w', 'grad_w_mod': 'grad_w', 'grad_b_mod': 'grad_w', 'grad_norm_g': 'grad_w', 'grad_w_ffn_gate_up': 'grad_w', 'grad_w_ffn_down': 'grad_w', 'grad_w_in': 'grad_w', 'grad_w_out': 'grad_w', 'grad_na_rpb': 'grad_w', 'grad_w_pool': 'grad_w', 'grad_pool_scale': 'grad_w', 'delta_c_ctx': 'delta_w', 'delta_w_mod': 'delta_w', 'delta_b_mod': 'delta_w', 'delta_norm_g': 'delta_w', 'delta_w_ffn_gate_up': 'delta_w', 'delta_w_ffn_down': 'delta_w', 'delta_w_in': 'delta_w', 'delta_w_out': 'delta_w', 'delta_na_rpb': 'delta_w', 'delta_w_pool': 'delta_w', 'delta_pool_scale': 'delta_w', 'new_m_c_ctx': 'new_m', 'new_m_w_mod': 'new_m', 'new_m_b_mod': 'new_m', 'new_m_norm_g': 'new_m', 'new_m_w_ffn_gate_up': 'new_m', 'new_m_w_ffn_down': 'new_m', 'new_m_w_in': 'new_m', 'new_m_w_out': 'new_m', 'new_m_na_rpb': 'new_m', 'new_m_w_pool': 'new_m', 'new_m_pool_scale': 'new_m', 'new_v_c_ctx': 'new_v', 'new_v_w_mod': 'new_v', 'new_v_b_mod': 'new_v', 'new_v_norm_g': 'new_v', 'new_v_w_ffn_gate_up': 'new_v', 'new_v_w_ffn_down': 'new_v', 'new_v_w_in': 'new_v', 'new_v_w_out': 'new_v', 'new_v_na_rpb': 'new_v', 'new_v_w_pool': 'new_v', 'new_v_pool_scale': 'new_v'}


def _forward(args):
    return _fwd_reference(*[args[k] for k in FWD_PARAMS])


def _output_shape():
    def fwd():
        inp = _fwd_setup_inputs(0)
        return _fwd_reference(*[inp[k] for k in FWD_PARAMS])
    out = _jax.eval_shape(fwd)
    return out.shape, out.dtype

N_MICROBATCH = 1
ADAM_LR = 0.001
ADAM_B1 = 0.9
ADAM_B2 = 0.999
ADAM_EPS = 1e-08
ADAM_WD = 0.01
ADAM_STEP = 10
PER_EXAMPLE_BATCH_AXIS = {'x': 0, 'c': 0, 'ctx': 0, 'loss_target': 0}
SHARED_INPUTS = []
_WEIGHT_DTYPES = {'c_ctx': _jnp.float32, 'w_mod': _jnp.float32, 'b_mod': _jnp.float32, 'norm_g': _jnp.float32, 'w_ffn_gate_up': _jnp.float32, 'w_ffn_down': _jnp.float32, 'w_in': _jnp.float32, 'w_out': _jnp.float32, 'na_rpb': _jnp.float32, 'w_pool': _jnp.float32, 'pool_scale': _jnp.float32}
MOMENT_SCALE = {'c_ctx': 1.824031e-01, 'w_mod': 1.490144e+00, 'b_mod': 3.261525e+00, 'norm_g': 2.844164e+00, 'w_ffn_gate_up': 4.484650e-02, 'w_ffn_down': 7.924602e-02, 'w_in': 1.865706e-01, 'w_out': 2.997748e-01, 'na_rpb': 5.303365e-03, 'w_pool': 3.246670e-01, 'pool_scale': 3.710635e-01}


def _to_microbatches(a, axis):
    t = _jnp.moveaxis(a, axis, 0)
    t = t.reshape((N_MICROBATCH, t.shape[0] // N_MICROBATCH) + t.shape[1:])
    return _jnp.moveaxis(t, 1, axis + 1)


def setup_inputs(seed: int = 0) -> dict:
    inp = _fwd_setup_inputs(seed)
    key = _jax.random.fold_in(_jax.random.key(seed), 7919)
    shape, _ = _output_shape()
    out = dict(inp)
    out["loss_target"] = _jax.random.normal(_jax.random.fold_in(key, 0), shape, _jnp.float32)
    for i, name in enumerate(TWIN_WEIGHTS):
        w = inp[name].astype(_jnp.float32)
        if MOMENT_SCALE is None:
            s = _jnp.sqrt(_jnp.mean(_jnp.square(w)) + 1e-30)
        else:
            s = MOMENT_SCALE[name]
        km, kv = _jax.random.split(_jax.random.fold_in(key, i + 1))
        out[name] = w
        out["m_" + name] = s * _jax.random.normal(km, w.shape, _jnp.float32)
        out["v_" + name] = (s * s) * _jax.random.uniform(kv, w.shape, _jnp.float32, 0.5, 1.5)
    if N_MICROBATCH > 1:
        for name, axis in PER_EXAMPLE_BATCH_AXIS.items():
            out[name] = _to_microbatches(out[name], axis)
    return {'x': out['x'], 'c': out['c'], 'ctx': out['ctx'], 'c_ctx': out['c_ctx'], 'w_mod': out['w_mod'], 'b_mod': out['b_mod'], 'norm_g': out['norm_g'], 'w_ffn_gate_up': out['w_ffn_gate_up'], 'w_ffn_down': out['w_ffn_down'], 'w_in': out['w_in'], 'w_out': out['w_out'], 'na_rpb': out['na_rpb'], 'w_pool': out['w_pool'], 'pool_scale': out['pool_scale'], 'loss_target': out['loss_target'], 'm_c_ctx': out['m_c_ctx'], 'm_w_mod': out['m_w_mod'], 'm_b_mod': out['m_b_mod'], 'm_norm_g': out['m_norm_g'], 'm_w_ffn_gate_up': out['m_w_ffn_gate_up'], 'm_w_ffn_down': out['m_w_ffn_down'], 'm_w_in': out['m_w_in'], 'm_w_out': out['m_w_out'], 'm_na_rpb': out['m_na_rpb'], 'm_w_pool': out['m_w_pool'], 'm_pool_scale': out['m_pool_scale'], 'v_c_ctx': out['v_c_ctx'], 'v_w_mod': out['v_w_mod'], 'v_b_mod': out['v_b_mod'], 'v_norm_g': out['v_norm_g'], 'v_w_ffn_gate_up': out['v_w_ffn_gate_up'], 'v_w_ffn_down': out['v_w_ffn_down'], 'v_w_in': out['v_w_in'], 'v_w_out': out['v_w_out'], 'v_na_rpb': out['v_na_rpb'], 'v_w_pool': out['v_w_pool'], 'v_pool_scale': out['v_pool_scale']}


def _loss(weights, diff, rest, loss_target):
    with _jax.named_scope("forward"):
        args = {**rest, TWIN_DIFF_INPUT: diff, **{k: w.astype(_WEIGHT_DTYPES[k]) for k, w in weights.items()}}
        y = _forward(args)
    with _jax.named_scope("loss_head"):
        err = _jnp.square(y.astype(_jnp.float32) - loss_target)
        return 0.5 * _jnp.sum(_jnp.mean(err, axis=-1)) if err.ndim else 0.5 * err


def _adamw(w, g, m, v):
    m = ADAM_B1 * m + (1.0 - ADAM_B1) * g
    v = ADAM_B2 * v + (1.0 - ADAM_B2) * _jnp.square(g)
    m_hat = m / (1.0 - ADAM_B1 ** ADAM_STEP)
    v_hat = v / (1.0 - ADAM_B2 ** ADAM_STEP)
    delta = -ADAM_LR * (m_hat / (_jnp.sqrt(v_hat) + ADAM_EPS) + ADAM_WD * w)
    return delta, m, v


def reference(x, c, ctx, c_ctx, w_mod, b_mod, norm_g, w_ffn_gate_up, w_ffn_down, w_in, w_out, na_rpb, w_pool, pool_scale, loss_target, m_c_ctx, m_w_mod, m_b_mod, m_norm_g, m_w_ffn_gate_up, m_w_ffn_down, m_w_in, m_w_out, m_na_rpb, m_w_pool, m_pool_scale, v_c_ctx, v_w_mod, v_b_mod, v_norm_g, v_w_ffn_gate_up, v_w_ffn_down, v_w_in, v_w_out, v_na_rpb, v_w_pool, v_pool_scale):
    given = dict(x=x, c=c, ctx=ctx, c_ctx=c_ctx, w_mod=w_mod, b_mod=b_mod, norm_g=norm_g, w_ffn_gate_up=w_ffn_gate_up, w_ffn_down=w_ffn_down, w_in=w_in, w_out=w_out, na_rpb=na_rpb, w_pool=w_pool, pool_scale=pool_scale, loss_target=loss_target, m_c_ctx=m_c_ctx, m_w_mod=m_w_mod, m_b_mod=m_b_mod, m_norm_g=m_norm_g, m_w_ffn_gate_up=m_w_ffn_gate_up, m_w_ffn_down=m_w_ffn_down, m_w_in=m_w_in, m_w_out=m_w_out, m_na_rpb=m_na_rpb, m_w_pool=m_w_pool, m_pool_scale=m_pool_scale, v_c_ctx=v_c_ctx, v_w_mod=v_w_mod, v_b_mod=v_b_mod, v_norm_g=v_norm_g, v_w_ffn_gate_up=v_w_ffn_gate_up, v_w_ffn_down=v_w_ffn_down, v_w_in=v_w_in, v_w_out=v_w_out, v_na_rpb=v_na_rpb, v_w_pool=v_w_pool, v_pool_scale=v_pool_scale)
    weights = {n: given[n] for n in TWIN_WEIGHTS}
    shared = {n: given[n] for n in SHARED_INPUTS}
    per_example = {n: given[n] for n in ['x', 'c', 'ctx']}
    grad_fn = _jax.value_and_grad(_loss, argnums=(0, 1))

    def one_microbatch(ex, loss_target):
        ex = dict(ex)
        diff = ex.pop(TWIN_DIFF_INPUT)
        return grad_fn(weights, diff, {**shared, **ex}, loss_target)

    if N_MICROBATCH == 1:
        loss, (grad_w, grad_x) = one_microbatch(per_example, given["loss_target"])
    else:
        def body(carry, xs):
            loss_sum, grad_sum = carry
            l_k, (gw_k, gx_k) = one_microbatch(xs[0], xs[1])
            with _jax.named_scope("update"):
                return (loss_sum + l_k, _jax.tree.map(_jnp.add, grad_sum, gw_k)), gx_k

        init = (_jnp.zeros((), _jnp.float32), _jax.tree.map(_jnp.zeros_like, weights))
        (loss, grad_w), grad_x = _jax.lax.scan(body, init, (per_example, given["loss_target"]))
    with _jax.named_scope("update"):
        delta_w, new_m, new_v = {}, {}, {}
        for n in TWIN_WEIGHTS:
            delta_w[n], new_m[n], new_v[n] = _adamw(weights[n], grad_w[n], given["m_" + n], given["v_" + n])
    return (loss, grad_x, *[grad_w[n] for n in TWIN_WEIGHTS], *[delta_w[n] for n in TWIN_WEIGHTS],
            *[new_m[n] for n in TWIN_WEIGHTS], *[new_v[n] for n in TWIN_WEIGHTS])
```

```python
import functools

import numpy as np
import jax
import jax.numpy as jnp
from jax import lax
from jax.experimental import pallas as pl
from jax.experimental.pallas import tpu as pltpu

f32, bf16 = jnp.float32, jnp.bfloat16

GRID_W = 64
N_MOD = 9
NA_HEADS = 8
HEAD_DIM = 64
NA_WIDTH = NA_HEADS * HEAD_DIM
NA_KH = 8
NA_KW = 16
POOL_GROUPS = 4
POOL_CH = 128
POOL_WIDTH = POOL_GROUPS * POOL_CH
POOL_WINDOWS = (2, 4, 8, 16)
IN_WIDTH = 3 * NA_WIDTH + POOL_WIDTH
MIX_WIDTH = NA_WIDTH + POOL_WIDTH
ROPE_THETA = 10000.0
ROPE_PAIRS = HEAD_DIM // 4
RMS_EPS = 1e-6
NEG_INF = -1e30
ADAM_LR, ADAM_B1, ADAM_B2, ADAM_EPS, ADAM_WD, ADAM_STEP = 0.001, 0.9, 0.999, 1e-08, 0.01, 10

N_DEV = 8
N_CHIPS = 4
LANES = 128
MIB = 1024 * 1024
VMEM_BIG = 52 * MIB
VMEM_MID = 40 * MIB
MESH = pl.DeviceIdType.MESH
ANY = pl.BlockSpec(memory_space=pl.ANY)
S_ = jax.ShapeDtypeStruct


def _cp(vmem=VMEM_MID, sem=None):
    return pltpu.CompilerParams(vmem_limit_bytes=vmem, dimension_semantics=sem)


def _sigmoid(x):
    return 1.0 / (1.0 + jnp.exp(-x))


def _rms_hat(x):
    rinv = lax.rsqrt(jnp.mean(x * x, axis=-1, keepdims=True) + RMS_EPS)
    return x * rinv, rinv


def _rms_bwd(dxhat, xhat, rinv):
    return rinv * (dxhat - xhat * jnp.mean(dxhat * xhat, axis=-1, keepdims=True))


def _rsum(a):
    return jnp.sum(a, axis=0, keepdims=True)


def _nt(a, b):
    return lax.dot_general(a, b, (((1,), (1,)), ((), ())), preferred_element_type=f32)


def _tn(a, b):
    return lax.dot_general(a, b, (((0,), (0,)), ((), ())), preferred_element_type=f32)


def _nn(a, b):
    return jnp.dot(a, b, preferred_element_type=f32)


def _swap16(x):
    lane = lax.broadcasted_iota(jnp.int32, x.shape, 1)
    n = x.shape[1]
    return jnp.where((lane % 32) < 16, pltpu.roll(x, n - 16, 1), pltpu.roll(x, 16, 1))


def _rope_tables(s_len, l_len):
    t = np.arange(s_len)
    inv = ROPE_THETA ** (-np.arange(ROPE_PAIRS, dtype=np.float32) / ROPE_PAIRS)
    ang_r = (t // GRID_W).astype(np.float32)[:, None] * inv
    ang_c = (t % GRID_W).astype(np.float32)[:, None] * inv
    cos = np.concatenate([np.cos(ang_r), np.cos(ang_r), np.cos(ang_c), np.cos(ang_c)], axis=-1)
    sin = np.concatenate([-np.sin(ang_r), np.sin(ang_r), -np.sin(ang_c), np.sin(ang_c)], axis=-1)
    cos = np.concatenate([cos, np.ones((l_len, HEAD_DIM), np.float32)], axis=0)
    sin = np.concatenate([sin, np.zeros((l_len, HEAD_DIM), np.float32)], axis=0)
    return (jnp.asarray(np.tile(cos, (1, 2)), f32), jnp.asarray(np.tile(sin, (1, 2)), f32))


def _pool_tables(tm, l_len):
    band = np.zeros((2, POOL_GROUPS, tm, tm), np.float32)
    inv = np.zeros((2, POOL_GROUPS, tm, 1), np.float32)
    for typ, length in ((0, GRID_W), (1, l_len)):
        for g, w in enumerate(POOL_WINDOWS):
            for t in range(tm):
                base, p = (t // length) * length, t % length
                lo = min(max(p - w // 2, 0), length)
                hi = min(max(p - w // 2 + w, 0), length)
                band[typ, g, t, base + lo:base + hi] = 1.0
                inv[typ, g, t, 0] = 1.0 / (hi - lo)
    return jnp.asarray(band, bf16), jnp.asarray(inv, f32)


def _rpb_index_tables(rows):
    j = np.arange(GRID_W)
    col_start = np.clip(j - NA_KW // 2, 0, GRID_W - NA_KW)
    valid = (j[None, :] >= col_start[:, None]) & (j[None, :] < col_start[:, None] + NA_KW)
    dc = np.clip(j[None, :] - j[:, None] + NA_KW - 1, 0, 2 * NA_KW - 2)
    return valid, dc


def _expand_rpb(rpb):
    valid, dc = _rpb_index_tables(None)
    d_idx = np.arange(NA_KH)[:, None] + np.arange(NA_KH)[None, :]
    b = rpb[:, d_idx]
    b = b[..., dc]
    b = jnp.where(valid[None, None, None], b, NEG_INF)
    b = jnp.transpose(b, (1, 0, 3, 2, 4))
    return b.reshape(NA_KH, NA_HEADS, GRID_W, NA_KH * GRID_W)


def _rpb_reduce_tables():
    _, dc = _rpb_index_tables(None)
    onehot = np.zeros((GRID_W * GRID_W, LANES), np.float32)
    onehot[np.arange(GRID_W * GRID_W), dc.reshape(-1)] = 1.0
    sel = np.zeros((16, NA_KH * NA_KH), np.float32)
    for d0 in range(NA_KH):
        for kk in range(NA_KH):
            sel[d0 + kk, d0 * NA_KH + kk] = 1.0
    return jnp.asarray(onehot), jnp.asarray(sel)


class _Cfg:
    def __init__(self, s_len, l_len, d, f):
        self.S, self.L, self.D, self.F = s_len, l_len, d, f
        self.T = s_len + l_len
        self.TM = 256 if l_len % 256 == 0 else 128
        assert l_len == self.TM, "context length must equal the row tile"
        assert s_len % self.TM == 0 and s_len % GRID_W == 0
        self.nxt = s_len // self.TM
        self.ntt = self.T // self.TM
        self.rows = s_len // GRID_W
        assert self.rows >= 2 * NA_KH
        assert f % (2 * LANES) == 0
        self.FC = f // 2

    def ntiles(self, with_ctx):
        return self.ntt if with_ctx else self.nxt


def _typ(cfg):
    return lambda i: (jnp.minimum(i // cfg.nxt, 1), 0, 0)


def _ffn_fwd(cfg, xs, mods, gvec, wgu, wd, l, idx, mi, gi, with_ctx, name):
    TM, D, F, FC = cfg.TM, cfg.D, cfg.F, cfg.FC
    nt = cfg.ntiles(with_ctx)
    R = nt * TM

    def body(xs_ref, mods_ref, g_ref, wgu_hbm, wd_hbm, out_ref, hb_ref, z_ref, y_ref, wgu_v, wd_v, sem):
        @pl.when(pl.program_id(0) == 0)
        def _():
            c0 = pltpu.make_async_copy(wgu_hbm.at[l, idx], wgu_v, sem.at[0])
            c1 = pltpu.make_async_copy(wd_hbm.at[l, idx], wd_v, sem.at[1])
            c0.start(); c1.start(); c0.wait(); c1.wait()
        x = xs_ref[...]
        m = mods_ref[0]
        sh, sc, gt = m[mi:mi + 1], m[mi + 1:mi + 2], m[mi + 2:mi + 3]
        xhat, _ = _rms_hat(x)
        h = (xhat * g_ref[gi:gi + 1]) * (1.0 + sc) + sh
        hb = h.astype(bf16)
        hb_ref[...] = hb
        y = jnp.zeros((TM, D), f32)
        for ch in range(F // FC):
            zg = _nn(hb, wgu_v[:, ch * FC:(ch + 1) * FC])
            zu = _nn(hb, wgu_v[:, F + ch * FC:F + (ch + 1) * FC])
            z_ref[:, ch * FC:(ch + 1) * FC] = zg.astype(bf16)
            z_ref[:, F + ch * FC:F + (ch + 1) * FC] = zu.astype(bf16)
            a = (zg * _sigmoid(zg)) * zu
            y = y + _nn(a.astype(bf16), wd_v[ch * FC:(ch + 1) * FC, :])
        y_ref[...] = y
        yhat, _ = _rms_hat(y)
        out_ref[...] = x + 0.5 * gt * (yhat * g_ref[gi + 1:gi + 2])

    rt = lambda c: pl.BlockSpec((TM, c), lambda i: (i, 0))
    return pl.pallas_call(
        body, name=name, grid=(nt,),
        in_specs=[rt(D), pl.BlockSpec((1, N_MOD, D), _typ(cfg)), pl.BlockSpec((6, D), lambda i: (0, 0)), ANY, ANY],
        out_specs=[rt(D), rt(D), rt(2 * F), rt(D)],
        out_shape=[S_((R, D), f32), S_((R, D), bf16), S_((R, 2 * F), bf16), S_((R, D), f32)],
        scratch_shapes=[pltpu.VMEM((D, 2 * F), bf16), pltpu.VMEM((F, D), bf16), pltpu.SemaphoreType.DMA((2,))],
        compiler_params=_cp(VMEM_BIG, ("arbitrary",)),
    )(xs, mods, gvec, wgu, wd)


def _ffn_bwd(cfg, dout, xs, z, y, mods, gvec, wgu, wd, l, idx, mi, gi, with_ctx, name):
    TM, D, F, FC = cfg.TM, cfg.D, cfg.F, cfg.FC
    nt = cfg.ntiles(with_ctx)
    R = nt * TM
    ntyp = 2 if with_ctx else 1

    def body(do_ref, xs_ref, z_ref, y_ref, mods_ref, g_ref, wgu_hbm, wd_hbm,
             dx_ref, dz_ref, dy_ref, a_ref, dm_ref, dg_ref, wgu_v, wd_v, sem):
        i = pl.program_id(0)

        @pl.when(i == 0)
        def _():
            c0 = pltpu.make_async_copy(wgu_hbm.at[l, idx], wgu_v, sem.at[0])
            c1 = pltpu.make_async_copy(wd_hbm.at[l, idx], wd_v, sem.at[1])
            c0.start(); c1.start(); c0.wait(); c1.wait()
            dg_ref[...] = jnp.zeros_like(dg_ref)

        @pl.when((i == 0) | (i == cfg.nxt))
        def _():
            dm_ref[...] = jnp.zeros_like(dm_ref)

        do = do_ref[...]
        x = xs_ref[...]
        m = mods_ref[0]
        sc, gt = m[mi + 1:mi + 2], m[mi + 2:mi + 3]
        g_pre, g_post = g_ref[gi:gi + 1], g_ref[gi + 1:gi + 2]
        xhat, rinv0 = _rms_hat(x)
        n0 = xhat * g_pre
        yhat, rinv1 = _rms_hat(y_ref[...])
        d_gt = _rsum(0.5 * do * (yhat * g_post))
        dr = (0.5 * gt) * do
        dg_post = _rsum(dr * yhat)
        dy = _rms_bwd(dr * g_post, yhat, rinv1)
        dyb = dy.astype(bf16)
        dy_ref[...] = dyb
        dh = jnp.zeros((TM, D), f32)
        for ch in range(F // FC):
            zg = z_ref[:, ch * FC:(ch + 1) * FC].astype(f32)
            zu = z_ref[:, F + ch * FC:F + (ch + 1) * FC].astype(f32)
            sg = _sigmoid(zg)
            silu = zg * sg
            a_ref[:, ch * FC:(ch + 1) * FC] = (silu * zu).astype(bf16)
            da = _nt(dyb, wd_v[ch * FC:(ch + 1) * FC, :])
            dzu = (da * silu).astype(bf16)
            dzg = (da * zu * (sg * (1.0 + zg * (1.0 - sg)))).astype(bf16)
            dz_ref[:, ch * FC:(ch + 1) * FC] = dzg
            dz_ref[:, F + ch * FC:F + (ch + 1) * FC] = dzu
            dh = dh + _nt(dzg, wgu_v[:, ch * FC:(ch + 1) * FC]) + _nt(dzu, wgu_v[:, F + ch * FC:F + (ch + 1) * FC])
        d_sh = _rsum(dh)
        d_sc = _rsum(dh * n0)
        dn = dh * (1.0 + sc)
        dg_pre = _rsum(dn * xhat)
        dx_ref[...] = do + _rms_bwd(dn * g_pre, xhat, rinv0)
        dm_ref[0] += jnp.concatenate([d_sh, d_sc, d_gt], axis=0)
        dg_ref[...] += jnp.concatenate([dg_pre, dg_post], axis=0)

    rt = lambda c: pl.BlockSpec((TM, c), lambda i: (i, 0))
    return pl.pallas_call(
        body, name=name, grid=(nt,),
        in_specs=[rt(D), rt(D), rt(2 * F), rt(D), pl.BlockSpec((1, N_MOD, D), _typ(cfg)),
                  pl.BlockSpec((6, D), lambda i: (0, 0)), ANY, ANY],
        out_specs=[rt(D), rt(2 * F), rt(D), rt(F), pl.BlockSpec((1, 3, D), _typ(cfg)), pl.BlockSpec((2, D), lambda i: (0, 0))],
        out_shape=[S_((R, D), f32), S_((R, 2 * F), bf16), S_((R, D), bf16), S_((R, F), bf16),
                   S_((ntyp, 3, D), f32), S_((2, D), f32)],
        scratch_shapes=[pltpu.VMEM((D, 2 * F), bf16), pltpu.VMEM((F, D), bf16), pltpu.SemaphoreType.DMA((2,))],
        compiler_params=_cp(VMEM_BIG, ("arbitrary",)),
    )(dout, xs, z, y, mods, gvec, wgu, wd)


def _wgrad(a, b, k_rows, buf, sel, name, tk=512):
    M, N = a.shape[1], b.shape[1]
    tk = tk if k_rows % tk == 0 else 256 if k_rows % 256 == 0 else 128
    tn = N
    for cand in (1408, 1024, 512):
        if N % cand == 0 and N > cand:
            tn = cand
            break
    nk = k_rows // tk
    nsel = len(sel)

    def body(a_ref, b_ref, buf_hbm, o_ref):
        @pl.when(pl.program_id(1) == 0)
        def _():
            o_ref[...] = jnp.zeros_like(o_ref)
        o_ref[...] += _tn(a_ref[...], b_ref[...])

    return pl.pallas_call(
        body, name=name, grid=(N // tn, nk),
        in_specs=[pl.BlockSpec((tk, M), lambda n, k: (k, 0)), pl.BlockSpec((tk, tn), lambda n, k: (k, n)), ANY],
        out_specs=pl.BlockSpec((None,) * nsel + (M, tn), lambda n, k: tuple(sel) + (0, n)),
        out_shape=S_(buf.shape, f32),
        input_output_aliases={2: 0},
        compiler_params=_cp(VMEM_MID, ("arbitrary", "arbitrary")),
    )(a, b, buf)


def _tmpre_fwd(cfg, xs, mods, gvec, w_in, cos, sin, l, name):
    TM, D = cfg.TM, cfg.D
    nt, R = cfg.ntt, cfg.T
    W = NA_WIDTH

    def body(xs_ref, mods_ref, g_ref, w_ref, cos_ref, sin_ref, hb_ref, q_ref, k_ref, v_ref, u_ref):
        x = xs_ref[...]
        m = mods_ref[0]
        xhat, _ = _rms_hat(x)
        hb = ((xhat * g_ref[2:3]) * (1.0 + m[4:5]) + m[3:4]).astype(bf16)
        hb_ref[...] = hb
        p = _nn(hb, w_ref[...])
        cs = jnp.tile(cos_ref[...], (1, W // LANES))
        sn = jnp.tile(sin_ref[...], (1, W // LANES))
        q = p[:, 0:W]
        k = p[:, W:2 * W]
        q_ref[...] = ((q * cs + _swap16(q) * sn) * (HEAD_DIM ** -0.5)).astype(bf16)
        k_ref[...] = (k * cs + _swap16(k) * sn).astype(bf16)
        v_ref[...] = p[:, 2 * W:3 * W].astype(bf16)
        u_ref[...] = p[:, 3 * W:]

    rt = lambda c: pl.BlockSpec((TM, c), lambda i: (i, 0))
    return pl.pallas_call(
        body, name=name, grid=(nt,),
        in_specs=[rt(D), pl.BlockSpec((1, N_MOD, D), _typ(cfg)), pl.BlockSpec((6, D), lambda i: (0, 0)),
                  pl.BlockSpec((None, D, IN_WIDTH), lambda i: (l, 0, 0)), rt(LANES), rt(LANES)],
        out_specs=[rt(D), rt(W), rt(W), rt(W), rt(POOL_WIDTH)],
        out_shape=[S_((R, D), bf16), S_((R, W), bf16), S_((R, W), bf16), S_((R, W), bf16), S_((R, POOL_WIDTH), f32)],
        compiler_params=_cp(VMEM_MID, ("arbitrary",)),
    )(xs, mods, gvec, w_in, cos, sin)


def _rope_bwd_assemble(cfg, dq, dk, dv, du, cos, sin, name):
    TM = cfg.TM
    W = NA_WIDTH

    def body(dq_ref, dk_ref, dv_ref, du_ref, cos_ref, sin_ref, o_ref):
        cs = jnp.tile(cos_ref[...], (1, W // LANES))
        sn = jnp.tile(sin_ref[...], (1, W // LANES))
        dq_ = dq_ref[...] * (HEAD_DIM ** -0.5)
        dk_ = dk_ref[...]
        o_ref[:, 0:W] = (dq_ * cs + _swap16(dq_ * sn)).astype(bf16)
        o_ref[:, W:2 * W] = (dk_ * cs + _swap16(dk_ * sn)).astype(bf16)
        o_ref[:, 2 * W:3 * W] = dv_ref[...].astype(bf16)
        o_ref[:, 3 * W:] = du_ref[...].astype(bf16)

    rt = lambda c: pl.BlockSpec((TM, c), lambda i: (i, 0))
    return pl.pallas_call(
        body, name=name, grid=(cfg.ntt,),
        in_specs=[rt(W), rt(W), rt(W), rt(POOL_WIDTH), rt(LANES), rt(LANES)],
        out_specs=rt(IN_WIDTH), out_shape=S_((cfg.T, IN_WIDTH), bf16),
        compiler_params=_cp(VMEM_MID, ("arbitrary",)),
    )(dq, dk, dv, du, cos, sin)


def _tmpre_bwd(cfg, dproj, w_in, xs, mods, gvec, dres, res_with_ctx, l, name):
    TM, D = cfg.TM, cfg.D
    nt, R = cfg.ntt, cfg.T
    nres = cfg.ntiles(res_with_ctx)

    def body(dp_ref, w_ref, xs_ref, mods_ref, g_ref, dres_ref, dx_ref, dm_ref, dg_ref):
        i = pl.program_id(0)

        @pl.when(i == 0)
        def _():
            dg_ref[...] = jnp.zeros_like(dg_ref)

        @pl.when((i == 0) | (i == cfg.nxt))
        def _():
            dm_ref[...] = jnp.zeros_like(dm_ref)

        dh = _nt(dp_ref[...], w_ref[...])
        x = xs_ref[...]
        m = mods_ref[0]
        g2 = g_ref[2:3]
        xhat, rinv = _rms_hat(x)
        d_sh = _rsum(dh)
        d_sc = _rsum(dh * (xhat * g2))
        dn = dh * (1.0 + m[4:5])
        dg_ref[...] += _rsum(dn * xhat)
        dx = _rms_bwd(dn * g2, xhat, rinv)
        res = dres_ref[...]
        if nres < nt:
            res = jnp.where(i < nres, res, 0.0)
        dx_ref[...] = res + dx
        dm_ref[0] += jnp.concatenate([d_sh, d_sc], axis=0)

    rt = lambda c: pl.BlockSpec((TM, c), lambda i: (i, 0))
    return pl.pallas_call(
        body, name=name, grid=(nt,),
        in_specs=[rt(IN_WIDTH), pl.BlockSpec((None, D, IN_WIDTH), lambda i: (l, 0, 0)), rt(D),
                  pl.BlockSpec((1, N_MOD, D), _typ(cfg)), pl.BlockSpec((6, D), lambda i: (0, 0)),
                  pl.BlockSpec((TM, D), lambda i: (jnp.minimum(i, nres - 1), 0))],
        out_specs=[rt(D), pl.BlockSpec((1, 2, D), _typ(cfg)), pl.BlockSpec((1, D), lambda i: (0, 0))],
        out_shape=[S_((R, D), f32), S_((2, 2, D), f32), S_((1, D), f32)],
        compiler_params=_cp(VMEM_MID, ("arbitrary",)),
    )(dproj, w_in, xs, mods, gvec, dres)


def _na_window(cfg, r):
    rs = jnp.clip(r - NA_KH // 2, 0, cfg.rows - NA_KH)
    return rs, rs - r + NA_KH - 1


def _na_probs(qh, klh, kch, bias):
    s_loc = _nt(qh, klh) + bias
    s_ctx = _nt(qh, kch)
    mx = jnp.maximum(jnp.max(s_loc, axis=-1, keepdims=True), jnp.max(s_ctx, axis=-1, keepdims=True))
    e_loc = jnp.exp(s_loc - mx)
    e_ctx = jnp.exp(s_ctx - mx)
    inv = 1.0 / (jnp.sum(e_loc, axis=-1, keepdims=True) + jnp.sum(e_ctx, axis=-1, keepdims=True))
    return e_loc * inv, e_ctx * inv


def _na_fwd(cfg, q, k, v, bexp, name):
    S, L, T = cfg.S, cfg.L, cfg.T
    NW = NA_KH * GRID_W

    def body(q_ref, k_hbm, v_hbm, b_hbm, o_ref, k_v, v_v, b_v, sem):
        r = pl.program_id(0)

        @pl.when(r == 0)
        def _():
            cs = [pltpu.make_async_copy(k_hbm, k_v, sem.at[0]), pltpu.make_async_copy(v_hbm, v_v, sem.at[1]),
                  pltpu.make_async_copy(b_hbm, b_v, sem.at[2])]
            for c_ in cs:
                c_.start()
            for c_ in cs:
                c_.wait()

        rs, d0 = _na_window(cfg, r)
        st = pl.multiple_of(rs * GRID_W, GRID_W)
        qv = q_ref[...]
        kl, vl = k_v[pl.ds(st, NW), :], v_v[pl.ds(st, NW), :]
        kc, vc = k_v[S:T, :], v_v[S:T, :]
        outs = []
        for h in range(NA_HEADS):
            hs = slice(h * HEAD_DIM, (h + 1) * HEAD_DIM)
            p_loc, p_ctx = _na_probs(qv[:, hs], kl[:, hs], kc[:, hs], b_v[d0, h])
            outs.append(_nn(p_loc.astype(bf16), vl[:, hs]) + _nn(p_ctx.astype(bf16), vc[:, hs]))
        o_ref[...] = jnp.concatenate(outs, axis=-1).astype(bf16)

    return pl.pallas_call(
        body, name=name, grid=(cfg.rows,),
        in_specs=[pl.BlockSpec((GRID_W, NA_WIDTH), lambda r: (r, 0)), ANY, ANY, ANY],
        out_specs=pl.BlockSpec((GRID_W, NA_WIDTH), lambda r: (r, 0)),
        out_shape=S_((S, NA_WIDTH), bf16),
        scratch_shapes=[pltpu.VMEM((T, NA_WIDTH), bf16), pltpu.VMEM((T, NA_WIDTH), bf16),
                        pltpu.VMEM((NA_KH, NA_HEADS, GRID_W, NW), f32), pltpu.SemaphoreType.DMA((3,))],
        compiler_params=_cp(VMEM_MID, ("arbitrary",)),
    )(q, k, v, bexp)


def _na_bwd(cfg, do, q, k, v, bexp, name):
    S, L, T, rows = cfg.S, cfg.L, cfg.T, cfg.rows
    NW = NA_KH * GRID_W
    NSLOT = 2 * NA_KH
    steps = rows + NA_KH
    W = NA_WIDTH

    def body(do_ref, q_ref, k_hbm, v_hbm, b_hbm, dq_ref, dk_ref, dv_ref, dkc_ref, dvc_ref, db_hbm,
             k_v, v_v, b_v, db_v, ak, av, akc, avc, sem):
        g = pl.program_id(0)

        @pl.when(g == 0)
        def _():
            cs = [pltpu.make_async_copy(k_hbm, k_v, sem.at[0]), pltpu.make_async_copy(v_hbm, v_v, sem.at[1]),
                  pltpu.make_async_copy(b_hbm, b_v, sem.at[2])]
            for c_ in cs:
                c_.start()
            db_v[...] = jnp.zeros_like(db_v)
            ak[...] = jnp.zeros_like(ak)
            av[...] = jnp.zeros_like(av)
            akc[...] = jnp.zeros_like(akc)
            avc[...] = jnp.zeros_like(avc)
            for c_ in cs:
                c_.wait()

        @pl.when(g < rows)
        def _():
            rs, d0 = _na_window(cfg, g)
            st = pl.multiple_of(rs * GRID_W, GRID_W)
            qv, dov = q_ref[...], do_ref[...]
            kl, vl = k_v[pl.ds(st, NW), :], v_v[pl.ds(st, NW), :]
            kc, vc = k_v[S:T, :], v_v[S:T, :]
            dqs, dks, dvs, dkcs, dvcs = [], [], [], [], []
            for h in range(NA_HEADS):
                hs = slice(h * HEAD_DIM, (h + 1) * HEAD_DIM)
                qh, doh = qv[:, hs], dov[:, hs]
                p_loc, p_ctx = _na_probs(qh, kl[:, hs], kc[:, hs], b_v[d0, h])
                dp_loc = _nt(doh, vl[:, hs])
                dp_ctx = _nt(doh, vc[:, hs])
                delta = jnp.sum(p_loc * dp_loc, axis=-1, keepdims=True) + jnp.sum(p_ctx * dp_ctx, axis=-1, keepdims=True)
                ds_loc = p_loc * (dp_loc - delta)
                ds_ctx = p_ctx * (dp_ctx - delta)
                db_v[d0, h] += ds_loc
                dsl, dsc = ds_loc.astype(bf16), ds_ctx.astype(bf16)
                dqs.append(_nn(dsl, kl[:, hs]) + _nn(dsc, kc[:, hs]))
                dks.append(_tn(dsl, qh))
                dvs.append(_tn(p_loc.astype(bf16), doh))
                dkcs.append(_tn(dsc, qh))
                dvcs.append(_tn(p_ctx.astype(bf16), doh))
            dq_ref[...] = jnp.concatenate(dqs, axis=-1)
            dkw = jnp.concatenate(dks, axis=-1)
            dvw = jnp.concatenate(dvs, axis=-1)
            akc[...] += jnp.concatenate(dkcs, axis=-1)
            avc[...] += jnp.concatenate(dvcs, axis=-1)
            for kk in range(NA_KH):
                slot = (rs + kk) % NSLOT
                ak[slot] += dkw[kk * GRID_W:(kk + 1) * GRID_W, :]
                av[slot] += dvw[kk * GRID_W:(kk + 1) * GRID_W, :]

        @pl.when(g >= NA_KH)
        def _():
            slot = (g - NA_KH) % NSLOT
            dk_ref[...] = ak[slot]
            dv_ref[...] = av[slot]
            ak[slot] = jnp.zeros((GRID_W, W), f32)
            av[slot] = jnp.zeros((GRID_W, W), f32)

        @pl.when(g == steps - 1)
        def _():
            dkc_ref[...] = akc[...]
            dvc_ref[...] = avc[...]
            cp = pltpu.make_async_copy(db_v, db_hbm, sem.at[0])
            cp.start()
            cp.wait()

    qmap = lambda g: (jnp.minimum(g, rows - 1), 0)
    kmap = lambda g: (jnp.maximum(g - NA_KH, 0), 0)
    full = lambda g: (0, 0)
    return pl.pallas_call(
        body, name=name, grid=(steps,),
        in_specs=[pl.BlockSpec((GRID_W, W), qmap), pl.BlockSpec((GRID_W, W), qmap), ANY, ANY, ANY],
        out_specs=[pl.BlockSpec((GRID_W, W), qmap), pl.BlockSpec((GRID_W, W), kmap), pl.BlockSpec((GRID_W, W), kmap),
                   pl.BlockSpec((L, W), full), pl.BlockSpec((L, W), full), ANY],
        out_shape=[S_((S, W), f32), S_((S, W), f32), S_((S, W), f32), S_((L, W), f32), S_((L, W), f32),
                   S_((NA_KH, NA_HEADS, GRID_W, NW), f32)],
        scratch_shapes=[pltpu.VMEM((T, W), bf16), pltpu.VMEM((T, W), bf16),
                        pltpu.VMEM((NA_KH, NA_HEADS, GRID_W, NW), f32), pltpu.VMEM((NA_KH, NA_HEADS, GRID_W, NW), f32),
                        pltpu.VMEM((NSLOT, GRID_W, W), f32), pltpu.VMEM((NSLOT, GRID_W, W), f32),
                        pltpu.VMEM((L, W), f32), pltpu.VMEM((L, W), f32), pltpu.SemaphoreType.DMA((3,))],
        compiler_params=_cp(VMEM_BIG, ("arbitrary",)),
    )(do, q, k, v, bexp)


def _rpb_reduce(dbexp, onehot, sel, name):
    x = dbexp.reshape(NA_KH, NA_HEADS, GRID_W, NA_KH, GRID_W)
    x = jnp.transpose(x, (1, 0, 3, 2, 4)).reshape(NA_HEADS, NA_KH * NA_KH, GRID_W * GRID_W)

    def body(x_ref, oh_ref, sel_ref, o_ref):
        y = jnp.dot(x_ref[...], oh_ref[...], preferred_element_type=f32, precision=lax.Precision.HIGHEST)
        o_ref[...] = jnp.dot(sel_ref[...], y, preferred_element_type=f32, precision=lax.Precision.HIGHEST)

    return pl.pallas_call(
        body, name=name, grid=(NA_HEADS,),
        in_specs=[pl.BlockSpec((None, NA_KH * NA_KH, GRID_W * GRID_W), lambda h: (h, 0, 0)),
                  pl.BlockSpec((GRID_W * GRID_W, LANES), lambda h: (0, 0)), pl.BlockSpec((16, NA_KH * NA_KH), lambda h: (0, 0))],
        out_specs=pl.BlockSpec((None, 16, LANES), lambda h: (h, 0, 0)),
        out_shape=S_((NA_HEADS, 16, LANES), f32),
        compiler_params=_cp(VMEM_MID, ("arbitrary",)),
    )(x, onehot, sel)


def _ctx_attn_fwd(cfg, q, k, v, name):
    L = cfg.L
    blk = cfg.S // L

    def body(q_ref, k_ref, v_ref, o_ref):
        qv, kv, vv = q_ref[...], k_ref[...], v_ref[...]
        outs = []
        for h in range(NA_HEADS):
            hs = slice(h * HEAD_DIM, (h + 1) * HEAD_DIM)
            s = _nt(qv[:, hs], kv[:, hs])
            e = jnp.exp(s - jnp.max(s, axis=-1, keepdims=True))
            p = e * (1.0 / jnp.sum(e, axis=-1, keepdims=True))
            outs.append(_nn(p.astype(bf16), vv[:, hs]))
        o_ref[...] = jnp.concatenate(outs, axis=-1).astype(bf16)

    spec = pl.BlockSpec((L, NA_WIDTH), lambda i: (blk, 0))
    return pl.pallas_call(
        body, name=name, grid=(1,), in_specs=[spec, spec, spec],
        out_specs=pl.BlockSpec((L, NA_WIDTH), lambda i: (0, 0)), out_shape=S_((L, NA_WIDTH), bf16),
        compiler_params=_cp(VMEM_MID, ("arbitrary",)),
    )(q, k, v)


def _ctx_attn_bwd(cfg, do, q, k, v, name):
    L = cfg.L
    blk = cfg.S // L

    def body(do_ref, q_ref, k_ref, v_ref, dq_ref, dk_ref, dv_ref):
        dov, qv, kv, vv = do_ref[...], q_ref[...], k_ref[...], v_ref[...]
        dqs, dks, dvs = [], [], []
        for h in range(NA_HEADS):
            hs = slice(h * HEAD_DIM, (h + 1) * HEAD_DIM)
            qh, kh, doh = qv[:, hs], kv[:, hs], dov[:, hs]
            s = _nt(qh, kh)
            e = jnp.exp(s - jnp.max(s, axis=-1, keepdims=True))
            p = e * (1.0 / jnp.sum(e, axis=-1, keepdims=True))
            dp = _nt(doh, vv[:, hs])
            ds = (p * (dp - jnp.sum(p * dp, axis=-1, keepdims=True))).astype(bf16)
            dqs.append(_nn(ds, kh))
            dks.append(_tn(ds, qh))
            dvs.append(_tn(p.astype(bf16), doh))
        dq_ref[...] = jnp.concatenate(dqs, axis=-1)
        dk_ref[...] = jnp.concatenate(dks, axis=-1)
        dv_ref[...] = jnp.concatenate(dvs, axis=-1)

    spec = pl.BlockSpec((L, NA_WIDTH), lambda i: (blk, 0))
    ospec = pl.BlockSpec((L, NA_WIDTH), lambda i: (0, 0))
    return pl.pallas_call(
        body, name=name, grid=(1,), in_specs=[spec, spec, spec, spec],
        out_specs=[ospec, ospec, ospec], out_shape=[S_((L, NA_WIDTH), f32)] * 3,
        compiler_params=_cp(VMEM_MID, ("arbitrary",)),
    )(do, q, k, v)


def _pool_centered(u, band, inv):
    hi = u.astype(bf16)
    lo = (u - hi.astype(f32)).astype(bf16)
    return (_nn(band, hi) + _nn(band, lo)) * inv - u


def _pool_fwd(cfg, u, band, inv, w_pool, pool_scale, with_ctx, name):
    TM = cfg.TM
    nt = cfg.ntiles(with_ctx)
    C = POOL_CH

    def body(u_ref, band_ref, inv_ref, w_ref, ps_ref, o_ref):
        outs = []
        for g in range(POOL_GROUPS):
            d = _pool_centered(u_ref[:, g * C:(g + 1) * C], band_ref[0, g], inv_ref[0, g])
            outs.append(_nn(d.astype(bf16), w_ref[g].astype(bf16)) * ps_ref[:, g * C:(g + 1) * C])
        o_ref[...] = jnp.concatenate(outs, axis=-1).astype(bf16)

    typ4 = lambda i: (jnp.minimum(i // cfg.nxt, 1), 0, 0, 0)
    return pl.pallas_call(
        body, name=name, grid=(nt,),
        in_specs=[pl.BlockSpec((TM, POOL_WIDTH), lambda i: (i, 0)), pl.BlockSpec((1, POOL_GROUPS, TM, TM), typ4),
                  pl.BlockSpec((1, POOL_GROUPS, TM, 1), typ4), pl.BlockSpec((POOL_GROUPS, C, C), lambda i: (0, 0, 0)),
                  pl.BlockSpec((1, POOL_WIDTH), lambda i: (0, 0))],
        out_specs=pl.BlockSpec((TM, POOL_WIDTH), lambda i: (i, 0)),
        out_shape=S_((nt * TM, POOL_WIDTH), bf16),
        compiler_params=_cp(VMEM_MID, ("arbitrary",)),
    )(u, band, inv, w_pool, pool_scale)


def _pool_bwd(cfg, dmix, u, band, inv, w_pool, pool_scale, with_ctx, name):
    TM = cfg.TM
    nt = cfg.ntiles(with_ctx)
    C = POOL_CH

    def body(dy_ref, u_ref, band_ref, inv_ref, w_ref, ps_ref, du_ref, dw_ref, dps_ref):
        @pl.when(pl.program_id(0) == 0)
        def _():
            dw_ref[...] = jnp.zeros_like(dw_ref)
            dps_ref[...] = jnp.zeros_like(dps_ref)

        dus, dpss = [], []
        for g in range(POOL_GROUPS):
            gs = slice(g * C, (g + 1) * C)
            band_g, inv_g = band_ref[0, g], inv_ref[0, g]
            db = _pool_centered(u_ref[:, gs], band_g, inv_g).astype(bf16)
            wb = w_ref[g].astype(bf16)
            dy = dy_ref[:, gs].astype(f32)
            dpss.append(_rsum(dy * _nn(db, wb)))
            dys = (dy * ps_ref[:, gs]).astype(bf16)
            dw_ref[g] += _tn(db, dys)
            dd = _nt(dys, wb)
            t = dd * inv_g
            hi = t.astype(bf16)
            lo = (t - hi.astype(f32)).astype(bf16)
            dus.append(_tn(band_g, hi) + _tn(band_g, lo) - dd)
        du_ref[...] = jnp.concatenate(dus, axis=-1)
        dps_ref[...] += jnp.concatenate(dpss, axis=-1)

    typ4 = lambda i: (jnp.minimum(i // cfg.nxt, 1), 0, 0, 0)
    return pl.pallas_call(
        body, name=name, grid=(nt,),
        in_specs=[pl.BlockSpec((TM, POOL_WIDTH), lambda i: (i, 1)), pl.BlockSpec((TM, POOL_WIDTH), lambda i: (i, 0)),
                  pl.BlockSpec((1, POOL_GROUPS, TM, TM), typ4), pl.BlockSpec((1, POOL_GROUPS, TM, 1), typ4),
                  pl.BlockSpec((POOL_GROUPS, C, C), lambda i: (0, 0, 0)), pl.BlockSpec((1, POOL_WIDTH), lambda i: (0, 0))],
        out_specs=[pl.BlockSpec((TM, POOL_WIDTH), lambda i: (i, 0)), pl.BlockSpec((POOL_GROUPS, C, C), lambda i: (0, 0, 0)),
                   pl.BlockSpec((1, POOL_WIDTH), lambda i: (0, 0))],
        out_shape=[S_((nt * TM, POOL_WIDTH), f32), S_((POOL_GROUPS, C, C), f32), S_((1, POOL_WIDTH), f32)],
        compiler_params=_cp(VMEM_MID, ("arbitrary",)),
    )(dmix, u, band, inv, w_pool, pool_scale)


def _tmpost_fwd(cfg, na, pool, w_out, xs, mods, gvec, l, with_ctx, name):
    TM, D = cfg.TM, cfg.D
    nt = cfg.ntiles(with_ctx)
    R = nt * TM

    def body(na_ref, pool_ref, w_ref, xs_ref, mods_ref, g_ref, out_ref, opre_ref):
        o = _nn(na_ref[...], w_ref[0:NA_WIDTH, :]) + _nn(pool_ref[...], w_ref[NA_WIDTH:, :])
        opre_ref[...] = o
        ohat, _ = _rms_hat(o)
        out_ref[...] = xs_ref[...] + mods_ref[0][5:6] * (ohat * g_ref[3:4])

    rt = lambda c: pl.BlockSpec((TM, c), lambda i: (i, 0))
    return pl.pallas_call(
        body, name=name, grid=(nt,),
        in_specs=[rt(NA_WIDTH), rt(POOL_WIDTH), pl.BlockSpec((None, MIX_WIDTH, D), lambda i: (l, 0, 0)), rt(D),
                  pl.BlockSpec((1, N_MOD, D), _typ(cfg)), pl.BlockSpec((6, D), lambda i: (0, 0))],
        out_specs=[rt(D), rt(D)], out_shape=[S_((R, D), f32), S_((R, D), f32)],
        compiler_params=_cp(VMEM_MID, ("arbitrary",)),
    )(na, pool, w_out, xs, mods, gvec)


def _tmpost_bwd(cfg, dout, opre, w_out, mods, gvec, l, with_ctx, name):
    TM, D = cfg.TM, cfg.D
    nt = cfg.ntiles(with_ctx)
    R = nt * TM
    ntyp = 2 if with_ctx else 1

    def body(do_ref, opre_ref, w_ref, mods_ref, g_ref, dop_ref, dmix_ref, dm_ref, dg_ref):
        i = pl.program_id(0)

        @pl.when(i == 0)
        def _():
            dg_ref[...] = jnp.zeros_like(dg_ref)

        @pl.when((i == 0) | (i == cfg.nxt))
        def _():
            dm_ref[...] = jnp.zeros_like(dm_ref)

        do = do_ref[...]
        g3 = g_ref[3:4]
        ohat, rinv = _rms_hat(opre_ref[...])
        dm_ref[0] += _rsum(do * (ohat * g3))
        dr = mods_ref[0][5:6] * do
        dg_ref[...] += _rsum(dr * ohat)
        dob = _rms_bwd(dr * g3, ohat, rinv).astype(bf16)
        dop_ref[...] = dob
        dmix_ref[...] = _nt(dob, w_ref[...]).astype(bf16)

    rt = lambda c: pl.BlockSpec((TM, c), lambda i: (i, 0))
    return pl.pallas_call(
        body, name=name, grid=(nt,),
        in_specs=[rt(D), rt(D), pl.BlockSpec((None, MIX_WIDTH, D), lambda i: (l, 0, 0)),
                  pl.BlockSpec((1, N_MOD, D), _typ(cfg)), pl.BlockSpec((6, D), lambda i: (0, 0))],
        out_specs=[rt(D), rt(MIX_WIDTH), pl.BlockSpec((1, 1, D), _typ(cfg)), pl.BlockSpec((1, D), lambda i: (0, 0))],
        out_shape=[S_((R, D), bf16), S_((R, MIX_WIDTH), bf16), S_((ntyp, 1, D), f32), S_((1, D), f32)],
        compiler_params=_cp(VMEM_MID, ("arbitrary",)),
    )(dout, opre, w_out, mods, gvec)


def _loss_head(cfg, y, target, name):
    TM, D = cfg.TM, cfg.D

    def body(y_ref, t_ref, dy_ref, loss_ref):
        @pl.when(pl.program_id(0) == 0)
        def _():
            loss_ref[...] = jnp.zeros_like(loss_ref)
        e = y_ref[...] - t_ref[...]
        dy_ref[...] = e * (1.0 / D)
        loss_ref[...] += jnp.sum(jnp.mean(e * e, axis=-1, keepdims=True), axis=0, keepdims=True) * 0.5

    rt = pl.BlockSpec((TM, D), lambda i: (i, 0))
    return pl.pallas_call(
        body, name=name, grid=(cfg.nxt,), in_specs=[rt, rt],
        out_specs=[rt, pl.BlockSpec((8, LANES), lambda i: (0, 0))],
        out_shape=[S_((cfg.S, D), f32), S_((8, LANES), f32)],
        compiler_params=_cp(VMEM_MID, ("arbitrary",)),
    )(y, target)


def _modvec_fwd(cvecs, w_mod, b_shard, name):
    nl, D, n = w_mod.shape
    tn = n // 3 if (n % 3 == 0 and (n // 3) % LANES == 0) else n

    def body(c_ref, w_ref, b_ref, o_ref, s_ref):
        cv = c_ref[...]
        sv = cv * _sigmoid(cv)
        s_ref[...] = sv
        o_ref[...] = _nn(sv.astype(bf16), w_ref[...].astype(bf16)) + b_ref[...]

    return pl.pallas_call(
        body, name=name, grid=(nl, n // tn),
        in_specs=[pl.BlockSpec((16, D), lambda l, j: (0, 0)), pl.BlockSpec((None, D, tn), lambda l, j: (l, 0, j)),
                  pl.BlockSpec((None, 1, tn), lambda l, j: (l, 0, j))],
        out_specs=[pl.BlockSpec((None, 16, tn), lambda l, j: (l, 0, j)), pl.BlockSpec((16, D), lambda l, j: (0, 0))],
        out_shape=[S_((nl, 16, n), f32), S_((16, D), f32)],
        compiler_params=_cp(VMEM_MID, ("arbitrary", "arbitrary")),
    )(cvecs, w_mod, b_shard)


def _modvec_bwd(s_t, dm, w_mod, name):
    nl, D, n = w_mod.shape
    tn = n // 3 if (n % 3 == 0 and (n // 3) % LANES == 0) else n

    def body(s_ref, dm_ref, w_ref, gw_ref, gc_ref):
        @pl.when(pl.program_id(1) == 0)
        def _():
            gc_ref[...] = jnp.zeros_like(gc_ref)
        dmv = dm_ref[...]
        gw_ref[...] = jnp.dot(s_ref[...], dmv, preferred_element_type=f32, precision=lax.Precision.HIGHEST)
        gc_ref[...] += _nt(dmv[8:16].astype(bf16), w_ref[...].astype(bf16))

    return pl.pallas_call(
        body, name=name, grid=(nl, n // tn),
        in_specs=[pl.BlockSpec((D, 16), lambda l, j: (0, 0)), pl.BlockSpec((None, 16, tn), lambda l, j: (l, 0, j)),
                  pl.BlockSpec((None, D, tn), lambda l, j: (l, 0, j))],
        out_specs=[pl.BlockSpec((None, D, tn), lambda l, j: (l, 0, j)), pl.BlockSpec((None, 8, D), lambda l, j: (l, 0, 0))],
        out_shape=[S_((nl, D, n), f32), S_((nl, 8, D), f32)],
        compiler_params=_cp(VMEM_MID, ("arbitrary", "arbitrary")),
    )(s_t, dm, w_mod)


def _as2d(a):
    n = a.size
    if a.ndim >= 2 and a.shape[-1] % LANES == 0:
        return a.reshape(-1, a.shape[-1])
    if n % LANES == 0:
        return a.reshape(-1, LANES)
    return a.reshape(-1, a.shape[-1]) if a.ndim >= 2 else a.reshape(1, n)


def _row_tile(r, c, budget_elems=512 * 1024):
    if r * c <= budget_elems or r % 8 != 0:
        return r
    t = r
    while t * c > budget_elems and t % 16 == 0:
        t //= 2
    return t


def _cast_bf16(a, name):
    a2 = _as2d(a)
    r, c = a2.shape
    tr = _row_tile(r, c)

    def body(a_ref, o_ref):
        o_ref[...] = a_ref[...].astype(bf16)

    spec = pl.BlockSpec((tr, c), lambda i: (i, 0))
    out = pl.pallas_call(body, name=name, grid=(r // tr,), in_specs=[spec], out_specs=spec,
                         out_shape=S_((r, c), bf16), compiler_params=_cp(VMEM_MID, ("arbitrary",)))(a2)
    return out.reshape(a.shape)


def _adamw(w, g, m, v, name):
    shape = w.shape
    w2, g2, m2, v2 = _as2d(w), _as2d(g), _as2d(m), _as2d(v)
    r, c = w2.shape
    tr = _row_tile(r, c, 256 * 1024)
    c1 = 1.0 - ADAM_B1 ** ADAM_STEP
    c2 = 1.0 - ADAM_B2 ** ADAM_STEP

    def body(w_ref, g_ref, m_ref, v_ref, d_ref, mo_ref, vo_ref):
        gv = g_ref[...]
        mn = ADAM_B1 * m_ref[...] + (1.0 - ADAM_B1) * gv
        vn = ADAM_B2 * v_ref[...] + (1.0 - ADAM_B2) * (gv * gv)
        mo_ref[...] = mn
        vo_ref[...] = vn
        d_ref[...] = -ADAM_LR * ((mn / c1) / (jnp.sqrt(vn / c2) + ADAM_EPS) + ADAM_WD * w_ref[...])

    spec = pl.BlockSpec((tr, c), lambda i: (i, 0))
    outs = pl.pallas_call(body, name=name, grid=(r // tr,), in_specs=[spec] * 4, out_specs=[spec] * 3,
                          out_shape=[S_((r, c), f32)] * 3, compiler_params=_cp(VMEM_MID, ("arbitrary",)))(w2, g2, m2, v2)
    return tuple(o.reshape(shape) for o in outs)


def _sum_devices(gathered, name):
    _, r, c = gathered.shape

    def body(a_ref, o_ref):
        acc = a_ref[0]
        for j in range(1, N_DEV):
            acc = acc + a_ref[j]
        o_ref[...] = acc

    tr = _row_tile(r, c, 64 * 1024)
    return pl.pallas_call(
        body, name=name, grid=(r // tr,),
        in_specs=[pl.BlockSpec((N_DEV, tr, c), lambda i: (0, i, 0))], out_specs=pl.BlockSpec((tr, c), lambda i: (i, 0)),
        out_shape=S_((r, c), f32), compiler_params=_cp(VMEM_MID, ("arbitrary",)))(gathered)


def _add_layer(full, sib, cidx, name):
    _, r, c = full.shape
    tr = _row_tile(r, c, 256 * 1024)

    def body(c_ref, a_ref, b_ref, o_ref):
        o_ref[...] = a_ref[...] + b_ref[...]

    return pl.pallas_call(
        body, name=name,
        grid_spec=pltpu.PrefetchScalarGridSpec(
            num_scalar_prefetch=1, grid=(r // tr,),
            in_specs=[pl.BlockSpec((1, tr, c), lambda i, cr: (cr[0], i, 0)), pl.BlockSpec((1, tr, c), lambda i, cr: (0, i, 0))],
            out_specs=pl.BlockSpec((1, tr, c), lambda i, cr: (0, i, 0))),
        out_shape=S_((1, r, c), f32), compiler_params=_cp(VMEM_MID, ("arbitrary",)))(cidx, full, sib)


def _sum_chips(parts, name):
    _, r, c = parts.shape
    tr = _row_tile(r, c, 128 * 1024)

    def body(a_ref, o_ref):
        o_ref[...] = ((a_ref[0] + a_ref[1]) + a_ref[2]) + a_ref[3]

    return pl.pallas_call(
        body, name=name, grid=(r // tr,),
        in_specs=[pl.BlockSpec((N_CHIPS, tr, c), lambda i: (0, i, 0))], out_specs=pl.BlockSpec((tr, c), lambda i: (i, 0)),
        out_shape=S_((r, c), f32), compiler_params=_cp(VMEM_MID, ("arbitrary",)))(parts)


def _mesh_pos():
    return lax.axis_index("x"), lax.axis_index("y"), lax.axis_index("c")


def _all_gather_small(block, name):
    m_per, n = block.shape

    def body(x_ref, out_ref, send_sems, recv_sems, local_sem):
        x, y, c = _mesh_pos()
        me, sibling = (x, y, c), (x, y, 1 - c)
        chips = [(1 - x, y), (x, 1 - y), (1 - x, 1 - y)]

        def rows(px, py, pc):
            return out_ref.at[pl.ds((4 * px + 2 * py + pc) * m_per, m_per), :]

        def copy(k, blk, to, src=None):
            return pltpu.make_async_remote_copy(
                src_ref=rows(*blk) if src is None else src, dst_ref=rows(*blk),
                send_sem=send_sems.at[k], recv_sem=recv_sems.at[k], device_id=to, device_id_type=MESH)

        mine = pltpu.make_async_copy(x_ref, rows(*me), local_sem)
        mine.start()
        first = [copy(0, me, sibling, src=x_ref)]
        first += [copy(1 + j, me, (*chip, c), src=x_ref) for j, chip in enumerate(chips)]
        for cp in first:
            cp.start()
        passed = [copy(4 + j, (*chip, c), sibling) for j, chip in enumerate(chips)]
        for j, chip in enumerate(chips):
            copy(1 + j, (*chip, c), me).wait_recv()
            passed[j].start()
        copy(0, sibling, me).wait_recv()
        for j, chip in enumerate(chips):
            copy(4 + j, (*chip, 1 - c), me).wait_recv()
        for cp in first + passed:
            cp.wait_send()
        mine.wait()

    return pl.pallas_call(
        body, name=name, out_shape=S_((N_DEV * m_per, n), block.dtype),
        in_specs=[pl.BlockSpec(memory_space=pltpu.VMEM)], out_specs=pl.BlockSpec(memory_space=pltpu.VMEM),
        scratch_shapes=[pltpu.SemaphoreType.DMA((7,)), pltpu.SemaphoreType.DMA((7,)), pltpu.SemaphoreType.DMA],
        compiler_params=_cp(VMEM_MID),
    )(block)


def _shard_view(ref, axis, kk, size, lsel):
    idx = [lsel] + [slice(None)] * (len(ref.shape) - 1)
    idx[axis] = pl.ds(pl.multiple_of(kk * size, LANES if axis == len(ref.shape) - 1 else 8), size)
    return ref.at[tuple(idx)]


def _gather_weights(shards, axes, name):
    nt = len(shards)
    sizes = [s.shape[a] for s, a in zip(shards, axes)]
    full_shapes = [tuple(d * N_CHIPS if i == a else d for i, d in enumerate(s.shape)) for s, a in zip(shards, axes)]

    def body(*refs):
        srcs, fulls = refs[:nt], refs[nt:2 * nt]
        send_sems, recv_sems, local_sems = refs[2 * nt:]
        x, y, c = _mesh_pos()
        sibling = (x, y, 1 - c)
        chips = [(1 - x, y), (x, 1 - y), (1 - x, 1 - y)]
        k_me = 2 * x + y
        both, mine_l, other_l = pl.ds(0, 2), pl.ds(c, 1), pl.ds(1 - c, 1)
        started, locals_ = [], []
        for t in range(nt):
            place = functools.partial(_shard_view, fulls[t], axes[t], size=sizes[t])
            lc = pltpu.make_async_copy(srcs[t], place(kk=k_me, lsel=both), local_sems.at[t])
            lc.start()
            locals_.append(lc)
            for j, chip in enumerate(chips):
                cp = pltpu.make_async_remote_copy(
                    src_ref=srcs[t].at[mine_l], dst_ref=place(kk=k_me, lsel=mine_l),
                    send_sem=send_sems.at[6 * t + j], recv_sem=recv_sems.at[6 * t + j],
                    device_id=(*chip, c), device_id_type=MESH)
                cp.start()
                started.append(cp)
        for t in range(nt):
            place = functools.partial(_shard_view, fulls[t], axes[t], size=sizes[t])
            for j, chip in enumerate(chips):
                landed = place(kk=2 * chip[0] + chip[1], lsel=mine_l)
                fwd = pltpu.make_async_remote_copy(
                    src_ref=landed, dst_ref=landed, send_sem=send_sems.at[6 * t + 3 + j], recv_sem=recv_sems.at[6 * t + 3 + j],
                    device_id=sibling, device_id_type=MESH)
                pltpu.make_async_remote_copy(
                    src_ref=landed, dst_ref=landed, send_sem=send_sems.at[6 * t + j], recv_sem=recv_sems.at[6 * t + j],
                    device_id=(*chip, c), device_id_type=MESH).wait_recv()
                fwd.start()
                started.append(fwd)
        for t in range(nt):
            place = functools.partial(_shard_view, fulls[t], axes[t], size=sizes[t])
            for j, chip in enumerate(chips):
                theirs = place(kk=2 * chip[0] + chip[1], lsel=other_l)
                pltpu.make_async_remote_copy(
                    src_ref=theirs, dst_ref=theirs, send_sem=send_sems.at[6 * t + 3 + j], recv_sem=recv_sems.at[6 * t + 3 + j],
                    device_id=sibling, device_id_type=MESH).wait_recv()
        for cp in started:
            cp.wait_send()
        for lc in locals_:
            lc.wait()

    return pl.pallas_call(
        body, name=name, in_specs=[ANY] * nt, out_specs=[ANY] * nt,
        out_shape=[S_(fs, bf16) for fs in full_shapes],
        scratch_shapes=[pltpu.SemaphoreType.DMA((6 * nt,)), pltpu.SemaphoreType.DMA((6 * nt,)), pltpu.SemaphoreType.DMA((nt,))],
        compiler_params=_cp(VMEM_MID),
    )(*shards)


def _sibling_layer_exchange(grads, name):
    nt = len(grads)

    def body(*refs):
        srcs, outs = refs[:nt], refs[nt:2 * nt]
        send_sems, recv_sems = refs[2 * nt:]
        x, y, c = _mesh_pos()
        cps = []
        for t in range(nt):
            cp = pltpu.make_async_remote_copy(
                src_ref=srcs[t].at[pl.ds(1 - c, 1)], dst_ref=outs[t], send_sem=send_sems.at[t], recv_sem=recv_sems.at[t],
                device_id=(x, y, 1 - c), device_id_type=MESH)
            cp.start()
            cps.append(cp)
        for cp in cps:
            cp.wait()

    return pl.pallas_call(
        body, name=name, in_specs=[ANY] * nt, out_specs=[ANY] * nt,
        out_shape=[S_((1,) + g.shape[1:], f32) for g in grads],
        scratch_shapes=[pltpu.SemaphoreType.DMA((nt,)), pltpu.SemaphoreType.DMA((nt,))],
        compiler_params=_cp(VMEM_MID),
    )(*grads)


def _chip_scatter(sums, axes, name):
    nt = len(sums)
    sizes = [s.shape[a] // N_CHIPS for s, a in zip(sums, axes)]
    shard_shapes = [tuple(sz if i == a else d for i, d in enumerate(s.shape)) for s, a, sz in zip(sums, axes, sizes)]

    def body(*refs):
        srcs, outs = refs[:nt], refs[nt:2 * nt]
        send_sems, recv_sems, local_sems = refs[2 * nt:]
        x, y, c = _mesh_pos()
        chips = [(1 - x, y), (x, 1 - y), (1 - x, 1 - y)]
        k_me = 2 * x + y
        one = pl.ds(0, 1)
        cps, lcs = [], []
        for t in range(nt):
            view = functools.partial(_shard_view, srcs[t], axes[t], size=sizes[t], lsel=one)
            lc = pltpu.make_async_copy(view(kk=k_me), outs[t].at[k_me], local_sems.at[t])
            lc.start()
            lcs.append(lc)
            for j, chip in enumerate(chips):
                cp = pltpu.make_async_remote_copy(
                    src_ref=view(kk=2 * chip[0] + chip[1]), dst_ref=outs[t].at[k_me],
                    send_sem=send_sems.at[3 * t + j], recv_sem=recv_sems.at[3 * t + j],
                    device_id=(*chip, c), device_id_type=MESH)
                cp.start()
                cps.append(cp)
        for t in range(nt):
            view = functools.partial(_shard_view, srcs[t], axes[t], size=sizes[t], lsel=one)
            for j, chip in enumerate(chips):
                kj = 2 * chip[0] + chip[1]
                pltpu.make_async_remote_copy(
                    src_ref=view(kk=kj), dst_ref=outs[t].at[kj], send_sem=send_sems.at[3 * t + j], recv_sem=recv_sems.at[3 * t + j],
                    device_id=(*chip, c), device_id_type=MESH).wait_recv()
        for cp in cps:
            cp.wait_send()
        for lc in lcs:
            lc.wait()

    return pl.pallas_call(
        body, name=name, in_specs=[ANY] * nt, out_specs=[ANY] * nt,
        out_shape=[S_((N_CHIPS,) + ss, f32) for ss in shard_shapes],
        scratch_shapes=[pltpu.SemaphoreType.DMA((3 * nt,)), pltpu.SemaphoreType.DMA((3 * nt,)), pltpu.SemaphoreType.DMA((nt,))],
        compiler_params=_cp(VMEM_MID),
    )(*sums)


def _sibling_layer_merge(finals, name):
    nt = len(finals)

    def body(*refs):
        srcs, outs = refs[:nt], refs[nt:2 * nt]
        send_sems, recv_sems, local_sems = refs[2 * nt:]
        x, y, c = _mesh_pos()
        cps, lcs = [], []
        for t in range(nt):
            lc = pltpu.make_async_copy(srcs[t], outs[t].at[pl.ds(c, 1)], local_sems.at[t])
            lc.start()
            lcs.append(lc)
            cp = pltpu.make_async_remote_copy(
                src_ref=srcs[t], dst_ref=outs[t].at[pl.ds(c, 1)], send_sem=send_sems.at[t], recv_sem=recv_sems.at[t],
                device_id=(x, y, 1 - c), device_id_type=MESH)
            cp.start()
            cps.append(cp)
        for t in range(nt):
            pltpu.make_async_remote_copy(
                src_ref=srcs[t], dst_ref=outs[t].at[pl.ds(1 - c, 1)], send_sem=send_sems.at[t], recv_sem=recv_sems.at[t],
                device_id=(x, y, 1 - c), device_id_type=MESH).wait_recv()
        for cp in cps:
            cp.wait_send()
        for lc in lcs:
            lc.wait()

    return pl.pallas_call(
        body, name=name, in_specs=[ANY] * nt, out_specs=[ANY] * nt,
        out_shape=[S_((2,) + f.shape[1:], f32) for f in finals],
        scratch_shapes=[pltpu.SemaphoreType.DMA((nt,)), pltpu.SemaphoreType.DMA((nt,)), pltpu.SemaphoreType.DMA((nt,))],
        compiler_params=_cp(VMEM_MID),
    )(*finals)


def _reduce_scatter_weights(grads, axes):
    cidx = lax.axis_index("c").astype(jnp.int32).reshape(1)
    sib = _sibling_layer_exchange(grads, "rs_sibling_exchange")
    sums = []
    for t, (g, s) in enumerate(zip(grads, sib)):
        g3 = g.reshape(2, -1, g.shape[-1])
        sums.append(_add_layer(g3, s.reshape(1, -1, g.shape[-1]), cidx, f"rs_add_sibling_{t}").reshape((1,) + g.shape[1:]))
    parts = _chip_scatter(sums, axes, "rs_chip_scatter")
    finals = []
    for t, p in enumerate(parts):
        p3 = p.reshape(N_CHIPS, -1, p.shape[-1])
        finals.append(_sum_chips(p3, f"rs_sum_chips_{t}").reshape(p.shape[1:]))
    return _sibling_layer_merge(finals, "rs_sibling_merge")


def _pack_rows(arrays):
    flat = jnp.concatenate([a.reshape(-1) for a in arrays])
    pad = (-flat.size) % (8 * LANES)
    return jnp.pad(flat, (0, pad)).reshape(-1, LANES)


def _unpack_rows(packed, shapes):
    flat = packed.reshape(-1)
    out, off = [], 0
    for s in shapes:
        n = int(np.prod(s))
        out.append(flat[off:off + n].reshape(s))
        off += n
    return out


def _local_step(cfg, xs0, target, mods, norm_g, gu, dn, wi, wo, na_rpb, w_pool, pool_scale):
    S, L, T, D, F = cfg.S, cfg.L, cfg.T, cfg.D, cfg.F
    depth = norm_g.shape[0]
    cos, sin = _rope_tables(S, L)
    band, inv = _pool_tables(cfg.TM, L)
    onehot, sel = _rpb_reduce_tables()

    saved = []
    xs = xs0
    for l in range(depth):
        last = l == depth - 1
        wc = not last
        gvec = norm_g[l]
        ps = pool_scale[l].reshape(1, POOL_WIDTH)
        bexp = _expand_rpb(na_rpb[l])
        xs1, hb1, z1, y1 = _ffn_fwd(cfg, xs, mods[l], gvec, gu, dn, l, 0, 0, 0, True, f"ffn_fwd_{l}_0")
        hb2, q, k, v, u = _tmpre_fwd(cfg, xs1, mods[l], gvec, wi, cos, sin, l, f"tmpre_fwd_{l}")
        na = _na_fwd(cfg, q, k, v, bexp, f"na_fwd_{l}")
        if wc:
            na = jnp.concatenate([na, _ctx_attn_fwd(cfg, q, k, v, f"ctx_attn_fwd_{l}")], axis=0)
        pool = _pool_fwd(cfg, u, band, inv, w_pool[l], ps, wc, f"pool_fwd_{l}")
        xs2, opre = _tmpost_fwd(cfg, na, pool, wo, xs1, mods[l], gvec, l, wc, f"tmpost_fwd_{l}")
        xs3, hb3, z3, y3 = _ffn_fwd(cfg, xs2, mods[l], gvec, gu, dn, l, 1, 6, 4, wc, f"ffn_fwd_{l}_1")
        saved.append(dict(xs=xs, xs1=xs1, xs2=xs2, hb1=hb1, z1=z1, y1=y1, hb2=hb2, q=q, k=k, v=v, u=u, na=na, pool=pool,
                          opre=opre, hb3=hb3, z3=z3, y3=y3, bexp=bexp, ps=ps, gvec=gvec))
        xs = xs3

    dxs, loss_blk = _loss_head(cfg, xs, target, "loss_head")

    g_gu = lax.empty((depth, 2, D, 2 * F), f32)
    g_dn = lax.empty((depth, 2, F, D), f32)
    g_wi = lax.empty((depth, D, IN_WIDTH), f32)
    g_wo = lax.empty((depth, MIX_WIDTH, D), f32)
    small = [None] * depth
    for l in reversed(range(depth)):
        last = l == depth - 1
        wc = not last
        sv = saved[l]
        gvec = sv["gvec"]
        rows_b = cfg.T if wc else cfg.S
        dxs2, dz, dyb, ab, dm678, dg45 = _ffn_bwd(cfg, dxs, sv["xs2"], sv["z3"], sv["y3"], mods[l], gvec, gu, dn,
                                                   l, 1, 6, 4, wc, f"ffn_bwd_{l}_1")
        g_gu = _wgrad(sv["hb3"], dz, rows_b, g_gu, (l, 1), f"wgrad_gu_{l}_1")
        g_dn = _wgrad(ab, dyb, rows_b, g_dn, (l, 1), f"wgrad_dn_{l}_1")
        dop, dmix, dm5, dg3 = _tmpost_bwd(cfg, dxs2, sv["opre"], wo, mods[l], gvec, l, wc, f"tmpost_bwd_{l}")
        mixed = jnp.concatenate([sv["na"], sv["pool"]], axis=1)
        g_wo = _wgrad(mixed, dop, rows_b, g_wo, (l,), f"wgrad_wo_{l}")
        du, dwp, dps = _pool_bwd(cfg, dmix, sv["u"], band, inv, w_pool[l], sv["ps"], wc, f"pool_bwd_{l}")
        dq, dk, dv, dkc, dvc, dbexp = _na_bwd(cfg, dmix, sv["q"], sv["k"], sv["v"], sv["bexp"], f"na_bwd_{l}")
        drpb = _rpb_reduce(dbexp, onehot, sel, f"rpb_reduce_{l}")
        if wc:
            dqc, dkc2, dvc2 = _ctx_attn_bwd(cfg, dmix, sv["q"], sv["k"], sv["v"], f"ctx_attn_bwd_{l}")
            dq_all = jnp.concatenate([dq, dqc], axis=0)
            dk_all = jnp.concatenate([dk, dkc + dkc2], axis=0)
            dv_all = jnp.concatenate([dv, dvc + dvc2], axis=0)
            du_all = du
        else:
            zc = jnp.zeros((L, NA_WIDTH), f32)
            dq_all = jnp.concatenate([dq, zc], axis=0)
            dk_all = jnp.concatenate([dk, dkc], axis=0)
            dv_all = jnp.concatenate([dv, dvc], axis=0)
            du_all = jnp.concatenate([du, zc], axis=0)
        dproj = _rope_bwd_assemble(cfg, dq_all, dk_all, dv_all, du_all, cos, sin, f"rope_bwd_{l}")
        g_wi = _wgrad(sv["hb2"], dproj, cfg.T, g_wi, (l,), f"wgrad_wi_{l}")
        dxs1, dm34, dg2 = _tmpre_bwd(cfg, dproj, wi, sv["xs1"], mods[l], gvec, dxs2, wc, l, f"tmpre_bwd_{l}")
        dxs, dz, dyb, ab, dm012, dg01 = _ffn_bwd(cfg, dxs1, sv["xs"], sv["z1"], sv["y1"], mods[l], gvec, gu, dn,
                                                  l, 0, 0, 0, True, f"ffn_bwd_{l}_0")
        g_gu = _wgrad(sv["hb1"], dz, cfg.T, g_gu, (l, 0), f"wgrad_gu_{l}_0")
        g_dn = _wgrad(ab, dyb, cfg.T, g_dn, (l, 0), f"wgrad_dn_{l}_0")
        if not wc:
            zero = lambda a: jnp.concatenate([a, jnp.zeros_like(a)], axis=0)
            dm5, dm678 = zero(dm5), zero(dm678)
        dmods = jnp.concatenate([dm012, dm34, dm5, dm678], axis=1)
        dgs = jnp.concatenate([dg01, dg2, dg3, dg45], axis=0)
        small[l] = dict(dmods=dmods, dg=dgs, drpb=drpb, dwp=dwp, dps=dps)
    return loss_blk, dxs, (g_gu, g_dn, g_wi, g_wo), small


def kernel(x, c, ctx, c_ctx, w_mod, b_mod, norm_g, w_ffn_gate_up, w_ffn_down, w_in, w_out, na_rpb, w_pool, pool_scale, loss_target, m_c_ctx, m_w_mod, m_b_mod, m_norm_g, m_w_ffn_gate_up, m_w_ffn_down, m_w_in, m_w_out, m_na_rpb, m_w_pool, m_pool_scale, v_c_ctx, v_w_mod, v_b_mod, v_norm_g, v_w_ffn_gate_up, v_w_ffn_down, v_w_in, v_w_out, v_na_rpb, v_w_pool, v_pool_scale):
    S, D = x.shape[1], x.shape[2]
    L = ctx.shape[1]
    depth = w_mod.shape[0]
    F = w_ffn_down.shape[2] * N_CHIPS
    nmod = w_mod.shape[2]
    gsh = norm_g.shape[2]
    cfg = _Cfg(S, L, D, F)
    mx, my, mc = _mesh_pos()
    chip = 2 * mx + my
    dev = 4 * mx + 2 * my + mc

    shards = [_cast_bf16(w_ffn_gate_up, "cast_gu"), _cast_bf16(w_ffn_down, "cast_dn"),
              _cast_bf16(w_in, "cast_wi"), _cast_bf16(w_out, "cast_wo")]
    axes = [3, 2, 2, 1]
    gu, dn, wi, wo = _gather_weights(shards, axes, "gather_weights")

    c_all = _all_gather_small(jnp.pad(c, ((0, 7), (0, 0))), "gather_c").reshape(N_DEV, 8, D)[:, 0]
    cvecs = jnp.concatenate([c_all, c_ctx[None], jnp.zeros((7, D), f32)], axis=0)
    b_shard = lax.dynamic_slice_in_dim(b_mod, chip * nmod, nmod, axis=1).reshape(depth, 1, nmod)
    m_part, silu_c = _modvec_fwd(cvecs, w_mod, b_shard, "modvec_fwd")
    m_all = _all_gather_small(m_part.reshape(depth * 16, nmod), "gather_mod").reshape(N_DEV, depth, 16, nmod)
    m_full = jnp.concatenate([m_all[2 * j] for j in range(N_CHIPS)], axis=-1)
    m_mine = lax.dynamic_index_in_dim(m_full, dev, axis=1, keepdims=False)
    mods = jnp.stack([m_mine, m_full[:, 8]], axis=1).reshape(depth, 2, N_MOD, D)

    norm_g_full = _all_gather_small(_pack_rows([norm_g]), "gather_norm_g")
    rows_g = norm_g_full.shape[0] // N_DEV
    ng = norm_g_full.reshape(N_DEV, rows_g * LANES)[:, :norm_g.size].reshape(N_DEV, depth, 6, gsh)
    norm_g_all = jnp.concatenate([ng[2 * j] for j in range(N_CHIPS)], axis=-1)
    xs0 = jnp.concatenate([x[0], ctx[0]], axis=0)
    loss_blk, dxs0, wgrads, small = _local_step(cfg, xs0, loss_target[0], mods, norm_g_all, gu, dn, wi, wo,
                                                na_rpb, w_pool, pool_scale)
    loss = lax.psum(loss_blk[0, 0], ("x", "y", "c"))
    grad_x = dxs0[:S][None]

    g_gu, g_dn, g_wi, g_wo = _reduce_scatter_weights(list(wgrads), axes)
    names = ("dmods", "dg", "drpb", "dwp", "dps")
    parts = [jnp.stack([small[l][n] for l in range(depth)]) for n in names]
    shapes = [p.shape for p in parts]
    packed = _pack_rows(parts)
    gathered = _all_gather_small(packed, "gather_small").reshape(N_DEV, packed.shape[0], LANES)
    total = _unpack_rows(_sum_devices(gathered, "sum_small"), shapes)
    dmods_sum, dg_sum, drpb_sum, dwp_sum, dps_sum = total
    dmods_each = jnp.stack([_unpack_rows(gathered[j], shapes[:1])[0] for j in range(N_DEV)])
    dm_rows = jnp.concatenate([jnp.transpose(dmods_each[:, :, 0], (1, 0, 2, 3)).reshape(depth, N_DEV, N_MOD * D),
                               dmods_sum[:, 1].reshape(depth, 1, N_MOD * D),
                               jnp.zeros((depth, 7, N_MOD * D), f32)], axis=1)
    dm_shard = lax.dynamic_slice_in_dim(dm_rows, chip * nmod, nmod, axis=2)
    grad_w_mod, gc_part = _modvec_bwd(silu_c.T, dm_shard, w_mod, "modvec_bwd")
    gc_all = _all_gather_small(gc_part.reshape(depth * 8, D), "gather_gc").reshape(N_DEV, depth, 8, D)
    grad_b_mod, grad_c_ctx = _small_finish(dm_rows, gc_all, c_ctx)
    grad_norm_g = lax.dynamic_slice_in_dim(dg_sum, chip * gsh, gsh, axis=2)
    grad_na_rpb = drpb_sum[:, :, :2 * NA_KH - 1, :2 * NA_KW - 1]
    grad_w_pool = dwp_sum
    grad_pool_scale = dps_sum.reshape(depth, POOL_WIDTH)

    grads = [grad_c_ctx, grad_w_mod, grad_b_mod, grad_norm_g, g_gu, g_dn, g_wi, g_wo, grad_na_rpb, grad_w_pool, grad_pool_scale]
    ws = [c_ctx, w_mod, b_mod, norm_g, w_ffn_gate_up, w_ffn_down, w_in, w_out, na_rpb, w_pool, pool_scale]
    ms = [m_c_ctx, m_w_mod, m_b_mod, m_norm_g, m_w_ffn_gate_up, m_w_ffn_down, m_w_in, m_w_out, m_na_rpb, m_w_pool, m_pool_scale]
    vs = [v_c_ctx, v_w_mod, v_b_mod, v_norm_g, v_w_ffn_gate_up, v_w_ffn_down, v_w_in, v_w_out, v_na_rpb, v_w_pool, v_pool_scale]
    tags = ["c_ctx", "w_mod", "b_mod", "norm_g", "gate_up", "down", "w_in", "w_out", "na_rpb", "w_pool", "pool_scale"]
    upd = [_adamw(w_, g_, m_, v_, f"adamw_{t}") for w_, g_, m_, v_, t in zip(ws, grads, ms, vs, tags)]
    return (loss, grad_x, *grads, *[u_[0] for u_ in upd], *[u_[1] for u_ in upd], *[u_[2] for u_ in upd])


def _small_finish(dm_rows, gc_all, c_ctx):
    depth, _, n = dm_rows.shape
    D = c_ctx.shape[0]

    def body(dm_ref, gc_ref, c_ref, gb_ref, gcx_ref):
        acc = dm_ref[:, 0]
        for j in range(1, N_DEV + 1):
            acc = acc + dm_ref[:, j]
        gb_ref[...] = acc
        t = jnp.zeros((1, D), f32)
        for l in range(depth):
            for j in range(N_CHIPS):
                t = t + gc_ref[2 * j, l, 0:1, :]
        cv = c_ref[...]
        sg = _sigmoid(cv)
        gcx_ref[...] = t * (sg * (1.0 + cv * (1.0 - sg)))

    gb, gcx = pl.pallas_call(
        body, name="small_finish",
        out_shape=[S_((depth, n), f32), S_((1, D), f32)],
        compiler_params=_cp(VMEM_MID),
    )(dm_rows, gc_all, c_ctx.reshape(1, D))
    return gb, gcx.reshape(D)
```

```python
import functools

import numpy as np
import jax
import jax.numpy as jnp
from jax import lax
from jax.experimental import pallas as pl
from jax.experimental.pallas import tpu as pltpu

f32, bf16 = jnp.float32, jnp.bfloat16

GRID_W = 64
N_MOD = 9
NA_HEADS = 8
HEAD_DIM = 64
NA_WIDTH = NA_HEADS * HEAD_DIM
NA_KH = 8
NA_KW = 16
POOL_GROUPS = 4
POOL_CH = 128
POOL_WIDTH = POOL_GROUPS * POOL_CH
POOL_WINDOWS = (2, 4, 8, 16)
IN_WIDTH = 3 * NA_WIDTH + POOL_WIDTH
MIX_WIDTH = NA_WIDTH + POOL_WIDTH
ROPE_THETA = 10000.0
ROPE_PAIRS = HEAD_DIM // 4
RMS_EPS = 1e-6
NEG_INF = -1e30
ADAM_LR, ADAM_B1, ADAM_B2, ADAM_EPS, ADAM_WD, ADAM_STEP = 0.001, 0.9, 0.999, 1e-08, 0.01, 10

N_DEV = 8
N_CHIPS = 4
LANES = 128
MIB = 1024 * 1024
VMEM_BIG = 52 * MIB
VMEM_MID = 40 * MIB
MESH = pl.DeviceIdType.MESH
ANY = pl.BlockSpec(memory_space=pl.ANY)
S_ = jax.ShapeDtypeStruct


def _cp(vmem=VMEM_MID, sem=None):
    return pltpu.CompilerParams(vmem_limit_bytes=vmem, dimension_semantics=sem)


def _sigmoid(x):
    return 1.0 / (1.0 + jnp.exp(-x))


def _rms_hat(x):
    rinv = lax.rsqrt(jnp.mean(x * x, axis=-1, keepdims=True) + RMS_EPS)
    return x * rinv, rinv


def _rms_bwd(dxhat, xhat, rinv):
    return rinv * (dxhat - xhat * jnp.mean(dxhat * xhat, axis=-1, keepdims=True))


def _rsum(a):
    return jnp.sum(a, axis=0, keepdims=True)


def _nt(a, b):
    return lax.dot_general(a, b, (((1,), (1,)), ((), ())), preferred_element_type=f32)


def _tn(a, b):
    return lax.dot_general(a, b, (((0,), (0,)), ((), ())), preferred_element_type=f32)


def _nn(a, b):
    return jnp.dot(a, b, preferred_element_type=f32)


def _swap16(x):
    lane = lax.broadcasted_iota(jnp.int32, x.shape, 1)
    n = x.shape[1]
    return jnp.where((lane % 32) < 16, pltpu.roll(x, n - 16, 1), pltpu.roll(x, 16, 1))


def _rope_tables(s_len, l_len):
    t = np.arange(s_len)
    inv = ROPE_THETA ** (-np.arange(ROPE_PAIRS, dtype=np.float32) / ROPE_PAIRS)
    ang_r = (t // GRID_W).astype(np.float32)[:, None] * inv
    ang_c = (t % GRID_W).astype(np.float32)[:, None] * inv
    cos = np.concatenate([np.cos(ang_r), np.cos(ang_r), np.cos(ang_c), np.cos(ang_c)], axis=-1)
    sin = np.concatenate([-np.sin(ang_r), np.sin(ang_r), -np.sin(ang_c), np.sin(ang_c)], axis=-1)
    cos = np.concatenate([cos, np.ones((l_len, HEAD_DIM), np.float32)], axis=0)
    sin = np.concatenate([sin, np.zeros((l_len, HEAD_DIM), np.float32)], axis=0)
    return (jnp.asarray(np.tile(cos, (1, 2)), f32), jnp.asarray(np.tile(sin, (1, 2)), f32))


def _pool_tables(tm, l_len):
    band = np.zeros((2, POOL_GROUPS, tm, tm), np.float32)
    inv = np.zeros((2, POOL_GROUPS, tm, 1), np.float32)
    for typ, length in ((0, GRID_W), (1, l_len)):
        for g, w in enumerate(POOL_WINDOWS):
            for t in range(tm):
                base, p = (t // length) * length, t % length
                lo = min(max(p - w // 2, 0), length)
                hi = min(max(p - w // 2 + w, 0), length)
                band[typ, g, t, base + lo:base + hi] = 1.0
                inv[typ, g, t, 0] = 1.0 / (hi - lo)
    return jnp.asarray(band, bf16), jnp.asarray(inv, f32)


NA_QR = 4
NA_WR = NA_KH + NA_QR - 1
NA_TYPES = 3
NA_SEL_ROWS = 136


def _rpb_index_tables():
    j = np.arange(GRID_W)
    col_start = np.clip(j - NA_KW // 2, 0, GRID_W - NA_KW)
    valid = (j[None, :] >= col_start[:, None]) & (j[None, :] < col_start[:, None] + NA_KW)
    dc = np.clip(j[None, :] - j[:, None] + NA_KW - 1, 0, 2 * NA_KW - 2)
    i = np.arange(NA_QR)[:, None]
    kk = np.arange(NA_WR)[None, :]
    off = np.stack([np.zeros_like(i), i, np.full_like(i, NA_QR - 1)])
    d = np.stack([kk - i + NA_KH - 1, kk - i + NA_KH - 1 - NA_QR, kk - i])
    row_ok = (kk[None] >= off) & (kk[None] < off + NA_KH)
    assert (d[row_ok] >= 0).all() and (d[row_ok] <= 2 * NA_KH - 2).all()
    return valid, dc, np.clip(d, 0, 2 * NA_KH - 2), row_ok


def _expand_rpb(rpb):
    valid, dc, d, row_ok = _rpb_index_tables()
    b = rpb[:, d]
    b = b[..., dc]
    ok = row_ok[None, :, :, :, None, None] & valid[None, None, None, None]
    b = jnp.where(ok, b, NEG_INF)
    b = jnp.transpose(b, (1, 0, 2, 4, 3, 5))
    return b.reshape(NA_TYPES, NA_HEADS, NA_QR * GRID_W, NA_WR * GRID_W)


def _rpb_reduce_tables():
    _, dc, d, row_ok = _rpb_index_tables()
    onehot = np.zeros((GRID_W * GRID_W, LANES), np.float32)
    onehot[np.arange(GRID_W * GRID_W), dc.reshape(-1)] = 1.0
    sel = np.zeros((16, NA_SEL_ROWS), np.float32)
    flat_d, flat_ok = d.reshape(-1), row_ok.reshape(-1)
    for n in range(flat_d.size):
        if flat_ok[n]:
            sel[flat_d[n], n] = 1.0
    return jnp.asarray(onehot), jnp.asarray(sel)


class _Cfg:
    def __init__(self, s_len, l_len, d, f):
        self.S, self.L, self.D, self.F = s_len, l_len, d, f
        self.T = s_len + l_len
        self.TM = 256 if l_len % 256 == 0 else 128
        assert l_len == self.TM, "context length must equal the row tile"
        assert s_len % self.TM == 0 and s_len % GRID_W == 0
        self.nxt = s_len // self.TM
        self.ntt = self.T // self.TM
        self.rows = s_len // GRID_W
        assert self.rows >= 2 * NA_KH
        assert f % (2 * LANES) == 0
        self.FC = f // 2

    def ntiles(self, with_ctx):
        return self.ntt if with_ctx else self.nxt


def _typ(cfg):
    return lambda i: (jnp.minimum(i // cfg.nxt, 1), 0, 0)


def _ffn_fwd(cfg, xs, mods, gvec, wgu, wd, l, idx, mi, gi, with_ctx, name):
    TM, D, F, FC = cfg.TM, cfg.D, cfg.F, cfg.FC
    nt = cfg.ntiles(with_ctx)
    R = nt * TM

    def body(xs_ref, mods_ref, g_ref, wgu_hbm, wd_hbm, out_ref, hb_ref, z_ref, y_ref, wgu_v, wd_v, sem):
        @pl.when(pl.program_id(0) == 0)
        def _():
            c0 = pltpu.make_async_copy(wgu_hbm.at[l, idx], wgu_v, sem.at[0])
            c1 = pltpu.make_async_copy(wd_hbm.at[l, idx], wd_v, sem.at[1])
            c0.start(); c1.start(); c0.wait(); c1.wait()
        x = xs_ref[...]
        m = mods_ref[0]
        sh, sc, gt = m[mi:mi + 1], m[mi + 1:mi + 2], m[mi + 2:mi + 3]
        xhat, _ = _rms_hat(x)
        h = (xhat * g_ref[gi:gi + 1]) * (1.0 + sc) + sh
        hb = h.astype(bf16)
        hb_ref[...] = hb
        y = jnp.zeros((TM, D), f32)
        for ch in range(F // FC):
            zg = _nn(hb, wgu_v[:, ch * FC:(ch + 1) * FC])
            zu = _nn(hb, wgu_v[:, F + ch * FC:F + (ch + 1) * FC])
            z_ref[:, ch * FC:(ch + 1) * FC] = zg.astype(bf16)
            z_ref[:, F + ch * FC:F + (ch + 1) * FC] = zu.astype(bf16)
            a = (zg * _sigmoid(zg)) * zu
            y = y + _nn(a.astype(bf16), wd_v[ch * FC:(ch + 1) * FC, :])
        y_ref[...] = y
        yhat, _ = _rms_hat(y)
        out_ref[...] = x + 0.5 * gt * (yhat * g_ref[gi + 1:gi + 2])

    rt = lambda c: pl.BlockSpec((TM, c), lambda i: (i, 0))
    return pl.pallas_call(
        body, name=name, grid=(nt,),
        in_specs=[rt(D), pl.BlockSpec((1, N_MOD, D), _typ(cfg)), pl.BlockSpec((6, D), lambda i: (0, 0)), ANY, ANY],
        out_specs=[rt(D), rt(D), rt(2 * F), rt(D)],
        out_shape=[S_((R, D), f32), S_((R, D), bf16), S_((R, 2 * F), bf16), S_((R, D), f32)],
        scratch_shapes=[pltpu.VMEM((D, 2 * F), bf16), pltpu.VMEM((F, D), bf16), pltpu.SemaphoreType.DMA((2,))],
        compiler_params=_cp(VMEM_BIG, ("arbitrary",)),
    )(xs, mods, gvec, wgu, wd)


def _ffn_bwd(cfg, dout, xs, z, y, mods, gvec, wgu, wd, l, idx, mi, gi, with_ctx, name):
    TM, D, F, FC = cfg.TM, cfg.D, cfg.F, cfg.FC
    nt = cfg.ntiles(with_ctx)
    R = nt * TM
    ntyp = 2 if with_ctx else 1

    def body(do_ref, xs_ref, z_ref, y_ref, mods_ref, g_ref, wgu_hbm, wd_hbm,
             dx_ref, dz_ref, dy_ref, a_ref, dm_ref, dg_ref, wgu_v, wd_v, sem):
        i = pl.program_id(0)

        @pl.when(i == 0)
        def _():
            c0 = pltpu.make_async_copy(wgu_hbm.at[l, idx], wgu_v, sem.at[0])
            c1 = pltpu.make_async_copy(wd_hbm.at[l, idx], wd_v, sem.at[1])
            c0.start(); c1.start(); c0.wait(); c1.wait()
            dg_ref[...] = jnp.zeros_like(dg_ref)

        @pl.when((i == 0) | (i == cfg.nxt))
        def _():
            dm_ref[...] = jnp.zeros_like(dm_ref)

        do = do_ref[...]
        x = xs_ref[...]
        m = mods_ref[0]
        sc, gt = m[mi + 1:mi + 2], m[mi + 2:mi + 3]
        g_pre, g_post = g_ref[gi:gi + 1], g_ref[gi + 1:gi + 2]
        xhat, rinv0 = _rms_hat(x)
        n0 = xhat * g_pre
        yhat, rinv1 = _rms_hat(y_ref[...])
        d_gt = _rsum(0.5 * do * (yhat * g_post))
        dr = (0.5 * gt) * do
        dg_post = _rsum(dr * yhat)
        dy = _rms_bwd(dr * g_post, yhat, rinv1)
        dyb = dy.astype(bf16)
        dy_ref[...] = dyb
        dh = jnp.zeros((TM, D), f32)
        for ch in range(F // FC):
            zg = z_ref[:, ch * FC:(ch + 1) * FC].astype(f32)
            zu = z_ref[:, F + ch * FC:F + (ch + 1) * FC].astype(f32)
            sg = _sigmoid(zg)
            silu = zg * sg
            a_ref[:, ch * FC:(ch + 1) * FC] = (silu * zu).astype(bf16)
            da = _nt(dyb, wd_v[ch * FC:(ch + 1) * FC, :])
            dzu = (da * silu).astype(bf16)
            dzg = (da * zu * (sg * (1.0 + zg * (1.0 - sg)))).astype(bf16)
            dz_ref[:, ch * FC:(ch + 1) * FC] = dzg
            dz_ref[:, F + ch * FC:F + (ch + 1) * FC] = dzu
            dh = dh + _nt(dzg, wgu_v[:, ch * FC:(ch + 1) * FC]) + _nt(dzu, wgu_v[:, F + ch * FC:F + (ch + 1) * FC])
        d_sh = _rsum(dh)
        d_sc = _rsum(dh * n0)
        dn = dh * (1.0 + sc)
        dg_pre = _rsum(dn * xhat)
        dx_ref[...] = do + _rms_bwd(dn * g_pre, xhat, rinv0)
        dm_ref[0] += jnp.concatenate([d_sh, d_sc, d_gt], axis=0)
        dg_ref[...] += jnp.concatenate([dg_pre, dg_post], axis=0)

    rt = lambda c: pl.BlockSpec((TM, c), lambda i: (i, 0))
    return pl.pallas_call(
        body, name=name, grid=(nt,),
        in_specs=[rt(D), rt(D), rt(2 * F), rt(D), pl.BlockSpec((1, N_MOD, D), _typ(cfg)),
                  pl.BlockSpec((6, D), lambda i: (0, 0)), ANY, ANY],
        out_specs=[rt(D), rt(2 * F), rt(D), rt(F), pl.BlockSpec((1, 3, D), _typ(cfg)), pl.BlockSpec((2, D), lambda i: (0, 0))],
        out_shape=[S_((R, D), f32), S_((R, 2 * F), bf16), S_((R, D), bf16), S_((R, F), bf16),
                   S_((ntyp, 3, D), f32), S_((2, D), f32)],
        scratch_shapes=[pltpu.VMEM((D, 2 * F), bf16), pltpu.VMEM((F, D), bf16), pltpu.SemaphoreType.DMA((2,))],
        compiler_params=_cp(VMEM_BIG, ("arbitrary",)),
    )(dout, xs, z, y, mods, gvec, wgu, wd)


def _wgrad(a, b, k_rows, buf, sel, name, tk=512):
    M, N = a.shape[1], b.shape[1]
    tk = tk if k_rows % tk == 0 else 256 if k_rows % 256 == 0 else 128
    tn = N
    for cand in (1408, 1024, 512):
        if N % cand == 0 and N > cand:
            tn = cand
            break
    nk = k_rows // tk
    nsel = len(sel)

    def body(a_ref, b_ref, buf_hbm, o_ref):
        @pl.when(pl.program_id(1) == 0)
        def _():
            o_ref[...] = jnp.zeros_like(o_ref)
        o_ref[...] += _tn(a_ref[...], b_ref[...])

    return pl.pallas_call(
        body, name=name, grid=(N // tn, nk),
        in_specs=[pl.BlockSpec((tk, M), lambda n, k: (k, 0)), pl.BlockSpec((tk, tn), lambda n, k: (k, n)), ANY],
        out_specs=pl.BlockSpec((None,) * nsel + (M, tn), lambda n, k: tuple(sel) + (0, n)),
        out_shape=S_(buf.shape, f32),
        input_output_aliases={2: 0},
        compiler_params=_cp(VMEM_MID, ("arbitrary", "arbitrary")),
    )(a, b, buf)


def _tmpre_fwd(cfg, xs, mods, gvec, w_in, cos, sin, l, name):
    TM, D = cfg.TM, cfg.D
    nt, R = cfg.ntt, cfg.T
    W = NA_WIDTH

    def body(xs_ref, mods_ref, g_ref, w_ref, cos_ref, sin_ref, hb_ref, q_ref, k_ref, v_ref, u_ref):
        x = xs_ref[...]
        m = mods_ref[0]
        xhat, _ = _rms_hat(x)
        hb = ((xhat * g_ref[2:3]) * (1.0 + m[4:5]) + m[3:4]).astype(bf16)
        hb_ref[...] = hb
        p = _nn(hb, w_ref[...])
        cs = jnp.tile(cos_ref[...], (1, W // LANES))
        sn = jnp.tile(sin_ref[...], (1, W // LANES))
        q = p[:, 0:W]
        k = p[:, W:2 * W]
        q_ref[...] = ((q * cs + _swap16(q) * sn) * (HEAD_DIM ** -0.5)).astype(bf16)
        k_ref[...] = (k * cs + _swap16(k) * sn).astype(bf16)
        v_ref[...] = p[:, 2 * W:3 * W].astype(bf16)
        u_ref[...] = p[:, 3 * W:]

    rt = lambda c: pl.BlockSpec((TM, c), lambda i: (i, 0))
    return pl.pallas_call(
        body, name=name, grid=(nt,),
        in_specs=[rt(D), pl.BlockSpec((1, N_MOD, D), _typ(cfg)), pl.BlockSpec((6, D), lambda i: (0, 0)),
                  pl.BlockSpec((None, D, IN_WIDTH), lambda i: (l, 0, 0)), rt(LANES), rt(LANES)],
        out_specs=[rt(D), rt(W), rt(W), rt(W), rt(POOL_WIDTH)],
        out_shape=[S_((R, D), bf16), S_((R, W), bf16), S_((R, W), bf16), S_((R, W), bf16), S_((R, POOL_WIDTH), f32)],
        compiler_params=_cp(VMEM_MID, ("arbitrary",)),
    )(xs, mods, gvec, w_in, cos, sin)


def _rope_bwd_assemble(cfg, lat, ctx_terms, du_has_ctx, cos, sin, name):
    TM = cfg.TM
    W = NA_WIDTH
    n_ctx = [len(t) for t in ctx_terms]
    flat_ctx = [a for t in ctx_terms for a in t]

    def body(*refs):
        dq_ref, dk_ref, dv_ref, du_ref = refs[:4]
        ctx_refs = refs[4:4 + len(flat_ctx)]
        cos_ref, sin_ref, o_ref = refs[4 + len(flat_ctx):]
        is_ctx = pl.program_id(0) >= cfg.nxt
        vals, off = [], 0
        for lat_ref, n in zip((dq_ref, dk_ref, dv_ref), n_ctx):
            cv = jnp.zeros((TM, W), f32)
            for r_ in ctx_refs[off:off + n]:
                cv = cv + r_[...]
            off += n
            vals.append(jnp.where(is_ctx, cv, lat_ref[...]))
        du_ = du_ref[...] if du_has_ctx else jnp.where(is_ctx, 0.0, du_ref[...])
        cs = jnp.tile(cos_ref[...], (1, W // LANES))
        sn = jnp.tile(sin_ref[...], (1, W // LANES))
        dq_ = vals[0] * (HEAD_DIM ** -0.5)
        dk_ = vals[1]
        o_ref[:, 0:W] = (dq_ * cs + _swap16(dq_ * sn)).astype(bf16)
        o_ref[:, W:2 * W] = (dk_ * cs + _swap16(dk_ * sn)).astype(bf16)
        o_ref[:, 2 * W:3 * W] = vals[2].astype(bf16)
        o_ref[:, 3 * W:] = du_.astype(bf16)

    rt = lambda c: pl.BlockSpec((TM, c), lambda i: (i, 0))
    lat_spec = pl.BlockSpec((TM, W), lambda i: (jnp.minimum(i, cfg.nxt - 1), 0))
    du_spec = rt(POOL_WIDTH) if du_has_ctx else lat_spec
    return pl.pallas_call(
        body, name=name, grid=(cfg.ntt,),
        in_specs=[lat_spec, lat_spec, lat_spec, du_spec] + [pl.BlockSpec((TM, W), lambda i: (0, 0))] * len(flat_ctx)
                 + [rt(LANES), rt(LANES)],
        out_specs=rt(IN_WIDTH), out_shape=S_((cfg.T, IN_WIDTH), bf16),
        compiler_params=_cp(VMEM_MID, ("arbitrary",)),
    )(*lat, *flat_ctx, cos, sin)


def _tmpre_bwd(cfg, dproj, w_in, xs, mods, gvec, dres, res_with_ctx, l, name):
    TM, D = cfg.TM, cfg.D
    nt, R = cfg.ntt, cfg.T
    nres = cfg.ntiles(res_with_ctx)

    def body(dp_ref, w_ref, xs_ref, mods_ref, g_ref, dres_ref, dx_ref, dm_ref, dg_ref):
        i = pl.program_id(0)

        @pl.when(i == 0)
        def _():
            dg_ref[...] = jnp.zeros_like(dg_ref)

        @pl.when((i == 0) | (i == cfg.nxt))
        def _():
            dm_ref[...] = jnp.zeros_like(dm_ref)

        dh = _nt(dp_ref[...], w_ref[...])
        x = xs_ref[...]
        m = mods_ref[0]
        g2 = g_ref[2:3]
        xhat, rinv = _rms_hat(x)
        d_sh = _rsum(dh)
        d_sc = _rsum(dh * (xhat * g2))
        dn = dh * (1.0 + m[4:5])
        dg_ref[...] += _rsum(dn * xhat)
        dx = _rms_bwd(dn * g2, xhat, rinv)
        res = dres_ref[...]
        if nres < nt:
            res = jnp.where(i < nres, res, 0.0)
        dx_ref[...] = res + dx
        dm_ref[0] += jnp.concatenate([d_sh, d_sc], axis=0)

    rt = lambda c: pl.BlockSpec((TM, c), lambda i: (i, 0))
    return pl.pallas_call(
        body, name=name, grid=(nt,),
        in_specs=[rt(IN_WIDTH), pl.BlockSpec((None, D, IN_WIDTH), lambda i: (l, 0, 0)), rt(D),
                  pl.BlockSpec((1, N_MOD, D), _typ(cfg)), pl.BlockSpec((6, D), lambda i: (0, 0)),
                  pl.BlockSpec((TM, D), lambda i: (jnp.minimum(i, nres - 1), 0))],
        out_specs=[rt(D), pl.BlockSpec((1, 2, D), _typ(cfg)), pl.BlockSpec((1, D), lambda i: (0, 0))],
        out_shape=[S_((R, D), f32), S_((2, 2, D), f32), S_((1, D), f32)],
        compiler_params=_cp(VMEM_MID, ("arbitrary",)),
    )(dproj, w_in, xs, mods, gvec, dres)


def _na_block(cfg, b):
    return jnp.clip(NA_QR * b - NA_KH // 2, 0, cfg.rows - NA_WR)


def _na_load_bias(b, nb, b_hbm, b_v, sem):
    for typ, at in ((0, 0), (1, 1), (2, nb - 1)):
        @pl.when(b == at)
        def _(typ=typ):
            cp = pltpu.make_async_copy(b_hbm.at[typ], b_v, sem)
            cp.start()
            cp.wait()


def _na_probs(qh, klh, kch, bias):
    s_loc = _nt(qh, klh) + bias
    s_ctx = _nt(qh, kch)
    mx = jnp.maximum(jnp.max(s_loc, axis=-1, keepdims=True), jnp.max(s_ctx, axis=-1, keepdims=True))
    e_loc = jnp.exp(s_loc - mx)
    e_ctx = jnp.exp(s_ctx - mx)
    inv = 1.0 / (jnp.sum(e_loc, axis=-1, keepdims=True) + jnp.sum(e_ctx, axis=-1, keepdims=True))
    return e_loc * inv, e_ctx * inv


def _na_fwd(cfg, q, k, v, bexp, name):
    S, L, T = cfg.S, cfg.L, cfg.T
    NQ, NW = NA_QR * GRID_W, NA_WR * GRID_W
    nb = cfg.rows // NA_QR

    def body(q_ref, k_hbm, v_hbm, b_hbm, o_ref, k_v, v_v, b_v, sem):
        b = pl.program_id(0)

        @pl.when(b == 0)
        def _():
            cs = [pltpu.make_async_copy(k_hbm, k_v, sem.at[0]), pltpu.make_async_copy(v_hbm, v_v, sem.at[1])]
            for c_ in cs:
                c_.start()
            for c_ in cs:
                c_.wait()

        _na_load_bias(b, nb, b_hbm, b_v, sem.at[2])
        st = pl.multiple_of(_na_block(cfg, b) * GRID_W, GRID_W)
        first = lax.broadcasted_iota(jnp.int32, (NQ, LANES), 1) < HEAD_DIM
        for hp in range(NA_HEADS // 2):
            ls = slice(hp * LANES, (hp + 1) * LANES)
            q2 = q_ref[:, ls]
            kl, vl = k_v[pl.ds(st, NW), ls], v_v[pl.ds(st, NW), ls]
            kc, vc = k_v[S:T, ls], v_v[S:T, ls]
            o2 = []
            for hh in range(2):
                qm = jnp.where(first if hh == 0 else ~first, q2, jnp.zeros_like(q2))
                p_loc, p_ctx = _na_probs(qm, kl, kc, b_v[2 * hp + hh])
                o2.append(_nn(p_loc.astype(bf16), vl) + _nn(p_ctx.astype(bf16), vc))
            o_ref[:, ls] = jnp.where(first, o2[0], o2[1]).astype(bf16)

    return pl.pallas_call(
        body, name=name, grid=(nb,),
        in_specs=[pl.BlockSpec((NQ, NA_WIDTH), lambda b: (b, 0)), ANY, ANY, ANY],
        out_specs=pl.BlockSpec((NQ, NA_WIDTH), lambda b: (b, 0)),
        out_shape=S_((S, NA_WIDTH), bf16),
        scratch_shapes=[pltpu.VMEM((T, NA_WIDTH), bf16), pltpu.VMEM((T, NA_WIDTH), bf16),
                        pltpu.VMEM((NA_HEADS, NQ, NW), f32), pltpu.SemaphoreType.DMA((3,))],
        compiler_params=_cp(VMEM_MID, ("arbitrary",)),
    )(q, k, v, bexp)


def _na_bwd(cfg, do, q, k, v, bexp, name):
    S, L, T, rows = cfg.S, cfg.L, cfg.T, cfg.rows
    NQ, NW = NA_QR * GRID_W, NA_WR * GRID_W
    NSLOT = 2 * NA_KH
    nb = rows // NA_QR
    bmax = (rows - NA_WR) // NA_QR
    steps = 2 * nb - bmax
    W = NA_WIDTH
    assert nb >= 3 and bmax >= 1 and rows - NA_QR * bmax <= NSLOT

    def out_group(g):
        return jnp.where(g >= nb, g - nb + bmax, jnp.clip(g - 1, 0, bmax - 1))

    def body(do_ref, q_ref, k_hbm, v_hbm, b_hbm, dq_ref, dk_ref, dv_ref, dkc_ref, dvc_ref, db_hbm,
             k_v, v_v, b_v, db_v, ak, av, akc, avc, sem):
        g = pl.program_id(0)

        @pl.when(g == 0)
        def _():
            cs = [pltpu.make_async_copy(k_hbm, k_v, sem.at[0]), pltpu.make_async_copy(v_hbm, v_v, sem.at[1])]
            for c_ in cs:
                c_.start()
            db_v[...] = jnp.zeros_like(db_v)
            ak[...] = jnp.zeros_like(ak)
            av[...] = jnp.zeros_like(av)
            akc[...] = jnp.zeros_like(akc)
            avc[...] = jnp.zeros_like(avc)
            for c_ in cs:
                c_.wait()

        for typ, at in ((0, 1), (1, nb - 1)):
            @pl.when(g == at)
            def _(typ=typ):
                cp = pltpu.make_async_copy(db_v, db_hbm.at[typ], sem.at[2])
                cp.start()
                cp.wait()
                db_v[...] = jnp.zeros_like(db_v)

        @pl.when(g < nb)
        def _():
            _na_load_bias(g, nb, b_hbm, b_v, sem.at[2])
            ws = _na_block(cfg, g)
            st = pl.multiple_of(ws * GRID_W, GRID_W)
            first = lax.broadcasted_iota(jnp.int32, (NQ, LANES), 1) < HEAD_DIM
            for hp in range(NA_HEADS // 2):
                ls = slice(hp * LANES, (hp + 1) * LANES)
                q2, do2 = q_ref[:, ls], do_ref[:, ls]
                kl, vl = k_v[pl.ds(st, NW), ls], v_v[pl.ds(st, NW), ls]
                kc, vc = k_v[S:T, ls], v_v[S:T, ls]
                dq2 = []
                dk2 = jnp.zeros((NW, LANES), f32)
                dv2 = jnp.zeros((NW, LANES), f32)
                dkc2 = jnp.zeros((L, LANES), f32)
                dvc2 = jnp.zeros((L, LANES), f32)
                for hh in range(2):
                    keep = first if hh == 0 else ~first
                    qm = jnp.where(keep, q2, jnp.zeros_like(q2))
                    dom = jnp.where(keep, do2, jnp.zeros_like(do2))
                    p_loc, p_ctx = _na_probs(qm, kl, kc, b_v[2 * hp + hh])
                    dp_loc = _nt(dom, vl)
                    dp_ctx = _nt(dom, vc)
                    delta = jnp.sum(p_loc * dp_loc, axis=-1, keepdims=True) + jnp.sum(p_ctx * dp_ctx, axis=-1, keepdims=True)
                    ds_loc = p_loc * (dp_loc - delta)
                    ds_ctx = p_ctx * (dp_ctx - delta)
                    db_v[2 * hp + hh] += ds_loc
                    dsl, dsc = ds_loc.astype(bf16), ds_ctx.astype(bf16)
                    dq2.append(_nn(dsl, kl) + _nn(dsc, kc))
                    dk2 = dk2 + _tn(dsl, qm)
                    dv2 = dv2 + _tn(p_loc.astype(bf16), dom)
                    dkc2 = dkc2 + _tn(dsc, qm)
                    dvc2 = dvc2 + _tn(p_ctx.astype(bf16), dom)
                dq_ref[:, ls] = jnp.where(first, dq2[0], dq2[1])
                akc[:, ls] += dkc2
                avc[:, ls] += dvc2
                for kk in range(NA_WR):
                    slot = (ws + kk) % NSLOT
                    ak[slot, :, ls] += dk2[kk * GRID_W:(kk + 1) * GRID_W, :]
                    av[slot, :, ls] += dv2[kk * GRID_W:(kk + 1) * GRID_W, :]

        @pl.when(((g >= 1) & (g <= bmax)) | (g >= nb))
        def _():
            base = NA_QR * (out_group(g) % (NSLOT // NA_QR))
            for t in range(NA_QR):
                dk_ref[t * GRID_W:(t + 1) * GRID_W, :] = ak[base + t]
                dv_ref[t * GRID_W:(t + 1) * GRID_W, :] = av[base + t]
                ak[base + t] = jnp.zeros((GRID_W, W), f32)
                av[base + t] = jnp.zeros((GRID_W, W), f32)

        @pl.when(g == nb - 1)
        def _():
            cp = pltpu.make_async_copy(db_v, db_hbm.at[2], sem.at[2])
            cp.start()
            cp.wait()

        @pl.when(g == steps - 1)
        def _():
            dkc_ref[...] = akc[...]
            dvc_ref[...] = avc[...]

    qmap = lambda g: (jnp.minimum(g, nb - 1), 0)
    kmap = lambda g: (out_group(g), 0)
    full = lambda g: (0, 0)
    return pl.pallas_call(
        body, name=name, grid=(steps,),
        in_specs=[pl.BlockSpec((NQ, W), qmap), pl.BlockSpec((NQ, W), qmap), ANY, ANY, ANY],
        out_specs=[pl.BlockSpec((NQ, W), qmap), pl.BlockSpec((NQ, W), kmap), pl.BlockSpec((NQ, W), kmap),
                   pl.BlockSpec((L, W), full), pl.BlockSpec((L, W), full), ANY],
        out_shape=[S_((S, W), f32), S_((S, W), f32), S_((S, W), f32), S_((L, W), f32), S_((L, W), f32),
                   S_((NA_TYPES, NA_HEADS, NQ, NW), f32)],
        scratch_shapes=[pltpu.VMEM((T, W), bf16), pltpu.VMEM((T, W), bf16),
                        pltpu.VMEM((NA_HEADS, NQ, NW), f32), pltpu.VMEM((NA_HEADS, NQ, NW), f32),
                        pltpu.VMEM((NSLOT, GRID_W, W), f32), pltpu.VMEM((NSLOT, GRID_W, W), f32),
                        pltpu.VMEM((L, W), f32), pltpu.VMEM((L, W), f32), pltpu.SemaphoreType.DMA((3,))],
        compiler_params=_cp(VMEM_BIG, ("arbitrary",)),
    )(do, q, k, v, bexp)


def _rpb_reduce(dbias, onehot, sel, name):
    x = dbias.reshape(NA_TYPES, NA_HEADS, NA_QR, GRID_W, NA_WR, GRID_W)
    x = jnp.transpose(x, (1, 0, 2, 4, 3, 5)).reshape(NA_HEADS, NA_TYPES * NA_QR * NA_WR, GRID_W * GRID_W)
    x = jnp.pad(x, ((0, 0), (0, NA_SEL_ROWS - x.shape[1]), (0, 0)))

    def body(x_ref, oh_ref, sel_ref, o_ref):
        y = jnp.dot(x_ref[...], oh_ref[...], preferred_element_type=f32, precision=lax.Precision.HIGHEST)
        o_ref[...] = jnp.dot(sel_ref[...], y, preferred_element_type=f32, precision=lax.Precision.HIGHEST)

    return pl.pallas_call(
        body, name=name, grid=(NA_HEADS,),
        in_specs=[pl.BlockSpec((None, NA_SEL_ROWS, GRID_W * GRID_W), lambda h: (h, 0, 0)),
                  pl.BlockSpec((GRID_W * GRID_W, LANES), lambda h: (0, 0)), pl.BlockSpec((16, NA_SEL_ROWS), lambda h: (0, 0))],
        out_specs=pl.BlockSpec((None, 16, LANES), lambda h: (h, 0, 0)),
        out_shape=S_((NA_HEADS, 16, LANES), f32),
        compiler_params=_cp(VMEM_MID, ("arbitrary",)),
    )(x, onehot, sel)


def _ctx_attn_fwd(cfg, q, k, v, name):
    L = cfg.L
    blk = cfg.S // L

    def body(q_ref, k_ref, v_ref, o_ref):
        qv, kv, vv = q_ref[...], k_ref[...], v_ref[...]
        outs = []
        for h in range(NA_HEADS):
            hs = slice(h * HEAD_DIM, (h + 1) * HEAD_DIM)
            s = _nt(qv[:, hs], kv[:, hs])
            e = jnp.exp(s - jnp.max(s, axis=-1, keepdims=True))
            p = e * (1.0 / jnp.sum(e, axis=-1, keepdims=True))
            outs.append(_nn(p.astype(bf16), vv[:, hs]))
        o_ref[...] = jnp.concatenate(outs, axis=-1).astype(bf16)

    spec = pl.BlockSpec((L, NA_WIDTH), lambda i: (blk, 0))
    return pl.pallas_call(
        body, name=name, grid=(1,), in_specs=[spec, spec, spec],
        out_specs=pl.BlockSpec((L, NA_WIDTH), lambda i: (0, 0)), out_shape=S_((L, NA_WIDTH), bf16),
        compiler_params=_cp(VMEM_MID, ("arbitrary",)),
    )(q, k, v)


def _ctx_attn_bwd(cfg, do, q, k, v, name):
    L = cfg.L
    blk = cfg.S // L

    def body(do_ref, q_ref, k_ref, v_ref, dq_ref, dk_ref, dv_ref):
        dov, qv, kv, vv = do_ref[...], q_ref[...], k_ref[...], v_ref[...]
        dqs, dks, dvs = [], [], []
        for h in range(NA_HEADS):
            hs = slice(h * HEAD_DIM, (h + 1) * HEAD_DIM)
            qh, kh, doh = qv[:, hs], kv[:, hs], dov[:, hs]
            s = _nt(qh, kh)
            e = jnp.exp(s - jnp.max(s, axis=-1, keepdims=True))
            p = e * (1.0 / jnp.sum(e, axis=-1, keepdims=True))
            dp = _nt(doh, vv[:, hs])
            ds = (p * (dp - jnp.sum(p * dp, axis=-1, keepdims=True))).astype(bf16)
            dqs.append(_nn(ds, kh))
            dks.append(_tn(ds, qh))
            dvs.append(_tn(p.astype(bf16), doh))
        dq_ref[...] = jnp.concatenate(dqs, axis=-1)
        dk_ref[...] = jnp.concatenate(dks, axis=-1)
        dv_ref[...] = jnp.concatenate(dvs, axis=-1)

    spec = pl.BlockSpec((L, NA_WIDTH), lambda i: (blk, 0))
    ospec = pl.BlockSpec((L, NA_WIDTH), lambda i: (0, 0))
    return pl.pallas_call(
        body, name=name, grid=(1,), in_specs=[spec, spec, spec, spec],
        out_specs=[ospec, ospec, ospec], out_shape=[S_((L, NA_WIDTH), f32)] * 3,
        compiler_params=_cp(VMEM_MID, ("arbitrary",)),
    )(do, q, k, v)


def _pool_centered(u, band, inv):
    hi = u.astype(bf16)
    lo = (u - hi.astype(f32)).astype(bf16)
    return (_nn(band, hi) + _nn(band, lo)) * inv - u


def _pool_fwd(cfg, u, band, inv, w_pool, pool_scale, with_ctx, name):
    TM = cfg.TM
    nt = cfg.ntiles(with_ctx)
    C = POOL_CH

    def body(u_ref, band_ref, inv_ref, w_ref, ps_ref, o_ref):
        outs = []
        for g in range(POOL_GROUPS):
            d = _pool_centered(u_ref[:, g * C:(g + 1) * C], band_ref[0, g], inv_ref[0, g])
            outs.append(_nn(d.astype(bf16), w_ref[g].astype(bf16)) * ps_ref[:, g * C:(g + 1) * C])
        o_ref[...] = jnp.concatenate(outs, axis=-1).astype(bf16)

    typ4 = lambda i: (jnp.minimum(i // cfg.nxt, 1), 0, 0, 0)
    return pl.pallas_call(
        body, name=name, grid=(nt,),
        in_specs=[pl.BlockSpec((TM, POOL_WIDTH), lambda i: (i, 0)), pl.BlockSpec((1, POOL_GROUPS, TM, TM), typ4),
                  pl.BlockSpec((1, POOL_GROUPS, TM, 1), typ4), pl.BlockSpec((POOL_GROUPS, C, C), lambda i: (0, 0, 0)),
                  pl.BlockSpec((1, POOL_WIDTH), lambda i: (0, 0))],
        out_specs=pl.BlockSpec((TM, POOL_WIDTH), lambda i: (i, 0)),
        out_shape=S_((nt * TM, POOL_WIDTH), bf16),
        compiler_params=_cp(VMEM_MID, ("arbitrary",)),
    )(u, band, inv, w_pool, pool_scale)


def _pool_bwd(cfg, dmix, u, band, inv, w_pool, pool_scale, with_ctx, name):
    TM = cfg.TM
    nt = cfg.ntiles(with_ctx)
    C = POOL_CH

    def body(dy_ref, u_ref, band_ref, inv_ref, w_ref, ps_ref, du_ref, dw_ref, dps_ref):
        @pl.when(pl.program_id(0) == 0)
        def _():
            dw_ref[...] = jnp.zeros_like(dw_ref)
            dps_ref[...] = jnp.zeros_like(dps_ref)

        dus, dpss = [], []
        for g in range(POOL_GROUPS):
            gs = slice(g * C, (g + 1) * C)
            band_g, inv_g = band_ref[0, g], inv_ref[0, g]
            db = _pool_centered(u_ref[:, gs], band_g, inv_g).astype(bf16)
            wb = w_ref[g].astype(bf16)
            dy = dy_ref[:, gs].astype(f32)
            dpss.append(_rsum(dy * _nn(db, wb)))
            dys = (dy * ps_ref[:, gs]).astype(bf16)
            dw_ref[g] += _tn(db, dys)
            dd = _nt(dys, wb)
            t = dd * inv_g
            hi = t.astype(bf16)
            lo = (t - hi.astype(f32)).astype(bf16)
            dus.append(_tn(band_g, hi) + _tn(band_g, lo) - dd)
        du_ref[...] = jnp.concatenate(dus, axis=-1)
        dps_ref[...] += jnp.concatenate(dpss, axis=-1)

    typ4 = lambda i: (jnp.minimum(i // cfg.nxt, 1), 0, 0, 0)
    return pl.pallas_call(
        body, name=name, grid=(nt,),
        in_specs=[pl.BlockSpec((TM, POOL_WIDTH), lambda i: (i, 1)), pl.BlockSpec((TM, POOL_WIDTH), lambda i: (i, 0)),
                  pl.BlockSpec((1, POOL_GROUPS, TM, TM), typ4), pl.BlockSpec((1, POOL_GROUPS, TM, 1), typ4),
                  pl.BlockSpec((POOL_GROUPS, C, C), lambda i: (0, 0, 0)), pl.BlockSpec((1, POOL_WIDTH), lambda i: (0, 0))],
        out_specs=[pl.BlockSpec((TM, POOL_WIDTH), lambda i: (i, 0)), pl.BlockSpec((POOL_GROUPS, C, C), lambda i: (0, 0, 0)),
                   pl.BlockSpec((1, POOL_WIDTH), lambda i: (0, 0))],
        out_shape=[S_((nt * TM, POOL_WIDTH), f32), S_((POOL_GROUPS, C, C), f32), S_((1, POOL_WIDTH), f32)],
        compiler_params=_cp(VMEM_MID, ("arbitrary",)),
    )(dmix, u, band, inv, w_pool, pool_scale)


def _tmpost_fwd(cfg, na_x, na_c, pool, w_out, xs, mods, gvec, l, name):
    TM, D = cfg.TM, cfg.D
    with_ctx = na_c is not None
    nt = cfg.ntiles(with_ctx)
    R = nt * TM

    def body(*refs):
        if with_ctx:
            nax_ref, nac_ref, pool_ref, w_ref, xs_ref, mods_ref, g_ref, out_ref, opre_ref, mix_ref = refs
            na = jnp.where(pl.program_id(0) < cfg.nxt, nax_ref[...], nac_ref[...])
        else:
            nax_ref, pool_ref, w_ref, xs_ref, mods_ref, g_ref, out_ref, opre_ref, mix_ref = refs
            na = nax_ref[...]
        pool_v = pool_ref[...]
        mix_ref[:, 0:NA_WIDTH] = na
        mix_ref[:, NA_WIDTH:] = pool_v
        o = _nn(na, w_ref[0:NA_WIDTH, :]) + _nn(pool_v, w_ref[NA_WIDTH:, :])
        opre_ref[...] = o
        ohat, _ = _rms_hat(o)
        out_ref[...] = xs_ref[...] + mods_ref[0][5:6] * (ohat * g_ref[3:4])

    rt = lambda c: pl.BlockSpec((TM, c), lambda i: (i, 0))
    na_specs = [pl.BlockSpec((TM, NA_WIDTH), lambda i: (jnp.minimum(i, cfg.nxt - 1), 0))]
    na_args = [na_x]
    if with_ctx:
        na_specs.append(pl.BlockSpec((TM, NA_WIDTH), lambda i: (0, 0)))
        na_args.append(na_c)
    return pl.pallas_call(
        body, name=name, grid=(nt,),
        in_specs=na_specs + [rt(POOL_WIDTH), pl.BlockSpec((None, MIX_WIDTH, D), lambda i: (l, 0, 0)), rt(D),
                             pl.BlockSpec((1, N_MOD, D), _typ(cfg)), pl.BlockSpec((6, D), lambda i: (0, 0))],
        out_specs=[rt(D), rt(D), rt(MIX_WIDTH)],
        out_shape=[S_((R, D), f32), S_((R, D), f32), S_((R, MIX_WIDTH), bf16)],
        compiler_params=_cp(VMEM_MID, ("arbitrary",)),
    )(*na_args, pool, w_out, xs, mods, gvec)


def _tmpost_bwd(cfg, dout, opre, w_out, mods, gvec, l, with_ctx, name):
    TM, D = cfg.TM, cfg.D
    nt = cfg.ntiles(with_ctx)
    R = nt * TM
    ntyp = 2 if with_ctx else 1

    def body(do_ref, opre_ref, w_ref, mods_ref, g_ref, dop_ref, dmix_ref, dm_ref, dg_ref):
        i = pl.program_id(0)

        @pl.when(i == 0)
        def _():
            dg_ref[...] = jnp.zeros_like(dg_ref)

        @pl.when((i == 0) | (i == cfg.nxt))
        def _():
            dm_ref[...] = jnp.zeros_like(dm_ref)

        do = do_ref[...]
        g3 = g_ref[3:4]
        ohat, rinv = _rms_hat(opre_ref[...])
        dm_ref[0] += _rsum(do * (ohat * g3))
        dr = mods_ref[0][5:6] * do
        dg_ref[...] += _rsum(dr * ohat)
        dob = _rms_bwd(dr * g3, ohat, rinv).astype(bf16)
        dop_ref[...] = dob
        dmix_ref[...] = _nt(dob, w_ref[...]).astype(bf16)

    rt = lambda c: pl.BlockSpec((TM, c), lambda i: (i, 0))
    return pl.pallas_call(
        body, name=name, grid=(nt,),
        in_specs=[rt(D), rt(D), pl.BlockSpec((None, MIX_WIDTH, D), lambda i: (l, 0, 0)),
                  pl.BlockSpec((1, N_MOD, D), _typ(cfg)), pl.BlockSpec((6, D), lambda i: (0, 0))],
        out_specs=[rt(D), rt(MIX_WIDTH), pl.BlockSpec((1, 1, D), _typ(cfg)), pl.BlockSpec((1, D), lambda i: (0, 0))],
        out_shape=[S_((R, D), bf16), S_((R, MIX_WIDTH), bf16), S_((ntyp, 1, D), f32), S_((1, D), f32)],
        compiler_params=_cp(VMEM_MID, ("arbitrary",)),
    )(dout, opre, w_out, mods, gvec)


def _loss_head(cfg, y, target, name):
    TM, D = cfg.TM, cfg.D

    def body(y_ref, t_ref, dy_ref, loss_ref):
        @pl.when(pl.program_id(0) == 0)
        def _():
            loss_ref[...] = jnp.zeros_like(loss_ref)
        e = y_ref[...] - t_ref[...]
        dy_ref[...] = e * (1.0 / D)
        loss_ref[...] += jnp.sum(jnp.mean(e * e, axis=-1, keepdims=True), axis=0, keepdims=True) * 0.5

    rt = pl.BlockSpec((TM, D), lambda i: (i, 0))
    return pl.pallas_call(
        body, name=name, grid=(cfg.nxt,), in_specs=[rt, rt],
        out_specs=[rt, pl.BlockSpec((8, LANES), lambda i: (0, 0))],
        out_shape=[S_((cfg.S, D), f32), S_((8, LANES), f32)],
        compiler_params=_cp(VMEM_MID, ("arbitrary",)),
    )(y, target)


def _modvec_fwd(cvecs, w_mod, b_shard, name):
    nl, D, n = w_mod.shape
    tn = n // 3 if (n % 3 == 0 and (n // 3) % LANES == 0) else n

    def body(c_ref, w_ref, b_ref, o_ref, s_ref):
        cv = c_ref[...]
        sv = cv * _sigmoid(cv)
        s_ref[...] = sv
        o_ref[...] = _nn(sv.astype(bf16), w_ref[...].astype(bf16)) + b_ref[...]

    return pl.pallas_call(
        body, name=name, grid=(nl, n // tn),
        in_specs=[pl.BlockSpec((16, D), lambda l, j: (0, 0)), pl.BlockSpec((None, D, tn), lambda l, j: (l, 0, j)),
                  pl.BlockSpec((None, 1, tn), lambda l, j: (l, 0, j))],
        out_specs=[pl.BlockSpec((None, 16, tn), lambda l, j: (l, 0, j)), pl.BlockSpec((16, D), lambda l, j: (0, 0))],
        out_shape=[S_((nl, 16, n), f32), S_((16, D), f32)],
        compiler_params=_cp(VMEM_MID, ("arbitrary", "arbitrary")),
    )(cvecs, w_mod, b_shard)


def _modvec_bwd(s_t, dm, w_mod, name):
    nl, D, n = w_mod.shape
    tn = n // 3 if (n % 3 == 0 and (n // 3) % LANES == 0) else n

    def body(s_ref, dm_ref, w_ref, gw_ref, gc_ref):
        @pl.when(pl.program_id(1) == 0)
        def _():
            gc_ref[...] = jnp.zeros_like(gc_ref)
        dmv = dm_ref[...]
        gw_ref[...] = jnp.dot(s_ref[...], dmv, preferred_element_type=f32, precision=lax.Precision.HIGHEST)
        gc_ref[...] += _nt(dmv[8:16].astype(bf16), w_ref[...].astype(bf16))

    return pl.pallas_call(
        body, name=name, grid=(nl, n // tn),
        in_specs=[pl.BlockSpec((D, 16), lambda l, j: (0, 0)), pl.BlockSpec((None, 16, tn), lambda l, j: (l, 0, j)),
                  pl.BlockSpec((None, D, tn), lambda l, j: (l, 0, j))],
        out_specs=[pl.BlockSpec((None, D, tn), lambda l, j: (l, 0, j)), pl.BlockSpec((None, 8, D), lambda l, j: (l, 0, 0))],
        out_shape=[S_((nl, D, n), f32), S_((nl, 8, D), f32)],
        compiler_params=_cp(VMEM_MID, ("arbitrary", "arbitrary")),
    )(s_t, dm, w_mod)


def _as2d(a):
    n = a.size
    if a.ndim >= 2 and a.shape[-1] % LANES == 0:
        return a.reshape(-1, a.shape[-1])
    if n % LANES == 0:
        return a.reshape(-1, LANES)
    return a.reshape(-1, a.shape[-1]) if a.ndim >= 2 else a.reshape(1, n)


def _row_tile(r, c, budget_elems=512 * 1024):
    if r * c <= budget_elems or r % 8 != 0:
        return r
    t = r
    while t * c > budget_elems and t % 16 == 0:
        t //= 2
    return t


def _div_tile(r, c, budget_elems, mult=16):
    best = None
    for t in range(mult, r + 1, mult):
        if r % t == 0 and t * c <= budget_elems:
            best = t
    return best if best is not None else r


def _cast_into_place(shard, axis, kidx, name):
    lead = shard.shape[:-2]
    r, c = shard.shape[-2:]
    nb_ = int(np.prod(lead))
    cols = axis == shard.ndim - 1
    assert cols or axis == shard.ndim - 2
    tr = _div_tile(r, c, 768 * 1024)
    nr = r // tr
    out_map = (lambda b, i, k: (b, i, k[0])) if cols else (lambda b, i, k: (b, k[0] * nr + i, 0))
    full2 = (r, c * N_CHIPS) if cols else (r * N_CHIPS, c)

    def body(k_ref, a_ref, o_ref):
        o_ref[...] = a_ref[...].astype(bf16)

    out = pl.pallas_call(
        body, name=name,
        grid_spec=pltpu.PrefetchScalarGridSpec(
            num_scalar_prefetch=1, grid=(nb_, nr),
            in_specs=[pl.BlockSpec((None, tr, c), lambda b, i, k: (b, i, 0))],
            out_specs=pl.BlockSpec((None, tr, c), out_map)),
        out_shape=S_((nb_,) + full2, bf16), compiler_params=_cp(VMEM_MID, ("arbitrary", "arbitrary")),
    )(kidx, shard.reshape((nb_, r, c)))
    return out.reshape(lead + full2)


def _adamw(w, g, m, v, name):
    shape = w.shape
    w2, g2, m2, v2 = _as2d(w), _as2d(g), _as2d(m), _as2d(v)
    r, c = w2.shape
    tr = _row_tile(r, c, 256 * 1024)
    c1 = 1.0 - ADAM_B1 ** ADAM_STEP
    c2 = 1.0 - ADAM_B2 ** ADAM_STEP

    def body(w_ref, g_ref, m_ref, v_ref, d_ref, mo_ref, vo_ref):
        gv = g_ref[...]
        mn = ADAM_B1 * m_ref[...] + (1.0 - ADAM_B1) * gv
        vn = ADAM_B2 * v_ref[...] + (1.0 - ADAM_B2) * (gv * gv)
        mo_ref[...] = mn
        vo_ref[...] = vn
        d_ref[...] = -ADAM_LR * ((mn / c1) / (jnp.sqrt(vn / c2) + ADAM_EPS) + ADAM_WD * w_ref[...])

    spec = pl.BlockSpec((tr, c), lambda i: (i, 0))
    outs = pl.pallas_call(body, name=name, grid=(r // tr,), in_specs=[spec] * 4, out_specs=[spec] * 3,
                          out_shape=[S_((r, c), f32)] * 3, compiler_params=_cp(VMEM_MID, ("arbitrary",)))(w2, g2, m2, v2)
    return tuple(o.reshape(shape) for o in outs)


def _sum_devices(gathered, name):
    _, r, c = gathered.shape

    def body(a_ref, o_ref):
        acc = a_ref[0]
        for j in range(1, N_DEV):
            acc = acc + a_ref[j]
        o_ref[...] = acc

    tr = _row_tile(r, c, 64 * 1024)
    return pl.pallas_call(
        body, name=name, grid=(r // tr,),
        in_specs=[pl.BlockSpec((N_DEV, tr, c), lambda i: (0, i, 0))], out_specs=pl.BlockSpec((tr, c), lambda i: (i, 0)),
        out_shape=S_((r, c), f32), compiler_params=_cp(VMEM_MID, ("arbitrary",)))(gathered)


def _add_layer(full, sib, cidx, name):
    _, r, c = full.shape
    tr = _row_tile(r, c, 256 * 1024)

    def body(c_ref, a_ref, b_ref, o_ref):
        o_ref[...] = (a_ref[...] + b_ref[...]).astype(bf16)

    return pl.pallas_call(
        body, name=name,
        grid_spec=pltpu.PrefetchScalarGridSpec(
            num_scalar_prefetch=1, grid=(r // tr,),
            in_specs=[pl.BlockSpec((1, tr, c), lambda i, cr: (cr[0], i, 0)), pl.BlockSpec((1, tr, c), lambda i, cr: (0, i, 0))],
            out_specs=pl.BlockSpec((1, tr, c), lambda i, cr: (0, i, 0))),
        out_shape=S_((1, r, c), bf16), compiler_params=_cp(VMEM_MID, ("arbitrary",)))(cidx, full, sib)


def _sum_chips(parts, own, axis, ck, name):
    _, nb_, rs, cs = parts.shape
    tr = _div_tile(rs, cs, 400 * 1024)
    nr = rs // tr
    own_map = (lambda b, i, s: (0, b, s[1] * nr + i, 0)) if axis == 2 else (lambda b, i, s: (0, b, i, s[1]))
    blk = (None, None, tr, cs)

    def part_spec(j):
        return pl.BlockSpec(blk, lambda b, i, s: ((s[1] + j) % N_CHIPS, b, i, 0))

    def body(s_ref, own_ref, p1_ref, p2_ref, p3_ref, o_ref):
        o_ref[...] = ((own_ref[...].astype(f32) + p1_ref[...].astype(f32)) + p2_ref[...].astype(f32)) + p3_ref[...].astype(f32)

    return pl.pallas_call(
        body, name=name,
        grid_spec=pltpu.PrefetchScalarGridSpec(
            num_scalar_prefetch=1, grid=(nb_, nr),
            in_specs=[pl.BlockSpec(blk, own_map), part_spec(1), part_spec(2), part_spec(3)],
            out_specs=pl.BlockSpec(blk, lambda b, i, s: (s[0], b, i, 0))),
        out_shape=S_((2, nb_, rs, cs), f32), compiler_params=_cp(VMEM_MID, ("arbitrary", "arbitrary")),
    )(ck, own, parts, parts, parts)


def _mesh_pos():
    return lax.axis_index("x"), lax.axis_index("y"), lax.axis_index("c")


def _all_gather_small(block, name):
    m_per, n = block.shape

    def body(x_ref, out_ref, send_sems, recv_sems, local_sem):
        x, y, c = _mesh_pos()
        me, sibling = (x, y, c), (x, y, 1 - c)
        chips = [(1 - x, y), (x, 1 - y), (1 - x, 1 - y)]

        def rows(px, py, pc):
            return out_ref.at[pl.ds((4 * px + 2 * py + pc) * m_per, m_per), :]

        def copy(k, blk, to, src=None):
            return pltpu.make_async_remote_copy(
                src_ref=rows(*blk) if src is None else src, dst_ref=rows(*blk),
                send_sem=send_sems.at[k], recv_sem=recv_sems.at[k], device_id=to, device_id_type=MESH)

        mine = pltpu.make_async_copy(x_ref, rows(*me), local_sem)
        mine.start()
        first = [copy(0, me, sibling, src=x_ref)]
        first += [copy(1 + j, me, (*chip, c), src=x_ref) for j, chip in enumerate(chips)]
        for cp in first:
            cp.start()
        passed = [copy(4 + j, (*chip, c), sibling) for j, chip in enumerate(chips)]
        for j, chip in enumerate(chips):
            copy(1 + j, (*chip, c), me).wait_recv()
            passed[j].start()
        copy(0, sibling, me).wait_recv()
        for j, chip in enumerate(chips):
            copy(4 + j, (*chip, 1 - c), me).wait_recv()
        for cp in first + passed:
            cp.wait_send()
        mine.wait()

    return pl.pallas_call(
        body, name=name, out_shape=S_((N_DEV * m_per, n), block.dtype),
        in_specs=[pl.BlockSpec(memory_space=pltpu.VMEM)], out_specs=pl.BlockSpec(memory_space=pltpu.VMEM),
        scratch_shapes=[pltpu.SemaphoreType.DMA((7,)), pltpu.SemaphoreType.DMA((7,)), pltpu.SemaphoreType.DMA],
        compiler_params=_cp(VMEM_MID),
    )(block)


def _shard_view(ref, axis, kk, size, lsel):
    idx = [lsel] + [slice(None)] * (len(ref.shape) - 1)
    idx[axis] = pl.ds(pl.multiple_of(kk * size, LANES if axis == len(ref.shape) - 1 else 8), size)
    return ref.at[tuple(idx)]


def _gather_weights(fulls_in, axes, name):
    nt = len(fulls_in)
    sizes = [f.shape[a] // N_CHIPS for f, a in zip(fulls_in, axes)]

    def body(*refs):
        fulls = refs[nt:2 * nt]
        send_sems, recv_sems = refs[2 * nt:]
        x, y, c = _mesh_pos()
        sibling = (x, y, 1 - c)
        chips = [(1 - x, y), (x, 1 - y), (1 - x, 1 - y)]
        k_me = 2 * x + y
        mine_l, other_l = pl.ds(c, 1), pl.ds(1 - c, 1)
        started = []
        for t in range(nt):
            place = functools.partial(_shard_view, fulls[t], axes[t], size=sizes[t])
            own = place(kk=k_me, lsel=mine_l)
            for j, chip in enumerate(chips):
                cp = pltpu.make_async_remote_copy(
                    src_ref=own, dst_ref=own,
                    send_sem=send_sems.at[6 * t + j], recv_sem=recv_sems.at[6 * t + j],
                    device_id=(*chip, c), device_id_type=MESH)
                cp.start()
                started.append(cp)
        for t in range(nt):
            place = functools.partial(_shard_view, fulls[t], axes[t], size=sizes[t])
            for j, chip in enumerate(chips):
                landed = place(kk=2 * chip[0] + chip[1], lsel=mine_l)
                fwd = pltpu.make_async_remote_copy(
                    src_ref=landed, dst_ref=landed, send_sem=send_sems.at[6 * t + 3 + j], recv_sem=recv_sems.at[6 * t + 3 + j],
                    device_id=sibling, device_id_type=MESH)
                pltpu.make_async_remote_copy(
                    src_ref=landed, dst_ref=landed, send_sem=send_sems.at[6 * t + j], recv_sem=recv_sems.at[6 * t + j],
                    device_id=(*chip, c), device_id_type=MESH).wait_recv()
                fwd.start()
                started.append(fwd)
        for t in range(nt):
            place = functools.partial(_shard_view, fulls[t], axes[t], size=sizes[t])
            for j, chip in enumerate(chips):
                theirs = place(kk=2 * chip[0] + chip[1], lsel=other_l)
                pltpu.make_async_remote_copy(
                    src_ref=theirs, dst_ref=theirs, send_sem=send_sems.at[6 * t + 3 + j], recv_sem=recv_sems.at[6 * t + 3 + j],
                    device_id=sibling, device_id_type=MESH).wait_recv()
        for cp in started:
            cp.wait_send()

    return pl.pallas_call(
        body, name=name, in_specs=[ANY] * nt, out_specs=[ANY] * nt,
        out_shape=[S_(f.shape, bf16) for f in fulls_in],
        input_output_aliases={t: t for t in range(nt)},
        scratch_shapes=[pltpu.SemaphoreType.DMA((6 * nt,)), pltpu.SemaphoreType.DMA((6 * nt,))],
        compiler_params=_cp(VMEM_MID),
    )(*fulls_in)


def _sibling_layer_exchange(grads, name):
    nt = len(grads)

    def body(*refs):
        srcs, outs = refs[:nt], refs[nt:2 * nt]
        send_sems, recv_sems = refs[2 * nt:]
        x, y, c = _mesh_pos()
        cps = []
        for t in range(nt):
            cp = pltpu.make_async_remote_copy(
                src_ref=srcs[t].at[pl.ds(1 - c, 1)], dst_ref=outs[t], send_sem=send_sems.at[t], recv_sem=recv_sems.at[t],
                device_id=(x, y, 1 - c), device_id_type=MESH)
            cp.start()
            cps.append(cp)
        for cp in cps:
            cp.wait()

    return pl.pallas_call(
        body, name=name, in_specs=[ANY] * nt, out_specs=[ANY] * nt,
        out_shape=[S_((1,) + g.shape[1:], f32) for g in grads],
        scratch_shapes=[pltpu.SemaphoreType.DMA((nt,)), pltpu.SemaphoreType.DMA((nt,))],
        compiler_params=_cp(VMEM_MID),
    )(*grads)


def _chip_scatter(sums, axes, name):
    nt = len(sums)
    sizes = [s.shape[a] // N_CHIPS for s, a in zip(sums, axes)]
    shard_shapes = [tuple(sz if i == a else d for i, d in enumerate(s.shape)) for s, a, sz in zip(sums, axes, sizes)]

    def body(*refs):
        srcs, outs = refs[:nt], refs[nt:2 * nt]
        send_sems, recv_sems = refs[2 * nt:]
        x, y, c = _mesh_pos()
        chips = [(1 - x, y), (x, 1 - y), (1 - x, 1 - y)]
        k_me = 2 * x + y
        one = pl.ds(0, 1)
        cps = []
        for t in range(nt):
            view = functools.partial(_shard_view, srcs[t], axes[t], size=sizes[t], lsel=one)
            for j, chip in enumerate(chips):
                cp = pltpu.make_async_remote_copy(
                    src_ref=view(kk=2 * chip[0] + chip[1]), dst_ref=outs[t].at[k_me],
                    send_sem=send_sems.at[3 * t + j], recv_sem=recv_sems.at[3 * t + j],
                    device_id=(*chip, c), device_id_type=MESH)
                cp.start()
                cps.append(cp)
        for t in range(nt):
            view = functools.partial(_shard_view, srcs[t], axes[t], size=sizes[t], lsel=one)
            for j, chip in enumerate(chips):
                kj = 2 * chip[0] + chip[1]
                pltpu.make_async_remote_copy(
                    src_ref=view(kk=kj), dst_ref=outs[t].at[kj], send_sem=send_sems.at[3 * t + j], recv_sem=recv_sems.at[3 * t + j],
                    device_id=(*chip, c), device_id_type=MESH).wait_recv()
        for cp in cps:
            cp.wait_send()

    return pl.pallas_call(
        body, name=name, in_specs=[ANY] * nt, out_specs=[ANY] * nt,
        out_shape=[S_((N_CHIPS,) + ss, s.dtype) for ss, s in zip(shard_shapes, sums)],
        scratch_shapes=[pltpu.SemaphoreType.DMA((3 * nt,)), pltpu.SemaphoreType.DMA((3 * nt,))],
        compiler_params=_cp(VMEM_MID),
    )(*sums)


def _sibling_layer_merge(finals, name):
    nt = len(finals)

    def body(*refs):
        bufs = refs[nt:2 * nt]
        send_sems, recv_sems = refs[2 * nt:]
        x, y, c = _mesh_pos()
        cps = []
        for t in range(nt):
            mine = bufs[t].at[pl.ds(c, 1)]
            cp = pltpu.make_async_remote_copy(
                src_ref=mine, dst_ref=mine, send_sem=send_sems.at[t], recv_sem=recv_sems.at[t],
                device_id=(x, y, 1 - c), device_id_type=MESH)
            cp.start()
            cps.append(cp)
        for t in range(nt):
            theirs = bufs[t].at[pl.ds(1 - c, 1)]
            pltpu.make_async_remote_copy(
                src_ref=theirs, dst_ref=theirs, send_sem=send_sems.at[t], recv_sem=recv_sems.at[t],
                device_id=(x, y, 1 - c), device_id_type=MESH).wait_recv()
        for cp in cps:
            cp.wait_send()

    return pl.pallas_call(
        body, name=name, in_specs=[ANY] * nt, out_specs=[ANY] * nt,
        out_shape=[S_(f.shape, f32) for f in finals],
        input_output_aliases={t: t for t in range(nt)},
        scratch_shapes=[pltpu.SemaphoreType.DMA((nt,)), pltpu.SemaphoreType.DMA((nt,))],
        compiler_params=_cp(VMEM_MID),
    )(*finals)


def _reduce_scatter_weights(grads, axes):
    mx, my, mc = _mesh_pos()
    cidx = mc.astype(jnp.int32).reshape(1)
    ck = jnp.stack([mc, 2 * mx + my]).astype(jnp.int32)
    sib = _sibling_layer_exchange(grads, "rs_sibling_exchange")
    as4 = lambda a: a.reshape((a.shape[0], -1) + a.shape[-2:])
    axes4 = [2 if a == g.ndim - 2 else 3 for g, a in zip(grads, axes)]
    sums = []
    for t, (g, s) in enumerate(zip(grads, sib)):
        g3 = g.reshape(2, -1, g.shape[-1])
        sums.append(as4(_add_layer(g3, s.reshape(1, -1, g.shape[-1]), cidx, f"rs_add_sibling_{t}").reshape((1,) + g.shape[1:])))
    parts = _chip_scatter(sums, axes4, "rs_chip_scatter")
    finals = [_sum_chips(p.reshape((N_CHIPS,) + p.shape[2:]), s, a, ck, f"rs_sum_chips_{t}")
              for t, (p, s, a) in enumerate(zip(parts, sums, axes4))]
    merged = _sibling_layer_merge(finals, "rs_sibling_merge")
    return [m.reshape(g.shape[:-2] + m.shape[-2:]) for m, g in zip(merged, grads)]


def _pack_rows(arrays):
    flat = jnp.concatenate([a.reshape(-1) for a in arrays])
    pad = (-flat.size) % (8 * LANES)
    return jnp.pad(flat, (0, pad)).reshape(-1, LANES)


def _unpack_rows(packed, shapes):
    flat = packed.reshape(-1)
    out, off = [], 0
    for s in shapes:
        n = int(np.prod(s))
        out.append(flat[off:off + n].reshape(s))
        off += n
    return out


def _local_step(cfg, xs0, target, mods, norm_g, gu, dn, wi, wo, na_rpb, w_pool, pool_scale):
    S, L, T, D, F = cfg.S, cfg.L, cfg.T, cfg.D, cfg.F
    depth = norm_g.shape[0]
    cos, sin = _rope_tables(S, L)
    band, inv = _pool_tables(cfg.TM, L)
    onehot, sel = _rpb_reduce_tables()

    saved = []
    xs = xs0
    for l in range(depth):
        last = l == depth - 1
        wc = not last
        gvec = norm_g[l]
        ps = pool_scale[l].reshape(1, POOL_WIDTH)
        bexp = _expand_rpb(na_rpb[l])
        xs1, hb1, z1, y1 = _ffn_fwd(cfg, xs, mods[l], gvec, gu, dn, l, 0, 0, 0, True, f"ffn_fwd_{l}_0")
        hb2, q, k, v, u = _tmpre_fwd(cfg, xs1, mods[l], gvec, wi, cos, sin, l, f"tmpre_fwd_{l}")
        na_x = _na_fwd(cfg, q, k, v, bexp, f"na_fwd_{l}")
        na_c = _ctx_attn_fwd(cfg, q, k, v, f"ctx_attn_fwd_{l}") if wc else None
        pool = _pool_fwd(cfg, u, band, inv, w_pool[l], ps, wc, f"pool_fwd_{l}")
        xs2, opre, mix = _tmpost_fwd(cfg, na_x, na_c, pool, wo, xs1, mods[l], gvec, l, f"tmpost_fwd_{l}")
        xs3, hb3, z3, y3 = _ffn_fwd(cfg, xs2, mods[l], gvec, gu, dn, l, 1, 6, 4, wc, f"ffn_fwd_{l}_1")
        saved.append(dict(xs=xs, xs1=xs1, xs2=xs2, hb1=hb1, z1=z1, y1=y1, hb2=hb2, q=q, k=k, v=v, u=u, mix=mix,
                          opre=opre, hb3=hb3, z3=z3, y3=y3, bexp=bexp, ps=ps, gvec=gvec))
        xs = xs3

    dxs, loss_blk = _loss_head(cfg, xs, target, "loss_head")

    g_gu = lax.empty((depth, 2, D, 2 * F), f32)
    g_dn = lax.empty((depth, 2, F, D), f32)
    g_wi = lax.empty((depth, D, IN_WIDTH), f32)
    g_wo = lax.empty((depth, MIX_WIDTH, D), f32)
    small = [None] * depth
    for l in reversed(range(depth)):
        last = l == depth - 1
        wc = not last
        sv = saved[l]
        gvec = sv["gvec"]
        rows_b = cfg.T if wc else cfg.S
        dxs2, dz, dyb, ab, dm678, dg45 = _ffn_bwd(cfg, dxs, sv["xs2"], sv["z3"], sv["y3"], mods[l], gvec, gu, dn,
                                                   l, 1, 6, 4, wc, f"ffn_bwd_{l}_1")
        g_gu = _wgrad(sv["hb3"], dz, rows_b, g_gu, (l, 1), f"wgrad_gu_{l}_1")
        g_dn = _wgrad(ab, dyb, rows_b, g_dn, (l, 1), f"wgrad_dn_{l}_1")
        dop, dmix, dm5, dg3 = _tmpost_bwd(cfg, dxs2, sv["opre"], wo, mods[l], gvec, l, wc, f"tmpost_bwd_{l}")
        g_wo = _wgrad(sv["mix"], dop, rows_b, g_wo, (l,), f"wgrad_wo_{l}")
        du, dwp, dps = _pool_bwd(cfg, dmix, sv["u"], band, inv, w_pool[l], sv["ps"], wc, f"pool_bwd_{l}")
        dq, dk, dv, dkc, dvc, dbexp = _na_bwd(cfg, dmix, sv["q"], sv["k"], sv["v"], sv["bexp"], f"na_bwd_{l}")
        drpb = _rpb_reduce(dbexp, onehot, sel, f"rpb_reduce_{l}")
        if wc:
            dqc, dkc2, dvc2 = _ctx_attn_bwd(cfg, dmix, sv["q"], sv["k"], sv["v"], f"ctx_attn_bwd_{l}")
            ctx_terms = ([dqc], [dkc, dkc2], [dvc, dvc2])
        else:
            ctx_terms = ([], [dkc], [dvc])
        dproj = _rope_bwd_assemble(cfg, (dq, dk, dv, du), ctx_terms, wc, cos, sin, f"rope_bwd_{l}")
        g_wi = _wgrad(sv["hb2"], dproj, cfg.T, g_wi, (l,), f"wgrad_wi_{l}")
        dxs1, dm34, dg2 = _tmpre_bwd(cfg, dproj, wi, sv["xs1"], mods[l], gvec, dxs2, wc, l, f"tmpre_bwd_{l}")
        dxs, dz, dyb, ab, dm012, dg01 = _ffn_bwd(cfg, dxs1, sv["xs"], sv["z1"], sv["y1"], mods[l], gvec, gu, dn,
                                                  l, 0, 0, 0, True, f"ffn_bwd_{l}_0")
        g_gu = _wgrad(sv["hb1"], dz, cfg.T, g_gu, (l, 0), f"wgrad_gu_{l}_0")
        g_dn = _wgrad(ab, dyb, cfg.T, g_dn, (l, 0), f"wgrad_dn_{l}_0")
        if not wc:
            zero = lambda a: jnp.concatenate([a, jnp.zeros_like(a)], axis=0)
            dm5, dm678 = zero(dm5), zero(dm678)
        dmods = jnp.concatenate([dm012, dm34, dm5, dm678], axis=1)
        dgs = jnp.concatenate([dg01, dg2, dg3, dg45], axis=0)
        small[l] = dict(dmods=dmods, dg=dgs, drpb=drpb, dwp=dwp, dps=dps)
    return loss_blk, dxs, (g_gu, g_dn, g_wi, g_wo), small


def kernel(x, c, ctx, c_ctx, w_mod, b_mod, norm_g, w_ffn_gate_up, w_ffn_down, w_in, w_out, na_rpb, w_pool, pool_scale, loss_target, m_c_ctx, m_w_mod, m_b_mod, m_norm_g, m_w_ffn_gate_up, m_w_ffn_down, m_w_in, m_w_out, m_na_rpb, m_w_pool, m_pool_scale, v_c_ctx, v_w_mod, v_b_mod, v_norm_g, v_w_ffn_gate_up, v_w_ffn_down, v_w_in, v_w_out, v_na_rpb, v_w_pool, v_pool_scale):
    S, D = x.shape[1], x.shape[2]
    L = ctx.shape[1]
    depth = w_mod.shape[0]
    F = w_ffn_down.shape[2] * N_CHIPS
    nmod = w_mod.shape[2]
    gsh = norm_g.shape[2]
    cfg = _Cfg(S, L, D, F)
    mx, my, mc = _mesh_pos()
    chip = 2 * mx + my
    dev = 4 * mx + 2 * my + mc

    axes = [3, 2, 2, 1]
    kidx = chip.astype(jnp.int32).reshape(1)
    placed = [_cast_into_place(w_, a_, kidx, f"cast_{t_}")
              for w_, a_, t_ in zip((w_ffn_gate_up, w_ffn_down, w_in, w_out), axes, ("gu", "dn", "wi", "wo"))]
    gu, dn, wi, wo = _gather_weights(placed, axes, "gather_weights")

    c_all = _all_gather_small(jnp.pad(c, ((0, 7), (0, 0))), "gather_c").reshape(N_DEV, 8, D)[:, 0]
    cvecs = jnp.concatenate([c_all, c_ctx[None], jnp.zeros((7, D), f32)], axis=0)
    b_shard = lax.dynamic_slice_in_dim(b_mod, chip * nmod, nmod, axis=1).reshape(depth, 1, nmod)
    m_part, silu_c = _modvec_fwd(cvecs, w_mod, b_shard, "modvec_fwd")
    m_all = _all_gather_small(m_part.reshape(depth * 16, nmod), "gather_mod").reshape(N_DEV, depth, 16, nmod)
    m_full = jnp.concatenate([m_all[2 * j] for j in range(N_CHIPS)], axis=-1)
    m_mine = lax.dynamic_index_in_dim(m_full, dev, axis=1, keepdims=False)
    mods = jnp.stack([m_mine, m_full[:, 8]], axis=1).reshape(depth, 2, N_MOD, D)

    norm_g_full = _all_gather_small(_pack_rows([norm_g]), "gather_norm_g")
    rows_g = norm_g_full.shape[0] // N_DEV
    ng = norm_g_full.reshape(N_DEV, rows_g * LANES)[:, :norm_g.size].reshape(N_DEV, depth, 6, gsh)
    norm_g_all = jnp.concatenate([ng[2 * j] for j in range(N_CHIPS)], axis=-1)
    xs0 = jnp.concatenate([x[0], ctx[0]], axis=0)
    loss_blk, dxs0, wgrads, small = _local_step(cfg, xs0, loss_target[0], mods, norm_g_all, gu, dn, wi, wo,
                                                na_rpb, w_pool, pool_scale)
    loss = lax.psum(loss_blk[0, 0], ("x", "y", "c"))
    grad_x = dxs0[:S][None]

    g_gu, g_dn, g_wi, g_wo = _reduce_scatter_weights(list(wgrads), axes)
    names = ("dmods", "dg", "drpb", "dwp", "dps")
    parts = [jnp.stack([small[l][n] for l in range(depth)]) for n in names]
    shapes = [p.shape for p in parts]
    packed = _pack_rows(parts)
    gathered = _all_gather_small(packed, "gather_small").reshape(N_DEV, packed.shape[0], LANES)
    total = _unpack_rows(_sum_devices(gathered, "sum_small"), shapes)
    dmods_sum, dg_sum, drpb_sum, dwp_sum, dps_sum = total
    dmods_each = jnp.stack([_unpack_rows(gathered[j], shapes[:1])[0] for j in range(N_DEV)])
    dm_rows = jnp.concatenate([jnp.transpose(dmods_each[:, :, 0], (1, 0, 2, 3)).reshape(depth, N_DEV, N_MOD * D),
                               dmods_sum[:, 1].reshape(depth, 1, N_MOD * D),
                               jnp.zeros((depth, 7, N_MOD * D), f32)], axis=1)
    dm_shard = lax.dynamic_slice_in_dim(dm_rows, chip * nmod, nmod, axis=2)
    grad_w_mod, gc_part = _modvec_bwd(silu_c.T, dm_shard, w_mod, "modvec_bwd")
    gc_all = _all_gather_small(gc_part.reshape(depth * 8, D), "gather_gc").reshape(N_DEV, depth, 8, D)
    grad_b_mod, grad_c_ctx = _small_finish(dm_rows, gc_all, c_ctx)
    grad_norm_g = lax.dynamic_slice_in_dim(dg_sum, chip * gsh, gsh, axis=2)
    grad_na_rpb = drpb_sum[:, :, :2 * NA_KH - 1, :2 * NA_KW - 1]
    grad_w_pool = dwp_sum
    grad_pool_scale = dps_sum.reshape(depth, POOL_WIDTH)

    grads = [grad_c_ctx, grad_w_mod, grad_b_mod, grad_norm_g, g_gu, g_dn, g_wi, g_wo, grad_na_rpb, grad_w_pool, grad_pool_scale]
    ws = [c_ctx, w_mod, b_mod, norm_g, w_ffn_gate_up, w_ffn_down, w_in, w_out, na_rpb, w_pool, pool_scale]
    ms = [m_c_ctx, m_w_mod, m_b_mod, m_norm_g, m_w_ffn_gate_up, m_w_ffn_down, m_w_in, m_w_out, m_na_rpb, m_w_pool, m_pool_scale]
    vs = [v_c_ctx, v_w_mod, v_b_mod, v_norm_g, v_w_ffn_gate_up, v_w_ffn_down, v_w_in, v_w_out, v_na_rpb, v_w_pool, v_pool_scale]
    tags = ["c_ctx", "w_mod", "b_mod", "norm_g", "gate_up", "down", "w_in", "w_out", "na_rpb", "w_pool", "pool_scale"]
    upd = [_adamw(w_, g_, m_, v_, f"adamw_{t}") for w_, g_, m_, v_, t in zip(ws, grads, ms, vs, tags)]
    return (loss, grad_x, *grads, *[u_[0] for u_ in upd], *[u_[1] for u_ in upd], *[u_[2] for u_ in upd])


def _small_finish(dm_rows, gc_all, c_ctx):
    depth, _, n = dm_rows.shape
    D = c_ctx.shape[0]

    def body(dm_ref, gc_ref, c_ref, gb_ref, gcx_ref):
        acc = dm_ref[:, 0]
        for j in range(1, N_DEV + 1):
            acc = acc + dm_ref[:, j]
        gb_ref[...] = acc
        t = jnp.zeros((1, D), f32)
        for l in range(depth):
            for j in range(N_CHIPS):
                t = t + gc_ref[2 * j, l, 0:1, :]
        cv = c_ref[...]
        sg = _sigmoid(cv)
        gcx_ref[...] = t * (sg * (1.0 + cv * (1.0 - sg)))

    gb, gcx = pl.pallas_call(
        body, name="small_finish",
        out_shape=[S_((depth, n), f32), S_((1, D), f32)],
        compiler_params=_cp(VMEM_MID),
    )(dm_rows, gc_all, c_ctx.reshape(1, D))
    return gb, gcx.reshape(D)
```

```python
import functools

import numpy as np
import jax
import jax.numpy as jnp
from jax import lax
from jax.experimental import pallas as pl
from jax.experimental.pallas import tpu as pltpu

f32, bf16 = jnp.float32, jnp.bfloat16

GRID_W = 64
N_MOD = 9
NA_HEADS = 8
HEAD_DIM = 64
NA_WIDTH = NA_HEADS * HEAD_DIM
NA_KH = 8
NA_KW = 16
POOL_GROUPS = 4
POOL_CH = 128
POOL_WIDTH = POOL_GROUPS * POOL_CH
POOL_WINDOWS = (2, 4, 8, 16)
IN_WIDTH = 3 * NA_WIDTH + POOL_WIDTH
MIX_WIDTH = NA_WIDTH + POOL_WIDTH
ROPE_THETA = 10000.0
ROPE_PAIRS = HEAD_DIM // 4
RMS_EPS = 1e-6
NEG_INF = -1e30
ADAM_LR, ADAM_B1, ADAM_B2, ADAM_EPS, ADAM_WD, ADAM_STEP = 0.001, 0.9, 0.999, 1e-08, 0.01, 10

N_DEV = 8
N_CHIPS = 4
LANES = 128
MIB = 1024 * 1024
VMEM_BIG = 52 * MIB
VMEM_MID = 40 * MIB
MESH = pl.DeviceIdType.MESH
ANY = pl.BlockSpec(memory_space=pl.ANY)
S_ = jax.ShapeDtypeStruct


def _cp(vmem=VMEM_MID, sem=None):
    return pltpu.CompilerParams(vmem_limit_bytes=vmem, dimension_semantics=sem)


def _sigmoid(x):
    return 1.0 / (1.0 + jnp.exp(-x))


def _rms_hat(x):
    rinv = lax.rsqrt(jnp.mean(x * x, axis=-1, keepdims=True) + RMS_EPS)
    return x * rinv, rinv


def _rms_bwd(dxhat, xhat, rinv):
    return rinv * (dxhat - xhat * jnp.mean(dxhat * xhat, axis=-1, keepdims=True))


def _rsum(a):
    return jnp.sum(a, axis=0, keepdims=True)


def _nt(a, b):
    return lax.dot_general(a, b, (((1,), (1,)), ((), ())), preferred_element_type=f32)


def _tn(a, b):
    return lax.dot_general(a, b, (((0,), (0,)), ((), ())), preferred_element_type=f32)


def _nn(a, b):
    return jnp.dot(a, b, preferred_element_type=f32)


def _swap16(x):
    lane = lax.broadcasted_iota(jnp.int32, x.shape, 1)
    n = x.shape[1]
    return jnp.where((lane % 32) < 16, pltpu.roll(x, n - 16, 1), pltpu.roll(x, 16, 1))


def _rope_tables(s_len, l_len):
    t = np.arange(s_len)
    inv = ROPE_THETA ** (-np.arange(ROPE_PAIRS, dtype=np.float32) / ROPE_PAIRS)
    ang_r = (t // GRID_W).astype(np.float32)[:, None] * inv
    ang_c = (t % GRID_W).astype(np.float32)[:, None] * inv
    cos = np.concatenate([np.cos(ang_r), np.cos(ang_r), np.cos(ang_c), np.cos(ang_c)], axis=-1)
    sin = np.concatenate([-np.sin(ang_r), np.sin(ang_r), -np.sin(ang_c), np.sin(ang_c)], axis=-1)
    cos = np.concatenate([cos, np.ones((l_len, HEAD_DIM), np.float32)], axis=0)
    sin = np.concatenate([sin, np.zeros((l_len, HEAD_DIM), np.float32)], axis=0)
    return (jnp.asarray(np.tile(cos, (1, 2)), f32), jnp.asarray(np.tile(sin, (1, 2)), f32))


def _pool_tables(tm, l_len):
    band = np.zeros((2, POOL_GROUPS, tm, tm), np.float32)
    inv = np.zeros((2, POOL_GROUPS, tm, 1), np.float32)
    for typ, length in ((0, GRID_W), (1, l_len)):
        for g, w in enumerate(POOL_WINDOWS):
            for t in range(tm):
                base, p = (t // length) * length, t % length
                lo = min(max(p - w // 2, 0), length)
                hi = min(max(p - w // 2 + w, 0), length)
                band[typ, g, t, base + lo:base + hi] = 1.0
                inv[typ, g, t, 0] = 1.0 / (hi - lo)
    return jnp.asarray(band, bf16), jnp.asarray(inv, f32)


NA_QR = 4
NA_WR = NA_KH + NA_QR - 1
NA_TYPES = 3
NA_SEL_ROWS = 136


def _rpb_index_tables():
    j = np.arange(GRID_W)
    col_start = np.clip(j - NA_KW // 2, 0, GRID_W - NA_KW)
    valid = (j[None, :] >= col_start[:, None]) & (j[None, :] < col_start[:, None] + NA_KW)
    dc = np.clip(j[None, :] - j[:, None] + NA_KW - 1, 0, 2 * NA_KW - 2)
    i = np.arange(NA_QR)[:, None]
    kk = np.arange(NA_WR)[None, :]
    off = np.stack([np.zeros_like(i), i, np.full_like(i, NA_QR - 1)])
    d = np.stack([kk - i + NA_KH - 1, kk - i + NA_KH - 1 - NA_QR, kk - i])
    row_ok = (kk[None] >= off) & (kk[None] < off + NA_KH)
    assert (d[row_ok] >= 0).all() and (d[row_ok] <= 2 * NA_KH - 2).all()
    return valid, dc, d, row_ok


def _expand_rpb(rpb):
    valid, _, d, row_ok = _rpb_index_tables()
    heads, nd, ne = rpb.shape
    w = GRID_W
    v = jnp.pad(rpb, ((0, 0), (0, 0), (w - NA_KW, 2 * w - (w - NA_KW) - ne)))
    x = jnp.broadcast_to(v[:, :, None, :], (heads, nd, w, 2 * w)).reshape(heads, nd, 2 * w * w)
    t = x[:, :, :w * (2 * w - 1)].reshape(heads, nd, w, 2 * w - 1)[..., w - 1:]
    lo, hi = int(d.min()), int(d.max())
    tp = jnp.pad(t, ((0, 0), (max(0, -lo), max(0, hi - nd + 1)), (0, 0), (0, 0)))
    types = []
    for typ in range(NA_TYPES):
        rows_i = [tp[:, d[typ, i, 0] - min(lo, 0):d[typ, i, 0] - min(lo, 0) + NA_WR] for i in range(NA_QR)]
        types.append(jnp.stack(rows_i, axis=1))
    b = jnp.stack(types)
    ok = row_ok[:, None, :, :, None, None] & valid[None, None, None, None]
    b = jnp.where(ok, b, NEG_INF)
    b = jnp.transpose(b, (0, 1, 2, 4, 3, 5))
    return b.reshape(NA_TYPES, NA_HEADS, NA_QR * GRID_W, NA_WR * GRID_W)


def _rpb_reduce_tables():
    _, dc, d, row_ok = _rpb_index_tables()
    onehot = np.zeros((GRID_W * GRID_W, LANES), np.float32)
    onehot[np.arange(GRID_W * GRID_W), dc.reshape(-1)] = 1.0
    sel = np.zeros((16, NA_SEL_ROWS), np.float32)
    flat_d, flat_ok = d.reshape(-1), row_ok.reshape(-1)
    for n in range(flat_d.size):
        if flat_ok[n]:
            sel[flat_d[n], n] = 1.0
    return jnp.asarray(onehot), jnp.asarray(sel)


class _Cfg:
    def __init__(self, s_len, l_len, d, f):
        self.S, self.L, self.D, self.F = s_len, l_len, d, f
        self.T = s_len + l_len
        self.TM = 256 if l_len % 256 == 0 else 128
        assert l_len == self.TM, "context length must equal the row tile"
        assert s_len % self.TM == 0 and s_len % GRID_W == 0
        self.nxt = s_len // self.TM
        self.ntt = self.T // self.TM
        self.rows = s_len // GRID_W
        assert self.rows >= 2 * NA_KH
        assert f % (2 * LANES) == 0
        self.FC = f // 2

    def ntiles(self, with_ctx):
        return self.ntt if with_ctx else self.nxt


def _typ(cfg):
    return lambda i: (jnp.minimum(i // cfg.nxt, 1), 0, 0)


def _mesh_pos():
    return lax.axis_index("x"), lax.axis_index("y"), lax.axis_index("c")


class _Comm:
    def __init__(self, ins, outs, alias, nsem, start, finish):
        self.ins, self.outs, self.alias, self.nsem, self.start, self.finish = ins, outs, alias, nsem, start, finish


def _call(body, args, comm=None, *, grid, in_specs, out_specs, out_shape, scratch_shapes=(), **kw):
    if comm is None:
        return pl.pallas_call(body, grid=grid, in_specs=list(in_specs), out_specs=list(out_specs), out_shape=list(out_shape),
                              scratch_shapes=list(scratch_shapes), **kw)(*args), ()
    n_in, n_out, n_sc = len(in_specs), len(out_specs), len(scratch_shapes)
    ci, co = len(comm.ins), len(comm.outs)

    def carrier(*refs):
        bounds = np.cumsum([0, n_in, ci, n_out, co, n_sc])
        ins, cins, outs, couts, scr = (refs[a:b] for a, b in zip(bounds[:-1], bounds[1:]))
        send, recv = refs[bounds[-1]], refs[bounds[-1] + 1]
        first = functools.reduce(jnp.logical_and, [pl.program_id(a) == 0 for a in range(len(grid))])
        last = functools.reduce(jnp.logical_and, [pl.program_id(a) == g - 1 for a, g in enumerate(grid)])

        @pl.when(first)
        def _():
            comm.start(cins, couts, send, recv)

        body(*ins, *outs, *scr)

        @pl.when(last)
        def _():
            comm.finish(cins, couts, send, recv)

    res = pl.pallas_call(
        carrier, grid=grid, in_specs=list(in_specs) + [ANY] * ci, out_specs=list(out_specs) + [ANY] * co,
        out_shape=list(out_shape) + list(comm.outs),
        input_output_aliases={n_in + a: n_out + b for a, b in comm.alias.items()},
        scratch_shapes=list(scratch_shapes) + [pltpu.SemaphoreType.DMA((comm.nsem,)), pltpu.SemaphoreType.DMA((comm.nsem,))],
        **kw)(*args, *comm.ins)
    return res[:n_out], res[n_out:]


def _comm_only(comm, name):
    ci, co = len(comm.ins), len(comm.outs)

    def body(*refs):
        cins, couts = refs[:ci], refs[ci:ci + co]
        send, recv = refs[ci + co], refs[ci + co + 1]
        comm.start(cins, couts, send, recv)
        comm.finish(cins, couts, send, recv)

    return pl.pallas_call(
        body, name=name, in_specs=[ANY] * ci, out_specs=[ANY] * co, out_shape=list(comm.outs),
        input_output_aliases=dict(comm.alias),
        scratch_shapes=[pltpu.SemaphoreType.DMA((comm.nsem,)), pltpu.SemaphoreType.DMA((comm.nsem,))],
        compiler_params=_cp(VMEM_MID),
    )(*comm.ins)


def _half_view(ref, axis, kk, h):
    r, c = ref.shape
    if axis == 1:
        n = c // N_CHIPS
        return ref.at[pl.ds(h * (r // 2), r // 2), pl.ds(pl.multiple_of(kk * n, LANES), n)]
    n = r // N_CHIPS
    return ref.at[pl.ds(pl.multiple_of(kk * n + h * (n // 2), 8), n // 2), :]


def _other_chips(x, y):
    return [(1 - x, y), (x, 1 - y), (1 - x, 1 - y)]


def _gather_comm(arrs, axes):
    n = len(arrs)

    def copy(ref, view, sems, k, to):
        send, recv = sems
        return pltpu.make_async_remote_copy(src_ref=view, dst_ref=view, send_sem=send.at[k], recv_sem=recv.at[k],
                                            device_id=to, device_id_type=MESH)

    def start(cins, bufs, send, recv):
        x, y, c = _mesh_pos()
        for t in range(n):
            own = _half_view(bufs[t], axes[t], 2 * x + y, c)
            for j, chip in enumerate(_other_chips(x, y)):
                copy(bufs[t], own, (send, recv), 6 * t + j, (*chip, c)).start()

    def finish(cins, bufs, send, recv):
        x, y, c = _mesh_pos()
        sibling = (x, y, 1 - c)
        chips = _other_chips(x, y)
        for t in range(n):
            for j, chip in enumerate(chips):
                landed = _half_view(bufs[t], axes[t], 2 * chip[0] + chip[1], c)
                copy(bufs[t], landed, (send, recv), 6 * t + j, (*chip, c)).wait_recv()
                copy(bufs[t], landed, (send, recv), 6 * t + 3 + j, sibling).start()
        for t in range(n):
            own = _half_view(bufs[t], axes[t], 2 * x + y, c)
            for j, chip in enumerate(chips):
                kj = 2 * chip[0] + chip[1]
                copy(bufs[t], _half_view(bufs[t], axes[t], kj, 1 - c), (send, recv), 6 * t + 3 + j, sibling).wait_recv()
                copy(bufs[t], own, (send, recv), 6 * t + j, (*chip, c)).wait_send()
                copy(bufs[t], _half_view(bufs[t], axes[t], kj, c), (send, recv), 6 * t + 3 + j, sibling).wait_send()

    return _Comm(list(arrs), [S_(a.shape, a.dtype) for a in arrs], {t: t for t in range(n)}, 6 * n, start, finish)


def _scatter_comm(parts, axes):
    n = len(parts)
    peers = [(fx, fy, fc) for fx in (0, 1) for fy in (0, 1) for fc in (0, 1)][1:]

    def half_shape(a, axis):
        r, c = a.shape
        return (r // 2, c // N_CHIPS) if axis == 1 else (r // N_CHIPS // 2, c)

    def start(srcs, lands, send, recv):
        x, y, c = _mesh_pos()
        me = 4 * x + 2 * y + c
        for t in range(n):
            for r_, (fx, fy, fc) in enumerate(peers):
                dx, dy, dc = (1 - x if fx else x), (1 - y if fy else y), (1 - c if fc else c)
                pltpu.make_async_remote_copy(
                    src_ref=_half_view(srcs[t], axes[t], 2 * dx + dy, dc), dst_ref=lands[t].at[me],
                    send_sem=send.at[7 * t + r_], recv_sem=recv.at[7 * t + r_],
                    device_id=(dx, dy, dc), device_id_type=MESH).start()

    def finish(srcs, lands, send, recv):
        x, y, c = _mesh_pos()
        for t in range(n):
            mine = _half_view(srcs[t], axes[t], 2 * x + y, c)
            for r_, (fx, fy, fc) in enumerate(peers):
                sx, sy, sc = (1 - x if fx else x), (1 - y if fy else y), (1 - c if fc else c)
                cp = pltpu.make_async_remote_copy(
                    src_ref=mine, dst_ref=lands[t].at[4 * sx + 2 * sy + sc],
                    send_sem=send.at[7 * t + r_], recv_sem=recv.at[7 * t + r_],
                    device_id=(sx, sy, sc), device_id_type=MESH)
                cp.wait_recv()
                cp.wait_send()

    return _Comm(list(parts), [S_((N_DEV,) + half_shape(a, ax), a.dtype) for a, ax in zip(parts, axes)], {}, 7 * n, start, finish)


def _ffn_fwd(cfg, xs, mods, gvec, wgu, wd, mi, gi, with_ctx, name, comm=None):
    TM, D, F, FC = cfg.TM, cfg.D, cfg.F, cfg.FC
    nt = cfg.ntiles(with_ctx)
    R = nt * TM

    def body(xs_ref, mods_ref, g_ref, wgu_hbm, wd_hbm, out_ref, hb_ref, z_ref, y_ref, wgu_v, wd_v, sem):
        @pl.when(pl.program_id(0) == 0)
        def _():
            c0 = pltpu.make_async_copy(wgu_hbm, wgu_v, sem.at[0])
            c1 = pltpu.make_async_copy(wd_hbm, wd_v, sem.at[1])
            c0.start(); c1.start(); c0.wait(); c1.wait()
        x = xs_ref[...]
        m = mods_ref[0]
        sh, sc, gt = m[mi:mi + 1], m[mi + 1:mi + 2], m[mi + 2:mi + 3]
        xhat, _ = _rms_hat(x)
        h = (xhat * g_ref[gi:gi + 1]) * (1.0 + sc) + sh
        hb = h.astype(bf16)
        hb_ref[...] = hb
        y = jnp.zeros((TM, D), f32)
        for ch in range(F // FC):
            zg = _nn(hb, wgu_v[:, ch * FC:(ch + 1) * FC])
            zu = _nn(hb, wgu_v[:, F + ch * FC:F + (ch + 1) * FC])
            z_ref[:, ch * FC:(ch + 1) * FC] = zg.astype(bf16)
            z_ref[:, F + ch * FC:F + (ch + 1) * FC] = zu.astype(bf16)
            a = (zg * _sigmoid(zg)) * zu
            y = y + _nn(a.astype(bf16), wd_v[ch * FC:(ch + 1) * FC, :])
        y_ref[...] = y
        yhat, _ = _rms_hat(y)
        out_ref[...] = x + 0.5 * gt * (yhat * g_ref[gi + 1:gi + 2])

    rt = lambda c: pl.BlockSpec((TM, c), lambda i: (i, 0))
    return _call(
        body, (xs, mods, gvec, wgu, wd), comm, name=name, grid=(nt,),
        in_specs=[rt(D), pl.BlockSpec((1, N_MOD, D), _typ(cfg)), pl.BlockSpec((6, D), lambda i: (0, 0)), ANY, ANY],
        out_specs=[rt(D), rt(D), rt(2 * F), rt(D)],
        out_shape=[S_((R, D), f32), S_((R, D), bf16), S_((R, 2 * F), bf16), S_((R, D), f32)],
        scratch_shapes=[pltpu.VMEM((D, 2 * F), bf16), pltpu.VMEM((F, D), bf16), pltpu.SemaphoreType.DMA((2,))],
        compiler_params=_cp(VMEM_BIG, ("arbitrary",)),
    )


def _ffn_bwd(cfg, dout, xs, z, y, mods, gvec, wgu, wd, mi, gi, with_ctx, name, comm=None):
    TM, D, F, FC = cfg.TM, cfg.D, cfg.F, cfg.FC
    nt = cfg.ntiles(with_ctx)
    R = nt * TM
    ntyp = 2 if with_ctx else 1

    def body(do_ref, xs_ref, z_ref, y_ref, mods_ref, g_ref, wgu_hbm, wd_hbm,
             dx_ref, dz_ref, dy_ref, a_ref, dm_ref, dg_ref, wgu_v, wd_v, sem):
        i = pl.program_id(0)

        @pl.when(i == 0)
        def _():
            c0 = pltpu.make_async_copy(wgu_hbm, wgu_v, sem.at[0])
            c1 = pltpu.make_async_copy(wd_hbm, wd_v, sem.at[1])
            c0.start(); c1.start(); c0.wait(); c1.wait()
            dg_ref[...] = jnp.zeros_like(dg_ref)

        @pl.when((i == 0) | (i == cfg.nxt))
        def _():
            dm_ref[...] = jnp.zeros_like(dm_ref)

        do = do_ref[...]
        x = xs_ref[...]
        m = mods_ref[0]
        sc, gt = m[mi + 1:mi + 2], m[mi + 2:mi + 3]
        g_pre, g_post = g_ref[gi:gi + 1], g_ref[gi + 1:gi + 2]
        xhat, rinv0 = _rms_hat(x)
        n0 = xhat * g_pre
        yhat, rinv1 = _rms_hat(y_ref[...])
        d_gt = _rsum(0.5 * do * (yhat * g_post))
        dr = (0.5 * gt) * do
        dg_post = _rsum(dr * yhat)
        dy = _rms_bwd(dr * g_post, yhat, rinv1)
        dyb = dy.astype(bf16)
        dy_ref[...] = dyb
        dh = jnp.zeros((TM, D), f32)
        for ch in range(F // FC):
            zg = z_ref[:, ch * FC:(ch + 1) * FC].astype(f32)
            zu = z_ref[:, F + ch * FC:F + (ch + 1) * FC].astype(f32)
            sg = _sigmoid(zg)
            silu = zg * sg
            a_ref[:, ch * FC:(ch + 1) * FC] = (silu * zu).astype(bf16)
            da = _nt(dyb, wd_v[ch * FC:(ch + 1) * FC, :])
            dzu = (da * silu).astype(bf16)
            dzg = (da * zu * (sg * (1.0 + zg * (1.0 - sg)))).astype(bf16)
            dz_ref[:, ch * FC:(ch + 1) * FC] = dzg
            dz_ref[:, F + ch * FC:F + (ch + 1) * FC] = dzu
            dh = dh + _nt(dzg, wgu_v[:, ch * FC:(ch + 1) * FC]) + _nt(dzu, wgu_v[:, F + ch * FC:F + (ch + 1) * FC])
        d_sh = _rsum(dh)
        d_sc = _rsum(dh * n0)
        dn = dh * (1.0 + sc)
        dg_pre = _rsum(dn * xhat)
        dx_ref[...] = do + _rms_bwd(dn * g_pre, xhat, rinv0)
        dm_ref[0] += jnp.concatenate([d_sh, d_sc, d_gt], axis=0)
        dg_ref[...] += jnp.concatenate([dg_pre, dg_post], axis=0)

    rt = lambda c: pl.BlockSpec((TM, c), lambda i: (i, 0))
    return _call(
        body, (dout, xs, z, y, mods, gvec, wgu, wd), comm, name=name, grid=(nt,),
        in_specs=[rt(D), rt(D), rt(2 * F), rt(D), pl.BlockSpec((1, N_MOD, D), _typ(cfg)),
                  pl.BlockSpec((6, D), lambda i: (0, 0)), ANY, ANY],
        out_specs=[rt(D), rt(2 * F), rt(D), rt(F), pl.BlockSpec((1, 3, D), _typ(cfg)), pl.BlockSpec((2, D), lambda i: (0, 0))],
        out_shape=[S_((R, D), f32), S_((R, 2 * F), bf16), S_((R, D), bf16), S_((R, F), bf16),
                   S_((ntyp, 3, D), f32), S_((2, D), f32)],
        scratch_shapes=[pltpu.VMEM((D, 2 * F), bf16), pltpu.VMEM((F, D), bf16), pltpu.SemaphoreType.DMA((2,))],
        compiler_params=_cp(VMEM_BIG, ("arbitrary",)),
    )


def _wgrad(a, b, k_rows, name, tk=512):
    M, N = a.shape[1], b.shape[1]
    tk = tk if k_rows % tk == 0 else 256 if k_rows % 256 == 0 else 128
    tn = N
    for cand in (1408, 1024, 512):
        if N % cand == 0 and N > cand:
            tn = cand
            break
    nk = k_rows // tk

    def body(a_ref, b_ref, o_ref, ob_ref):
        k = pl.program_id(1)

        @pl.when(k == 0)
        def _():
            o_ref[...] = jnp.zeros_like(o_ref)
        o_ref[...] += _tn(a_ref[...], b_ref[...])

        @pl.when(k == nk - 1)
        def _():
            ob_ref[...] = o_ref[...].astype(bf16)

    ospec = pl.BlockSpec((M, tn), lambda n, k: (0, n))
    return pl.pallas_call(
        body, name=name, grid=(N // tn, nk),
        in_specs=[pl.BlockSpec((tk, M), lambda n, k: (k, 0)), pl.BlockSpec((tk, tn), lambda n, k: (k, n))],
        out_specs=[ospec, ospec], out_shape=[S_((M, N), f32), S_((M, N), bf16)],
        compiler_params=_cp(VMEM_MID, ("arbitrary", "arbitrary")),
    )(a, b)


def _tmpre_fwd(cfg, xs, mods, gvec, w_in, cos, sin, name):
    TM, D = cfg.TM, cfg.D
    nt, R = cfg.ntt, cfg.T
    W = NA_WIDTH

    def body(xs_ref, mods_ref, g_ref, w_ref, cos_ref, sin_ref, hb_ref, q_ref, k_ref, v_ref, u_ref):
        x = xs_ref[...]
        m = mods_ref[0]
        xhat, _ = _rms_hat(x)
        hb = ((xhat * g_ref[2:3]) * (1.0 + m[4:5]) + m[3:4]).astype(bf16)
        hb_ref[...] = hb
        p = _nn(hb, w_ref[...])
        cs = jnp.tile(cos_ref[...], (1, W // LANES))
        sn = jnp.tile(sin_ref[...], (1, W // LANES))
        q = p[:, 0:W]
        k = p[:, W:2 * W]
        q_ref[...] = ((q * cs + _swap16(q) * sn) * (HEAD_DIM ** -0.5)).astype(bf16)
        k_ref[...] = (k * cs + _swap16(k) * sn).astype(bf16)
        v_ref[...] = p[:, 2 * W:3 * W].astype(bf16)
        u_ref[...] = p[:, 3 * W:]

    rt = lambda c: pl.BlockSpec((TM, c), lambda i: (i, 0))
    return pl.pallas_call(
        body, name=name, grid=(nt,),
        in_specs=[rt(D), pl.BlockSpec((1, N_MOD, D), _typ(cfg)), pl.BlockSpec((6, D), lambda i: (0, 0)),
                  pl.BlockSpec((D, IN_WIDTH), lambda i: (0, 0)), rt(LANES), rt(LANES)],
        out_specs=[rt(D), rt(W), rt(W), rt(W), rt(POOL_WIDTH)],
        out_shape=[S_((R, D), bf16), S_((R, W), bf16), S_((R, W), bf16), S_((R, W), bf16), S_((R, POOL_WIDTH), f32)],
        compiler_params=_cp(VMEM_MID, ("arbitrary",)),
    )(xs, mods, gvec, w_in, cos, sin)


def _rope_bwd_assemble(cfg, lat, ctx_terms, du_has_ctx, cos, sin, name):
    TM = cfg.TM
    W = NA_WIDTH
    n_ctx = [len(t) for t in ctx_terms]
    flat_ctx = [a for t in ctx_terms for a in t]

    def body(*refs):
        dq_ref, dk_ref, dv_ref, du_ref = refs[:4]
        ctx_refs = refs[4:4 + len(flat_ctx)]
        cos_ref, sin_ref, o_ref = refs[4 + len(flat_ctx):]
        is_ctx = pl.program_id(0) >= cfg.nxt
        vals, off = [], 0
        for lat_ref, n in zip((dq_ref, dk_ref, dv_ref), n_ctx):
            cv = jnp.zeros((TM, W), f32)
            for r_ in ctx_refs[off:off + n]:
                cv = cv + r_[...]
            off += n
            vals.append(jnp.where(is_ctx, cv, lat_ref[...]))
        du_ = du_ref[...] if du_has_ctx else jnp.where(is_ctx, 0.0, du_ref[...])
        cs = jnp.tile(cos_ref[...], (1, W // LANES))
        sn = jnp.tile(sin_ref[...], (1, W // LANES))
        dq_ = vals[0] * (HEAD_DIM ** -0.5)
        dk_ = vals[1]
        o_ref[:, 0:W] = (dq_ * cs + _swap16(dq_ * sn)).astype(bf16)
        o_ref[:, W:2 * W] = (dk_ * cs + _swap16(dk_ * sn)).astype(bf16)
        o_ref[:, 2 * W:3 * W] = vals[2].astype(bf16)
        o_ref[:, 3 * W:] = du_.astype(bf16)

    rt = lambda c: pl.BlockSpec((TM, c), lambda i: (i, 0))
    lat_spec = pl.BlockSpec((TM, W), lambda i: (jnp.minimum(i, cfg.nxt - 1), 0))
    du_spec = rt(POOL_WIDTH) if du_has_ctx else lat_spec
    return pl.pallas_call(
        body, name=name, grid=(cfg.ntt,),
        in_specs=[lat_spec, lat_spec, lat_spec, du_spec] + [pl.BlockSpec((TM, W), lambda i: (0, 0))] * len(flat_ctx)
                 + [rt(LANES), rt(LANES)],
        out_specs=rt(IN_WIDTH), out_shape=S_((cfg.T, IN_WIDTH), bf16),
        compiler_params=_cp(VMEM_MID, ("arbitrary",)),
    )(*lat, *flat_ctx, cos, sin)


def _tmpre_bwd(cfg, dproj, w_in, xs, mods, gvec, dres, res_with_ctx, name):
    TM, D = cfg.TM, cfg.D
    nt, R = cfg.ntt, cfg.T
    nres = cfg.ntiles(res_with_ctx)

    def body(dp_ref, w_ref, xs_ref, mods_ref, g_ref, dres_ref, dx_ref, dm_ref, dg_ref):
        i = pl.program_id(0)

        @pl.when(i == 0)
        def _():
            dg_ref[...] = jnp.zeros_like(dg_ref)

        @pl.when((i == 0) | (i == cfg.nxt))
        def _():
            dm_ref[...] = jnp.zeros_like(dm_ref)

        dh = _nt(dp_ref[...], w_ref[...])
        x = xs_ref[...]
        m = mods_ref[0]
        g2 = g_ref[2:3]
        xhat, rinv = _rms_hat(x)
        d_sh = _rsum(dh)
        d_sc = _rsum(dh * (xhat * g2))
        dn = dh * (1.0 + m[4:5])
        dg_ref[...] += _rsum(dn * xhat)
        dx = _rms_bwd(dn * g2, xhat, rinv)
        res = dres_ref[...]
        if nres < nt:
            res = jnp.where(i < nres, res, 0.0)
        dx_ref[...] = res + dx
        dm_ref[0] += jnp.concatenate([d_sh, d_sc], axis=0)

    rt = lambda c: pl.BlockSpec((TM, c), lambda i: (i, 0))
    return pl.pallas_call(
        body, name=name, grid=(nt,),
        in_specs=[rt(IN_WIDTH), pl.BlockSpec((D, IN_WIDTH), lambda i: (0, 0)), rt(D),
                  pl.BlockSpec((1, N_MOD, D), _typ(cfg)), pl.BlockSpec((6, D), lambda i: (0, 0)),
                  pl.BlockSpec((TM, D), lambda i: (jnp.minimum(i, nres - 1), 0))],
        out_specs=[rt(D), pl.BlockSpec((1, 2, D), _typ(cfg)), pl.BlockSpec((1, D), lambda i: (0, 0))],
        out_shape=[S_((R, D), f32), S_((2, 2, D), f32), S_((1, D), f32)],
        compiler_params=_cp(VMEM_MID, ("arbitrary",)),
    )(dproj, w_in, xs, mods, gvec, dres)


def _na_block(cfg, b):
    return jnp.clip(NA_QR * b - NA_KH // 2, 0, cfg.rows - NA_WR)


def _na_load_bias(b, nb, b_hbm, b_v, sem):
    for typ, at in ((0, 0), (1, 1), (2, nb - 1)):
        @pl.when(b == at)
        def _(typ=typ):
            cp = pltpu.make_async_copy(b_hbm.at[typ], b_v, sem)
            cp.start()
            cp.wait()


def _na_probs(qh, klh, kch, bias):
    s_loc = _nt(qh, klh) + bias
    s_ctx = _nt(qh, kch)
    mx = jnp.maximum(jnp.max(s_loc, axis=-1, keepdims=True), jnp.max(s_ctx, axis=-1, keepdims=True))
    e_loc = jnp.exp(s_loc - mx)
    e_ctx = jnp.exp(s_ctx - mx)
    inv = 1.0 / (jnp.sum(e_loc, axis=-1, keepdims=True) + jnp.sum(e_ctx, axis=-1, keepdims=True))
    return e_loc * inv, e_ctx * inv


def _na_fwd(cfg, q, k, v, bexp, name, comm=None):
    S, L, T = cfg.S, cfg.L, cfg.T
    NQ, NW = NA_QR * GRID_W, NA_WR * GRID_W
    nb = cfg.rows // NA_QR

    def body(q_ref, k_hbm, v_hbm, b_hbm, o_ref, k_v, v_v, b_v, sem):
        b = pl.program_id(0)

        @pl.when(b == 0)
        def _():
            cs = [pltpu.make_async_copy(k_hbm, k_v, sem.at[0]), pltpu.make_async_copy(v_hbm, v_v, sem.at[1])]
            for c_ in cs:
                c_.start()
            for c_ in cs:
                c_.wait()

        _na_load_bias(b, nb, b_hbm, b_v, sem.at[2])
        st = pl.multiple_of(_na_block(cfg, b) * GRID_W, GRID_W)
        first = lax.broadcasted_iota(jnp.int32, (NQ, LANES), 1) < HEAD_DIM
        for hp in range(NA_HEADS // 2):
            ls = slice(hp * LANES, (hp + 1) * LANES)
            q2 = q_ref[:, ls]
            kl, vl = k_v[pl.ds(st, NW), ls], v_v[pl.ds(st, NW), ls]
            kc, vc = k_v[S:T, ls], v_v[S:T, ls]
            o2 = []
            for hh in range(2):
                qm = jnp.where(first if hh == 0 else ~first, q2, jnp.zeros_like(q2))
                p_loc, p_ctx = _na_probs(qm, kl, kc, b_v[2 * hp + hh])
                o2.append(_nn(p_loc.astype(bf16), vl) + _nn(p_ctx.astype(bf16), vc))
            o_ref[:, ls] = jnp.where(first, o2[0], o2[1]).astype(bf16)

    return _call(
        body, (q, k, v, bexp), comm, name=name, grid=(nb,),
        in_specs=[pl.BlockSpec((NQ, NA_WIDTH), lambda b: (b, 0)), ANY, ANY, ANY],
        out_specs=[pl.BlockSpec((NQ, NA_WIDTH), lambda b: (b, 0))],
        out_shape=[S_((S, NA_WIDTH), bf16)],
        scratch_shapes=[pltpu.VMEM((T, NA_WIDTH), bf16), pltpu.VMEM((T, NA_WIDTH), bf16),
                        pltpu.VMEM((NA_HEADS, NQ, NW), f32), pltpu.SemaphoreType.DMA((3,))],
        compiler_params=_cp(VMEM_MID, ("arbitrary",)),
    )


def _na_bwd(cfg, do, q, k, v, bexp, name, comm=None):
    S, L, T, rows = cfg.S, cfg.L, cfg.T, cfg.rows
    NQ, NW = NA_QR * GRID_W, NA_WR * GRID_W
    NSLOT = 2 * NA_KH
    nb = rows // NA_QR
    bmax = (rows - NA_WR) // NA_QR
    steps = 2 * nb - bmax
    W = NA_WIDTH
    assert nb >= 3 and bmax >= 1 and rows - NA_QR * bmax <= NSLOT

    def out_group(g):
        return jnp.where(g >= nb, g - nb + bmax, jnp.clip(g - 1, 0, bmax - 1))

    def body(do_ref, q_ref, k_hbm, v_hbm, b_hbm, dq_ref, dk_ref, dv_ref, dkc_ref, dvc_ref, db_hbm,
             k_v, v_v, b_v, db_v, ak, av, akc, avc, sem):
        g = pl.program_id(0)

        @pl.when(g == 0)
        def _():
            cs = [pltpu.make_async_copy(k_hbm, k_v, sem.at[0]), pltpu.make_async_copy(v_hbm, v_v, sem.at[1])]
            for c_ in cs:
                c_.start()
            db_v[...] = jnp.zeros_like(db_v)
            ak[...] = jnp.zeros_like(ak)
            av[...] = jnp.zeros_like(av)
            akc[...] = jnp.zeros_like(akc)
            avc[...] = jnp.zeros_like(avc)
            for c_ in cs:
                c_.wait()

        for typ, at in ((0, 1), (1, nb - 1)):
            @pl.when(g == at)
            def _(typ=typ):
                cp = pltpu.make_async_copy(db_v, db_hbm.at[typ], sem.at[2])
                cp.start()
                cp.wait()
                db_v[...] = jnp.zeros_like(db_v)

        @pl.when(g < nb)
        def _():
            _na_load_bias(g, nb, b_hbm, b_v, sem.at[2])
            ws = _na_block(cfg, g)
            st = pl.multiple_of(ws * GRID_W, GRID_W)
            first = lax.broadcasted_iota(jnp.int32, (NQ, LANES), 1) < HEAD_DIM
            for hp in range(NA_HEADS // 2):
                ls = slice(hp * LANES, (hp + 1) * LANES)
                q2, do2 = q_ref[:, ls], do_ref[:, ls]
                kl, vl = k_v[pl.ds(st, NW), ls], v_v[pl.ds(st, NW), ls]
                kc, vc = k_v[S:T, ls], v_v[S:T, ls]
                dq2 = []
                dk2 = jnp.zeros((NW, LANES), f32)
                dv2 = jnp.zeros((NW, LANES), f32)
                dkc2 = jnp.zeros((L, LANES), f32)
                dvc2 = jnp.zeros((L, LANES), f32)
                for hh in range(2):
                    keep = first if hh == 0 else ~first
                    qm = jnp.where(keep, q2, jnp.zeros_like(q2))
                    dom = jnp.where(keep, do2, jnp.zeros_like(do2))
                    p_loc, p_ctx = _na_probs(qm, kl, kc, b_v[2 * hp + hh])
                    dp_loc = _nt(dom, vl)
                    dp_ctx = _nt(dom, vc)
                    delta = jnp.sum(p_loc * dp_loc, axis=-1, keepdims=True) + jnp.sum(p_ctx * dp_ctx, axis=-1, keepdims=True)
                    ds_loc = p_loc * (dp_loc - delta)
                    ds_ctx = p_ctx * (dp_ctx - delta)
                    db_v[2 * hp + hh] += ds_loc
                    dsl, dsc = ds_loc.astype(bf16), ds_ctx.astype(bf16)
                    dq2.append(_nn(dsl, kl) + _nn(dsc, kc))
                    dk2 = dk2 + _tn(dsl, qm)
                    dv2 = dv2 + _tn(p_loc.astype(bf16), dom)
                    dkc2 = dkc2 + _tn(dsc, qm)
                    dvc2 = dvc2 + _tn(p_ctx.astype(bf16), dom)
                dq_ref[:, ls] = jnp.where(first, dq2[0], dq2[1])
                akc[:, ls] += dkc2
                avc[:, ls] += dvc2
                for kk in range(NA_WR):
                    slot = (ws + kk) % NSLOT
                    ak[slot, :, ls] += dk2[kk * GRID_W:(kk + 1) * GRID_W, :]
                    av[slot, :, ls] += dv2[kk * GRID_W:(kk + 1) * GRID_W, :]

        @pl.when(((g >= 1) & (g <= bmax)) | (g >= nb))
        def _():
            base = NA_QR * (out_group(g) % (NSLOT // NA_QR))
            for t in range(NA_QR):
                dk_ref[t * GRID_W:(t + 1) * GRID_W, :] = ak[base + t]
                dv_ref[t * GRID_W:(t + 1) * GRID_W, :] = av[base + t]
                ak[base + t] = jnp.zeros((GRID_W, W), f32)
                av[base + t] = jnp.zeros((GRID_W, W), f32)

        @pl.when(g == nb - 1)
        def _():
            cp = pltpu.make_async_copy(db_v, db_hbm.at[2], sem.at[2])
            cp.start()
            cp.wait()

        @pl.when(g == steps - 1)
        def _():
            dkc_ref[...] = akc[...]
            dvc_ref[...] = avc[...]

    qmap = lambda g: (jnp.minimum(g, nb - 1), 0)
    kmap = lambda g: (out_group(g), 0)
    full = lambda g: (0, 0)
    return _call(
        body, (do, q, k, v, bexp), comm, name=name, grid=(steps,),
        in_specs=[pl.BlockSpec((NQ, W), qmap), pl.BlockSpec((NQ, W), qmap), ANY, ANY, ANY],
        out_specs=[pl.BlockSpec((NQ, W), qmap), pl.BlockSpec((NQ, W), kmap), pl.BlockSpec((NQ, W), kmap),
                   pl.BlockSpec((L, W), full), pl.BlockSpec((L, W), full), ANY],
        out_shape=[S_((S, W), f32), S_((S, W), f32), S_((S, W), f32), S_((L, W), f32), S_((L, W), f32),
                   S_((NA_TYPES, NA_HEADS, NQ, NW), f32)],
        scratch_shapes=[pltpu.VMEM((T, W), bf16), pltpu.VMEM((T, W), bf16),
                        pltpu.VMEM((NA_HEADS, NQ, NW), f32), pltpu.VMEM((NA_HEADS, NQ, NW), f32),
                        pltpu.VMEM((NSLOT, GRID_W, W), f32), pltpu.VMEM((NSLOT, GRID_W, W), f32),
                        pltpu.VMEM((L, W), f32), pltpu.VMEM((L, W), f32), pltpu.SemaphoreType.DMA((3,))],
        compiler_params=_cp(VMEM_BIG, ("arbitrary",)),
    )


def _rpb_reduce(dbias, onehot, sel, name):
    x = dbias.reshape(NA_TYPES, NA_HEADS, NA_QR, GRID_W, NA_WR, GRID_W)
    x = jnp.transpose(x, (1, 0, 2, 4, 3, 5)).reshape(NA_HEADS, NA_TYPES * NA_QR * NA_WR, GRID_W * GRID_W)
    x = jnp.pad(x, ((0, 0), (0, NA_SEL_ROWS - x.shape[1]), (0, 0)))

    def body(x_ref, oh_ref, sel_ref, o_ref):
        y = jnp.dot(x_ref[...], oh_ref[...], preferred_element_type=f32, precision=lax.Precision.HIGHEST)
        o_ref[...] = jnp.dot(sel_ref[...], y, preferred_element_type=f32, precision=lax.Precision.HIGHEST)

    return pl.pallas_call(
        body, name=name, grid=(NA_HEADS,),
        in_specs=[pl.BlockSpec((None, NA_SEL_ROWS, GRID_W * GRID_W), lambda h: (h, 0, 0)),
                  pl.BlockSpec((GRID_W * GRID_W, LANES), lambda h: (0, 0)), pl.BlockSpec((16, NA_SEL_ROWS), lambda h: (0, 0))],
        out_specs=pl.BlockSpec((None, 16, LANES), lambda h: (h, 0, 0)),
        out_shape=S_((NA_HEADS, 16, LANES), f32),
        compiler_params=_cp(VMEM_MID, ("arbitrary",)),
    )(x, onehot, sel)


def _ctx_attn_fwd(cfg, q, k, v, name):
    L = cfg.L
    blk = cfg.S // L

    def body(q_ref, k_ref, v_ref, o_ref):
        qv, kv, vv = q_ref[...], k_ref[...], v_ref[...]
        outs = []
        for h in range(NA_HEADS):
            hs = slice(h * HEAD_DIM, (h + 1) * HEAD_DIM)
            s = _nt(qv[:, hs], kv[:, hs])
            e = jnp.exp(s - jnp.max(s, axis=-1, keepdims=True))
            p = e * (1.0 / jnp.sum(e, axis=-1, keepdims=True))
            outs.append(_nn(p.astype(bf16), vv[:, hs]))
        o_ref[...] = jnp.concatenate(outs, axis=-1).astype(bf16)

    spec = pl.BlockSpec((L, NA_WIDTH), lambda i: (blk, 0))
    return pl.pallas_call(
        body, name=name, grid=(1,), in_specs=[spec, spec, spec],
        out_specs=pl.BlockSpec((L, NA_WIDTH), lambda i: (0, 0)), out_shape=S_((L, NA_WIDTH), bf16),
        compiler_params=_cp(VMEM_MID, ("arbitrary",)),
    )(q, k, v)


def _ctx_attn_bwd(cfg, do, q, k, v, name):
    L = cfg.L
    blk = cfg.S // L

    def body(do_ref, q_ref, k_ref, v_ref, dq_ref, dk_ref, dv_ref):
        dov, qv, kv, vv = do_ref[...], q_ref[...], k_ref[...], v_ref[...]
        dqs, dks, dvs = [], [], []
        for h in range(NA_HEADS):
            hs = slice(h * HEAD_DIM, (h + 1) * HEAD_DIM)
            qh, kh, doh = qv[:, hs], kv[:, hs], dov[:, hs]
            s = _nt(qh, kh)
            e = jnp.exp(s - jnp.max(s, axis=-1, keepdims=True))
            p = e * (1.0 / jnp.sum(e, axis=-1, keepdims=True))
            dp = _nt(doh, vv[:, hs])
            ds = (p * (dp - jnp.sum(p * dp, axis=-1, keepdims=True))).astype(bf16)
            dqs.append(_nn(ds, kh))
            dks.append(_tn(ds, qh))
            dvs.append(_tn(p.astype(bf16), doh))
        dq_ref[...] = jnp.concatenate(dqs, axis=-1)
        dk_ref[...] = jnp.concatenate(dks, axis=-1)
        dv_ref[...] = jnp.concatenate(dvs, axis=-1)

    spec = pl.BlockSpec((L, NA_WIDTH), lambda i: (blk, 0))
    ospec = pl.BlockSpec((L, NA_WIDTH), lambda i: (0, 0))
    return pl.pallas_call(
        body, name=name, grid=(1,), in_specs=[spec, spec, spec, spec],
        out_specs=[ospec, ospec, ospec], out_shape=[S_((L, NA_WIDTH), f32)] * 3,
        compiler_params=_cp(VMEM_MID, ("arbitrary",)),
    )(do, q, k, v)


def _pool_centered(u, band, inv):
    hi = u.astype(bf16)
    lo = (u - hi.astype(f32)).astype(bf16)
    return (_nn(band, hi) + _nn(band, lo)) * inv - u


def _pool_fwd(cfg, u, band, inv, w_pool, pool_scale, with_ctx, name):
    TM = cfg.TM
    nt = cfg.ntiles(with_ctx)
    C = POOL_CH

    def body(u_ref, band_ref, inv_ref, w_ref, ps_ref, o_ref):
        outs = []
        for g in range(POOL_GROUPS):
            d = _pool_centered(u_ref[:, g * C:(g + 1) * C], band_ref[0, g], inv_ref[0, g])
            outs.append(_nn(d.astype(bf16), w_ref[g].astype(bf16)) * ps_ref[:, g * C:(g + 1) * C])
        o_ref[...] = jnp.concatenate(outs, axis=-1).astype(bf16)

    typ4 = lambda i: (jnp.minimum(i // cfg.nxt, 1), 0, 0, 0)
    return pl.pallas_call(
        body, name=name, grid=(nt,),
        in_specs=[pl.BlockSpec((TM, POOL_WIDTH), lambda i: (i, 0)), pl.BlockSpec((1, POOL_GROUPS, TM, TM), typ4),
                  pl.BlockSpec((1, POOL_GROUPS, TM, 1), typ4), pl.BlockSpec((POOL_GROUPS, C, C), lambda i: (0, 0, 0)),
                  pl.BlockSpec((1, POOL_WIDTH), lambda i: (0, 0))],
        out_specs=pl.BlockSpec((TM, POOL_WIDTH), lambda i: (i, 0)),
        out_shape=S_((nt * TM, POOL_WIDTH), bf16),
        compiler_params=_cp(VMEM_MID, ("arbitrary",)),
    )(u, band, inv, w_pool, pool_scale)


def _pool_bwd(cfg, dmix, u, band, inv, w_pool, pool_scale, with_ctx, name):
    TM = cfg.TM
    nt = cfg.ntiles(with_ctx)
    C = POOL_CH

    def body(dy_ref, u_ref, band_ref, inv_ref, w_ref, ps_ref, du_ref, dw_ref, dps_ref):
        @pl.when(pl.program_id(0) == 0)
        def _():
            dw_ref[...] = jnp.zeros_like(dw_ref)
            dps_ref[...] = jnp.zeros_like(dps_ref)

        dus, dpss = [], []
        for g in range(POOL_GROUPS):
            gs = slice(g * C, (g + 1) * C)
            band_g, inv_g = band_ref[0, g], inv_ref[0, g]
            db = _pool_centered(u_ref[:, gs], band_g, inv_g).astype(bf16)
            wb = w_ref[g].astype(bf16)
            dy = dy_ref[:, gs].astype(f32)
            dpss.append(_rsum(dy * _nn(db, wb)))
            dys = (dy * ps_ref[:, gs]).astype(bf16)
            dw_ref[g] += _tn(db, dys)
            dd = _nt(dys, wb)
            t = dd * inv_g
            hi = t.astype(bf16)
            lo = (t - hi.astype(f32)).astype(bf16)
            dus.append(_tn(band_g, hi) + _tn(band_g, lo) - dd)
        du_ref[...] = jnp.concatenate(dus, axis=-1)
        dps_ref[...] += jnp.concatenate(dpss, axis=-1)

    typ4 = lambda i: (jnp.minimum(i // cfg.nxt, 1), 0, 0, 0)
    return pl.pallas_call(
        body, name=name, grid=(nt,),
        in_specs=[pl.BlockSpec((TM, POOL_WIDTH), lambda i: (i, 1)), pl.BlockSpec((TM, POOL_WIDTH), lambda i: (i, 0)),
                  pl.BlockSpec((1, POOL_GROUPS, TM, TM), typ4), pl.BlockSpec((1, POOL_GROUPS, TM, 1), typ4),
                  pl.BlockSpec((POOL_GROUPS, C, C), lambda i: (0, 0, 0)), pl.BlockSpec((1, POOL_WIDTH), lambda i: (0, 0))],
        out_specs=[pl.BlockSpec((TM, POOL_WIDTH), lambda i: (i, 0)), pl.BlockSpec((POOL_GROUPS, C, C), lambda i: (0, 0, 0)),
                   pl.BlockSpec((1, POOL_WIDTH), lambda i: (0, 0))],
        out_shape=[S_((nt * TM, POOL_WIDTH), f32), S_((POOL_GROUPS, C, C), f32), S_((1, POOL_WIDTH), f32)],
        compiler_params=_cp(VMEM_MID, ("arbitrary",)),
    )(dmix, u, band, inv, w_pool, pool_scale)


def _tmpost_fwd(cfg, na_x, na_c, pool, w_out, xs, mods, gvec, name):
    TM, D = cfg.TM, cfg.D
    with_ctx = na_c is not None
    nt = cfg.ntiles(with_ctx)
    R = nt * TM

    def body(*refs):
        if with_ctx:
            nax_ref, nac_ref, pool_ref, w_ref, xs_ref, mods_ref, g_ref, out_ref, opre_ref, mix_ref = refs
            na = jnp.where(pl.program_id(0) < cfg.nxt, nax_ref[...], nac_ref[...])
        else:
            nax_ref, pool_ref, w_ref, xs_ref, mods_ref, g_ref, out_ref, opre_ref, mix_ref = refs
            na = nax_ref[...]
        pool_v = pool_ref[...]
        mix_ref[:, 0:NA_WIDTH] = na
        mix_ref[:, NA_WIDTH:] = pool_v
        o = _nn(na, w_ref[0:NA_WIDTH, :]) + _nn(pool_v, w_ref[NA_WIDTH:, :])
        opre_ref[...] = o
        ohat, _ = _rms_hat(o)
        out_ref[...] = xs_ref[...] + mods_ref[0][5:6] * (ohat * g_ref[3:4])

    rt = lambda c: pl.BlockSpec((TM, c), lambda i: (i, 0))
    na_specs = [pl.BlockSpec((TM, NA_WIDTH), lambda i: (jnp.minimum(i, cfg.nxt - 1), 0))]
    na_args = [na_x]
    if with_ctx:
        na_specs.append(pl.BlockSpec((TM, NA_WIDTH), lambda i: (0, 0)))
        na_args.append(na_c)
    return pl.pallas_call(
        body, name=name, grid=(nt,),
        in_specs=na_specs + [rt(POOL_WIDTH), pl.BlockSpec((MIX_WIDTH, D), lambda i: (0, 0)), rt(D),
                             pl.BlockSpec((1, N_MOD, D), _typ(cfg)), pl.BlockSpec((6, D), lambda i: (0, 0))],
        out_specs=[rt(D), rt(D), rt(MIX_WIDTH)],
        out_shape=[S_((R, D), f32), S_((R, D), f32), S_((R, MIX_WIDTH), bf16)],
        compiler_params=_cp(VMEM_MID, ("arbitrary",)),
    )(*na_args, pool, w_out, xs, mods, gvec)


def _tmpost_bwd(cfg, dout, opre, w_out, mods, gvec, with_ctx, name):
    TM, D = cfg.TM, cfg.D
    nt = cfg.ntiles(with_ctx)
    R = nt * TM
    ntyp = 2 if with_ctx else 1

    def body(do_ref, opre_ref, w_ref, mods_ref, g_ref, dop_ref, dmix_ref, dm_ref, dg_ref):
        i = pl.program_id(0)

        @pl.when(i == 0)
        def _():
            dg_ref[...] = jnp.zeros_like(dg_ref)

        @pl.when((i == 0) | (i == cfg.nxt))
        def _():
            dm_ref[...] = jnp.zeros_like(dm_ref)

        do = do_ref[...]
        g3 = g_ref[3:4]
        ohat, rinv = _rms_hat(opre_ref[...])
        dm_ref[0] += _rsum(do * (ohat * g3))
        dr = mods_ref[0][5:6] * do
        dg_ref[...] += _rsum(dr * ohat)
        dob = _rms_bwd(dr * g3, ohat, rinv).astype(bf16)
        dop_ref[...] = dob
        dmix_ref[...] = _nt(dob, w_ref[...]).astype(bf16)

    rt = lambda c: pl.BlockSpec((TM, c), lambda i: (i, 0))
    return pl.pallas_call(
        body, name=name, grid=(nt,),
        in_specs=[rt(D), rt(D), pl.BlockSpec((MIX_WIDTH, D), lambda i: (0, 0)),
                  pl.BlockSpec((1, N_MOD, D), _typ(cfg)), pl.BlockSpec((6, D), lambda i: (0, 0))],
        out_specs=[rt(D), rt(MIX_WIDTH), pl.BlockSpec((1, 1, D), _typ(cfg)), pl.BlockSpec((1, D), lambda i: (0, 0))],
        out_shape=[S_((R, D), bf16), S_((R, MIX_WIDTH), bf16), S_((ntyp, 1, D), f32), S_((1, D), f32)],
        compiler_params=_cp(VMEM_MID, ("arbitrary",)),
    )(dout, opre, w_out, mods, gvec)


def _loss_head(cfg, y, target, name):
    TM, D = cfg.TM, cfg.D

    def body(y_ref, t_ref, dy_ref, loss_ref):
        @pl.when(pl.program_id(0) == 0)
        def _():
            loss_ref[...] = jnp.zeros_like(loss_ref)
        e = y_ref[...] - t_ref[...]
        dy_ref[...] = e * (1.0 / D)
        loss_ref[...] += jnp.sum(jnp.mean(e * e, axis=-1, keepdims=True), axis=0, keepdims=True) * 0.5

    rt = pl.BlockSpec((TM, D), lambda i: (i, 0))
    return pl.pallas_call(
        body, name=name, grid=(cfg.nxt,), in_specs=[rt, rt],
        out_specs=[rt, pl.BlockSpec((8, LANES), lambda i: (0, 0))],
        out_shape=[S_((cfg.S, D), f32), S_((8, LANES), f32)],
        compiler_params=_cp(VMEM_MID, ("arbitrary",)),
    )(y, target)


def _modvec_fwd(cvecs, w_mod, b_shard, name):
    nl, D, n = w_mod.shape
    tn = n // 3 if (n % 3 == 0 and (n // 3) % LANES == 0) else n

    def body(c_ref, w_ref, b_ref, o_ref, s_ref):
        cv = c_ref[...]
        sv = cv * _sigmoid(cv)
        s_ref[...] = sv
        o_ref[...] = _nn(sv.astype(bf16), w_ref[...].astype(bf16)) + b_ref[...]

    return pl.pallas_call(
        body, name=name, grid=(nl, n // tn),
        in_specs=[pl.BlockSpec((16, D), lambda l, j: (0, 0)), pl.BlockSpec((None, D, tn), lambda l, j: (l, 0, j)),
                  pl.BlockSpec((None, 1, tn), lambda l, j: (l, 0, j))],
        out_specs=[pl.BlockSpec((None, 16, tn), lambda l, j: (l, 0, j)), pl.BlockSpec((16, D), lambda l, j: (0, 0))],
        out_shape=[S_((nl, 16, n), f32), S_((16, D), f32)],
        compiler_params=_cp(VMEM_MID, ("arbitrary", "arbitrary")),
    )(cvecs, w_mod, b_shard)


def _modvec_bwd(s_t, dm, w_mod, name):
    nl, D, n = w_mod.shape
    tn = n // 3 if (n % 3 == 0 and (n // 3) % LANES == 0) else n

    def body(s_ref, dm_ref, w_ref, gw_ref, gc_ref):
        @pl.when(pl.program_id(1) == 0)
        def _():
            gc_ref[...] = jnp.zeros_like(gc_ref)
        dmv = dm_ref[...]
        gw_ref[...] = jnp.dot(s_ref[...], dmv, preferred_element_type=f32, precision=lax.Precision.HIGHEST)
        gc_ref[...] += _nt(dmv[8:16].astype(bf16), w_ref[...].astype(bf16))

    return pl.pallas_call(
        body, name=name, grid=(nl, n // tn),
        in_specs=[pl.BlockSpec((D, 16), lambda l, j: (0, 0)), pl.BlockSpec((None, 16, tn), lambda l, j: (l, 0, j)),
                  pl.BlockSpec((None, D, tn), lambda l, j: (l, 0, j))],
        out_specs=[pl.BlockSpec((None, D, tn), lambda l, j: (l, 0, j)), pl.BlockSpec((None, 8, D), lambda l, j: (l, 0, 0))],
        out_shape=[S_((nl, D, n), f32), S_((nl, 8, D), f32)],
        compiler_params=_cp(VMEM_MID, ("arbitrary", "arbitrary")),
    )(s_t, dm, w_mod)


def _as2d(a):
    n = a.size
    if a.ndim >= 2 and a.shape[-1] % LANES == 0:
        return a.reshape(-1, a.shape[-1])
    if n % LANES == 0:
        return a.reshape(-1, LANES)
    return a.reshape(-1, a.shape[-1]) if a.ndim >= 2 else a.reshape(1, n)


def _row_tile(r, c, budget_elems=512 * 1024):
    if r * c <= budget_elems or r % 8 != 0:
        return r
    t = r
    while t * c > budget_elems and t % 16 == 0:
        t //= 2
    return t


def _div_tile(r, c, budget_elems, mult=16):
    best = None
    for t in range(mult, r + 1, mult):
        if r % t == 0 and t * c <= budget_elems:
            best = t
    return best if best is not None else r


def _cast_into_place(shards, lead, axis, kidx, name):
    r, c = shards.shape[-2:]
    tr = _div_tile(r, c, 768 * 1024)
    nr = r // tr
    out_map = (lambda i, k: (i, k[0])) if axis == 1 else (lambda i, k: (k[0] * nr + i, 0))
    full2 = (r, c * N_CHIPS) if axis == 1 else (r * N_CHIPS, c)

    def body(k_ref, a_ref, o_ref):
        o_ref[...] = a_ref[...].astype(bf16)

    return pl.pallas_call(
        body, name=name,
        grid_spec=pltpu.PrefetchScalarGridSpec(
            num_scalar_prefetch=1, grid=(nr,),
            in_specs=[pl.BlockSpec((None,) * len(lead) + (tr, c), lambda i, k: tuple(lead) + (i, 0))],
            out_specs=pl.BlockSpec((tr, c), out_map)),
        out_shape=S_(full2, bf16), compiler_params=_cp(VMEM_MID, ("arbitrary",)),
    )(kidx, shards)


def _sum_devices8(own, land, axis, into, lead, dck, name):
    _, rh, cs = land.shape
    tr = _div_tile(rh, cs, 400 * 1024)
    nr = rh // tr
    if axis == 1:
        own_map = lambda i, s: (s[1] * nr + i, s[2])
    else:
        own_map = lambda i, s: (s[2] * 2 * nr + s[1] * nr + i, 0)
    nl = len(lead)

    def land_spec(j):
        return pl.BlockSpec((None, tr, cs), lambda i, s: ((s[0] + j) % N_DEV, i, 0))

    def body(s_ref, own_ref, *rest):
        acc = own_ref[...]
        for p_ref in rest[:N_DEV - 1]:
            acc = acc + p_ref[...].astype(f32)
        rest[-1][...] = acc

    return pl.pallas_call(
        body, name=name,
        grid_spec=pltpu.PrefetchScalarGridSpec(
            num_scalar_prefetch=1, grid=(nr,),
            in_specs=[pl.BlockSpec((tr, cs), own_map)] + [land_spec(j) for j in range(1, N_DEV)] + [ANY],
            out_specs=pl.BlockSpec((None,) * nl + (tr, cs), lambda i, s: tuple(lead) + (s[1] * nr + i, 0))),
        out_shape=S_(into.shape, f32), input_output_aliases={N_DEV + 1: 0},
        compiler_params=_cp(VMEM_MID, ("arbitrary",)),
    )(dck, own, *([land] * (N_DEV - 1)), into)


def _adamw(w, g, m, v, name):
    shape = w.shape
    w2, g2, m2, v2 = _as2d(w), _as2d(g), _as2d(m), _as2d(v)
    r, c = w2.shape
    tr = _row_tile(r, c, 256 * 1024)
    c1 = 1.0 - ADAM_B1 ** ADAM_STEP
    c2 = 1.0 - ADAM_B2 ** ADAM_STEP

    def body(w_ref, g_ref, m_ref, v_ref, d_ref, mo_ref, vo_ref):
        gv = g_ref[...]
        mn = ADAM_B1 * m_ref[...] + (1.0 - ADAM_B1) * gv
        vn = ADAM_B2 * v_ref[...] + (1.0 - ADAM_B2) * (gv * gv)
        mo_ref[...] = mn
        vo_ref[...] = vn
        d_ref[...] = -ADAM_LR * ((mn / c1) / (jnp.sqrt(vn / c2) + ADAM_EPS) + ADAM_WD * w_ref[...])

    spec = pl.BlockSpec((tr, c), lambda i: (i, 0))
    outs = pl.pallas_call(body, name=name, grid=(r // tr,), in_specs=[spec] * 4, out_specs=[spec] * 3,
                          out_shape=[S_((r, c), f32)] * 3, compiler_params=_cp(VMEM_MID, ("arbitrary",)))(w2, g2, m2, v2)
    return tuple(o.reshape(shape) for o in outs)


def _sum_devices(gathered, name):
    _, r, c = gathered.shape

    def body(a_ref, o_ref):
        acc = a_ref[0]
        for j in range(1, N_DEV):
            acc = acc + a_ref[j]
        o_ref[...] = acc

    tr = _row_tile(r, c, 64 * 1024)
    return pl.pallas_call(
        body, name=name, grid=(r // tr,),
        in_specs=[pl.BlockSpec((N_DEV, tr, c), lambda i: (0, i, 0))], out_specs=pl.BlockSpec((tr, c), lambda i: (i, 0)),
        out_shape=S_((r, c), f32), compiler_params=_cp(VMEM_MID, ("arbitrary",)))(gathered)


def _all_gather_small(block, name):
    m_per, n = block.shape

    def body(x_ref, out_ref, send_sems, recv_sems, local_sem):
        x, y, c = _mesh_pos()
        me, sibling = (x, y, c), (x, y, 1 - c)
        chips = [(1 - x, y), (x, 1 - y), (1 - x, 1 - y)]

        def rows(px, py, pc):
            return out_ref.at[pl.ds((4 * px + 2 * py + pc) * m_per, m_per), :]

        def copy(k, blk, to, src=None):
            return pltpu.make_async_remote_copy(
                src_ref=rows(*blk) if src is None else src, dst_ref=rows(*blk),
                send_sem=send_sems.at[k], recv_sem=recv_sems.at[k], device_id=to, device_id_type=MESH)

        mine = pltpu.make_async_copy(x_ref, rows(*me), local_sem)
        mine.start()
        first = [copy(0, me, sibling, src=x_ref)]
        first += [copy(1 + j, me, (*chip, c), src=x_ref) for j, chip in enumerate(chips)]
        for cp in first:
            cp.start()
        passed = [copy(4 + j, (*chip, c), sibling) for j, chip in enumerate(chips)]
        for j, chip in enumerate(chips):
            copy(1 + j, (*chip, c), me).wait_recv()
            passed[j].start()
        copy(0, sibling, me).wait_recv()
        for j, chip in enumerate(chips):
            copy(4 + j, (*chip, 1 - c), me).wait_recv()
        for cp in first + passed:
            cp.wait_send()
        mine.wait()

    return pl.pallas_call(
        body, name=name, out_shape=S_((N_DEV * m_per, n), block.dtype),
        in_specs=[pl.BlockSpec(memory_space=pltpu.VMEM)], out_specs=pl.BlockSpec(memory_space=pltpu.VMEM),
        scratch_shapes=[pltpu.SemaphoreType.DMA((7,)), pltpu.SemaphoreType.DMA((7,)), pltpu.SemaphoreType.DMA],
        compiler_params=_cp(VMEM_MID),
    )(block)


def _pack_rows(arrays):
    flat = jnp.concatenate([a.reshape(-1) for a in arrays])
    pad = (-flat.size) % (8 * LANES)
    return jnp.pad(flat, (0, pad)).reshape(-1, LANES)


def _unpack_rows(packed, shapes):
    flat = packed.reshape(-1)
    out, off = [], 0
    for s in shapes:
        n = int(np.prod(s))
        out.append(flat[off:off + n].reshape(s))
        off += n
    return out


W_AXIS = {"gu": 1, "dn": 0, "wi": 1, "wo": 0}


def _half_merge(bufs, name):
    nt = len(bufs)

    def body(*refs):
        outs = refs[nt:2 * nt]
        send_sems, recv_sems = refs[2 * nt:]
        x, y, c = _mesh_pos()

        def half(ref, h):
            rh = ref.shape[-2] // 2
            return ref.at[(slice(None),) * (len(ref.shape) - 2) + (pl.ds(h * rh, rh), slice(None))]

        cps = []
        for t in range(nt):
            cp = pltpu.make_async_remote_copy(
                src_ref=half(outs[t], c), dst_ref=half(outs[t], c), send_sem=send_sems.at[t], recv_sem=recv_sems.at[t],
                device_id=(x, y, 1 - c), device_id_type=MESH)
            cp.start()
            cps.append(cp)
        for t in range(nt):
            pltpu.make_async_remote_copy(
                src_ref=half(outs[t], 1 - c), dst_ref=half(outs[t], 1 - c), send_sem=send_sems.at[t], recv_sem=recv_sems.at[t],
                device_id=(x, y, 1 - c), device_id_type=MESH).wait_recv()
        for cp in cps:
            cp.wait_send()

    return pl.pallas_call(
        body, name=name, in_specs=[ANY] * nt, out_specs=[ANY] * nt, out_shape=[S_(b.shape, f32) for b in bufs],
        input_output_aliases={t: t for t in range(nt)},
        scratch_shapes=[pltpu.SemaphoreType.DMA((nt,)), pltpu.SemaphoreType.DMA((nt,))],
        compiler_params=_cp(VMEM_MID),
    )(*bufs)


def _local_step(cfg, xs0, target, mods, norm_g, W, G, dck, na_rpb, w_pool, pool_scale):
    S, L, T, D, F = cfg.S, cfg.L, cfg.T, cfg.D, cfg.F
    depth = norm_g.shape[0]
    cos, sin = _rope_tables(S, L)
    band, inv = _pool_tables(cfg.TM, L)
    onehot, sel = _rpb_reduce_tables()

    assert depth == 2, "the carrier schedules below are written for two layers"
    fwd_carry = {"ffn_fwd_0_0": [("wi", 0), ("wo", 0), ("gu", 0, 1), ("dn", 0, 1)],
                 "na_fwd_0": [("gu", 1, 0), ("dn", 1, 0)],
                 "ffn_fwd_0_1": [("wi", 1), ("wo", 1), ("gu", 1, 1), ("dn", 1, 1)]}
    bwd_carry = {"na_bwd_1": [("gu", 1, 1), ("dn", 1, 1)], "ffn_bwd_1_0": [("wi", 1), ("wo", 1)],
                 "ffn_bwd_0_1": [("gu", 1, 0), ("dn", 1, 0)], "na_bwd_0": [("gu", 0, 1), ("dn", 0, 1)],
                 "ffn_bwd_0_0": [("wi", 0), ("wo", 0)]}
    last_scatter = [("gu", 0, 0), ("dn", 0, 0)]
    tag = lambda key: "_".join(str(p) for p in key)
    g_f32, g_b16 = {}, {}

    def gather_on(name):
        keys = fwd_carry.get(name)
        return None if keys is None else _gather_comm([W[k_] for k_ in keys], [W_AXIS[k_[0]] for k_ in keys])

    def gathered(name, res):
        if name in fwd_carry:
            W.update(zip(fwd_carry[name], res))

    def scatter_on(name):
        keys = bwd_carry.get(name)
        return None if keys is None else _scatter_comm([g_b16[k_] for k_ in keys], [W_AXIS[k_[0]] for k_ in keys])

    def scattered(keys, lands):
        for key, land in zip(keys, lands):
            G[key[0]] = _sum_devices8(g_f32[key], land, W_AXIS[key[0]], G[key[0]], key[1:], dck, f"sum8_{tag(key)}")

    saved = []
    xs = xs0
    for l in range(depth):
        last = l == depth - 1
        wc = not last
        gvec = norm_g[l]
        ps = pool_scale[l].reshape(1, POOL_WIDTH)
        bexp = _expand_rpb(na_rpb[l])
        name = f"ffn_fwd_{l}_0"
        (xs1, hb1, z1, y1), res = _ffn_fwd(cfg, xs, mods[l], gvec, W["gu", l, 0], W["dn", l, 0], 0, 0, True, name, gather_on(name))
        gathered(name, res)
        hb2, q, k, v, u = _tmpre_fwd(cfg, xs1, mods[l], gvec, W["wi", l], cos, sin, f"tmpre_fwd_{l}")
        name = f"na_fwd_{l}"
        (na_x,), res = _na_fwd(cfg, q, k, v, bexp, name, gather_on(name))
        gathered(name, res)
        na_c = _ctx_attn_fwd(cfg, q, k, v, f"ctx_attn_fwd_{l}") if wc else None
        pool = _pool_fwd(cfg, u, band, inv, w_pool[l], ps, wc, f"pool_fwd_{l}")
        xs2, opre, mix = _tmpost_fwd(cfg, na_x, na_c, pool, W["wo", l], xs1, mods[l], gvec, f"tmpost_fwd_{l}")
        name = f"ffn_fwd_{l}_1"
        (xs3, hb3, z3, y3), res = _ffn_fwd(cfg, xs2, mods[l], gvec, W["gu", l, 1], W["dn", l, 1], 6, 4, wc, name, gather_on(name))
        gathered(name, res)
        saved.append(dict(xs=xs, xs1=xs1, xs2=xs2, hb1=hb1, z1=z1, y1=y1, hb2=hb2, q=q, k=k, v=v, u=u, mix=mix,
                          opre=opre, hb3=hb3, z3=z3, y3=y3, bexp=bexp, ps=ps, gvec=gvec))
        xs = xs3

    dxs, loss_blk = _loss_head(cfg, xs, target, "loss_head")

    small = [None] * depth
    for l in reversed(range(depth)):
        last = l == depth - 1
        wc = not last
        sv = saved[l]
        gvec = sv["gvec"]
        rows_b = cfg.T if wc else cfg.S
        name = f"ffn_bwd_{l}_1"
        (dxs2, dz, dyb, ab, dm678, dg45), lands = _ffn_bwd(cfg, dxs, sv["xs2"], sv["z3"], sv["y3"], mods[l], gvec,
                                                           W["gu", l, 1], W["dn", l, 1], 6, 4, wc, name, scatter_on(name))
        scattered(bwd_carry.get(name, ()), lands)
        g_f32["gu", l, 1], g_b16["gu", l, 1] = _wgrad(sv["hb3"], dz, rows_b, f"wgrad_gu_{l}_1")
        g_f32["dn", l, 1], g_b16["dn", l, 1] = _wgrad(ab, dyb, rows_b, f"wgrad_dn_{l}_1")
        dop, dmix, dm5, dg3 = _tmpost_bwd(cfg, dxs2, sv["opre"], W["wo", l], mods[l], gvec, wc, f"tmpost_bwd_{l}")
        g_f32["wo", l], g_b16["wo", l] = _wgrad(sv["mix"], dop, rows_b, f"wgrad_wo_{l}")
        du, dwp, dps = _pool_bwd(cfg, dmix, sv["u"], band, inv, w_pool[l], sv["ps"], wc, f"pool_bwd_{l}")
        name = f"na_bwd_{l}"
        (dq, dk, dv, dkc, dvc, dbexp), lands = _na_bwd(cfg, dmix, sv["q"], sv["k"], sv["v"], sv["bexp"], name, scatter_on(name))
        scattered(bwd_carry.get(name, ()), lands)
        drpb = _rpb_reduce(dbexp, onehot, sel, f"rpb_reduce_{l}")
        if wc:
            dqc, dkc2, dvc2 = _ctx_attn_bwd(cfg, dmix, sv["q"], sv["k"], sv["v"], f"ctx_attn_bwd_{l}")
            ctx_terms = ([dqc], [dkc, dkc2], [dvc, dvc2])
        else:
            ctx_terms = ([], [dkc], [dvc])
        dproj = _rope_bwd_assemble(cfg, (dq, dk, dv, du), ctx_terms, wc, cos, sin, f"rope_bwd_{l}")
        g_f32["wi", l], g_b16["wi", l] = _wgrad(sv["hb2"], dproj, cfg.T, f"wgrad_wi_{l}")
        dxs1, dm34, dg2 = _tmpre_bwd(cfg, dproj, W["wi", l], sv["xs1"], mods[l], gvec, dxs2, wc, f"tmpre_bwd_{l}")
        name = f"ffn_bwd_{l}_0"
        (dxs, dz, dyb, ab, dm012, dg01), lands = _ffn_bwd(cfg, dxs1, sv["xs"], sv["z1"], sv["y1"], mods[l], gvec,
                                                          W["gu", l, 0], W["dn", l, 0], 0, 0, True, name, scatter_on(name))
        scattered(bwd_carry.get(name, ()), lands)
        g_f32["gu", l, 0], g_b16["gu", l, 0] = _wgrad(sv["hb1"], dz, cfg.T, f"wgrad_gu_{l}_0")
        g_f32["dn", l, 0], g_b16["dn", l, 0] = _wgrad(ab, dyb, cfg.T, f"wgrad_dn_{l}_0")
        if not wc:
            zero = lambda a: jnp.concatenate([a, jnp.zeros_like(a)], axis=0)
            dm5, dm678 = zero(dm5), zero(dm678)
        dmods = jnp.concatenate([dm012, dm34, dm5, dm678], axis=1)
        dgs = jnp.concatenate([dg01, dg2, dg3, dg45], axis=0)
        small[l] = dict(dmods=dmods, dg=dgs, drpb=drpb, dwp=dwp, dps=dps)
    lands = _comm_only(_scatter_comm([g_b16[k_] for k_ in last_scatter], [W_AXIS[k_[0]] for k_ in last_scatter]), "scatter_last")
    scattered(last_scatter, lands)
    kinds = ("gu", "dn", "wi", "wo")
    merged = _half_merge([G[k_] for k_ in kinds], "merge_halves")
    return loss_blk, dxs, dict(zip(kinds, merged)), small


def kernel(x, c, ctx, c_ctx, w_mod, b_mod, norm_g, w_ffn_gate_up, w_ffn_down, w_in, w_out, na_rpb, w_pool, pool_scale, loss_target, m_c_ctx, m_w_mod, m_b_mod, m_norm_g, m_w_ffn_gate_up, m_w_ffn_down, m_w_in, m_w_out, m_na_rpb, m_w_pool, m_pool_scale, v_c_ctx, v_w_mod, v_b_mod, v_norm_g, v_w_ffn_gate_up, v_w_ffn_down, v_w_in, v_w_out, v_na_rpb, v_w_pool, v_pool_scale):
    S, D = x.shape[1], x.shape[2]
    L = ctx.shape[1]
    depth = w_mod.shape[0]
    F = w_ffn_down.shape[2] * N_CHIPS
    nmod = w_mod.shape[2]
    gsh = norm_g.shape[2]
    cfg = _Cfg(S, L, D, F)
    mx, my, mc = _mesh_pos()
    chip = 2 * mx + my
    dev = 4 * mx + 2 * my + mc

    kidx = chip.astype(jnp.int32).reshape(1)
    dck = jnp.stack([dev, mc, chip]).astype(jnp.int32)
    W = {}
    for l in range(depth):
        for i in range(2):
            W["gu", l, i] = _cast_into_place(w_ffn_gate_up, (l, i), W_AXIS["gu"], kidx, f"cast_gu_{l}_{i}")
            W["dn", l, i] = _cast_into_place(w_ffn_down, (l, i), W_AXIS["dn"], kidx, f"cast_dn_{l}_{i}")
        W["wi", l] = _cast_into_place(w_in, (l,), W_AXIS["wi"], kidx, f"cast_wi_{l}")
        W["wo", l] = _cast_into_place(w_out, (l,), W_AXIS["wo"], kidx, f"cast_wo_{l}")
    first = [("gu", 0, 0), ("dn", 0, 0)]
    W.update(zip(first, _comm_only(_gather_comm([W[k_] for k_ in first], [W_AXIS[k_[0]] for k_ in first]), "gather_first")))
    G = {"gu": lax.empty(w_ffn_gate_up.shape, f32), "dn": lax.empty(w_ffn_down.shape, f32),
         "wi": lax.empty(w_in.shape, f32), "wo": lax.empty(w_out.shape, f32)}

    c_all = _all_gather_small(jnp.pad(c, ((0, 7), (0, 0))), "gather_c").reshape(N_DEV, 8, D)[:, 0]
    cvecs = jnp.concatenate([c_all, c_ctx[None], jnp.zeros((7, D), f32)], axis=0)
    b_shard = lax.dynamic_slice_in_dim(b_mod, chip * nmod, nmod, axis=1).reshape(depth, 1, nmod)
    m_part, silu_c = _modvec_fwd(cvecs, w_mod, b_shard, "modvec_fwd")
    m_all = _all_gather_small(m_part.reshape(depth * 16, nmod), "gather_mod").reshape(N_DEV, depth, 16, nmod)
    m_full = jnp.concatenate([m_all[2 * j] for j in range(N_CHIPS)], axis=-1)
    m_mine = lax.dynamic_index_in_dim(m_full, dev, axis=1, keepdims=False)
    mods = jnp.stack([m_mine, m_full[:, 8]], axis=1).reshape(depth, 2, N_MOD, D)

    norm_g_full = _all_gather_small(_pack_rows([norm_g]), "gather_norm_g")
    rows_g = norm_g_full.shape[0] // N_DEV
    ng = norm_g_full.reshape(N_DEV, rows_g * LANES)[:, :norm_g.size].reshape(N_DEV, depth, 6, gsh)
    norm_g_all = jnp.concatenate([ng[2 * j] for j in range(N_CHIPS)], axis=-1)
    xs0 = jnp.concatenate([x[0], ctx[0]], axis=0)
    loss_blk, dxs0, wgrads, small = _local_step(cfg, xs0, loss_target[0], mods, norm_g_all, W, G, dck,
                                                na_rpb, w_pool, pool_scale)
    loss = lax.psum(loss_blk[0, 0], ("x", "y", "c"))
    grad_x = dxs0[:S][None]

    g_gu, g_dn, g_wi, g_wo = wgrads["gu"], wgrads["dn"], wgrads["wi"], wgrads["wo"]
    names = ("dmods", "dg", "drpb", "dwp", "dps")
    parts = [jnp.stack([small[l][n] for l in range(depth)]) for n in names]
    shapes = [p.shape for p in parts]
    packed = _pack_rows(parts)
    gathered = _all_gather_small(packed, "gather_small").reshape(N_DEV, packed.shape[0], LANES)
    total = _unpack_rows(_sum_devices(gathered, "sum_small"), shapes)
    dmods_sum, dg_sum, drpb_sum, dwp_sum, dps_sum = total
    dmods_each = jnp.stack([_unpack_rows(gathered[j], shapes[:1])[0] for j in range(N_DEV)])
    dm_rows = jnp.concatenate([jnp.transpose(dmods_each[:, :, 0], (1, 0, 2, 3)).reshape(depth, N_DEV, N_MOD * D),
                               dmods_sum[:, 1].reshape(depth, 1, N_MOD * D),
                               jnp.zeros((depth, 7, N_MOD * D), f32)], axis=1)
    dm_shard = lax.dynamic_slice_in_dim(dm_rows, chip * nmod, nmod, axis=2)
    grad_w_mod, gc_part = _modvec_bwd(silu_c.T, dm_shard, w_mod, "modvec_bwd")
    gc_all = _all_gather_small(gc_part.reshape(depth * 8, D), "gather_gc").reshape(N_DEV, depth, 8, D)
    grad_b_mod, grad_c_ctx = _small_finish(dm_rows, gc_all, c_ctx)
    grad_norm_g = lax.dynamic_slice_in_dim(dg_sum, chip * gsh, gsh, axis=2)
    grad_na_rpb = drpb_sum[:, :, :2 * NA_KH - 1, :2 * NA_KW - 1]
    grad_w_pool = dwp_sum
    grad_pool_scale = dps_sum.reshape(depth, POOL_WIDTH)

    grads = [grad_c_ctx, grad_w_mod, grad_b_mod, grad_norm_g, g_gu, g_dn, g_wi, g_wo, grad_na_rpb, grad_w_pool, grad_pool_scale]
    ws = [c_ctx, w_mod, b_mod, norm_g, w_ffn_gate_up, w_ffn_down, w_in, w_out, na_rpb, w_pool, pool_scale]
    ms = [m_c_ctx, m_w_mod, m_b_mod, m_norm_g, m_w_ffn_gate_up, m_w_ffn_down, m_w_in, m_w_out, m_na_rpb, m_w_pool, m_pool_scale]
    vs = [v_c_ctx, v_w_mod, v_b_mod, v_norm_g, v_w_ffn_gate_up, v_w_ffn_down, v_w_in, v_w_out, v_na_rpb, v_w_pool, v_pool_scale]
    tags = ["c_ctx", "w_mod", "b_mod", "norm_g", "gate_up", "down", "w_in", "w_out", "na_rpb", "w_pool", "pool_scale"]
    upd = [_adamw(w_, g_, m_, v_, f"adamw_{t}") for w_, g_, m_, v_, t in zip(ws, grads, ms, vs, tags)]
    return (loss, grad_x, *grads, *[u_[0] for u_ in upd], *[u_[1] for u_ in upd], *[u_[2] for u_ in upd])


def _small_finish(dm_rows, gc_all, c_ctx):
    depth, _, n = dm_rows.shape
    D = c_ctx.shape[0]

    def body(dm_ref, gc_ref, c_ref, gb_ref, gcx_ref):
        acc = dm_ref[:, 0]
        for j in range(1, N_DEV + 1):
            acc = acc + dm_ref[:, j]
        gb_ref[...] = acc
        t = jnp.zeros((1, D), f32)
        for l in range(depth):
            for j in range(N_CHIPS):
                t = t + gc_ref[2 * j, l, 0:1, :]
        cv = c_ref[...]
        sg = _sigmoid(cv)
        gcx_ref[...] = t * (sg * (1.0 + cv * (1.0 - sg)))

    gb, gcx = pl.pallas_call(
        body, name="small_finish",
        out_shape=[S_((depth, n), f32), S_((1, D), f32)],
        compiler_params=_cp(VMEM_MID),
    )(dm_rows, gc_all, c_ctx.reshape(1, D))
    return gb, gcx.reshape(D)
```

```python
import functools

import numpy as np
import jax
import jax.numpy as jnp
from jax import lax
from jax.experimental import pallas as pl
from jax.experimental.pallas import tpu as pltpu

f32, bf16 = jnp.float32, jnp.bfloat16

GRID_W = 64
N_MOD = 9
NA_HEADS = 8
HEAD_DIM = 64
NA_WIDTH = NA_HEADS * HEAD_DIM
NA_KH = 8
NA_KW = 16
POOL_GROUPS = 4
POOL_CH = 128
POOL_WIDTH = POOL_GROUPS * POOL_CH
POOL_WINDOWS = (2, 4, 8, 16)
IN_WIDTH = 3 * NA_WIDTH + POOL_WIDTH
MIX_WIDTH = NA_WIDTH + POOL_WIDTH
ROPE_THETA = 10000.0
ROPE_PAIRS = HEAD_DIM // 4
RMS_EPS = 1e-6
NEG_INF = -1e30
ADAM_LR, ADAM_B1, ADAM_B2, ADAM_EPS, ADAM_WD, ADAM_STEP = 0.001, 0.9, 0.999, 1e-08, 0.01, 10

N_DEV = 8
N_CHIPS = 4
LANES = 128
MIB = 1024 * 1024
VMEM_BIG = 52 * MIB
VMEM_MID = 40 * MIB
MESH = pl.DeviceIdType.MESH
ANY = pl.BlockSpec(memory_space=pl.ANY)
S_ = jax.ShapeDtypeStruct


def _cp(vmem=VMEM_MID, sem=None):
    return pltpu.CompilerParams(vmem_limit_bytes=vmem, dimension_semantics=sem)


def _sigmoid(x):
    return 1.0 / (1.0 + jnp.exp(-x))


def _rms_hat(x):
    rinv = lax.rsqrt(jnp.mean(x * x, axis=-1, keepdims=True) + RMS_EPS)
    return x * rinv, rinv


def _rms_bwd(dxhat, xhat, rinv):
    return rinv * (dxhat - xhat * jnp.mean(dxhat * xhat, axis=-1, keepdims=True))


def _rsum(a):
    return jnp.sum(a, axis=0, keepdims=True)


def _nt(a, b):
    return lax.dot_general(a, b, (((1,), (1,)), ((), ())), preferred_element_type=f32)


def _tn(a, b):
    return lax.dot_general(a, b, (((0,), (0,)), ((), ())), preferred_element_type=f32)


def _nn(a, b):
    return jnp.dot(a, b, preferred_element_type=f32)


def _swap16(x):
    lane = lax.broadcasted_iota(jnp.int32, x.shape, 1)
    n = x.shape[1]
    return jnp.where((lane % 32) < 16, pltpu.roll(x, n - 16, 1), pltpu.roll(x, 16, 1))


def _rope_tables(s_len, l_len):
    t = np.arange(s_len)
    inv = ROPE_THETA ** (-np.arange(ROPE_PAIRS, dtype=np.float32) / ROPE_PAIRS)
    ang_r = (t // GRID_W).astype(np.float32)[:, None] * inv
    ang_c = (t % GRID_W).astype(np.float32)[:, None] * inv
    cos = np.concatenate([np.cos(ang_r), np.cos(ang_r), np.cos(ang_c), np.cos(ang_c)], axis=-1)
    sin = np.concatenate([-np.sin(ang_r), np.sin(ang_r), -np.sin(ang_c), np.sin(ang_c)], axis=-1)
    cos = np.concatenate([cos, np.ones((l_len, HEAD_DIM), np.float32)], axis=0)
    sin = np.concatenate([sin, np.zeros((l_len, HEAD_DIM), np.float32)], axis=0)
    return (jnp.asarray(np.tile(cos, (1, 2)), f32), jnp.asarray(np.tile(sin, (1, 2)), f32))


def _pool_tables(tm, l_len):
    band = np.zeros((2, POOL_GROUPS, tm, tm), np.float32)
    inv = np.zeros((2, POOL_GROUPS, tm, 1), np.float32)
    for typ, length in ((0, GRID_W), (1, l_len)):
        for g, w in enumerate(POOL_WINDOWS):
            for t in range(tm):
                base, p = (t // length) * length, t % length
                lo = min(max(p - w // 2, 0), length)
                hi = min(max(p - w // 2 + w, 0), length)
                band[typ, g, t, base + lo:base + hi] = 1.0
                inv[typ, g, t, 0] = 1.0 / (hi - lo)
    return jnp.asarray(band, bf16), jnp.asarray(inv, f32)


NA_QR = 4
NA_WR = NA_KH + NA_QR - 1
NA_TYPES = 3
NA_SEL_ROWS = 136
NA_WPAD = 768


def _rpb_index_tables():
    j = np.arange(GRID_W)
    col_start = np.clip(j - NA_KW // 2, 0, GRID_W - NA_KW)
    valid = (j[None, :] >= col_start[:, None]) & (j[None, :] < col_start[:, None] + NA_KW)
    dc = np.clip(j[None, :] - j[:, None] + NA_KW - 1, 0, 2 * NA_KW - 2)
    i = np.arange(NA_QR)[:, None]
    kk = np.arange(NA_WR)[None, :]
    off = np.stack([np.zeros_like(i), i, np.full_like(i, NA_QR - 1)])
    d = np.stack([kk - i + NA_KH - 1, kk - i + NA_KH - 1 - NA_QR, kk - i])
    row_ok = (kk[None] >= off) & (kk[None] < off + NA_KH)
    assert (d[row_ok] >= 0).all() and (d[row_ok] <= 2 * NA_KH - 2).all()
    return valid, dc, d, row_ok


def _expand_rpb(rpb):
    valid, _, d, row_ok = _rpb_index_tables()
    heads, nd, ne = rpb.shape
    w = GRID_W
    v = jnp.pad(rpb, ((0, 0), (0, 0), (w - NA_KW, 2 * w - (w - NA_KW) - ne)))
    x = jnp.broadcast_to(v[:, :, None, :], (heads, nd, w, 2 * w)).reshape(heads, nd, 2 * w * w)
    t = x[:, :, :w * (2 * w - 1)].reshape(heads, nd, w, 2 * w - 1)[..., w - 1:]
    lo, hi = int(d.min()), int(d.max())
    tp = jnp.pad(t, ((0, 0), (max(0, -lo), max(0, hi - nd + 1)), (0, 0), (0, 0)))
    types = []
    for typ in range(NA_TYPES):
        rows_i = [tp[:, d[typ, i, 0] - min(lo, 0):d[typ, i, 0] - min(lo, 0) + NA_WR] for i in range(NA_QR)]
        types.append(jnp.stack(rows_i, axis=1))
    b = jnp.stack(types)
    ok = row_ok[:, None, :, :, None, None] & valid[None, None, None, None]
    b = jnp.where(ok, b, NEG_INF)
    b = jnp.transpose(b, (0, 1, 2, 4, 3, 5))
    return b.reshape(NA_TYPES, NA_HEADS, NA_QR * GRID_W, NA_WR * GRID_W)


def _rpb_reduce_tables():
    _, _, d, row_ok = _rpb_index_tables()
    flip = np.eye(GRID_W, dtype=np.float32)[::-1].copy()
    sel = np.zeros((16, NA_SEL_ROWS), np.float32)
    flat_d, flat_ok = d.reshape(-1), row_ok.reshape(-1)
    for n in range(flat_d.size):
        if flat_ok[n]:
            sel[flat_d[n], n] = 1.0
    return jnp.asarray(flip), jnp.asarray(sel)


class _Cfg:
    def __init__(self, s_len, l_len, d, f):
        self.S, self.L, self.D, self.F = s_len, l_len, d, f
        self.T = s_len + l_len
        self.TM = 256 if l_len % 256 == 0 else 128
        assert l_len == self.TM, "context length must equal the row tile"
        assert s_len % self.TM == 0 and s_len % GRID_W == 0
        self.nxt = s_len // self.TM
        self.ntt = self.T // self.TM
        self.rows = s_len // GRID_W
        assert self.rows >= 2 * NA_KH
        assert f % (2 * LANES) == 0
        self.FC = f // 2

    def ntiles(self, with_ctx):
        return self.ntt if with_ctx else self.nxt


def _typ(cfg):
    return lambda i: (jnp.minimum(i // cfg.nxt, 1), 0, 0)


def _mesh_pos():
    return lax.axis_index("x"), lax.axis_index("y"), lax.axis_index("c")


class _Comm:
    def __init__(self, ins, outs, alias, nsem, start, finish):
        self.ins, self.outs, self.alias, self.nsem, self.start, self.finish = ins, outs, alias, nsem, start, finish


def _call(body, args, comm=None, *, grid, in_specs, out_specs, out_shape, scratch_shapes=(), **kw):
    if comm is None:
        return pl.pallas_call(body, grid=grid, in_specs=list(in_specs), out_specs=list(out_specs), out_shape=list(out_shape),
                              scratch_shapes=list(scratch_shapes), **kw)(*args), ()
    n_in, n_out, n_sc = len(in_specs), len(out_specs), len(scratch_shapes)
    ci, co = len(comm.ins), len(comm.outs)

    def carrier(*refs):
        bounds = np.cumsum([0, n_in, ci, n_out, co, n_sc])
        ins, cins, outs, couts, scr = (refs[a:b] for a, b in zip(bounds[:-1], bounds[1:]))
        send, recv = refs[bounds[-1]], refs[bounds[-1] + 1]
        first = functools.reduce(jnp.logical_and, [pl.program_id(a) == 0 for a in range(len(grid))])
        last = functools.reduce(jnp.logical_and, [pl.program_id(a) == g - 1 for a, g in enumerate(grid)])

        @pl.when(first)
        def _():
            comm.start(cins, couts, send, recv)

        body(*ins, *outs, *scr)

        @pl.when(last)
        def _():
            comm.finish(cins, couts, send, recv)

    res = pl.pallas_call(
        carrier, grid=grid, in_specs=list(in_specs) + [ANY] * ci, out_specs=list(out_specs) + [ANY] * co,
        out_shape=list(out_shape) + list(comm.outs),
        input_output_aliases={n_in + a: n_out + b for a, b in comm.alias.items()},
        scratch_shapes=list(scratch_shapes) + [pltpu.SemaphoreType.DMA((comm.nsem,)), pltpu.SemaphoreType.DMA((comm.nsem,))],
        **kw)(*args, *comm.ins)
    return res[:n_out], res[n_out:]


def _comm_only(comm, name):
    ci, co = len(comm.ins), len(comm.outs)

    def body(*refs):
        cins, couts = refs[:ci], refs[ci:ci + co]
        send, recv = refs[ci + co], refs[ci + co + 1]
        comm.start(cins, couts, send, recv)
        comm.finish(cins, couts, send, recv)

    return pl.pallas_call(
        body, name=name, in_specs=[ANY] * ci, out_specs=[ANY] * co, out_shape=list(comm.outs),
        input_output_aliases=dict(comm.alias),
        scratch_shapes=[pltpu.SemaphoreType.DMA((comm.nsem,)), pltpu.SemaphoreType.DMA((comm.nsem,))],
        compiler_params=_cp(VMEM_MID),
    )(*comm.ins)


def _half_view(ref, axis, kk, h):
    r, c = ref.shape
    if axis == 1:
        n = c // N_CHIPS
        return ref.at[pl.ds(h * (r // 2), r // 2), pl.ds(pl.multiple_of(kk * n, LANES), n)]
    n = r // N_CHIPS
    return ref.at[pl.ds(pl.multiple_of(kk * n + h * (n // 2), 8), n // 2), :]


def _other_chips(x, y):
    return [(1 - x, y), (x, 1 - y), (1 - x, 1 - y)]


def _gather_comm(arrs, axes):
    n = len(arrs)

    def copy(ref, view, sems, k, to):
        send, recv = sems
        return pltpu.make_async_remote_copy(src_ref=view, dst_ref=view, send_sem=send.at[k], recv_sem=recv.at[k],
                                            device_id=to, device_id_type=MESH)

    def start(cins, bufs, send, recv):
        x, y, c = _mesh_pos()
        for t in range(n):
            own = _half_view(bufs[t], axes[t], 2 * x + y, c)
            for j, chip in enumerate(_other_chips(x, y)):
                copy(bufs[t], own, (send, recv), 6 * t + j, (*chip, c)).start()

    def finish(cins, bufs, send, recv):
        x, y, c = _mesh_pos()
        sibling = (x, y, 1 - c)
        chips = _other_chips(x, y)
        for t in range(n):
            for j, chip in enumerate(chips):
                landed = _half_view(bufs[t], axes[t], 2 * chip[0] + chip[1], c)
                copy(bufs[t], landed, (send, recv), 6 * t + j, (*chip, c)).wait_recv()
                copy(bufs[t], landed, (send, recv), 6 * t + 3 + j, sibling).start()
        for t in range(n):
            own = _half_view(bufs[t], axes[t], 2 * x + y, c)
            for j, chip in enumerate(chips):
                kj = 2 * chip[0] + chip[1]
                copy(bufs[t], _half_view(bufs[t], axes[t], kj, 1 - c), (send, recv), 6 * t + 3 + j, sibling).wait_recv()
                copy(bufs[t], own, (send, recv), 6 * t + j, (*chip, c)).wait_send()
                copy(bufs[t], _half_view(bufs[t], axes[t], kj, c), (send, recv), 6 * t + 3 + j, sibling).wait_send()

    return _Comm(list(arrs), [S_(a.shape, a.dtype) for a in arrs], {t: t for t in range(n)}, 6 * n, start, finish)


def _scatter_comm(parts, axes):
    n = len(parts)
    peers = [(fx, fy, fc) for fx in (0, 1) for fy in (0, 1) for fc in (0, 1)][1:]

    def half_shape(a, axis):
        r, c = a.shape
        return (r // 2, c // N_CHIPS) if axis == 1 else (r // N_CHIPS // 2, c)

    def start(srcs, lands, send, recv):
        x, y, c = _mesh_pos()
        me = 4 * x + 2 * y + c
        for t in range(n):
            for r_, (fx, fy, fc) in enumerate(peers):
                dx, dy, dc = (1 - x if fx else x), (1 - y if fy else y), (1 - c if fc else c)
                pltpu.make_async_remote_copy(
                    src_ref=_half_view(srcs[t], axes[t], 2 * dx + dy, dc), dst_ref=lands[t].at[me],
                    send_sem=send.at[7 * t + r_], recv_sem=recv.at[7 * t + r_],
                    device_id=(dx, dy, dc), device_id_type=MESH).start()

    def finish(srcs, lands, send, recv):
        x, y, c = _mesh_pos()
        for t in range(n):
            mine = _half_view(srcs[t], axes[t], 2 * x + y, c)
            for r_, (fx, fy, fc) in enumerate(peers):
                sx, sy, sc = (1 - x if fx else x), (1 - y if fy else y), (1 - c if fc else c)
                cp = pltpu.make_async_remote_copy(
                    src_ref=mine, dst_ref=lands[t].at[4 * sx + 2 * sy + sc],
                    send_sem=send.at[7 * t + r_], recv_sem=recv.at[7 * t + r_],
                    device_id=(sx, sy, sc), device_id_type=MESH)
                cp.wait_recv()
                cp.wait_send()

    return _Comm(list(parts), [S_((N_DEV,) + half_shape(a, ax), a.dtype) for a, ax in zip(parts, axes)], {}, 7 * n, start, finish)


def _ffn_fwd(cfg, xs, mods, gvec, wgu, wd, mi, gi, with_ctx, name, comm=None):
    TM, D, F, FC = cfg.TM, cfg.D, cfg.F, cfg.FC
    nt = cfg.ntiles(with_ctx)
    R = nt * TM

    def body(xs_ref, mods_ref, g_ref, wgu_hbm, wd_hbm, out_ref, hb_ref, z_ref, y_ref, wgu_v, wd_v, sem):
        @pl.when(pl.program_id(0) == 0)
        def _():
            c0 = pltpu.make_async_copy(wgu_hbm, wgu_v, sem.at[0])
            c1 = pltpu.make_async_copy(wd_hbm, wd_v, sem.at[1])
            c0.start(); c1.start(); c0.wait(); c1.wait()
        x = xs_ref[...]
        m = mods_ref[0]
        sh, sc, gt = m[mi:mi + 1], m[mi + 1:mi + 2], m[mi + 2:mi + 3]
        xhat, _ = _rms_hat(x)
        h = (xhat * g_ref[gi:gi + 1]) * (1.0 + sc) + sh
        hb = h.astype(bf16)
        hb_ref[...] = hb
        y = jnp.zeros((TM, D), f32)
        for ch in range(F // FC):
            zg = _nn(hb, wgu_v[:, ch * FC:(ch + 1) * FC])
            zu = _nn(hb, wgu_v[:, F + ch * FC:F + (ch + 1) * FC])
            z_ref[:, ch * FC:(ch + 1) * FC] = zg.astype(bf16)
            z_ref[:, F + ch * FC:F + (ch + 1) * FC] = zu.astype(bf16)
            a = (zg * _sigmoid(zg)) * zu
            y = y + _nn(a.astype(bf16), wd_v[ch * FC:(ch + 1) * FC, :])
        y_ref[...] = y
        yhat, _ = _rms_hat(y)
        out_ref[...] = x + 0.5 * gt * (yhat * g_ref[gi + 1:gi + 2])

    rt = lambda c: pl.BlockSpec((TM, c), lambda i: (i, 0))
    return _call(
        body, (xs, mods, gvec, wgu, wd), comm, name=name, grid=(nt,),
        in_specs=[rt(D), pl.BlockSpec((1, N_MOD, D), _typ(cfg)), pl.BlockSpec((6, D), lambda i: (0, 0)), ANY, ANY],
        out_specs=[rt(D), rt(D), rt(2 * F), rt(D)],
        out_shape=[S_((R, D), f32), S_((R, D), bf16), S_((R, 2 * F), bf16), S_((R, D), f32)],
        scratch_shapes=[pltpu.VMEM((D, 2 * F), bf16), pltpu.VMEM((F, D), bf16), pltpu.SemaphoreType.DMA((2,))],
        compiler_params=_cp(VMEM_BIG, ("arbitrary",)),
    )


def _ffn_bwd(cfg, dout, xs, z, y, mods, gvec, wgu, wd, mi, gi, with_ctx, name, comm=None):
    TM, D, F, FC = cfg.TM, cfg.D, cfg.F, cfg.FC
    nt = cfg.ntiles(with_ctx)
    R = nt * TM
    ntyp = 2 if with_ctx else 1

    def body(do_ref, xs_ref, z_ref, y_ref, mods_ref, g_ref, wgu_hbm, wd_hbm,
             dx_ref, dz_ref, dy_ref, a_ref, dm_ref, dg_ref, wgu_v, wd_v, sem):
        i = pl.program_id(0)

        @pl.when(i == 0)
        def _():
            c0 = pltpu.make_async_copy(wgu_hbm, wgu_v, sem.at[0])
            c1 = pltpu.make_async_copy(wd_hbm, wd_v, sem.at[1])
            c0.start(); c1.start(); c0.wait(); c1.wait()
            dg_ref[...] = jnp.zeros_like(dg_ref)

        @pl.when((i == 0) | (i == cfg.nxt))
        def _():
            dm_ref[...] = jnp.zeros_like(dm_ref)

        do = do_ref[...]
        x = xs_ref[...]
        m = mods_ref[0]
        sc, gt = m[mi + 1:mi + 2], m[mi + 2:mi + 3]
        g_pre, g_post = g_ref[gi:gi + 1], g_ref[gi + 1:gi + 2]
        xhat, rinv0 = _rms_hat(x)
        n0 = xhat * g_pre
        yhat, rinv1 = _rms_hat(y_ref[...])
        d_gt = _rsum(0.5 * do * (yhat * g_post))
        dr = (0.5 * gt) * do
        dg_post = _rsum(dr * yhat)
        dy = _rms_bwd(dr * g_post, yhat, rinv1)
        dyb = dy.astype(bf16)
        dy_ref[...] = dyb
        dh = jnp.zeros((TM, D), f32)
        for ch in range(F // FC):
            zg = z_ref[:, ch * FC:(ch + 1) * FC].astype(f32)
            zu = z_ref[:, F + ch * FC:F + (ch + 1) * FC].astype(f32)
            sg = _sigmoid(zg)
            silu = zg * sg
            a_ref[:, ch * FC:(ch + 1) * FC] = (silu * zu).astype(bf16)
            da = _nt(dyb, wd_v[ch * FC:(ch + 1) * FC, :])
            dzu = (da * silu).astype(bf16)
            dzg = (da * zu * (sg * (1.0 + zg * (1.0 - sg)))).astype(bf16)
            dz_ref[:, ch * FC:(ch + 1) * FC] = dzg
            dz_ref[:, F + ch * FC:F + (ch + 1) * FC] = dzu
            dh = dh + _nt(dzg, wgu_v[:, ch * FC:(ch + 1) * FC]) + _nt(dzu, wgu_v[:, F + ch * FC:F + (ch + 1) * FC])
        d_sh = _rsum(dh)
        d_sc = _rsum(dh * n0)
        dn = dh * (1.0 + sc)
        dg_pre = _rsum(dn * xhat)
        dx_ref[...] = do + _rms_bwd(dn * g_pre, xhat, rinv0)
        dm_ref[0] += jnp.concatenate([d_sh, d_sc, d_gt], axis=0)
        dg_ref[...] += jnp.concatenate([dg_pre, dg_post], axis=0)

    rt = lambda c: pl.BlockSpec((TM, c), lambda i: (i, 0))
    return _call(
        body, (dout, xs, z, y, mods, gvec, wgu, wd), comm, name=name, grid=(nt,),
        in_specs=[rt(D), rt(D), rt(2 * F), rt(D), pl.BlockSpec((1, N_MOD, D), _typ(cfg)),
                  pl.BlockSpec((6, D), lambda i: (0, 0)), ANY, ANY],
        out_specs=[rt(D), rt(2 * F), rt(D), rt(F), pl.BlockSpec((1, 3, D), _typ(cfg)), pl.BlockSpec((2, D), lambda i: (0, 0))],
        out_shape=[S_((R, D), f32), S_((R, 2 * F), bf16), S_((R, D), bf16), S_((R, F), bf16),
                   S_((ntyp, 3, D), f32), S_((2, D), f32)],
        scratch_shapes=[pltpu.VMEM((D, 2 * F), bf16), pltpu.VMEM((F, D), bf16), pltpu.SemaphoreType.DMA((2,))],
        compiler_params=_cp(VMEM_BIG, ("arbitrary",)),
    )


def _wgrad(a, b, k_rows, name, comm=None):
    M, N = a.shape[1], b.shape[1]
    tn = N
    for cand in (1408, 1024, 512):
        if N % cand == 0 and N > cand:
            tn = cand
            break
    room = VMEM_BIG - 6 * MIB - 2 * M * tn * 6
    tk = _div_tile(k_rows, 1, min(2816, room // (4 * (M + tn))), LANES)
    nk = k_rows // tk

    def body(a_ref, b_ref, o_ref, ob_ref):
        k = pl.program_id(1)

        @pl.when(k == 0)
        def _():
            o_ref[...] = jnp.zeros_like(o_ref)
        o_ref[...] += _tn(a_ref[...], b_ref[...])

        @pl.when(k == nk - 1)
        def _():
            ob_ref[...] = o_ref[...].astype(bf16)

    ospec = pl.BlockSpec((M, tn), lambda n, k: (0, n))
    return _call(
        body, (a, b), comm, name=name, grid=(N // tn, nk),
        in_specs=[pl.BlockSpec((tk, M), lambda n, k: (k, 0)), pl.BlockSpec((tk, tn), lambda n, k: (k, n))],
        out_specs=[ospec, ospec], out_shape=[S_((M, N), f32), S_((M, N), bf16)],
        compiler_params=_cp(VMEM_BIG, ("arbitrary", "arbitrary")),
    )


def _tmpre_fwd(cfg, xs, mods, gvec, w_in, cos, sin, name):
    TM, D = cfg.TM, cfg.D
    nt, R = cfg.ntt, cfg.T
    W = NA_WIDTH

    def body(xs_ref, mods_ref, g_ref, w_ref, cos_ref, sin_ref, hb_ref, q_ref, k_ref, v_ref, u_ref):
        x = xs_ref[...]
        m = mods_ref[0]
        xhat, _ = _rms_hat(x)
        hb = ((xhat * g_ref[2:3]) * (1.0 + m[4:5]) + m[3:4]).astype(bf16)
        hb_ref[...] = hb
        p = _nn(hb, w_ref[...])
        cs = jnp.tile(cos_ref[...], (1, W // LANES))
        sn = jnp.tile(sin_ref[...], (1, W // LANES))
        q = p[:, 0:W]
        k = p[:, W:2 * W]
        q_ref[...] = ((q * cs + _swap16(q) * sn) * (HEAD_DIM ** -0.5)).astype(bf16)
        k_ref[...] = (k * cs + _swap16(k) * sn).astype(bf16)
        v_ref[...] = p[:, 2 * W:3 * W].astype(bf16)
        u_ref[...] = p[:, 3 * W:]

    rt = lambda c: pl.BlockSpec((TM, c), lambda i: (i, 0))
    return pl.pallas_call(
        body, name=name, grid=(nt,),
        in_specs=[rt(D), pl.BlockSpec((1, N_MOD, D), _typ(cfg)), pl.BlockSpec((6, D), lambda i: (0, 0)),
                  pl.BlockSpec((D, IN_WIDTH), lambda i: (0, 0)), rt(LANES), rt(LANES)],
        out_specs=[rt(D), rt(W), rt(W), rt(W), rt(POOL_WIDTH)],
        out_shape=[S_((R, D), bf16), S_((R, W), bf16), S_((R, W), bf16), S_((R, W), bf16), S_((R, POOL_WIDTH), f32)],
        compiler_params=_cp(VMEM_MID, ("arbitrary",)),
    )(xs, mods, gvec, w_in, cos, sin)


def _rope_bwd_assemble(cfg, lat, ctx_terms, du_has_ctx, cos, sin, name):
    TM = cfg.TM
    W = NA_WIDTH
    n_ctx = [len(t) for t in ctx_terms]
    flat_ctx = [a for t in ctx_terms for a in t]

    def body(*refs):
        dq_ref, dk_ref, dv_ref, du_ref = refs[:4]
        ctx_refs = refs[4:4 + len(flat_ctx)]
        cos_ref, sin_ref, o_ref = refs[4 + len(flat_ctx):]
        is_ctx = pl.program_id(0) >= cfg.nxt
        vals, off = [], 0
        for lat_ref, n in zip((dq_ref, dk_ref, dv_ref), n_ctx):
            cv = jnp.zeros((TM, W), f32)
            for r_ in ctx_refs[off:off + n]:
                cv = cv + r_[...]
            off += n
            vals.append(jnp.where(is_ctx, cv, lat_ref[...]))
        du_ = du_ref[...] if du_has_ctx else jnp.where(is_ctx, 0.0, du_ref[...])
        cs = jnp.tile(cos_ref[...], (1, W // LANES))
        sn = jnp.tile(sin_ref[...], (1, W // LANES))
        dq_ = vals[0] * (HEAD_DIM ** -0.5)
        dk_ = vals[1]
        o_ref[:, 0:W] = (dq_ * cs + _swap16(dq_ * sn)).astype(bf16)
        o_ref[:, W:2 * W] = (dk_ * cs + _swap16(dk_ * sn)).astype(bf16)
        o_ref[:, 2 * W:3 * W] = vals[2].astype(bf16)
        o_ref[:, 3 * W:] = du_.astype(bf16)

    rt = lambda c: pl.BlockSpec((TM, c), lambda i: (i, 0))
    lat_spec = pl.BlockSpec((TM, W), lambda i: (jnp.minimum(i, cfg.nxt - 1), 0))
    du_spec = rt(POOL_WIDTH) if du_has_ctx else lat_spec
    return pl.pallas_call(
        body, name=name, grid=(cfg.ntt,),
        in_specs=[lat_spec, lat_spec, lat_spec, du_spec] + [pl.BlockSpec((TM, W), lambda i: (0, 0))] * len(flat_ctx)
                 + [rt(LANES), rt(LANES)],
        out_specs=rt(IN_WIDTH), out_shape=S_((cfg.T, IN_WIDTH), bf16),
        compiler_params=_cp(VMEM_MID, ("arbitrary",)),
    )(*lat, *flat_ctx, cos, sin)


def _tmpre_bwd(cfg, dproj, w_in, xs, mods, gvec, dres, res_with_ctx, name):
    TM, D = cfg.TM, cfg.D
    nt, R = cfg.ntt, cfg.T
    nres = cfg.ntiles(res_with_ctx)

    def body(dp_ref, w_ref, xs_ref, mods_ref, g_ref, dres_ref, dx_ref, dm_ref, dg_ref):
        i = pl.program_id(0)

        @pl.when(i == 0)
        def _():
            dg_ref[...] = jnp.zeros_like(dg_ref)

        @pl.when((i == 0) | (i == cfg.nxt))
        def _():
            dm_ref[...] = jnp.zeros_like(dm_ref)

        dh = _nt(dp_ref[...], w_ref[...])
        x = xs_ref[...]
        m = mods_ref[0]
        g2 = g_ref[2:3]
        xhat, rinv = _rms_hat(x)
        d_sh = _rsum(dh)
        d_sc = _rsum(dh * (xhat * g2))
        dn = dh * (1.0 + m[4:5])
        dg_ref[...] += _rsum(dn * xhat)
        dx = _rms_bwd(dn * g2, xhat, rinv)
        res = dres_ref[...]
        if nres < nt:
            res = jnp.where(i < nres, res, 0.0)
        dx_ref[...] = res + dx
        dm_ref[0] += jnp.concatenate([d_sh, d_sc], axis=0)

    rt = lambda c: pl.BlockSpec((TM, c), lambda i: (i, 0))
    return pl.pallas_call(
        body, name=name, grid=(nt,),
        in_specs=[rt(IN_WIDTH), pl.BlockSpec((D, IN_WIDTH), lambda i: (0, 0)), rt(D),
                  pl.BlockSpec((1, N_MOD, D), _typ(cfg)), pl.BlockSpec((6, D), lambda i: (0, 0)),
                  pl.BlockSpec((TM, D), lambda i: (jnp.minimum(i, nres - 1), 0))],
        out_specs=[rt(D), pl.BlockSpec((1, 2, D), _typ(cfg)), pl.BlockSpec((1, D), lambda i: (0, 0))],
        out_shape=[S_((R, D), f32), S_((2, 2, D), f32), S_((1, D), f32)],
        compiler_params=_cp(VMEM_MID, ("arbitrary",)),
    )(dproj, w_in, xs, mods, gvec, dres)


def _na_block(cfg, b):
    return jnp.clip(NA_QR * b - NA_KH // 2, 0, cfg.rows - NA_WR)


def _na_load_bias(b, nb, b_hbm, b_v, sem):
    for typ, at in ((0, 0), (1, 1), (2, nb - 1)):
        @pl.when(b == at)
        def _(typ=typ):
            cp = pltpu.make_async_copy(b_hbm.at[typ], b_v, sem)
            cp.start()
            cp.wait()


def _na_probs(qh, klh, kch, bias):
    s_loc = _nt(qh, klh) + bias
    s_ctx = _nt(qh, kch)
    mx = jnp.maximum(jnp.max(s_loc, axis=-1, keepdims=True), jnp.max(s_ctx, axis=-1, keepdims=True))
    e_loc = jnp.exp(s_loc - mx)
    e_ctx = jnp.exp(s_ctx - mx)
    inv = 1.0 / (jnp.sum(e_loc, axis=-1, keepdims=True) + jnp.sum(e_ctx, axis=-1, keepdims=True))
    return e_loc * inv, e_ctx * inv


def _na_fwd(cfg, q, k, v, bexp, name, comm=None):
    S, L, T = cfg.S, cfg.L, cfg.T
    NQ, NW = NA_QR * GRID_W, NA_WR * GRID_W
    nb = cfg.rows // NA_QR

    def body(q_ref, k_hbm, v_hbm, b_hbm, o_ref, k_v, v_v, b_v, sem):
        b = pl.program_id(0)

        @pl.when(b == 0)
        def _():
            cs = [pltpu.make_async_copy(k_hbm, k_v, sem.at[0]), pltpu.make_async_copy(v_hbm, v_v, sem.at[1])]
            for c_ in cs:
                c_.start()
            for c_ in cs:
                c_.wait()

        _na_load_bias(b, nb, b_hbm, b_v, sem.at[2])
        st = pl.multiple_of(_na_block(cfg, b) * GRID_W, GRID_W)
        first = lax.broadcasted_iota(jnp.int32, (NQ, LANES), 1) < HEAD_DIM
        for hp in range(NA_HEADS // 2):
            ls = slice(hp * LANES, (hp + 1) * LANES)
            q2 = q_ref[:, ls]
            kl, vl = k_v[pl.ds(st, NW), ls], v_v[pl.ds(st, NW), ls]
            kc, vc = k_v[S:T, ls], v_v[S:T, ls]
            o2 = []
            for hh in range(2):
                qm = jnp.where(first if hh == 0 else ~first, q2, jnp.zeros_like(q2))
                p_loc, p_ctx = _na_probs(qm, kl, kc, b_v[2 * hp + hh])
                o2.append(_nn(p_loc.astype(bf16), vl) + _nn(p_ctx.astype(bf16), vc))
            o_ref[:, ls] = jnp.where(first, o2[0], o2[1]).astype(bf16)

    return _call(
        body, (q, k, v, bexp), comm, name=name, grid=(nb,),
        in_specs=[pl.BlockSpec((NQ, NA_WIDTH), lambda b: (b, 0)), ANY, ANY, ANY],
        out_specs=[pl.BlockSpec((NQ, NA_WIDTH), lambda b: (b, 0))],
        out_shape=[S_((S, NA_WIDTH), bf16)],
        scratch_shapes=[pltpu.VMEM((T, NA_WIDTH), bf16), pltpu.VMEM((T, NA_WIDTH), bf16),
                        pltpu.VMEM((NA_HEADS, NQ, NW), f32), pltpu.SemaphoreType.DMA((3,))],
        compiler_params=_cp(VMEM_MID, ("arbitrary",)),
    )


def _na_bwd(cfg, do, q, k, v, bexp, name, comm=None):
    S, L, T, rows = cfg.S, cfg.L, cfg.T, cfg.rows
    NQ, NW = NA_QR * GRID_W, NA_WR * GRID_W
    NSLOT = 2 * NA_KH
    nb = rows // NA_QR
    bmax = (rows - NA_WR) // NA_QR
    steps = 2 * nb - bmax
    W = NA_WIDTH
    assert nb >= 3 and bmax >= 1 and rows - NA_QR * bmax <= NSLOT

    def out_group(g):
        return jnp.where(g >= nb, g - nb + bmax, jnp.clip(g - 1, 0, bmax - 1))

    def body(do_ref, q_ref, k_hbm, v_hbm, b_hbm, dq_ref, dk_ref, dv_ref, dkc_ref, dvc_ref, db_hbm,
             k_v, v_v, b_v, db_v, ak, av, akc, avc, sem):
        g = pl.program_id(0)

        @pl.when(g == 0)
        def _():
            cs = [pltpu.make_async_copy(k_hbm, k_v, sem.at[0]), pltpu.make_async_copy(v_hbm, v_v, sem.at[1])]
            for c_ in cs:
                c_.start()
            db_v[...] = jnp.zeros_like(db_v)
            ak[...] = jnp.zeros_like(ak)
            av[...] = jnp.zeros_like(av)
            akc[...] = jnp.zeros_like(akc)
            avc[...] = jnp.zeros_like(avc)
            for c_ in cs:
                c_.wait()

        for typ, at in ((0, 1), (1, nb - 1)):
            @pl.when(g == at)
            def _(typ=typ):
                cp = pltpu.make_async_copy(db_v, db_hbm.at[typ], sem.at[2])
                cp.start()
                cp.wait()
                db_v[...] = jnp.zeros_like(db_v)

        @pl.when(g < nb)
        def _():
            _na_load_bias(g, nb, b_hbm, b_v, sem.at[2])
            ws = _na_block(cfg, g)
            st = pl.multiple_of(ws * GRID_W, GRID_W)
            first = lax.broadcasted_iota(jnp.int32, (NQ, LANES), 1) < HEAD_DIM
            for hp in range(NA_HEADS // 2):
                ls = slice(hp * LANES, (hp + 1) * LANES)
                q2, do2 = q_ref[:, ls], do_ref[:, ls]
                kl, vl = k_v[pl.ds(st, NW), ls], v_v[pl.ds(st, NW), ls]
                kc, vc = k_v[S:T, ls], v_v[S:T, ls]
                dq2 = []
                dk2 = jnp.zeros((NW, LANES), f32)
                dv2 = jnp.zeros((NW, LANES), f32)
                dkc2 = jnp.zeros((L, LANES), f32)
                dvc2 = jnp.zeros((L, LANES), f32)
                for hh in range(2):
                    keep = first if hh == 0 else ~first
                    qm = jnp.where(keep, q2, jnp.zeros_like(q2))
                    dom = jnp.where(keep, do2, jnp.zeros_like(do2))
                    p_loc, p_ctx = _na_probs(qm, kl, kc, b_v[2 * hp + hh])
                    dp_loc = _nt(dom, vl)
                    dp_ctx = _nt(dom, vc)
                    delta = jnp.sum(p_loc * dp_loc, axis=-1, keepdims=True) + jnp.sum(p_ctx * dp_ctx, axis=-1, keepdims=True)
                    ds_loc = p_loc * (dp_loc - delta)
                    ds_ctx = p_ctx * (dp_ctx - delta)
                    db_v[2 * hp + hh, :, 0:NW] += ds_loc
                    dsl, dsc = ds_loc.astype(bf16), ds_ctx.astype(bf16)
                    dq2.append(_nn(dsl, kl) + _nn(dsc, kc))
                    dk2 = dk2 + _tn(dsl, qm)
                    dv2 = dv2 + _tn(p_loc.astype(bf16), dom)
                    dkc2 = dkc2 + _tn(dsc, qm)
                    dvc2 = dvc2 + _tn(p_ctx.astype(bf16), dom)
                dq_ref[:, ls] = jnp.where(first, dq2[0], dq2[1])
                akc[:, ls] += dkc2
                avc[:, ls] += dvc2
                for kk in range(NA_WR):
                    slot = (ws + kk) % NSLOT
                    ak[slot, :, ls] += dk2[kk * GRID_W:(kk + 1) * GRID_W, :]
                    av[slot, :, ls] += dv2[kk * GRID_W:(kk + 1) * GRID_W, :]

        @pl.when(((g >= 1) & (g <= bmax)) | (g >= nb))
        def _():
            base = NA_QR * (out_group(g) % (NSLOT // NA_QR))
            for t in range(NA_QR):
                dk_ref[t * GRID_W:(t + 1) * GRID_W, :] = ak[base + t]
                dv_ref[t * GRID_W:(t + 1) * GRID_W, :] = av[base + t]
                ak[base + t] = jnp.zeros((GRID_W, W), f32)
                av[base + t] = jnp.zeros((GRID_W, W), f32)

        @pl.when(g == nb - 1)
        def _():
            cp = pltpu.make_async_copy(db_v, db_hbm.at[2], sem.at[2])
            cp.start()
            cp.wait()

        @pl.when(g == steps - 1)
        def _():
            dkc_ref[...] = akc[...]
            dvc_ref[...] = avc[...]

    qmap = lambda g: (jnp.minimum(g, nb - 1), 0)
    kmap = lambda g: (out_group(g), 0)
    full = lambda g: (0, 0)
    return _call(
        body, (do, q, k, v, bexp), comm, name=name, grid=(steps,),
        in_specs=[pl.BlockSpec((NQ, W), qmap), pl.BlockSpec((NQ, W), qmap), ANY, ANY, ANY],
        out_specs=[pl.BlockSpec((NQ, W), qmap), pl.BlockSpec((NQ, W), kmap), pl.BlockSpec((NQ, W), kmap),
                   pl.BlockSpec((L, W), full), pl.BlockSpec((L, W), full), ANY],
        out_shape=[S_((S, W), f32), S_((S, W), f32), S_((S, W), f32), S_((L, W), f32), S_((L, W), f32),
                   S_((NA_TYPES, NA_HEADS, NQ, NA_WPAD), f32)],
        scratch_shapes=[pltpu.VMEM((T, W), bf16), pltpu.VMEM((T, W), bf16),
                        pltpu.VMEM((NA_HEADS, NQ, NW), f32), pltpu.VMEM((NA_HEADS, NQ, NA_WPAD), f32),
                        pltpu.VMEM((NSLOT, GRID_W, W), f32), pltpu.VMEM((NSLOT, GRID_W, W), f32),
                        pltpu.VMEM((L, W), f32), pltpu.VMEM((L, W), f32), pltpu.SemaphoreType.DMA((3,))],
        compiler_params=_cp(VMEM_BIG, ("arbitrary",)),
    )


def _rpb_reduce(dbias, flip, sel, name):
    nq, w = NA_QR * GRID_W, GRID_W

    def diag_body(x_ref, j_ref, o_ref):
        rows = []
        for i in range(NA_QR):
            xr = jnp.dot(j_ref[...], x_ref[i * w:(i + 1) * w, :], preferred_element_type=f32, precision=lax.Precision.HIGHEST)
            rows.append(jnp.sum(pltpu.roll(xr, 0, 1, stride=1, stride_axis=0), axis=0, keepdims=True))
        o_ref[...] = jnp.concatenate(rows + [jnp.zeros((8 - NA_QR, NA_WPAD), f32)], axis=0)

    diag = pl.pallas_call(
        diag_body, name=name + "_diag", grid=(NA_TYPES, NA_HEADS),
        in_specs=[pl.BlockSpec((None, None, nq, NA_WPAD), lambda t, h: (t, h, 0, 0)), pl.BlockSpec((w, w), lambda t, h: (0, 0))],
        out_specs=pl.BlockSpec((None, None, 8, NA_WPAD), lambda t, h: (t, h, 0, 0)),
        out_shape=S_((NA_TYPES, NA_HEADS, 8, NA_WPAD), f32),
        compiler_params=_cp(VMEM_MID, ("arbitrary", "arbitrary")),
    )(dbias, flip)
    lo = w - NA_KW
    y = diag[:, :, :NA_QR, lo:lo + NA_WR * w].reshape(NA_TYPES, NA_HEADS, NA_QR, NA_WR, w)
    y = jnp.transpose(y, (1, 0, 2, 3, 4)).reshape(NA_HEADS, NA_TYPES * NA_QR * NA_WR, w)
    y = jnp.pad(y, ((0, 0), (0, NA_SEL_ROWS - y.shape[1]), (0, LANES - w)))

    def body(y_ref, sel_ref, o_ref):
        o_ref[...] = jnp.dot(sel_ref[...], y_ref[...], preferred_element_type=f32, precision=lax.Precision.HIGHEST)

    return pl.pallas_call(
        body, name=name, grid=(NA_HEADS,),
        in_specs=[pl.BlockSpec((None, NA_SEL_ROWS, LANES), lambda h: (h, 0, 0)), pl.BlockSpec((16, NA_SEL_ROWS), lambda h: (0, 0))],
        out_specs=pl.BlockSpec((None, 16, LANES), lambda h: (h, 0, 0)),
        out_shape=S_((NA_HEADS, 16, LANES), f32),
        compiler_params=_cp(VMEM_MID, ("arbitrary",)),
    )(y, sel)


def _ctx_attn_fwd(cfg, q, k, v, name):
    L = cfg.L
    blk = cfg.S // L

    def body(q_ref, k_ref, v_ref, o_ref):
        qv, kv, vv = q_ref[...], k_ref[...], v_ref[...]
        outs = []
        for h in range(NA_HEADS):
            hs = slice(h * HEAD_DIM, (h + 1) * HEAD_DIM)
            s = _nt(qv[:, hs], kv[:, hs])
            e = jnp.exp(s - jnp.max(s, axis=-1, keepdims=True))
            p = e * (1.0 / jnp.sum(e, axis=-1, keepdims=True))
            outs.append(_nn(p.astype(bf16), vv[:, hs]))
        o_ref[...] = jnp.concatenate(outs, axis=-1).astype(bf16)

    spec = pl.BlockSpec((L, NA_WIDTH), lambda i: (blk, 0))
    return pl.pallas_call(
        body, name=name, grid=(1,), in_specs=[spec, spec, spec],
        out_specs=pl.BlockSpec((L, NA_WIDTH), lambda i: (0, 0)), out_shape=S_((L, NA_WIDTH), bf16),
        compiler_params=_cp(VMEM_MID, ("arbitrary",)),
    )(q, k, v)


def _ctx_attn_bwd(cfg, do, q, k, v, name):
    L = cfg.L
    blk = cfg.S // L

    def body(do_ref, q_ref, k_ref, v_ref, dq_ref, dk_ref, dv_ref):
        dov, qv, kv, vv = do_ref[...], q_ref[...], k_ref[...], v_ref[...]
        dqs, dks, dvs = [], [], []
        for h in range(NA_HEADS):
            hs = slice(h * HEAD_DIM, (h + 1) * HEAD_DIM)
            qh, kh, doh = qv[:, hs], kv[:, hs], dov[:, hs]
            s = _nt(qh, kh)
            e = jnp.exp(s - jnp.max(s, axis=-1, keepdims=True))
            p = e * (1.0 / jnp.sum(e, axis=-1, keepdims=True))
            dp = _nt(doh, vv[:, hs])
            ds = (p * (dp - jnp.sum(p * dp, axis=-1, keepdims=True))).astype(bf16)
            dqs.append(_nn(ds, kh))
            dks.append(_tn(ds, qh))
            dvs.append(_tn(p.astype(bf16), doh))
        dq_ref[...] = jnp.concatenate(dqs, axis=-1)
        dk_ref[...] = jnp.concatenate(dks, axis=-1)
        dv_ref[...] = jnp.concatenate(dvs, axis=-1)

    spec = pl.BlockSpec((L, NA_WIDTH), lambda i: (blk, 0))
    ospec = pl.BlockSpec((L, NA_WIDTH), lambda i: (0, 0))
    return pl.pallas_call(
        body, name=name, grid=(1,), in_specs=[spec, spec, spec, spec],
        out_specs=[ospec, ospec, ospec], out_shape=[S_((L, NA_WIDTH), f32)] * 3,
        compiler_params=_cp(VMEM_MID, ("arbitrary",)),
    )(do, q, k, v)


def _pool_centered(u, band, inv):
    hi = u.astype(bf16)
    lo = (u - hi.astype(f32)).astype(bf16)
    return (_nn(band, hi) + _nn(band, lo)) * inv - u


def _pool_fwd(cfg, u, band, inv, w_pool, pool_scale, with_ctx, name):
    TM = cfg.TM
    nt = cfg.ntiles(with_ctx)
    C = POOL_CH

    def body(u_ref, band_ref, inv_ref, w_ref, ps_ref, o_ref):
        outs = []
        for g in range(POOL_GROUPS):
            d = _pool_centered(u_ref[:, g * C:(g + 1) * C], band_ref[0, g], inv_ref[0, g])
            outs.append(_nn(d.astype(bf16), w_ref[g].astype(bf16)) * ps_ref[:, g * C:(g + 1) * C])
        o_ref[...] = jnp.concatenate(outs, axis=-1).astype(bf16)

    typ4 = lambda i: (jnp.minimum(i // cfg.nxt, 1), 0, 0, 0)
    return pl.pallas_call(
        body, name=name, grid=(nt,),
        in_specs=[pl.BlockSpec((TM, POOL_WIDTH), lambda i: (i, 0)), pl.BlockSpec((1, POOL_GROUPS, TM, TM), typ4),
                  pl.BlockSpec((1, POOL_GROUPS, TM, 1), typ4), pl.BlockSpec((POOL_GROUPS, C, C), lambda i: (0, 0, 0)),
                  pl.BlockSpec((1, POOL_WIDTH), lambda i: (0, 0))],
        out_specs=pl.BlockSpec((TM, POOL_WIDTH), lambda i: (i, 0)),
        out_shape=S_((nt * TM, POOL_WIDTH), bf16),
        compiler_params=_cp(VMEM_MID, ("arbitrary",)),
    )(u, band, inv, w_pool, pool_scale)


def _pool_bwd(cfg, dmix, u, band, inv, w_pool, pool_scale, with_ctx, name):
    TM = cfg.TM
    nt = cfg.ntiles(with_ctx)
    C = POOL_CH

    def body(dy_ref, u_ref, band_ref, inv_ref, w_ref, ps_ref, du_ref, dw_ref, dps_ref):
        @pl.when(pl.program_id(0) == 0)
        def _():
            dw_ref[...] = jnp.zeros_like(dw_ref)
            dps_ref[...] = jnp.zeros_like(dps_ref)

        dus, dpss = [], []
        for g in range(POOL_GROUPS):
            gs = slice(g * C, (g + 1) * C)
            band_g, inv_g = band_ref[0, g], inv_ref[0, g]
            db = _pool_centered(u_ref[:, gs], band_g, inv_g).astype(bf16)
            wb = w_ref[g].astype(bf16)
            dy = dy_ref[:, gs].astype(f32)
            dpss.append(_rsum(dy * _nn(db, wb)))
            dys = (dy * ps_ref[:, gs]).astype(bf16)
            dw_ref[g] += _tn(db, dys)
            dd = _nt(dys, wb)
            t = dd * inv_g
            hi = t.astype(bf16)
            lo = (t - hi.astype(f32)).astype(bf16)
            dus.append(_tn(band_g, hi) + _tn(band_g, lo) - dd)
        du_ref[...] = jnp.concatenate(dus, axis=-1)
        dps_ref[...] += jnp.concatenate(dpss, axis=-1)

    typ4 = lambda i: (jnp.minimum(i // cfg.nxt, 1), 0, 0, 0)
    return pl.pallas_call(
        body, name=name, grid=(nt,),
        in_specs=[pl.BlockSpec((TM, POOL_WIDTH), lambda i: (i, 1)), pl.BlockSpec((TM, POOL_WIDTH), lambda i: (i, 0)),
                  pl.BlockSpec((1, POOL_GROUPS, TM, TM), typ4), pl.BlockSpec((1, POOL_GROUPS, TM, 1), typ4),
                  pl.BlockSpec((POOL_GROUPS, C, C), lambda i: (0, 0, 0)), pl.BlockSpec((1, POOL_WIDTH), lambda i: (0, 0))],
        out_specs=[pl.BlockSpec((TM, POOL_WIDTH), lambda i: (i, 0)), pl.BlockSpec((POOL_GROUPS, C, C), lambda i: (0, 0, 0)),
                   pl.BlockSpec((1, POOL_WIDTH), lambda i: (0, 0))],
        out_shape=[S_((nt * TM, POOL_WIDTH), f32), S_((POOL_GROUPS, C, C), f32), S_((1, POOL_WIDTH), f32)],
        compiler_params=_cp(VMEM_MID, ("arbitrary",)),
    )(dmix, u, band, inv, w_pool, pool_scale)


def _tmpost_fwd(cfg, na_x, na_c, pool, w_out, xs, mods, gvec, name):
    TM, D = cfg.TM, cfg.D
    with_ctx = na_c is not None
    nt = cfg.ntiles(with_ctx)
    R = nt * TM

    def body(*refs):
        if with_ctx:
            nax_ref, nac_ref, pool_ref, w_ref, xs_ref, mods_ref, g_ref, out_ref, opre_ref, mix_ref = refs
            na = jnp.where(pl.program_id(0) < cfg.nxt, nax_ref[...], nac_ref[...])
        else:
            nax_ref, pool_ref, w_ref, xs_ref, mods_ref, g_ref, out_ref, opre_ref, mix_ref = refs
            na = nax_ref[...]
        pool_v = pool_ref[...]
        mix_ref[:, 0:NA_WIDTH] = na
        mix_ref[:, NA_WIDTH:] = pool_v
        o = _nn(na, w_ref[0:NA_WIDTH, :]) + _nn(pool_v, w_ref[NA_WIDTH:, :])
        opre_ref[...] = o
        ohat, _ = _rms_hat(o)
        out_ref[...] = xs_ref[...] + mods_ref[0][5:6] * (ohat * g_ref[3:4])

    rt = lambda c: pl.BlockSpec((TM, c), lambda i: (i, 0))
    na_specs = [pl.BlockSpec((TM, NA_WIDTH), lambda i: (jnp.minimum(i, cfg.nxt - 1), 0))]
    na_args = [na_x]
    if with_ctx:
        na_specs.append(pl.BlockSpec((TM, NA_WIDTH), lambda i: (0, 0)))
        na_args.append(na_c)
    return pl.pallas_call(
        body, name=name, grid=(nt,),
        in_specs=na_specs + [rt(POOL_WIDTH), pl.BlockSpec((MIX_WIDTH, D), lambda i: (0, 0)), rt(D),
                             pl.BlockSpec((1, N_MOD, D), _typ(cfg)), pl.BlockSpec((6, D), lambda i: (0, 0))],
        out_specs=[rt(D), rt(D), rt(MIX_WIDTH)],
        out_shape=[S_((R, D), f32), S_((R, D), f32), S_((R, MIX_WIDTH), bf16)],
        compiler_params=_cp(VMEM_MID, ("arbitrary",)),
    )(*na_args, pool, w_out, xs, mods, gvec)


def _tmpost_bwd(cfg, dout, opre, w_out, mods, gvec, with_ctx, name):
    TM, D = cfg.TM, cfg.D
    nt = cfg.ntiles(with_ctx)
    R = nt * TM
    ntyp = 2 if with_ctx else 1

    def body(do_ref, opre_ref, w_ref, mods_ref, g_ref, dop_ref, dmix_ref, dm_ref, dg_ref):
        i = pl.program_id(0)

        @pl.when(i == 0)
        def _():
            dg_ref[...] = jnp.zeros_like(dg_ref)

        @pl.when((i == 0) | (i == cfg.nxt))
        def _():
            dm_ref[...] = jnp.zeros_like(dm_ref)

        do = do_ref[...]
        g3 = g_ref[3:4]
        ohat, rinv = _rms_hat(opre_ref[...])
        dm_ref[0] += _rsum(do * (ohat * g3))
        dr = mods_ref[0][5:6] * do
        dg_ref[...] += _rsum(dr * ohat)
        dob = _rms_bwd(dr * g3, ohat, rinv).astype(bf16)
        dop_ref[...] = dob
        dmix_ref[...] = _nt(dob, w_ref[...]).astype(bf16)

    rt = lambda c: pl.BlockSpec((TM, c), lambda i: (i, 0))
    return pl.pallas_call(
        body, name=name, grid=(nt,),
        in_specs=[rt(D), rt(D), pl.BlockSpec((MIX_WIDTH, D), lambda i: (0, 0)),
                  pl.BlockSpec((1, N_MOD, D), _typ(cfg)), pl.BlockSpec((6, D), lambda i: (0, 0))],
        out_specs=[rt(D), rt(MIX_WIDTH), pl.BlockSpec((1, 1, D), _typ(cfg)), pl.BlockSpec((1, D), lambda i: (0, 0))],
        out_shape=[S_((R, D), bf16), S_((R, MIX_WIDTH), bf16), S_((ntyp, 1, D), f32), S_((1, D), f32)],
        compiler_params=_cp(VMEM_MID, ("arbitrary",)),
    )(dout, opre, w_out, mods, gvec)


def _loss_head(cfg, y, target, name):
    TM, D = cfg.TM, cfg.D

    def body(y_ref, t_ref, dy_ref, loss_ref):
        @pl.when(pl.program_id(0) == 0)
        def _():
            loss_ref[...] = jnp.zeros_like(loss_ref)
        e = y_ref[...] - t_ref[...]
        dy_ref[...] = e * (1.0 / D)
        loss_ref[...] += jnp.sum(jnp.mean(e * e, axis=-1, keepdims=True), axis=0, keepdims=True) * 0.5

    rt = pl.BlockSpec((TM, D), lambda i: (i, 0))
    return pl.pallas_call(
        body, name=name, grid=(cfg.nxt,), in_specs=[rt, rt],
        out_specs=[rt, pl.BlockSpec((8, LANES), lambda i: (0, 0))],
        out_shape=[S_((cfg.S, D), f32), S_((8, LANES), f32)],
        compiler_params=_cp(VMEM_MID, ("arbitrary",)),
    )(y, target)


def _modvec_fwd(cvecs, w_mod, b_shard, name):
    nl, D, n = w_mod.shape
    tn = n // 3 if (n % 3 == 0 and (n // 3) % LANES == 0) else n

    def body(c_ref, w_ref, b_ref, o_ref, s_ref):
        cv = c_ref[...]
        sv = cv * _sigmoid(cv)
        s_ref[...] = sv
        o_ref[...] = _nn(sv.astype(bf16), w_ref[...].astype(bf16)) + b_ref[...]

    return pl.pallas_call(
        body, name=name, grid=(nl, n // tn),
        in_specs=[pl.BlockSpec((16, D), lambda l, j: (0, 0)), pl.BlockSpec((None, D, tn), lambda l, j: (l, 0, j)),
                  pl.BlockSpec((None, 1, tn), lambda l, j: (l, 0, j))],
        out_specs=[pl.BlockSpec((None, 16, tn), lambda l, j: (l, 0, j)), pl.BlockSpec((16, D), lambda l, j: (0, 0))],
        out_shape=[S_((nl, 16, n), f32), S_((16, D), f32)],
        compiler_params=_cp(VMEM_MID, ("arbitrary", "arbitrary")),
    )(cvecs, w_mod, b_shard)


def _modvec_bwd(s_t, dm, w_mod, name):
    nl, D, n = w_mod.shape
    tn = n // 3 if (n % 3 == 0 and (n // 3) % LANES == 0) else n

    def body(s_ref, dm_ref, w_ref, gw_ref, gc_ref):
        @pl.when(pl.program_id(1) == 0)
        def _():
            gc_ref[...] = jnp.zeros_like(gc_ref)
        dmv = dm_ref[...]
        gw_ref[...] = jnp.dot(s_ref[...], dmv, preferred_element_type=f32, precision=lax.Precision.HIGHEST)
        gc_ref[...] += _nt(dmv[8:16].astype(bf16), w_ref[...].astype(bf16))

    return pl.pallas_call(
        body, name=name, grid=(nl, n // tn),
        in_specs=[pl.BlockSpec((D, 16), lambda l, j: (0, 0)), pl.BlockSpec((None, 16, tn), lambda l, j: (l, 0, j)),
                  pl.BlockSpec((None, D, tn), lambda l, j: (l, 0, j))],
        out_specs=[pl.BlockSpec((None, D, tn), lambda l, j: (l, 0, j)), pl.BlockSpec((None, 8, D), lambda l, j: (l, 0, 0))],
        out_shape=[S_((nl, D, n), f32), S_((nl, 8, D), f32)],
        compiler_params=_cp(VMEM_MID, ("arbitrary", "arbitrary")),
    )(s_t, dm, w_mod)


def _as2d(a):
    n = a.size
    if a.ndim >= 2 and a.shape[-1] % LANES == 0:
        return a.reshape(-1, a.shape[-1])
    if n % LANES == 0:
        return a.reshape(-1, LANES)
    return a.reshape(-1, a.shape[-1]) if a.ndim >= 2 else a.reshape(1, n)


def _row_tile(r, c, budget_elems=512 * 1024):
    if r * c <= budget_elems or r % 8 != 0:
        return r
    t = r
    while t * c > budget_elems and t % 16 == 0:
        t //= 2
    return t


def _div_tile(r, c, budget_elems, mult=16):
    best = None
    for t in range(mult, r + 1, mult):
        if r % t == 0 and t * c <= budget_elems:
            best = t
    return best if best is not None else r


def _cast_into_place(shards, lead, axis, kidx, name):
    r, c = shards.shape[-2:]
    tr = _div_tile(r, c, 768 * 1024)
    nr = r // tr
    out_map = (lambda i, k: (i, k[0])) if axis == 1 else (lambda i, k: (k[0] * nr + i, 0))
    full2 = (r, c * N_CHIPS) if axis == 1 else (r * N_CHIPS, c)

    def body(k_ref, a_ref, o_ref):
        o_ref[...] = a_ref[...].astype(bf16)

    return pl.pallas_call(
        body, name=name,
        grid_spec=pltpu.PrefetchScalarGridSpec(
            num_scalar_prefetch=1, grid=(nr,),
            in_specs=[pl.BlockSpec((None,) * len(lead) + (tr, c), lambda i, k: tuple(lead) + (i, 0))],
            out_specs=pl.BlockSpec((tr, c), out_map)),
        out_shape=S_(full2, bf16), compiler_params=_cp(VMEM_MID, ("arbitrary",)),
    )(kidx, shards)


def _sum_devices8(own, land, axis, into, lead, dck, name):
    _, rh, cs = land.shape
    tr = _div_tile(rh, cs, 400 * 1024)
    nr = rh // tr
    if axis == 1:
        own_map = lambda i, s: (s[1] * nr + i, s[2])
    else:
        own_map = lambda i, s: (s[2] * 2 * nr + s[1] * nr + i, 0)
    nl = len(lead)

    def land_spec(j):
        return pl.BlockSpec((None, tr, cs), lambda i, s: ((s[0] + j) % N_DEV, i, 0))

    def body(s_ref, own_ref, *rest):
        acc = own_ref[...]
        for p_ref in rest[:N_DEV - 1]:
            acc = acc + p_ref[...].astype(f32)
        rest[-1][...] = acc

    return pl.pallas_call(
        body, name=name,
        grid_spec=pltpu.PrefetchScalarGridSpec(
            num_scalar_prefetch=1, grid=(nr,),
            in_specs=[pl.BlockSpec((tr, cs), own_map)] + [land_spec(j) for j in range(1, N_DEV)] + [ANY],
            out_specs=pl.BlockSpec((None,) * nl + (tr, cs), lambda i, s: tuple(lead) + (s[1] * nr + i, 0))),
        out_shape=S_(into.shape, f32), input_output_aliases={N_DEV + 1: 0},
        compiler_params=_cp(VMEM_MID, ("arbitrary",)),
    )(dck, own, *([land] * (N_DEV - 1)), into)


def _adamw(w, g, m, v, name):
    shape = w.shape
    w2, g2, m2, v2 = _as2d(w), _as2d(g), _as2d(m), _as2d(v)
    r, c = w2.shape
    tr = _row_tile(r, c, 256 * 1024)
    c1 = 1.0 - ADAM_B1 ** ADAM_STEP
    c2 = 1.0 - ADAM_B2 ** ADAM_STEP

    def body(w_ref, g_ref, m_ref, v_ref, d_ref, mo_ref, vo_ref):
        gv = g_ref[...]
        mn = ADAM_B1 * m_ref[...] + (1.0 - ADAM_B1) * gv
        vn = ADAM_B2 * v_ref[...] + (1.0 - ADAM_B2) * (gv * gv)
        mo_ref[...] = mn
        vo_ref[...] = vn
        d_ref[...] = -ADAM_LR * ((mn / c1) / (jnp.sqrt(vn / c2) + ADAM_EPS) + ADAM_WD * w_ref[...])

    spec = pl.BlockSpec((tr, c), lambda i: (i, 0))
    outs = pl.pallas_call(body, name=name, grid=(r // tr,), in_specs=[spec] * 4, out_specs=[spec] * 3,
                          out_shape=[S_((r, c), f32)] * 3, compiler_params=_cp(VMEM_MID, ("arbitrary",)))(w2, g2, m2, v2)
    return tuple(o.reshape(shape) for o in outs)


def _sum_devices(gathered, name):
    _, r, c = gathered.shape

    def body(a_ref, o_ref):
        acc = a_ref[0]
        for j in range(1, N_DEV):
            acc = acc + a_ref[j]
        o_ref[...] = acc

    tr = _row_tile(r, c, 64 * 1024)
    return pl.pallas_call(
        body, name=name, grid=(r // tr,),
        in_specs=[pl.BlockSpec((N_DEV, tr, c), lambda i: (0, i, 0))], out_specs=pl.BlockSpec((tr, c), lambda i: (i, 0)),
        out_shape=S_((r, c), f32), compiler_params=_cp(VMEM_MID, ("arbitrary",)))(gathered)


def _all_gather_small(block, name):
    m_per, n = block.shape

    def body(x_ref, out_ref, send_sems, recv_sems, local_sem):
        x, y, c = _mesh_pos()
        me, sibling = (x, y, c), (x, y, 1 - c)
        chips = [(1 - x, y), (x, 1 - y), (1 - x, 1 - y)]

        def rows(px, py, pc):
            return out_ref.at[pl.ds((4 * px + 2 * py + pc) * m_per, m_per), :]

        def copy(k, blk, to, src=None):
            return pltpu.make_async_remote_copy(
                src_ref=rows(*blk) if src is None else src, dst_ref=rows(*blk),
                send_sem=send_sems.at[k], recv_sem=recv_sems.at[k], device_id=to, device_id_type=MESH)

        mine = pltpu.make_async_copy(x_ref, rows(*me), local_sem)
        mine.start()
        first = [copy(0, me, sibling, src=x_ref)]
        first += [copy(1 + j, me, (*chip, c), src=x_ref) for j, chip in enumerate(chips)]
        for cp in first:
            cp.start()
        passed = [copy(4 + j, (*chip, c), sibling) for j, chip in enumerate(chips)]
        for j, chip in enumerate(chips):
            copy(1 + j, (*chip, c), me).wait_recv()
            passed[j].start()
        copy(0, sibling, me).wait_recv()
        for j, chip in enumerate(chips):
            copy(4 + j, (*chip, 1 - c), me).wait_recv()
        for cp in first + passed:
            cp.wait_send()
        mine.wait()

    return pl.pallas_call(
        body, name=name, out_shape=S_((N_DEV * m_per, n), block.dtype),
        in_specs=[pl.BlockSpec(memory_space=pltpu.VMEM)], out_specs=pl.BlockSpec(memory_space=pltpu.VMEM),
        scratch_shapes=[pltpu.SemaphoreType.DMA((7,)), pltpu.SemaphoreType.DMA((7,)), pltpu.SemaphoreType.DMA],
        compiler_params=_cp(VMEM_MID),
    )(block)


def _pack_rows(arrays):
    flat = jnp.concatenate([a.reshape(-1) for a in arrays])
    pad = (-flat.size) % (8 * LANES)
    return jnp.pad(flat, (0, pad)).reshape(-1, LANES)


def _unpack_rows(packed, shapes):
    flat = packed.reshape(-1)
    out, off = [], 0
    for s in shapes:
        n = int(np.prod(s))
        out.append(flat[off:off + n].reshape(s))
        off += n
    return out


W_AXIS = {"gu": 1, "dn": 0, "wi": 1, "wo": 0}


def _half_merge(bufs, name):
    nt = len(bufs)

    def body(*refs):
        outs = refs[nt:2 * nt]
        send_sems, recv_sems = refs[2 * nt:]
        x, y, c = _mesh_pos()

        def half(ref, h):
            rh = ref.shape[-2] // 2
            return ref.at[(slice(None),) * (len(ref.shape) - 2) + (pl.ds(h * rh, rh), slice(None))]

        cps = []
        for t in range(nt):
            cp = pltpu.make_async_remote_copy(
                src_ref=half(outs[t], c), dst_ref=half(outs[t], c), send_sem=send_sems.at[t], recv_sem=recv_sems.at[t],
                device_id=(x, y, 1 - c), device_id_type=MESH)
            cp.start()
            cps.append(cp)
        for t in range(nt):
            pltpu.make_async_remote_copy(
                src_ref=half(outs[t], 1 - c), dst_ref=half(outs[t], 1 - c), send_sem=send_sems.at[t], recv_sem=recv_sems.at[t],
                device_id=(x, y, 1 - c), device_id_type=MESH).wait_recv()
        for cp in cps:
            cp.wait_send()

    return pl.pallas_call(
        body, name=name, in_specs=[ANY] * nt, out_specs=[ANY] * nt, out_shape=[S_(b.shape, f32) for b in bufs],
        input_output_aliases={t: t for t in range(nt)},
        scratch_shapes=[pltpu.SemaphoreType.DMA((nt,)), pltpu.SemaphoreType.DMA((nt,))],
        compiler_params=_cp(VMEM_MID),
    )(*bufs)


def _local_step(cfg, xs0, target, mods, norm_g, W, G, dck, na_rpb, w_pool, pool_scale):
    S, L, T, D, F = cfg.S, cfg.L, cfg.T, cfg.D, cfg.F
    depth = norm_g.shape[0]
    cos, sin = _rope_tables(S, L)
    band, inv = _pool_tables(cfg.TM, L)
    flip, sel = _rpb_reduce_tables()

    assert depth == 2, "the carrier schedules below are written for two layers"
    fwd_carry = {"ffn_fwd_0_0": [("wi", 0), ("wo", 0), ("gu", 0, 1), ("dn", 0, 1)],
                 "na_fwd_0": [("gu", 1, 0), ("dn", 1, 0)],
                 "ffn_fwd_0_1": [("wi", 1), ("wo", 1), ("gu", 1, 1), ("dn", 1, 1)]}
    bwd_carry = {"na_bwd_1": [("gu", 1, 1), ("dn", 1, 1)], "ffn_bwd_1_0": [("wi", 1), ("wo", 1)],
                 "ffn_bwd_0_1": [("gu", 1, 0), ("dn", 1, 0)], "na_bwd_0": [("gu", 0, 1), ("dn", 0, 1)],
                 "ffn_bwd_0_0": [("wi", 0), ("wo", 0)], "wgrad_dn_0_0": [("gu", 0, 0)]}
    last_scatter = [("dn", 0, 0)]
    tag = lambda key: "_".join(str(p) for p in key)
    g_f32, g_b16 = {}, {}

    def gather_on(name):
        keys = fwd_carry.get(name)
        return None if keys is None else _gather_comm([W[k_] for k_ in keys], [W_AXIS[k_[0]] for k_ in keys])

    def gathered(name, res):
        if name in fwd_carry:
            W.update(zip(fwd_carry[name], res))

    def scatter_on(name):
        keys = bwd_carry.get(name)
        return None if keys is None else _scatter_comm([g_b16[k_] for k_ in keys], [W_AXIS[k_[0]] for k_ in keys])

    def scattered(keys, lands):
        for key, land in zip(keys, lands):
            G[key[0]] = _sum_devices8(g_f32[key], land, W_AXIS[key[0]], G[key[0]], key[1:], dck, f"sum8_{tag(key)}")

    def wgrad(key, a, b, rows):
        name = f"wgrad_{tag(key)}"
        (g_f32[key], g_b16[key]), lands = _wgrad(a, b, rows, name, scatter_on(name))
        scattered(bwd_carry.get(name, ()), lands)

    saved = []
    xs = xs0
    for l in range(depth):
        last = l == depth - 1
        wc = not last
        gvec = norm_g[l]
        ps = pool_scale[l].reshape(1, POOL_WIDTH)
        bexp = _expand_rpb(na_rpb[l])
        name = f"ffn_fwd_{l}_0"
        (xs1, hb1, z1, y1), res = _ffn_fwd(cfg, xs, mods[l], gvec, W["gu", l, 0], W["dn", l, 0], 0, 0, True, name, gather_on(name))
        gathered(name, res)
        hb2, q, k, v, u = _tmpre_fwd(cfg, xs1, mods[l], gvec, W["wi", l], cos, sin, f"tmpre_fwd_{l}")
        name = f"na_fwd_{l}"
        (na_x,), res = _na_fwd(cfg, q, k, v, bexp, name, gather_on(name))
        gathered(name, res)
        na_c = _ctx_attn_fwd(cfg, q, k, v, f"ctx_attn_fwd_{l}") if wc else None
        pool = _pool_fwd(cfg, u, band, inv, w_pool[l], ps, wc, f"pool_fwd_{l}")
        xs2, opre, mix = _tmpost_fwd(cfg, na_x, na_c, pool, W["wo", l], xs1, mods[l], gvec, f"tmpost_fwd_{l}")
        name = f"ffn_fwd_{l}_1"
        (xs3, hb3, z3, y3), res = _ffn_fwd(cfg, xs2, mods[l], gvec, W["gu", l, 1], W["dn", l, 1], 6, 4, wc, name, gather_on(name))
        gathered(name, res)
        saved.append(dict(xs=xs, xs1=xs1, xs2=xs2, hb1=hb1, z1=z1, y1=y1, hb2=hb2, q=q, k=k, v=v, u=u, mix=mix,
                          opre=opre, hb3=hb3, z3=z3, y3=y3, bexp=bexp, ps=ps, gvec=gvec))
        xs = xs3

    dxs, loss_blk = _loss_head(cfg, xs, target, "loss_head")

    small = [None] * depth
    for l in reversed(range(depth)):
        last = l == depth - 1
        wc = not last
        sv = saved[l]
        gvec = sv["gvec"]
        rows_b = cfg.T if wc else cfg.S
        name = f"ffn_bwd_{l}_1"
        (dxs2, dz, dyb, ab, dm678, dg45), lands = _ffn_bwd(cfg, dxs, sv["xs2"], sv["z3"], sv["y3"], mods[l], gvec,
                                                           W["gu", l, 1], W["dn", l, 1], 6, 4, wc, name, scatter_on(name))
        scattered(bwd_carry.get(name, ()), lands)
        wgrad(("gu", l, 1), sv["hb3"], dz, rows_b)
        wgrad(("dn", l, 1), ab, dyb, rows_b)
        dop, dmix, dm5, dg3 = _tmpost_bwd(cfg, dxs2, sv["opre"], W["wo", l], mods[l], gvec, wc, f"tmpost_bwd_{l}")
        wgrad(("wo", l), sv["mix"], dop, rows_b)
        du, dwp, dps = _pool_bwd(cfg, dmix, sv["u"], band, inv, w_pool[l], sv["ps"], wc, f"pool_bwd_{l}")
        name = f"na_bwd_{l}"
        (dq, dk, dv, dkc, dvc, dbexp), lands = _na_bwd(cfg, dmix, sv["q"], sv["k"], sv["v"], sv["bexp"], name, scatter_on(name))
        scattered(bwd_carry.get(name, ()), lands)
        drpb = _rpb_reduce(dbexp, flip, sel, f"rpb_reduce_{l}")
        if wc:
            dqc, dkc2, dvc2 = _ctx_attn_bwd(cfg, dmix, sv["q"], sv["k"], sv["v"], f"ctx_attn_bwd_{l}")
            ctx_terms = ([dqc], [dkc, dkc2], [dvc, dvc2])
        else:
            ctx_terms = ([], [dkc], [dvc])
        dproj = _rope_bwd_assemble(cfg, (dq, dk, dv, du), ctx_terms, wc, cos, sin, f"rope_bwd_{l}")
        wgrad(("wi", l), sv["hb2"], dproj, cfg.T)
        dxs1, dm34, dg2 = _tmpre_bwd(cfg, dproj, W["wi", l], sv["xs1"], mods[l], gvec, dxs2, wc, f"tmpre_bwd_{l}")
        name = f"ffn_bwd_{l}_0"
        (dxs, dz, dyb, ab, dm012, dg01), lands = _ffn_bwd(cfg, dxs1, sv["xs"], sv["z1"], sv["y1"], mods[l], gvec,
                                                          W["gu", l, 0], W["dn", l, 0], 0, 0, True, name, scatter_on(name))
        scattered(bwd_carry.get(name, ()), lands)
        wgrad(("gu", l, 0), sv["hb1"], dz, cfg.T)
        wgrad(("dn", l, 0), ab, dyb, cfg.T)
        if not wc:
            zero = lambda a: jnp.concatenate([a, jnp.zeros_like(a)], axis=0)
            dm5, dm678 = zero(dm5), zero(dm678)
        dmods = jnp.concatenate([dm012, dm34, dm5, dm678], axis=1)
        dgs = jnp.concatenate([dg01, dg2, dg3, dg45], axis=0)
        small[l] = dict(dmods=dmods, dg=dgs, drpb=drpb, dwp=dwp, dps=dps)
    lands = _comm_only(_scatter_comm([g_b16[k_] for k_ in last_scatter], [W_AXIS[k_[0]] for k_ in last_scatter]), "scatter_last")
    scattered(last_scatter, lands)
    kinds = ("gu", "dn", "wi", "wo")
    merged = _half_merge([G[k_] for k_ in kinds], "merge_halves")
    return loss_blk, dxs, dict(zip(kinds, merged)), small


def kernel(x, c, ctx, c_ctx, w_mod, b_mod, norm_g, w_ffn_gate_up, w_ffn_down, w_in, w_out, na_rpb, w_pool, pool_scale, loss_target, m_c_ctx, m_w_mod, m_b_mod, m_norm_g, m_w_ffn_gate_up, m_w_ffn_down, m_w_in, m_w_out, m_na_rpb, m_w_pool, m_pool_scale, v_c_ctx, v_w_mod, v_b_mod, v_norm_g, v_w_ffn_gate_up, v_w_ffn_down, v_w_in, v_w_out, v_na_rpb, v_w_pool, v_pool_scale):
    S, D = x.shape[1], x.shape[2]
    L = ctx.shape[1]
    depth = w_mod.shape[0]
    F = w_ffn_down.shape[2] * N_CHIPS
    nmod = w_mod.shape[2]
    gsh = norm_g.shape[2]
    cfg = _Cfg(S, L, D, F)
    mx, my, mc = _mesh_pos()
    chip = 2 * mx + my
    dev = 4 * mx + 2 * my + mc

    kidx = chip.astype(jnp.int32).reshape(1)
    dck = jnp.stack([dev, mc, chip]).astype(jnp.int32)
    W = {}
    for l in range(depth):
        for i in range(2):
            W["gu", l, i] = _cast_into_place(w_ffn_gate_up, (l, i), W_AXIS["gu"], kidx, f"cast_gu_{l}_{i}")
            W["dn", l, i] = _cast_into_place(w_ffn_down, (l, i), W_AXIS["dn"], kidx, f"cast_dn_{l}_{i}")
        W["wi", l] = _cast_into_place(w_in, (l,), W_AXIS["wi"], kidx, f"cast_wi_{l}")
        W["wo", l] = _cast_into_place(w_out, (l,), W_AXIS["wo"], kidx, f"cast_wo_{l}")
    first = [("gu", 0, 0), ("dn", 0, 0)]
    W.update(zip(first, _comm_only(_gather_comm([W[k_] for k_ in first], [W_AXIS[k_[0]] for k_ in first]), "gather_first")))
    G = {"gu": lax.empty(w_ffn_gate_up.shape, f32), "dn": lax.empty(w_ffn_down.shape, f32),
         "wi": lax.empty(w_in.shape, f32), "wo": lax.empty(w_out.shape, f32)}

    c_all = _all_gather_small(jnp.pad(c, ((0, 7), (0, 0))), "gather_c").reshape(N_DEV, 8, D)[:, 0]
    cvecs = jnp.concatenate([c_all, c_ctx[None], jnp.zeros((7, D), f32)], axis=0)
    b_shard = lax.dynamic_slice_in_dim(b_mod, chip * nmod, nmod, axis=1).reshape(depth, 1, nmod)
    m_part, silu_c = _modvec_fwd(cvecs, w_mod, b_shard, "modvec_fwd")
    m_all = _all_gather_small(m_part.reshape(depth * 16, nmod), "gather_mod").reshape(N_DEV, depth, 16, nmod)
    m_full = jnp.concatenate([m_all[2 * j] for j in range(N_CHIPS)], axis=-1)
    m_mine = lax.dynamic_index_in_dim(m_full, dev, axis=1, keepdims=False)
    mods = jnp.stack([m_mine, m_full[:, 8]], axis=1).reshape(depth, 2, N_MOD, D)

    norm_g_full = _all_gather_small(_pack_rows([norm_g]), "gather_norm_g")
    rows_g = norm_g_full.shape[0] // N_DEV
    ng = norm_g_full.reshape(N_DEV, rows_g * LANES)[:, :norm_g.size].reshape(N_DEV, depth, 6, gsh)
    norm_g_all = jnp.concatenate([ng[2 * j] for j in range(N_CHIPS)], axis=-1)
    xs0 = jnp.concatenate([x[0], ctx[0]], axis=0)
    loss_blk, dxs0, wgrads, small = _local_step(cfg, xs0, loss_target[0], mods, norm_g_all, W, G, dck,
                                                na_rpb, w_pool, pool_scale)
    loss = lax.psum(loss_blk[0, 0], ("x", "y", "c"))
    grad_x = dxs0[:S][None]

    g_gu, g_dn, g_wi, g_wo = wgrads["gu"], wgrads["dn"], wgrads["wi"], wgrads["wo"]
    names = ("dmods", "dg", "drpb", "dwp", "dps")
    parts = [jnp.stack([small[l][n] for l in range(depth)]) for n in names]
    shapes = [p.shape for p in parts]
    packed = _pack_rows(parts)
    gathered = _all_gather_small(packed, "gather_small").reshape(N_DEV, packed.shape[0], LANES)
    total = _unpack_rows(_sum_devices(gathered, "sum_small"), shapes)
    dmods_sum, dg_sum, drpb_sum, dwp_sum, dps_sum = total
    dmods_each = jnp.stack([_unpack_rows(gathered[j], shapes[:1])[0] for j in range(N_DEV)])
    dm_rows = jnp.concatenate([jnp.transpose(dmods_each[:, :, 0], (1, 0, 2, 3)).reshape(depth, N_DEV, N_MOD * D),
                               dmods_sum[:, 1].reshape(depth, 1, N_MOD * D),
                               jnp.zeros((depth, 7, N_MOD * D), f32)], axis=1)
    dm_shard = lax.dynamic_slice_in_dim(dm_rows, chip * nmod, nmod, axis=2)
    grad_w_mod, gc_part = _modvec_bwd(silu_c.T, dm_shard, w_mod, "modvec_bwd")
    gc_all = _all_gather_small(gc_part.reshape(depth * 8, D), "gather_gc").reshape(N_DEV, depth, 8, D)
    grad_b_mod, grad_c_ctx = _small_finish(dm_rows, gc_all, c_ctx)
    grad_norm_g = lax.dynamic_slice_in_dim(dg_sum, chip * gsh, gsh, axis=2)
    grad_na_rpb = drpb_sum[:, :, :2 * NA_KH - 1, :2 * NA_KW - 1]
    grad_w_pool = dwp_sum
    grad_pool_scale = dps_sum.reshape(depth, POOL_WIDTH)

    grads = [grad_c_ctx, grad_w_mod, grad_b_mod, grad_norm_g, g_gu, g_dn, g_wi, g_wo, grad_na_rpb, grad_w_pool, grad_pool_scale]
    ws = [c_ctx, w_mod, b_mod, norm_g, w_ffn_gate_up, w_ffn_down, w_in, w_out, na_rpb, w_pool, pool_scale]
    ms = [m_c_ctx, m_w_mod, m_b_mod, m_norm_g, m_w_ffn_gate_up, m_w_ffn_down, m_w_in, m_w_out, m_na_rpb, m_w_pool, m_pool_scale]
    vs = [v_c_ctx, v_w_mod, v_b_mod, v_norm_g, v_w_ffn_gate_up, v_w_ffn_down, v_w_in, v_w_out, v_na_rpb, v_w_pool, v_pool_scale]
    tags = ["c_ctx", "w_mod", "b_mod", "norm_g", "gate_up", "down", "w_in", "w_out", "na_rpb", "w_pool", "pool_scale"]
    upd = [_adamw(w_, g_, m_, v_, f"adamw_{t}") for w_, g_, m_, v_, t in zip(ws, grads, ms, vs, tags)]
    return (loss, grad_x, *grads, *[u_[0] for u_ in upd], *[u_[1] for u_ in upd], *[u_[2] for u_ in upd])


def _small_finish(dm_rows, gc_all, c_ctx):
    depth, _, n = dm_rows.shape
    D = c_ctx.shape[0]

    def body(dm_ref, gc_ref, c_ref, gb_ref, gcx_ref):
        acc = dm_ref[:, 0]
        for j in range(1, N_DEV + 1):
            acc = acc + dm_ref[:, j]
        gb_ref[...] = acc
        t = jnp.zeros((1, D), f32)
        for l in range(depth):
            for j in range(N_CHIPS):
                t = t + gc_ref[2 * j, l, 0:1, :]
        cv = c_ref[...]
        sg = _sigmoid(cv)
        gcx_ref[...] = t * (sg * (1.0 + cv * (1.0 - sg)))

    gb, gcx = pl.pallas_call(
        body, name="small_finish",
        out_shape=[S_((depth, n), f32), S_((1, D), f32)],
        compiler_params=_cp(VMEM_MID),
    )(dm_rows, gc_all, c_ctx.reshape(1, D))
    return gb, gcx.reshape(D)
```

```python
import functools

import numpy as np
import jax
import jax.numpy as jnp
from jax import lax
from jax.experimental import pallas as pl
from jax.experimental.pallas import tpu as pltpu

f32, bf16 = jnp.float32, jnp.bfloat16

GRID_W = 64
N_MOD = 9
NA_HEADS = 8
HEAD_DIM = 64
NA_WIDTH = NA_HEADS * HEAD_DIM
NA_KH = 8
NA_KW = 16
POOL_GROUPS = 4
POOL_CH = 128
POOL_WIDTH = POOL_GROUPS * POOL_CH
POOL_WINDOWS = (2, 4, 8, 16)
IN_WIDTH = 3 * NA_WIDTH + POOL_WIDTH
MIX_WIDTH = NA_WIDTH + POOL_WIDTH
ROPE_THETA = 10000.0
ROPE_PAIRS = HEAD_DIM // 4
RMS_EPS = 1e-6
NEG_INF = -1e30
ADAM_LR, ADAM_B1, ADAM_B2, ADAM_EPS, ADAM_WD, ADAM_STEP = 0.001, 0.9, 0.999, 1e-08, 0.01, 10

N_DEV = 8
N_CHIPS = 4
LANES = 128
MIB = 1024 * 1024
VMEM_BIG = 52 * MIB
VMEM_MID = 40 * MIB
MESH = pl.DeviceIdType.MESH
ANY = pl.BlockSpec(memory_space=pl.ANY)
S_ = jax.ShapeDtypeStruct


def _cp(vmem=VMEM_MID, sem=None):
    return pltpu.CompilerParams(vmem_limit_bytes=vmem, dimension_semantics=sem)


def _sigmoid(x):
    return 1.0 / (1.0 + jnp.exp(-x))


def _rms_hat(x):
    rinv = lax.rsqrt(jnp.mean(x * x, axis=-1, keepdims=True) + RMS_EPS)
    return x * rinv, rinv


def _rms_bwd(dxhat, xhat, rinv):
    return rinv * (dxhat - xhat * jnp.mean(dxhat * xhat, axis=-1, keepdims=True))


def _rsum(a):
    return jnp.sum(a, axis=0, keepdims=True)


def _nt(a, b):
    return lax.dot_general(a, b, (((1,), (1,)), ((), ())), preferred_element_type=f32)


def _tn(a, b):
    return lax.dot_general(a, b, (((0,), (0,)), ((), ())), preferred_element_type=f32)


def _nn(a, b):
    return jnp.dot(a, b, preferred_element_type=f32)


def _swap16(x):
    lane = lax.broadcasted_iota(jnp.int32, x.shape, 1)
    n = x.shape[1]
    return jnp.where((lane % 32) < 16, pltpu.roll(x, n - 16, 1), pltpu.roll(x, 16, 1))


def _rope_tables(s_len, l_len):
    t = np.arange(s_len)
    inv = ROPE_THETA ** (-np.arange(ROPE_PAIRS, dtype=np.float32) / ROPE_PAIRS)
    ang_r = (t // GRID_W).astype(np.float32)[:, None] * inv
    ang_c = (t % GRID_W).astype(np.float32)[:, None] * inv
    cos = np.concatenate([np.cos(ang_r), np.cos(ang_r), np.cos(ang_c), np.cos(ang_c)], axis=-1)
    sin = np.concatenate([-np.sin(ang_r), np.sin(ang_r), -np.sin(ang_c), np.sin(ang_c)], axis=-1)
    cos = np.concatenate([cos, np.ones((l_len, HEAD_DIM), np.float32)], axis=0)
    sin = np.concatenate([sin, np.zeros((l_len, HEAD_DIM), np.float32)], axis=0)
    return (jnp.asarray(np.tile(cos, (1, 2)), f32), jnp.asarray(np.tile(sin, (1, 2)), f32))


def _pool_tables(tm, l_len):
    band = np.zeros((2, POOL_GROUPS, tm, tm), np.float32)
    inv = np.zeros((2, POOL_GROUPS, tm, 1), np.float32)
    for typ, length in ((0, GRID_W), (1, l_len)):
        for g, w in enumerate(POOL_WINDOWS):
            for t in range(tm):
                base, p = (t // length) * length, t % length
                lo = min(max(p - w // 2, 0), length)
                hi = min(max(p - w // 2 + w, 0), length)
                band[typ, g, t, base + lo:base + hi] = 1.0
                inv[typ, g, t, 0] = 1.0 / (hi - lo)
    return jnp.asarray(band, bf16), jnp.asarray(inv, f32)


NA_QR = 4
NA_WR = NA_KH + NA_QR - 1
NA_TYPES = 3
NA_SEL_ROWS = 136
NA_WPAD = 768


def _rpb_index_tables():
    j = np.arange(GRID_W)
    col_start = np.clip(j - NA_KW // 2, 0, GRID_W - NA_KW)
    valid = (j[None, :] >= col_start[:, None]) & (j[None, :] < col_start[:, None] + NA_KW)
    dc = np.clip(j[None, :] - j[:, None] + NA_KW - 1, 0, 2 * NA_KW - 2)
    i = np.arange(NA_QR)[:, None]
    kk = np.arange(NA_WR)[None, :]
    off = np.stack([np.zeros_like(i), i, np.full_like(i, NA_QR - 1)])
    d = np.stack([kk - i + NA_KH - 1, kk - i + NA_KH - 1 - NA_QR, kk - i])
    row_ok = (kk[None] >= off) & (kk[None] < off + NA_KH)
    assert (d[row_ok] >= 0).all() and (d[row_ok] <= 2 * NA_KH - 2).all()
    return valid, dc, d, row_ok


def _expand_rpb(rpb, name):
    _, _, d, row_ok = _rpb_index_tables()
    heads, nd, ne = rpb.shape
    w = GRID_W
    v = jnp.pad(rpb, ((0, 0), (0, 0), (w - NA_KW, 2 * w - (w - NA_KW) - ne)))
    x = jnp.broadcast_to(v[:, :, None, :], (heads, nd, w, 2 * w)).reshape(heads, nd, 2 * w * w)
    t = x[:, :, :w * (2 * w - 1)].reshape(heads, nd, w, 2 * w - 1)[..., w - 1:]

    def body(t_ref, o_ref):
        q = lax.broadcasted_iota(jnp.int32, (w, w), 0)
        c = lax.broadcasted_iota(jnp.int32, (w, w), 1)
        c0 = jnp.clip(q - NA_KW // 2, 0, w - NA_KW)
        in_cols = (c >= c0) & (c < c0 + NA_KW)
        outside = jnp.full((w, w), NEG_INF, f32)
        blocks = [jnp.where(in_cols, t_ref[dd], NEG_INF) for dd in range(nd)]
        for typ in range(NA_TYPES):
            for i in range(NA_QR):
                row = [blocks[d[typ, i, kk]] if row_ok[typ, i, kk] else outside for kk in range(NA_WR)]
                o_ref[typ, i * w:(i + 1) * w, :] = jnp.concatenate(row, axis=1)

    return pl.pallas_call(
        body, name=name, grid=(heads,),
        in_specs=[pl.BlockSpec((None, nd, w, w), lambda h: (h, 0, 0, 0))],
        out_specs=pl.BlockSpec((NA_TYPES, None, NA_QR * w, NA_WR * w), lambda h: (0, h, 0, 0)),
        out_shape=S_((NA_TYPES, heads, NA_QR * w, NA_WR * w), f32),
        compiler_params=_cp(VMEM_MID, ("arbitrary",)),
    )(t)


def _rpb_reduce_tables():
    _, _, d, row_ok = _rpb_index_tables()
    flip = np.eye(GRID_W, dtype=np.float32)[::-1].copy()
    sel = np.zeros((16, NA_SEL_ROWS), np.float32)
    flat_d, flat_ok = d.reshape(-1), row_ok.reshape(-1)
    for n in range(flat_d.size):
        if flat_ok[n]:
            sel[flat_d[n], n] = 1.0
    return jnp.asarray(flip), jnp.asarray(sel)


class _Cfg:
    def __init__(self, s_len, l_len, d, f):
        self.S, self.L, self.D, self.F = s_len, l_len, d, f
        self.T = s_len + l_len
        self.TM = 256 if l_len % 256 == 0 else 128
        assert l_len == self.TM, "context length must equal the row tile"
        assert s_len % self.TM == 0 and s_len % GRID_W == 0
        self.nxt = s_len // self.TM
        self.ntt = self.T // self.TM
        self.rows = s_len // GRID_W
        assert self.rows >= 2 * NA_KH
        assert f % (2 * LANES) == 0
        self.FC = f // 2

    def ntiles(self, with_ctx):
        return self.ntt if with_ctx else self.nxt


def _typ(cfg):
    return lambda i: (jnp.minimum(i // cfg.nxt, 1), 0, 0)


def _mesh_pos():
    return lax.axis_index("x"), lax.axis_index("y"), lax.axis_index("c")


class _Comm:
    def __init__(self, ins, outs, alias, nsem, start, finish):
        self.ins, self.outs, self.alias, self.nsem, self.start, self.finish = ins, outs, alias, nsem, start, finish


def _call(body, args, comm=None, *, grid, in_specs, out_specs, out_shape, scratch_shapes=(), **kw):
    if comm is None:
        return pl.pallas_call(body, grid=grid, in_specs=list(in_specs), out_specs=list(out_specs), out_shape=list(out_shape),
                              scratch_shapes=list(scratch_shapes), **kw)(*args), ()
    n_in, n_out, n_sc = len(in_specs), len(out_specs), len(scratch_shapes)
    ci, co = len(comm.ins), len(comm.outs)

    def carrier(*refs):
        bounds = np.cumsum([0, n_in, ci, n_out, co, n_sc])
        ins, cins, outs, couts, scr = (refs[a:b] for a, b in zip(bounds[:-1], bounds[1:]))
        send, recv = refs[bounds[-1]], refs[bounds[-1] + 1]
        first = functools.reduce(jnp.logical_and, [pl.program_id(a) == 0 for a in range(len(grid))])
        last = functools.reduce(jnp.logical_and, [pl.program_id(a) == g - 1 for a, g in enumerate(grid)])

        @pl.when(first)
        def _():
            comm.start(cins, couts, send, recv)

        body(*ins, *outs, *scr)

        @pl.when(last)
        def _():
            comm.finish(cins, couts, send, recv)

    res = pl.pallas_call(
        carrier, grid=grid, in_specs=list(in_specs) + [ANY] * ci, out_specs=list(out_specs) + [ANY] * co,
        out_shape=list(out_shape) + list(comm.outs),
        input_output_aliases={n_in + a: n_out + b for a, b in comm.alias.items()},
        scratch_shapes=list(scratch_shapes) + [pltpu.SemaphoreType.DMA((comm.nsem,)), pltpu.SemaphoreType.DMA((comm.nsem,))],
        **kw)(*args, *comm.ins)
    return res[:n_out], res[n_out:]


def _comm_only(comm, name):
    ci, co = len(comm.ins), len(comm.outs)

    def body(*refs):
        cins, couts = refs[:ci], refs[ci:ci + co]
        send, recv = refs[ci + co], refs[ci + co + 1]
        comm.start(cins, couts, send, recv)
        comm.finish(cins, couts, send, recv)

    return pl.pallas_call(
        body, name=name, in_specs=[ANY] * ci, out_specs=[ANY] * co, out_shape=list(comm.outs),
        input_output_aliases=dict(comm.alias),
        scratch_shapes=[pltpu.SemaphoreType.DMA((comm.nsem,)), pltpu.SemaphoreType.DMA((comm.nsem,))],
        compiler_params=_cp(VMEM_MID),
    )(*comm.ins)


def _half_view(ref, axis, kk, h):
    r, c = ref.shape
    if axis == 1:
        n = c // N_CHIPS
        return ref.at[pl.ds(h * (r // 2), r // 2), pl.ds(pl.multiple_of(kk * n, LANES), n)]
    n = r // N_CHIPS
    return ref.at[pl.ds(pl.multiple_of(kk * n + h * (n // 2), 8), n // 2), :]


def _other_chips(x, y):
    return [(1 - x, y), (x, 1 - y), (1 - x, 1 - y)]


def _gather_comm(arrs, axes):
    n = len(arrs)

    def copy(ref, view, sems, k, to):
        send, recv = sems
        return pltpu.make_async_remote_copy(src_ref=view, dst_ref=view, send_sem=send.at[k], recv_sem=recv.at[k],
                                            device_id=to, device_id_type=MESH)

    def start(cins, bufs, send, recv):
        x, y, c = _mesh_pos()
        for t in range(n):
            own = _half_view(bufs[t], axes[t], 2 * x + y, c)
            for j, chip in enumerate(_other_chips(x, y)):
                copy(bufs[t], own, (send, recv), 6 * t + j, (*chip, c)).start()

    def finish(cins, bufs, send, recv):
        x, y, c = _mesh_pos()
        sibling = (x, y, 1 - c)
        chips = _other_chips(x, y)
        for t in range(n):
            for j, chip in enumerate(chips):
                landed = _half_view(bufs[t], axes[t], 2 * chip[0] + chip[1], c)
                copy(bufs[t], landed, (send, recv), 6 * t + j, (*chip, c)).wait_recv()
                copy(bufs[t], landed, (send, recv), 6 * t + 3 + j, sibling).start()
        for t in range(n):
            own = _half_view(bufs[t], axes[t], 2 * x + y, c)
            for j, chip in enumerate(chips):
                kj = 2 * chip[0] + chip[1]
                copy(bufs[t], _half_view(bufs[t], axes[t], kj, 1 - c), (send, recv), 6 * t + 3 + j, sibling).wait_recv()
                copy(bufs[t], own, (send, recv), 6 * t + j, (*chip, c)).wait_send()
                copy(bufs[t], _half_view(bufs[t], axes[t], kj, c), (send, recv), 6 * t + 3 + j, sibling).wait_send()

    return _Comm(list(arrs), [S_(a.shape, a.dtype) for a in arrs], {t: t for t in range(n)}, 6 * n, start, finish)


def _scatter_comm(parts, axes):
    n = len(parts)
    peers = [(fx, fy, fc) for fx in (0, 1) for fy in (0, 1) for fc in (0, 1)][1:]

    def half_shape(a, axis):
        r, c = a.shape
        return (r // 2, c // N_CHIPS) if axis == 1 else (r // N_CHIPS // 2, c)

    def start(srcs, lands, send, recv):
        x, y, c = _mesh_pos()
        me = 4 * x + 2 * y + c
        for t in range(n):
            for r_, (fx, fy, fc) in enumerate(peers):
                dx, dy, dc = (1 - x if fx else x), (1 - y if fy else y), (1 - c if fc else c)
                pltpu.make_async_remote_copy(
                    src_ref=_half_view(srcs[t], axes[t], 2 * dx + dy, dc), dst_ref=lands[t].at[me],
                    send_sem=send.at[7 * t + r_], recv_sem=recv.at[7 * t + r_],
                    device_id=(dx, dy, dc), device_id_type=MESH).start()

    def finish(srcs, lands, send, recv):
        x, y, c = _mesh_pos()
        for t in range(n):
            mine = _half_view(srcs[t], axes[t], 2 * x + y, c)
            for r_, (fx, fy, fc) in enumerate(peers):
                sx, sy, sc = (1 - x if fx else x), (1 - y if fy else y), (1 - c if fc else c)
                cp = pltpu.make_async_remote_copy(
                    src_ref=mine, dst_ref=lands[t].at[4 * sx + 2 * sy + sc],
                    send_sem=send.at[7 * t + r_], recv_sem=recv.at[7 * t + r_],
                    device_id=(sx, sy, sc), device_id_type=MESH)
                cp.wait_recv()
                cp.wait_send()

    return _Comm(list(parts), [S_((N_DEV,) + half_shape(a, ax), a.dtype) for a, ax in zip(parts, axes)], {}, 7 * n, start, finish)


def _ffn_fwd(cfg, xs, mods, gvec, wgu, wd, mi, gi, with_ctx, name, comm=None, xs_ctx=None, loss_target=None):
    TM, D, F, FC = cfg.TM, cfg.D, cfg.F, cfg.FC
    nt = cfg.ntiles(with_ctx)
    R = nt * TM
    split, head = xs_ctx is not None, loss_target is not None

    def body(*refs):
        it = iter(refs)
        xs_ref = next(it)
        xc_ref = next(it) if split else None
        mods_ref, g_ref, wgu_hbm, wd_hbm = next(it), next(it), next(it), next(it)
        t_ref = next(it) if head else None
        out_ref, hb_ref, z_ref, y_ref = next(it), next(it), next(it), next(it)
        loss_ref = next(it) if head else None
        wgu_v, wd_v, sem = next(it), next(it), next(it)
        i = pl.program_id(0)

        @pl.when(i == 0)
        def _():
            c0 = pltpu.make_async_copy(wgu_hbm, wgu_v, sem.at[0])
            c1 = pltpu.make_async_copy(wd_hbm, wd_v, sem.at[1])
            c0.start(); c1.start(); c0.wait(); c1.wait()
            if head:
                loss_ref[...] = jnp.zeros_like(loss_ref)
        x = xs_ref[...]
        if split:
            x = jnp.where(i < cfg.nxt, x, xc_ref[...])
        m = mods_ref[0]
        sh, sc, gt = m[mi:mi + 1], m[mi + 1:mi + 2], m[mi + 2:mi + 3]
        xhat, _ = _rms_hat(x)
        h = (xhat * g_ref[gi:gi + 1]) * (1.0 + sc) + sh
        hb = h.astype(bf16)
        hb_ref[...] = hb
        y = jnp.zeros((TM, D), f32)
        for ch in range(F // FC):
            zg = _nn(hb, wgu_v[:, ch * FC:(ch + 1) * FC])
            zu = _nn(hb, wgu_v[:, F + ch * FC:F + (ch + 1) * FC])
            z_ref[:, ch * FC:(ch + 1) * FC] = zg.astype(bf16)
            z_ref[:, F + ch * FC:F + (ch + 1) * FC] = zu.astype(bf16)
            a = (zg * _sigmoid(zg)) * zu
            y = y + _nn(a.astype(bf16), wd_v[ch * FC:(ch + 1) * FC, :])
        y_ref[...] = y
        yhat, _ = _rms_hat(y)
        out = x + 0.5 * gt * (yhat * g_ref[gi + 1:gi + 2])
        if head:
            e = out - t_ref[...]
            out_ref[...] = e * (1.0 / D)
            loss_ref[...] += jnp.sum(jnp.mean(e * e, axis=-1, keepdims=True), axis=0, keepdims=True) * 0.5
        else:
            out_ref[...] = out

    rt = lambda c: pl.BlockSpec((TM, c), lambda i: (i, 0))
    lat = pl.BlockSpec((TM, D), lambda i: (jnp.minimum(i, cfg.nxt - 1), 0))
    x_specs, x_args = ([lat, pl.BlockSpec((TM, D), lambda i: (0, 0))], [xs, xs_ctx]) if split else ([rt(D)], [xs])
    t_specs, t_args = ([rt(D)], [loss_target]) if head else ([], [])
    l_specs, l_shape = ([pl.BlockSpec((8, LANES), lambda i: (0, 0))], [S_((8, LANES), f32)]) if head else ([], [])
    return _call(
        body, (*x_args, mods, gvec, wgu, wd, *t_args), comm, name=name, grid=(nt,),
        in_specs=x_specs + [pl.BlockSpec((1, N_MOD, D), _typ(cfg)), pl.BlockSpec((6, D), lambda i: (0, 0)), ANY, ANY] + t_specs,
        out_specs=[rt(D), rt(D), rt(2 * F), rt(D)] + l_specs,
        out_shape=[S_((R, D), f32), S_((R, D), bf16), S_((R, 2 * F), bf16), S_((R, D), f32)] + l_shape,
        scratch_shapes=[pltpu.VMEM((D, 2 * F), bf16), pltpu.VMEM((F, D), bf16), pltpu.SemaphoreType.DMA((2,))],
        compiler_params=_cp(VMEM_BIG, ("arbitrary",)),
    )


def _ffn_bwd(cfg, dout, xs, z, y, mods, gvec, wgu, wd, mi, gi, with_ctx, name, comm=None, xs_ctx=None):
    TM, D, F, FC = cfg.TM, cfg.D, cfg.F, cfg.FC
    nt = cfg.ntiles(with_ctx)
    R = nt * TM
    ntyp = 2 if with_ctx else 1
    split = xs_ctx is not None

    def body(*refs):
        it = iter(refs)
        do_ref, xs_ref = next(it), next(it)
        xc_ref = next(it) if split else None
        z_ref, y_ref, mods_ref, g_ref, wgu_hbm, wd_hbm = (next(it) for _ in range(6))
        dx_ref, dz_ref, dy_ref, a_ref, dm_ref, dg_ref, wgu_v, wd_v, sem = (next(it) for _ in range(9))
        i = pl.program_id(0)

        @pl.when(i == 0)
        def _():
            c0 = pltpu.make_async_copy(wgu_hbm, wgu_v, sem.at[0])
            c1 = pltpu.make_async_copy(wd_hbm, wd_v, sem.at[1])
            c0.start(); c1.start(); c0.wait(); c1.wait()
            dg_ref[...] = jnp.zeros_like(dg_ref)

        @pl.when((i == 0) | (i == cfg.nxt))
        def _():
            dm_ref[...] = jnp.zeros_like(dm_ref)

        do = do_ref[...]
        x = xs_ref[...]
        if split:
            x = jnp.where(i < cfg.nxt, x, xc_ref[...])
        m = mods_ref[0]
        sc, gt = m[mi + 1:mi + 2], m[mi + 2:mi + 3]
        g_pre, g_post = g_ref[gi:gi + 1], g_ref[gi + 1:gi + 2]
        xhat, rinv0 = _rms_hat(x)
        n0 = xhat * g_pre
        yhat, rinv1 = _rms_hat(y_ref[...])
        d_gt = _rsum(0.5 * do * (yhat * g_post))
        dr = (0.5 * gt) * do
        dg_post = _rsum(dr * yhat)
        dy = _rms_bwd(dr * g_post, yhat, rinv1)
        dyb = dy.astype(bf16)
        dy_ref[...] = dyb
        dh = jnp.zeros((TM, D), f32)
        for ch in range(F // FC):
            zg = z_ref[:, ch * FC:(ch + 1) * FC].astype(f32)
            zu = z_ref[:, F + ch * FC:F + (ch + 1) * FC].astype(f32)
            sg = _sigmoid(zg)
            silu = zg * sg
            a_ref[:, ch * FC:(ch + 1) * FC] = (silu * zu).astype(bf16)
            da = _nt(dyb, wd_v[ch * FC:(ch + 1) * FC, :])
            dzu = (da * silu).astype(bf16)
            dzg = (da * zu * (sg * (1.0 + zg * (1.0 - sg)))).astype(bf16)
            dz_ref[:, ch * FC:(ch + 1) * FC] = dzg
            dz_ref[:, F + ch * FC:F + (ch + 1) * FC] = dzu
            dh = dh + _nt(dzg, wgu_v[:, ch * FC:(ch + 1) * FC]) + _nt(dzu, wgu_v[:, F + ch * FC:F + (ch + 1) * FC])
        d_sh = _rsum(dh)
        d_sc = _rsum(dh * n0)
        dn = dh * (1.0 + sc)
        dg_pre = _rsum(dn * xhat)
        dx = do + _rms_bwd(dn * g_pre, xhat, rinv0)
        if split:
            @pl.when(i < cfg.nxt)
            def _():
                dx_ref[...] = dx
        else:
            dx_ref[...] = dx
        dm_ref[0] += jnp.concatenate([d_sh, d_sc, d_gt], axis=0)
        dg_ref[...] += jnp.concatenate([dg_pre, dg_post], axis=0)

    rt = lambda c: pl.BlockSpec((TM, c), lambda i: (i, 0))
    lat = pl.BlockSpec((TM, D), lambda i: (jnp.minimum(i, cfg.nxt - 1), 0))
    x_specs, x_args = ([lat, pl.BlockSpec((TM, D), lambda i: (0, 0))], [xs, xs_ctx]) if split else ([rt(D)], [xs])
    return _call(
        body, (dout, *x_args, z, y, mods, gvec, wgu, wd), comm, name=name, grid=(nt,),
        in_specs=[rt(D)] + x_specs + [rt(2 * F), rt(D), pl.BlockSpec((1, N_MOD, D), _typ(cfg)),
                                       pl.BlockSpec((6, D), lambda i: (0, 0)), ANY, ANY],
        out_specs=[lat if split else rt(D), rt(2 * F), rt(D), rt(F), pl.BlockSpec((1, 3, D), _typ(cfg)),
                   pl.BlockSpec((2, D), lambda i: (0, 0))],
        out_shape=[S_((cfg.S if split else R, D), f32), S_((R, 2 * F), bf16), S_((R, D), bf16), S_((R, F), bf16),
                   S_((ntyp, 3, D), f32), S_((2, D), f32)],
        scratch_shapes=[pltpu.VMEM((D, 2 * F), bf16), pltpu.VMEM((F, D), bf16), pltpu.SemaphoreType.DMA((2,))],
        compiler_params=_cp(VMEM_BIG, ("arbitrary",)),
    )


def _wgrad(a, b, k_rows, name, comm=None):
    M, N = a.shape[1], b.shape[1]
    tn = N
    for cand in (1408, 1024, 512):
        if N % cand == 0 and N > cand:
            tn = cand
            break
    room = VMEM_BIG - 6 * MIB - 2 * M * tn * 6
    tk = _div_tile(k_rows, 1, min(2816, room // (4 * (M + tn))), LANES)
    nk = k_rows // tk

    def body(a_ref, b_ref, o_ref, ob_ref):
        k = pl.program_id(1)

        @pl.when(k == 0)
        def _():
            o_ref[...] = jnp.zeros_like(o_ref)
        o_ref[...] += _tn(a_ref[...], b_ref[...])

        @pl.when(k == nk - 1)
        def _():
            ob_ref[...] = o_ref[...].astype(bf16)

    ospec = pl.BlockSpec((M, tn), lambda n, k: (0, n))
    return _call(
        body, (a, b), comm, name=name, grid=(N // tn, nk),
        in_specs=[pl.BlockSpec((tk, M), lambda n, k: (k, 0)), pl.BlockSpec((tk, tn), lambda n, k: (k, n))],
        out_specs=[ospec, ospec], out_shape=[S_((M, N), f32), S_((M, N), bf16)],
        compiler_params=_cp(VMEM_BIG, ("arbitrary", "arbitrary")),
    )


def _tmpre_fwd(cfg, xs, mods, gvec, w_in, cos, sin, name):
    TM, D = cfg.TM, cfg.D
    nt, R = cfg.ntt, cfg.T
    W = NA_WIDTH

    def body(xs_ref, mods_ref, g_ref, w_ref, cos_ref, sin_ref, hb_ref, q_ref, k_ref, v_ref, u_ref):
        x = xs_ref[...]
        m = mods_ref[0]
        xhat, _ = _rms_hat(x)
        hb = ((xhat * g_ref[2:3]) * (1.0 + m[4:5]) + m[3:4]).astype(bf16)
        hb_ref[...] = hb
        p = _nn(hb, w_ref[...])
        cs = jnp.tile(cos_ref[...], (1, W // LANES))
        sn = jnp.tile(sin_ref[...], (1, W // LANES))
        q = p[:, 0:W]
        k = p[:, W:2 * W]
        q_ref[...] = ((q * cs + _swap16(q) * sn) * (HEAD_DIM ** -0.5)).astype(bf16)
        k_ref[...] = (k * cs + _swap16(k) * sn).astype(bf16)
        v_ref[...] = p[:, 2 * W:3 * W].astype(bf16)
        u_ref[...] = p[:, 3 * W:]

    rt = lambda c: pl.BlockSpec((TM, c), lambda i: (i, 0))
    return pl.pallas_call(
        body, name=name, grid=(nt,),
        in_specs=[rt(D), pl.BlockSpec((1, N_MOD, D), _typ(cfg)), pl.BlockSpec((6, D), lambda i: (0, 0)),
                  pl.BlockSpec((D, IN_WIDTH), lambda i: (0, 0)), rt(LANES), rt(LANES)],
        out_specs=[rt(D), rt(W), rt(W), rt(W), rt(POOL_WIDTH)],
        out_shape=[S_((R, D), bf16), S_((R, W), bf16), S_((R, W), bf16), S_((R, W), bf16), S_((R, POOL_WIDTH), f32)],
        compiler_params=_cp(VMEM_MID, ("arbitrary",)),
    )(xs, mods, gvec, w_in, cos, sin)


def _rope_bwd_assemble(cfg, lat, ctx_terms, du_has_ctx, cos, sin, name):
    TM = cfg.TM
    W = NA_WIDTH
    n_ctx = [len(t) for t in ctx_terms]
    flat_ctx = [a for t in ctx_terms for a in t]

    def body(*refs):
        dq_ref, dk_ref, dv_ref, du_ref = refs[:4]
        ctx_refs = refs[4:4 + len(flat_ctx)]
        cos_ref, sin_ref, o_ref = refs[4 + len(flat_ctx):]
        is_ctx = pl.program_id(0) >= cfg.nxt
        vals, off = [], 0
        for lat_ref, n in zip((dq_ref, dk_ref, dv_ref), n_ctx):
            cv = jnp.zeros((TM, W), f32)
            for r_ in ctx_refs[off:off + n]:
                cv = cv + r_[...]
            off += n
            vals.append(jnp.where(is_ctx, cv, lat_ref[...]))
        du_ = du_ref[...] if du_has_ctx else jnp.where(is_ctx, 0.0, du_ref[...])
        cs = jnp.tile(cos_ref[...], (1, W // LANES))
        sn = jnp.tile(sin_ref[...], (1, W // LANES))
        dq_ = vals[0] * (HEAD_DIM ** -0.5)
        dk_ = vals[1]
        o_ref[:, 0:W] = (dq_ * cs + _swap16(dq_ * sn)).astype(bf16)
        o_ref[:, W:2 * W] = (dk_ * cs + _swap16(dk_ * sn)).astype(bf16)
        o_ref[:, 2 * W:3 * W] = vals[2].astype(bf16)
        o_ref[:, 3 * W:] = du_.astype(bf16)

    rt = lambda c: pl.BlockSpec((TM, c), lambda i: (i, 0))
    lat_spec = pl.BlockSpec((TM, W), lambda i: (jnp.minimum(i, cfg.nxt - 1), 0))
    du_spec = rt(POOL_WIDTH) if du_has_ctx else lat_spec
    return pl.pallas_call(
        body, name=name, grid=(cfg.ntt,),
        in_specs=[lat_spec, lat_spec, lat_spec, du_spec] + [pl.BlockSpec((TM, W), lambda i: (0, 0))] * len(flat_ctx)
                 + [rt(LANES), rt(LANES)],
        out_specs=rt(IN_WIDTH), out_shape=S_((cfg.T, IN_WIDTH), bf16),
        compiler_params=_cp(VMEM_MID, ("arbitrary",)),
    )(*lat, *flat_ctx, cos, sin)


def _tmpre_bwd(cfg, dproj, w_in, xs, mods, gvec, dres, res_with_ctx, name):
    TM, D = cfg.TM, cfg.D
    nt, R = cfg.ntt, cfg.T
    nres = cfg.ntiles(res_with_ctx)

    def body(dp_ref, w_ref, xs_ref, mods_ref, g_ref, dres_ref, dx_ref, dm_ref, dg_ref):
        i = pl.program_id(0)

        @pl.when(i == 0)
        def _():
            dg_ref[...] = jnp.zeros_like(dg_ref)

        @pl.when((i == 0) | (i == cfg.nxt))
        def _():
            dm_ref[...] = jnp.zeros_like(dm_ref)

        dh = _nt(dp_ref[...], w_ref[...])
        x = xs_ref[...]
        m = mods_ref[0]
        g2 = g_ref[2:3]
        xhat, rinv = _rms_hat(x)
        d_sh = _rsum(dh)
        d_sc = _rsum(dh * (xhat * g2))
        dn = dh * (1.0 + m[4:5])
        dg_ref[...] += _rsum(dn * xhat)
        dx = _rms_bwd(dn * g2, xhat, rinv)
        res = dres_ref[...]
        if nres < nt:
            res = jnp.where(i < nres, res, 0.0)
        dx_ref[...] = res + dx
        dm_ref[0] += jnp.concatenate([d_sh, d_sc], axis=0)

    rt = lambda c: pl.BlockSpec((TM, c), lambda i: (i, 0))
    return pl.pallas_call(
        body, name=name, grid=(nt,),
        in_specs=[rt(IN_WIDTH), pl.BlockSpec((D, IN_WIDTH), lambda i: (0, 0)), rt(D),
                  pl.BlockSpec((1, N_MOD, D), _typ(cfg)), pl.BlockSpec((6, D), lambda i: (0, 0)),
                  pl.BlockSpec((TM, D), lambda i: (jnp.minimum(i, nres - 1), 0))],
        out_specs=[rt(D), pl.BlockSpec((1, 2, D), _typ(cfg)), pl.BlockSpec((1, D), lambda i: (0, 0))],
        out_shape=[S_((R, D), f32), S_((2, 2, D), f32), S_((1, D), f32)],
        compiler_params=_cp(VMEM_MID, ("arbitrary",)),
    )(dproj, w_in, xs, mods, gvec, dres)


def _na_block(cfg, b):
    return jnp.clip(NA_QR * b - NA_KH // 2, 0, cfg.rows - NA_WR)


def _na_load_bias(b, nb, b_hbm, b_v, sem):
    for typ, at in ((0, 0), (1, 1), (2, nb - 1)):
        @pl.when(b == at)
        def _(typ=typ):
            cp = pltpu.make_async_copy(b_hbm.at[typ], b_v, sem)
            cp.start()
            cp.wait()


def _na_probs(qh, klh, kch, bias):
    s_loc = _nt(qh, klh) + bias
    s_ctx = _nt(qh, kch)
    mx = jnp.maximum(jnp.max(s_loc, axis=-1, keepdims=True), jnp.max(s_ctx, axis=-1, keepdims=True))
    e_loc = jnp.exp(s_loc - mx)
    e_ctx = jnp.exp(s_ctx - mx)
    inv = 1.0 / (jnp.sum(e_loc, axis=-1, keepdims=True) + jnp.sum(e_ctx, axis=-1, keepdims=True))
    return e_loc * inv, e_ctx * inv


def _na_fwd(cfg, q, k, v, bexp, name, comm=None):
    S, L, T = cfg.S, cfg.L, cfg.T
    NQ, NW = NA_QR * GRID_W, NA_WR * GRID_W
    nb = cfg.rows // NA_QR

    def body(q_ref, k_hbm, v_hbm, b_hbm, o_ref, k_v, v_v, b_v, sem):
        b = pl.program_id(0)

        @pl.when(b == 0)
        def _():
            cs = [pltpu.make_async_copy(k_hbm, k_v, sem.at[0]), pltpu.make_async_copy(v_hbm, v_v, sem.at[1])]
            for c_ in cs:
                c_.start()
            for c_ in cs:
                c_.wait()

        _na_load_bias(b, nb, b_hbm, b_v, sem.at[2])
        st = pl.multiple_of(_na_block(cfg, b) * GRID_W, GRID_W)
        first = lax.broadcasted_iota(jnp.int32, (NQ, LANES), 1) < HEAD_DIM
        for hp in range(NA_HEADS // 2):
            ls = slice(hp * LANES, (hp + 1) * LANES)
            q2 = q_ref[:, ls]
            kl, vl = k_v[pl.ds(st, NW), ls], v_v[pl.ds(st, NW), ls]
            kc, vc = k_v[S:T, ls], v_v[S:T, ls]
            o2 = []
            for hh in range(2):
                qm = jnp.where(first if hh == 0 else ~first, q2, jnp.zeros_like(q2))
                p_loc, p_ctx = _na_probs(qm, kl, kc, b_v[2 * hp + hh])
                o2.append(_nn(p_loc.astype(bf16), vl) + _nn(p_ctx.astype(bf16), vc))
            o_ref[:, ls] = jnp.where(first, o2[0], o2[1]).astype(bf16)

    return _call(
        body, (q, k, v, bexp), comm, name=name, grid=(nb,),
        in_specs=[pl.BlockSpec((NQ, NA_WIDTH), lambda b: (b, 0)), ANY, ANY, ANY],
        out_specs=[pl.BlockSpec((NQ, NA_WIDTH), lambda b: (b, 0))],
        out_shape=[S_((S, NA_WIDTH), bf16)],
        scratch_shapes=[pltpu.VMEM((T, NA_WIDTH), bf16), pltpu.VMEM((T, NA_WIDTH), bf16),
                        pltpu.VMEM((NA_HEADS, NQ, NW), f32), pltpu.SemaphoreType.DMA((3,))],
        compiler_params=_cp(VMEM_MID, ("arbitrary",)),
    )


def _na_bwd(cfg, do, q, k, v, bexp, name, comm=None):
    S, L, T, rows = cfg.S, cfg.L, cfg.T, cfg.rows
    NQ, NW = NA_QR * GRID_W, NA_WR * GRID_W
    NSLOT = 2 * NA_KH
    nb = rows // NA_QR
    bmax = (rows - NA_WR) // NA_QR
    steps = 2 * nb - bmax
    W = NA_WIDTH
    assert nb >= 3 and bmax >= 1 and rows - NA_QR * bmax <= NSLOT

    def out_group(g):
        return jnp.where(g >= nb, g - nb + bmax, jnp.clip(g - 1, 0, bmax - 1))

    def body(do_ref, q_ref, k_hbm, v_hbm, b_hbm, dq_ref, dk_ref, dv_ref, dkc_ref, dvc_ref, db_hbm,
             k_v, v_v, b_v, db_v, ak, av, akc, avc, sem):
        g = pl.program_id(0)

        @pl.when(g == 0)
        def _():
            cs = [pltpu.make_async_copy(k_hbm, k_v, sem.at[0]), pltpu.make_async_copy(v_hbm, v_v, sem.at[1])]
            for c_ in cs:
                c_.start()
            db_v[...] = jnp.zeros_like(db_v)
            ak[...] = jnp.zeros_like(ak)
            av[...] = jnp.zeros_like(av)
            akc[...] = jnp.zeros_like(akc)
            avc[...] = jnp.zeros_like(avc)
            for c_ in cs:
                c_.wait()

        for typ, at in ((0, 1), (1, nb - 1)):
            @pl.when(g == at)
            def _(typ=typ):
                cp = pltpu.make_async_copy(db_v, db_hbm.at[typ], sem.at[2])
                cp.start()
                cp.wait()
                db_v[...] = jnp.zeros_like(db_v)

        @pl.when(g < nb)
        def _():
            _na_load_bias(g, nb, b_hbm, b_v, sem.at[2])
            ws = _na_block(cfg, g)
            st = pl.multiple_of(ws * GRID_W, GRID_W)
            first = lax.broadcasted_iota(jnp.int32, (NQ, LANES), 1) < HEAD_DIM
            for hp in range(NA_HEADS // 2):
                ls = slice(hp * LANES, (hp + 1) * LANES)
                q2, do2 = q_ref[:, ls], do_ref[:, ls]
                kl, vl = k_v[pl.ds(st, NW), ls], v_v[pl.ds(st, NW), ls]
                kc, vc = k_v[S:T, ls], v_v[S:T, ls]
                dq2 = []
                dk2 = jnp.zeros((NW, LANES), f32)
                dv2 = jnp.zeros((NW, LANES), f32)
                dkc2 = jnp.zeros((L, LANES), f32)
                dvc2 = jnp.zeros((L, LANES), f32)
                for hh in range(2):
                    keep = first if hh == 0 else ~first
                    qm = jnp.where(keep, q2, jnp.zeros_like(q2))
                    dom = jnp.where(keep, do2, jnp.zeros_like(do2))
                    p_loc, p_ctx = _na_probs(qm, kl, kc, b_v[2 * hp + hh])
                    dp_loc = _nt(dom, vl)
                    dp_ctx = _nt(dom, vc)
                    delta = jnp.sum(p_loc * dp_loc, axis=-1, keepdims=True) + jnp.sum(p_ctx * dp_ctx, axis=-1, keepdims=True)
                    ds_loc = p_loc * (dp_loc - delta)
                    ds_ctx = p_ctx * (dp_ctx - delta)
                    db_v[2 * hp + hh, :, 0:NW] += ds_loc
                    dsl, dsc = ds_loc.astype(bf16), ds_ctx.astype(bf16)
                    dq2.append(_nn(dsl, kl) + _nn(dsc, kc))
                    dk2 = dk2 + _tn(dsl, qm)
                    dv2 = dv2 + _tn(p_loc.astype(bf16), dom)
                    dkc2 = dkc2 + _tn(dsc, qm)
                    dvc2 = dvc2 + _tn(p_ctx.astype(bf16), dom)
                dq_ref[:, ls] = jnp.where(first, dq2[0], dq2[1])
                akc[:, ls] += dkc2
                avc[:, ls] += dvc2
                for kk in range(NA_WR):
                    slot = (ws + kk) % NSLOT
                    ak[slot, :, ls] += dk2[kk * GRID_W:(kk + 1) * GRID_W, :]
                    av[slot, :, ls] += dv2[kk * GRID_W:(kk + 1) * GRID_W, :]

        @pl.when(((g >= 1) & (g <= bmax)) | (g >= nb))
        def _():
            base = NA_QR * (out_group(g) % (NSLOT // NA_QR))
            for t in range(NA_QR):
                dk_ref[t * GRID_W:(t + 1) * GRID_W, :] = ak[base + t]
                dv_ref[t * GRID_W:(t + 1) * GRID_W, :] = av[base + t]
                ak[base + t] = jnp.zeros((GRID_W, W), f32)
                av[base + t] = jnp.zeros((GRID_W, W), f32)

        @pl.when(g == nb - 1)
        def _():
            cp = pltpu.make_async_copy(db_v, db_hbm.at[2], sem.at[2])
            cp.start()
            cp.wait()

        @pl.when(g == steps - 1)
        def _():
            dkc_ref[...] = akc[...]
            dvc_ref[...] = avc[...]

    qmap = lambda g: (jnp.minimum(g, nb - 1), 0)
    kmap = lambda g: (out_group(g), 0)
    full = lambda g: (0, 0)
    return _call(
        body, (do, q, k, v, bexp), comm, name=name, grid=(steps,),
        in_specs=[pl.BlockSpec((NQ, W), qmap), pl.BlockSpec((NQ, W), qmap), ANY, ANY, ANY],
        out_specs=[pl.BlockSpec((NQ, W), qmap), pl.BlockSpec((NQ, W), kmap), pl.BlockSpec((NQ, W), kmap),
                   pl.BlockSpec((L, W), full), pl.BlockSpec((L, W), full), ANY],
        out_shape=[S_((S, W), f32), S_((S, W), f32), S_((S, W), f32), S_((L, W), f32), S_((L, W), f32),
                   S_((NA_TYPES, NA_HEADS, NQ, NA_WPAD), f32)],
        scratch_shapes=[pltpu.VMEM((T, W), bf16), pltpu.VMEM((T, W), bf16),
                        pltpu.VMEM((NA_HEADS, NQ, NW), f32), pltpu.VMEM((NA_HEADS, NQ, NA_WPAD), f32),
                        pltpu.VMEM((NSLOT, GRID_W, W), f32), pltpu.VMEM((NSLOT, GRID_W, W), f32),
                        pltpu.VMEM((L, W), f32), pltpu.VMEM((L, W), f32), pltpu.SemaphoreType.DMA((3,))],
        compiler_params=_cp(VMEM_BIG, ("arbitrary",)),
    )


def _rpb_reduce(dbias, flip, sel, name):
    nq, w = NA_QR * GRID_W, GRID_W

    def diag_body(x_ref, j_ref, o_ref):
        rows = []
        for i in range(NA_QR):
            xr = jnp.dot(j_ref[...], x_ref[i * w:(i + 1) * w, :], preferred_element_type=f32, precision=lax.Precision.HIGHEST)
            rows.append(jnp.sum(pltpu.roll(xr, 0, 1, stride=1, stride_axis=0), axis=0, keepdims=True))
        o_ref[...] = jnp.concatenate(rows + [jnp.zeros((8 - NA_QR, NA_WPAD), f32)], axis=0)

    diag = pl.pallas_call(
        diag_body, name=name + "_diag", grid=(NA_TYPES, NA_HEADS),
        in_specs=[pl.BlockSpec((None, None, nq, NA_WPAD), lambda t, h: (t, h, 0, 0)), pl.BlockSpec((w, w), lambda t, h: (0, 0))],
        out_specs=pl.BlockSpec((None, None, 8, NA_WPAD), lambda t, h: (t, h, 0, 0)),
        out_shape=S_((NA_TYPES, NA_HEADS, 8, NA_WPAD), f32),
        compiler_params=_cp(VMEM_MID, ("arbitrary", "arbitrary")),
    )(dbias, flip)
    lo = w - NA_KW
    y = diag[:, :, :NA_QR, lo:lo + NA_WR * w].reshape(NA_TYPES, NA_HEADS, NA_QR, NA_WR, w)
    y = jnp.transpose(y, (1, 0, 2, 3, 4)).reshape(NA_HEADS, NA_TYPES * NA_QR * NA_WR, w)
    y = jnp.pad(y, ((0, 0), (0, NA_SEL_ROWS - y.shape[1]), (0, LANES - w)))

    def body(y_ref, sel_ref, o_ref):
        o_ref[...] = jnp.dot(sel_ref[...], y_ref[...], preferred_element_type=f32, precision=lax.Precision.HIGHEST)

    return pl.pallas_call(
        body, name=name, grid=(NA_HEADS,),
        in_specs=[pl.BlockSpec((None, NA_SEL_ROWS, LANES), lambda h: (h, 0, 0)), pl.BlockSpec((16, NA_SEL_ROWS), lambda h: (0, 0))],
        out_specs=pl.BlockSpec((None, 16, LANES), lambda h: (h, 0, 0)),
        out_shape=S_((NA_HEADS, 16, LANES), f32),
        compiler_params=_cp(VMEM_MID, ("arbitrary",)),
    )(y, sel)


def _ctx_attn_fwd(cfg, q, k, v, name):
    L = cfg.L
    blk = cfg.S // L

    def body(q_ref, k_ref, v_ref, o_ref):
        qv, kv, vv = q_ref[...], k_ref[...], v_ref[...]
        outs = []
        for h in range(NA_HEADS):
            hs = slice(h * HEAD_DIM, (h + 1) * HEAD_DIM)
            s = _nt(qv[:, hs], kv[:, hs])
            e = jnp.exp(s - jnp.max(s, axis=-1, keepdims=True))
            p = e * (1.0 / jnp.sum(e, axis=-1, keepdims=True))
            outs.append(_nn(p.astype(bf16), vv[:, hs]))
        o_ref[...] = jnp.concatenate(outs, axis=-1).astype(bf16)

    spec = pl.BlockSpec((L, NA_WIDTH), lambda i: (blk, 0))
    return pl.pallas_call(
        body, name=name, grid=(1,), in_specs=[spec, spec, spec],
        out_specs=pl.BlockSpec((L, NA_WIDTH), lambda i: (0, 0)), out_shape=S_((L, NA_WIDTH), bf16),
        compiler_params=_cp(VMEM_MID, ("arbitrary",)),
    )(q, k, v)


def _ctx_attn_bwd(cfg, do, q, k, v, name):
    L = cfg.L
    blk = cfg.S // L

    def body(do_ref, q_ref, k_ref, v_ref, dq_ref, dk_ref, dv_ref):
        dov, qv, kv, vv = do_ref[...], q_ref[...], k_ref[...], v_ref[...]
        dqs, dks, dvs = [], [], []
        for h in range(NA_HEADS):
            hs = slice(h * HEAD_DIM, (h + 1) * HEAD_DIM)
            qh, kh, doh = qv[:, hs], kv[:, hs], dov[:, hs]
            s = _nt(qh, kh)
            e = jnp.exp(s - jnp.max(s, axis=-1, keepdims=True))
            p = e * (1.0 / jnp.sum(e, axis=-1, keepdims=True))
            dp = _nt(doh, vv[:, hs])
            ds = (p * (dp - jnp.sum(p * dp, axis=-1, keepdims=True))).astype(bf16)
            dqs.append(_nn(ds, kh))
            dks.append(_tn(ds, qh))
            dvs.append(_tn(p.astype(bf16), doh))
        dq_ref[...] = jnp.concatenate(dqs, axis=-1)
        dk_ref[...] = jnp.concatenate(dks, axis=-1)
        dv_ref[...] = jnp.concatenate(dvs, axis=-1)

    spec = pl.BlockSpec((L, NA_WIDTH), lambda i: (blk, 0))
    ospec = pl.BlockSpec((L, NA_WIDTH), lambda i: (0, 0))
    return pl.pallas_call(
        body, name=name, grid=(1,), in_specs=[spec, spec, spec, spec],
        out_specs=[ospec, ospec, ospec], out_shape=[S_((L, NA_WIDTH), f32)] * 3,
        compiler_params=_cp(VMEM_MID, ("arbitrary",)),
    )(do, q, k, v)


def _pool_centered(u, band, inv):
    hi = u.astype(bf16)
    lo = (u - hi.astype(f32)).astype(bf16)
    return (_nn(band, hi) + _nn(band, lo)) * inv - u


def _pool_fwd(cfg, u, band, inv, w_pool, pool_scale, with_ctx, name):
    TM = cfg.TM
    nt = cfg.ntiles(with_ctx)
    C = POOL_CH

    def body(u_ref, band_ref, inv_ref, w_ref, ps_ref, o_ref):
        outs = []
        for g in range(POOL_GROUPS):
            d = _pool_centered(u_ref[:, g * C:(g + 1) * C], band_ref[0, g], inv_ref[0, g])
            outs.append(_nn(d.astype(bf16), w_ref[g].astype(bf16)) * ps_ref[:, g * C:(g + 1) * C])
        o_ref[...] = jnp.concatenate(outs, axis=-1).astype(bf16)

    typ4 = lambda i: (jnp.minimum(i // cfg.nxt, 1), 0, 0, 0)
    return pl.pallas_call(
        body, name=name, grid=(nt,),
        in_specs=[pl.BlockSpec((TM, POOL_WIDTH), lambda i: (i, 0)), pl.BlockSpec((1, POOL_GROUPS, TM, TM), typ4),
                  pl.BlockSpec((1, POOL_GROUPS, TM, 1), typ4), pl.BlockSpec((POOL_GROUPS, C, C), lambda i: (0, 0, 0)),
                  pl.BlockSpec((1, POOL_WIDTH), lambda i: (0, 0))],
        out_specs=pl.BlockSpec((TM, POOL_WIDTH), lambda i: (i, 0)),
        out_shape=S_((nt * TM, POOL_WIDTH), bf16),
        compiler_params=_cp(VMEM_MID, ("arbitrary",)),
    )(u, band, inv, w_pool, pool_scale)


def _pool_bwd(cfg, dmix, u, band, inv, w_pool, pool_scale, with_ctx, name):
    TM = cfg.TM
    nt = cfg.ntiles(with_ctx)
    C = POOL_CH

    def body(dy_ref, u_ref, band_ref, inv_ref, w_ref, ps_ref, du_ref, dw_ref, dps_ref):
        @pl.when(pl.program_id(0) == 0)
        def _():
            dw_ref[...] = jnp.zeros_like(dw_ref)
            dps_ref[...] = jnp.zeros_like(dps_ref)

        dus, dpss = [], []
        for g in range(POOL_GROUPS):
            gs = slice(g * C, (g + 1) * C)
            band_g, inv_g = band_ref[0, g], inv_ref[0, g]
            db = _pool_centered(u_ref[:, gs], band_g, inv_g).astype(bf16)
            wb = w_ref[g].astype(bf16)
            dy = dy_ref[:, gs].astype(f32)
            dpss.append(_rsum(dy * _nn(db, wb)))
            dys = (dy * ps_ref[:, gs]).astype(bf16)
            dw_ref[g] += _tn(db, dys)
            dd = _nt(dys, wb)
            t = dd * inv_g
            hi = t.astype(bf16)
            lo = (t - hi.astype(f32)).astype(bf16)
            dus.append(_tn(band_g, hi) + _tn(band_g, lo) - dd)
        du_ref[...] = jnp.concatenate(dus, axis=-1)
        dps_ref[...] += jnp.concatenate(dpss, axis=-1)

    typ4 = lambda i: (jnp.minimum(i // cfg.nxt, 1), 0, 0, 0)
    return pl.pallas_call(
        body, name=name, grid=(nt,),
        in_specs=[pl.BlockSpec((TM, POOL_WIDTH), lambda i: (i, 1)), pl.BlockSpec((TM, POOL_WIDTH), lambda i: (i, 0)),
                  pl.BlockSpec((1, POOL_GROUPS, TM, TM), typ4), pl.BlockSpec((1, POOL_GROUPS, TM, 1), typ4),
                  pl.BlockSpec((POOL_GROUPS, C, C), lambda i: (0, 0, 0)), pl.BlockSpec((1, POOL_WIDTH), lambda i: (0, 0))],
        out_specs=[pl.BlockSpec((TM, POOL_WIDTH), lambda i: (i, 0)), pl.BlockSpec((POOL_GROUPS, C, C), lambda i: (0, 0, 0)),
                   pl.BlockSpec((1, POOL_WIDTH), lambda i: (0, 0))],
        out_shape=[S_((nt * TM, POOL_WIDTH), f32), S_((POOL_GROUPS, C, C), f32), S_((1, POOL_WIDTH), f32)],
        compiler_params=_cp(VMEM_MID, ("arbitrary",)),
    )(dmix, u, band, inv, w_pool, pool_scale)


def _tmpost_fwd(cfg, na_x, na_c, pool, w_out, xs, mods, gvec, name):
    TM, D = cfg.TM, cfg.D
    with_ctx = na_c is not None
    nt = cfg.ntiles(with_ctx)
    R = nt * TM

    def body(*refs):
        if with_ctx:
            nax_ref, nac_ref, pool_ref, w_ref, xs_ref, mods_ref, g_ref, out_ref, opre_ref, mix_ref = refs
            na = jnp.where(pl.program_id(0) < cfg.nxt, nax_ref[...], nac_ref[...])
        else:
            nax_ref, pool_ref, w_ref, xs_ref, mods_ref, g_ref, out_ref, opre_ref, mix_ref = refs
            na = nax_ref[...]
        pool_v = pool_ref[...]
        mix_ref[:, 0:NA_WIDTH] = na
        mix_ref[:, NA_WIDTH:] = pool_v
        o = _nn(na, w_ref[0:NA_WIDTH, :]) + _nn(pool_v, w_ref[NA_WIDTH:, :])
        opre_ref[...] = o
        ohat, _ = _rms_hat(o)
        out_ref[...] = xs_ref[...] + mods_ref[0][5:6] * (ohat * g_ref[3:4])

    rt = lambda c: pl.BlockSpec((TM, c), lambda i: (i, 0))
    na_specs = [pl.BlockSpec((TM, NA_WIDTH), lambda i: (jnp.minimum(i, cfg.nxt - 1), 0))]
    na_args = [na_x]
    if with_ctx:
        na_specs.append(pl.BlockSpec((TM, NA_WIDTH), lambda i: (0, 0)))
        na_args.append(na_c)
    return pl.pallas_call(
        body, name=name, grid=(nt,),
        in_specs=na_specs + [rt(POOL_WIDTH), pl.BlockSpec((MIX_WIDTH, D), lambda i: (0, 0)), rt(D),
                             pl.BlockSpec((1, N_MOD, D), _typ(cfg)), pl.BlockSpec((6, D), lambda i: (0, 0))],
        out_specs=[rt(D), rt(D), rt(MIX_WIDTH)],
        out_shape=[S_((R, D), f32), S_((R, D), f32), S_((R, MIX_WIDTH), bf16)],
        compiler_params=_cp(VMEM_MID, ("arbitrary",)),
    )(*na_args, pool, w_out, xs, mods, gvec)


def _tmpost_bwd(cfg, dout, opre, w_out, mods, gvec, with_ctx, name):
    TM, D = cfg.TM, cfg.D
    nt = cfg.ntiles(with_ctx)
    R = nt * TM
    ntyp = 2 if with_ctx else 1

    def body(do_ref, opre_ref, w_ref, mods_ref, g_ref, dop_ref, dmix_ref, dm_ref, dg_ref):
        i = pl.program_id(0)

        @pl.when(i == 0)
        def _():
            dg_ref[...] = jnp.zeros_like(dg_ref)

        @pl.when((i == 0) | (i == cfg.nxt))
        def _():
            dm_ref[...] = jnp.zeros_like(dm_ref)

        do = do_ref[...]
        g3 = g_ref[3:4]
        ohat, rinv = _rms_hat(opre_ref[...])
        dm_ref[0] += _rsum(do * (ohat * g3))
        dr = mods_ref[0][5:6] * do
        dg_ref[...] += _rsum(dr * ohat)
        dob = _rms_bwd(dr * g3, ohat, rinv).astype(bf16)
        dop_ref[...] = dob
        dmix_ref[...] = _nt(dob, w_ref[...]).astype(bf16)

    rt = lambda c: pl.BlockSpec((TM, c), lambda i: (i, 0))
    return pl.pallas_call(
        body, name=name, grid=(nt,),
        in_specs=[rt(D), rt(D), pl.BlockSpec((MIX_WIDTH, D), lambda i: (0, 0)),
                  pl.BlockSpec((1, N_MOD, D), _typ(cfg)), pl.BlockSpec((6, D), lambda i: (0, 0))],
        out_specs=[rt(D), rt(MIX_WIDTH), pl.BlockSpec((1, 1, D), _typ(cfg)), pl.BlockSpec((1, D), lambda i: (0, 0))],
        out_shape=[S_((R, D), bf16), S_((R, MIX_WIDTH), bf16), S_((ntyp, 1, D), f32), S_((1, D), f32)],
        compiler_params=_cp(VMEM_MID, ("arbitrary",)),
    )(dout, opre, w_out, mods, gvec)


def _modvec_fwd(cvecs, w_mod, b_shard, name):
    nl, D, n = w_mod.shape
    tn = n // 3 if (n % 3 == 0 and (n // 3) % LANES == 0) else n

    def body(c_ref, w_ref, b_ref, o_ref, s_ref):
        cv = c_ref[...]
        sv = cv * _sigmoid(cv)
        s_ref[...] = sv
        o_ref[...] = _nn(sv.astype(bf16), w_ref[...].astype(bf16)) + b_ref[...]

    return pl.pallas_call(
        body, name=name, grid=(nl, n // tn),
        in_specs=[pl.BlockSpec((16, D), lambda l, j: (0, 0)), pl.BlockSpec((None, D, tn), lambda l, j: (l, 0, j)),
                  pl.BlockSpec((None, 1, tn), lambda l, j: (l, 0, j))],
        out_specs=[pl.BlockSpec((None, 16, tn), lambda l, j: (l, 0, j)), pl.BlockSpec((16, D), lambda l, j: (0, 0))],
        out_shape=[S_((nl, 16, n), f32), S_((16, D), f32)],
        compiler_params=_cp(VMEM_MID, ("arbitrary", "arbitrary")),
    )(cvecs, w_mod, b_shard)


def _modvec_bwd(s_t, dm, w_mod, name):
    nl, D, n = w_mod.shape
    tn = n // 3 if (n % 3 == 0 and (n // 3) % LANES == 0) else n

    def body(s_ref, dm_ref, w_ref, gw_ref, gc_ref):
        @pl.when(pl.program_id(1) == 0)
        def _():
            gc_ref[...] = jnp.zeros_like(gc_ref)
        dmv = dm_ref[...]
        gw_ref[...] = jnp.dot(s_ref[...], dmv, preferred_element_type=f32, precision=lax.Precision.HIGHEST)
        gc_ref[...] += _nt(dmv[8:16].astype(bf16), w_ref[...].astype(bf16))

    return pl.pallas_call(
        body, name=name, grid=(nl, n // tn),
        in_specs=[pl.BlockSpec((D, 16), lambda l, j: (0, 0)), pl.BlockSpec((None, 16, tn), lambda l, j: (l, 0, j)),
                  pl.BlockSpec((None, D, tn), lambda l, j: (l, 0, j))],
        out_specs=[pl.BlockSpec((None, D, tn), lambda l, j: (l, 0, j)), pl.BlockSpec((None, 8, D), lambda l, j: (l, 0, 0))],
        out_shape=[S_((nl, D, n), f32), S_((nl, 8, D), f32)],
        compiler_params=_cp(VMEM_MID, ("arbitrary", "arbitrary")),
    )(s_t, dm, w_mod)


def _as2d(a):
    n = a.size
    if a.ndim >= 2 and a.shape[-1] % LANES == 0:
        return a.reshape(-1, a.shape[-1])
    if n % LANES == 0:
        return a.reshape(-1, LANES)
    return a.reshape(-1, a.shape[-1]) if a.ndim >= 2 else a.reshape(1, n)


def _row_tile(r, c, budget_elems=512 * 1024):
    if r * c <= budget_elems or r % 8 != 0:
        return r
    t = r
    while t * c > budget_elems and t % 16 == 0:
        t //= 2
    return t


def _div_tile(r, c, budget_elems, mult=16):
    best = None
    for t in range(mult, r + 1, mult):
        if r % t == 0 and t * c <= budget_elems:
            best = t
    return best if best is not None else r


def _cast_into_place(shards, lead, axis, kidx, name):
    r, c = shards.shape[-2:]
    tr = _div_tile(r, c, 768 * 1024)
    nr = r // tr
    out_map = (lambda i, k: (i, k[0])) if axis == 1 else (lambda i, k: (k[0] * nr + i, 0))
    full2 = (r, c * N_CHIPS) if axis == 1 else (r * N_CHIPS, c)

    def body(k_ref, a_ref, o_ref):
        o_ref[...] = a_ref[...].astype(bf16)

    return pl.pallas_call(
        body, name=name,
        grid_spec=pltpu.PrefetchScalarGridSpec(
            num_scalar_prefetch=1, grid=(nr,),
            in_specs=[pl.BlockSpec((None,) * len(lead) + (tr, c), lambda i, k: tuple(lead) + (i, 0))],
            out_specs=pl.BlockSpec((tr, c), out_map)),
        out_shape=S_(full2, bf16), compiler_params=_cp(VMEM_MID, ("arbitrary",)),
    )(kidx, shards)


def _sum_devices8(own, land, axis, into, lead, dck, name):
    _, rh, cs = land.shape
    tr = _div_tile(rh, cs, 400 * 1024)
    nr = rh // tr
    if axis == 1:
        own_map = lambda i, s: (s[1] * nr + i, s[2])
    else:
        own_map = lambda i, s: (s[2] * 2 * nr + s[1] * nr + i, 0)
    nl = len(lead)

    def land_spec(j):
        return pl.BlockSpec((None, tr, cs), lambda i, s: ((s[0] + j) % N_DEV, i, 0))

    def body(s_ref, own_ref, *rest):
        acc = own_ref[...]
        for p_ref in rest[:N_DEV - 1]:
            acc = acc + p_ref[...].astype(f32)
        rest[-1][...] = acc

    return pl.pallas_call(
        body, name=name,
        grid_spec=pltpu.PrefetchScalarGridSpec(
            num_scalar_prefetch=1, grid=(nr,),
            in_specs=[pl.BlockSpec((tr, cs), own_map)] + [land_spec(j) for j in range(1, N_DEV)] + [ANY],
            out_specs=pl.BlockSpec((None,) * nl + (tr, cs), lambda i, s: tuple(lead) + (s[1] * nr + i, 0))),
        out_shape=S_(into.shape, f32), input_output_aliases={N_DEV + 1: 0},
        compiler_params=_cp(VMEM_MID, ("arbitrary",)),
    )(dck, own, *([land] * (N_DEV - 1)), into)


def _adamw(w, g, m, v, name):
    shape = w.shape
    w2, g2, m2, v2 = _as2d(w), _as2d(g), _as2d(m), _as2d(v)
    r, c = w2.shape
    tr = _row_tile(r, c, 256 * 1024)
    c1 = 1.0 - ADAM_B1 ** ADAM_STEP
    c2 = 1.0 - ADAM_B2 ** ADAM_STEP

    def body(w_ref, g_ref, m_ref, v_ref, d_ref, mo_ref, vo_ref):
        gv = g_ref[...]
        mn = ADAM_B1 * m_ref[...] + (1.0 - ADAM_B1) * gv
        vn = ADAM_B2 * v_ref[...] + (1.0 - ADAM_B2) * (gv * gv)
        mo_ref[...] = mn
        vo_ref[...] = vn
        d_ref[...] = -ADAM_LR * ((mn / c1) / (jnp.sqrt(vn / c2) + ADAM_EPS) + ADAM_WD * w_ref[...])

    spec = pl.BlockSpec((tr, c), lambda i: (i, 0))
    outs = pl.pallas_call(body, name=name, grid=(r // tr,), in_specs=[spec] * 4, out_specs=[spec] * 3,
                          out_shape=[S_((r, c), f32)] * 3, compiler_params=_cp(VMEM_MID, ("arbitrary",)))(w2, g2, m2, v2)
    return tuple(o.reshape(shape) for o in outs)


def _sum_devices(gathered, name):
    _, r, c = gathered.shape

    def body(a_ref, o_ref):
        acc = a_ref[0]
        for j in range(1, N_DEV):
            acc = acc + a_ref[j]
        o_ref[...] = acc

    tr = _row_tile(r, c, 64 * 1024)
    return pl.pallas_call(
        body, name=name, grid=(r // tr,),
        in_specs=[pl.BlockSpec((N_DEV, tr, c), lambda i: (0, i, 0))], out_specs=pl.BlockSpec((tr, c), lambda i: (i, 0)),
        out_shape=S_((r, c), f32), compiler_params=_cp(VMEM_MID, ("arbitrary",)))(gathered)


def _all_gather_small(block, name):
    m_per, n = block.shape

    def body(x_ref, out_ref, send_sems, recv_sems, local_sem):
        x, y, c = _mesh_pos()
        me, sibling = (x, y, c), (x, y, 1 - c)
        chips = [(1 - x, y), (x, 1 - y), (1 - x, 1 - y)]

        def rows(px, py, pc):
            return out_ref.at[pl.ds((4 * px + 2 * py + pc) * m_per, m_per), :]

        def copy(k, blk, to, src=None):
            return pltpu.make_async_remote_copy(
                src_ref=rows(*blk) if src is None else src, dst_ref=rows(*blk),
                send_sem=send_sems.at[k], recv_sem=recv_sems.at[k], device_id=to, device_id_type=MESH)

        mine = pltpu.make_async_copy(x_ref, rows(*me), local_sem)
        mine.start()
        first = [copy(0, me, sibling, src=x_ref)]
        first += [copy(1 + j, me, (*chip, c), src=x_ref) for j, chip in enumerate(chips)]
        for cp in first:
            cp.start()
        passed = [copy(4 + j, (*chip, c), sibling) for j, chip in enumerate(chips)]
        for j, chip in enumerate(chips):
            copy(1 + j, (*chip, c), me).wait_recv()
            passed[j].start()
        copy(0, sibling, me).wait_recv()
        for j, chip in enumerate(chips):
            copy(4 + j, (*chip, 1 - c), me).wait_recv()
        for cp in first + passed:
            cp.wait_send()
        mine.wait()

    return pl.pallas_call(
        body, name=name, out_shape=S_((N_DEV * m_per, n), block.dtype),
        in_specs=[pl.BlockSpec(memory_space=pltpu.VMEM)], out_specs=pl.BlockSpec(memory_space=pltpu.VMEM),
        scratch_shapes=[pltpu.SemaphoreType.DMA((7,)), pltpu.SemaphoreType.DMA((7,)), pltpu.SemaphoreType.DMA],
        compiler_params=_cp(VMEM_MID),
    )(block)


def _pack_rows(arrays):
    flat = jnp.concatenate([a.reshape(-1) for a in arrays])
    pad = (-flat.size) % (8 * LANES)
    return jnp.pad(flat, (0, pad)).reshape(-1, LANES)


def _unpack_rows(packed, shapes):
    flat = packed.reshape(-1)
    out, off = [], 0
    for s in shapes:
        n = int(np.prod(s))
        out.append(flat[off:off + n].reshape(s))
        off += n
    return out


W_AXIS = {"gu": 1, "dn": 0, "wi": 1, "wo": 0}


def _half_merge(bufs, name):
    nt = len(bufs)

    def body(*refs):
        outs = refs[nt:2 * nt]
        send_sems, recv_sems = refs[2 * nt:]
        x, y, c = _mesh_pos()

        def half(ref, h):
            rh = ref.shape[-2] // 2
            return ref.at[(slice(None),) * (len(ref.shape) - 2) + (pl.ds(h * rh, rh), slice(None))]

        cps = []
        for t in range(nt):
            cp = pltpu.make_async_remote_copy(
                src_ref=half(outs[t], c), dst_ref=half(outs[t], c), send_sem=send_sems.at[t], recv_sem=recv_sems.at[t],
                device_id=(x, y, 1 - c), device_id_type=MESH)
            cp.start()
            cps.append(cp)
        for t in range(nt):
            pltpu.make_async_remote_copy(
                src_ref=half(outs[t], 1 - c), dst_ref=half(outs[t], 1 - c), send_sem=send_sems.at[t], recv_sem=recv_sems.at[t],
                device_id=(x, y, 1 - c), device_id_type=MESH).wait_recv()
        for cp in cps:
            cp.wait_send()

    return pl.pallas_call(
        body, name=name, in_specs=[ANY] * nt, out_specs=[ANY] * nt, out_shape=[S_(b.shape, f32) for b in bufs],
        input_output_aliases={t: t for t in range(nt)},
        scratch_shapes=[pltpu.SemaphoreType.DMA((nt,)), pltpu.SemaphoreType.DMA((nt,))],
        compiler_params=_cp(VMEM_MID),
    )(*bufs)


def _local_step(cfg, x_lat, x_ctx, target, mods, norm_g, W, G, dck, na_rpb, w_pool, pool_scale):
    S, L, T, D, F = cfg.S, cfg.L, cfg.T, cfg.D, cfg.F
    depth = norm_g.shape[0]
    cos, sin = _rope_tables(S, L)
    band, inv = _pool_tables(cfg.TM, L)
    flip, sel = _rpb_reduce_tables()

    assert depth == 2, "the carrier schedules below are written for two layers"
    fwd_carry = {"ffn_fwd_0_0": [("wi", 0), ("wo", 0), ("gu", 0, 1), ("dn", 0, 1)],
                 "na_fwd_0": [("gu", 1, 0), ("dn", 1, 0)],
                 "ffn_fwd_0_1": [("wi", 1), ("wo", 1), ("gu", 1, 1), ("dn", 1, 1)]}
    bwd_carry = {"na_bwd_1": [("gu", 1, 1), ("dn", 1, 1)], "ffn_bwd_1_0": [("wi", 1), ("wo", 1)],
                 "ffn_bwd_0_1": [("gu", 1, 0), ("dn", 1, 0)], "na_bwd_0": [("gu", 0, 1), ("dn", 0, 1)],
                 "ffn_bwd_0_0": [("wi", 0), ("wo", 0)], "wgrad_dn_0_0": [("gu", 0, 0)]}
    last_scatter = [("dn", 0, 0)]
    tag = lambda key: "_".join(str(p) for p in key)
    g_f32, g_b16 = {}, {}

    def gather_on(name):
        keys = fwd_carry.get(name)
        return None if keys is None else _gather_comm([W[k_] for k_ in keys], [W_AXIS[k_[0]] for k_ in keys])

    def gathered(name, res):
        if name in fwd_carry:
            W.update(zip(fwd_carry[name], res))

    def scatter_on(name):
        keys = bwd_carry.get(name)
        return None if keys is None else _scatter_comm([g_b16[k_] for k_ in keys], [W_AXIS[k_[0]] for k_ in keys])

    def scattered(keys, lands):
        for key, land in zip(keys, lands):
            G[key[0]] = _sum_devices8(g_f32[key], land, W_AXIS[key[0]], G[key[0]], key[1:], dck, f"sum8_{tag(key)}")

    def wgrad(key, a, b, rows):
        name = f"wgrad_{tag(key)}"
        (g_f32[key], g_b16[key]), lands = _wgrad(a, b, rows, name, scatter_on(name))
        scattered(bwd_carry.get(name, ()), lands)

    saved = []
    xs, xs_ctx = x_lat, x_ctx
    for l in range(depth):
        last = l == depth - 1
        wc = not last
        gvec = norm_g[l]
        ps = pool_scale[l].reshape(1, POOL_WIDTH)
        bexp = _expand_rpb(na_rpb[l], f"bias_expand_{l}")
        name = f"ffn_fwd_{l}_0"
        (xs1, hb1, z1, y1), res = _ffn_fwd(cfg, xs, mods[l], gvec, W["gu", l, 0], W["dn", l, 0], 0, 0, True, name,
                                           gather_on(name), xs_ctx=xs_ctx)
        gathered(name, res)
        hb2, q, k, v, u = _tmpre_fwd(cfg, xs1, mods[l], gvec, W["wi", l], cos, sin, f"tmpre_fwd_{l}")
        name = f"na_fwd_{l}"
        (na_x,), res = _na_fwd(cfg, q, k, v, bexp, name, gather_on(name))
        gathered(name, res)
        na_c = _ctx_attn_fwd(cfg, q, k, v, f"ctx_attn_fwd_{l}") if wc else None
        pool = _pool_fwd(cfg, u, band, inv, w_pool[l], ps, wc, f"pool_fwd_{l}")
        xs2, opre, mix = _tmpost_fwd(cfg, na_x, na_c, pool, W["wo", l], xs1, mods[l], gvec, f"tmpost_fwd_{l}")
        name = f"ffn_fwd_{l}_1"
        outs, res = _ffn_fwd(cfg, xs2, mods[l], gvec, W["gu", l, 1], W["dn", l, 1], 6, 4, wc, name, gather_on(name),
                             loss_target=target if last else None)
        xs3, hb3, z3, y3 = outs[:4]
        gathered(name, res)
        saved.append(dict(xs=xs, xs_ctx=xs_ctx, xs1=xs1, xs2=xs2, hb1=hb1, z1=z1, y1=y1, hb2=hb2, q=q, k=k, v=v, u=u, mix=mix,
                          opre=opre, hb3=hb3, z3=z3, y3=y3, bexp=bexp, ps=ps, gvec=gvec))
        xs, xs_ctx = xs3, None

    dxs, loss_blk = xs, outs[4]

    small = [None] * depth
    for l in reversed(range(depth)):
        last = l == depth - 1
        wc = not last
        sv = saved[l]
        gvec = sv["gvec"]
        rows_b = cfg.T if wc else cfg.S
        name = f"ffn_bwd_{l}_1"
        (dxs2, dz, dyb, ab, dm678, dg45), lands = _ffn_bwd(cfg, dxs, sv["xs2"], sv["z3"], sv["y3"], mods[l], gvec,
                                                           W["gu", l, 1], W["dn", l, 1], 6, 4, wc, name, scatter_on(name))
        scattered(bwd_carry.get(name, ()), lands)
        wgrad(("gu", l, 1), sv["hb3"], dz, rows_b)
        wgrad(("dn", l, 1), ab, dyb, rows_b)
        dop, dmix, dm5, dg3 = _tmpost_bwd(cfg, dxs2, sv["opre"], W["wo", l], mods[l], gvec, wc, f"tmpost_bwd_{l}")
        wgrad(("wo", l), sv["mix"], dop, rows_b)
        du, dwp, dps = _pool_bwd(cfg, dmix, sv["u"], band, inv, w_pool[l], sv["ps"], wc, f"pool_bwd_{l}")
        name = f"na_bwd_{l}"
        (dq, dk, dv, dkc, dvc, dbexp), lands = _na_bwd(cfg, dmix, sv["q"], sv["k"], sv["v"], sv["bexp"], name, scatter_on(name))
        scattered(bwd_carry.get(name, ()), lands)
        drpb = _rpb_reduce(dbexp, flip, sel, f"rpb_reduce_{l}")
        if wc:
            dqc, dkc2, dvc2 = _ctx_attn_bwd(cfg, dmix, sv["q"], sv["k"], sv["v"], f"ctx_attn_bwd_{l}")
            ctx_terms = ([dqc], [dkc, dkc2], [dvc, dvc2])
        else:
            ctx_terms = ([], [dkc], [dvc])
        dproj = _rope_bwd_assemble(cfg, (dq, dk, dv, du), ctx_terms, wc, cos, sin, f"rope_bwd_{l}")
        wgrad(("wi", l), sv["hb2"], dproj, cfg.T)
        dxs1, dm34, dg2 = _tmpre_bwd(cfg, dproj, W["wi", l], sv["xs1"], mods[l], gvec, dxs2, wc, f"tmpre_bwd_{l}")
        name = f"ffn_bwd_{l}_0"
        (dxs, dz, dyb, ab, dm012, dg01), lands = _ffn_bwd(cfg, dxs1, sv["xs"], sv["z1"], sv["y1"], mods[l], gvec,
                                                          W["gu", l, 0], W["dn", l, 0], 0, 0, True, name, scatter_on(name),
                                                          xs_ctx=sv["xs_ctx"])
        scattered(bwd_carry.get(name, ()), lands)
        wgrad(("gu", l, 0), sv["hb1"], dz, cfg.T)
        wgrad(("dn", l, 0), ab, dyb, cfg.T)
        if not wc:
            zero = lambda a: jnp.concatenate([a, jnp.zeros_like(a)], axis=0)
            dm5, dm678 = zero(dm5), zero(dm678)
        dmods = jnp.concatenate([dm012, dm34, dm5, dm678], axis=1)
        dgs = jnp.concatenate([dg01, dg2, dg3, dg45], axis=0)
        small[l] = dict(dmods=dmods, dg=dgs, drpb=drpb, dwp=dwp, dps=dps)
    lands = _comm_only(_scatter_comm([g_b16[k_] for k_ in last_scatter], [W_AXIS[k_[0]] for k_ in last_scatter]), "scatter_last")
    scattered(last_scatter, lands)
    kinds = ("gu", "dn", "wi", "wo")
    merged = _half_merge([G[k_] for k_ in kinds], "merge_halves")
    return loss_blk, dxs, dict(zip(kinds, merged)), small


def kernel(x, c, ctx, c_ctx, w_mod, b_mod, norm_g, w_ffn_gate_up, w_ffn_down, w_in, w_out, na_rpb, w_pool, pool_scale, loss_target, m_c_ctx, m_w_mod, m_b_mod, m_norm_g, m_w_ffn_gate_up, m_w_ffn_down, m_w_in, m_w_out, m_na_rpb, m_w_pool, m_pool_scale, v_c_ctx, v_w_mod, v_b_mod, v_norm_g, v_w_ffn_gate_up, v_w_ffn_down, v_w_in, v_w_out, v_na_rpb, v_w_pool, v_pool_scale):
    S, D = x.shape[1], x.shape[2]
    L = ctx.shape[1]
    depth = w_mod.shape[0]
    F = w_ffn_down.shape[2] * N_CHIPS
    nmod = w_mod.shape[2]
    gsh = norm_g.shape[2]
    cfg = _Cfg(S, L, D, F)
    mx, my, mc = _mesh_pos()
    chip = 2 * mx + my
    dev = 4 * mx + 2 * my + mc

    kidx = chip.astype(jnp.int32).reshape(1)
    dck = jnp.stack([dev, mc, chip]).astype(jnp.int32)
    W = {}
    for l in range(depth):
        for i in range(2):
            W["gu", l, i] = _cast_into_place(w_ffn_gate_up, (l, i), W_AXIS["gu"], kidx, f"cast_gu_{l}_{i}")
            W["dn", l, i] = _cast_into_place(w_ffn_down, (l, i), W_AXIS["dn"], kidx, f"cast_dn_{l}_{i}")
        W["wi", l] = _cast_into_place(w_in, (l,), W_AXIS["wi"], kidx, f"cast_wi_{l}")
        W["wo", l] = _cast_into_place(w_out, (l,), W_AXIS["wo"], kidx, f"cast_wo_{l}")
    first = [("gu", 0, 0), ("dn", 0, 0)]
    W.update(zip(first, _comm_only(_gather_comm([W[k_] for k_ in first], [W_AXIS[k_[0]] for k_ in first]), "gather_first")))
    G = {"gu": lax.empty(w_ffn_gate_up.shape, f32), "dn": lax.empty(w_ffn_down.shape, f32),
         "wi": lax.empty(w_in.shape, f32), "wo": lax.empty(w_out.shape, f32)}

    c_all = _all_gather_small(jnp.pad(c, ((0, 7), (0, 0))), "gather_c").reshape(N_DEV, 8, D)[:, 0]
    cvecs = jnp.concatenate([c_all, c_ctx[None], jnp.zeros((7, D), f32)], axis=0)
    b_shard = lax.dynamic_slice_in_dim(b_mod, chip * nmod, nmod, axis=1).reshape(depth, 1, nmod)
    m_part, silu_c = _modvec_fwd(cvecs, w_mod, b_shard, "modvec_fwd")
    m_all = _all_gather_small(m_part.reshape(depth * 16, nmod), "gather_mod").reshape(N_DEV, depth, 16, nmod)
    m_full = jnp.concatenate([m_all[2 * j] for j in range(N_CHIPS)], axis=-1)
    m_mine = lax.dynamic_index_in_dim(m_full, dev, axis=1, keepdims=False)
    mods = jnp.stack([m_mine, m_full[:, 8]], axis=1).reshape(depth, 2, N_MOD, D)

    norm_g_full = _all_gather_small(_pack_rows([norm_g]), "gather_norm_g")
    rows_g = norm_g_full.shape[0] // N_DEV
    ng = norm_g_full.reshape(N_DEV, rows_g * LANES)[:, :norm_g.size].reshape(N_DEV, depth, 6, gsh)
    norm_g_all = jnp.concatenate([ng[2 * j] for j in range(N_CHIPS)], axis=-1)
    loss_blk, dx_lat, wgrads, small = _local_step(cfg, x[0], ctx[0], loss_target[0], mods, norm_g_all, W, G, dck,
                                                  na_rpb, w_pool, pool_scale)
    loss = lax.psum(loss_blk[0, 0], ("x", "y", "c"))
    grad_x = dx_lat[None]

    g_gu, g_dn, g_wi, g_wo = wgrads["gu"], wgrads["dn"], wgrads["wi"], wgrads["wo"]
    names = ("dmods", "dg", "drpb", "dwp", "dps")
    parts = [jnp.stack([small[l][n] for l in range(depth)]) for n in names]
    shapes = [p.shape for p in parts]
    packed = _pack_rows(parts)
    gathered = _all_gather_small(packed, "gather_small").reshape(N_DEV, packed.shape[0], LANES)
    total = _unpack_rows(_sum_devices(gathered, "sum_small"), shapes)
    dmods_sum, dg_sum, drpb_sum, dwp_sum, dps_sum = total
    dmods_each = jnp.stack([_unpack_rows(gathered[j], shapes[:1])[0] for j in range(N_DEV)])
    dm_rows = jnp.concatenate([jnp.transpose(dmods_each[:, :, 0], (1, 0, 2, 3)).reshape(depth, N_DEV, N_MOD * D),
                               dmods_sum[:, 1].reshape(depth, 1, N_MOD * D),
                               jnp.zeros((depth, 7, N_MOD * D), f32)], axis=1)
    dm_shard = lax.dynamic_slice_in_dim(dm_rows, chip * nmod, nmod, axis=2)
    grad_w_mod, gc_part = _modvec_bwd(silu_c.T, dm_shard, w_mod, "modvec_bwd")
    gc_all = _all_gather_small(gc_part.reshape(depth * 8, D), "gather_gc").reshape(N_DEV, depth, 8, D)
    grad_b_mod, grad_c_ctx = _small_finish(dm_rows, gc_all, c_ctx)
    grad_norm_g = lax.dynamic_slice_in_dim(dg_sum, chip * gsh, gsh, axis=2)
    grad_na_rpb = drpb_sum[:, :, :2 * NA_KH - 1, :2 * NA_KW - 1]
    grad_w_pool = dwp_sum
    grad_pool_scale = dps_sum.reshape(depth, POOL_WIDTH)

    grads = [grad_c_ctx, grad_w_mod, grad_b_mod, grad_norm_g, g_gu, g_dn, g_wi, g_wo, grad_na_rpb, grad_w_pool, grad_pool_scale]
    ws = [c_ctx, w_mod, b_mod, norm_g, w_ffn_gate_up, w_ffn_down, w_in, w_out, na_rpb, w_pool, pool_scale]
    ms = [m_c_ctx, m_w_mod, m_b_mod, m_norm_g, m_w_ffn_gate_up, m_w_ffn_down, m_w_in, m_w_out, m_na_rpb, m_w_pool, m_pool_scale]
    vs = [v_c_ctx, v_w_mod, v_b_mod, v_norm_g, v_w_ffn_gate_up, v_w_ffn_down, v_w_in, v_w_out, v_na_rpb, v_w_pool, v_pool_scale]
    tags = ["c_ctx", "w_mod", "b_mod", "norm_g", "gate_up", "down", "w_in", "w_out", "na_rpb", "w_pool", "pool_scale"]
    upd = [_adamw(w_, g_, m_, v_, f"adamw_{t}") for w_, g_, m_, v_, t in zip(ws, grads, ms, vs, tags)]
    return (loss, grad_x, *grads, *[u_[0] for u_ in upd], *[u_[1] for u_ in upd], *[u_[2] for u_ in upd])


def _small_finish(dm_rows, gc_all, c_ctx):
    depth, _, n = dm_rows.shape
    D = c_ctx.shape[0]

    def body(dm_ref, gc_ref, c_ref, gb_ref, gcx_ref):
        acc = dm_ref[:, 0]
        for j in range(1, N_DEV + 1):
            acc = acc + dm_ref[:, j]
        gb_ref[...] = acc
        t = jnp.zeros((1, D), f32)
        for l in range(depth):
            for j in range(N_CHIPS):
                t = t + gc_ref[2 * j, l, 0:1, :]
        cv = c_ref[...]
        sg = _sigmoid(cv)
        gcx_ref[...] = t * (sg * (1.0 + cv * (1.0 - sg)))

    gb, gcx = pl.pallas_call(
        body, name="small_finish",
        out_shape=[S_((depth, n), f32), S_((1, D), f32)],
        compiler_params=_cp(VMEM_MID),
    )(dm_rows, gc_all, c_ctx.reshape(1, D))
    return gb, gcx.reshape(D)
```

```python
import functools

import numpy as np
import jax
import jax.numpy as jnp
from jax import lax
from jax.experimental import pallas as pl
from jax.experimental.pallas import tpu as pltpu

f32, bf16 = jnp.float32, jnp.bfloat16

GRID_W = 64
N_MOD = 9
NA_HEADS = 8
HEAD_DIM = 64
NA_WIDTH = NA_HEADS * HEAD_DIM
NA_KH = 8
NA_KW = 16
POOL_GROUPS = 4
POOL_CH = 128
POOL_WIDTH = POOL_GROUPS * POOL_CH
POOL_WINDOWS = (2, 4, 8, 16)
IN_WIDTH = 3 * NA_WIDTH + POOL_WIDTH
MIX_WIDTH = NA_WIDTH + POOL_WIDTH
ROPE_THETA = 10000.0
ROPE_PAIRS = HEAD_DIM // 4
RMS_EPS = 1e-6
NEG_INF = -1e30
ADAM_LR, ADAM_B1, ADAM_B2, ADAM_EPS, ADAM_WD, ADAM_STEP = 0.001, 0.9, 0.999, 1e-08, 0.01, 10

N_DEV = 8
N_CHIPS = 4
LANES = 128
MIB = 1024 * 1024
VMEM_BIG = 52 * MIB
VMEM_MID = 40 * MIB
MESH = pl.DeviceIdType.MESH
ANY = pl.BlockSpec(memory_space=pl.ANY)
S_ = jax.ShapeDtypeStruct


def _cp(vmem=VMEM_MID, sem=None):
    return pltpu.CompilerParams(vmem_limit_bytes=vmem, dimension_semantics=sem)


def _sigmoid(x):
    return 0.5 * jnp.tanh(0.5 * x) + 0.5


def _rms_hat(x):
    rinv = lax.rsqrt(jnp.mean(x * x, axis=-1, keepdims=True) + RMS_EPS)
    return x * rinv, rinv


def _rms_bwd(dxhat, xhat, rinv):
    return rinv * (dxhat - xhat * jnp.mean(dxhat * xhat, axis=-1, keepdims=True))


def _rsum(a):
    return jnp.sum(a, axis=0, keepdims=True)


def _nt(a, b):
    return lax.dot_general(a, b, (((1,), (1,)), ((), ())), preferred_element_type=f32)


def _tn(a, b):
    return lax.dot_general(a, b, (((0,), (0,)), ((), ())), preferred_element_type=f32)


def _nn(a, b):
    return jnp.dot(a, b, preferred_element_type=f32)


def _swap16(x):
    lane = lax.broadcasted_iota(jnp.int32, x.shape, 1)
    n = x.shape[1]
    return jnp.where((lane % 32) < 16, pltpu.roll(x, n - 16, 1), pltpu.roll(x, 16, 1))


def _rope_tables(s_len, l_len):
    t = np.arange(s_len)
    inv = ROPE_THETA ** (-np.arange(ROPE_PAIRS, dtype=np.float32) / ROPE_PAIRS)
    ang_r = (t // GRID_W).astype(np.float32)[:, None] * inv
    ang_c = (t % GRID_W).astype(np.float32)[:, None] * inv
    cos = np.concatenate([np.cos(ang_r), np.cos(ang_r), np.cos(ang_c), np.cos(ang_c)], axis=-1)
    sin = np.concatenate([-np.sin(ang_r), np.sin(ang_r), -np.sin(ang_c), np.sin(ang_c)], axis=-1)
    cos = np.concatenate([cos, np.ones((l_len, HEAD_DIM), np.float32)], axis=0)
    sin = np.concatenate([sin, np.zeros((l_len, HEAD_DIM), np.float32)], axis=0)
    return (jnp.asarray(np.tile(cos, (1, 2)), f32), jnp.asarray(np.tile(sin, (1, 2)), f32))


def _pool_tables(tm, l_len):
    band = np.zeros((2, POOL_GROUPS, tm, tm), np.float32)
    inv = np.zeros((2, POOL_GROUPS, tm, 1), np.float32)
    for typ, length in ((0, GRID_W), (1, l_len)):
        for g, w in enumerate(POOL_WINDOWS):
            for t in range(tm):
                base, p = (t // length) * length, t % length
                lo = min(max(p - w // 2, 0), length)
                hi = min(max(p - w // 2 + w, 0), length)
                band[typ, g, t, base + lo:base + hi] = 1.0
                inv[typ, g, t, 0] = 1.0 / (hi - lo)
    return jnp.asarray(band, bf16), jnp.asarray(inv, f32)


NA_QR = 4
NA_WR = NA_KH + NA_QR - 1
NA_TYPES = 3
NA_SEL_ROWS = 136
NA_WPAD = 768


def _rpb_index_tables():
    j = np.arange(GRID_W)
    col_start = np.clip(j - NA_KW // 2, 0, GRID_W - NA_KW)
    valid = (j[None, :] >= col_start[:, None]) & (j[None, :] < col_start[:, None] + NA_KW)
    dc = np.clip(j[None, :] - j[:, None] + NA_KW - 1, 0, 2 * NA_KW - 2)
    i = np.arange(NA_QR)[:, None]
    kk = np.arange(NA_WR)[None, :]
    off = np.stack([np.zeros_like(i), i, np.full_like(i, NA_QR - 1)])
    d = np.stack([kk - i + NA_KH - 1, kk - i + NA_KH - 1 - NA_QR, kk - i])
    row_ok = (kk[None] >= off) & (kk[None] < off + NA_KH)
    assert (d[row_ok] >= 0).all() and (d[row_ok] <= 2 * NA_KH - 2).all()
    return valid, dc, d, row_ok


def _expand_rpb(rpb, name):
    _, _, d, row_ok = _rpb_index_tables()
    heads, nd, ne = rpb.shape
    w = GRID_W
    v = jnp.pad(rpb, ((0, 0), (0, 0), (w - NA_KW, 2 * w - (w - NA_KW) - ne)))
    x = jnp.broadcast_to(v[:, :, None, :], (heads, nd, w, 2 * w)).reshape(heads, nd, 2 * w * w)
    t = x[:, :, :w * (2 * w - 1)].reshape(heads, nd, w, 2 * w - 1)[..., w - 1:]

    def body(t_ref, o_ref):
        q = lax.broadcasted_iota(jnp.int32, (w, w), 0)
        c = lax.broadcasted_iota(jnp.int32, (w, w), 1)
        c0 = jnp.clip(q - NA_KW // 2, 0, w - NA_KW)
        in_cols = (c >= c0) & (c < c0 + NA_KW)
        outside = jnp.full((w, w), NEG_INF, f32)
        blocks = [jnp.where(in_cols, t_ref[dd], NEG_INF) for dd in range(nd)]
        for typ in range(NA_TYPES):
            for i in range(NA_QR):
                row = [blocks[d[typ, i, kk]] if row_ok[typ, i, kk] else outside for kk in range(NA_WR)]
                o_ref[typ, i * w:(i + 1) * w, :] = jnp.concatenate(row, axis=1)

    return pl.pallas_call(
        body, name=name, grid=(heads,),
        in_specs=[pl.BlockSpec((None, nd, w, w), lambda h: (h, 0, 0, 0))],
        out_specs=pl.BlockSpec((NA_TYPES, None, NA_QR * w, NA_WR * w), lambda h: (0, h, 0, 0)),
        out_shape=S_((NA_TYPES, heads, NA_QR * w, NA_WR * w), f32),
        compiler_params=_cp(VMEM_MID, ("arbitrary",)),
    )(t)


def _rpb_reduce_tables():
    _, _, d, row_ok = _rpb_index_tables()
    flip = np.eye(GRID_W, dtype=np.float32)[::-1].copy()
    sel = np.zeros((16, NA_SEL_ROWS), np.float32)
    flat_d, flat_ok = d.reshape(-1), row_ok.reshape(-1)
    for n in range(flat_d.size):
        if flat_ok[n]:
            sel[flat_d[n], n] = 1.0
    return jnp.asarray(flip), jnp.asarray(sel)


class _Cfg:
    def __init__(self, s_len, l_len, d, f):
        self.S, self.L, self.D, self.F = s_len, l_len, d, f
        self.T = s_len + l_len
        self.TM = 256 if l_len % 256 == 0 else 128
        assert l_len == self.TM, "context length must equal the row tile"
        assert s_len % self.TM == 0 and s_len % GRID_W == 0
        self.nxt = s_len // self.TM
        self.ntt = self.T // self.TM
        self.rows = s_len // GRID_W
        assert self.rows >= 2 * NA_KH
        assert f % (2 * LANES) == 0
        self.FC = f

    def ntiles(self, with_ctx):
        return self.ntt if with_ctx else self.nxt


def _typ(cfg):
    return lambda i: (jnp.minimum(i // cfg.nxt, 1), 0, 0)


def _mesh_pos():
    return lax.axis_index("x"), lax.axis_index("y"), lax.axis_index("c")


class _Comm:
    def __init__(self, ins, outs, alias, nsem, start, finish):
        self.ins, self.outs, self.alias, self.nsem, self.start, self.finish = ins, outs, alias, nsem, start, finish


def _call(body, args, comm=None, *, grid, in_specs, out_specs, out_shape, scratch_shapes=(), **kw):
    if comm is None:
        return pl.pallas_call(body, grid=grid, in_specs=list(in_specs), out_specs=list(out_specs), out_shape=list(out_shape),
                              scratch_shapes=list(scratch_shapes), **kw)(*args), ()
    n_in, n_out, n_sc = len(in_specs), len(out_specs), len(scratch_shapes)
    ci, co = len(comm.ins), len(comm.outs)

    def carrier(*refs):
        bounds = np.cumsum([0, n_in, ci, n_out, co, n_sc])
        ins, cins, outs, couts, scr = (refs[a:b] for a, b in zip(bounds[:-1], bounds[1:]))
        send, recv = refs[bounds[-1]], refs[bounds[-1] + 1]
        first = functools.reduce(jnp.logical_and, [pl.program_id(a) == 0 for a in range(len(grid))])
        last = functools.reduce(jnp.logical_and, [pl.program_id(a) == g - 1 for a, g in enumerate(grid)])

        @pl.when(first)
        def _():
            comm.start(cins, couts, send, recv)

        body(*ins, *outs, *scr)

        @pl.when(last)
        def _():
            comm.finish(cins, couts, send, recv)

    res = pl.pallas_call(
        carrier, grid=grid, in_specs=list(in_specs) + [ANY] * ci, out_specs=list(out_specs) + [ANY] * co,
        out_shape=list(out_shape) + list(comm.outs),
        input_output_aliases={n_in + a: n_out + b for a, b in comm.alias.items()},
        scratch_shapes=list(scratch_shapes) + [pltpu.SemaphoreType.DMA((comm.nsem,)), pltpu.SemaphoreType.DMA((comm.nsem,))],
        **kw)(*args, *comm.ins)
    return res[:n_out], res[n_out:]


def _comm_only(comm, name):
    ci, co = len(comm.ins), len(comm.outs)

    def body(*refs):
        cins, couts = refs[:ci], refs[ci:ci + co]
        send, recv = refs[ci + co], refs[ci + co + 1]
        comm.start(cins, couts, send, recv)
        comm.finish(cins, couts, send, recv)

    return pl.pallas_call(
        body, name=name, in_specs=[ANY] * ci, out_specs=[ANY] * co, out_shape=list(comm.outs),
        input_output_aliases=dict(comm.alias),
        scratch_shapes=[pltpu.SemaphoreType.DMA((comm.nsem,)), pltpu.SemaphoreType.DMA((comm.nsem,))],
        compiler_params=_cp(VMEM_MID),
    )(*comm.ins)


def _half_view(ref, axis, kk, h):
    r, c = ref.shape
    if axis == 1:
        n = c // N_CHIPS
        return ref.at[pl.ds(h * (r // 2), r // 2), pl.ds(pl.multiple_of(kk * n, LANES), n)]
    n = r // N_CHIPS
    return ref.at[pl.ds(pl.multiple_of(kk * n + h * (n // 2), 8), n // 2), :]


def _other_chips(x, y):
    return [(1 - x, y), (x, 1 - y), (1 - x, 1 - y)]


def _gather_comm(arrs, axes):
    n = len(arrs)

    def copy(ref, view, sems, k, to):
        send, recv = sems
        return pltpu.make_async_remote_copy(src_ref=view, dst_ref=view, send_sem=send.at[k], recv_sem=recv.at[k],
                                            device_id=to, device_id_type=MESH)

    def start(cins, bufs, send, recv):
        x, y, c = _mesh_pos()
        for t in range(n):
            own = _half_view(bufs[t], axes[t], 2 * x + y, c)
            for j, chip in enumerate(_other_chips(x, y)):
                copy(bufs[t], own, (send, recv), 6 * t + j, (*chip, c)).start()

    def finish(cins, bufs, send, recv):
        x, y, c = _mesh_pos()
        sibling = (x, y, 1 - c)
        chips = _other_chips(x, y)
        for t in range(n):
            for j, chip in enumerate(chips):
                landed = _half_view(bufs[t], axes[t], 2 * chip[0] + chip[1], c)
                copy(bufs[t], landed, (send, recv), 6 * t + j, (*chip, c)).wait_recv()
                copy(bufs[t], landed, (send, recv), 6 * t + 3 + j, sibling).start()
        for t in range(n):
            own = _half_view(bufs[t], axes[t], 2 * x + y, c)
            for j, chip in enumerate(chips):
                kj = 2 * chip[0] + chip[1]
                copy(bufs[t], _half_view(bufs[t], axes[t], kj, 1 - c), (send, recv), 6 * t + 3 + j, sibling).wait_recv()
                copy(bufs[t], own, (send, recv), 6 * t + j, (*chip, c)).wait_send()
                copy(bufs[t], _half_view(bufs[t], axes[t], kj, c), (send, recv), 6 * t + 3 + j, sibling).wait_send()

    return _Comm(list(arrs), [S_(a.shape, a.dtype) for a in arrs], {t: t for t in range(n)}, 6 * n, start, finish)


def _scatter_comm(parts, axes):
    n = len(parts)
    peers = [(fx, fy, fc) for fx in (0, 1) for fy in (0, 1) for fc in (0, 1)][1:]

    def half_shape(a, axis):
        r, c = a.shape
        return (r // 2, c // N_CHIPS) if axis == 1 else (r // N_CHIPS // 2, c)

    def start(srcs, lands, send, recv):
        x, y, c = _mesh_pos()
        me = 4 * x + 2 * y + c
        for t in range(n):
            for r_, (fx, fy, fc) in enumerate(peers):
                dx, dy, dc = (1 - x if fx else x), (1 - y if fy else y), (1 - c if fc else c)
                pltpu.make_async_remote_copy(
                    src_ref=_half_view(srcs[t], axes[t], 2 * dx + dy, dc), dst_ref=lands[t].at[me],
                    send_sem=send.at[7 * t + r_], recv_sem=recv.at[7 * t + r_],
                    device_id=(dx, dy, dc), device_id_type=MESH).start()

    def finish(srcs, lands, send, recv):
        x, y, c = _mesh_pos()
        for t in range(n):
            mine = _half_view(srcs[t], axes[t], 2 * x + y, c)
            for r_, (fx, fy, fc) in enumerate(peers):
                sx, sy, sc = (1 - x if fx else x), (1 - y if fy else y), (1 - c if fc else c)
                cp = pltpu.make_async_remote_copy(
                    src_ref=mine, dst_ref=lands[t].at[4 * sx + 2 * sy + sc],
                    send_sem=send.at[7 * t + r_], recv_sem=recv.at[7 * t + r_],
                    device_id=(sx, sy, sc), device_id_type=MESH)
                cp.wait_recv()
                cp.wait_send()

    return _Comm(list(parts), [S_((N_DEV,) + half_shape(a, ax), a.dtype) for a, ax in zip(parts, axes)], {}, 7 * n, start, finish)


def _ffn_fwd(cfg, xs, mods, gvec, wgu, wd, mi, gi, with_ctx, name, comm=None, xs_ctx=None, loss_target=None):
    TM, D, F, FC = cfg.TM, cfg.D, cfg.F, cfg.FC
    nt = cfg.ntiles(with_ctx)
    R = nt * TM
    split, head = xs_ctx is not None, loss_target is not None

    def body(*refs):
        it = iter(refs)
        xs_ref = next(it)
        xc_ref = next(it) if split else None
        mods_ref, g_ref, wgu_hbm, wd_hbm = next(it), next(it), next(it), next(it)
        t_ref = next(it) if head else None
        out_ref, hb_ref, z_ref, y_ref = next(it), next(it), next(it), next(it)
        loss_ref = next(it) if head else None
        wgu_v, wd_v, sem = next(it), next(it), next(it)
        i = pl.program_id(0)

        @pl.when(i == 0)
        def _():
            c0 = pltpu.make_async_copy(wgu_hbm, wgu_v, sem.at[0])
            c1 = pltpu.make_async_copy(wd_hbm, wd_v, sem.at[1])
            c0.start(); c1.start(); c0.wait(); c1.wait()
            if head:
                loss_ref[...] = jnp.zeros_like(loss_ref)
        x = xs_ref[...]
        if split:
            x = jnp.where(i < cfg.nxt, x, xc_ref[...])
        m = mods_ref[0]
        sh, sc, gt = m[mi:mi + 1], m[mi + 1:mi + 2], m[mi + 2:mi + 3]
        xhat, _ = _rms_hat(x)
        h = (xhat * g_ref[gi:gi + 1]) * (1.0 + sc) + sh
        hb = h.astype(bf16)
        hb_ref[...] = hb
        y = jnp.zeros((TM, D), f32)
        for ch in range(F // FC):
            zg = _nn(hb, wgu_v[:, ch * FC:(ch + 1) * FC])
            zu = _nn(hb, wgu_v[:, F + ch * FC:F + (ch + 1) * FC])
            z_ref[:, ch * FC:(ch + 1) * FC] = zg.astype(bf16)
            z_ref[:, F + ch * FC:F + (ch + 1) * FC] = zu.astype(bf16)
            a = (zg * _sigmoid(zg)) * zu
            y = y + _nn(a.astype(bf16), wd_v[ch * FC:(ch + 1) * FC, :])
        y_ref[...] = y
        yhat, _ = _rms_hat(y)
        out = x + 0.5 * gt * (yhat * g_ref[gi + 1:gi + 2])
        if head:
            e = out - t_ref[...]
            out_ref[...] = e * (1.0 / D)
            loss_ref[...] += jnp.sum(jnp.mean(e * e, axis=-1, keepdims=True), axis=0, keepdims=True) * 0.5
        else:
            out_ref[...] = out

    rt = lambda c: pl.BlockSpec((TM, c), lambda i: (i, 0))
    lat = pl.BlockSpec((TM, D), lambda i: (jnp.minimum(i, cfg.nxt - 1), 0))
    x_specs, x_args = ([lat, pl.BlockSpec((TM, D), lambda i: (0, 0))], [xs, xs_ctx]) if split else ([rt(D)], [xs])
    t_specs, t_args = ([rt(D)], [loss_target]) if head else ([], [])
    l_specs, l_shape = ([pl.BlockSpec((8, LANES), lambda i: (0, 0))], [S_((8, LANES), f32)]) if head else ([], [])
    return _call(
        body, (*x_args, mods, gvec, wgu, wd, *t_args), comm, name=name, grid=(nt,),
        in_specs=x_specs + [pl.BlockSpec((1, N_MOD, D), _typ(cfg)), pl.BlockSpec((6, D), lambda i: (0, 0)), ANY, ANY] + t_specs,
        out_specs=[rt(D), rt(D), rt(2 * F), rt(D)] + l_specs,
        out_shape=[S_((R, D), f32), S_((R, D), bf16), S_((R, 2 * F), bf16), S_((R, D), f32)] + l_shape,
        scratch_shapes=[pltpu.VMEM((D, 2 * F), bf16), pltpu.VMEM((F, D), bf16), pltpu.SemaphoreType.DMA((2,))],
        compiler_params=_cp(VMEM_BIG, ("arbitrary",)),
    )


def _ffn_bwd(cfg, dout, xs, z, y, mods, gvec, wgu, wd, mi, gi, with_ctx, name, comm=None, xs_ctx=None):
    TM, D, F, FC = cfg.TM, cfg.D, cfg.F, cfg.FC
    nt = cfg.ntiles(with_ctx)
    R = nt * TM
    ntyp = 2 if with_ctx else 1
    split = xs_ctx is not None

    def body(*refs):
        it = iter(refs)
        do_ref, xs_ref = next(it), next(it)
        xc_ref = next(it) if split else None
        z_ref, y_ref, mods_ref, g_ref, wgu_hbm, wd_hbm = (next(it) for _ in range(6))
        dx_ref, dz_ref, dy_ref, a_ref, dm_ref, dg_ref, wgu_v, wd_v, sem = (next(it) for _ in range(9))
        i = pl.program_id(0)

        @pl.when(i == 0)
        def _():
            c0 = pltpu.make_async_copy(wgu_hbm, wgu_v, sem.at[0])
            c1 = pltpu.make_async_copy(wd_hbm, wd_v, sem.at[1])
            c0.start(); c1.start(); c0.wait(); c1.wait()
            dg_ref[...] = jnp.zeros_like(dg_ref)

        @pl.when((i == 0) | (i == cfg.nxt))
        def _():
            dm_ref[...] = jnp.zeros_like(dm_ref)

        do = do_ref[...]
        x = xs_ref[...]
        if split:
            x = jnp.where(i < cfg.nxt, x, xc_ref[...])
        m = mods_ref[0]
        sc, gt = m[mi + 1:mi + 2], m[mi + 2:mi + 3]
        g_pre, g_post = g_ref[gi:gi + 1], g_ref[gi + 1:gi + 2]
        xhat, rinv0 = _rms_hat(x)
        n0 = xhat * g_pre
        yhat, rinv1 = _rms_hat(y_ref[...])
        d_gt = _rsum(0.5 * do * (yhat * g_post))
        dr = (0.5 * gt) * do
        dg_post = _rsum(dr * yhat)
        dy = _rms_bwd(dr * g_post, yhat, rinv1)
        dyb = dy.astype(bf16)
        dy_ref[...] = dyb
        dh = jnp.zeros((TM, D), f32)
        for ch in range(F // FC):
            zg = z_ref[:, ch * FC:(ch + 1) * FC].astype(f32)
            zu = z_ref[:, F + ch * FC:F + (ch + 1) * FC].astype(f32)
            sg = _sigmoid(zg)
            silu = zg * sg
            a_ref[:, ch * FC:(ch + 1) * FC] = (silu * zu).astype(bf16)
            da = _nt(dyb, wd_v[ch * FC:(ch + 1) * FC, :])
            dzu = (da * silu).astype(bf16)
            dzg = (da * zu * (sg * (1.0 + zg * (1.0 - sg)))).astype(bf16)
            dz_ref[:, ch * FC:(ch + 1) * FC] = dzg
            dz_ref[:, F + ch * FC:F + (ch + 1) * FC] = dzu
            dh = dh + _nt(dzg, wgu_v[:, ch * FC:(ch + 1) * FC]) + _nt(dzu, wgu_v[:, F + ch * FC:F + (ch + 1) * FC])
        d_sh = _rsum(dh)
        d_sc = _rsum(dh * n0)
        dn = dh * (1.0 + sc)
        dg_pre = _rsum(dn * xhat)
        dx = do + _rms_bwd(dn * g_pre, xhat, rinv0)
        if split:
            @pl.when(i < cfg.nxt)
            def _():
                dx_ref[...] = dx
        else:
            dx_ref[...] = dx
        dm_ref[0] += jnp.concatenate([d_sh, d_sc, d_gt], axis=0)
        dg_ref[...] += jnp.concatenate([dg_pre, dg_post], axis=0)

    rt = lambda c: pl.BlockSpec((TM, c), lambda i: (i, 0))
    lat = pl.BlockSpec((TM, D), lambda i: (jnp.minimum(i, cfg.nxt - 1), 0))
    x_specs, x_args = ([lat, pl.BlockSpec((TM, D), lambda i: (0, 0))], [xs, xs_ctx]) if split else ([rt(D)], [xs])
    return _call(
        body, (dout, *x_args, z, y, mods, gvec, wgu, wd), comm, name=name, grid=(nt,),
        in_specs=[rt(D)] + x_specs + [rt(2 * F), rt(D), pl.BlockSpec((1, N_MOD, D), _typ(cfg)),
                                       pl.BlockSpec((6, D), lambda i: (0, 0)), ANY, ANY],
        out_specs=[lat if split else rt(D), rt(2 * F), rt(D), rt(F), pl.BlockSpec((1, 3, D), _typ(cfg)),
                   pl.BlockSpec((2, D), lambda i: (0, 0))],
        out_shape=[S_((cfg.S if split else R, D), f32), S_((R, 2 * F), bf16), S_((R, D), bf16), S_((R, F), bf16),
                   S_((ntyp, 3, D), f32), S_((2, D), f32)],
        scratch_shapes=[pltpu.VMEM((D, 2 * F), bf16), pltpu.VMEM((F, D), bf16), pltpu.SemaphoreType.DMA((2,))],
        compiler_params=_cp(VMEM_BIG, ("arbitrary",)),
    )


def _wgrad(a, b, k_rows, name, comm=None):
    M, N = a.shape[1], b.shape[1]
    tn = N
    for cand in (1408, 1024, 512):
        if N % cand == 0 and N > cand:
            tn = cand
            break
    room = VMEM_BIG - 6 * MIB - 2 * M * tn * 6
    tk = _div_tile(k_rows, 1, min(2816, room // (4 * (M + tn))), LANES)
    nk = k_rows // tk

    def body(a_ref, b_ref, o_ref, ob_ref):
        k = pl.program_id(1)

        @pl.when(k == 0)
        def _():
            o_ref[...] = jnp.zeros_like(o_ref)
        o_ref[...] += _tn(a_ref[...], b_ref[...])

        @pl.when(k == nk - 1)
        def _():
            ob_ref[...] = o_ref[...].astype(bf16)

    ospec = pl.BlockSpec((M, tn), lambda n, k: (0, n))
    return _call(
        body, (a, b), comm, name=name, grid=(N // tn, nk),
        in_specs=[pl.BlockSpec((tk, M), lambda n, k: (k, 0)), pl.BlockSpec((tk, tn), lambda n, k: (k, n))],
        out_specs=[ospec, ospec], out_shape=[S_((M, N), f32), S_((M, N), bf16)],
        compiler_params=_cp(VMEM_BIG, ("arbitrary", "arbitrary")),
    )


def _tmpre_fwd(cfg, xs, mods, gvec, w_in, cos, sin, name):
    TM, D = cfg.TM, cfg.D
    nt, R = cfg.ntt, cfg.T
    W = NA_WIDTH

    def body(xs_ref, mods_ref, g_ref, w_ref, cos_ref, sin_ref, hb_ref, q_ref, k_ref, v_ref, u_ref):
        x = xs_ref[...]
        m = mods_ref[0]
        xhat, _ = _rms_hat(x)
        hb = ((xhat * g_ref[2:3]) * (1.0 + m[4:5]) + m[3:4]).astype(bf16)
        hb_ref[...] = hb
        p = _nn(hb, w_ref[...])
        cs = jnp.tile(cos_ref[...], (1, W // LANES))
        sn = jnp.tile(sin_ref[...], (1, W // LANES))
        q = p[:, 0:W]
        k = p[:, W:2 * W]
        q_ref[...] = ((q * cs + _swap16(q) * sn) * (HEAD_DIM ** -0.5)).astype(bf16)
        k_ref[...] = (k * cs + _swap16(k) * sn).astype(bf16)
        v_ref[...] = p[:, 2 * W:3 * W].astype(bf16)
        u_ref[...] = p[:, 3 * W:]

    rt = lambda c: pl.BlockSpec((TM, c), lambda i: (i, 0))
    return pl.pallas_call(
        body, name=name, grid=(nt,),
        in_specs=[rt(D), pl.BlockSpec((1, N_MOD, D), _typ(cfg)), pl.BlockSpec((6, D), lambda i: (0, 0)),
                  pl.BlockSpec((D, IN_WIDTH), lambda i: (0, 0)), rt(LANES), rt(LANES)],
        out_specs=[rt(D), rt(W), rt(W), rt(W), rt(POOL_WIDTH)],
        out_shape=[S_((R, D), bf16), S_((R, W), bf16), S_((R, W), bf16), S_((R, W), bf16), S_((R, POOL_WIDTH), f32)],
        compiler_params=_cp(VMEM_MID, ("arbitrary",)),
    )(xs, mods, gvec, w_in, cos, sin)


def _rope_bwd_assemble(cfg, lat, ctx_terms, du_has_ctx, cos, sin, name):
    TM = cfg.TM
    W = NA_WIDTH
    n_ctx = [len(t) for t in ctx_terms]
    flat_ctx = [a for t in ctx_terms for a in t]

    def body(*refs):
        dq_ref, dk_ref, dv_ref, du_ref = refs[:4]
        ctx_refs = refs[4:4 + len(flat_ctx)]
        cos_ref, sin_ref, o_ref = refs[4 + len(flat_ctx):]
        is_ctx = pl.program_id(0) >= cfg.nxt
        vals, off = [], 0
        for lat_ref, n in zip((dq_ref, dk_ref, dv_ref), n_ctx):
            cv = jnp.zeros((TM, W), f32)
            for r_ in ctx_refs[off:off + n]:
                cv = cv + r_[...]
            off += n
            vals.append(jnp.where(is_ctx, cv, lat_ref[...]))
        du_ = du_ref[...] if du_has_ctx else jnp.where(is_ctx, 0.0, du_ref[...])
        cs = jnp.tile(cos_ref[...], (1, W // LANES))
        sn = jnp.tile(sin_ref[...], (1, W // LANES))
        dq_ = vals[0] * (HEAD_DIM ** -0.5)
        dk_ = vals[1]
        o_ref[:, 0:W] = (dq_ * cs + _swap16(dq_ * sn)).astype(bf16)
        o_ref[:, W:2 * W] = (dk_ * cs + _swap16(dk_ * sn)).astype(bf16)
        o_ref[:, 2 * W:3 * W] = vals[2].astype(bf16)
        o_ref[:, 3 * W:] = du_.astype(bf16)

    rt = lambda c: pl.BlockSpec((TM, c), lambda i: (i, 0))
    lat_spec = pl.BlockSpec((TM, W), lambda i: (jnp.minimum(i, cfg.nxt - 1), 0))
    du_spec = rt(POOL_WIDTH) if du_has_ctx else lat_spec
    return pl.pallas_call(
        body, name=name, grid=(cfg.ntt,),
        in_specs=[lat_spec, lat_spec, lat_spec, du_spec] + [pl.BlockSpec((TM, W), lambda i: (0, 0))] * len(flat_ctx)
                 + [rt(LANES), rt(LANES)],
        out_specs=rt(IN_WIDTH), out_shape=S_((cfg.T, IN_WIDTH), bf16),
        compiler_params=_cp(VMEM_MID, ("arbitrary",)),
    )(*lat, *flat_ctx, cos, sin)


def _tmpre_bwd(cfg, dproj, w_in, xs, mods, gvec, dres, res_with_ctx, name):
    TM, D = cfg.TM, cfg.D
    nt, R = cfg.ntt, cfg.T
    nres = cfg.ntiles(res_with_ctx)

    def body(dp_ref, w_ref, xs_ref, mods_ref, g_ref, dres_ref, dx_ref, dm_ref, dg_ref):
        i = pl.program_id(0)

        @pl.when(i == 0)
        def _():
            dg_ref[...] = jnp.zeros_like(dg_ref)

        @pl.when((i == 0) | (i == cfg.nxt))
        def _():
            dm_ref[...] = jnp.zeros_like(dm_ref)

        dh = _nt(dp_ref[...], w_ref[...])
        x = xs_ref[...]
        m = mods_ref[0]
        g2 = g_ref[2:3]
        xhat, rinv = _rms_hat(x)
        d_sh = _rsum(dh)
        d_sc = _rsum(dh * (xhat * g2))
        dn = dh * (1.0 + m[4:5])
        dg_ref[...] += _rsum(dn * xhat)
        dx = _rms_bwd(dn * g2, xhat, rinv)
        res = dres_ref[...]
        if nres < nt:
            res = jnp.where(i < nres, res, 0.0)
        dx_ref[...] = res + dx
        dm_ref[0] += jnp.concatenate([d_sh, d_sc], axis=0)

    rt = lambda c: pl.BlockSpec((TM, c), lambda i: (i, 0))
    return pl.pallas_call(
        body, name=name, grid=(nt,),
        in_specs=[rt(IN_WIDTH), pl.BlockSpec((D, IN_WIDTH), lambda i: (0, 0)), rt(D),
                  pl.BlockSpec((1, N_MOD, D), _typ(cfg)), pl.BlockSpec((6, D), lambda i: (0, 0)),
                  pl.BlockSpec((TM, D), lambda i: (jnp.minimum(i, nres - 1), 0))],
        out_specs=[rt(D), pl.BlockSpec((1, 2, D), _typ(cfg)), pl.BlockSpec((1, D), lambda i: (0, 0))],
        out_shape=[S_((R, D), f32), S_((2, 2, D), f32), S_((1, D), f32)],
        compiler_params=_cp(VMEM_MID, ("arbitrary",)),
    )(dproj, w_in, xs, mods, gvec, dres)


def _na_block(cfg, b):
    return jnp.clip(NA_QR * b - NA_KH // 2, 0, cfg.rows - NA_WR)


def _na_load_bias(b, nb, b_hbm, b_v, sem):
    for typ, at in ((0, 0), (1, 1), (2, nb - 1)):
        @pl.when(b == at)
        def _(typ=typ):
            cp = pltpu.make_async_copy(b_hbm.at[typ], b_v, sem)
            cp.start()
            cp.wait()


def _na_probs(qh, klh, kch, bias):
    s_loc = _nt(qh, klh) + bias
    s_ctx = _nt(qh, kch)
    mx = jnp.maximum(jnp.max(s_loc, axis=-1, keepdims=True), jnp.max(s_ctx, axis=-1, keepdims=True))
    e_loc = jnp.exp(s_loc - mx)
    e_ctx = jnp.exp(s_ctx - mx)
    inv = 1.0 / (jnp.sum(e_loc, axis=-1, keepdims=True) + jnp.sum(e_ctx, axis=-1, keepdims=True))
    return e_loc * inv, e_ctx * inv


def _na_fwd(cfg, q, k, v, bexp, name, comm=None):
    S, L, T = cfg.S, cfg.L, cfg.T
    NQ, NW = NA_QR * GRID_W, NA_WR * GRID_W
    nb = cfg.rows // NA_QR

    def body(q_ref, k_hbm, v_hbm, b_hbm, o_ref, k_v, v_v, b_v, sem):
        b = pl.program_id(0)

        @pl.when(b == 0)
        def _():
            cs = [pltpu.make_async_copy(k_hbm, k_v, sem.at[0]), pltpu.make_async_copy(v_hbm, v_v, sem.at[1])]
            for c_ in cs:
                c_.start()
            for c_ in cs:
                c_.wait()

        _na_load_bias(b, nb, b_hbm, b_v, sem.at[2])
        st = pl.multiple_of(_na_block(cfg, b) * GRID_W, GRID_W)
        first = lax.broadcasted_iota(jnp.int32, (NQ, LANES), 1) < HEAD_DIM
        for hp in range(NA_HEADS // 2):
            ls = slice(hp * LANES, (hp + 1) * LANES)
            q2 = q_ref[:, ls]
            kl, vl = k_v[pl.ds(st, NW), ls], v_v[pl.ds(st, NW), ls]
            kc, vc = k_v[S:T, ls], v_v[S:T, ls]
            o2 = []
            for hh in range(2):
                qm = jnp.where(first if hh == 0 else ~first, q2, jnp.zeros_like(q2))
                p_loc, p_ctx = _na_probs(qm, kl, kc, b_v[2 * hp + hh])
                o2.append(_nn(p_loc.astype(bf16), vl) + _nn(p_ctx.astype(bf16), vc))
            o_ref[:, ls] = jnp.where(first, o2[0], o2[1]).astype(bf16)

    return _call(
        body, (q, k, v, bexp), comm, name=name, grid=(nb,),
        in_specs=[pl.BlockSpec((NQ, NA_WIDTH), lambda b: (b, 0)), ANY, ANY, ANY],
        out_specs=[pl.BlockSpec((NQ, NA_WIDTH), lambda b: (b, 0))],
        out_shape=[S_((S, NA_WIDTH), bf16)],
        scratch_shapes=[pltpu.VMEM((T, NA_WIDTH), bf16), pltpu.VMEM((T, NA_WIDTH), bf16),
                        pltpu.VMEM((NA_HEADS, NQ, NW), f32), pltpu.SemaphoreType.DMA((3,))],
        compiler_params=_cp(VMEM_MID, ("arbitrary",)),
    )


def _na_bwd(cfg, do, q, k, v, bexp, name, comm=None):
    S, L, T, rows = cfg.S, cfg.L, cfg.T, cfg.rows
    NQ, NW = NA_QR * GRID_W, NA_WR * GRID_W
    NSLOT = 2 * NA_KH
    nb = rows // NA_QR
    bmax = (rows - NA_WR) // NA_QR
    steps = 2 * nb - bmax
    W = NA_WIDTH
    assert nb >= 3 and bmax >= 1 and rows - NA_QR * bmax <= NSLOT

    def out_group(g):
        return jnp.where(g >= nb, g - nb + bmax, jnp.clip(g - 1, 0, bmax - 1))

    def body(do_ref, q_ref, k_hbm, v_hbm, b_hbm, dq_ref, dk_ref, dv_ref, dkc_ref, dvc_ref, db_hbm,
             k_v, v_v, b_v, db_v, ak, av, akc, avc, sem):
        g = pl.program_id(0)

        @pl.when(g == 0)
        def _():
            cs = [pltpu.make_async_copy(k_hbm, k_v, sem.at[0]), pltpu.make_async_copy(v_hbm, v_v, sem.at[1])]
            for c_ in cs:
                c_.start()
            db_v[...] = jnp.zeros_like(db_v)
            ak[...] = jnp.zeros_like(ak)
            av[...] = jnp.zeros_like(av)
            akc[...] = jnp.zeros_like(akc)
            avc[...] = jnp.zeros_like(avc)
            for c_ in cs:
                c_.wait()

        for typ, at in ((0, 1), (1, nb - 1)):
            @pl.when(g == at)
            def _(typ=typ):
                cp = pltpu.make_async_copy(db_v, db_hbm.at[typ], sem.at[2])
                cp.start()
                cp.wait()
                db_v[...] = jnp.zeros_like(db_v)

        @pl.when(g < nb)
        def _():
            _na_load_bias(g, nb, b_hbm, b_v, sem.at[2])
            ws = _na_block(cfg, g)
            st = pl.multiple_of(ws * GRID_W, GRID_W)
            first = lax.broadcasted_iota(jnp.int32, (NQ, LANES), 1) < HEAD_DIM
            for hp in range(NA_HEADS // 2):
                ls = slice(hp * LANES, (hp + 1) * LANES)
                q2, do2 = q_ref[:, ls], do_ref[:, ls]
                kl, vl = k_v[pl.ds(st, NW), ls], v_v[pl.ds(st, NW), ls]
                kc, vc = k_v[S:T, ls], v_v[S:T, ls]
                dq2 = []
                dk2 = jnp.zeros((NW, LANES), f32)
                dv2 = jnp.zeros((NW, LANES), f32)
                dkc2 = jnp.zeros((L, LANES), f32)
                dvc2 = jnp.zeros((L, LANES), f32)
                for hh in range(2):
                    keep = first if hh == 0 else ~first
                    qm = jnp.where(keep, q2, jnp.zeros_like(q2))
                    dom = jnp.where(keep, do2, jnp.zeros_like(do2))
                    p_loc, p_ctx = _na_probs(qm, kl, kc, b_v[2 * hp + hh])
                    dp_loc = _nt(dom, vl)
                    dp_ctx = _nt(dom, vc)
                    delta = jnp.sum(p_loc * dp_loc, axis=-1, keepdims=True) + jnp.sum(p_ctx * dp_ctx, axis=-1, keepdims=True)
                    ds_loc = p_loc * (dp_loc - delta)
                    ds_ctx = p_ctx * (dp_ctx - delta)
                    db_v[2 * hp + hh, :, 0:NW] += ds_loc
                    dsl, dsc = ds_loc.astype(bf16), ds_ctx.astype(bf16)
                    dq2.append(_nn(dsl, kl) + _nn(dsc, kc))
                    dk2 = dk2 + _tn(dsl, qm)
                    dv2 = dv2 + _tn(p_loc.astype(bf16), dom)
                    dkc2 = dkc2 + _tn(dsc, qm)
                    dvc2 = dvc2 + _tn(p_ctx.astype(bf16), dom)
                dq_ref[:, ls] = jnp.where(first, dq2[0], dq2[1])
                akc[:, ls] += dkc2
                avc[:, ls] += dvc2
                for kk in range(NA_WR):
                    slot = (ws + kk) % NSLOT
                    ak[slot, :, ls] += dk2[kk * GRID_W:(kk + 1) * GRID_W, :]
                    av[slot, :, ls] += dv2[kk * GRID_W:(kk + 1) * GRID_W, :]

        @pl.when(((g >= 1) & (g <= bmax)) | (g >= nb))
        def _():
            base = NA_QR * (out_group(g) % (NSLOT // NA_QR))
            for t in range(NA_QR):
                dk_ref[t * GRID_W:(t + 1) * GRID_W, :] = ak[base + t]
                dv_ref[t * GRID_W:(t + 1) * GRID_W, :] = av[base + t]
                ak[base + t] = jnp.zeros((GRID_W, W), f32)
                av[base + t] = jnp.zeros((GRID_W, W), f32)

        @pl.when(g == nb - 1)
        def _():
            cp = pltpu.make_async_copy(db_v, db_hbm.at[2], sem.at[2])
            cp.start()
            cp.wait()

        @pl.when(g == steps - 1)
        def _():
            dkc_ref[...] = akc[...]
            dvc_ref[...] = avc[...]

    qmap = lambda g: (jnp.minimum(g, nb - 1), 0)
    kmap = lambda g: (out_group(g), 0)
    full = lambda g: (0, 0)
    return _call(
        body, (do, q, k, v, bexp), comm, name=name, grid=(steps,),
        in_specs=[pl.BlockSpec((NQ, W), qmap), pl.BlockSpec((NQ, W), qmap), ANY, ANY, ANY],
        out_specs=[pl.BlockSpec((NQ, W), qmap), pl.BlockSpec((NQ, W), kmap), pl.BlockSpec((NQ, W), kmap),
                   pl.BlockSpec((L, W), full), pl.BlockSpec((L, W), full), ANY],
        out_shape=[S_((S, W), f32), S_((S, W), f32), S_((S, W), f32), S_((L, W), f32), S_((L, W), f32),
                   S_((NA_TYPES, NA_HEADS, NQ, NA_WPAD), f32)],
        scratch_shapes=[pltpu.VMEM((T, W), bf16), pltpu.VMEM((T, W), bf16),
                        pltpu.VMEM((NA_HEADS, NQ, NW), f32), pltpu.VMEM((NA_HEADS, NQ, NA_WPAD), f32),
                        pltpu.VMEM((NSLOT, GRID_W, W), f32), pltpu.VMEM((NSLOT, GRID_W, W), f32),
                        pltpu.VMEM((L, W), f32), pltpu.VMEM((L, W), f32), pltpu.SemaphoreType.DMA((3,))],
        compiler_params=_cp(VMEM_BIG, ("arbitrary",)),
    )


def _rpb_reduce(dbias, flip, sel, name):
    nq, w = NA_QR * GRID_W, GRID_W

    def diag_body(x_ref, j_ref, o_ref):
        rows = []
        for i in range(NA_QR):
            xr = jnp.dot(j_ref[...], x_ref[i * w:(i + 1) * w, :], preferred_element_type=f32, precision=lax.Precision.HIGHEST)
            rows.append(jnp.sum(pltpu.roll(xr, 0, 1, stride=1, stride_axis=0), axis=0, keepdims=True))
        o_ref[...] = jnp.concatenate(rows + [jnp.zeros((8 - NA_QR, NA_WPAD), f32)], axis=0)

    diag = pl.pallas_call(
        diag_body, name=name + "_diag", grid=(NA_TYPES, NA_HEADS),
        in_specs=[pl.BlockSpec((None, None, nq, NA_WPAD), lambda t, h: (t, h, 0, 0)), pl.BlockSpec((w, w), lambda t, h: (0, 0))],
        out_specs=pl.BlockSpec((None, None, 8, NA_WPAD), lambda t, h: (t, h, 0, 0)),
        out_shape=S_((NA_TYPES, NA_HEADS, 8, NA_WPAD), f32),
        compiler_params=_cp(VMEM_MID, ("arbitrary", "arbitrary")),
    )(dbias, flip)
    lo = w - NA_KW
    y = diag[:, :, :NA_QR, lo:lo + NA_WR * w].reshape(NA_TYPES, NA_HEADS, NA_QR, NA_WR, w)
    y = jnp.transpose(y, (1, 0, 2, 3, 4)).reshape(NA_HEADS, NA_TYPES * NA_QR * NA_WR, w)
    y = jnp.pad(y, ((0, 0), (0, NA_SEL_ROWS - y.shape[1]), (0, LANES - w)))

    def body(y_ref, sel_ref, o_ref):
        o_ref[...] = jnp.dot(sel_ref[...], y_ref[...], preferred_element_type=f32, precision=lax.Precision.HIGHEST)

    return pl.pallas_call(
        body, name=name, grid=(NA_HEADS,),
        in_specs=[pl.BlockSpec((None, NA_SEL_ROWS, LANES), lambda h: (h, 0, 0)), pl.BlockSpec((16, NA_SEL_ROWS), lambda h: (0, 0))],
        out_specs=pl.BlockSpec((None, 16, LANES), lambda h: (h, 0, 0)),
        out_shape=S_((NA_HEADS, 16, LANES), f32),
        compiler_params=_cp(VMEM_MID, ("arbitrary",)),
    )(y, sel)


def _ctx_attn_fwd(cfg, q, k, v, name):
    L = cfg.L
    blk = cfg.S // L

    def body(q_ref, k_ref, v_ref, o_ref):
        qv, kv, vv = q_ref[...], k_ref[...], v_ref[...]
        outs = []
        for h in range(NA_HEADS):
            hs = slice(h * HEAD_DIM, (h + 1) * HEAD_DIM)
            s = _nt(qv[:, hs], kv[:, hs])
            e = jnp.exp(s - jnp.max(s, axis=-1, keepdims=True))
            p = e * (1.0 / jnp.sum(e, axis=-1, keepdims=True))
            outs.append(_nn(p.astype(bf16), vv[:, hs]))
        o_ref[...] = jnp.concatenate(outs, axis=-1).astype(bf16)

    spec = pl.BlockSpec((L, NA_WIDTH), lambda i: (blk, 0))
    return pl.pallas_call(
        body, name=name, grid=(1,), in_specs=[spec, spec, spec],
        out_specs=pl.BlockSpec((L, NA_WIDTH), lambda i: (0, 0)), out_shape=S_((L, NA_WIDTH), bf16),
        compiler_params=_cp(VMEM_MID, ("arbitrary",)),
    )(q, k, v)


def _ctx_attn_bwd(cfg, do, q, k, v, name):
    L = cfg.L
    blk = cfg.S // L

    def body(do_ref, q_ref, k_ref, v_ref, dq_ref, dk_ref, dv_ref):
        dov, qv, kv, vv = do_ref[...], q_ref[...], k_ref[...], v_ref[...]
        dqs, dks, dvs = [], [], []
        for h in range(NA_HEADS):
            hs = slice(h * HEAD_DIM, (h + 1) * HEAD_DIM)
            qh, kh, doh = qv[:, hs], kv[:, hs], dov[:, hs]
            s = _nt(qh, kh)
            e = jnp.exp(s - jnp.max(s, axis=-1, keepdims=True))
            p = e * (1.0 / jnp.sum(e, axis=-1, keepdims=True))
            dp = _nt(doh, vv[:, hs])
            ds = (p * (dp - jnp.sum(p * dp, axis=-1, keepdims=True))).astype(bf16)
            dqs.append(_nn(ds, kh))
            dks.append(_tn(ds, qh))
            dvs.append(_tn(p.astype(bf16), doh))
        dq_ref[...] = jnp.concatenate(dqs, axis=-1)
        dk_ref[...] = jnp.concatenate(dks, axis=-1)
        dv_ref[...] = jnp.concatenate(dvs, axis=-1)

    spec = pl.BlockSpec((L, NA_WIDTH), lambda i: (blk, 0))
    ospec = pl.BlockSpec((L, NA_WIDTH), lambda i: (0, 0))
    return pl.pallas_call(
        body, name=name, grid=(1,), in_specs=[spec, spec, spec, spec],
        out_specs=[ospec, ospec, ospec], out_shape=[S_((L, NA_WIDTH), f32)] * 3,
        compiler_params=_cp(VMEM_MID, ("arbitrary",)),
    )(do, q, k, v)


def _pool_centered(u, band, inv):
    return _split_sum(_nn, band, u) * inv - u


def _split_sum(mm, band, t):
    hi = t.astype(bf16)
    lo = (t - hi.astype(f32)).astype(bf16)
    s = mm(band, jnp.concatenate([hi, lo], axis=1))
    n = t.shape[1]
    return s[:, :n] + s[:, n:]


def _pool_fwd(cfg, u, band, inv, w_pool, pool_scale, with_ctx, name):
    TM = cfg.TM
    nt = cfg.ntiles(with_ctx)
    C = POOL_CH

    def body(u_ref, band_ref, inv_ref, w_ref, ps_ref, o_ref):
        outs = []
        for g in range(POOL_GROUPS):
            d = _pool_centered(u_ref[:, g * C:(g + 1) * C], band_ref[0, g], inv_ref[0, g])
            outs.append(_nn(d.astype(bf16), w_ref[g].astype(bf16)) * ps_ref[:, g * C:(g + 1) * C])
        o_ref[...] = jnp.concatenate(outs, axis=-1).astype(bf16)

    typ4 = lambda i: (jnp.minimum(i // cfg.nxt, 1), 0, 0, 0)
    return pl.pallas_call(
        body, name=name, grid=(nt,),
        in_specs=[pl.BlockSpec((TM, POOL_WIDTH), lambda i: (i, 0)), pl.BlockSpec((1, POOL_GROUPS, TM, TM), typ4),
                  pl.BlockSpec((1, POOL_GROUPS, TM, 1), typ4), pl.BlockSpec((POOL_GROUPS, C, C), lambda i: (0, 0, 0)),
                  pl.BlockSpec((1, POOL_WIDTH), lambda i: (0, 0))],
        out_specs=pl.BlockSpec((TM, POOL_WIDTH), lambda i: (i, 0)),
        out_shape=S_((nt * TM, POOL_WIDTH), bf16),
        compiler_params=_cp(VMEM_MID, ("arbitrary",)),
    )(u, band, inv, w_pool, pool_scale)


def _pool_bwd(cfg, dmix, u, band, inv, w_pool, pool_scale, with_ctx, name):
    TM = cfg.TM
    nt = cfg.ntiles(with_ctx)
    C = POOL_CH

    def body(dy_ref, u_ref, band_ref, inv_ref, w_ref, ps_ref, du_ref, dw_ref, dps_ref):
        @pl.when(pl.program_id(0) == 0)
        def _():
            dw_ref[...] = jnp.zeros_like(dw_ref)
            dps_ref[...] = jnp.zeros_like(dps_ref)

        dus, dpss = [], []
        for g in range(POOL_GROUPS):
            gs = slice(g * C, (g + 1) * C)
            band_g, inv_g = band_ref[0, g], inv_ref[0, g]
            db = _pool_centered(u_ref[:, gs], band_g, inv_g).astype(bf16)
            wb = w_ref[g].astype(bf16)
            dy = dy_ref[:, gs].astype(f32)
            dpss.append(_rsum(dy * _nn(db, wb)))
            dys = (dy * ps_ref[:, gs]).astype(bf16)
            dw_ref[g] += _tn(db, dys)
            dd = _nt(dys, wb)
            dus.append(_split_sum(_tn, band_g, dd * inv_g) - dd)
        du_ref[...] = jnp.concatenate(dus, axis=-1)
        dps_ref[...] += jnp.concatenate(dpss, axis=-1)

    typ4 = lambda i: (jnp.minimum(i // cfg.nxt, 1), 0, 0, 0)
    return pl.pallas_call(
        body, name=name, grid=(nt,),
        in_specs=[pl.BlockSpec((TM, POOL_WIDTH), lambda i: (i, 1)), pl.BlockSpec((TM, POOL_WIDTH), lambda i: (i, 0)),
                  pl.BlockSpec((1, POOL_GROUPS, TM, TM), typ4), pl.BlockSpec((1, POOL_GROUPS, TM, 1), typ4),
                  pl.BlockSpec((POOL_GROUPS, C, C), lambda i: (0, 0, 0)), pl.BlockSpec((1, POOL_WIDTH), lambda i: (0, 0))],
        out_specs=[pl.BlockSpec((TM, POOL_WIDTH), lambda i: (i, 0)), pl.BlockSpec((POOL_GROUPS, C, C), lambda i: (0, 0, 0)),
                   pl.BlockSpec((1, POOL_WIDTH), lambda i: (0, 0))],
        out_shape=[S_((nt * TM, POOL_WIDTH), f32), S_((POOL_GROUPS, C, C), f32), S_((1, POOL_WIDTH), f32)],
        compiler_params=_cp(VMEM_MID, ("arbitrary",)),
    )(dmix, u, band, inv, w_pool, pool_scale)


def _tmpost_fwd(cfg, na_x, na_c, pool, w_out, xs, mods, gvec, name):
    TM, D = cfg.TM, cfg.D
    with_ctx = na_c is not None
    nt = cfg.ntiles(with_ctx)
    R = nt * TM

    def body(*refs):
        if with_ctx:
            nax_ref, nac_ref, pool_ref, w_ref, xs_ref, mods_ref, g_ref, out_ref, opre_ref, mix_ref = refs
            na = jnp.where(pl.program_id(0) < cfg.nxt, nax_ref[...], nac_ref[...])
        else:
            nax_ref, pool_ref, w_ref, xs_ref, mods_ref, g_ref, out_ref, opre_ref, mix_ref = refs
            na = nax_ref[...]
        pool_v = pool_ref[...]
        mix_ref[:, 0:NA_WIDTH] = na
        mix_ref[:, NA_WIDTH:] = pool_v
        o = _nn(na, w_ref[0:NA_WIDTH, :]) + _nn(pool_v, w_ref[NA_WIDTH:, :])
        opre_ref[...] = o
        ohat, _ = _rms_hat(o)
        out_ref[...] = xs_ref[...] + mods_ref[0][5:6] * (ohat * g_ref[3:4])

    rt = lambda c: pl.BlockSpec((TM, c), lambda i: (i, 0))
    na_specs = [pl.BlockSpec((TM, NA_WIDTH), lambda i: (jnp.minimum(i, cfg.nxt - 1), 0))]
    na_args = [na_x]
    if with_ctx:
        na_specs.append(pl.BlockSpec((TM, NA_WIDTH), lambda i: (0, 0)))
        na_args.append(na_c)
    return pl.pallas_call(
        body, name=name, grid=(nt,),
        in_specs=na_specs + [rt(POOL_WIDTH), pl.BlockSpec((MIX_WIDTH, D), lambda i: (0, 0)), rt(D),
                             pl.BlockSpec((1, N_MOD, D), _typ(cfg)), pl.BlockSpec((6, D), lambda i: (0, 0))],
        out_specs=[rt(D), rt(D), rt(MIX_WIDTH)],
        out_shape=[S_((R, D), f32), S_((R, D), f32), S_((R, MIX_WIDTH), bf16)],
        compiler_params=_cp(VMEM_MID, ("arbitrary",)),
    )(*na_args, pool, w_out, xs, mods, gvec)


def _tmpost_bwd(cfg, dout, opre, w_out, mods, gvec, with_ctx, name):
    TM, D = cfg.TM, cfg.D
    nt = cfg.ntiles(with_ctx)
    R = nt * TM
    ntyp = 2 if with_ctx else 1

    def body(do_ref, opre_ref, w_ref, mods_ref, g_ref, dop_ref, dmix_ref, dm_ref, dg_ref):
        i = pl.program_id(0)

        @pl.when(i == 0)
        def _():
            dg_ref[...] = jnp.zeros_like(dg_ref)

        @pl.when((i == 0) | (i == cfg.nxt))
        def _():
            dm_ref[...] = jnp.zeros_like(dm_ref)

        do = do_ref[...]
        g3 = g_ref[3:4]
        ohat, rinv = _rms_hat(opre_ref[...])
        dm_ref[0] += _rsum(do * (ohat * g3))
        dr = mods_ref[0][5:6] * do
        dg_ref[...] += _rsum(dr * ohat)
        dob = _rms_bwd(dr * g3, ohat, rinv).astype(bf16)
        dop_ref[...] = dob
        dmix_ref[...] = _nt(dob, w_ref[...]).astype(bf16)

    rt = lambda c: pl.BlockSpec((TM, c), lambda i: (i, 0))
    return pl.pallas_call(
        body, name=name, grid=(nt,),
        in_specs=[rt(D), rt(D), pl.BlockSpec((MIX_WIDTH, D), lambda i: (0, 0)),
                  pl.BlockSpec((1, N_MOD, D), _typ(cfg)), pl.BlockSpec((6, D), lambda i: (0, 0))],
        out_specs=[rt(D), rt(MIX_WIDTH), pl.BlockSpec((1, 1, D), _typ(cfg)), pl.BlockSpec((1, D), lambda i: (0, 0))],
        out_shape=[S_((R, D), bf16), S_((R, MIX_WIDTH), bf16), S_((ntyp, 1, D), f32), S_((1, D), f32)],
        compiler_params=_cp(VMEM_MID, ("arbitrary",)),
    )(dout, opre, w_out, mods, gvec)


def _modvec_fwd(cvecs, w_mod, b_shard, name):
    nl, D, n = w_mod.shape
    tn = n // 3 if (n % 3 == 0 and (n // 3) % LANES == 0) else n

    def body(c_ref, w_ref, b_ref, o_ref, s_ref):
        cv = c_ref[...]
        sv = cv * _sigmoid(cv)
        s_ref[...] = sv
        o_ref[...] = _nn(sv.astype(bf16), w_ref[...].astype(bf16)) + b_ref[...]

    return pl.pallas_call(
        body, name=name, grid=(nl, n // tn),
        in_specs=[pl.BlockSpec((16, D), lambda l, j: (0, 0)), pl.BlockSpec((None, D, tn), lambda l, j: (l, 0, j)),
                  pl.BlockSpec((None, 1, tn), lambda l, j: (l, 0, j))],
        out_specs=[pl.BlockSpec((None, 16, tn), lambda l, j: (l, 0, j)), pl.BlockSpec((16, D), lambda l, j: (0, 0))],
        out_shape=[S_((nl, 16, n), f32), S_((16, D), f32)],
        compiler_params=_cp(VMEM_MID, ("arbitrary", "arbitrary")),
    )(cvecs, w_mod, b_shard)


def _modvec_bwd(s_t, dm, w_mod, name):
    nl, D, n = w_mod.shape
    tn = n // 3 if (n % 3 == 0 and (n // 3) % LANES == 0) else n

    def body(s_ref, dm_ref, w_ref, gw_ref, gc_ref):
        @pl.when(pl.program_id(1) == 0)
        def _():
            gc_ref[...] = jnp.zeros_like(gc_ref)
        dmv = dm_ref[...]
        gw_ref[...] = jnp.dot(s_ref[...], dmv, preferred_element_type=f32, precision=lax.Precision.HIGHEST)
        gc_ref[...] += _nt(dmv[8:16].astype(bf16), w_ref[...].astype(bf16))

    return pl.pallas_call(
        body, name=name, grid=(nl, n // tn),
        in_specs=[pl.BlockSpec((D, 16), lambda l, j: (0, 0)), pl.BlockSpec((None, 16, tn), lambda l, j: (l, 0, j)),
                  pl.BlockSpec((None, D, tn), lambda l, j: (l, 0, j))],
        out_specs=[pl.BlockSpec((None, D, tn), lambda l, j: (l, 0, j)), pl.BlockSpec((None, 8, D), lambda l, j: (l, 0, 0))],
        out_shape=[S_((nl, D, n), f32), S_((nl, 8, D), f32)],
        compiler_params=_cp(VMEM_MID, ("arbitrary", "arbitrary")),
    )(s_t, dm, w_mod)


def _as2d(a):
    n = a.size
    if a.ndim >= 2 and a.shape[-1] % LANES == 0:
        return a.reshape(-1, a.shape[-1])
    if n % LANES == 0:
        return a.reshape(-1, LANES)
    return a.reshape(-1, a.shape[-1]) if a.ndim >= 2 else a.reshape(1, n)


def _row_tile(r, c, budget_elems=512 * 1024):
    if r * c <= budget_elems or r % 8 != 0:
        return r
    t = r
    while t * c > budget_elems and t % 16 == 0:
        t //= 2
    return t


def _div_tile(r, c, budget_elems, mult=16):
    best = None
    for t in range(mult, r + 1, mult):
        if r % t == 0 and t * c <= budget_elems:
            best = t
    return best if best is not None else r


def _cast_into_place(shards, lead, axis, kidx, name):
    r, c = shards.shape[-2:]
    tr = _div_tile(r, c, 768 * 1024)
    nr = r // tr
    out_map = (lambda i, k: (i, k[0])) if axis == 1 else (lambda i, k: (k[0] * nr + i, 0))
    full2 = (r, c * N_CHIPS) if axis == 1 else (r * N_CHIPS, c)

    def body(k_ref, a_ref, o_ref):
        o_ref[...] = a_ref[...].astype(bf16)

    return pl.pallas_call(
        body, name=name,
        grid_spec=pltpu.PrefetchScalarGridSpec(
            num_scalar_prefetch=1, grid=(nr,),
            in_specs=[pl.BlockSpec((None,) * len(lead) + (tr, c), lambda i, k: tuple(lead) + (i, 0))],
            out_specs=pl.BlockSpec((tr, c), out_map)),
        out_shape=S_(full2, bf16), compiler_params=_cp(VMEM_MID, ("arbitrary",)),
    )(kidx, shards)


def _sum_devices8(own, land, axis, into, lead, dck, name):
    _, rh, cs = land.shape
    tr = _div_tile(rh, cs, 400 * 1024)
    nr = rh // tr
    if axis == 1:
        own_map = lambda i, s: (s[1] * nr + i, s[2])
    else:
        own_map = lambda i, s: (s[2] * 2 * nr + s[1] * nr + i, 0)
    nl = len(lead)

    def land_spec(j):
        return pl.BlockSpec((None, tr, cs), lambda i, s: ((s[0] + j) % N_DEV, i, 0))

    def body(s_ref, own_ref, *rest):
        acc = own_ref[...]
        for p_ref in rest[:N_DEV - 1]:
            acc = acc + p_ref[...].astype(f32)
        rest[-1][...] = acc

    return pl.pallas_call(
        body, name=name,
        grid_spec=pltpu.PrefetchScalarGridSpec(
            num_scalar_prefetch=1, grid=(nr,),
            in_specs=[pl.BlockSpec((tr, cs), own_map)] + [land_spec(j) for j in range(1, N_DEV)] + [ANY],
            out_specs=pl.BlockSpec((None,) * nl + (tr, cs), lambda i, s: tuple(lead) + (s[1] * nr + i, 0))),
        out_shape=S_(into.shape, f32), input_output_aliases={N_DEV + 1: 0},
        compiler_params=_cp(VMEM_MID, ("arbitrary",)),
    )(dck, own, *([land] * (N_DEV - 1)), into)


def _adamw(w, g, m, v, name):
    shape = w.shape
    w2, g2, m2, v2 = _as2d(w), _as2d(g), _as2d(m), _as2d(v)
    r, c = w2.shape
    tr = _row_tile(r, c, 256 * 1024)
    c1 = 1.0 - ADAM_B1 ** ADAM_STEP
    c2 = 1.0 - ADAM_B2 ** ADAM_STEP

    def body(w_ref, g_ref, m_ref, v_ref, d_ref, mo_ref, vo_ref):
        gv = g_ref[...]
        mn = ADAM_B1 * m_ref[...] + (1.0 - ADAM_B1) * gv
        vn = ADAM_B2 * v_ref[...] + (1.0 - ADAM_B2) * (gv * gv)
        mo_ref[...] = mn
        vo_ref[...] = vn
        d_ref[...] = -ADAM_LR * ((mn / c1) / (jnp.sqrt(vn / c2) + ADAM_EPS) + ADAM_WD * w_ref[...])

    spec = pl.BlockSpec((tr, c), lambda i: (i, 0))
    outs = pl.pallas_call(body, name=name, grid=(r // tr,), in_specs=[spec] * 4, out_specs=[spec] * 3,
                          out_shape=[S_((r, c), f32)] * 3, compiler_params=_cp(VMEM_MID, ("arbitrary",)))(w2, g2, m2, v2)
    return tuple(o.reshape(shape) for o in outs)


def _sum_devices(gathered, name):
    _, r, c = gathered.shape

    def body(a_ref, o_ref):
        acc = a_ref[0]
        for j in range(1, N_DEV):
            acc = acc + a_ref[j]
        o_ref[...] = acc

    tr = _row_tile(r, c, 64 * 1024)
    return pl.pallas_call(
        body, name=name, grid=(r // tr,),
        in_specs=[pl.BlockSpec((N_DEV, tr, c), lambda i: (0, i, 0))], out_specs=pl.BlockSpec((tr, c), lambda i: (i, 0)),
        out_shape=S_((r, c), f32), compiler_params=_cp(VMEM_MID, ("arbitrary",)))(gathered)


def _all_gather_small(block, name):
    m_per, n = block.shape

    def body(x_ref, out_ref, send_sems, recv_sems, local_sem):
        x, y, c = _mesh_pos()
        me, sibling = (x, y, c), (x, y, 1 - c)
        chips = [(1 - x, y), (x, 1 - y), (1 - x, 1 - y)]

        def rows(px, py, pc):
            return out_ref.at[pl.ds((4 * px + 2 * py + pc) * m_per, m_per), :]

        def copy(k, blk, to, src=None):
            return pltpu.make_async_remote_copy(
                src_ref=rows(*blk) if src is None else src, dst_ref=rows(*blk),
                send_sem=send_sems.at[k], recv_sem=recv_sems.at[k], device_id=to, device_id_type=MESH)

        mine = pltpu.make_async_copy(x_ref, rows(*me), local_sem)
        mine.start()
        first = [copy(0, me, sibling, src=x_ref)]
        first += [copy(1 + j, me, (*chip, c), src=x_ref) for j, chip in enumerate(chips)]
        for cp in first:
            cp.start()
        passed = [copy(4 + j, (*chip, c), sibling) for j, chip in enumerate(chips)]
        for j, chip in enumerate(chips):
            copy(1 + j, (*chip, c), me).wait_recv()
            passed[j].start()
        copy(0, sibling, me).wait_recv()
        for j, chip in enumerate(chips):
            copy(4 + j, (*chip, 1 - c), me).wait_recv()
        for cp in first + passed:
            cp.wait_send()
        mine.wait()

    return pl.pallas_call(
        body, name=name, out_shape=S_((N_DEV * m_per, n), block.dtype),
        in_specs=[pl.BlockSpec(memory_space=pltpu.VMEM)], out_specs=pl.BlockSpec(memory_space=pltpu.VMEM),
        scratch_shapes=[pltpu.SemaphoreType.DMA((7,)), pltpu.SemaphoreType.DMA((7,)), pltpu.SemaphoreType.DMA],
        compiler_params=_cp(VMEM_MID),
    )(block)


def _pack_rows(arrays):
    flat = jnp.concatenate([a.reshape(-1) for a in arrays])
    pad = (-flat.size) % (8 * LANES)
    return jnp.pad(flat, (0, pad)).reshape(-1, LANES)


def _unpack_rows(packed, shapes):
    flat = packed.reshape(-1)
    out, off = [], 0
    for s in shapes:
        n = int(np.prod(s))
        out.append(flat[off:off + n].reshape(s))
        off += n
    return out


W_AXIS = {"gu": 1, "dn": 0, "wi": 1, "wo": 0}


def _half_merge(bufs, name):
    nt = len(bufs)

    def body(*refs):
        outs = refs[nt:2 * nt]
        send_sems, recv_sems = refs[2 * nt:]
        x, y, c = _mesh_pos()

        def half(ref, h):
            rh = ref.shape[-2] // 2
            return ref.at[(slice(None),) * (len(ref.shape) - 2) + (pl.ds(h * rh, rh), slice(None))]

        cps = []
        for t in range(nt):
            cp = pltpu.make_async_remote_copy(
                src_ref=half(outs[t], c), dst_ref=half(outs[t], c), send_sem=send_sems.at[t], recv_sem=recv_sems.at[t],
                device_id=(x, y, 1 - c), device_id_type=MESH)
            cp.start()
            cps.append(cp)
        for t in range(nt):
            pltpu.make_async_remote_copy(
                src_ref=half(outs[t], 1 - c), dst_ref=half(outs[t], 1 - c), send_sem=send_sems.at[t], recv_sem=recv_sems.at[t],
                device_id=(x, y, 1 - c), device_id_type=MESH).wait_recv()
        for cp in cps:
            cp.wait_send()

    return pl.pallas_call(
        body, name=name, in_specs=[ANY] * nt, out_specs=[ANY] * nt, out_shape=[S_(b.shape, f32) for b in bufs],
        input_output_aliases={t: t for t in range(nt)},
        scratch_shapes=[pltpu.SemaphoreType.DMA((nt,)), pltpu.SemaphoreType.DMA((nt,))],
        compiler_params=_cp(VMEM_MID),
    )(*bufs)


def _local_step(cfg, x_lat, x_ctx, target, mods, norm_g, W, G, dck, na_rpb, w_pool, pool_scale):
    S, L, T, D, F = cfg.S, cfg.L, cfg.T, cfg.D, cfg.F
    depth = norm_g.shape[0]
    cos, sin = _rope_tables(S, L)
    band, inv = _pool_tables(cfg.TM, L)
    flip, sel = _rpb_reduce_tables()

    assert depth == 2, "the carrier schedules below are written for two layers"
    fwd_carry = {"ffn_fwd_0_0": [("wi", 0), ("wo", 0), ("gu", 0, 1), ("dn", 0, 1)],
                 "na_fwd_0": [("gu", 1, 0), ("dn", 1, 0)],
                 "ffn_fwd_0_1": [("wi", 1), ("wo", 1), ("gu", 1, 1), ("dn", 1, 1)]}
    bwd_carry = {"na_bwd_1": [("gu", 1, 1), ("dn", 1, 1)], "ffn_bwd_1_0": [("wi", 1), ("wo", 1)],
                 "ffn_bwd_0_1": [("gu", 1, 0), ("dn", 1, 0)], "na_bwd_0": [("gu", 0, 1), ("dn", 0, 1)],
                 "ffn_bwd_0_0": [("wi", 0), ("wo", 0)], "wgrad_dn_0_0": [("gu", 0, 0)]}
    last_scatter = [("dn", 0, 0)]
    tag = lambda key: "_".join(str(p) for p in key)
    g_f32, g_b16 = {}, {}

    def gather_on(name):
        keys = fwd_carry.get(name)
        return None if keys is None else _gather_comm([W[k_] for k_ in keys], [W_AXIS[k_[0]] for k_ in keys])

    def gathered(name, res):
        if name in fwd_carry:
            W.update(zip(fwd_carry[name], res))

    def scatter_on(name):
        keys = bwd_carry.get(name)
        return None if keys is None else _scatter_comm([g_b16[k_] for k_ in keys], [W_AXIS[k_[0]] for k_ in keys])

    def scattered(keys, lands):
        for key, land in zip(keys, lands):
            G[key[0]] = _sum_devices8(g_f32[key], land, W_AXIS[key[0]], G[key[0]], key[1:], dck, f"sum8_{tag(key)}")

    def wgrad(key, a, b, rows):
        name = f"wgrad_{tag(key)}"
        (g_f32[key], g_b16[key]), lands = _wgrad(a, b, rows, name, scatter_on(name))
        scattered(bwd_carry.get(name, ()), lands)

    saved = []
    xs, xs_ctx = x_lat, x_ctx
    for l in range(depth):
        last = l == depth - 1
        wc = not last
        gvec = norm_g[l]
        ps = pool_scale[l].reshape(1, POOL_WIDTH)
        bexp = _expand_rpb(na_rpb[l], f"bias_expand_{l}")
        name = f"ffn_fwd_{l}_0"
        (xs1, hb1, z1, y1), res = _ffn_fwd(cfg, xs, mods[l], gvec, W["gu", l, 0], W["dn", l, 0], 0, 0, True, name,
                                           gather_on(name), xs_ctx=xs_ctx)
        gathered(name, res)
        hb2, q, k, v, u = _tmpre_fwd(cfg, xs1, mods[l], gvec, W["wi", l], cos, sin, f"tmpre_fwd_{l}")
        name = f"na_fwd_{l}"
        (na_x,), res = _na_fwd(cfg, q, k, v, bexp, name, gather_on(name))
        gathered(name, res)
        na_c = _ctx_attn_fwd(cfg, q, k, v, f"ctx_attn_fwd_{l}") if wc else None
        pool = _pool_fwd(cfg, u, band, inv, w_pool[l], ps, wc, f"pool_fwd_{l}")
        xs2, opre, mix = _tmpost_fwd(cfg, na_x, na_c, pool, W["wo", l], xs1, mods[l], gvec, f"tmpost_fwd_{l}")
        name = f"ffn_fwd_{l}_1"
        outs, res = _ffn_fwd(cfg, xs2, mods[l], gvec, W["gu", l, 1], W["dn", l, 1], 6, 4, wc, name, gather_on(name),
                             loss_target=target if last else None)
        xs3, hb3, z3, y3 = outs[:4]
        gathered(name, res)
        saved.append(dict(xs=xs, xs_ctx=xs_ctx, xs1=xs1, xs2=xs2, hb1=hb1, z1=z1, y1=y1, hb2=hb2, q=q, k=k, v=v, u=u, mix=mix,
                          opre=opre, hb3=hb3, z3=z3, y3=y3, bexp=bexp, ps=ps, gvec=gvec))
        xs, xs_ctx = xs3, None

    dxs, loss_blk = xs, outs[4]

    small = [None] * depth
    for l in reversed(range(depth)):
        last = l == depth - 1
        wc = not last
        sv = saved[l]
        gvec = sv["gvec"]
        rows_b = cfg.T if wc else cfg.S
        name = f"ffn_bwd_{l}_1"
        (dxs2, dz, dyb, ab, dm678, dg45), lands = _ffn_bwd(cfg, dxs, sv["xs2"], sv["z3"], sv["y3"], mods[l], gvec,
                                                           W["gu", l, 1], W["dn", l, 1], 6, 4, wc, name, scatter_on(name))
        scattered(bwd_carry.get(name, ()), lands)
        wgrad(("gu", l, 1), sv["hb3"], dz, rows_b)
        wgrad(("dn", l, 1), ab, dyb, rows_b)
        dop, dmix, dm5, dg3 = _tmpost_bwd(cfg, dxs2, sv["opre"], W["wo", l], mods[l], gvec, wc, f"tmpost_bwd_{l}")
        wgrad(("wo", l), sv["mix"], dop, rows_b)
        du, dwp, dps = _pool_bwd(cfg, dmix, sv["u"], band, inv, w_pool[l], sv["ps"], wc, f"pool_bwd_{l}")
        name = f"na_bwd_{l}"
        (dq, dk, dv, dkc, dvc, dbexp), lands = _na_bwd(cfg, dmix, sv["q"], sv["k"], sv["v"], sv["bexp"], name, scatter_on(name))
        scattered(bwd_carry.get(name, ()), lands)
        drpb = _rpb_reduce(dbexp, flip, sel, f"rpb_reduce_{l}")
        if wc:
            dqc, dkc2, dvc2 = _ctx_attn_bwd(cfg, dmix, sv["q"], sv["k"], sv["v"], f"ctx_attn_bwd_{l}")
            ctx_terms = ([dqc], [dkc, dkc2], [dvc, dvc2])
        else:
            ctx_terms = ([], [dkc], [dvc])
        dproj = _rope_bwd_assemble(cfg, (dq, dk, dv, du), ctx_terms, wc, cos, sin, f"rope_bwd_{l}")
        wgrad(("wi", l), sv["hb2"], dproj, cfg.T)
        dxs1, dm34, dg2 = _tmpre_bwd(cfg, dproj, W["wi", l], sv["xs1"], mods[l], gvec, dxs2, wc, f"tmpre_bwd_{l}")
        name = f"ffn_bwd_{l}_0"
        (dxs, dz, dyb, ab, dm012, dg01), lands = _ffn_bwd(cfg, dxs1, sv["xs"], sv["z1"], sv["y1"], mods[l], gvec,
                                                          W["gu", l, 0], W["dn", l, 0], 0, 0, True, name, scatter_on(name),
                                                          xs_ctx=sv["xs_ctx"])
        scattered(bwd_carry.get(name, ()), lands)
        wgrad(("gu", l, 0), sv["hb1"], dz, cfg.T)
        wgrad(("dn", l, 0), ab, dyb, cfg.T)
        if not wc:
            zero = lambda a: jnp.concatenate([a, jnp.zeros_like(a)], axis=0)
            dm5, dm678 = zero(dm5), zero(dm678)
        dmods = jnp.concatenate([dm012, dm34, dm5, dm678], axis=1)
        dgs = jnp.concatenate([dg01, dg2, dg3, dg45], axis=0)
        small[l] = dict(dmods=dmods, dg=dgs, drpb=drpb, dwp=dwp, dps=dps)
    lands = _comm_only(_scatter_comm([g_b16[k_] for k_ in last_scatter], [W_AXIS[k_[0]] for k_ in last_scatter]), "scatter_last")
    scattered(last_scatter, lands)
    kinds = ("gu", "dn", "wi", "wo")
    merged = _half_merge([G[k_] for k_ in kinds], "merge_halves")
    return loss_blk, dxs, dict(zip(kinds, merged)), small


def kernel(x, c, ctx, c_ctx, w_mod, b_mod, norm_g, w_ffn_gate_up, w_ffn_down, w_in, w_out, na_rpb, w_pool, pool_scale, loss_target, m_c_ctx, m_w_mod, m_b_mod, m_norm_g, m_w_ffn_gate_up, m_w_ffn_down, m_w_in, m_w_out, m_na_rpb, m_w_pool, m_pool_scale, v_c_ctx, v_w_mod, v_b_mod, v_norm_g, v_w_ffn_gate_up, v_w_ffn_down, v_w_in, v_w_out, v_na_rpb, v_w_pool, v_pool_scale):
    S, D = x.shape[1], x.shape[2]
    L = ctx.shape[1]
    depth = w_mod.shape[0]
    F = w_ffn_down.shape[2] * N_CHIPS
    nmod = w_mod.shape[2]
    gsh = norm_g.shape[2]
    cfg = _Cfg(S, L, D, F)
    mx, my, mc = _mesh_pos()
    chip = 2 * mx + my
    dev = 4 * mx + 2 * my + mc

    kidx = chip.astype(jnp.int32).reshape(1)
    dck = jnp.stack([dev, mc, chip]).astype(jnp.int32)
    W = {}
    for l in range(depth):
        for i in range(2):
            W["gu", l, i] = _cast_into_place(w_ffn_gate_up, (l, i), W_AXIS["gu"], kidx, f"cast_gu_{l}_{i}")
            W["dn", l, i] = _cast_into_place(w_ffn_down, (l, i), W_AXIS["dn"], kidx, f"cast_dn_{l}_{i}")
        W["wi", l] = _cast_into_place(w_in, (l,), W_AXIS["wi"], kidx, f"cast_wi_{l}")
        W["wo", l] = _cast_into_place(w_out, (l,), W_AXIS["wo"], kidx, f"cast_wo_{l}")
    first = [("gu", 0, 0), ("dn", 0, 0)]
    W.update(zip(first, _comm_only(_gather_comm([W[k_] for k_ in first], [W_AXIS[k_[0]] for k_ in first]), "gather_first")))
    G = {"gu": lax.empty(w_ffn_gate_up.shape, f32), "dn": lax.empty(w_ffn_down.shape, f32),
         "wi": lax.empty(w_in.shape, f32), "wo": lax.empty(w_out.shape, f32)}

    c_all = _all_gather_small(jnp.pad(c, ((0, 7), (0, 0))), "gather_c").reshape(N_DEV, 8, D)[:, 0]
    cvecs = jnp.concatenate([c_all, c_ctx[None], jnp.zeros((7, D), f32)], axis=0)
    b_shard = lax.dynamic_slice_in_dim(b_mod, chip * nmod, nmod, axis=1).reshape(depth, 1, nmod)
    m_part, silu_c = _modvec_fwd(cvecs, w_mod, b_shard, "modvec_fwd")
    m_all = _all_gather_small(m_part.reshape(depth * 16, nmod), "gather_mod").reshape(N_DEV, depth, 16, nmod)
    m_full = jnp.concatenate([m_all[2 * j] for j in range(N_CHIPS)], axis=-1)
    m_mine = lax.dynamic_index_in_dim(m_full, dev, axis=1, keepdims=False)
    mods = jnp.stack([m_mine, m_full[:, 8]], axis=1).reshape(depth, 2, N_MOD, D)

    norm_g_full = _all_gather_small(_pack_rows([norm_g]), "gather_norm_g")
    rows_g = norm_g_full.shape[0] // N_DEV
    ng = norm_g_full.reshape(N_DEV, rows_g * LANES)[:, :norm_g.size].reshape(N_DEV, depth, 6, gsh)
    norm_g_all = jnp.concatenate([ng[2 * j] for j in range(N_CHIPS)], axis=-1)
    loss_blk, dx_lat, wgrads, small = _local_step(cfg, x[0], ctx[0], loss_target[0], mods, norm_g_all, W, G, dck,
                                                  na_rpb, w_pool, pool_scale)
    loss = lax.psum(loss_blk[0, 0], ("x", "y", "c"))
    grad_x = dx_lat[None]

    g_gu, g_dn, g_wi, g_wo = wgrads["gu"], wgrads["dn"], wgrads["wi"], wgrads["wo"]
    names = ("dmods", "dg", "drpb", "dwp", "dps")
    parts = [jnp.stack([small[l][n] for l in range(depth)]) for n in names]
    shapes = [p.shape for p in parts]
    packed = _pack_rows(parts)
    gathered = _all_gather_small(packed, "gather_small").reshape(N_DEV, packed.shape[0], LANES)
    total = _unpack_rows(_sum_devices(gathered, "sum_small"), shapes)
    dmods_sum, dg_sum, drpb_sum, dwp_sum, dps_sum = total
    dmods_each = jnp.stack([_unpack_rows(gathered[j], shapes[:1])[0] for j in range(N_DEV)])
    dm_rows = jnp.concatenate([jnp.transpose(dmods_each[:, :, 0], (1, 0, 2, 3)).reshape(depth, N_DEV, N_MOD * D),
                               dmods_sum[:, 1].reshape(depth, 1, N_MOD * D),
                               jnp.zeros((depth, 7, N_MOD * D), f32)], axis=1)
    dm_shard = lax.dynamic_slice_in_dim(dm_rows, chip * nmod, nmod, axis=2)
    grad_w_mod, gc_part = _modvec_bwd(silu_c.T, dm_shard, w_mod, "modvec_bwd")
    gc_all = _all_gather_small(gc_part.reshape(depth * 8, D), "gather_gc").reshape(N_DEV, depth, 8, D)
    grad_b_mod, grad_c_ctx = _small_finish(dm_rows, gc_all, c_ctx)
    grad_norm_g = lax.dynamic_slice_in_dim(dg_sum, chip * gsh, gsh, axis=2)
    grad_na_rpb = drpb_sum[:, :, :2 * NA_KH - 1, :2 * NA_KW - 1]
    grad_w_pool = dwp_sum
    grad_pool_scale = dps_sum.reshape(depth, POOL_WIDTH)

    grads = [grad_c_ctx, grad_w_mod, grad_b_mod, grad_norm_g, g_gu, g_dn, g_wi, g_wo, grad_na_rpb, grad_w_pool, grad_pool_scale]
    ws = [c_ctx, w_mod, b_mod, norm_g, w_ffn_gate_up, w_ffn_down, w_in, w_out, na_rpb, w_pool, pool_scale]
    ms = [m_c_ctx, m_w_mod, m_b_mod, m_norm_g, m_w_ffn_gate_up, m_w_ffn_down, m_w_in, m_w_out, m_na_rpb, m_w_pool, m_pool_scale]
    vs = [v_c_ctx, v_w_mod, v_b_mod, v_norm_g, v_w_ffn_gate_up, v_w_ffn_down, v_w_in, v_w_out, v_na_rpb, v_w_pool, v_pool_scale]
    tags = ["c_ctx", "w_mod", "b_mod", "norm_g", "gate_up", "down", "w_in", "w_out", "na_rpb", "w_pool", "pool_scale"]
    upd = [_adamw(w_, g_, m_, v_, f"adamw_{t}") for w_, g_, m_, v_, t in zip(ws, grads, ms, vs, tags)]
    return (loss, grad_x, *grads, *[u_[0] for u_ in upd], *[u_[1] for u_ in upd], *[u_[2] for u_ in upd])


def _small_finish(dm_rows, gc_all, c_ctx):
    depth, _, n = dm_rows.shape
    D = c_ctx.shape[0]

    def body(dm_ref, gc_ref, c_ref, gb_ref, gcx_ref):
        acc = dm_ref[:, 0]
        for j in range(1, N_DEV + 1):
            acc = acc + dm_ref[:, j]
        gb_ref[...] = acc
        t = jnp.zeros((1, D), f32)
        for l in range(depth):
            for j in range(N_CHIPS):
                t = t + gc_ref[2 * j, l, 0:1, :]
        cv = c_ref[...]
        sg = _sigmoid(cv)
        gcx_ref[...] = t * (sg * (1.0 + cv * (1.0 - sg)))

    gb, gcx = pl.pallas_call(
        body, name="small_finish",
        out_shape=[S_((depth, n), f32), S_((1, D), f32)],
        compiler_params=_cp(VMEM_MID),
    )(dm_rows, gc_all, c_ctx.reshape(1, D))
    return gb, gcx.reshape(D)
```

```python
import functools

import numpy as np
import jax
import jax.numpy as jnp
from jax import lax
from jax.experimental import pallas as pl
from jax.experimental.pallas import tpu as pltpu

f32, bf16 = jnp.float32, jnp.bfloat16

GRID_W = 64
N_MOD = 9
NA_HEADS = 8
HEAD_DIM = 64
NA_WIDTH = NA_HEADS * HEAD_DIM
NA_KH = 8
NA_KW = 16
POOL_GROUPS = 4
POOL_CH = 128
POOL_WIDTH = POOL_GROUPS * POOL_CH
POOL_WINDOWS = (2, 4, 8, 16)
IN_WIDTH = 3 * NA_WIDTH + POOL_WIDTH
MIX_WIDTH = NA_WIDTH + POOL_WIDTH
ROPE_THETA = 10000.0
ROPE_PAIRS = HEAD_DIM // 4
RMS_EPS = 1e-6
NEG_INF = -1e30
ADAM_LR, ADAM_B1, ADAM_B2, ADAM_EPS, ADAM_WD, ADAM_STEP = 0.001, 0.9, 0.999, 1e-08, 0.01, 10

N_DEV = 8
N_CHIPS = 4
LANES = 128
MIB = 1024 * 1024
VMEM_BIG = 52 * MIB
VMEM_MID = 40 * MIB
MESH = pl.DeviceIdType.MESH
ANY = pl.BlockSpec(memory_space=pl.ANY)
S_ = jax.ShapeDtypeStruct


def _cp(vmem=VMEM_MID, sem=None):
    return pltpu.CompilerParams(vmem_limit_bytes=vmem, dimension_semantics=sem)


def _sigmoid(x):
    return 0.5 * jnp.tanh(0.5 * x) + 0.5


def _rms_hat(x):
    rinv = lax.rsqrt(jnp.mean(x * x, axis=-1, keepdims=True) + RMS_EPS)
    return x * rinv, rinv


def _rms_bwd(dxhat, xhat, rinv):
    return rinv * (dxhat - xhat * jnp.mean(dxhat * xhat, axis=-1, keepdims=True))


def _rsum(a):
    return jnp.sum(a, axis=0, keepdims=True)


def _nt(a, b):
    return lax.dot_general(a, b, (((1,), (1,)), ((), ())), preferred_element_type=f32)


def _tn(a, b):
    return lax.dot_general(a, b, (((0,), (0,)), ((), ())), preferred_element_type=f32)


def _nn(a, b):
    return jnp.dot(a, b, preferred_element_type=f32)


def _swap16(x):
    lane = lax.broadcasted_iota(jnp.int32, x.shape, 1)
    n = x.shape[1]
    return jnp.where((lane % 32) < 16, pltpu.roll(x, n - 16, 1), pltpu.roll(x, 16, 1))


def _rope_tables(s_len, l_len):
    t = np.arange(s_len)
    inv = ROPE_THETA ** (-np.arange(ROPE_PAIRS, dtype=np.float32) / ROPE_PAIRS)
    ang_r = (t // GRID_W).astype(np.float32)[:, None] * inv
    ang_c = (t % GRID_W).astype(np.float32)[:, None] * inv
    cos = np.concatenate([np.cos(ang_r), np.cos(ang_r), np.cos(ang_c), np.cos(ang_c)], axis=-1)
    sin = np.concatenate([-np.sin(ang_r), np.sin(ang_r), -np.sin(ang_c), np.sin(ang_c)], axis=-1)
    cos = np.concatenate([cos, np.ones((l_len, HEAD_DIM), np.float32)], axis=0)
    sin = np.concatenate([sin, np.zeros((l_len, HEAD_DIM), np.float32)], axis=0)
    return (jnp.asarray(np.tile(cos, (1, 2)), f32), jnp.asarray(np.tile(sin, (1, 2)), f32))


def _pool_tables(tm, l_len):
    band = np.zeros((2, POOL_GROUPS, tm, tm), np.float32)
    inv = np.zeros((2, POOL_GROUPS, tm, 1), np.float32)
    for typ, length in ((0, GRID_W), (1, l_len)):
        for g, w in enumerate(POOL_WINDOWS):
            for t in range(tm):
                base, p = (t // length) * length, t % length
                lo = min(max(p - w // 2, 0), length)
                hi = min(max(p - w // 2 + w, 0), length)
                band[typ, g, t, base + lo:base + hi] = 1.0
                inv[typ, g, t, 0] = 1.0 / (hi - lo)
    return jnp.asarray(band, bf16), jnp.asarray(inv, f32)


NA_QR = 4
NA_WR = NA_KH + NA_QR - 1
NA_TYPES = 3
NA_SEL_ROWS = 136
NA_WPAD = 768


def _rpb_index_tables():
    j = np.arange(GRID_W)
    col_start = np.clip(j - NA_KW // 2, 0, GRID_W - NA_KW)
    valid = (j[None, :] >= col_start[:, None]) & (j[None, :] < col_start[:, None] + NA_KW)
    dc = np.clip(j[None, :] - j[:, None] + NA_KW - 1, 0, 2 * NA_KW - 2)
    i = np.arange(NA_QR)[:, None]
    kk = np.arange(NA_WR)[None, :]
    off = np.stack([np.zeros_like(i), i, np.full_like(i, NA_QR - 1)])
    d = np.stack([kk - i + NA_KH - 1, kk - i + NA_KH - 1 - NA_QR, kk - i])
    row_ok = (kk[None] >= off) & (kk[None] < off + NA_KH)
    assert (d[row_ok] >= 0).all() and (d[row_ok] <= 2 * NA_KH - 2).all()
    return valid, dc, d, row_ok


def _expand_rpb(rpb, name):
    _, _, d, row_ok = _rpb_index_tables()
    heads, nd, ne = rpb.shape
    w = GRID_W
    v = jnp.pad(rpb, ((0, 0), (0, 0), (w - NA_KW, 2 * w - (w - NA_KW) - ne)))
    x = jnp.broadcast_to(v[:, :, None, :], (heads, nd, w, 2 * w)).reshape(heads, nd, 2 * w * w)
    t = x[:, :, :w * (2 * w - 1)].reshape(heads, nd, w, 2 * w - 1)[..., w - 1:]

    def body(t_ref, o_ref):
        q = lax.broadcasted_iota(jnp.int32, (w, w), 0)
        c = lax.broadcasted_iota(jnp.int32, (w, w), 1)
        c0 = jnp.clip(q - NA_KW // 2, 0, w - NA_KW)
        in_cols = (c >= c0) & (c < c0 + NA_KW)
        outside = jnp.full((w, w), NEG_INF, f32)
        blocks = [jnp.where(in_cols, t_ref[dd], NEG_INF) for dd in range(nd)]
        for typ in range(NA_TYPES):
            for i in range(NA_QR):
                row = [blocks[d[typ, i, kk]] if row_ok[typ, i, kk] else outside for kk in range(NA_WR)]
                o_ref[typ, i * w:(i + 1) * w, :] = jnp.concatenate(row, axis=1)

    return pl.pallas_call(
        body, name=name, grid=(heads,),
        in_specs=[pl.BlockSpec((None, nd, w, w), lambda h: (h, 0, 0, 0))],
        out_specs=pl.BlockSpec((NA_TYPES, None, NA_QR * w, NA_WR * w), lambda h: (0, h, 0, 0)),
        out_shape=S_((NA_TYPES, heads, NA_QR * w, NA_WR * w), f32),
        compiler_params=_cp(VMEM_MID, ("arbitrary",)),
    )(t)


def _rpb_reduce_tables():
    _, _, d, row_ok = _rpb_index_tables()
    flip = np.eye(GRID_W, dtype=np.float32)[::-1].copy()
    sel = np.zeros((16, NA_SEL_ROWS), np.float32)
    flat_d, flat_ok = d.reshape(-1), row_ok.reshape(-1)
    for n in range(flat_d.size):
        if flat_ok[n]:
            sel[flat_d[n], n] = 1.0
    return jnp.asarray(flip), jnp.asarray(sel)


class _Cfg:
    def __init__(self, s_len, l_len, d, f):
        self.S, self.L, self.D, self.F = s_len, l_len, d, f
        self.T = s_len + l_len
        self.TM = 256 if l_len % 256 == 0 else 128
        assert l_len == self.TM, "context length must equal the row tile"
        assert s_len % self.TM == 0 and s_len % GRID_W == 0
        self.nxt = s_len // self.TM
        self.ntt = self.T // self.TM
        self.rows = s_len // GRID_W
        assert self.rows >= 2 * NA_KH
        assert f % (2 * LANES) == 0
        self.FC = f

    def ntiles(self, with_ctx):
        return self.ntt if with_ctx else self.nxt


def _typ(cfg):
    return lambda i: (jnp.minimum(i // cfg.nxt, 1), 0, 0)


def _mesh_pos():
    return lax.axis_index("x"), lax.axis_index("y"), lax.axis_index("c")


class _Comm:
    def __init__(self, ins, outs, alias, nsem, start, finish):
        self.ins, self.outs, self.alias, self.nsem, self.start, self.finish = ins, outs, alias, nsem, start, finish


def _call(body, args, comm=None, *, grid, in_specs, out_specs, out_shape, scratch_shapes=(), **kw):
    if comm is None:
        return pl.pallas_call(body, grid=grid, in_specs=list(in_specs), out_specs=list(out_specs), out_shape=list(out_shape),
                              scratch_shapes=list(scratch_shapes), **kw)(*args), ()
    n_in, n_out, n_sc = len(in_specs), len(out_specs), len(scratch_shapes)
    ci, co = len(comm.ins), len(comm.outs)

    def carrier(*refs):
        bounds = np.cumsum([0, n_in, ci, n_out, co, n_sc])
        ins, cins, outs, couts, scr = (refs[a:b] for a, b in zip(bounds[:-1], bounds[1:]))
        send, recv = refs[bounds[-1]], refs[bounds[-1] + 1]
        first = functools.reduce(jnp.logical_and, [pl.program_id(a) == 0 for a in range(len(grid))])
        last = functools.reduce(jnp.logical_and, [pl.program_id(a) == g - 1 for a, g in enumerate(grid)])

        @pl.when(first)
        def _():
            comm.start(cins, couts, send, recv)

        body(*ins, *outs, *scr)

        @pl.when(last)
        def _():
            comm.finish(cins, couts, send, recv)

    res = pl.pallas_call(
        carrier, grid=grid, in_specs=list(in_specs) + [ANY] * ci, out_specs=list(out_specs) + [ANY] * co,
        out_shape=list(out_shape) + list(comm.outs),
        input_output_aliases={n_in + a: n_out + b for a, b in comm.alias.items()},
        scratch_shapes=list(scratch_shapes) + [pltpu.SemaphoreType.DMA((comm.nsem,)), pltpu.SemaphoreType.DMA((comm.nsem,))],
        **kw)(*args, *comm.ins)
    return res[:n_out], res[n_out:]


def _comm_only(comm, name):
    ci, co = len(comm.ins), len(comm.outs)

    def body(*refs):
        cins, couts = refs[:ci], refs[ci:ci + co]
        send, recv = refs[ci + co], refs[ci + co + 1]
        comm.start(cins, couts, send, recv)
        comm.finish(cins, couts, send, recv)

    return pl.pallas_call(
        body, name=name, in_specs=[ANY] * ci, out_specs=[ANY] * co, out_shape=list(comm.outs),
        input_output_aliases=dict(comm.alias),
        scratch_shapes=[pltpu.SemaphoreType.DMA((comm.nsem,)), pltpu.SemaphoreType.DMA((comm.nsem,))],
        compiler_params=_cp(VMEM_MID),
    )(*comm.ins)


def _half_view(ref, axis, kk, h):
    r, c = ref.shape
    if axis == 1:
        n = c // N_CHIPS
        return ref.at[pl.ds(h * (r // 2), r // 2), pl.ds(pl.multiple_of(kk * n, LANES), n)]
    n = r // N_CHIPS
    return ref.at[pl.ds(pl.multiple_of(kk * n + h * (n // 2), 8), n // 2), :]


def _other_chips(x, y):
    return [(1 - x, y), (x, 1 - y), (1 - x, 1 - y)]


def _gather_comm(arrs, axes):
    n = len(arrs)

    def copy(ref, view, sems, k, to):
        send, recv = sems
        return pltpu.make_async_remote_copy(src_ref=view, dst_ref=view, send_sem=send.at[k], recv_sem=recv.at[k],
                                            device_id=to, device_id_type=MESH)

    def start(cins, bufs, send, recv):
        x, y, c = _mesh_pos()
        for t in range(n):
            own = _half_view(bufs[t], axes[t], 2 * x + y, c)
            for j, chip in enumerate(_other_chips(x, y)):
                copy(bufs[t], own, (send, recv), 6 * t + j, (*chip, c)).start()

    def finish(cins, bufs, send, recv):
        x, y, c = _mesh_pos()
        sibling = (x, y, 1 - c)
        chips = _other_chips(x, y)
        for t in range(n):
            for j, chip in enumerate(chips):
                landed = _half_view(bufs[t], axes[t], 2 * chip[0] + chip[1], c)
                copy(bufs[t], landed, (send, recv), 6 * t + j, (*chip, c)).wait_recv()
                copy(bufs[t], landed, (send, recv), 6 * t + 3 + j, sibling).start()
        for t in range(n):
            own = _half_view(bufs[t], axes[t], 2 * x + y, c)
            for j, chip in enumerate(chips):
                kj = 2 * chip[0] + chip[1]
                copy(bufs[t], _half_view(bufs[t], axes[t], kj, 1 - c), (send, recv), 6 * t + 3 + j, sibling).wait_recv()
                copy(bufs[t], own, (send, recv), 6 * t + j, (*chip, c)).wait_send()
                copy(bufs[t], _half_view(bufs[t], axes[t], kj, c), (send, recv), 6 * t + 3 + j, sibling).wait_send()

    return _Comm(list(arrs), [S_(a.shape, a.dtype) for a in arrs], {t: t for t in range(n)}, 6 * n, start, finish)


def _scatter_comm(parts, axes):
    n = len(parts)
    peers = [(fx, fy, fc) for fx in (0, 1) for fy in (0, 1) for fc in (0, 1)][1:]

    def half_shape(a, axis):
        r, c = a.shape
        return (r // 2, c // N_CHIPS) if axis == 1 else (r // N_CHIPS // 2, c)

    def start(srcs, lands, send, recv):
        x, y, c = _mesh_pos()
        me = 4 * x + 2 * y + c
        for t in range(n):
            for r_, (fx, fy, fc) in enumerate(peers):
                dx, dy, dc = (1 - x if fx else x), (1 - y if fy else y), (1 - c if fc else c)
                pltpu.make_async_remote_copy(
                    src_ref=_half_view(srcs[t], axes[t], 2 * dx + dy, dc), dst_ref=lands[t].at[me],
                    send_sem=send.at[7 * t + r_], recv_sem=recv.at[7 * t + r_],
                    device_id=(dx, dy, dc), device_id_type=MESH).start()

    def finish(srcs, lands, send, recv):
        x, y, c = _mesh_pos()
        for t in range(n):
            mine = _half_view(srcs[t], axes[t], 2 * x + y, c)
            for r_, (fx, fy, fc) in enumerate(peers):
                sx, sy, sc = (1 - x if fx else x), (1 - y if fy else y), (1 - c if fc else c)
                cp = pltpu.make_async_remote_copy(
                    src_ref=mine, dst_ref=lands[t].at[4 * sx + 2 * sy + sc],
                    send_sem=send.at[7 * t + r_], recv_sem=recv.at[7 * t + r_],
                    device_id=(sx, sy, sc), device_id_type=MESH)
                cp.wait_recv()
                cp.wait_send()

    return _Comm(list(parts), [S_((N_DEV,) + half_shape(a, ax), a.dtype) for a, ax in zip(parts, axes)], {}, 7 * n, start, finish)


def _ffn_fwd(cfg, xs, mods, gvec, wgu, wd, mi, gi, with_ctx, name, comm=None, xs_ctx=None, loss_target=None):
    TM, D, F, FC = cfg.TM, cfg.D, cfg.F, cfg.FC
    nt = cfg.ntiles(with_ctx)
    R = nt * TM
    split, head = xs_ctx is not None, loss_target is not None

    def body(*refs):
        it = iter(refs)
        xs_ref = next(it)
        xc_ref = next(it) if split else None
        mods_ref, g_ref, wgu_hbm, wd_hbm = next(it), next(it), next(it), next(it)
        t_ref = next(it) if head else None
        out_ref, hb_ref, z_ref, y_ref = next(it), next(it), next(it), next(it)
        loss_ref = next(it) if head else None
        wgu_v, wd_v, sem = next(it), next(it), next(it)
        i = pl.program_id(0)

        @pl.when(i == 0)
        def _():
            c0 = pltpu.make_async_copy(wgu_hbm, wgu_v, sem.at[0])
            c1 = pltpu.make_async_copy(wd_hbm, wd_v, sem.at[1])
            c0.start(); c1.start(); c0.wait(); c1.wait()
            if head:
                loss_ref[...] = jnp.zeros_like(loss_ref)
        x = xs_ref[...]
        if split:
            x = jnp.where(i < cfg.nxt, x, xc_ref[...])
        m = mods_ref[0]
        sh, sc, gt = m[mi:mi + 1], m[mi + 1:mi + 2], m[mi + 2:mi + 3]
        xhat, _ = _rms_hat(x)
        h = (xhat * g_ref[gi:gi + 1]) * (1.0 + sc) + sh
        hb = h.astype(bf16)
        hb_ref[...] = hb
        y = jnp.zeros((TM, D), f32)
        for ch in range(F // FC):
            zg = _nn(hb, wgu_v[:, ch * FC:(ch + 1) * FC])
            zu = _nn(hb, wgu_v[:, F + ch * FC:F + (ch + 1) * FC])
            z_ref[:, ch * FC:(ch + 1) * FC] = zg.astype(bf16)
            z_ref[:, F + ch * FC:F + (ch + 1) * FC] = zu.astype(bf16)
            a = (zg * _sigmoid(zg)) * zu
            y = y + _nn(a.astype(bf16), wd_v[ch * FC:(ch + 1) * FC, :])
        y_ref[...] = y
        yhat, _ = _rms_hat(y)
        out = x + 0.5 * gt * (yhat * g_ref[gi + 1:gi + 2])
        if head:
            e = out - t_ref[...]
            out_ref[...] = e * (1.0 / D)
            loss_ref[...] += jnp.sum(jnp.mean(e * e, axis=-1, keepdims=True), axis=0, keepdims=True) * 0.5
        else:
            out_ref[...] = out

    rt = lambda c: pl.BlockSpec((TM, c), lambda i: (i, 0))
    lat = pl.BlockSpec((TM, D), lambda i: (jnp.minimum(i, cfg.nxt - 1), 0))
    x_specs, x_args = ([lat, pl.BlockSpec((TM, D), lambda i: (0, 0))], [xs, xs_ctx]) if split else ([rt(D)], [xs])
    t_specs, t_args = ([rt(D)], [loss_target]) if head else ([], [])
    l_specs, l_shape = ([pl.BlockSpec((8, LANES), lambda i: (0, 0))], [S_((8, LANES), f32)]) if head else ([], [])
    return _call(
        body, (*x_args, mods, gvec, wgu, wd, *t_args), comm, name=name, grid=(nt,),
        in_specs=x_specs + [pl.BlockSpec((1, N_MOD, D), _typ(cfg)), pl.BlockSpec((6, D), lambda i: (0, 0)), ANY, ANY] + t_specs,
        out_specs=[rt(D), rt(D), rt(2 * F), rt(D)] + l_specs,
        out_shape=[S_((R, D), f32), S_((R, D), bf16), S_((R, 2 * F), bf16), S_((R, D), f32)] + l_shape,
        scratch_shapes=[pltpu.VMEM((D, 2 * F), bf16), pltpu.VMEM((F, D), bf16), pltpu.SemaphoreType.DMA((2,))],
        compiler_params=_cp(VMEM_BIG, ("arbitrary",)),
    )


def _ffn_bwd(cfg, dout, xs, z, y, mods, gvec, wgu, wd, mi, gi, with_ctx, name, comm=None, xs_ctx=None):
    TM, D, F, FC = cfg.TM, cfg.D, cfg.F, cfg.FC
    nt = cfg.ntiles(with_ctx)
    R = nt * TM
    ntyp = 2 if with_ctx else 1
    split = xs_ctx is not None

    def body(*refs):
        it = iter(refs)
        do_ref, xs_ref = next(it), next(it)
        xc_ref = next(it) if split else None
        z_ref, y_ref, mods_ref, g_ref, wgu_hbm, wd_hbm = (next(it) for _ in range(6))
        dx_ref, dz_ref, dy_ref, a_ref, dm_ref, dg_ref, wgu_v, wd_v, sem = (next(it) for _ in range(9))
        i = pl.program_id(0)

        @pl.when(i == 0)
        def _():
            c0 = pltpu.make_async_copy(wgu_hbm, wgu_v, sem.at[0])
            c1 = pltpu.make_async_copy(wd_hbm, wd_v, sem.at[1])
            c0.start(); c1.start(); c0.wait(); c1.wait()
            dg_ref[...] = jnp.zeros_like(dg_ref)

        @pl.when((i == 0) | (i == cfg.nxt))
        def _():
            dm_ref[...] = jnp.zeros_like(dm_ref)

        do = do_ref[...]
        x = xs_ref[...]
        if split:
            x = jnp.where(i < cfg.nxt, x, xc_ref[...])
        m = mods_ref[0]
        sc, gt = m[mi + 1:mi + 2], m[mi + 2:mi + 3]
        g_pre, g_post = g_ref[gi:gi + 1], g_ref[gi + 1:gi + 2]
        xhat, rinv0 = _rms_hat(x)
        n0 = xhat * g_pre
        yhat, rinv1 = _rms_hat(y_ref[...])
        d_gt = _rsum(0.5 * do * (yhat * g_post))
        dr = (0.5 * gt) * do
        dg_post = _rsum(dr * yhat)
        dy = _rms_bwd(dr * g_post, yhat, rinv1)
        dyb = dy.astype(bf16)
        dy_ref[...] = dyb
        dh = jnp.zeros((TM, D), f32)
        for ch in range(F // FC):
            zg = z_ref[:, ch * FC:(ch + 1) * FC].astype(f32)
            zu = z_ref[:, F + ch * FC:F + (ch + 1) * FC].astype(f32)
            sg = _sigmoid(zg)
            silu = zg * sg
            a_ref[:, ch * FC:(ch + 1) * FC] = (silu * zu).astype(bf16)
            da = _nt(dyb, wd_v[ch * FC:(ch + 1) * FC, :])
            dzu = (da * silu).astype(bf16)
            dzg = (da * zu * (sg * (1.0 + zg * (1.0 - sg)))).astype(bf16)
            dz_ref[:, ch * FC:(ch + 1) * FC] = dzg
            dz_ref[:, F + ch * FC:F + (ch + 1) * FC] = dzu
            dh = dh + _nt(dzg, wgu_v[:, ch * FC:(ch + 1) * FC]) + _nt(dzu, wgu_v[:, F + ch * FC:F + (ch + 1) * FC])
        d_sh = _rsum(dh)
        d_sc = _rsum(dh * n0)
        dn = dh * (1.0 + sc)
        dg_pre = _rsum(dn * xhat)
        dx = do + _rms_bwd(dn * g_pre, xhat, rinv0)
        if split:
            @pl.when(i < cfg.nxt)
            def _():
                dx_ref[...] = dx
        else:
            dx_ref[...] = dx
        dm_ref[0] += jnp.concatenate([d_sh, d_sc, d_gt], axis=0)
        dg_ref[...] += jnp.concatenate([dg_pre, dg_post], axis=0)

    rt = lambda c: pl.BlockSpec((TM, c), lambda i: (i, 0))
    lat = pl.BlockSpec((TM, D), lambda i: (jnp.minimum(i, cfg.nxt - 1), 0))
    x_specs, x_args = ([lat, pl.BlockSpec((TM, D), lambda i: (0, 0))], [xs, xs_ctx]) if split else ([rt(D)], [xs])
    return _call(
        body, (dout, *x_args, z, y, mods, gvec, wgu, wd), comm, name=name, grid=(nt,),
        in_specs=[rt(D)] + x_specs + [rt(2 * F), rt(D), pl.BlockSpec((1, N_MOD, D), _typ(cfg)),
                                       pl.BlockSpec((6, D), lambda i: (0, 0)), ANY, ANY],
        out_specs=[lat if split else rt(D), rt(2 * F), rt(D), rt(F), pl.BlockSpec((1, 3, D), _typ(cfg)),
                   pl.BlockSpec((2, D), lambda i: (0, 0))],
        out_shape=[S_((cfg.S if split else R, D), f32), S_((R, 2 * F), bf16), S_((R, D), bf16), S_((R, F), bf16),
                   S_((ntyp, 3, D), f32), S_((2, D), f32)],
        scratch_shapes=[pltpu.VMEM((D, 2 * F), bf16), pltpu.VMEM((F, D), bf16), pltpu.SemaphoreType.DMA((2,))],
        compiler_params=_cp(VMEM_BIG, ("arbitrary",)),
    )


def _wgrad(a, b, k_rows, name, comm=None):
    M, N = a.shape[1], b.shape[1]
    tn = N
    for cand in (1408, 1024, 512):
        if N % cand == 0 and N > cand:
            tn = cand
            break
    room = VMEM_BIG - 6 * MIB - 2 * M * tn * 6
    tk = _div_tile(k_rows, 1, min(2816, room // (4 * (M + tn))), LANES)
    nk = k_rows // tk

    def body(a_ref, b_ref, o_ref, ob_ref):
        k = pl.program_id(1)

        @pl.when(k == 0)
        def _():
            o_ref[...] = jnp.zeros_like(o_ref)
        o_ref[...] += _tn(a_ref[...], b_ref[...])

        @pl.when(k == nk - 1)
        def _():
            ob_ref[...] = o_ref[...].astype(bf16)

    ospec = pl.BlockSpec((M, tn), lambda n, k: (0, n))
    return _call(
        body, (a, b), comm, name=name, grid=(N // tn, nk),
        in_specs=[pl.BlockSpec((tk, M), lambda n, k: (k, 0)), pl.BlockSpec((tk, tn), lambda n, k: (k, n))],
        out_specs=[ospec, ospec], out_shape=[S_((M, N), f32), S_((M, N), bf16)],
        compiler_params=_cp(VMEM_BIG, ("arbitrary", "arbitrary")),
    )


def _tmpre_fwd(cfg, xs, mods, gvec, w_in, cos, sin, name):
    TM, D = cfg.TM, cfg.D
    nt, R = cfg.ntt, cfg.T
    W = NA_WIDTH

    def body(xs_ref, mods_ref, g_ref, w_ref, cos_ref, sin_ref, hb_ref, q_ref, k_ref, v_ref, u_ref):
        x = xs_ref[...]
        m = mods_ref[0]
        xhat, _ = _rms_hat(x)
        hb = ((xhat * g_ref[2:3]) * (1.0 + m[4:5]) + m[3:4]).astype(bf16)
        hb_ref[...] = hb
        p = _nn(hb, w_ref[...])
        cs = jnp.tile(cos_ref[...], (1, W // LANES))
        sn = jnp.tile(sin_ref[...], (1, W // LANES))
        q = p[:, 0:W]
        k = p[:, W:2 * W]
        q_ref[...] = ((q * cs + _swap16(q) * sn) * (HEAD_DIM ** -0.5)).astype(bf16)
        k_ref[...] = (k * cs + _swap16(k) * sn).astype(bf16)
        v_ref[...] = p[:, 2 * W:3 * W].astype(bf16)
        u_ref[...] = p[:, 3 * W:]

    rt = lambda c: pl.BlockSpec((TM, c), lambda i: (i, 0))
    return pl.pallas_call(
        body, name=name, grid=(nt,),
        in_specs=[rt(D), pl.BlockSpec((1, N_MOD, D), _typ(cfg)), pl.BlockSpec((6, D), lambda i: (0, 0)),
                  pl.BlockSpec((D, IN_WIDTH), lambda i: (0, 0)), rt(LANES), rt(LANES)],
        out_specs=[rt(D), rt(W), rt(W), rt(W), rt(POOL_WIDTH)],
        out_shape=[S_((R, D), bf16), S_((R, W), bf16), S_((R, W), bf16), S_((R, W), bf16), S_((R, POOL_WIDTH), f32)],
        compiler_params=_cp(VMEM_MID, ("arbitrary",)),
    )(xs, mods, gvec, w_in, cos, sin)


def _tmpre_bwd(cfg, lat, ctx_terms, du_has_ctx, cos, sin, w_in, xs, mods, gvec, dres, res_with_ctx, name):
    TM, D = cfg.TM, cfg.D
    nt, R = cfg.ntt, cfg.T
    nres = cfg.ntiles(res_with_ctx)
    W = NA_WIDTH
    n_ctx = [len(t) for t in ctx_terms]
    flat_ctx = [a for t in ctx_terms for a in t]
    n_asm = 4 + len(flat_ctx) + 2

    def assemble(refs, o_ref):
        dq_ref, dk_ref, dv_ref, du_ref = refs[:4]
        ctx_refs = refs[4:4 + len(flat_ctx)]
        cos_ref, sin_ref = refs[4 + len(flat_ctx):]
        is_ctx = pl.program_id(0) >= cfg.nxt
        vals, off = [], 0
        for lat_ref, n in zip((dq_ref, dk_ref, dv_ref), n_ctx):
            cv = jnp.zeros((TM, W), f32)
            for r_ in ctx_refs[off:off + n]:
                cv = cv + r_[...]
            off += n
            vals.append(jnp.where(is_ctx, cv, lat_ref[...]))
        du_ = du_ref[...] if du_has_ctx else jnp.where(is_ctx, 0.0, du_ref[...])
        cs = jnp.tile(cos_ref[...], (1, W // LANES))
        sn = jnp.tile(sin_ref[...], (1, W // LANES))
        dq_ = vals[0] * (HEAD_DIM ** -0.5)
        dk_ = vals[1]
        o_ref[:, 0:W] = (dq_ * cs + _swap16(dq_ * sn)).astype(bf16)
        o_ref[:, W:2 * W] = (dk_ * cs + _swap16(dk_ * sn)).astype(bf16)
        o_ref[:, 2 * W:3 * W] = vals[2].astype(bf16)
        o_ref[:, 3 * W:] = du_.astype(bf16)

    def body(*refs):
        w_ref, xs_ref, mods_ref, g_ref, dres_ref, dx_ref, dp_ref, dm_ref, dg_ref = refs[n_asm:]
        i = pl.program_id(0)

        @pl.when(i == 0)
        def _():
            dg_ref[...] = jnp.zeros_like(dg_ref)

        @pl.when((i == 0) | (i == cfg.nxt))
        def _():
            dm_ref[...] = jnp.zeros_like(dm_ref)

        assemble(refs[:n_asm], dp_ref)
        dh = _nt(dp_ref[...], w_ref[...])
        x = xs_ref[...]
        m = mods_ref[0]
        g2 = g_ref[2:3]
        xhat, rinv = _rms_hat(x)
        d_sh = _rsum(dh)
        d_sc = _rsum(dh * (xhat * g2))
        dn = dh * (1.0 + m[4:5])
        dg_ref[...] += _rsum(dn * xhat)
        dx = _rms_bwd(dn * g2, xhat, rinv)
        res = dres_ref[...]
        if nres < nt:
            res = jnp.where(i < nres, res, 0.0)
        dx_ref[...] = res + dx
        dm_ref[0] += jnp.concatenate([d_sh, d_sc], axis=0)

    rt = lambda c: pl.BlockSpec((TM, c), lambda i: (i, 0))
    lat_spec = pl.BlockSpec((TM, W), lambda i: (jnp.minimum(i, cfg.nxt - 1), 0))
    du_spec = rt(POOL_WIDTH) if du_has_ctx else lat_spec
    asm_specs = ([lat_spec, lat_spec, lat_spec, du_spec] + [pl.BlockSpec((TM, W), lambda i: (0, 0))] * len(flat_ctx)
                 + [rt(LANES), rt(LANES)])
    return pl.pallas_call(
        body, name=name, grid=(nt,),
        in_specs=asm_specs + [pl.BlockSpec((D, IN_WIDTH), lambda i: (0, 0)), rt(D),
                              pl.BlockSpec((1, N_MOD, D), _typ(cfg)), pl.BlockSpec((6, D), lambda i: (0, 0)),
                              pl.BlockSpec((TM, D), lambda i: (jnp.minimum(i, nres - 1), 0))],
        out_specs=[rt(D), rt(IN_WIDTH), pl.BlockSpec((1, 2, D), _typ(cfg)), pl.BlockSpec((1, D), lambda i: (0, 0))],
        out_shape=[S_((R, D), f32), S_((R, IN_WIDTH), bf16), S_((2, 2, D), f32), S_((1, D), f32)],
        compiler_params=_cp(VMEM_MID, ("arbitrary",)),
    )(*lat, *flat_ctx, cos, sin, w_in, xs, mods, gvec, dres)


def _na_block(cfg, b):
    return jnp.clip(NA_QR * b - NA_KH // 2, 0, cfg.rows - NA_WR)


def _na_load_bias(b, nb, b_hbm, b_v, sem):
    for typ, at in ((0, 0), (1, 1), (2, nb - 1)):
        @pl.when(b == at)
        def _(typ=typ):
            cp = pltpu.make_async_copy(b_hbm.at[typ], b_v, sem)
            cp.start()
            cp.wait()


def _na_probs(qh, klh, kch, bias):
    s_loc = _nt(qh, klh) + bias
    s_ctx = _nt(qh, kch)
    mx = jnp.maximum(jnp.max(s_loc, axis=-1, keepdims=True), jnp.max(s_ctx, axis=-1, keepdims=True))
    e_loc = jnp.exp(s_loc - mx)
    e_ctx = jnp.exp(s_ctx - mx)
    inv = 1.0 / (jnp.sum(e_loc, axis=-1, keepdims=True) + jnp.sum(e_ctx, axis=-1, keepdims=True))
    return e_loc * inv, e_ctx * inv


def _na_fwd(cfg, q, k, v, bexp, name, comm=None):
    S, L, T = cfg.S, cfg.L, cfg.T
    NQ, NW = NA_QR * GRID_W, NA_WR * GRID_W
    nb = cfg.rows // NA_QR

    def body(q_ref, k_hbm, v_hbm, b_hbm, o_ref, k_v, v_v, b_v, sem):
        b = pl.program_id(0)

        @pl.when(b == 0)
        def _():
            cs = [pltpu.make_async_copy(k_hbm, k_v, sem.at[0]), pltpu.make_async_copy(v_hbm, v_v, sem.at[1])]
            for c_ in cs:
                c_.start()
            for c_ in cs:
                c_.wait()

        _na_load_bias(b, nb, b_hbm, b_v, sem.at[2])
        st = pl.multiple_of(_na_block(cfg, b) * GRID_W, GRID_W)
        first = lax.broadcasted_iota(jnp.int32, (NQ, LANES), 1) < HEAD_DIM
        for hp in range(NA_HEADS // 2):
            ls = slice(hp * LANES, (hp + 1) * LANES)
            q2 = q_ref[:, ls]
            kl, vl = k_v[pl.ds(st, NW), ls], v_v[pl.ds(st, NW), ls]
            kc, vc = k_v[S:T, ls], v_v[S:T, ls]
            o2 = []
            for hh in range(2):
                qm = jnp.where(first if hh == 0 else ~first, q2, jnp.zeros_like(q2))
                p_loc, p_ctx = _na_probs(qm, kl, kc, b_v[2 * hp + hh])
                o2.append(_nn(p_loc.astype(bf16), vl) + _nn(p_ctx.astype(bf16), vc))
            o_ref[:, ls] = jnp.where(first, o2[0], o2[1]).astype(bf16)

    return _call(
        body, (q, k, v, bexp), comm, name=name, grid=(nb,),
        in_specs=[pl.BlockSpec((NQ, NA_WIDTH), lambda b: (b, 0)), ANY, ANY, ANY],
        out_specs=[pl.BlockSpec((NQ, NA_WIDTH), lambda b: (b, 0))],
        out_shape=[S_((S, NA_WIDTH), bf16)],
        scratch_shapes=[pltpu.VMEM((T, NA_WIDTH), bf16), pltpu.VMEM((T, NA_WIDTH), bf16),
                        pltpu.VMEM((NA_HEADS, NQ, NW), f32), pltpu.SemaphoreType.DMA((3,))],
        compiler_params=_cp(VMEM_MID, ("arbitrary",)),
    )


def _na_bwd(cfg, do, q, k, v, bexp, name, comm=None):
    S, L, T, rows = cfg.S, cfg.L, cfg.T, cfg.rows
    NQ, NW = NA_QR * GRID_W, NA_WR * GRID_W
    NSLOT = 2 * NA_KH
    nb = rows // NA_QR
    bmax = (rows - NA_WR) // NA_QR
    steps = 2 * nb - bmax
    W = NA_WIDTH
    assert nb >= 3 and bmax >= 1 and rows - NA_QR * bmax <= NSLOT

    def out_group(g):
        return jnp.where(g >= nb, g - nb + bmax, jnp.clip(g - 1, 0, bmax - 1))

    def body(do_ref, q_ref, k_hbm, v_hbm, b_hbm, dq_ref, dk_ref, dv_ref, dkc_ref, dvc_ref, db_hbm,
             k_v, v_v, b_v, db_v, ak, av, akc, avc, sem):
        g = pl.program_id(0)

        @pl.when(g == 0)
        def _():
            cs = [pltpu.make_async_copy(k_hbm, k_v, sem.at[0]), pltpu.make_async_copy(v_hbm, v_v, sem.at[1])]
            for c_ in cs:
                c_.start()
            db_v[...] = jnp.zeros_like(db_v)
            ak[...] = jnp.zeros_like(ak)
            av[...] = jnp.zeros_like(av)
            akc[...] = jnp.zeros_like(akc)
            avc[...] = jnp.zeros_like(avc)
            for c_ in cs:
                c_.wait()

        for typ, at in ((0, 1), (1, nb - 1)):
            @pl.when(g == at)
            def _(typ=typ):
                cp = pltpu.make_async_copy(db_v, db_hbm.at[typ], sem.at[2])
                cp.start()
                cp.wait()
                db_v[...] = jnp.zeros_like(db_v)

        @pl.when(g < nb)
        def _():
            _na_load_bias(g, nb, b_hbm, b_v, sem.at[2])
            ws = _na_block(cfg, g)
            st = pl.multiple_of(ws * GRID_W, GRID_W)
            first = lax.broadcasted_iota(jnp.int32, (NQ, LANES), 1) < HEAD_DIM
            for hp in range(NA_HEADS // 2):
                ls = slice(hp * LANES, (hp + 1) * LANES)
                q2, do2 = q_ref[:, ls], do_ref[:, ls]
                kl, vl = k_v[pl.ds(st, NW), ls], v_v[pl.ds(st, NW), ls]
                kc, vc = k_v[S:T, ls], v_v[S:T, ls]
                dq2 = []
                dk2 = jnp.zeros((NW, LANES), f32)
                dv2 = jnp.zeros((NW, LANES), f32)
                dkc2 = jnp.zeros((L, LANES), f32)
                dvc2 = jnp.zeros((L, LANES), f32)
                for hh in range(2):
                    keep = first if hh == 0 else ~first
                    qm = jnp.where(keep, q2, jnp.zeros_like(q2))
                    dom = jnp.where(keep, do2, jnp.zeros_like(do2))
                    p_loc, p_ctx = _na_probs(qm, kl, kc, b_v[2 * hp + hh])
                    dp_loc = _nt(dom, vl)
                    dp_ctx = _nt(dom, vc)
                    delta = jnp.sum(p_loc * dp_loc, axis=-1, keepdims=True) + jnp.sum(p_ctx * dp_ctx, axis=-1, keepdims=True)
                    ds_loc = p_loc * (dp_loc - delta)
                    ds_ctx = p_ctx * (dp_ctx - delta)
                    db_v[2 * hp + hh, :, 0:NW] += ds_loc
                    dsl, dsc = ds_loc.astype(bf16), ds_ctx.astype(bf16)
                    dq2.append(_nn(dsl, kl) + _nn(dsc, kc))
                    dk2 = dk2 + _tn(dsl, qm)
                    dv2 = dv2 + _tn(p_loc.astype(bf16), dom)
                    dkc2 = dkc2 + _tn(dsc, qm)
                    dvc2 = dvc2 + _tn(p_ctx.astype(bf16), dom)
                dq_ref[:, ls] = jnp.where(first, dq2[0], dq2[1])
                akc[:, ls] += dkc2
                avc[:, ls] += dvc2
                for kk in range(NA_WR):
                    slot = (ws + kk) % NSLOT
                    ak[slot, :, ls] += dk2[kk * GRID_W:(kk + 1) * GRID_W, :]
                    av[slot, :, ls] += dv2[kk * GRID_W:(kk + 1) * GRID_W, :]

        @pl.when(((g >= 1) & (g <= bmax)) | (g >= nb))
        def _():
            base = NA_QR * (out_group(g) % (NSLOT // NA_QR))
            for t in range(NA_QR):
                dk_ref[t * GRID_W:(t + 1) * GRID_W, :] = ak[base + t]
                dv_ref[t * GRID_W:(t + 1) * GRID_W, :] = av[base + t]
                ak[base + t] = jnp.zeros((GRID_W, W), f32)
                av[base + t] = jnp.zeros((GRID_W, W), f32)

        @pl.when(g == nb - 1)
        def _():
            cp = pltpu.make_async_copy(db_v, db_hbm.at[2], sem.at[2])
            cp.start()
            cp.wait()

        @pl.when(g == steps - 1)
        def _():
            dkc_ref[...] = akc[...]
            dvc_ref[...] = avc[...]

    qmap = lambda g: (jnp.minimum(g, nb - 1), 0)
    kmap = lambda g: (out_group(g), 0)
    full = lambda g: (0, 0)
    return _call(
        body, (do, q, k, v, bexp), comm, name=name, grid=(steps,),
        in_specs=[pl.BlockSpec((NQ, W), qmap), pl.BlockSpec((NQ, W), qmap), ANY, ANY, ANY],
        out_specs=[pl.BlockSpec((NQ, W), qmap), pl.BlockSpec((NQ, W), kmap), pl.BlockSpec((NQ, W), kmap),
                   pl.BlockSpec((L, W), full), pl.BlockSpec((L, W), full), ANY],
        out_shape=[S_((S, W), f32), S_((S, W), f32), S_((S, W), f32), S_((L, W), f32), S_((L, W), f32),
                   S_((NA_TYPES, NA_HEADS, NQ, NA_WPAD), f32)],
        scratch_shapes=[pltpu.VMEM((T, W), bf16), pltpu.VMEM((T, W), bf16),
                        pltpu.VMEM((NA_HEADS, NQ, NW), f32), pltpu.VMEM((NA_HEADS, NQ, NA_WPAD), f32),
                        pltpu.VMEM((NSLOT, GRID_W, W), f32), pltpu.VMEM((NSLOT, GRID_W, W), f32),
                        pltpu.VMEM((L, W), f32), pltpu.VMEM((L, W), f32), pltpu.SemaphoreType.DMA((3,))],
        compiler_params=_cp(VMEM_BIG, ("arbitrary",)),
    )


def _rpb_reduce(dbias, flip, sel, name):
    nq, w = NA_QR * GRID_W, GRID_W

    def diag_body(x_ref, j_ref, o_ref):
        rows = []
        for i in range(NA_QR):
            xr = jnp.dot(j_ref[...], x_ref[i * w:(i + 1) * w, :], preferred_element_type=f32, precision=lax.Precision.HIGHEST)
            rows.append(jnp.sum(pltpu.roll(xr, 0, 1, stride=1, stride_axis=0), axis=0, keepdims=True))
        o_ref[...] = jnp.concatenate(rows + [jnp.zeros((8 - NA_QR, NA_WPAD), f32)], axis=0)

    diag = pl.pallas_call(
        diag_body, name=name + "_diag", grid=(NA_TYPES, NA_HEADS),
        in_specs=[pl.BlockSpec((None, None, nq, NA_WPAD), lambda t, h: (t, h, 0, 0)), pl.BlockSpec((w, w), lambda t, h: (0, 0))],
        out_specs=pl.BlockSpec((None, None, 8, NA_WPAD), lambda t, h: (t, h, 0, 0)),
        out_shape=S_((NA_TYPES, NA_HEADS, 8, NA_WPAD), f32),
        compiler_params=_cp(VMEM_MID, ("arbitrary", "arbitrary")),
    )(dbias, flip)
    lo = w - NA_KW
    y = diag[:, :, :NA_QR, lo:lo + NA_WR * w].reshape(NA_TYPES, NA_HEADS, NA_QR, NA_WR, w)
    y = jnp.transpose(y, (1, 0, 2, 3, 4)).reshape(NA_HEADS, NA_TYPES * NA_QR * NA_WR, w)
    y = jnp.pad(y, ((0, 0), (0, NA_SEL_ROWS - y.shape[1]), (0, LANES - w)))

    def body(y_ref, sel_ref, o_ref):
        o_ref[...] = jnp.dot(sel_ref[...], y_ref[...], preferred_element_type=f32, precision=lax.Precision.HIGHEST)

    return pl.pallas_call(
        body, name=name, grid=(NA_HEADS,),
        in_specs=[pl.BlockSpec((None, NA_SEL_ROWS, LANES), lambda h: (h, 0, 0)), pl.BlockSpec((16, NA_SEL_ROWS), lambda h: (0, 0))],
        out_specs=pl.BlockSpec((None, 16, LANES), lambda h: (h, 0, 0)),
        out_shape=S_((NA_HEADS, 16, LANES), f32),
        compiler_params=_cp(VMEM_MID, ("arbitrary",)),
    )(y, sel)


def _ctx_attn_fwd(cfg, q, k, v, name):
    L = cfg.L
    blk = cfg.S // L

    def body(q_ref, k_ref, v_ref, o_ref):
        qv, kv, vv = q_ref[...], k_ref[...], v_ref[...]
        outs = []
        for h in range(NA_HEADS):
            hs = slice(h * HEAD_DIM, (h + 1) * HEAD_DIM)
            s = _nt(qv[:, hs], kv[:, hs])
            e = jnp.exp(s - jnp.max(s, axis=-1, keepdims=True))
            p = e * (1.0 / jnp.sum(e, axis=-1, keepdims=True))
            outs.append(_nn(p.astype(bf16), vv[:, hs]))
        o_ref[...] = jnp.concatenate(outs, axis=-1).astype(bf16)

    spec = pl.BlockSpec((L, NA_WIDTH), lambda i: (blk, 0))
    return pl.pallas_call(
        body, name=name, grid=(1,), in_specs=[spec, spec, spec],
        out_specs=pl.BlockSpec((L, NA_WIDTH), lambda i: (0, 0)), out_shape=S_((L, NA_WIDTH), bf16),
        compiler_params=_cp(VMEM_MID, ("arbitrary",)),
    )(q, k, v)


def _ctx_attn_bwd(cfg, do, q, k, v, name):
    L = cfg.L
    blk = cfg.S // L

    def body(do_ref, q_ref, k_ref, v_ref, dq_ref, dk_ref, dv_ref):
        dov, qv, kv, vv = do_ref[...], q_ref[...], k_ref[...], v_ref[...]
        dqs, dks, dvs = [], [], []
        for h in range(NA_HEADS):
            hs = slice(h * HEAD_DIM, (h + 1) * HEAD_DIM)
            qh, kh, doh = qv[:, hs], kv[:, hs], dov[:, hs]
            s = _nt(qh, kh)
            e = jnp.exp(s - jnp.max(s, axis=-1, keepdims=True))
            p = e * (1.0 / jnp.sum(e, axis=-1, keepdims=True))
            dp = _nt(doh, vv[:, hs])
            ds = (p * (dp - jnp.sum(p * dp, axis=-1, keepdims=True))).astype(bf16)
            dqs.append(_nn(ds, kh))
            dks.append(_tn(ds, qh))
            dvs.append(_tn(p.astype(bf16), doh))
        dq_ref[...] = jnp.concatenate(dqs, axis=-1)
        dk_ref[...] = jnp.concatenate(dks, axis=-1)
        dv_ref[...] = jnp.concatenate(dvs, axis=-1)

    spec = pl.BlockSpec((L, NA_WIDTH), lambda i: (blk, 0))
    ospec = pl.BlockSpec((L, NA_WIDTH), lambda i: (0, 0))
    return pl.pallas_call(
        body, name=name, grid=(1,), in_specs=[spec, spec, spec, spec],
        out_specs=[ospec, ospec, ospec], out_shape=[S_((L, NA_WIDTH), f32)] * 3,
        compiler_params=_cp(VMEM_MID, ("arbitrary",)),
    )(do, q, k, v)


def _pool_centered(u, band, inv):
    return _split_sum(_nn, band, u) * inv - u


def _split_sum(mm, band, t):
    hi = t.astype(bf16)
    lo = (t - hi.astype(f32)).astype(bf16)
    s = mm(band, jnp.concatenate([hi, lo], axis=1))
    n = t.shape[1]
    return s[:, :n] + s[:, n:]


def _pool_fwd(cfg, u, band, inv, w_pool, pool_scale, with_ctx, name):
    TM = cfg.TM
    nt = cfg.ntiles(with_ctx)
    C = POOL_CH

    def body(u_ref, band_ref, inv_ref, w_ref, ps_ref, o_ref):
        outs = []
        for g in range(POOL_GROUPS):
            d = _pool_centered(u_ref[:, g * C:(g + 1) * C], band_ref[0, g], inv_ref[0, g])
            outs.append(_nn(d.astype(bf16), w_ref[g].astype(bf16)) * ps_ref[:, g * C:(g + 1) * C])
        o_ref[...] = jnp.concatenate(outs, axis=-1).astype(bf16)

    typ4 = lambda i: (jnp.minimum(i // cfg.nxt, 1), 0, 0, 0)
    return pl.pallas_call(
        body, name=name, grid=(nt,),
        in_specs=[pl.BlockSpec((TM, POOL_WIDTH), lambda i: (i, 0)), pl.BlockSpec((1, POOL_GROUPS, TM, TM), typ4),
                  pl.BlockSpec((1, POOL_GROUPS, TM, 1), typ4), pl.BlockSpec((POOL_GROUPS, C, C), lambda i: (0, 0, 0)),
                  pl.BlockSpec((1, POOL_WIDTH), lambda i: (0, 0))],
        out_specs=pl.BlockSpec((TM, POOL_WIDTH), lambda i: (i, 0)),
        out_shape=S_((nt * TM, POOL_WIDTH), bf16),
        compiler_params=_cp(VMEM_MID, ("arbitrary",)),
    )(u, band, inv, w_pool, pool_scale)


def _pool_bwd(cfg, dmix, u, band, inv, w_pool, pool_scale, with_ctx, name):
    TM = cfg.TM
    nt = cfg.ntiles(with_ctx)
    C = POOL_CH

    def body(dy_ref, u_ref, band_ref, inv_ref, w_ref, ps_ref, du_ref, dw_ref, dps_ref):
        @pl.when(pl.program_id(0) == 0)
        def _():
            dw_ref[...] = jnp.zeros_like(dw_ref)
            dps_ref[...] = jnp.zeros_like(dps_ref)

        dus, dpss = [], []
        for g in range(POOL_GROUPS):
            gs = slice(g * C, (g + 1) * C)
            band_g, inv_g = band_ref[0, g], inv_ref[0, g]
            db = _pool_centered(u_ref[:, gs], band_g, inv_g).astype(bf16)
            wb = w_ref[g].astype(bf16)
            dy = dy_ref[:, gs].astype(f32)
            dpss.append(_rsum(dy * _nn(db, wb)))
            dys = (dy * ps_ref[:, gs]).astype(bf16)
            dw_ref[g] += _tn(db, dys)
            dd = _nt(dys, wb)
            dus.append(_split_sum(_tn, band_g, dd * inv_g) - dd)
        du_ref[...] = jnp.concatenate(dus, axis=-1)
        dps_ref[...] += jnp.concatenate(dpss, axis=-1)

    typ4 = lambda i: (jnp.minimum(i // cfg.nxt, 1), 0, 0, 0)
    return pl.pallas_call(
        body, name=name, grid=(nt,),
        in_specs=[pl.BlockSpec((TM, POOL_WIDTH), lambda i: (i, 1)), pl.BlockSpec((TM, POOL_WIDTH), lambda i: (i, 0)),
                  pl.BlockSpec((1, POOL_GROUPS, TM, TM), typ4), pl.BlockSpec((1, POOL_GROUPS, TM, 1), typ4),
                  pl.BlockSpec((POOL_GROUPS, C, C), lambda i: (0, 0, 0)), pl.BlockSpec((1, POOL_WIDTH), lambda i: (0, 0))],
        out_specs=[pl.BlockSpec((TM, POOL_WIDTH), lambda i: (i, 0)), pl.BlockSpec((POOL_GROUPS, C, C), lambda i: (0, 0, 0)),
                   pl.BlockSpec((1, POOL_WIDTH), lambda i: (0, 0))],
        out_shape=[S_((nt * TM, POOL_WIDTH), f32), S_((POOL_GROUPS, C, C), f32), S_((1, POOL_WIDTH), f32)],
        compiler_params=_cp(VMEM_MID, ("arbitrary",)),
    )(dmix, u, band, inv, w_pool, pool_scale)


def _tmpost_fwd(cfg, na_x, na_c, pool, w_out, xs, mods, gvec, name):
    TM, D = cfg.TM, cfg.D
    with_ctx = na_c is not None
    nt = cfg.ntiles(with_ctx)
    R = nt * TM

    def body(*refs):
        if with_ctx:
            nax_ref, nac_ref, pool_ref, w_ref, xs_ref, mods_ref, g_ref, out_ref, opre_ref, mix_ref = refs
            na = jnp.where(pl.program_id(0) < cfg.nxt, nax_ref[...], nac_ref[...])
        else:
            nax_ref, pool_ref, w_ref, xs_ref, mods_ref, g_ref, out_ref, opre_ref, mix_ref = refs
            na = nax_ref[...]
        pool_v = pool_ref[...]
        mix_ref[:, 0:NA_WIDTH] = na
        mix_ref[:, NA_WIDTH:] = pool_v
        o = _nn(na, w_ref[0:NA_WIDTH, :]) + _nn(pool_v, w_ref[NA_WIDTH:, :])
        opre_ref[...] = o
        ohat, _ = _rms_hat(o)
        out_ref[...] = xs_ref[...] + mods_ref[0][5:6] * (ohat * g_ref[3:4])

    rt = lambda c: pl.BlockSpec((TM, c), lambda i: (i, 0))
    na_specs = [pl.BlockSpec((TM, NA_WIDTH), lambda i: (jnp.minimum(i, cfg.nxt - 1), 0))]
    na_args = [na_x]
    if with_ctx:
        na_specs.append(pl.BlockSpec((TM, NA_WIDTH), lambda i: (0, 0)))
        na_args.append(na_c)
    return pl.pallas_call(
        body, name=name, grid=(nt,),
        in_specs=na_specs + [rt(POOL_WIDTH), pl.BlockSpec((MIX_WIDTH, D), lambda i: (0, 0)), rt(D),
                             pl.BlockSpec((1, N_MOD, D), _typ(cfg)), pl.BlockSpec((6, D), lambda i: (0, 0))],
        out_specs=[rt(D), rt(D), rt(MIX_WIDTH)],
        out_shape=[S_((R, D), f32), S_((R, D), f32), S_((R, MIX_WIDTH), bf16)],
        compiler_params=_cp(VMEM_MID, ("arbitrary",)),
    )(*na_args, pool, w_out, xs, mods, gvec)


def _tmpost_bwd(cfg, dout, opre, w_out, mods, gvec, with_ctx, name):
    TM, D = cfg.TM, cfg.D
    nt = cfg.ntiles(with_ctx)
    R = nt * TM
    ntyp = 2 if with_ctx else 1

    def body(do_ref, opre_ref, w_ref, mods_ref, g_ref, dop_ref, dmix_ref, dm_ref, dg_ref):
        i = pl.program_id(0)

        @pl.when(i == 0)
        def _():
            dg_ref[...] = jnp.zeros_like(dg_ref)

        @pl.when((i == 0) | (i == cfg.nxt))
        def _():
            dm_ref[...] = jnp.zeros_like(dm_ref)

        do = do_ref[...]
        g3 = g_ref[3:4]
        ohat, rinv = _rms_hat(opre_ref[...])
        dm_ref[0] += _rsum(do * (ohat * g3))
        dr = mods_ref[0][5:6] * do
        dg_ref[...] += _rsum(dr * ohat)
        dob = _rms_bwd(dr * g3, ohat, rinv).astype(bf16)
        dop_ref[...] = dob
        dmix_ref[...] = _nt(dob, w_ref[...]).astype(bf16)

    rt = lambda c: pl.BlockSpec((TM, c), lambda i: (i, 0))
    return pl.pallas_call(
        body, name=name, grid=(nt,),
        in_specs=[rt(D), rt(D), pl.BlockSpec((MIX_WIDTH, D), lambda i: (0, 0)),
                  pl.BlockSpec((1, N_MOD, D), _typ(cfg)), pl.BlockSpec((6, D), lambda i: (0, 0))],
        out_specs=[rt(D), rt(MIX_WIDTH), pl.BlockSpec((1, 1, D), _typ(cfg)), pl.BlockSpec((1, D), lambda i: (0, 0))],
        out_shape=[S_((R, D), bf16), S_((R, MIX_WIDTH), bf16), S_((ntyp, 1, D), f32), S_((1, D), f32)],
        compiler_params=_cp(VMEM_MID, ("arbitrary",)),
    )(dout, opre, w_out, mods, gvec)


def _modvec_fwd(cvecs, w_mod, b_shard, name):
    nl, D, n = w_mod.shape
    tn = n // 3 if (n % 3 == 0 and (n // 3) % LANES == 0) else n

    def body(c_ref, w_ref, b_ref, o_ref, s_ref):
        cv = c_ref[...]
        sv = cv * _sigmoid(cv)
        s_ref[...] = sv
        o_ref[...] = _nn(sv.astype(bf16), w_ref[...].astype(bf16)) + b_ref[...]

    return pl.pallas_call(
        body, name=name, grid=(nl, n // tn),
        in_specs=[pl.BlockSpec((16, D), lambda l, j: (0, 0)), pl.BlockSpec((None, D, tn), lambda l, j: (l, 0, j)),
                  pl.BlockSpec((None, 1, tn), lambda l, j: (l, 0, j))],
        out_specs=[pl.BlockSpec((None, 16, tn), lambda l, j: (l, 0, j)), pl.BlockSpec((16, D), lambda l, j: (0, 0))],
        out_shape=[S_((nl, 16, n), f32), S_((16, D), f32)],
        compiler_params=_cp(VMEM_MID, ("arbitrary", "arbitrary")),
    )(cvecs, w_mod, b_shard)


def _modvec_bwd(s_t, dm, w_mod, name):
    nl, D, n = w_mod.shape
    tn = n // 3 if (n % 3 == 0 and (n // 3) % LANES == 0) else n

    def body(s_ref, dm_ref, w_ref, gw_ref, gc_ref):
        @pl.when(pl.program_id(1) == 0)
        def _():
            gc_ref[...] = jnp.zeros_like(gc_ref)
        dmv = dm_ref[...]
        gw_ref[...] = jnp.dot(s_ref[...], dmv, preferred_element_type=f32, precision=lax.Precision.HIGHEST)
        gc_ref[...] += _nt(dmv[8:16].astype(bf16), w_ref[...].astype(bf16))

    return pl.pallas_call(
        body, name=name, grid=(nl, n // tn),
        in_specs=[pl.BlockSpec((D, 16), lambda l, j: (0, 0)), pl.BlockSpec((None, 16, tn), lambda l, j: (l, 0, j)),
                  pl.BlockSpec((None, D, tn), lambda l, j: (l, 0, j))],
        out_specs=[pl.BlockSpec((None, D, tn), lambda l, j: (l, 0, j)), pl.BlockSpec((None, 8, D), lambda l, j: (l, 0, 0))],
        out_shape=[S_((nl, D, n), f32), S_((nl, 8, D), f32)],
        compiler_params=_cp(VMEM_MID, ("arbitrary", "arbitrary")),
    )(s_t, dm, w_mod)


def _as2d(a):
    n = a.size
    if a.ndim >= 2 and a.shape[-1] % LANES == 0:
        return a.reshape(-1, a.shape[-1])
    if n % LANES == 0:
        return a.reshape(-1, LANES)
    return a.reshape(-1, a.shape[-1]) if a.ndim >= 2 else a.reshape(1, n)


def _row_tile(r, c, budget_elems=512 * 1024):
    if r * c <= budget_elems or r % 8 != 0:
        return r
    t = r
    while t * c > budget_elems and t % 16 == 0:
        t //= 2
    return t


def _div_tile(r, c, budget_elems, mult=16):
    best = None
    for t in range(mult, r + 1, mult):
        if r % t == 0 and t * c <= budget_elems:
            best = t
    return best if best is not None else r


def _cast_into_place(shards, lead, axis, kidx, name):
    r, c = shards.shape[-2:]
    tr = _div_tile(r, c, 768 * 1024)
    nr = r // tr
    out_map = (lambda i, k: (i, k[0])) if axis == 1 else (lambda i, k: (k[0] * nr + i, 0))
    full2 = (r, c * N_CHIPS) if axis == 1 else (r * N_CHIPS, c)

    def body(k_ref, a_ref, o_ref):
        o_ref[...] = a_ref[...].astype(bf16)

    return pl.pallas_call(
        body, name=name,
        grid_spec=pltpu.PrefetchScalarGridSpec(
            num_scalar_prefetch=1, grid=(nr,),
            in_specs=[pl.BlockSpec((None,) * len(lead) + (tr, c), lambda i, k: tuple(lead) + (i, 0))],
            out_specs=pl.BlockSpec((tr, c), out_map)),
        out_shape=S_(full2, bf16), compiler_params=_cp(VMEM_MID, ("arbitrary",)),
    )(kidx, shards)


def _sum_devices8(own, land, axis, into, lead, dck, name):
    _, rh, cs = land.shape
    tr = _div_tile(rh, cs, 400 * 1024)
    nr = rh // tr
    if axis == 1:
        own_map = lambda i, s: (s[1] * nr + i, s[2])
    else:
        own_map = lambda i, s: (s[2] * 2 * nr + s[1] * nr + i, 0)
    nl = len(lead)

    def land_spec(j):
        return pl.BlockSpec((None, tr, cs), lambda i, s: ((s[0] + j) % N_DEV, i, 0))

    def body(s_ref, own_ref, *rest):
        acc = own_ref[...]
        for p_ref in rest[:N_DEV - 1]:
            acc = acc + p_ref[...].astype(f32)
        rest[-1][...] = acc

    return pl.pallas_call(
        body, name=name,
        grid_spec=pltpu.PrefetchScalarGridSpec(
            num_scalar_prefetch=1, grid=(nr,),
            in_specs=[pl.BlockSpec((tr, cs), own_map)] + [land_spec(j) for j in range(1, N_DEV)] + [ANY],
            out_specs=pl.BlockSpec((None,) * nl + (tr, cs), lambda i, s: tuple(lead) + (s[1] * nr + i, 0))),
        out_shape=S_(into.shape, f32), input_output_aliases={N_DEV + 1: 0},
        compiler_params=_cp(VMEM_MID, ("arbitrary",)),
    )(dck, own, *([land] * (N_DEV - 1)), into)


def _adamw(w, g, m, v, name, emit_grad=False):
    shape = w.shape
    w2, g2, m2, v2 = _as2d(w), _as2d(g), _as2d(m), _as2d(v)
    r, c = w2.shape
    tr = _row_tile(r, c, 256 * 1024)
    c1 = 1.0 - ADAM_B1 ** ADAM_STEP
    c2 = 1.0 - ADAM_B2 ** ADAM_STEP
    n_out = 4 if emit_grad else 3

    def body(w_ref, g_ref, m_ref, v_ref, d_ref, mo_ref, vo_ref, *go_ref):
        gv = g_ref[...]
        mn = ADAM_B1 * m_ref[...] + (1.0 - ADAM_B1) * gv
        vn = ADAM_B2 * v_ref[...] + (1.0 - ADAM_B2) * (gv * gv)
        mo_ref[...] = mn
        vo_ref[...] = vn
        d_ref[...] = -ADAM_LR * ((mn / c1) / (jnp.sqrt(vn / c2) + ADAM_EPS) + ADAM_WD * w_ref[...])
        if emit_grad:
            go_ref[0][...] = gv

    spec = pl.BlockSpec((tr, c), lambda i: (i, 0))
    outs = pl.pallas_call(body, name=name, grid=(r // tr,), in_specs=[spec] * 4, out_specs=[spec] * n_out,
                          out_shape=[S_((r, c), f32)] * n_out, compiler_params=_cp(VMEM_MID, ("arbitrary",)))(w2, g2, m2, v2)
    return tuple(o.reshape(shape) for o in outs)


def _sum_devices(gathered, name):
    _, r, c = gathered.shape

    def body(a_ref, o_ref):
        acc = a_ref[0]
        for j in range(1, N_DEV):
            acc = acc + a_ref[j]
        o_ref[...] = acc

    tr = _row_tile(r, c, 64 * 1024)
    return pl.pallas_call(
        body, name=name, grid=(r // tr,),
        in_specs=[pl.BlockSpec((N_DEV, tr, c), lambda i: (0, i, 0))], out_specs=pl.BlockSpec((tr, c), lambda i: (i, 0)),
        out_shape=S_((r, c), f32), compiler_params=_cp(VMEM_MID, ("arbitrary",)))(gathered)


def _all_gather_small(block, name):
    m_per, n = block.shape

    def body(x_ref, out_ref, send_sems, recv_sems, local_sem):
        x, y, c = _mesh_pos()
        me, sibling = (x, y, c), (x, y, 1 - c)
        chips = [(1 - x, y), (x, 1 - y), (1 - x, 1 - y)]

        def rows(px, py, pc):
            return out_ref.at[pl.ds((4 * px + 2 * py + pc) * m_per, m_per), :]

        def copy(k, blk, to, src=None):
            return pltpu.make_async_remote_copy(
                src_ref=rows(*blk) if src is None else src, dst_ref=rows(*blk),
                send_sem=send_sems.at[k], recv_sem=recv_sems.at[k], device_id=to, device_id_type=MESH)

        mine = pltpu.make_async_copy(x_ref, rows(*me), local_sem)
        mine.start()
        first = [copy(0, me, sibling, src=x_ref)]
        first += [copy(1 + j, me, (*chip, c), src=x_ref) for j, chip in enumerate(chips)]
        for cp in first:
            cp.start()
        passed = [copy(4 + j, (*chip, c), sibling) for j, chip in enumerate(chips)]
        for j, chip in enumerate(chips):
            copy(1 + j, (*chip, c), me).wait_recv()
            passed[j].start()
        copy(0, sibling, me).wait_recv()
        for j, chip in enumerate(chips):
            copy(4 + j, (*chip, 1 - c), me).wait_recv()
        for cp in first + passed:
            cp.wait_send()
        mine.wait()

    return pl.pallas_call(
        body, name=name, out_shape=S_((N_DEV * m_per, n), block.dtype),
        in_specs=[pl.BlockSpec(memory_space=pltpu.VMEM)], out_specs=pl.BlockSpec(memory_space=pltpu.VMEM),
        scratch_shapes=[pltpu.SemaphoreType.DMA((7,)), pltpu.SemaphoreType.DMA((7,)), pltpu.SemaphoreType.DMA],
        compiler_params=_cp(VMEM_MID),
    )(block)


def _pack_rows(arrays):
    flat = jnp.concatenate([a.reshape(-1) for a in arrays])
    pad = (-flat.size) % (8 * LANES)
    return jnp.pad(flat, (0, pad)).reshape(-1, LANES)


def _unpack_rows(packed, shapes):
    flat = packed.reshape(-1)
    out, off = [], 0
    for s in shapes:
        n = int(np.prod(s))
        out.append(flat[off:off + n].reshape(s))
        off += n
    return out


W_AXIS = {"gu": 1, "dn": 0, "wi": 1, "wo": 0}


def _half_merge(bufs, name):
    nt = len(bufs)

    def body(*refs):
        outs = refs[nt:2 * nt]
        send_sems, recv_sems = refs[2 * nt:]
        x, y, c = _mesh_pos()

        def half(ref, h):
            rh = ref.shape[-2] // 2
            return ref.at[(slice(None),) * (len(ref.shape) - 2) + (pl.ds(h * rh, rh), slice(None))]

        cps = []
        for t in range(nt):
            cp = pltpu.make_async_remote_copy(
                src_ref=half(outs[t], c), dst_ref=half(outs[t], c), send_sem=send_sems.at[t], recv_sem=recv_sems.at[t],
                device_id=(x, y, 1 - c), device_id_type=MESH)
            cp.start()
            cps.append(cp)
        for t in range(nt):
            pltpu.make_async_remote_copy(
                src_ref=half(outs[t], 1 - c), dst_ref=half(outs[t], 1 - c), send_sem=send_sems.at[t], recv_sem=recv_sems.at[t],
                device_id=(x, y, 1 - c), device_id_type=MESH).wait_recv()
        for cp in cps:
            cp.wait_send()

    return pl.pallas_call(
        body, name=name, in_specs=[ANY] * nt, out_specs=[ANY] * nt, out_shape=[S_(b.shape, f32) for b in bufs],
        input_output_aliases={t: t for t in range(nt)},
        scratch_shapes=[pltpu.SemaphoreType.DMA((nt,)), pltpu.SemaphoreType.DMA((nt,))],
        compiler_params=_cp(VMEM_MID),
    )(*bufs)


def _local_step(cfg, x_lat, x_ctx, target, mods, norm_g, W, G, dck, na_rpb, w_pool, pool_scale):
    S, L, T, D, F = cfg.S, cfg.L, cfg.T, cfg.D, cfg.F
    depth = norm_g.shape[0]
    cos, sin = _rope_tables(S, L)
    band, inv = _pool_tables(cfg.TM, L)
    flip, sel = _rpb_reduce_tables()

    assert depth == 2, "the carrier schedules below are written for two layers"
    fwd_carry = {"ffn_fwd_0_0": [("wi", 0), ("wo", 0), ("gu", 0, 1), ("dn", 0, 1)],
                 "na_fwd_0": [("gu", 1, 0), ("dn", 1, 0)],
                 "ffn_fwd_0_1": [("wi", 1), ("wo", 1), ("gu", 1, 1), ("dn", 1, 1)]}
    bwd_carry = {"na_bwd_1": [("gu", 1, 1), ("dn", 1, 1)], "ffn_bwd_1_0": [("wi", 1), ("wo", 1)],
                 "ffn_bwd_0_1": [("gu", 1, 0), ("dn", 1, 0)], "na_bwd_0": [("gu", 0, 1), ("dn", 0, 1)],
                 "ffn_bwd_0_0": [("wi", 0), ("wo", 0)], "wgrad_dn_0_0": [("gu", 0, 0)]}
    last_scatter = [("dn", 0, 0)]
    tag = lambda key: "_".join(str(p) for p in key)
    g_f32, g_b16 = {}, {}

    def gather_on(name):
        keys = fwd_carry.get(name)
        return None if keys is None else _gather_comm([W[k_] for k_ in keys], [W_AXIS[k_[0]] for k_ in keys])

    def gathered(name, res):
        if name in fwd_carry:
            W.update(zip(fwd_carry[name], res))

    def scatter_on(name):
        keys = bwd_carry.get(name)
        return None if keys is None else _scatter_comm([g_b16[k_] for k_ in keys], [W_AXIS[k_[0]] for k_ in keys])

    def scattered(keys, lands):
        for key, land in zip(keys, lands):
            G[key[0]] = _sum_devices8(g_f32[key], land, W_AXIS[key[0]], G[key[0]], key[1:], dck, f"sum8_{tag(key)}")

    def wgrad(key, a, b, rows):
        name = f"wgrad_{tag(key)}"
        (g_f32[key], g_b16[key]), lands = _wgrad(a, b, rows, name, scatter_on(name))
        scattered(bwd_carry.get(name, ()), lands)

    saved = []
    xs, xs_ctx = x_lat, x_ctx
    for l in range(depth):
        last = l == depth - 1
        wc = not last
        gvec = norm_g[l]
        ps = pool_scale[l].reshape(1, POOL_WIDTH)
        bexp = _expand_rpb(na_rpb[l], f"bias_expand_{l}")
        name = f"ffn_fwd_{l}_0"
        (xs1, hb1, z1, y1), res = _ffn_fwd(cfg, xs, mods[l], gvec, W["gu", l, 0], W["dn", l, 0], 0, 0, True, name,
                                           gather_on(name), xs_ctx=xs_ctx)
        gathered(name, res)
        hb2, q, k, v, u = _tmpre_fwd(cfg, xs1, mods[l], gvec, W["wi", l], cos, sin, f"tmpre_fwd_{l}")
        name = f"na_fwd_{l}"
        (na_x,), res = _na_fwd(cfg, q, k, v, bexp, name, gather_on(name))
        gathered(name, res)
        na_c = _ctx_attn_fwd(cfg, q, k, v, f"ctx_attn_fwd_{l}") if wc else None
        pool = _pool_fwd(cfg, u, band, inv, w_pool[l], ps, wc, f"pool_fwd_{l}")
        xs2, opre, mix = _tmpost_fwd(cfg, na_x, na_c, pool, W["wo", l], xs1, mods[l], gvec, f"tmpost_fwd_{l}")
        name = f"ffn_fwd_{l}_1"
        outs, res = _ffn_fwd(cfg, xs2, mods[l], gvec, W["gu", l, 1], W["dn", l, 1], 6, 4, wc, name, gather_on(name),
                             loss_target=target if last else None)
        xs3, hb3, z3, y3 = outs[:4]
        gathered(name, res)
        saved.append(dict(xs=xs, xs_ctx=xs_ctx, xs1=xs1, xs2=xs2, hb1=hb1, z1=z1, y1=y1, hb2=hb2, q=q, k=k, v=v, u=u, mix=mix,
                          opre=opre, hb3=hb3, z3=z3, y3=y3, bexp=bexp, ps=ps, gvec=gvec))
        xs, xs_ctx = xs3, None

    dxs, loss_blk = xs, outs[4]

    small = [None] * depth
    for l in reversed(range(depth)):
        last = l == depth - 1
        wc = not last
        sv = saved[l]
        gvec = sv["gvec"]
        rows_b = cfg.T if wc else cfg.S
        name = f"ffn_bwd_{l}_1"
        (dxs2, dz, dyb, ab, dm678, dg45), lands = _ffn_bwd(cfg, dxs, sv["xs2"], sv["z3"], sv["y3"], mods[l], gvec,
                                                           W["gu", l, 1], W["dn", l, 1], 6, 4, wc, name, scatter_on(name))
        scattered(bwd_carry.get(name, ()), lands)
        wgrad(("gu", l, 1), sv["hb3"], dz, rows_b)
        wgrad(("dn", l, 1), ab, dyb, rows_b)
        dop, dmix, dm5, dg3 = _tmpost_bwd(cfg, dxs2, sv["opre"], W["wo", l], mods[l], gvec, wc, f"tmpost_bwd_{l}")
        wgrad(("wo", l), sv["mix"], dop, rows_b)
        du, dwp, dps = _pool_bwd(cfg, dmix, sv["u"], band, inv, w_pool[l], sv["ps"], wc, f"pool_bwd_{l}")
        name = f"na_bwd_{l}"
        (dq, dk, dv, dkc, dvc, dbexp), lands = _na_bwd(cfg, dmix, sv["q"], sv["k"], sv["v"], sv["bexp"], name, scatter_on(name))
        scattered(bwd_carry.get(name, ()), lands)
        drpb = _rpb_reduce(dbexp, flip, sel, f"rpb_reduce_{l}")
        if wc:
            dqc, dkc2, dvc2 = _ctx_attn_bwd(cfg, dmix, sv["q"], sv["k"], sv["v"], f"ctx_attn_bwd_{l}")
            ctx_terms = ([dqc], [dkc, dkc2], [dvc, dvc2])
        else:
            ctx_terms = ([], [dkc], [dvc])
        dxs1, dproj, dm34, dg2 = _tmpre_bwd(cfg, (dq, dk, dv, du), ctx_terms, wc, cos, sin, W["wi", l], sv["xs1"], mods[l], gvec,
                                            dxs2, wc, f"tmpre_bwd_{l}")
        wgrad(("wi", l), sv["hb2"], dproj, cfg.T)
        name = f"ffn_bwd_{l}_0"
        (dxs, dz, dyb, ab, dm012, dg01), lands = _ffn_bwd(cfg, dxs1, sv["xs"], sv["z1"], sv["y1"], mods[l], gvec,
                                                          W["gu", l, 0], W["dn", l, 0], 0, 0, True, name, scatter_on(name),
                                                          xs_ctx=sv["xs_ctx"])
        scattered(bwd_carry.get(name, ()), lands)
        wgrad(("gu", l, 0), sv["hb1"], dz, cfg.T)
        wgrad(("dn", l, 0), ab, dyb, cfg.T)
        if not wc:
            zero = lambda a: jnp.concatenate([a, jnp.zeros_like(a)], axis=0)
            dm5, dm678 = zero(dm5), zero(dm678)
        dmods = jnp.concatenate([dm012, dm34, dm5, dm678], axis=1)
        dgs = jnp.concatenate([dg01, dg2, dg3, dg45], axis=0)
        small[l] = dict(dmods=dmods, dg=dgs, drpb=drpb, dwp=dwp, dps=dps)
    lands = _comm_only(_scatter_comm([g_b16[k_] for k_ in last_scatter], [W_AXIS[k_[0]] for k_ in last_scatter]), "scatter_last")
    scattered(last_scatter, lands)
    kinds = ("gu", "dn", "wi", "wo")
    merged = _half_merge([G[k_] for k_ in kinds], "merge_halves")
    return loss_blk, dxs, dict(zip(kinds, merged)), small


def kernel(x, c, ctx, c_ctx, w_mod, b_mod, norm_g, w_ffn_gate_up, w_ffn_down, w_in, w_out, na_rpb, w_pool, pool_scale, loss_target, m_c_ctx, m_w_mod, m_b_mod, m_norm_g, m_w_ffn_gate_up, m_w_ffn_down, m_w_in, m_w_out, m_na_rpb, m_w_pool, m_pool_scale, v_c_ctx, v_w_mod, v_b_mod, v_norm_g, v_w_ffn_gate_up, v_w_ffn_down, v_w_in, v_w_out, v_na_rpb, v_w_pool, v_pool_scale):
    S, D = x.shape[1], x.shape[2]
    L = ctx.shape[1]
    depth = w_mod.shape[0]
    F = w_ffn_down.shape[2] * N_CHIPS
    nmod = w_mod.shape[2]
    gsh = norm_g.shape[2]
    cfg = _Cfg(S, L, D, F)
    mx, my, mc = _mesh_pos()
    chip = 2 * mx + my
    dev = 4 * mx + 2 * my + mc

    kidx = chip.astype(jnp.int32).reshape(1)
    dck = jnp.stack([dev, mc, chip]).astype(jnp.int32)
    W = {}
    for l in range(depth):
        for i in range(2):
            W["gu", l, i] = _cast_into_place(w_ffn_gate_up, (l, i), W_AXIS["gu"], kidx, f"cast_gu_{l}_{i}")
            W["dn", l, i] = _cast_into_place(w_ffn_down, (l, i), W_AXIS["dn"], kidx, f"cast_dn_{l}_{i}")
        W["wi", l] = _cast_into_place(w_in, (l,), W_AXIS["wi"], kidx, f"cast_wi_{l}")
        W["wo", l] = _cast_into_place(w_out, (l,), W_AXIS["wo"], kidx, f"cast_wo_{l}")
    first = [("gu", 0, 0), ("dn", 0, 0)]
    W.update(zip(first, _comm_only(_gather_comm([W[k_] for k_ in first], [W_AXIS[k_[0]] for k_ in first]), "gather_first")))
    G = {"gu": lax.empty(w_ffn_gate_up.shape, f32), "dn": lax.empty(w_ffn_down.shape, f32),
         "wi": lax.empty(w_in.shape, f32), "wo": lax.empty(w_out.shape, f32)}

    c_all = _all_gather_small(jnp.pad(c, ((0, 7), (0, 0))), "gather_c").reshape(N_DEV, 8, D)[:, 0]
    cvecs = jnp.concatenate([c_all, c_ctx[None], jnp.zeros((7, D), f32)], axis=0)
    b_shard = lax.dynamic_slice_in_dim(b_mod, chip * nmod, nmod, axis=1).reshape(depth, 1, nmod)
    m_part, silu_c = _modvec_fwd(cvecs, w_mod, b_shard, "modvec_fwd")
    m_all = _all_gather_small(m_part.reshape(depth * 16, nmod), "gather_mod").reshape(N_DEV, depth, 16, nmod)
    m_full = jnp.concatenate([m_all[2 * j] for j in range(N_CHIPS)], axis=-1)
    m_mine = lax.dynamic_index_in_dim(m_full, dev, axis=1, keepdims=False)
    mods = jnp.stack([m_mine, m_full[:, 8]], axis=1).reshape(depth, 2, N_MOD, D)

    norm_g_full = _all_gather_small(_pack_rows([norm_g]), "gather_norm_g")
    rows_g = norm_g_full.shape[0] // N_DEV
    ng = norm_g_full.reshape(N_DEV, rows_g * LANES)[:, :norm_g.size].reshape(N_DEV, depth, 6, gsh)
    norm_g_all = jnp.concatenate([ng[2 * j] for j in range(N_CHIPS)], axis=-1)
    loss_blk, dx_lat, wgrads, small = _local_step(cfg, x[0], ctx[0], loss_target[0], mods, norm_g_all, W, G, dck,
                                                  na_rpb, w_pool, pool_scale)
    loss = lax.psum(loss_blk[0, 0], ("x", "y", "c"))
    grad_x = dx_lat[None]

    g_gu, g_dn, g_wi, g_wo = wgrads["gu"], wgrads["dn"], wgrads["wi"], wgrads["wo"]
    names = ("dmods", "dg", "drpb", "dwp", "dps")
    parts = [jnp.stack([small[l][n] for l in range(depth)]) for n in names]
    shapes = [p.shape for p in parts]
    packed = _pack_rows(parts)
    gathered = _all_gather_small(packed, "gather_small").reshape(N_DEV, packed.shape[0], LANES)
    total = _unpack_rows(_sum_devices(gathered, "sum_small"), shapes)
    dmods_sum, dg_sum, drpb_sum, dwp_sum, dps_sum = total
    dmods_each = jnp.stack([_unpack_rows(gathered[j], shapes[:1])[0] for j in range(N_DEV)])
    dm_rows = jnp.concatenate([jnp.transpose(dmods_each[:, :, 0], (1, 0, 2, 3)).reshape(depth, N_DEV, N_MOD * D),
                               dmods_sum[:, 1].reshape(depth, 1, N_MOD * D),
                               jnp.zeros((depth, 7, N_MOD * D), f32)], axis=1)
    dm_shard = lax.dynamic_slice_in_dim(dm_rows, chip * nmod, nmod, axis=2)
    grad_w_mod, gc_part = _modvec_bwd(silu_c.T, dm_shard, w_mod, "modvec_bwd")
    gc_all = _all_gather_small(gc_part.reshape(depth * 8, D), "gather_gc").reshape(N_DEV, depth, 8, D)
    grad_b_mod, grad_c_ctx = _small_finish(dm_rows, gc_all, c_ctx)
    grad_norm_g = lax.dynamic_slice_in_dim(dg_sum, chip * gsh, gsh, axis=2)
    grad_na_rpb = drpb_sum[:, :, :2 * NA_KH - 1, :2 * NA_KW - 1]
    grad_w_pool = dwp_sum
    grad_pool_scale = dps_sum.reshape(depth, POOL_WIDTH)

    grads = [grad_c_ctx, grad_w_mod, grad_b_mod, grad_norm_g, g_gu, g_dn, g_wi, g_wo, grad_na_rpb, grad_w_pool, grad_pool_scale]
    ws = [c_ctx, w_mod, b_mod, norm_g, w_ffn_gate_up, w_ffn_down, w_in, w_out, na_rpb, w_pool, pool_scale]
    ms = [m_c_ctx, m_w_mod, m_b_mod, m_norm_g, m_w_ffn_gate_up, m_w_ffn_down, m_w_in, m_w_out, m_na_rpb, m_w_pool, m_pool_scale]
    vs = [v_c_ctx, v_w_mod, v_b_mod, v_norm_g, v_w_ffn_gate_up, v_w_ffn_down, v_w_in, v_w_out, v_na_rpb, v_w_pool, v_pool_scale]
    tags = ["c_ctx", "w_mod", "b_mod", "norm_g", "gate_up", "down", "w_in", "w_out", "na_rpb", "w_pool", "pool_scale"]
    merged = ("gate_up", "down", "w_in", "w_out")
    upd = [_adamw(w_, g_, m_, v_, f"adamw_{t}", emit_grad=t in merged) for w_, g_, m_, v_, t in zip(ws, grads, ms, vs, tags)]
    grads = [u_[3] if t in merged else g_ for g_, u_, t in zip(grads, upd, tags)]
    return (loss, grad_x, *grads, *[u_[0] for u_ in upd], *[u_[1] for u_ in upd], *[u_[2] for u_ in upd])


def _small_finish(dm_rows, gc_all, c_ctx):
    depth, _, n = dm_rows.shape
    D = c_ctx.shape[0]

    def body(dm_ref, gc_ref, c_ref, gb_ref, gcx_ref):
        acc = dm_ref[:, 0]
        for j in range(1, N_DEV + 1):
            acc = acc + dm_ref[:, j]
        gb_ref[...] = acc
        t = jnp.zeros((1, D), f32)
        for l in range(depth):
            for j in range(N_CHIPS):
                t = t + gc_ref[2 * j, l, 0:1, :]
        cv = c_ref[...]
        sg = _sigmoid(cv)
        gcx_ref[...] = t * (sg * (1.0 + cv * (1.0 - sg)))

    gb, gcx = pl.pallas_call(
        body, name="small_finish",
        out_shape=[S_((depth, n), f32), S_((1, D), f32)],
        compiler_params=_cp(VMEM_MID),
    )(dm_rows, gc_all, c_ctx.reshape(1, D))
    return gb, gcx.reshape(D)
```

```python
import functools

import numpy as np
import jax
import jax.numpy as jnp
from jax import lax
from jax.experimental import pallas as pl
from jax.experimental.pallas import tpu as pltpu

f32, bf16 = jnp.float32, jnp.bfloat16

GRID_W = 64
N_MOD = 9
NA_HEADS = 8
HEAD_DIM = 64
NA_WIDTH = NA_HEADS * HEAD_DIM
NA_KH = 8
NA_KW = 16
POOL_GROUPS = 4
POOL_CH = 128
POOL_WIDTH = POOL_GROUPS * POOL_CH
POOL_WINDOWS = (2, 4, 8, 16)
IN_WIDTH = 3 * NA_WIDTH + POOL_WIDTH
MIX_WIDTH = NA_WIDTH + POOL_WIDTH
ROPE_THETA = 10000.0
ROPE_PAIRS = HEAD_DIM // 4
RMS_EPS = 1e-6
NEG_INF = -1e30
ADAM_LR, ADAM_B1, ADAM_B2, ADAM_EPS, ADAM_WD, ADAM_STEP = 0.001, 0.9, 0.999, 1e-08, 0.01, 10

N_DEV = 8
N_CHIPS = 4
LANES = 128
MIB = 1024 * 1024
VMEM_BIG = 52 * MIB
VMEM_MID = 40 * MIB
MESH = pl.DeviceIdType.MESH
ANY = pl.BlockSpec(memory_space=pl.ANY)
S_ = jax.ShapeDtypeStruct


def _cp(vmem=VMEM_MID, sem=None):
    return pltpu.CompilerParams(vmem_limit_bytes=vmem, dimension_semantics=sem)


def _sigmoid(x):
    return 0.5 * jnp.tanh(0.5 * x) + 0.5


def _rms_hat(x):
    rinv = lax.rsqrt(jnp.mean(x * x, axis=-1, keepdims=True) + RMS_EPS)
    return x * rinv, rinv


def _rms_bwd(dxhat, xhat, rinv):
    return rinv * (dxhat - xhat * jnp.mean(dxhat * xhat, axis=-1, keepdims=True))


def _rsum(a):
    return jnp.sum(a, axis=0, keepdims=True)


def _nt(a, b):
    return lax.dot_general(a, b, (((1,), (1,)), ((), ())), preferred_element_type=f32)


def _tn(a, b):
    return lax.dot_general(a, b, (((0,), (0,)), ((), ())), preferred_element_type=f32)


def _nn(a, b):
    return jnp.dot(a, b, preferred_element_type=f32)


def _swap16(x):
    lane = lax.broadcasted_iota(jnp.int32, x.shape, 1)
    n = x.shape[1]
    return jnp.where((lane % 32) < 16, pltpu.roll(x, n - 16, 1), pltpu.roll(x, 16, 1))


def _rope_tables(s_len, l_len):
    t = np.arange(s_len)
    inv = ROPE_THETA ** (-np.arange(ROPE_PAIRS, dtype=np.float32) / ROPE_PAIRS)
    ang_r = (t // GRID_W).astype(np.float32)[:, None] * inv
    ang_c = (t % GRID_W).astype(np.float32)[:, None] * inv
    cos = np.concatenate([np.cos(ang_r), np.cos(ang_r), np.cos(ang_c), np.cos(ang_c)], axis=-1)
    sin = np.concatenate([-np.sin(ang_r), np.sin(ang_r), -np.sin(ang_c), np.sin(ang_c)], axis=-1)
    cos = np.concatenate([cos, np.ones((l_len, HEAD_DIM), np.float32)], axis=0)
    sin = np.concatenate([sin, np.zeros((l_len, HEAD_DIM), np.float32)], axis=0)
    return (jnp.asarray(np.tile(cos, (1, 2)), f32), jnp.asarray(np.tile(sin, (1, 2)), f32))


def _pool_tables(tm, l_len):
    band = np.zeros((2, POOL_GROUPS, tm, tm), np.float32)
    inv = np.zeros((2, POOL_GROUPS, tm, 1), np.float32)
    for typ, length in ((0, GRID_W), (1, l_len)):
        for g, w in enumerate(POOL_WINDOWS):
            for t in range(tm):
                base, p = (t // length) * length, t % length
                lo = min(max(p - w // 2, 0), length)
                hi = min(max(p - w // 2 + w, 0), length)
                band[typ, g, t, base + lo:base + hi] = 1.0
                inv[typ, g, t, 0] = 1.0 / (hi - lo)
    return jnp.asarray(band, bf16), jnp.asarray(inv, f32)


NA_QR = 4
NA_WR = NA_KH + NA_QR - 1
NA_TYPES = 3
NA_SEL_ROWS = 136
NA_WPAD = 768


def _rpb_index_tables():
    j = np.arange(GRID_W)
    col_start = np.clip(j - NA_KW // 2, 0, GRID_W - NA_KW)
    valid = (j[None, :] >= col_start[:, None]) & (j[None, :] < col_start[:, None] + NA_KW)
    dc = np.clip(j[None, :] - j[:, None] + NA_KW - 1, 0, 2 * NA_KW - 2)
    i = np.arange(NA_QR)[:, None]
    kk = np.arange(NA_WR)[None, :]
    off = np.stack([np.zeros_like(i), i, np.full_like(i, NA_QR - 1)])
    d = np.stack([kk - i + NA_KH - 1, kk - i + NA_KH - 1 - NA_QR, kk - i])
    row_ok = (kk[None] >= off) & (kk[None] < off + NA_KH)
    assert (d[row_ok] >= 0).all() and (d[row_ok] <= 2 * NA_KH - 2).all()
    return valid, dc, d, row_ok


def _expand_rpb(rpb, name):
    _, _, d, row_ok = _rpb_index_tables()
    heads, nd, ne = rpb.shape
    w = GRID_W
    v = jnp.pad(rpb, ((0, 0), (0, 0), (w - NA_KW, 2 * w - (w - NA_KW) - ne)))
    x = jnp.broadcast_to(v[:, :, None, :], (heads, nd, w, 2 * w)).reshape(heads, nd, 2 * w * w)
    t = x[:, :, :w * (2 * w - 1)].reshape(heads, nd, w, 2 * w - 1)[..., w - 1:]

    def body(t_ref, o_ref):
        q = lax.broadcasted_iota(jnp.int32, (w, w), 0)
        c = lax.broadcasted_iota(jnp.int32, (w, w), 1)
        c0 = jnp.clip(q - NA_KW // 2, 0, w - NA_KW)
        in_cols = (c >= c0) & (c < c0 + NA_KW)
        outside = jnp.full((w, w), NEG_INF, f32)
        blocks = [jnp.where(in_cols, t_ref[dd], NEG_INF) for dd in range(nd)]
        for typ in range(NA_TYPES):
            for i in range(NA_QR):
                row = [blocks[d[typ, i, kk]] if row_ok[typ, i, kk] else outside for kk in range(NA_WR)]
                o_ref[typ, i * w:(i + 1) * w, :] = jnp.concatenate(row, axis=1)

    return pl.pallas_call(
        body, name=name, grid=(heads,),
        in_specs=[pl.BlockSpec((None, nd, w, w), lambda h: (h, 0, 0, 0))],
        out_specs=pl.BlockSpec((NA_TYPES, None, NA_QR * w, NA_WR * w), lambda h: (0, h, 0, 0)),
        out_shape=S_((NA_TYPES, heads, NA_QR * w, NA_WR * w), f32),
        compiler_params=_cp(VMEM_MID, ("arbitrary",)),
    )(t)


def _rpb_reduce_tables():
    _, _, d, row_ok = _rpb_index_tables()
    flip = np.eye(GRID_W, dtype=np.float32)[::-1].copy()
    sel = np.zeros((16, NA_SEL_ROWS), np.float32)
    flat_d, flat_ok = d.reshape(-1), row_ok.reshape(-1)
    for n in range(flat_d.size):
        if flat_ok[n]:
            sel[flat_d[n], n] = 1.0
    return jnp.asarray(flip), jnp.asarray(sel)


class _Cfg:
    def __init__(self, s_len, l_len, d, f):
        self.S, self.L, self.D, self.F = s_len, l_len, d, f
        self.T = s_len + l_len
        self.TM = 256 if l_len % 256 == 0 else 128
        assert l_len == self.TM, "context length must equal the row tile"
        assert s_len % self.TM == 0 and s_len % GRID_W == 0
        self.nxt = s_len // self.TM
        self.ntt = self.T // self.TM
        self.rows = s_len // GRID_W
        assert self.rows >= 2 * NA_KH
        assert f % (2 * LANES) == 0
        self.FC = f

    def ntiles(self, with_ctx):
        return self.ntt if with_ctx else self.nxt


def _typ(cfg):
    return lambda i: (jnp.minimum(i // cfg.nxt, 1), 0, 0)


def _mesh_pos():
    return lax.axis_index("x"), lax.axis_index("y"), lax.axis_index("c")


class _Comm:
    def __init__(self, ins, outs, alias, nsem, start, finish):
        self.ins, self.outs, self.alias, self.nsem, self.start, self.finish = ins, outs, alias, nsem, start, finish


def _call(body, args, comm=None, *, grid, in_specs, out_specs, out_shape, scratch_shapes=(), **kw):
    if comm is None:
        return pl.pallas_call(body, grid=grid, in_specs=list(in_specs), out_specs=list(out_specs), out_shape=list(out_shape),
                              scratch_shapes=list(scratch_shapes), **kw)(*args), ()
    n_in, n_out, n_sc = len(in_specs), len(out_specs), len(scratch_shapes)
    ci, co = len(comm.ins), len(comm.outs)

    def carrier(*refs):
        bounds = np.cumsum([0, n_in, ci, n_out, co, n_sc])
        ins, cins, outs, couts, scr = (refs[a:b] for a, b in zip(bounds[:-1], bounds[1:]))
        send, recv = refs[bounds[-1]], refs[bounds[-1] + 1]
        first = functools.reduce(jnp.logical_and, [pl.program_id(a) == 0 for a in range(len(grid))])
        last = functools.reduce(jnp.logical_and, [pl.program_id(a) == g - 1 for a, g in enumerate(grid)])

        @pl.when(first)
        def _():
            comm.start(cins, couts, send, recv)

        body(*ins, *outs, *scr)

        @pl.when(last)
        def _():
            comm.finish(cins, couts, send, recv)

    res = pl.pallas_call(
        carrier, grid=grid, in_specs=list(in_specs) + [ANY] * ci, out_specs=list(out_specs) + [ANY] * co,
        out_shape=list(out_shape) + list(comm.outs),
        input_output_aliases={n_in + a: n_out + b for a, b in comm.alias.items()},
        scratch_shapes=list(scratch_shapes) + [pltpu.SemaphoreType.DMA((comm.nsem,)), pltpu.SemaphoreType.DMA((comm.nsem,))],
        **kw)(*args, *comm.ins)
    return res[:n_out], res[n_out:]


def _comm_only(comm, name):
    ci, co = len(comm.ins), len(comm.outs)

    def body(*refs):
        cins, couts = refs[:ci], refs[ci:ci + co]
        send, recv = refs[ci + co], refs[ci + co + 1]
        comm.start(cins, couts, send, recv)
        comm.finish(cins, couts, send, recv)

    return pl.pallas_call(
        body, name=name, in_specs=[ANY] * ci, out_specs=[ANY] * co, out_shape=list(comm.outs),
        input_output_aliases=dict(comm.alias),
        scratch_shapes=[pltpu.SemaphoreType.DMA((comm.nsem,)), pltpu.SemaphoreType.DMA((comm.nsem,))],
        compiler_params=_cp(VMEM_MID),
    )(*comm.ins)


def _half_view(ref, axis, kk, h):
    r, c = ref.shape
    if axis == 1:
        n = c // N_CHIPS
        return ref.at[pl.ds(h * (r // 2), r // 2), pl.ds(pl.multiple_of(kk * n, LANES), n)]
    n = r // N_CHIPS
    return ref.at[pl.ds(pl.multiple_of(kk * n + h * (n // 2), 8), n // 2), :]


def _other_chips(x, y):
    return [(1 - x, y), (x, 1 - y), (1 - x, 1 - y)]


def _gather_comm(arrs, axes):
    n = len(arrs)

    def copy(ref, view, sems, k, to):
        send, recv = sems
        return pltpu.make_async_remote_copy(src_ref=view, dst_ref=view, send_sem=send.at[k], recv_sem=recv.at[k],
                                            device_id=to, device_id_type=MESH)

    def start(cins, bufs, send, recv):
        x, y, c = _mesh_pos()
        for t in range(n):
            own = _half_view(bufs[t], axes[t], 2 * x + y, c)
            for j, chip in enumerate(_other_chips(x, y)):
                copy(bufs[t], own, (send, recv), 6 * t + j, (*chip, c)).start()

    def finish(cins, bufs, send, recv):
        x, y, c = _mesh_pos()
        sibling = (x, y, 1 - c)
        chips = _other_chips(x, y)
        for t in range(n):
            for j, chip in enumerate(chips):
                landed = _half_view(bufs[t], axes[t], 2 * chip[0] + chip[1], c)
                copy(bufs[t], landed, (send, recv), 6 * t + j, (*chip, c)).wait_recv()
                copy(bufs[t], landed, (send, recv), 6 * t + 3 + j, sibling).start()
        for t in range(n):
            own = _half_view(bufs[t], axes[t], 2 * x + y, c)
            for j, chip in enumerate(chips):
                kj = 2 * chip[0] + chip[1]
                copy(bufs[t], _half_view(bufs[t], axes[t], kj, 1 - c), (send, recv), 6 * t + 3 + j, sibling).wait_recv()
                copy(bufs[t], own, (send, recv), 6 * t + j, (*chip, c)).wait_send()
                copy(bufs[t], _half_view(bufs[t], axes[t], kj, c), (send, recv), 6 * t + 3 + j, sibling).wait_send()

    return _Comm(list(arrs), [S_(a.shape, a.dtype) for a in arrs], {t: t for t in range(n)}, 6 * n, start, finish)


def _scatter_comm(parts, axes):
    n = len(parts)
    peers = [(fx, fy, fc) for fx in (0, 1) for fy in (0, 1) for fc in (0, 1)][1:]

    def half_shape(a, axis):
        r, c = a.shape
        return (r // 2, c // N_CHIPS) if axis == 1 else (r // N_CHIPS // 2, c)

    def start(srcs, lands, send, recv):
        x, y, c = _mesh_pos()
        me = 4 * x + 2 * y + c
        for t in range(n):
            for r_, (fx, fy, fc) in enumerate(peers):
                dx, dy, dc = (1 - x if fx else x), (1 - y if fy else y), (1 - c if fc else c)
                pltpu.make_async_remote_copy(
                    src_ref=_half_view(srcs[t], axes[t], 2 * dx + dy, dc), dst_ref=lands[t].at[me],
                    send_sem=send.at[7 * t + r_], recv_sem=recv.at[7 * t + r_],
                    device_id=(dx, dy, dc), device_id_type=MESH).start()

    def finish(srcs, lands, send, recv):
        x, y, c = _mesh_pos()
        for t in range(n):
            mine = _half_view(srcs[t], axes[t], 2 * x + y, c)
            for r_, (fx, fy, fc) in enumerate(peers):
                sx, sy, sc = (1 - x if fx else x), (1 - y if fy else y), (1 - c if fc else c)
                cp = pltpu.make_async_remote_copy(
                    src_ref=mine, dst_ref=lands[t].at[4 * sx + 2 * sy + sc],
                    send_sem=send.at[7 * t + r_], recv_sem=recv.at[7 * t + r_],
                    device_id=(sx, sy, sc), device_id_type=MESH)
                cp.wait_recv()
                cp.wait_send()

    return _Comm(list(parts), [S_((N_DEV,) + half_shape(a, ax), a.dtype) for a, ax in zip(parts, axes)], {}, 7 * n, start, finish)


def _ffn_fwd(cfg, xs, mods, gvec, wgu, wd, mi, gi, with_ctx, name, comm=None, xs_ctx=None, loss_target=None):
    TM, D, F, FC = cfg.TM, cfg.D, cfg.F, cfg.FC
    nt = cfg.ntiles(with_ctx)
    R = nt * TM
    split, head = xs_ctx is not None, loss_target is not None

    def body(*refs):
        it = iter(refs)
        xs_ref = next(it)
        xc_ref = next(it) if split else None
        mods_ref, g_ref, wgu_hbm, wd_hbm = next(it), next(it), next(it), next(it)
        t_ref = next(it) if head else None
        out_ref, hb_ref, z_ref, y_ref = next(it), next(it), next(it), next(it)
        loss_ref = next(it) if head else None
        wgu_v, wd_v, sem = next(it), next(it), next(it)
        i = pl.program_id(0)

        @pl.when(i == 0)
        def _():
            c0 = pltpu.make_async_copy(wgu_hbm, wgu_v, sem.at[0])
            c1 = pltpu.make_async_copy(wd_hbm, wd_v, sem.at[1])
            c0.start(); c1.start(); c0.wait(); c1.wait()
            if head:
                loss_ref[...] = jnp.zeros_like(loss_ref)
        x = xs_ref[...]
        if split:
            x = jnp.where(i < cfg.nxt, x, xc_ref[...])
        m = mods_ref[0]
        sh, sc, gt = m[mi:mi + 1], m[mi + 1:mi + 2], m[mi + 2:mi + 3]
        xhat, _ = _rms_hat(x)
        h = (xhat * g_ref[gi:gi + 1]) * (1.0 + sc) + sh
        hb = h.astype(bf16)
        hb_ref[...] = hb
        y = jnp.zeros((TM, D), f32)
        for ch in range(F // FC):
            zg = _nn(hb, wgu_v[:, ch * FC:(ch + 1) * FC])
            zu = _nn(hb, wgu_v[:, F + ch * FC:F + (ch + 1) * FC])
            z_ref[:, ch * FC:(ch + 1) * FC] = zg.astype(bf16)
            z_ref[:, F + ch * FC:F + (ch + 1) * FC] = zu.astype(bf16)
            a = (zg * _sigmoid(zg)) * zu
            y = y + _nn(a.astype(bf16), wd_v[ch * FC:(ch + 1) * FC, :])
        y_ref[...] = y
        yhat, _ = _rms_hat(y)
        out = x + 0.5 * gt * (yhat * g_ref[gi + 1:gi + 2])
        if head:
            e = out - t_ref[...]
            out_ref[...] = e * (1.0 / D)
            loss_ref[...] += jnp.sum(jnp.mean(e * e, axis=-1, keepdims=True), axis=0, keepdims=True) * 0.5
        else:
            out_ref[...] = out

    rt = lambda c: pl.BlockSpec((TM, c), lambda i: (i, 0))
    lat = pl.BlockSpec((TM, D), lambda i: (jnp.minimum(i, cfg.nxt - 1), 0))
    x_specs, x_args = ([lat, pl.BlockSpec((TM, D), lambda i: (0, 0))], [xs, xs_ctx]) if split else ([rt(D)], [xs])
    t_specs, t_args = ([rt(D)], [loss_target]) if head else ([], [])
    l_specs, l_shape = ([pl.BlockSpec((8, LANES), lambda i: (0, 0))], [S_((8, LANES), f32)]) if head else ([], [])
    return _call(
        body, (*x_args, mods, gvec, wgu, wd, *t_args), comm, name=name, grid=(nt,),
        in_specs=x_specs + [pl.BlockSpec((1, N_MOD, D), _typ(cfg)), pl.BlockSpec((6, D), lambda i: (0, 0)), ANY, ANY] + t_specs,
        out_specs=[rt(D), rt(D), rt(2 * F), rt(D)] + l_specs,
        out_shape=[S_((R, D), f32), S_((R, D), bf16), S_((R, 2 * F), bf16), S_((R, D), f32)] + l_shape,
        scratch_shapes=[pltpu.VMEM((D, 2 * F), bf16), pltpu.VMEM((F, D), bf16), pltpu.SemaphoreType.DMA((2,))],
        compiler_params=_cp(VMEM_BIG, ("arbitrary",)),
    )


def _ffn_bwd(cfg, dout, xs, z, y, mods, gvec, wgu, wd, mi, gi, with_ctx, name, comm=None, xs_ctx=None):
    TM, D, F, FC = cfg.TM, cfg.D, cfg.F, cfg.FC
    nt = cfg.ntiles(with_ctx)
    R = nt * TM
    ntyp = 2 if with_ctx else 1
    split = xs_ctx is not None

    def body(*refs):
        it = iter(refs)
        do_ref, xs_ref = next(it), next(it)
        xc_ref = next(it) if split else None
        z_ref, y_ref, mods_ref, g_ref, wgu_hbm, wd_hbm = (next(it) for _ in range(6))
        dx_ref, dz_ref, dy_ref, a_ref, dm_ref, dg_ref, wgu_v, wd_v, sem = (next(it) for _ in range(9))
        i = pl.program_id(0)

        @pl.when(i == 0)
        def _():
            c0 = pltpu.make_async_copy(wgu_hbm, wgu_v, sem.at[0])
            c1 = pltpu.make_async_copy(wd_hbm, wd_v, sem.at[1])
            c0.start(); c1.start(); c0.wait(); c1.wait()
            dg_ref[...] = jnp.zeros_like(dg_ref)

        @pl.when((i == 0) | (i == cfg.nxt))
        def _():
            dm_ref[...] = jnp.zeros_like(dm_ref)

        do = do_ref[...]
        x = xs_ref[...]
        if split:
            x = jnp.where(i < cfg.nxt, x, xc_ref[...])
        m = mods_ref[0]
        sc, gt = m[mi + 1:mi + 2], m[mi + 2:mi + 3]
        g_pre, g_post = g_ref[gi:gi + 1], g_ref[gi + 1:gi + 2]
        xhat, rinv0 = _rms_hat(x)
        n0 = xhat * g_pre
        yhat, rinv1 = _rms_hat(y_ref[...])
        d_gt = _rsum(0.5 * do * (yhat * g_post))
        dr = (0.5 * gt) * do
        dg_post = _rsum(dr * yhat)
        dy = _rms_bwd(dr * g_post, yhat, rinv1)
        dyb = dy.astype(bf16)
        dy_ref[...] = dyb
        dh = jnp.zeros((TM, D), f32)
        for ch in range(F // FC):
            zg = z_ref[:, ch * FC:(ch + 1) * FC].astype(f32)
            zu = z_ref[:, F + ch * FC:F + (ch + 1) * FC].astype(f32)
            sg = _sigmoid(zg)
            silu = zg * sg
            a_ref[:, ch * FC:(ch + 1) * FC] = (silu * zu).astype(bf16)
            da = _nt(dyb, wd_v[ch * FC:(ch + 1) * FC, :])
            dzu = (da * silu).astype(bf16)
            dzg = (da * zu * (sg * (1.0 + zg * (1.0 - sg)))).astype(bf16)
            dz_ref[:, ch * FC:(ch + 1) * FC] = dzg
            dz_ref[:, F + ch * FC:F + (ch + 1) * FC] = dzu
            dh = dh + _nt(dzg, wgu_v[:, ch * FC:(ch + 1) * FC]) + _nt(dzu, wgu_v[:, F + ch * FC:F + (ch + 1) * FC])
        d_sh = _rsum(dh)
        d_sc = _rsum(dh * n0)
        dn = dh * (1.0 + sc)
        dg_pre = _rsum(dn * xhat)
        dx = do + _rms_bwd(dn * g_pre, xhat, rinv0)
        if split:
            @pl.when(i < cfg.nxt)
            def _():
                dx_ref[...] = dx
        else:
            dx_ref[...] = dx
        dm_ref[0] += jnp.concatenate([d_sh, d_sc, d_gt], axis=0)
        dg_ref[...] += jnp.concatenate([dg_pre, dg_post], axis=0)

    rt = lambda c: pl.BlockSpec((TM, c), lambda i: (i, 0))
    lat = pl.BlockSpec((TM, D), lambda i: (jnp.minimum(i, cfg.nxt - 1), 0))
    x_specs, x_args = ([lat, pl.BlockSpec((TM, D), lambda i: (0, 0))], [xs, xs_ctx]) if split else ([rt(D)], [xs])
    return _call(
        body, (dout, *x_args, z, y, mods, gvec, wgu, wd), comm, name=name, grid=(nt,),
        in_specs=[rt(D)] + x_specs + [rt(2 * F), rt(D), pl.BlockSpec((1, N_MOD, D), _typ(cfg)),
                                       pl.BlockSpec((6, D), lambda i: (0, 0)), ANY, ANY],
        out_specs=[lat if split else rt(D), rt(2 * F), rt(D), rt(F), pl.BlockSpec((1, 3, D), _typ(cfg)),
                   pl.BlockSpec((2, D), lambda i: (0, 0))],
        out_shape=[S_((cfg.S if split else R, D), f32), S_((R, 2 * F), bf16), S_((R, D), bf16), S_((R, F), bf16),
                   S_((ntyp, 3, D), f32), S_((2, D), f32)],
        scratch_shapes=[pltpu.VMEM((D, 2 * F), bf16), pltpu.VMEM((F, D), bf16), pltpu.SemaphoreType.DMA((2,))],
        compiler_params=_cp(VMEM_BIG, ("arbitrary",)),
    )


def _wgrad(a, b, k_rows, name, comm=None):
    M, N = a.shape[1], b.shape[1]
    tn = N
    for cand in (1408, 1024, 512):
        if N % cand == 0 and N > cand:
            tn = cand
            break
    room = VMEM_BIG - 6 * MIB - 2 * M * tn * 6
    tk = _div_tile(k_rows, 1, min(2816, room // (4 * (M + tn))), LANES)
    nk = k_rows // tk

    def body(a_ref, b_ref, o_ref, ob_ref):
        k = pl.program_id(1)

        @pl.when(k == 0)
        def _():
            o_ref[...] = jnp.zeros_like(o_ref)
        o_ref[...] += _tn(a_ref[...], b_ref[...])

        @pl.when(k == nk - 1)
        def _():
            ob_ref[...] = o_ref[...].astype(bf16)

    ospec = pl.BlockSpec((M, tn), lambda n, k: (0, n))
    return _call(
        body, (a, b), comm, name=name, grid=(N // tn, nk),
        in_specs=[pl.BlockSpec((tk, M), lambda n, k: (k, 0)), pl.BlockSpec((tk, tn), lambda n, k: (k, n))],
        out_specs=[ospec, ospec], out_shape=[S_((M, N), f32), S_((M, N), bf16)],
        compiler_params=_cp(VMEM_BIG, ("arbitrary", "arbitrary")),
    )


def _tmpre_fwd(cfg, xs, mods, gvec, w_in, cos, sin, name):
    TM, D = cfg.TM, cfg.D
    nt, R = cfg.ntt, cfg.T
    W = NA_WIDTH

    def body(xs_ref, mods_ref, g_ref, w_ref, cos_ref, sin_ref, hb_ref, q_ref, k_ref, v_ref, u_ref):
        x = xs_ref[...]
        m = mods_ref[0]
        xhat, _ = _rms_hat(x)
        hb = ((xhat * g_ref[2:3]) * (1.0 + m[4:5]) + m[3:4]).astype(bf16)
        hb_ref[...] = hb
        p = _nn(hb, w_ref[...])
        cs = jnp.tile(cos_ref[...], (1, W // LANES))
        sn = jnp.tile(sin_ref[...], (1, W // LANES))
        q = p[:, 0:W]
        k = p[:, W:2 * W]
        q_ref[...] = ((q * cs + _swap16(q) * sn) * (HEAD_DIM ** -0.5)).astype(bf16)
        k_ref[...] = (k * cs + _swap16(k) * sn).astype(bf16)
        v_ref[...] = p[:, 2 * W:3 * W].astype(bf16)
        u_ref[...] = p[:, 3 * W:]

    rt = lambda c: pl.BlockSpec((TM, c), lambda i: (i, 0))
    return pl.pallas_call(
        body, name=name, grid=(nt,),
        in_specs=[rt(D), pl.BlockSpec((1, N_MOD, D), _typ(cfg)), pl.BlockSpec((6, D), lambda i: (0, 0)),
                  pl.BlockSpec((D, IN_WIDTH), lambda i: (0, 0)), rt(LANES), rt(LANES)],
        out_specs=[rt(D), rt(W), rt(W), rt(W), rt(POOL_WIDTH)],
        out_shape=[S_((R, D), bf16), S_((R, W), bf16), S_((R, W), bf16), S_((R, W), bf16), S_((R, POOL_WIDTH), f32)],
        compiler_params=_cp(VMEM_MID, ("arbitrary",)),
    )(xs, mods, gvec, w_in, cos, sin)


def _tmpre_bwd(cfg, lat, ctx_terms, du_has_ctx, cos, sin, w_in, xs, mods, gvec, dres, res_with_ctx, name):
    TM, D = cfg.TM, cfg.D
    nt, R = cfg.ntt, cfg.T
    nres = cfg.ntiles(res_with_ctx)
    W = NA_WIDTH
    n_ctx = [len(t) for t in ctx_terms]
    flat_ctx = [a for t in ctx_terms for a in t]
    n_asm = 4 + len(flat_ctx) + 2

    def assemble(refs, o_ref):
        dq_ref, dk_ref, dv_ref, du_ref = refs[:4]
        ctx_refs = refs[4:4 + len(flat_ctx)]
        cos_ref, sin_ref = refs[4 + len(flat_ctx):]
        is_ctx = pl.program_id(0) >= cfg.nxt
        vals, off = [], 0
        for lat_ref, n in zip((dq_ref, dk_ref, dv_ref), n_ctx):
            cv = jnp.zeros((TM, W), f32)
            for r_ in ctx_refs[off:off + n]:
                cv = cv + r_[...]
            off += n
            vals.append(jnp.where(is_ctx, cv, lat_ref[...]))
        du_ = du_ref[...] if du_has_ctx else jnp.where(is_ctx, 0.0, du_ref[...])
        cs = jnp.tile(cos_ref[...], (1, W // LANES))
        sn = jnp.tile(sin_ref[...], (1, W // LANES))
        dq_ = vals[0] * (HEAD_DIM ** -0.5)
        dk_ = vals[1]
        o_ref[:, 0:W] = (dq_ * cs + _swap16(dq_ * sn)).astype(bf16)
        o_ref[:, W:2 * W] = (dk_ * cs + _swap16(dk_ * sn)).astype(bf16)
        o_ref[:, 2 * W:3 * W] = vals[2].astype(bf16)
        o_ref[:, 3 * W:] = du_.astype(bf16)

    def body(*refs):
        w_ref, xs_ref, mods_ref, g_ref, dres_ref, dx_ref, dp_ref, dm_ref, dg_ref = refs[n_asm:]
        i = pl.program_id(0)

        @pl.when(i == 0)
        def _():
            dg_ref[...] = jnp.zeros_like(dg_ref)

        @pl.when((i == 0) | (i == cfg.nxt))
        def _():
            dm_ref[...] = jnp.zeros_like(dm_ref)

        assemble(refs[:n_asm], dp_ref)
        dh = _nt(dp_ref[...], w_ref[...])
        x = xs_ref[...]
        m = mods_ref[0]
        g2 = g_ref[2:3]
        xhat, rinv = _rms_hat(x)
        d_sh = _rsum(dh)
        d_sc = _rsum(dh * (xhat * g2))
        dn = dh * (1.0 + m[4:5])
        dg_ref[...] += _rsum(dn * xhat)
        dx = _rms_bwd(dn * g2, xhat, rinv)
        res = dres_ref[...]
        if nres < nt:
            res = jnp.where(i < nres, res, 0.0)
        dx_ref[...] = res + dx
        dm_ref[0] += jnp.concatenate([d_sh, d_sc], axis=0)

    rt = lambda c: pl.BlockSpec((TM, c), lambda i: (i, 0))
    lat_spec = pl.BlockSpec((TM, W), lambda i: (jnp.minimum(i, cfg.nxt - 1), 0))
    du_spec = rt(POOL_WIDTH) if du_has_ctx else lat_spec
    asm_specs = ([lat_spec, lat_spec, lat_spec, du_spec] + [pl.BlockSpec((TM, W), lambda i: (0, 0))] * len(flat_ctx)
                 + [rt(LANES), rt(LANES)])
    return pl.pallas_call(
        body, name=name, grid=(nt,),
        in_specs=asm_specs + [pl.BlockSpec((D, IN_WIDTH), lambda i: (0, 0)), rt(D),
                              pl.BlockSpec((1, N_MOD, D), _typ(cfg)), pl.BlockSpec((6, D), lambda i: (0, 0)),
                              pl.BlockSpec((TM, D), lambda i: (jnp.minimum(i, nres - 1), 0))],
        out_specs=[rt(D), rt(IN_WIDTH), pl.BlockSpec((1, 2, D), _typ(cfg)), pl.BlockSpec((1, D), lambda i: (0, 0))],
        out_shape=[S_((R, D), f32), S_((R, IN_WIDTH), bf16), S_((2, 2, D), f32), S_((1, D), f32)],
        compiler_params=_cp(VMEM_MID, ("arbitrary",)),
    )(*lat, *flat_ctx, cos, sin, w_in, xs, mods, gvec, dres)


def _na_block(cfg, b):
    return jnp.clip(NA_QR * b - NA_KH // 2, 0, cfg.rows - NA_WR)


def _na_load_bias(b, nb, b_hbm, b_v, sem):
    for typ, at in ((0, 0), (1, 1), (2, nb - 1)):
        @pl.when(b == at)
        def _(typ=typ):
            cp = pltpu.make_async_copy(b_hbm.at[typ], b_v, sem)
            cp.start()
            cp.wait()


def _na_probs(qh, klh, kch, bias):
    s_loc = _nt(qh, klh) + bias
    s_ctx = _nt(qh, kch)
    mx = jnp.maximum(jnp.max(s_loc, axis=-1, keepdims=True), jnp.max(s_ctx, axis=-1, keepdims=True))
    e_loc = jnp.exp(s_loc - mx)
    e_ctx = jnp.exp(s_ctx - mx)
    inv = 1.0 / (jnp.sum(e_loc, axis=-1, keepdims=True) + jnp.sum(e_ctx, axis=-1, keepdims=True))
    return e_loc * inv, e_ctx * inv


def _na_fwd(cfg, q, k, v, bexp, name, comm=None):
    S, L, T = cfg.S, cfg.L, cfg.T
    NQ, NW = NA_QR * GRID_W, NA_WR * GRID_W
    nb = cfg.rows // NA_QR

    def body(q_ref, k_hbm, v_hbm, b_hbm, o_ref, k_v, v_v, b_v, sem):
        b = pl.program_id(0)

        @pl.when(b == 0)
        def _():
            cs = [pltpu.make_async_copy(k_hbm, k_v, sem.at[0]), pltpu.make_async_copy(v_hbm, v_v, sem.at[1])]
            for c_ in cs:
                c_.start()
            for c_ in cs:
                c_.wait()

        _na_load_bias(b, nb, b_hbm, b_v, sem.at[2])
        st = pl.multiple_of(_na_block(cfg, b) * GRID_W, GRID_W)
        first = lax.broadcasted_iota(jnp.int32, (NQ, LANES), 1) < HEAD_DIM
        for hp in range(NA_HEADS // 2):
            ls = slice(hp * LANES, (hp + 1) * LANES)
            q2 = q_ref[:, ls]
            kl, vl = k_v[pl.ds(st, NW), ls], v_v[pl.ds(st, NW), ls]
            kc, vc = k_v[S:T, ls], v_v[S:T, ls]
            o2 = []
            for hh in range(2):
                qm = jnp.where(first if hh == 0 else ~first, q2, jnp.zeros_like(q2))
                p_loc, p_ctx = _na_probs(qm, kl, kc, b_v[2 * hp + hh])
                o2.append(_nn(p_loc.astype(bf16), vl) + _nn(p_ctx.astype(bf16), vc))
            o_ref[:, ls] = jnp.where(first, o2[0], o2[1]).astype(bf16)

    return _call(
        body, (q, k, v, bexp), comm, name=name, grid=(nb,),
        in_specs=[pl.BlockSpec((NQ, NA_WIDTH), lambda b: (b, 0)), ANY, ANY, ANY],
        out_specs=[pl.BlockSpec((NQ, NA_WIDTH), lambda b: (b, 0))],
        out_shape=[S_((S, NA_WIDTH), bf16)],
        scratch_shapes=[pltpu.VMEM((T, NA_WIDTH), bf16), pltpu.VMEM((T, NA_WIDTH), bf16),
                        pltpu.VMEM((NA_HEADS, NQ, NW), f32), pltpu.SemaphoreType.DMA((3,))],
        compiler_params=_cp(VMEM_MID, ("arbitrary",)),
    )


def _na_bwd(cfg, do, q, k, v, bexp, name, comm=None):
    S, L, T, rows = cfg.S, cfg.L, cfg.T, cfg.rows
    NQ, NW = NA_QR * GRID_W, NA_WR * GRID_W
    NSLOT = 2 * NA_KH
    nb = rows // NA_QR
    bmax = (rows - NA_WR) // NA_QR
    steps = 2 * nb - bmax
    W = NA_WIDTH
    assert nb >= 3 and bmax >= 1 and rows - NA_QR * bmax <= NSLOT

    def out_group(g):
        return jnp.where(g >= nb, g - nb + bmax, jnp.clip(g - 1, 0, bmax - 1))

    def body(do_ref, q_ref, k_hbm, v_hbm, b_hbm, dq_ref, dk_ref, dv_ref, dkc_ref, dvc_ref, db_hbm,
             k_v, v_v, b_v, db_v, ak, av, akc, avc, sem):
        g = pl.program_id(0)

        @pl.when(g == 0)
        def _():
            cs = [pltpu.make_async_copy(k_hbm, k_v, sem.at[0]), pltpu.make_async_copy(v_hbm, v_v, sem.at[1])]
            for c_ in cs:
                c_.start()
            db_v[...] = jnp.zeros_like(db_v)
            ak[...] = jnp.zeros_like(ak)
            av[...] = jnp.zeros_like(av)
            akc[...] = jnp.zeros_like(akc)
            avc[...] = jnp.zeros_like(avc)
            for c_ in cs:
                c_.wait()

        for typ, at in ((0, 1), (1, nb - 1)):
            @pl.when(g == at)
            def _(typ=typ):
                cp = pltpu.make_async_copy(db_v, db_hbm.at[typ], sem.at[2])
                cp.start()
                cp.wait()
                db_v[...] = jnp.zeros_like(db_v)

        @pl.when(g < nb)
        def _():
            _na_load_bias(g, nb, b_hbm, b_v, sem.at[2])
            ws = _na_block(cfg, g)
            st = pl.multiple_of(ws * GRID_W, GRID_W)
            first = lax.broadcasted_iota(jnp.int32, (NQ, LANES), 1) < HEAD_DIM
            for hp in range(NA_HEADS // 2):
                ls = slice(hp * LANES, (hp + 1) * LANES)
                q2, do2 = q_ref[:, ls], do_ref[:, ls]
                kl, vl = k_v[pl.ds(st, NW), ls], v_v[pl.ds(st, NW), ls]
                kc, vc = k_v[S:T, ls], v_v[S:T, ls]
                dq2 = []
                dk2 = jnp.zeros((NW, LANES), f32)
                dv2 = jnp.zeros((NW, LANES), f32)
                dkc2 = jnp.zeros((L, LANES), f32)
                dvc2 = jnp.zeros((L, LANES), f32)
                for hh in range(2):
                    keep = first if hh == 0 else ~first
                    qm = jnp.where(keep, q2, jnp.zeros_like(q2))
                    dom = jnp.where(keep, do2, jnp.zeros_like(do2))
                    p_loc, p_ctx = _na_probs(qm, kl, kc, b_v[2 * hp + hh])
                    dp_loc = _nt(dom, vl)
                    dp_ctx = _nt(dom, vc)
                    delta = jnp.sum(p_loc * dp_loc, axis=-1, keepdims=True) + jnp.sum(p_ctx * dp_ctx, axis=-1, keepdims=True)
                    ds_loc = p_loc * (dp_loc - delta)
                    ds_ctx = p_ctx * (dp_ctx - delta)
                    db_v[2 * hp + hh, :, 0:NW] += ds_loc
                    dsl, dsc = ds_loc.astype(bf16), ds_ctx.astype(bf16)
                    dq2.append(_nn(dsl, kl) + _nn(dsc, kc))
                    dk2 = dk2 + _tn(dsl, qm)
                    dv2 = dv2 + _tn(p_loc.astype(bf16), dom)
                    dkc2 = dkc2 + _tn(dsc, qm)
                    dvc2 = dvc2 + _tn(p_ctx.astype(bf16), dom)
                dq_ref[:, ls] = jnp.where(first, dq2[0], dq2[1])
                akc[:, ls] += dkc2
                avc[:, ls] += dvc2
                for kk in range(NA_WR):
                    slot = (ws + kk) % NSLOT
                    ak[slot, :, ls] += dk2[kk * GRID_W:(kk + 1) * GRID_W, :]
                    av[slot, :, ls] += dv2[kk * GRID_W:(kk + 1) * GRID_W, :]

        @pl.when(((g >= 1) & (g <= bmax)) | (g >= nb))
        def _():
            base = NA_QR * (out_group(g) % (NSLOT // NA_QR))
            for t in range(NA_QR):
                dk_ref[t * GRID_W:(t + 1) * GRID_W, :] = ak[base + t]
                dv_ref[t * GRID_W:(t + 1) * GRID_W, :] = av[base + t]
                ak[base + t] = jnp.zeros((GRID_W, W), f32)
                av[base + t] = jnp.zeros((GRID_W, W), f32)

        @pl.when(g == nb - 1)
        def _():
            cp = pltpu.make_async_copy(db_v, db_hbm.at[2], sem.at[2])
            cp.start()
            cp.wait()

        @pl.when(g == steps - 1)
        def _():
            dkc_ref[...] = akc[...]
            dvc_ref[...] = avc[...]

    qmap = lambda g: (jnp.minimum(g, nb - 1), 0)
    kmap = lambda g: (out_group(g), 0)
    full = lambda g: (0, 0)
    return _call(
        body, (do, q, k, v, bexp), comm, name=name, grid=(steps,),
        in_specs=[pl.BlockSpec((NQ, W), qmap), pl.BlockSpec((NQ, W), qmap), ANY, ANY, ANY],
        out_specs=[pl.BlockSpec((NQ, W), qmap), pl.BlockSpec((NQ, W), kmap), pl.BlockSpec((NQ, W), kmap),
                   pl.BlockSpec((L, W), full), pl.BlockSpec((L, W), full), ANY],
        out_shape=[S_((S, W), f32), S_((S, W), f32), S_((S, W), f32), S_((L, W), f32), S_((L, W), f32),
                   S_((NA_TYPES, NA_HEADS, NQ, NA_WPAD), f32)],
        scratch_shapes=[pltpu.VMEM((T, W), bf16), pltpu.VMEM((T, W), bf16),
                        pltpu.VMEM((NA_HEADS, NQ, NW), f32), pltpu.VMEM((NA_HEADS, NQ, NA_WPAD), f32),
                        pltpu.VMEM((NSLOT, GRID_W, W), f32), pltpu.VMEM((NSLOT, GRID_W, W), f32),
                        pltpu.VMEM((L, W), f32), pltpu.VMEM((L, W), f32), pltpu.SemaphoreType.DMA((3,))],
        compiler_params=_cp(VMEM_BIG, ("arbitrary",)),
    )


def _rpb_reduce(dbias, flip, sel, name):
    nq, w = NA_QR * GRID_W, GRID_W

    def diag_body(x_ref, j_ref, o_ref):
        rows = []
        for i in range(NA_QR):
            xr = jnp.dot(j_ref[...], x_ref[i * w:(i + 1) * w, :], preferred_element_type=f32, precision=lax.Precision.HIGHEST)
            rows.append(jnp.sum(pltpu.roll(xr, 0, 1, stride=1, stride_axis=0), axis=0, keepdims=True))
        o_ref[...] = jnp.concatenate(rows + [jnp.zeros((8 - NA_QR, NA_WPAD), f32)], axis=0)

    diag = pl.pallas_call(
        diag_body, name=name + "_diag", grid=(NA_TYPES, NA_HEADS),
        in_specs=[pl.BlockSpec((None, None, nq, NA_WPAD), lambda t, h: (t, h, 0, 0)), pl.BlockSpec((w, w), lambda t, h: (0, 0))],
        out_specs=pl.BlockSpec((None, None, 8, NA_WPAD), lambda t, h: (t, h, 0, 0)),
        out_shape=S_((NA_TYPES, NA_HEADS, 8, NA_WPAD), f32),
        compiler_params=_cp(VMEM_MID, ("arbitrary", "arbitrary")),
    )(dbias, flip)
    lo = w - NA_KW
    y = diag[:, :, :NA_QR, lo:lo + NA_WR * w].reshape(NA_TYPES, NA_HEADS, NA_QR, NA_WR, w)
    y = jnp.transpose(y, (1, 0, 2, 3, 4)).reshape(NA_HEADS, NA_TYPES * NA_QR * NA_WR, w)
    y = jnp.pad(y, ((0, 0), (0, NA_SEL_ROWS - y.shape[1]), (0, LANES - w)))

    def body(y_ref, sel_ref, o_ref):
        o_ref[...] = jnp.dot(sel_ref[...], y_ref[...], preferred_element_type=f32, precision=lax.Precision.HIGHEST)

    return pl.pallas_call(
        body, name=name, grid=(NA_HEADS,),
        in_specs=[pl.BlockSpec((None, NA_SEL_ROWS, LANES), lambda h: (h, 0, 0)), pl.BlockSpec((16, NA_SEL_ROWS), lambda h: (0, 0))],
        out_specs=pl.BlockSpec((None, 16, LANES), lambda h: (h, 0, 0)),
        out_shape=S_((NA_HEADS, 16, LANES), f32),
        compiler_params=_cp(VMEM_MID, ("arbitrary",)),
    )(y, sel)


def _ctx_attn_fwd(cfg, q, k, v, name):
    L = cfg.L
    blk = cfg.S // L

    def body(q_ref, k_ref, v_ref, o_ref):
        qv, kv, vv = q_ref[...], k_ref[...], v_ref[...]
        outs = []
        for h in range(NA_HEADS):
            hs = slice(h * HEAD_DIM, (h + 1) * HEAD_DIM)
            s = _nt(qv[:, hs], kv[:, hs])
            e = jnp.exp(s - jnp.max(s, axis=-1, keepdims=True))
            p = e * (1.0 / jnp.sum(e, axis=-1, keepdims=True))
            outs.append(_nn(p.astype(bf16), vv[:, hs]))
        o_ref[...] = jnp.concatenate(outs, axis=-1).astype(bf16)

    spec = pl.BlockSpec((L, NA_WIDTH), lambda i: (blk, 0))
    return pl.pallas_call(
        body, name=name, grid=(1,), in_specs=[spec, spec, spec],
        out_specs=pl.BlockSpec((L, NA_WIDTH), lambda i: (0, 0)), out_shape=S_((L, NA_WIDTH), bf16),
        compiler_params=_cp(VMEM_MID, ("arbitrary",)),
    )(q, k, v)


def _ctx_attn_bwd(cfg, do, q, k, v, name):
    L = cfg.L
    blk = cfg.S // L

    def body(do_ref, q_ref, k_ref, v_ref, dq_ref, dk_ref, dv_ref):
        dov, qv, kv, vv = do_ref[...], q_ref[...], k_ref[...], v_ref[...]
        dqs, dks, dvs = [], [], []
        for h in range(NA_HEADS):
            hs = slice(h * HEAD_DIM, (h + 1) * HEAD_DIM)
            qh, kh, doh = qv[:, hs], kv[:, hs], dov[:, hs]
            s = _nt(qh, kh)
            e = jnp.exp(s - jnp.max(s, axis=-1, keepdims=True))
            p = e * (1.0 / jnp.sum(e, axis=-1, keepdims=True))
            dp = _nt(doh, vv[:, hs])
            ds = (p * (dp - jnp.sum(p * dp, axis=-1, keepdims=True))).astype(bf16)
            dqs.append(_nn(ds, kh))
            dks.append(_tn(ds, qh))
            dvs.append(_tn(p.astype(bf16), doh))
        dq_ref[...] = jnp.concatenate(dqs, axis=-1)
        dk_ref[...] = jnp.concatenate(dks, axis=-1)
        dv_ref[...] = jnp.concatenate(dvs, axis=-1)

    spec = pl.BlockSpec((L, NA_WIDTH), lambda i: (blk, 0))
    ospec = pl.BlockSpec((L, NA_WIDTH), lambda i: (0, 0))
    return pl.pallas_call(
        body, name=name, grid=(1,), in_specs=[spec, spec, spec, spec],
        out_specs=[ospec, ospec, ospec], out_shape=[S_((L, NA_WIDTH), f32)] * 3,
        compiler_params=_cp(VMEM_MID, ("arbitrary",)),
    )(do, q, k, v)


def _pool_centered(u, band, inv):
    return _split_sum(_nn, band, u) * inv - u


def _split_sum(mm, band, t):
    hi = t.astype(bf16)
    lo = (t - hi.astype(f32)).astype(bf16)
    s = mm(band, jnp.concatenate([hi, lo], axis=1))
    n = t.shape[1]
    return s[:, :n] + s[:, n:]


def _pool_mix(u_ref, band_ref, inv_ref, w_ref, ps_ref):
    C = POOL_CH
    outs = []
    for g in range(POOL_GROUPS):
        d = _pool_centered(u_ref[:, g * C:(g + 1) * C], band_ref[0, g], inv_ref[0, g])
        outs.append(_nn(d.astype(bf16), w_ref[g].astype(bf16)) * ps_ref[:, g * C:(g + 1) * C])
    return jnp.concatenate(outs, axis=-1).astype(bf16)


def _pool_bwd(cfg, dmix, u, band, inv, w_pool, pool_scale, with_ctx, name):
    TM = cfg.TM
    nt = cfg.ntiles(with_ctx)
    C = POOL_CH

    def body(dy_ref, u_ref, band_ref, inv_ref, w_ref, ps_ref, du_ref, dw_ref, dps_ref):
        @pl.when(pl.program_id(0) == 0)
        def _():
            dw_ref[...] = jnp.zeros_like(dw_ref)
            dps_ref[...] = jnp.zeros_like(dps_ref)

        dus, dpss = [], []
        for g in range(POOL_GROUPS):
            gs = slice(g * C, (g + 1) * C)
            band_g, inv_g = band_ref[0, g], inv_ref[0, g]
            db = _pool_centered(u_ref[:, gs], band_g, inv_g).astype(bf16)
            wb = w_ref[g].astype(bf16)
            dy = dy_ref[:, gs].astype(f32)
            dpss.append(_rsum(dy * _nn(db, wb)))
            dys = (dy * ps_ref[:, gs]).astype(bf16)
            dw_ref[g] += _tn(db, dys)
            dd = _nt(dys, wb)
            dus.append(_split_sum(_tn, band_g, dd * inv_g) - dd)
        du_ref[...] = jnp.concatenate(dus, axis=-1)
        dps_ref[...] += jnp.concatenate(dpss, axis=-1)

    typ4 = lambda i: (jnp.minimum(i // cfg.nxt, 1), 0, 0, 0)
    return pl.pallas_call(
        body, name=name, grid=(nt,),
        in_specs=[pl.BlockSpec((TM, POOL_WIDTH), lambda i: (i, 1)), pl.BlockSpec((TM, POOL_WIDTH), lambda i: (i, 0)),
                  pl.BlockSpec((1, POOL_GROUPS, TM, TM), typ4), pl.BlockSpec((1, POOL_GROUPS, TM, 1), typ4),
                  pl.BlockSpec((POOL_GROUPS, C, C), lambda i: (0, 0, 0)), pl.BlockSpec((1, POOL_WIDTH), lambda i: (0, 0))],
        out_specs=[pl.BlockSpec((TM, POOL_WIDTH), lambda i: (i, 0)), pl.BlockSpec((POOL_GROUPS, C, C), lambda i: (0, 0, 0)),
                   pl.BlockSpec((1, POOL_WIDTH), lambda i: (0, 0))],
        out_shape=[S_((nt * TM, POOL_WIDTH), f32), S_((POOL_GROUPS, C, C), f32), S_((1, POOL_WIDTH), f32)],
        compiler_params=_cp(VMEM_MID, ("arbitrary",)),
    )(dmix, u, band, inv, w_pool, pool_scale)


def _tmpost_fwd(cfg, na_x, na_c, u, band, inv, w_pool, pool_scale, w_out, xs, mods, gvec, name):
    TM, D = cfg.TM, cfg.D
    with_ctx = na_c is not None
    nt = cfg.ntiles(with_ctx)
    R = nt * TM

    def body(*refs):
        if with_ctx:
            nax_ref, nac_ref = refs[:2]
            na = jnp.where(pl.program_id(0) < cfg.nxt, nax_ref[...], nac_ref[...])
        else:
            na = refs[0][...]
        (u_ref, band_ref, inv_ref, wp_ref, ps_ref, w_ref, xs_ref, mods_ref, g_ref,
         out_ref, opre_ref, mix_ref) = refs[2 if with_ctx else 1:]
        pool_v = _pool_mix(u_ref, band_ref, inv_ref, wp_ref, ps_ref)
        mix_ref[:, 0:NA_WIDTH] = na
        mix_ref[:, NA_WIDTH:] = pool_v
        o = _nn(na, w_ref[0:NA_WIDTH, :]) + _nn(pool_v, w_ref[NA_WIDTH:, :])
        opre_ref[...] = o
        ohat, _ = _rms_hat(o)
        out_ref[...] = xs_ref[...] + mods_ref[0][5:6] * (ohat * g_ref[3:4])

    rt = lambda c: pl.BlockSpec((TM, c), lambda i: (i, 0))
    na_specs = [pl.BlockSpec((TM, NA_WIDTH), lambda i: (jnp.minimum(i, cfg.nxt - 1), 0))]
    na_args = [na_x]
    if with_ctx:
        na_specs.append(pl.BlockSpec((TM, NA_WIDTH), lambda i: (0, 0)))
        na_args.append(na_c)
    typ4 = lambda i: (jnp.minimum(i // cfg.nxt, 1), 0, 0, 0)
    pool_specs = [rt(POOL_WIDTH), pl.BlockSpec((1, POOL_GROUPS, TM, TM), typ4), pl.BlockSpec((1, POOL_GROUPS, TM, 1), typ4),
                  pl.BlockSpec((POOL_GROUPS, POOL_CH, POOL_CH), lambda i: (0, 0, 0)), pl.BlockSpec((1, POOL_WIDTH), lambda i: (0, 0))]
    return pl.pallas_call(
        body, name=name, grid=(nt,),
        in_specs=na_specs + pool_specs + [pl.BlockSpec((MIX_WIDTH, D), lambda i: (0, 0)), rt(D),
                                          pl.BlockSpec((1, N_MOD, D), _typ(cfg)), pl.BlockSpec((6, D), lambda i: (0, 0))],
        out_specs=[rt(D), rt(D), rt(MIX_WIDTH)],
        out_shape=[S_((R, D), f32), S_((R, D), f32), S_((R, MIX_WIDTH), bf16)],
        compiler_params=_cp(VMEM_MID, ("arbitrary",)),
    )(*na_args, u, band, inv, w_pool, pool_scale, w_out, xs, mods, gvec)


def _tmpost_bwd(cfg, dout, opre, w_out, mods, gvec, with_ctx, name):
    TM, D = cfg.TM, cfg.D
    nt = cfg.ntiles(with_ctx)
    R = nt * TM
    ntyp = 2 if with_ctx else 1

    def body(do_ref, opre_ref, w_ref, mods_ref, g_ref, dop_ref, dmix_ref, dm_ref, dg_ref):
        i = pl.program_id(0)

        @pl.when(i == 0)
        def _():
            dg_ref[...] = jnp.zeros_like(dg_ref)

        @pl.when((i == 0) | (i == cfg.nxt))
        def _():
            dm_ref[...] = jnp.zeros_like(dm_ref)

        do = do_ref[...]
        g3 = g_ref[3:4]
        ohat, rinv = _rms_hat(opre_ref[...])
        dm_ref[0] += _rsum(do * (ohat * g3))
        dr = mods_ref[0][5:6] * do
        dg_ref[...] += _rsum(dr * ohat)
        dob = _rms_bwd(dr * g3, ohat, rinv).astype(bf16)
        dop_ref[...] = dob
        dmix_ref[...] = _nt(dob, w_ref[...]).astype(bf16)

    rt = lambda c: pl.BlockSpec((TM, c), lambda i: (i, 0))
    return pl.pallas_call(
        body, name=name, grid=(nt,),
        in_specs=[rt(D), rt(D), pl.BlockSpec((MIX_WIDTH, D), lambda i: (0, 0)),
                  pl.BlockSpec((1, N_MOD, D), _typ(cfg)), pl.BlockSpec((6, D), lambda i: (0, 0))],
        out_specs=[rt(D), rt(MIX_WIDTH), pl.BlockSpec((1, 1, D), _typ(cfg)), pl.BlockSpec((1, D), lambda i: (0, 0))],
        out_shape=[S_((R, D), bf16), S_((R, MIX_WIDTH), bf16), S_((ntyp, 1, D), f32), S_((1, D), f32)],
        compiler_params=_cp(VMEM_MID, ("arbitrary",)),
    )(dout, opre, w_out, mods, gvec)


def _modvec_fwd(cvecs, w_mod, b_shard, name):
    nl, D, n = w_mod.shape
    tn = n // 3 if (n % 3 == 0 and (n // 3) % LANES == 0) else n

    def body(c_ref, w_ref, b_ref, o_ref, s_ref):
        cv = c_ref[...]
        sv = cv * _sigmoid(cv)
        s_ref[...] = sv
        o_ref[...] = _nn(sv.astype(bf16), w_ref[...].astype(bf16)) + b_ref[...]

    return pl.pallas_call(
        body, name=name, grid=(nl, n // tn),
        in_specs=[pl.BlockSpec((16, D), lambda l, j: (0, 0)), pl.BlockSpec((None, D, tn), lambda l, j: (l, 0, j)),
                  pl.BlockSpec((None, 1, tn), lambda l, j: (l, 0, j))],
        out_specs=[pl.BlockSpec((None, 16, tn), lambda l, j: (l, 0, j)), pl.BlockSpec((16, D), lambda l, j: (0, 0))],
        out_shape=[S_((nl, 16, n), f32), S_((16, D), f32)],
        compiler_params=_cp(VMEM_MID, ("arbitrary", "arbitrary")),
    )(cvecs, w_mod, b_shard)


def _modvec_bwd(s_t, dm, w_mod, name):
    nl, D, n = w_mod.shape
    tn = n // 3 if (n % 3 == 0 and (n // 3) % LANES == 0) else n

    def body(s_ref, dm_ref, w_ref, gw_ref, gc_ref):
        @pl.when(pl.program_id(1) == 0)
        def _():
            gc_ref[...] = jnp.zeros_like(gc_ref)
        dmv = dm_ref[...]
        gw_ref[...] = jnp.dot(s_ref[...], dmv, preferred_element_type=f32, precision=lax.Precision.HIGHEST)
        gc_ref[...] += _nt(dmv[8:16].astype(bf16), w_ref[...].astype(bf16))

    return pl.pallas_call(
        body, name=name, grid=(nl, n // tn),
        in_specs=[pl.BlockSpec((D, 16), lambda l, j: (0, 0)), pl.BlockSpec((None, 16, tn), lambda l, j: (l, 0, j)),
                  pl.BlockSpec((None, D, tn), lambda l, j: (l, 0, j))],
        out_specs=[pl.BlockSpec((None, D, tn), lambda l, j: (l, 0, j)), pl.BlockSpec((None, 8, D), lambda l, j: (l, 0, 0))],
        out_shape=[S_((nl, D, n), f32), S_((nl, 8, D), f32)],
        compiler_params=_cp(VMEM_MID, ("arbitrary", "arbitrary")),
    )(s_t, dm, w_mod)


def _as2d(a):
    n = a.size
    if a.ndim >= 2 and a.shape[-1] % LANES == 0:
        return a.reshape(-1, a.shape[-1])
    if n % LANES == 0:
        return a.reshape(-1, LANES)
    return a.reshape(-1, a.shape[-1]) if a.ndim >= 2 else a.reshape(1, n)


def _row_tile(r, c, budget_elems=512 * 1024):
    if r * c <= budget_elems or r % 8 != 0:
        return r
    t = r
    while t * c > budget_elems and t % 16 == 0:
        t //= 2
    return t


def _div_tile(r, c, budget_elems, mult=16):
    best = None
    for t in range(mult, r + 1, mult):
        if r % t == 0 and t * c <= budget_elems:
            best = t
    return best if best is not None else r


def _chip_index():
    return 2 * lax.axis_index("x") + lax.axis_index("y")


def _cast_into_place(shards, lead, axis, name):
    r, c = shards.shape[-2:]
    tr = _div_tile(r, c, 768 * 1024)
    nr = r // tr
    out_map = (lambda i: (i, _chip_index())) if axis == 1 else (lambda i: (_chip_index() * nr + i, 0))
    full2 = (r, c * N_CHIPS) if axis == 1 else (r * N_CHIPS, c)

    def body(a_ref, o_ref):
        o_ref[...] = a_ref[...].astype(bf16)

    return pl.pallas_call(
        body, name=name, grid=(nr,),
        in_specs=[pl.BlockSpec((None,) * len(lead) + (tr, c), lambda i: tuple(lead) + (i, 0))],
        out_specs=pl.BlockSpec((tr, c), out_map),
        out_shape=S_(full2, bf16), compiler_params=_cp(VMEM_MID, ("arbitrary",)),
    )(shards)


def _sum_devices8(own, land, axis, into, lead, name):
    _, rh, cs = land.shape
    tr = _div_tile(rh, cs, 400 * 1024)
    nr = rh // tr
    core = lambda: lax.axis_index("c")
    if axis == 1:
        own_map = lambda i: (core() * nr + i, _chip_index())
    else:
        own_map = lambda i: (_chip_index() * 2 * nr + core() * nr + i, 0)
    nl = len(lead)

    def land_spec(j):
        return pl.BlockSpec((None, tr, cs), lambda i: ((2 * _chip_index() + core() + j) % N_DEV, i, 0))

    def body(own_ref, *rest):
        acc = own_ref[...]
        for p_ref in rest[:N_DEV - 1]:
            acc = acc + p_ref[...].astype(f32)
        rest[-1][...] = acc

    return pl.pallas_call(
        body, name=name, grid=(nr,),
        in_specs=[pl.BlockSpec((tr, cs), own_map)] + [land_spec(j) for j in range(1, N_DEV)] + [ANY],
        out_specs=pl.BlockSpec((None,) * nl + (tr, cs), lambda i: tuple(lead) + (core() * nr + i, 0)),
        out_shape=S_(into.shape, f32), input_output_aliases={N_DEV: 0},
        compiler_params=_cp(VMEM_MID, ("arbitrary",)),
    )(own, *([land] * (N_DEV - 1)), into)


def _adamw(w, g, m, v, name, emit_grad=False):
    shape = w.shape
    w2, g2, m2, v2 = _as2d(w), _as2d(g), _as2d(m), _as2d(v)
    r, c = w2.shape
    tr = _row_tile(r, c, 256 * 1024)
    c1 = 1.0 - ADAM_B1 ** ADAM_STEP
    c2 = 1.0 - ADAM_B2 ** ADAM_STEP
    n_out = 4 if emit_grad else 3

    def body(w_ref, g_ref, m_ref, v_ref, d_ref, mo_ref, vo_ref, *go_ref):
        gv = g_ref[...]
        mn = ADAM_B1 * m_ref[...] + (1.0 - ADAM_B1) * gv
        vn = ADAM_B2 * v_ref[...] + (1.0 - ADAM_B2) * (gv * gv)
        mo_ref[...] = mn
        vo_ref[...] = vn
        d_ref[...] = -ADAM_LR * ((mn / c1) / (jnp.sqrt(vn / c2) + ADAM_EPS) + ADAM_WD * w_ref[...])
        if emit_grad:
            go_ref[0][...] = gv

    spec = pl.BlockSpec((tr, c), lambda i: (i, 0))
    outs = pl.pallas_call(body, name=name, grid=(r // tr,), in_specs=[spec] * 4, out_specs=[spec] * n_out,
                          out_shape=[S_((r, c), f32)] * n_out, compiler_params=_cp(VMEM_MID, ("arbitrary",)))(w2, g2, m2, v2)
    return tuple(o.reshape(shape) for o in outs)


def _sum_devices(gathered, name):
    _, r, c = gathered.shape

    def body(a_ref, o_ref):
        acc = a_ref[0]
        for j in range(1, N_DEV):
            acc = acc + a_ref[j]
        o_ref[...] = acc

    tr = _row_tile(r, c, 64 * 1024)
    return pl.pallas_call(
        body, name=name, grid=(r // tr,),
        in_specs=[pl.BlockSpec((N_DEV, tr, c), lambda i: (0, i, 0))], out_specs=pl.BlockSpec((tr, c), lambda i: (i, 0)),
        out_shape=S_((r, c), f32), compiler_params=_cp(VMEM_MID, ("arbitrary",)))(gathered)


def _all_gather_small(block, name):
    m_per, n = block.shape

    def body(x_ref, out_ref, send_sems, recv_sems, local_sem):
        x, y, c = _mesh_pos()
        me, sibling = (x, y, c), (x, y, 1 - c)
        chips = [(1 - x, y), (x, 1 - y), (1 - x, 1 - y)]

        def rows(px, py, pc):
            return out_ref.at[pl.ds((4 * px + 2 * py + pc) * m_per, m_per), :]

        def copy(k, blk, to, src=None):
            return pltpu.make_async_remote_copy(
                src_ref=rows(*blk) if src is None else src, dst_ref=rows(*blk),
                send_sem=send_sems.at[k], recv_sem=recv_sems.at[k], device_id=to, device_id_type=MESH)

        mine = pltpu.make_async_copy(x_ref, rows(*me), local_sem)
        mine.start()
        first = [copy(0, me, sibling, src=x_ref)]
        first += [copy(1 + j, me, (*chip, c), src=x_ref) for j, chip in enumerate(chips)]
        for cp in first:
            cp.start()
        passed = [copy(4 + j, (*chip, c), sibling) for j, chip in enumerate(chips)]
        for j, chip in enumerate(chips):
            copy(1 + j, (*chip, c), me).wait_recv()
            passed[j].start()
        copy(0, sibling, me).wait_recv()
        for j, chip in enumerate(chips):
            copy(4 + j, (*chip, 1 - c), me).wait_recv()
        for cp in first + passed:
            cp.wait_send()
        mine.wait()

    return pl.pallas_call(
        body, name=name, out_shape=S_((N_DEV * m_per, n), block.dtype),
        in_specs=[pl.BlockSpec(memory_space=pltpu.VMEM)], out_specs=pl.BlockSpec(memory_space=pltpu.VMEM),
        scratch_shapes=[pltpu.SemaphoreType.DMA((7,)), pltpu.SemaphoreType.DMA((7,)), pltpu.SemaphoreType.DMA],
        compiler_params=_cp(VMEM_MID),
    )(block)


def _pack_rows(arrays):
    flat = jnp.concatenate([a.reshape(-1) for a in arrays])
    pad = (-flat.size) % (8 * LANES)
    return jnp.pad(flat, (0, pad)).reshape(-1, LANES)


def _unpack_rows(packed, shapes):
    flat = packed.reshape(-1)
    out, off = [], 0
    for s in shapes:
        n = int(np.prod(s))
        out.append(flat[off:off + n].reshape(s))
        off += n
    return out


W_AXIS = {"gu": 1, "dn": 0, "wi": 1, "wo": 0}


def _half_merge(bufs, name):
    nt = len(bufs)

    def body(*refs):
        outs = refs[nt:2 * nt]
        send_sems, recv_sems = refs[2 * nt:]
        x, y, c = _mesh_pos()

        def half(ref, h):
            rh = ref.shape[-2] // 2
            return ref.at[(slice(None),) * (len(ref.shape) - 2) + (pl.ds(h * rh, rh), slice(None))]

        cps = []
        for t in range(nt):
            cp = pltpu.make_async_remote_copy(
                src_ref=half(outs[t], c), dst_ref=half(outs[t], c), send_sem=send_sems.at[t], recv_sem=recv_sems.at[t],
                device_id=(x, y, 1 - c), device_id_type=MESH)
            cp.start()
            cps.append(cp)
        for t in range(nt):
            pltpu.make_async_remote_copy(
                src_ref=half(outs[t], 1 - c), dst_ref=half(outs[t], 1 - c), send_sem=send_sems.at[t], recv_sem=recv_sems.at[t],
                device_id=(x, y, 1 - c), device_id_type=MESH).wait_recv()
        for cp in cps:
            cp.wait_send()

    return pl.pallas_call(
        body, name=name, in_specs=[ANY] * nt, out_specs=[ANY] * nt, out_shape=[S_(b.shape, f32) for b in bufs],
        input_output_aliases={t: t for t in range(nt)},
        scratch_shapes=[pltpu.SemaphoreType.DMA((nt,)), pltpu.SemaphoreType.DMA((nt,))],
        compiler_params=_cp(VMEM_MID),
    )(*bufs)


def _local_step(cfg, x_lat, x_ctx, target, mods, norm_g, W, G, na_rpb, w_pool, pool_scale):
    S, L, T, D, F = cfg.S, cfg.L, cfg.T, cfg.D, cfg.F
    depth = norm_g.shape[0]
    cos, sin = _rope_tables(S, L)
    band, inv = _pool_tables(cfg.TM, L)
    flip, sel = _rpb_reduce_tables()

    assert depth == 2, "the carrier schedules below are written for two layers"
    fwd_carry = {"ffn_fwd_0_0": [("wi", 0), ("wo", 0), ("gu", 0, 1), ("dn", 0, 1)],
                 "na_fwd_0": [("gu", 1, 0), ("dn", 1, 0)],
                 "ffn_fwd_0_1": [("wi", 1), ("wo", 1), ("gu", 1, 1), ("dn", 1, 1)]}
    bwd_carry = {"na_bwd_1": [("gu", 1, 1), ("dn", 1, 1)], "ffn_bwd_1_0": [("wi", 1), ("wo", 1)],
                 "ffn_bwd_0_1": [("gu", 1, 0), ("dn", 1, 0)], "na_bwd_0": [("gu", 0, 1), ("dn", 0, 1)],
                 "ffn_bwd_0_0": [("wi", 0), ("wo", 0)], "wgrad_dn_0_0": [("gu", 0, 0)]}
    last_scatter = [("dn", 0, 0)]
    tag = lambda key: "_".join(str(p) for p in key)
    g_f32, g_b16 = {}, {}

    def gather_on(name):
        keys = fwd_carry.get(name)
        return None if keys is None else _gather_comm([W[k_] for k_ in keys], [W_AXIS[k_[0]] for k_ in keys])

    def gathered(name, res):
        if name in fwd_carry:
            W.update(zip(fwd_carry[name], res))

    def scatter_on(name):
        keys = bwd_carry.get(name)
        return None if keys is None else _scatter_comm([g_b16[k_] for k_ in keys], [W_AXIS[k_[0]] for k_ in keys])

    def scattered(keys, lands):
        for key, land in zip(keys, lands):
            G[key[0]] = _sum_devices8(g_f32[key], land, W_AXIS[key[0]], G[key[0]], key[1:], f"sum8_{tag(key)}")

    def wgrad(key, a, b, rows):
        name = f"wgrad_{tag(key)}"
        (g_f32[key], g_b16[key]), lands = _wgrad(a, b, rows, name, scatter_on(name))
        scattered(bwd_carry.get(name, ()), lands)

    saved = []
    xs, xs_ctx = x_lat, x_ctx
    for l in range(depth):
        last = l == depth - 1
        wc = not last
        gvec = norm_g[l]
        ps = pool_scale[l].reshape(1, POOL_WIDTH)
        bexp = _expand_rpb(na_rpb[l], f"bias_expand_{l}")
        name = f"ffn_fwd_{l}_0"
        (xs1, hb1, z1, y1), res = _ffn_fwd(cfg, xs, mods[l], gvec, W["gu", l, 0], W["dn", l, 0], 0, 0, True, name,
                                           gather_on(name), xs_ctx=xs_ctx)
        gathered(name, res)
        hb2, q, k, v, u = _tmpre_fwd(cfg, xs1, mods[l], gvec, W["wi", l], cos, sin, f"tmpre_fwd_{l}")
        name = f"na_fwd_{l}"
        (na_x,), res = _na_fwd(cfg, q, k, v, bexp, name, gather_on(name))
        gathered(name, res)
        na_c = _ctx_attn_fwd(cfg, q, k, v, f"ctx_attn_fwd_{l}") if wc else None
        xs2, opre, mix = _tmpost_fwd(cfg, na_x, na_c, u, band, inv, w_pool[l], ps, W["wo", l], xs1, mods[l], gvec,
                                     f"tmpost_fwd_{l}")
        name = f"ffn_fwd_{l}_1"
        outs, res = _ffn_fwd(cfg, xs2, mods[l], gvec, W["gu", l, 1], W["dn", l, 1], 6, 4, wc, name, gather_on(name),
                             loss_target=target if last else None)
        xs3, hb3, z3, y3 = outs[:4]
        gathered(name, res)
        saved.append(dict(xs=xs, xs_ctx=xs_ctx, xs1=xs1, xs2=xs2, hb1=hb1, z1=z1, y1=y1, hb2=hb2, q=q, k=k, v=v, u=u, mix=mix,
                          opre=opre, hb3=hb3, z3=z3, y3=y3, bexp=bexp, ps=ps, gvec=gvec))
        xs, xs_ctx = xs3, None

    dxs, loss_blk = xs, outs[4]

    small = [None] * depth
    for l in reversed(range(depth)):
        last = l == depth - 1
        wc = not last
        sv = saved[l]
        gvec = sv["gvec"]
        rows_b = cfg.T if wc else cfg.S
        name = f"ffn_bwd_{l}_1"
        (dxs2, dz, dyb, ab, dm678, dg45), lands = _ffn_bwd(cfg, dxs, sv["xs2"], sv["z3"], sv["y3"], mods[l], gvec,
                                                           W["gu", l, 1], W["dn", l, 1], 6, 4, wc, name, scatter_on(name))
        scattered(bwd_carry.get(name, ()), lands)
        wgrad(("gu", l, 1), sv["hb3"], dz, rows_b)
        wgrad(("dn", l, 1), ab, dyb, rows_b)
        dop, dmix, dm5, dg3 = _tmpost_bwd(cfg, dxs2, sv["opre"], W["wo", l], mods[l], gvec, wc, f"tmpost_bwd_{l}")
        wgrad(("wo", l), sv["mix"], dop, rows_b)
        du, dwp, dps = _pool_bwd(cfg, dmix, sv["u"], band, inv, w_pool[l], sv["ps"], wc, f"pool_bwd_{l}")
        name = f"na_bwd_{l}"
        (dq, dk, dv, dkc, dvc, dbexp), lands = _na_bwd(cfg, dmix, sv["q"], sv["k"], sv["v"], sv["bexp"], name, scatter_on(name))
        scattered(bwd_carry.get(name, ()), lands)
        drpb = _rpb_reduce(dbexp, flip, sel, f"rpb_reduce_{l}")
        if wc:
            dqc, dkc2, dvc2 = _ctx_attn_bwd(cfg, dmix, sv["q"], sv["k"], sv["v"], f"ctx_attn_bwd_{l}")
            ctx_terms = ([dqc], [dkc, dkc2], [dvc, dvc2])
        else:
            ctx_terms = ([], [dkc], [dvc])
        dxs1, dproj, dm34, dg2 = _tmpre_bwd(cfg, (dq, dk, dv, du), ctx_terms, wc, cos, sin, W["wi", l], sv["xs1"], mods[l], gvec,
                                            dxs2, wc, f"tmpre_bwd_{l}")
        wgrad(("wi", l), sv["hb2"], dproj, cfg.T)
        name = f"ffn_bwd_{l}_0"
        (dxs, dz, dyb, ab, dm012, dg01), lands = _ffn_bwd(cfg, dxs1, sv["xs"], sv["z1"], sv["y1"], mods[l], gvec,
                                                          W["gu", l, 0], W["dn", l, 0], 0, 0, True, name, scatter_on(name),
                                                          xs_ctx=sv["xs_ctx"])
        scattered(bwd_carry.get(name, ()), lands)
        wgrad(("gu", l, 0), sv["hb1"], dz, cfg.T)
        wgrad(("dn", l, 0), ab, dyb, cfg.T)
        if not wc:
            zero = lambda a: jnp.concatenate([a, jnp.zeros_like(a)], axis=0)
            dm5, dm678 = zero(dm5), zero(dm678)
        dmods = jnp.concatenate([dm012, dm34, dm5, dm678], axis=1)
        dgs = jnp.concatenate([dg01, dg2, dg3, dg45], axis=0)
        small[l] = dict(dmods=dmods, dg=dgs, drpb=drpb, dwp=dwp, dps=dps)
    lands = _comm_only(_scatter_comm([g_b16[k_] for k_ in last_scatter], [W_AXIS[k_[0]] for k_ in last_scatter]), "scatter_last")
    scattered(last_scatter, lands)
    kinds = ("gu", "dn", "wi", "wo")
    merged = _half_merge([G[k_] for k_ in kinds], "merge_halves")
    return loss_blk, dxs, dict(zip(kinds, merged)), small


def kernel(x, c, ctx, c_ctx, w_mod, b_mod, norm_g, w_ffn_gate_up, w_ffn_down, w_in, w_out, na_rpb, w_pool, pool_scale, loss_target, m_c_ctx, m_w_mod, m_b_mod, m_norm_g, m_w_ffn_gate_up, m_w_ffn_down, m_w_in, m_w_out, m_na_rpb, m_w_pool, m_pool_scale, v_c_ctx, v_w_mod, v_b_mod, v_norm_g, v_w_ffn_gate_up, v_w_ffn_down, v_w_in, v_w_out, v_na_rpb, v_w_pool, v_pool_scale):
    S, D = x.shape[1], x.shape[2]
    L = ctx.shape[1]
    depth = w_mod.shape[0]
    F = w_ffn_down.shape[2] * N_CHIPS
    nmod = w_mod.shape[2]
    gsh = norm_g.shape[2]
    cfg = _Cfg(S, L, D, F)
    mx, my, mc = _mesh_pos()
    chip = 2 * mx + my
    dev = 4 * mx + 2 * my + mc

    W = {}
    for l in range(depth):
        for i in range(2):
            W["gu", l, i] = _cast_into_place(w_ffn_gate_up, (l, i), W_AXIS["gu"], f"cast_gu_{l}_{i}")
            W["dn", l, i] = _cast_into_place(w_ffn_down, (l, i), W_AXIS["dn"], f"cast_dn_{l}_{i}")
        W["wi", l] = _cast_into_place(w_in, (l,), W_AXIS["wi"], f"cast_wi_{l}")
        W["wo", l] = _cast_into_place(w_out, (l,), W_AXIS["wo"], f"cast_wo_{l}")
    first = [("gu", 0, 0), ("dn", 0, 0)]
    W.update(zip(first, _comm_only(_gather_comm([W[k_] for k_ in first], [W_AXIS[k_[0]] for k_ in first]), "gather_first")))
    G = {"gu": lax.empty(w_ffn_gate_up.shape, f32), "dn": lax.empty(w_ffn_down.shape, f32),
         "wi": lax.empty(w_in.shape, f32), "wo": lax.empty(w_out.shape, f32)}

    c_all = _all_gather_small(jnp.pad(c, ((0, 7), (0, 0))), "gather_c").reshape(N_DEV, 8, D)[:, 0]
    cvecs = jnp.concatenate([c_all, c_ctx[None], jnp.zeros((7, D), f32)], axis=0)
    b_shard = lax.dynamic_slice_in_dim(b_mod, chip * nmod, nmod, axis=1).reshape(depth, 1, nmod)
    m_part, silu_c = _modvec_fwd(cvecs, w_mod, b_shard, "modvec_fwd")
    m_all = _all_gather_small(m_part.reshape(depth * 16, nmod), "gather_mod").reshape(N_DEV, depth, 16, nmod)
    m_full = jnp.concatenate([m_all[2 * j] for j in range(N_CHIPS)], axis=-1)
    m_mine = lax.dynamic_index_in_dim(m_full, dev, axis=1, keepdims=False)
    mods = jnp.stack([m_mine, m_full[:, 8]], axis=1).reshape(depth, 2, N_MOD, D)

    norm_g_full = _all_gather_small(_pack_rows([norm_g]), "gather_norm_g")
    rows_g = norm_g_full.shape[0] // N_DEV
    ng = norm_g_full.reshape(N_DEV, rows_g * LANES)[:, :norm_g.size].reshape(N_DEV, depth, 6, gsh)
    norm_g_all = jnp.concatenate([ng[2 * j] for j in range(N_CHIPS)], axis=-1)
    loss_blk, dx_lat, wgrads, small = _local_step(cfg, x[0], ctx[0], loss_target[0], mods, norm_g_all, W, G,
                                                  na_rpb, w_pool, pool_scale)
    loss = lax.psum(loss_blk[0, 0], ("x", "y", "c"))
    grad_x = dx_lat[None]

    g_gu, g_dn, g_wi, g_wo = wgrads["gu"], wgrads["dn"], wgrads["wi"], wgrads["wo"]
    names = ("dmods", "dg", "drpb", "dwp", "dps")
    parts = [jnp.stack([small[l][n] for l in range(depth)]) for n in names]
    shapes = [p.shape for p in parts]
    packed = _pack_rows(parts)
    gathered = _all_gather_small(packed, "gather_small").reshape(N_DEV, packed.shape[0], LANES)
    total = _unpack_rows(_sum_devices(gathered, "sum_small"), shapes)
    dmods_sum, dg_sum, drpb_sum, dwp_sum, dps_sum = total
    dmods_each = jnp.stack([_unpack_rows(gathered[j], shapes[:1])[0] for j in range(N_DEV)])
    dm_rows = jnp.concatenate([jnp.transpose(dmods_each[:, :, 0], (1, 0, 2, 3)).reshape(depth, N_DEV, N_MOD * D),
                               dmods_sum[:, 1].reshape(depth, 1, N_MOD * D),
                               jnp.zeros((depth, 7, N_MOD * D), f32)], axis=1)
    dm_shard = lax.dynamic_slice_in_dim(dm_rows, chip * nmod, nmod, axis=2)
    grad_w_mod, gc_part = _modvec_bwd(silu_c.T, dm_shard, w_mod, "modvec_bwd")
    gc_all = _all_gather_small(gc_part.reshape(depth * 8, D), "gather_gc").reshape(N_DEV, depth, 8, D)
    grad_b_mod, grad_c_ctx = _small_finish(dm_rows, gc_all, c_ctx)
    grad_norm_g = lax.dynamic_slice_in_dim(dg_sum, chip * gsh, gsh, axis=2)
    grad_na_rpb = drpb_sum[:, :, :2 * NA_KH - 1, :2 * NA_KW - 1]
    grad_w_pool = dwp_sum
    grad_pool_scale = dps_sum.reshape(depth, POOL_WIDTH)

    grads = [grad_c_ctx, grad_w_mod, grad_b_mod, grad_norm_g, g_gu, g_dn, g_wi, g_wo, grad_na_rpb, grad_w_pool, grad_pool_scale]
    ws = [c_ctx, w_mod, b_mod, norm_g, w_ffn_gate_up, w_ffn_down, w_in, w_out, na_rpb, w_pool, pool_scale]
    ms = [m_c_ctx, m_w_mod, m_b_mod, m_norm_g, m_w_ffn_gate_up, m_w_ffn_down, m_w_in, m_w_out, m_na_rpb, m_w_pool, m_pool_scale]
    vs = [v_c_ctx, v_w_mod, v_b_mod, v_norm_g, v_w_ffn_gate_up, v_w_ffn_down, v_w_in, v_w_out, v_na_rpb, v_w_pool, v_pool_scale]
    tags = ["c_ctx", "w_mod", "b_mod", "norm_g", "gate_up", "down", "w_in", "w_out", "na_rpb", "w_pool", "pool_scale"]
    merged = ("gate_up", "down", "w_in", "w_out")
    upd = [_adamw(w_, g_, m_, v_, f"adamw_{t}", emit_grad=t in merged) for w_, g_, m_, v_, t in zip(ws, grads, ms, vs, tags)]
    grads = [u_[3] if t in merged else g_ for g_, u_, t in zip(grads, upd, tags)]
    return (loss, grad_x, *grads, *[u_[0] for u_ in upd], *[u_[1] for u_ in upd], *[u_[2] for u_ in upd])


def _small_finish(dm_rows, gc_all, c_ctx):
    depth, _, n = dm_rows.shape
    D = c_ctx.shape[0]

    def body(dm_ref, gc_ref, c_ref, gb_ref, gcx_ref):
        acc = dm_ref[:, 0]
        for j in range(1, N_DEV + 1):
            acc = acc + dm_ref[:, j]
        gb_ref[...] = acc
        t = jnp.zeros((1, D), f32)
        for l in range(depth):
            for j in range(N_CHIPS):
                t = t + gc_ref[2 * j, l, 0:1, :]
        cv = c_ref[...]
        sg = _sigmoid(cv)
        gcx_ref[...] = t * (sg * (1.0 + cv * (1.0 - sg)))

    gb, gcx = pl.pallas_call(
        body, name="small_finish",
        out_shape=[S_((depth, n), f32), S_((1, D), f32)],
        compiler_params=_cp(VMEM_MID),
    )(dm_rows, gc_all, c_ctx.reshape(1, D))
    return gb, gcx.reshape(D)
```

```python
import functools

import numpy as np
import jax
import jax.numpy as jnp
from jax import lax
from jax.experimental import pallas as pl
from jax.experimental.pallas import tpu as pltpu

f32, bf16 = jnp.float32, jnp.bfloat16

GRID_W = 64
N_MOD = 9
NA_HEADS = 8
HEAD_DIM = 64
NA_WIDTH = NA_HEADS * HEAD_DIM
NA_KH = 8
NA_KW = 16
POOL_GROUPS = 4
POOL_CH = 128
POOL_WIDTH = POOL_GROUPS * POOL_CH
POOL_WINDOWS = (2, 4, 8, 16)
IN_WIDTH = 3 * NA_WIDTH + POOL_WIDTH
MIX_WIDTH = NA_WIDTH + POOL_WIDTH
ROPE_THETA = 10000.0
ROPE_PAIRS = HEAD_DIM // 4
RMS_EPS = 1e-6
NEG_INF = -1e30
ADAM_LR, ADAM_B1, ADAM_B2, ADAM_EPS, ADAM_WD, ADAM_STEP = 0.001, 0.9, 0.999, 1e-08, 0.01, 10

N_DEV = 8
N_CHIPS = 4
LANES = 128
MIB = 1024 * 1024
VMEM_BIG = 52 * MIB
VMEM_MID = 40 * MIB
MESH = pl.DeviceIdType.MESH
ANY = pl.BlockSpec(memory_space=pl.ANY)
S_ = jax.ShapeDtypeStruct


def _cp(vmem=VMEM_MID, sem=None):
    return pltpu.CompilerParams(vmem_limit_bytes=vmem, dimension_semantics=sem)


def _sigmoid(x):
    return 0.5 * jnp.tanh(0.5 * x) + 0.5


def _rms_hat(x):
    rinv = lax.rsqrt(jnp.mean(x * x, axis=-1, keepdims=True) + RMS_EPS)
    return x * rinv, rinv


def _rms_bwd(dxhat, xhat, rinv):
    return rinv * (dxhat - xhat * jnp.mean(dxhat * xhat, axis=-1, keepdims=True))


def _rsum(a):
    return jnp.sum(a, axis=0, keepdims=True)


def _nt(a, b):
    return lax.dot_general(a, b, (((1,), (1,)), ((), ())), preferred_element_type=f32)


def _tn(a, b):
    return lax.dot_general(a, b, (((0,), (0,)), ((), ())), preferred_element_type=f32)


def _nn(a, b):
    return jnp.dot(a, b, preferred_element_type=f32)


def _swap16(x):
    lane = lax.broadcasted_iota(jnp.int32, x.shape, 1)
    n = x.shape[1]
    return jnp.where((lane % 32) < 16, pltpu.roll(x, n - 16, 1), pltpu.roll(x, 16, 1))


def _rope_tables(s_len, l_len):
    t = np.arange(s_len)
    inv = ROPE_THETA ** (-np.arange(ROPE_PAIRS, dtype=np.float32) / ROPE_PAIRS)
    ang_r = (t // GRID_W).astype(np.float32)[:, None] * inv
    ang_c = (t % GRID_W).astype(np.float32)[:, None] * inv
    cos = np.concatenate([np.cos(ang_r), np.cos(ang_r), np.cos(ang_c), np.cos(ang_c)], axis=-1)
    sin = np.concatenate([-np.sin(ang_r), np.sin(ang_r), -np.sin(ang_c), np.sin(ang_c)], axis=-1)
    cos = np.concatenate([cos, np.ones((l_len, HEAD_DIM), np.float32)], axis=0)
    sin = np.concatenate([sin, np.zeros((l_len, HEAD_DIM), np.float32)], axis=0)
    return (jnp.asarray(np.tile(cos, (1, 2)), f32), jnp.asarray(np.tile(sin, (1, 2)), f32))


def _pool_tables(tm, l_len):
    band = np.zeros((2, POOL_GROUPS, tm, tm), np.float32)
    inv = np.zeros((2, POOL_GROUPS, tm, 1), np.float32)
    for typ, length in ((0, GRID_W), (1, l_len)):
        for g, w in enumerate(POOL_WINDOWS):
            for t in range(tm):
                base, p = (t // length) * length, t % length
                lo = min(max(p - w // 2, 0), length)
                hi = min(max(p - w // 2 + w, 0), length)
                band[typ, g, t, base + lo:base + hi] = 1.0
                inv[typ, g, t, 0] = 1.0 / (hi - lo)
    return jnp.asarray(band, bf16), jnp.asarray(inv, f32)


NA_QR = 4
NA_WR = NA_KH + NA_QR - 1
NA_TYPES = 3
NA_SEL_ROWS = 136
NA_WPAD = 768


def _rpb_index_tables():
    j = np.arange(GRID_W)
    col_start = np.clip(j - NA_KW // 2, 0, GRID_W - NA_KW)
    valid = (j[None, :] >= col_start[:, None]) & (j[None, :] < col_start[:, None] + NA_KW)
    dc = np.clip(j[None, :] - j[:, None] + NA_KW - 1, 0, 2 * NA_KW - 2)
    i = np.arange(NA_QR)[:, None]
    kk = np.arange(NA_WR)[None, :]
    off = np.stack([np.zeros_like(i), i, np.full_like(i, NA_QR - 1)])
    d = np.stack([kk - i + NA_KH - 1, kk - i + NA_KH - 1 - NA_QR, kk - i])
    row_ok = (kk[None] >= off) & (kk[None] < off + NA_KH)
    assert (d[row_ok] >= 0).all() and (d[row_ok] <= 2 * NA_KH - 2).all()
    return valid, dc, d, row_ok


def _expand_rpb(rpb, name):
    _, _, d, row_ok = _rpb_index_tables()
    heads, nd, ne = rpb.shape
    w = GRID_W
    v = jnp.pad(rpb, ((0, 0), (0, 0), (w - NA_KW, 2 * w - (w - NA_KW) - ne)))
    x = jnp.broadcast_to(v[:, :, None, :], (heads, nd, w, 2 * w)).reshape(heads, nd, 2 * w * w)
    t = x[:, :, :w * (2 * w - 1)].reshape(heads, nd, w, 2 * w - 1)[..., w - 1:]

    def body(t_ref, o_ref):
        q = lax.broadcasted_iota(jnp.int32, (w, w), 0)
        c = lax.broadcasted_iota(jnp.int32, (w, w), 1)
        c0 = jnp.clip(q - NA_KW // 2, 0, w - NA_KW)
        in_cols = (c >= c0) & (c < c0 + NA_KW)
        outside = jnp.full((w, w), NEG_INF, f32)
        blocks = [jnp.where(in_cols, t_ref[dd], NEG_INF) for dd in range(nd)]
        for typ in range(NA_TYPES):
            for i in range(NA_QR):
                row = [blocks[d[typ, i, kk]] if row_ok[typ, i, kk] else outside for kk in range(NA_WR)]
                o_ref[typ, i * w:(i + 1) * w, :] = jnp.concatenate(row, axis=1)

    return pl.pallas_call(
        body, name=name, grid=(heads,),
        in_specs=[pl.BlockSpec((None, nd, w, w), lambda h: (h, 0, 0, 0))],
        out_specs=pl.BlockSpec((NA_TYPES, None, NA_QR * w, NA_WR * w), lambda h: (0, h, 0, 0)),
        out_shape=S_((NA_TYPES, heads, NA_QR * w, NA_WR * w), f32),
        compiler_params=_cp(VMEM_MID, ("arbitrary",)),
    )(t)


def _rpb_reduce_tables():
    _, _, d, row_ok = _rpb_index_tables()
    flip = np.eye(GRID_W, dtype=np.float32)[::-1].copy()
    sel = np.zeros((16, NA_SEL_ROWS), np.float32)
    flat_d, flat_ok = d.reshape(-1), row_ok.reshape(-1)
    for n in range(flat_d.size):
        if flat_ok[n]:
            sel[flat_d[n], n] = 1.0
    return jnp.asarray(flip), jnp.asarray(sel)


class _Cfg:
    def __init__(self, s_len, l_len, d, f):
        self.S, self.L, self.D, self.F = s_len, l_len, d, f
        self.T = s_len + l_len
        self.TM = 256 if l_len % 256 == 0 else 128
        assert l_len == self.TM, "context length must equal the row tile"
        assert s_len % self.TM == 0 and s_len % GRID_W == 0
        self.nxt = s_len // self.TM
        self.ntt = self.T // self.TM
        self.rows = s_len // GRID_W
        assert self.rows >= 2 * NA_KH
        assert f % (2 * LANES) == 0
        self.FC = f

    def ntiles(self, with_ctx):
        return self.ntt if with_ctx else self.nxt


def _typ(cfg):
    return lambda i: (jnp.minimum(i // cfg.nxt, 1), 0, 0)


def _mesh_pos():
    return lax.axis_index("x"), lax.axis_index("y"), lax.axis_index("c")


class _Comm:
    def __init__(self, ins, outs, alias, nsem, start, finish):
        self.ins, self.outs, self.alias, self.nsem, self.start, self.finish = ins, outs, alias, nsem, start, finish


def _call(body, args, comm=None, *, grid, in_specs, out_specs, out_shape, scratch_shapes=(), **kw):
    if comm is None:
        return pl.pallas_call(body, grid=grid, in_specs=list(in_specs), out_specs=list(out_specs), out_shape=list(out_shape),
                              scratch_shapes=list(scratch_shapes), **kw)(*args), ()
    n_in, n_out, n_sc = len(in_specs), len(out_specs), len(scratch_shapes)
    ci, co = len(comm.ins), len(comm.outs)

    def carrier(*refs):
        bounds = np.cumsum([0, n_in, ci, n_out, co, n_sc])
        ins, cins, outs, couts, scr = (refs[a:b] for a, b in zip(bounds[:-1], bounds[1:]))
        send, recv = refs[bounds[-1]], refs[bounds[-1] + 1]
        first = functools.reduce(jnp.logical_and, [pl.program_id(a) == 0 for a in range(len(grid))])
        last = functools.reduce(jnp.logical_and, [pl.program_id(a) == g - 1 for a, g in enumerate(grid)])

        @pl.when(first)
        def _():
            comm.start(cins, couts, send, recv)

        body(*ins, *outs, *scr)

        @pl.when(last)
        def _():
            comm.finish(cins, couts, send, recv)

    res = pl.pallas_call(
        carrier, grid=grid, in_specs=list(in_specs) + [ANY] * ci, out_specs=list(out_specs) + [ANY] * co,
        out_shape=list(out_shape) + list(comm.outs),
        input_output_aliases={n_in + a: n_out + b for a, b in comm.alias.items()},
        scratch_shapes=list(scratch_shapes) + [pltpu.SemaphoreType.DMA((comm.nsem,)), pltpu.SemaphoreType.DMA((comm.nsem,))],
        **kw)(*args, *comm.ins)
    return res[:n_out], res[n_out:]


def _comm_only(comm, name):
    ci, co = len(comm.ins), len(comm.outs)

    def body(*refs):
        cins, couts = refs[:ci], refs[ci:ci + co]
        send, recv = refs[ci + co], refs[ci + co + 1]
        comm.start(cins, couts, send, recv)
        comm.finish(cins, couts, send, recv)

    return pl.pallas_call(
        body, name=name, in_specs=[ANY] * ci, out_specs=[ANY] * co, out_shape=list(comm.outs),
        input_output_aliases=dict(comm.alias),
        scratch_shapes=[pltpu.SemaphoreType.DMA((comm.nsem,)), pltpu.SemaphoreType.DMA((comm.nsem,))],
        compiler_params=_cp(VMEM_MID),
    )(*comm.ins)


def _half_view(ref, axis, kk, h):
    r, c = ref.shape
    if axis == 1:
        n = c // N_CHIPS
        return ref.at[pl.ds(h * (r // 2), r // 2), pl.ds(pl.multiple_of(kk * n, LANES), n)]
    n = r // N_CHIPS
    return ref.at[pl.ds(pl.multiple_of(kk * n + h * (n // 2), 8), n // 2), :]


def _other_chips(x, y):
    return [(1 - x, y), (x, 1 - y), (1 - x, 1 - y)]


def _gather_comm(arrs, axes):
    n = len(arrs)

    def copy(ref, view, sems, k, to):
        send, recv = sems
        return pltpu.make_async_remote_copy(src_ref=view, dst_ref=view, send_sem=send.at[k], recv_sem=recv.at[k],
                                            device_id=to, device_id_type=MESH)

    def start(cins, bufs, send, recv):
        x, y, c = _mesh_pos()
        for t in range(n):
            own = _half_view(bufs[t], axes[t], 2 * x + y, c)
            for j, chip in enumerate(_other_chips(x, y)):
                copy(bufs[t], own, (send, recv), 6 * t + j, (*chip, c)).start()

    def finish(cins, bufs, send, recv):
        x, y, c = _mesh_pos()
        sibling = (x, y, 1 - c)
        chips = _other_chips(x, y)
        for t in range(n):
            for j, chip in enumerate(chips):
                landed = _half_view(bufs[t], axes[t], 2 * chip[0] + chip[1], c)
                copy(bufs[t], landed, (send, recv), 6 * t + j, (*chip, c)).wait_recv()
                copy(bufs[t], landed, (send, recv), 6 * t + 3 + j, sibling).start()
        for t in range(n):
            own = _half_view(bufs[t], axes[t], 2 * x + y, c)
            for j, chip in enumerate(chips):
                kj = 2 * chip[0] + chip[1]
                copy(bufs[t], _half_view(bufs[t], axes[t], kj, 1 - c), (send, recv), 6 * t + 3 + j, sibling).wait_recv()
                copy(bufs[t], own, (send, recv), 6 * t + j, (*chip, c)).wait_send()
                copy(bufs[t], _half_view(bufs[t], axes[t], kj, c), (send, recv), 6 * t + 3 + j, sibling).wait_send()

    return _Comm(list(arrs), [S_(a.shape, a.dtype) for a in arrs], {t: t for t in range(n)}, 6 * n, start, finish)


def _scatter_comm(parts, axes):
    n = len(parts)
    peers = [(fx, fy, fc) for fx in (0, 1) for fy in (0, 1) for fc in (0, 1)][1:]

    def half_shape(a, axis):
        r, c = a.shape
        return (r // 2, c // N_CHIPS) if axis == 1 else (r // N_CHIPS // 2, c)

    def start(srcs, lands, send, recv):
        x, y, c = _mesh_pos()
        me = 4 * x + 2 * y + c
        for t in range(n):
            for r_, (fx, fy, fc) in enumerate(peers):
                dx, dy, dc = (1 - x if fx else x), (1 - y if fy else y), (1 - c if fc else c)
                pltpu.make_async_remote_copy(
                    src_ref=_half_view(srcs[t], axes[t], 2 * dx + dy, dc), dst_ref=lands[t].at[me],
                    send_sem=send.at[7 * t + r_], recv_sem=recv.at[7 * t + r_],
                    device_id=(dx, dy, dc), device_id_type=MESH).start()

    def finish(srcs, lands, send, recv):
        x, y, c = _mesh_pos()
        for t in range(n):
            mine = _half_view(srcs[t], axes[t], 2 * x + y, c)
            for r_, (fx, fy, fc) in enumerate(peers):
                sx, sy, sc = (1 - x if fx else x), (1 - y if fy else y), (1 - c if fc else c)
                cp = pltpu.make_async_remote_copy(
                    src_ref=mine, dst_ref=lands[t].at[4 * sx + 2 * sy + sc],
                    send_sem=send.at[7 * t + r_], recv_sem=recv.at[7 * t + r_],
                    device_id=(sx, sy, sc), device_id_type=MESH)
                cp.wait_recv()
                cp.wait_send()

    return _Comm(list(parts), [S_((N_DEV,) + half_shape(a, ax), a.dtype) for a, ax in zip(parts, axes)], {}, 7 * n, start, finish)


def _allgather_comm(block):
    peers = [(fx, fy, fc) for fx in (0, 1) for fy in (0, 1) for fc in (0, 1)][1:]

    def ends(x, y, c):
        for r_, (fx, fy, fc) in enumerate(peers):
            yield r_, ((1 - x if fx else x), (1 - y if fy else y), (1 - c if fc else c))

    def start(srcs, lands, send, recv):
        x, y, c = _mesh_pos()
        for r_, peer in ends(x, y, c):
            pltpu.make_async_remote_copy(src_ref=srcs[0], dst_ref=lands[0].at[4 * x + 2 * y + c], send_sem=send.at[r_],
                                         recv_sem=recv.at[r_], device_id=peer, device_id_type=MESH).start()

    def finish(srcs, lands, send, recv):
        x, y, c = _mesh_pos()
        for r_, (px, py, pc) in ends(x, y, c):
            cp = pltpu.make_async_remote_copy(src_ref=srcs[0], dst_ref=lands[0].at[4 * px + 2 * py + pc], send_sem=send.at[r_],
                                              recv_sem=recv.at[r_], device_id=(px, py, pc), device_id_type=MESH)
            cp.wait_recv()
            cp.wait_send()

    return _Comm([block], [S_((N_DEV,) + block.shape, block.dtype)], {}, len(peers), start, finish)


def _ffn_fwd(cfg, xs, mods, gvec, wgu, wd, mi, gi, with_ctx, name, comm=None, xs_ctx=None, loss_target=None):
    TM, D, F, FC = cfg.TM, cfg.D, cfg.F, cfg.FC
    nt = cfg.ntiles(with_ctx)
    R = nt * TM
    split, head = xs_ctx is not None, loss_target is not None

    def body(*refs):
        it = iter(refs)
        xs_ref = next(it)
        xc_ref = next(it) if split else None
        mods_ref, g_ref, wgu_hbm, wd_hbm = next(it), next(it), next(it), next(it)
        t_ref = next(it) if head else None
        out_ref, hb_ref, z_ref, y_ref = next(it), next(it), next(it), next(it)
        loss_ref = next(it) if head else None
        wgu_v, wd_v, sem = next(it), next(it), next(it)
        i = pl.program_id(0)

        @pl.when(i == 0)
        def _():
            c0 = pltpu.make_async_copy(wgu_hbm, wgu_v, sem.at[0])
            c1 = pltpu.make_async_copy(wd_hbm, wd_v, sem.at[1])
            c0.start(); c1.start(); c0.wait(); c1.wait()
            if head:
                loss_ref[...] = jnp.zeros_like(loss_ref)
        x = xs_ref[...]
        if split:
            x = jnp.where(i < cfg.nxt, x, xc_ref[...])
        m = mods_ref[0]
        sh, sc, gt = m[mi:mi + 1], m[mi + 1:mi + 2], m[mi + 2:mi + 3]
        xhat, _ = _rms_hat(x)
        h = (xhat * g_ref[gi:gi + 1]) * (1.0 + sc) + sh
        hb = h.astype(bf16)
        hb_ref[...] = hb
        y = jnp.zeros((TM, D), f32)
        for ch in range(F // FC):
            zg = _nn(hb, wgu_v[:, ch * FC:(ch + 1) * FC])
            zu = _nn(hb, wgu_v[:, F + ch * FC:F + (ch + 1) * FC])
            z_ref[:, ch * FC:(ch + 1) * FC] = zg.astype(bf16)
            z_ref[:, F + ch * FC:F + (ch + 1) * FC] = zu.astype(bf16)
            a = (zg * _sigmoid(zg)) * zu
            y = y + _nn(a.astype(bf16), wd_v[ch * FC:(ch + 1) * FC, :])
        y_ref[...] = y
        yhat, _ = _rms_hat(y)
        out = x + 0.5 * gt * (yhat * g_ref[gi + 1:gi + 2])
        if head:
            e = out - t_ref[...]
            out_ref[...] = e * (1.0 / D)
            loss_ref[...] += jnp.sum(jnp.mean(e * e, axis=-1, keepdims=True), axis=0, keepdims=True) * 0.5
        else:
            out_ref[...] = out

    rt = lambda c: pl.BlockSpec((TM, c), lambda i: (i, 0))
    lat = pl.BlockSpec((TM, D), lambda i: (jnp.minimum(i, cfg.nxt - 1), 0))
    x_specs, x_args = ([lat, pl.BlockSpec((TM, D), lambda i: (0, 0))], [xs, xs_ctx]) if split else ([rt(D)], [xs])
    t_specs, t_args = ([rt(D)], [loss_target]) if head else ([], [])
    l_specs, l_shape = ([pl.BlockSpec((8, LANES), lambda i: (0, 0))], [S_((8, LANES), f32)]) if head else ([], [])
    return _call(
        body, (*x_args, mods, gvec, wgu, wd, *t_args), comm, name=name, grid=(nt,),
        in_specs=x_specs + [pl.BlockSpec((1, N_MOD, D), _typ(cfg)), pl.BlockSpec((6, D), lambda i: (0, 0)), ANY, ANY] + t_specs,
        out_specs=[rt(D), rt(D), rt(2 * F), rt(D)] + l_specs,
        out_shape=[S_((R, D), f32), S_((R, D), bf16), S_((R, 2 * F), bf16), S_((R, D), f32)] + l_shape,
        scratch_shapes=[pltpu.VMEM((D, 2 * F), bf16), pltpu.VMEM((F, D), bf16), pltpu.SemaphoreType.DMA((2,))],
        compiler_params=_cp(VMEM_BIG, ("arbitrary",)),
    )


def _ffn_bwd(cfg, dout, xs, z, y, mods, gvec, wgu, wd, mi, gi, with_ctx, name, comm=None, xs_ctx=None):
    TM, D, F, FC = cfg.TM, cfg.D, cfg.F, cfg.FC
    nt = cfg.ntiles(with_ctx)
    R = nt * TM
    ntyp = 2 if with_ctx else 1
    split = xs_ctx is not None

    def body(*refs):
        it = iter(refs)
        do_ref, xs_ref = next(it), next(it)
        xc_ref = next(it) if split else None
        z_ref, y_ref, mods_ref, g_ref, wgu_hbm, wd_hbm = (next(it) for _ in range(6))
        dx_ref, dz_ref, dy_ref, a_ref, dm_ref, dg_ref, wgu_v, wd_v, sem = (next(it) for _ in range(9))
        i = pl.program_id(0)

        @pl.when(i == 0)
        def _():
            c0 = pltpu.make_async_copy(wgu_hbm, wgu_v, sem.at[0])
            c1 = pltpu.make_async_copy(wd_hbm, wd_v, sem.at[1])
            c0.start(); c1.start(); c0.wait(); c1.wait()
            dg_ref[...] = jnp.zeros_like(dg_ref)

        @pl.when((i == 0) | (i == cfg.nxt))
        def _():
            dm_ref[...] = jnp.zeros_like(dm_ref)

        do = do_ref[...]
        x = xs_ref[...]
        if split:
            x = jnp.where(i < cfg.nxt, x, xc_ref[...])
        m = mods_ref[0]
        sc, gt = m[mi + 1:mi + 2], m[mi + 2:mi + 3]
        g_pre, g_post = g_ref[gi:gi + 1], g_ref[gi + 1:gi + 2]
        xhat, rinv0 = _rms_hat(x)
        n0 = xhat * g_pre
        yhat, rinv1 = _rms_hat(y_ref[...])
        d_gt = _rsum(0.5 * do * (yhat * g_post))
        dr = (0.5 * gt) * do
        dg_post = _rsum(dr * yhat)
        dy = _rms_bwd(dr * g_post, yhat, rinv1)
        dyb = dy.astype(bf16)
        dy_ref[...] = dyb
        dh = jnp.zeros((TM, D), f32)
        for ch in range(F // FC):
            zg = z_ref[:, ch * FC:(ch + 1) * FC].astype(f32)
            zu = z_ref[:, F + ch * FC:F + (ch + 1) * FC].astype(f32)
            sg = _sigmoid(zg)
            silu = zg * sg
            a_ref[:, ch * FC:(ch + 1) * FC] = (silu * zu).astype(bf16)
            da = _nt(dyb, wd_v[ch * FC:(ch + 1) * FC, :])
            dzu = (da * silu).astype(bf16)
            dzg = (da * zu * (sg * (1.0 + zg * (1.0 - sg)))).astype(bf16)
            dz_ref[:, ch * FC:(ch + 1) * FC] = dzg
            dz_ref[:, F + ch * FC:F + (ch + 1) * FC] = dzu
            dh = dh + _nt(dzg, wgu_v[:, ch * FC:(ch + 1) * FC]) + _nt(dzu, wgu_v[:, F + ch * FC:F + (ch + 1) * FC])
        d_sh = _rsum(dh)
        d_sc = _rsum(dh * n0)
        dn = dh * (1.0 + sc)
        dg_pre = _rsum(dn * xhat)
        dx = do + _rms_bwd(dn * g_pre, xhat, rinv0)
        if split:
            @pl.when(i < cfg.nxt)
            def _():
                dx_ref[...] = dx
        else:
            dx_ref[...] = dx
        dm_ref[0] += jnp.concatenate([d_sh, d_sc, d_gt], axis=0)
        dg_ref[...] += jnp.concatenate([dg_pre, dg_post], axis=0)

    rt = lambda c: pl.BlockSpec((TM, c), lambda i: (i, 0))
    lat = pl.BlockSpec((TM, D), lambda i: (jnp.minimum(i, cfg.nxt - 1), 0))
    x_specs, x_args = ([lat, pl.BlockSpec((TM, D), lambda i: (0, 0))], [xs, xs_ctx]) if split else ([rt(D)], [xs])
    return _call(
        body, (dout, *x_args, z, y, mods, gvec, wgu, wd), comm, name=name, grid=(nt,),
        in_specs=[rt(D)] + x_specs + [rt(2 * F), rt(D), pl.BlockSpec((1, N_MOD, D), _typ(cfg)),
                                       pl.BlockSpec((6, D), lambda i: (0, 0)), ANY, ANY],
        out_specs=[lat if split else rt(D), rt(2 * F), rt(D), rt(F), pl.BlockSpec((1, 3, D), _typ(cfg)),
                   pl.BlockSpec((2, D), lambda i: (0, 0))],
        out_shape=[S_((cfg.S if split else R, D), f32), S_((R, 2 * F), bf16), S_((R, D), bf16), S_((R, F), bf16),
                   S_((ntyp, 3, D), f32), S_((2, D), f32)],
        scratch_shapes=[pltpu.VMEM((D, 2 * F), bf16), pltpu.VMEM((F, D), bf16), pltpu.SemaphoreType.DMA((2,))],
        compiler_params=_cp(VMEM_BIG, ("arbitrary",)),
    )


def _wgrad(a, b, k_rows, name, comm=None):
    M, N = a.shape[1], b.shape[1]
    tn = N
    for cand in (1408, 1024, 512):
        if N % cand == 0 and N > cand:
            tn = cand
            break
    room = VMEM_BIG - 6 * MIB - 2 * M * tn * 6
    tk = _div_tile(k_rows, 1, min(2816, room // (4 * (M + tn))), LANES)
    nk = k_rows // tk

    def body(a_ref, b_ref, o_ref, ob_ref):
        k = pl.program_id(1)

        @pl.when(k == 0)
        def _():
            o_ref[...] = jnp.zeros_like(o_ref)
        o_ref[...] += _tn(a_ref[...], b_ref[...])

        @pl.when(k == nk - 1)
        def _():
            ob_ref[...] = o_ref[...].astype(bf16)

    ospec = pl.BlockSpec((M, tn), lambda n, k: (0, n))
    return _call(
        body, (a, b), comm, name=name, grid=(N // tn, nk),
        in_specs=[pl.BlockSpec((tk, M), lambda n, k: (k, 0)), pl.BlockSpec((tk, tn), lambda n, k: (k, n))],
        out_specs=[ospec, ospec], out_shape=[S_((M, N), f32), S_((M, N), bf16)],
        compiler_params=_cp(VMEM_BIG, ("arbitrary", "arbitrary")),
    )


def _tmpre_fwd(cfg, xs, mods, gvec, w_in, cos, sin, name):
    TM, D = cfg.TM, cfg.D
    nt, R = cfg.ntt, cfg.T
    W = NA_WIDTH

    def body(xs_ref, mods_ref, g_ref, w_ref, cos_ref, sin_ref, hb_ref, q_ref, k_ref, v_ref, u_ref):
        x = xs_ref[...]
        m = mods_ref[0]
        xhat, _ = _rms_hat(x)
        hb = ((xhat * g_ref[2:3]) * (1.0 + m[4:5]) + m[3:4]).astype(bf16)
        hb_ref[...] = hb
        p = _nn(hb, w_ref[...])
        cs = jnp.tile(cos_ref[...], (1, W // LANES))
        sn = jnp.tile(sin_ref[...], (1, W // LANES))
        q = p[:, 0:W]
        k = p[:, W:2 * W]
        q_ref[...] = ((q * cs + _swap16(q) * sn) * (HEAD_DIM ** -0.5)).astype(bf16)
        k_ref[...] = (k * cs + _swap16(k) * sn).astype(bf16)
        v_ref[...] = p[:, 2 * W:3 * W].astype(bf16)
        u_ref[...] = p[:, 3 * W:]

    rt = lambda c: pl.BlockSpec((TM, c), lambda i: (i, 0))
    return pl.pallas_call(
        body, name=name, grid=(nt,),
        in_specs=[rt(D), pl.BlockSpec((1, N_MOD, D), _typ(cfg)), pl.BlockSpec((6, D), lambda i: (0, 0)),
                  pl.BlockSpec((D, IN_WIDTH), lambda i: (0, 0)), rt(LANES), rt(LANES)],
        out_specs=[rt(D), rt(W), rt(W), rt(W), rt(POOL_WIDTH)],
        out_shape=[S_((R, D), bf16), S_((R, W), bf16), S_((R, W), bf16), S_((R, W), bf16), S_((R, POOL_WIDTH), f32)],
        compiler_params=_cp(VMEM_MID, ("arbitrary",)),
    )(xs, mods, gvec, w_in, cos, sin)


def _tmpre_bwd(cfg, lat, ctx_terms, du_has_ctx, cos, sin, w_in, xs, mods, gvec, dres, res_with_ctx, name):
    TM, D = cfg.TM, cfg.D
    nt, R = cfg.ntt, cfg.T
    nres = cfg.ntiles(res_with_ctx)
    W = NA_WIDTH
    n_ctx = [len(t) for t in ctx_terms]
    flat_ctx = [a for t in ctx_terms for a in t]
    n_asm = 4 + len(flat_ctx) + 2

    def assemble(refs, o_ref):
        dq_ref, dk_ref, dv_ref, du_ref = refs[:4]
        ctx_refs = refs[4:4 + len(flat_ctx)]
        cos_ref, sin_ref = refs[4 + len(flat_ctx):]
        is_ctx = pl.program_id(0) >= cfg.nxt
        vals, off = [], 0
        for lat_ref, n in zip((dq_ref, dk_ref, dv_ref), n_ctx):
            cv = jnp.zeros((TM, W), f32)
            for r_ in ctx_refs[off:off + n]:
                cv = cv + r_[...]
            off += n
            vals.append(jnp.where(is_ctx, cv, lat_ref[...]))
        du_ = du_ref[...] if du_has_ctx else jnp.where(is_ctx, 0.0, du_ref[...])
        cs = jnp.tile(cos_ref[...], (1, W // LANES))
        sn = jnp.tile(sin_ref[...], (1, W // LANES))
        dq_ = vals[0] * (HEAD_DIM ** -0.5)
        dk_ = vals[1]
        o_ref[:, 0:W] = (dq_ * cs + _swap16(dq_ * sn)).astype(bf16)
        o_ref[:, W:2 * W] = (dk_ * cs + _swap16(dk_ * sn)).astype(bf16)
        o_ref[:, 2 * W:3 * W] = vals[2].astype(bf16)
        o_ref[:, 3 * W:] = du_.astype(bf16)

    def body(*refs):
        w_ref, xs_ref, mods_ref, g_ref, dres_ref, dx_ref, dp_ref, dm_ref, dg_ref = refs[n_asm:]
        i = pl.program_id(0)

        @pl.when(i == 0)
        def _():
            dg_ref[...] = jnp.zeros_like(dg_ref)

        @pl.when((i == 0) | (i == cfg.nxt))
        def _():
            dm_ref[...] = jnp.zeros_like(dm_ref)

        assemble(refs[:n_asm], dp_ref)
        dh = _nt(dp_ref[...], w_ref[...])
        x = xs_ref[...]
        m = mods_ref[0]
        g2 = g_ref[2:3]
        xhat, rinv = _rms_hat(x)
        d_sh = _rsum(dh)
        d_sc = _rsum(dh * (xhat * g2))
        dn = dh * (1.0 + m[4:5])
        dg_ref[...] += _rsum(dn * xhat)
        dx = _rms_bwd(dn * g2, xhat, rinv)
        res = dres_ref[...]
        if nres < nt:
            res = jnp.where(i < nres, res, 0.0)
        dx_ref[...] = res + dx
        dm_ref[0] += jnp.concatenate([d_sh, d_sc], axis=0)

    rt = lambda c: pl.BlockSpec((TM, c), lambda i: (i, 0))
    lat_spec = pl.BlockSpec((TM, W), lambda i: (jnp.minimum(i, cfg.nxt - 1), 0))
    du_spec = rt(POOL_WIDTH) if du_has_ctx else lat_spec
    asm_specs = ([lat_spec, lat_spec, lat_spec, du_spec] + [pl.BlockSpec((TM, W), lambda i: (0, 0))] * len(flat_ctx)
                 + [rt(LANES), rt(LANES)])
    return pl.pallas_call(
        body, name=name, grid=(nt,),
        in_specs=asm_specs + [pl.BlockSpec((D, IN_WIDTH), lambda i: (0, 0)), rt(D),
                              pl.BlockSpec((1, N_MOD, D), _typ(cfg)), pl.BlockSpec((6, D), lambda i: (0, 0)),
                              pl.BlockSpec((TM, D), lambda i: (jnp.minimum(i, nres - 1), 0))],
        out_specs=[rt(D), rt(IN_WIDTH), pl.BlockSpec((1, 2, D), _typ(cfg)), pl.BlockSpec((1, D), lambda i: (0, 0))],
        out_shape=[S_((R, D), f32), S_((R, IN_WIDTH), bf16), S_((2, 2, D), f32), S_((1, D), f32)],
        compiler_params=_cp(VMEM_MID, ("arbitrary",)),
    )(*lat, *flat_ctx, cos, sin, w_in, xs, mods, gvec, dres)


def _na_block(cfg, b):
    return jnp.clip(NA_QR * b - NA_KH // 2, 0, cfg.rows - NA_WR)


def _na_load_bias(b, nb, b_hbm, b_v, sem):
    for typ, at in ((0, 0), (1, 1), (2, nb - 1)):
        @pl.when(b == at)
        def _(typ=typ):
            cp = pltpu.make_async_copy(b_hbm.at[typ], b_v, sem)
            cp.start()
            cp.wait()


def _na_probs(qh, klh, kch, bias):
    s_loc = _nt(qh, klh) + bias
    s_ctx = _nt(qh, kch)
    mx = jnp.maximum(jnp.max(s_loc, axis=-1, keepdims=True), jnp.max(s_ctx, axis=-1, keepdims=True))
    e_loc = jnp.exp(s_loc - mx)
    e_ctx = jnp.exp(s_ctx - mx)
    inv = 1.0 / (jnp.sum(e_loc, axis=-1, keepdims=True) + jnp.sum(e_ctx, axis=-1, keepdims=True))
    return e_loc * inv, e_ctx * inv


def _na_fwd(cfg, q, k, v, bexp, name, comm=None):
    S, L, T = cfg.S, cfg.L, cfg.T
    NQ, NW = NA_QR * GRID_W, NA_WR * GRID_W
    nb = cfg.rows // NA_QR

    def body(q_ref, k_hbm, v_hbm, b_hbm, o_ref, k_v, v_v, b_v, sem):
        b = pl.program_id(0)

        @pl.when(b == 0)
        def _():
            cs = [pltpu.make_async_copy(k_hbm, k_v, sem.at[0]), pltpu.make_async_copy(v_hbm, v_v, sem.at[1])]
            for c_ in cs:
                c_.start()
            for c_ in cs:
                c_.wait()

        _na_load_bias(b, nb, b_hbm, b_v, sem.at[2])
        st = pl.multiple_of(_na_block(cfg, b) * GRID_W, GRID_W)
        first = lax.broadcasted_iota(jnp.int32, (NQ, LANES), 1) < HEAD_DIM
        for hp in range(NA_HEADS // 2):
            ls = slice(hp * LANES, (hp + 1) * LANES)
            q2 = q_ref[:, ls]
            kl, vl = k_v[pl.ds(st, NW), ls], v_v[pl.ds(st, NW), ls]
            kc, vc = k_v[S:T, ls], v_v[S:T, ls]
            o2 = []
            for hh in range(2):
                qm = jnp.where(first if hh == 0 else ~first, q2, jnp.zeros_like(q2))
                p_loc, p_ctx = _na_probs(qm, kl, kc, b_v[2 * hp + hh])
                o2.append(_nn(p_loc.astype(bf16), vl) + _nn(p_ctx.astype(bf16), vc))
            o_ref[:, ls] = jnp.where(first, o2[0], o2[1]).astype(bf16)

    return _call(
        body, (q, k, v, bexp), comm, name=name, grid=(nb,),
        in_specs=[pl.BlockSpec((NQ, NA_WIDTH), lambda b: (b, 0)), ANY, ANY, ANY],
        out_specs=[pl.BlockSpec((NQ, NA_WIDTH), lambda b: (b, 0))],
        out_shape=[S_((S, NA_WIDTH), bf16)],
        scratch_shapes=[pltpu.VMEM((T, NA_WIDTH), bf16), pltpu.VMEM((T, NA_WIDTH), bf16),
                        pltpu.VMEM((NA_HEADS, NQ, NW), f32), pltpu.SemaphoreType.DMA((3,))],
        compiler_params=_cp(VMEM_MID, ("arbitrary",)),
    )


def _na_bwd(cfg, do, q, k, v, bexp, name, comm=None):
    S, L, T, rows = cfg.S, cfg.L, cfg.T, cfg.rows
    NQ, NW = NA_QR * GRID_W, NA_WR * GRID_W
    NSLOT = 2 * NA_KH
    nb = rows // NA_QR
    bmax = (rows - NA_WR) // NA_QR
    steps = 2 * nb - bmax
    W = NA_WIDTH
    assert nb >= 3 and bmax >= 1 and rows - NA_QR * bmax <= NSLOT

    def out_group(g):
        return jnp.where(g >= nb, g - nb + bmax, jnp.clip(g - 1, 0, bmax - 1))

    def body(do_ref, q_ref, k_hbm, v_hbm, b_hbm, dq_ref, dk_ref, dv_ref, dkc_ref, dvc_ref, db_hbm,
             k_v, v_v, b_v, db_v, ak, av, akc, avc, sem):
        g = pl.program_id(0)

        @pl.when(g == 0)
        def _():
            cs = [pltpu.make_async_copy(k_hbm, k_v, sem.at[0]), pltpu.make_async_copy(v_hbm, v_v, sem.at[1])]
            for c_ in cs:
                c_.start()
            db_v[...] = jnp.zeros_like(db_v)
            ak[...] = jnp.zeros_like(ak)
            av[...] = jnp.zeros_like(av)
            akc[...] = jnp.zeros_like(akc)
            avc[...] = jnp.zeros_like(avc)
            for c_ in cs:
                c_.wait()

        for typ, at in ((0, 1), (1, nb - 1)):
            @pl.when(g == at)
            def _(typ=typ):
                cp = pltpu.make_async_copy(db_v, db_hbm.at[typ], sem.at[2])
                cp.start()
                cp.wait()
                db_v[...] = jnp.zeros_like(db_v)

        @pl.when(g < nb)
        def _():
            _na_load_bias(g, nb, b_hbm, b_v, sem.at[2])
            ws = _na_block(cfg, g)
            st = pl.multiple_of(ws * GRID_W, GRID_W)
            first = lax.broadcasted_iota(jnp.int32, (NQ, LANES), 1) < HEAD_DIM
            for hp in range(NA_HEADS // 2):
                ls = slice(hp * LANES, (hp + 1) * LANES)
                q2, do2 = q_ref[:, ls], do_ref[:, ls]
                kl, vl = k_v[pl.ds(st, NW), ls], v_v[pl.ds(st, NW), ls]
                kc, vc = k_v[S:T, ls], v_v[S:T, ls]
                dq2 = []
                dk2 = jnp.zeros((NW, LANES), f32)
                dv2 = jnp.zeros((NW, LANES), f32)
                dkc2 = jnp.zeros((L, LANES), f32)
                dvc2 = jnp.zeros((L, LANES), f32)
                for hh in range(2):
                    keep = first if hh == 0 else ~first
                    qm = jnp.where(keep, q2, jnp.zeros_like(q2))
                    dom = jnp.where(keep, do2, jnp.zeros_like(do2))
                    p_loc, p_ctx = _na_probs(qm, kl, kc, b_v[2 * hp + hh])
                    dp_loc = _nt(dom, vl)
                    dp_ctx = _nt(dom, vc)
                    delta = jnp.sum(p_loc * dp_loc, axis=-1, keepdims=True) + jnp.sum(p_ctx * dp_ctx, axis=-1, keepdims=True)
                    ds_loc = p_loc * (dp_loc - delta)
                    ds_ctx = p_ctx * (dp_ctx - delta)
                    db_v[2 * hp + hh, :, 0:NW] += ds_loc
                    dsl, dsc = ds_loc.astype(bf16), ds_ctx.astype(bf16)
                    dq2.append(_nn(dsl, kl) + _nn(dsc, kc))
                    dk2 = dk2 + _tn(dsl, qm)
                    dv2 = dv2 + _tn(p_loc.astype(bf16), dom)
                    dkc2 = dkc2 + _tn(dsc, qm)
                    dvc2 = dvc2 + _tn(p_ctx.astype(bf16), dom)
                dq_ref[:, ls] = jnp.where(first, dq2[0], dq2[1])
                akc[:, ls] += dkc2
                avc[:, ls] += dvc2
                for kk in range(NA_WR):
                    slot = (ws + kk) % NSLOT
                    ak[slot, :, ls] += dk2[kk * GRID_W:(kk + 1) * GRID_W, :]
                    av[slot, :, ls] += dv2[kk * GRID_W:(kk + 1) * GRID_W, :]

        @pl.when(((g >= 1) & (g <= bmax)) | (g >= nb))
        def _():
            base = NA_QR * (out_group(g) % (NSLOT // NA_QR))
            for t in range(NA_QR):
                dk_ref[t * GRID_W:(t + 1) * GRID_W, :] = ak[base + t]
                dv_ref[t * GRID_W:(t + 1) * GRID_W, :] = av[base + t]
                ak[base + t] = jnp.zeros((GRID_W, W), f32)
                av[base + t] = jnp.zeros((GRID_W, W), f32)

        @pl.when(g == nb - 1)
        def _():
            cp = pltpu.make_async_copy(db_v, db_hbm.at[2], sem.at[2])
            cp.start()
            cp.wait()

        @pl.when(g == steps - 1)
        def _():
            dkc_ref[...] = akc[...]
            dvc_ref[...] = avc[...]

    qmap = lambda g: (jnp.minimum(g, nb - 1), 0)
    kmap = lambda g: (out_group(g), 0)
    full = lambda g: (0, 0)
    return _call(
        body, (do, q, k, v, bexp), comm, name=name, grid=(steps,),
        in_specs=[pl.BlockSpec((NQ, W), qmap), pl.BlockSpec((NQ, W), qmap), ANY, ANY, ANY],
        out_specs=[pl.BlockSpec((NQ, W), qmap), pl.BlockSpec((NQ, W), kmap), pl.BlockSpec((NQ, W), kmap),
                   pl.BlockSpec((L, W), full), pl.BlockSpec((L, W), full), ANY],
        out_shape=[S_((S, W), f32), S_((S, W), f32), S_((S, W), f32), S_((L, W), f32), S_((L, W), f32),
                   S_((NA_TYPES, NA_HEADS, NQ, NA_WPAD), f32)],
        scratch_shapes=[pltpu.VMEM((T, W), bf16), pltpu.VMEM((T, W), bf16),
                        pltpu.VMEM((NA_HEADS, NQ, NW), f32), pltpu.VMEM((NA_HEADS, NQ, NA_WPAD), f32),
                        pltpu.VMEM((NSLOT, GRID_W, W), f32), pltpu.VMEM((NSLOT, GRID_W, W), f32),
                        pltpu.VMEM((L, W), f32), pltpu.VMEM((L, W), f32), pltpu.SemaphoreType.DMA((3,))],
        compiler_params=_cp(VMEM_BIG, ("arbitrary",)),
    )


def _rpb_reduce(dbias, flip, sel, name):
    nq, w = NA_QR * GRID_W, GRID_W

    def diag_body(x_ref, j_ref, o_ref):
        rows = []
        for i in range(NA_QR):
            xr = jnp.dot(j_ref[...], x_ref[i * w:(i + 1) * w, :], preferred_element_type=f32, precision=lax.Precision.HIGHEST)
            rows.append(jnp.sum(pltpu.roll(xr, 0, 1, stride=1, stride_axis=0), axis=0, keepdims=True))
        o_ref[...] = jnp.concatenate(rows + [jnp.zeros((8 - NA_QR, NA_WPAD), f32)], axis=0)

    diag = pl.pallas_call(
        diag_body, name=name + "_diag", grid=(NA_TYPES, NA_HEADS),
        in_specs=[pl.BlockSpec((None, None, nq, NA_WPAD), lambda t, h: (t, h, 0, 0)), pl.BlockSpec((w, w), lambda t, h: (0, 0))],
        out_specs=pl.BlockSpec((None, None, 8, NA_WPAD), lambda t, h: (t, h, 0, 0)),
        out_shape=S_((NA_TYPES, NA_HEADS, 8, NA_WPAD), f32),
        compiler_params=_cp(VMEM_MID, ("arbitrary", "arbitrary")),
    )(dbias, flip)
    lo = w - NA_KW
    y = diag[:, :, :NA_QR, lo:lo + NA_WR * w].reshape(NA_TYPES, NA_HEADS, NA_QR, NA_WR, w)
    y = jnp.transpose(y, (1, 0, 2, 3, 4)).reshape(NA_HEADS, NA_TYPES * NA_QR * NA_WR, w)
    y = jnp.pad(y, ((0, 0), (0, NA_SEL_ROWS - y.shape[1]), (0, LANES - w)))

    def body(y_ref, sel_ref, o_ref):
        o_ref[...] = jnp.dot(sel_ref[...], y_ref[...], preferred_element_type=f32, precision=lax.Precision.HIGHEST)

    return pl.pallas_call(
        body, name=name, grid=(NA_HEADS,),
        in_specs=[pl.BlockSpec((None, NA_SEL_ROWS, LANES), lambda h: (h, 0, 0)), pl.BlockSpec((16, NA_SEL_ROWS), lambda h: (0, 0))],
        out_specs=pl.BlockSpec((None, 16, LANES), lambda h: (h, 0, 0)),
        out_shape=S_((NA_HEADS, 16, LANES), f32),
        compiler_params=_cp(VMEM_MID, ("arbitrary",)),
    )(y, sel)


def _ctx_attn_fwd(cfg, q, k, v, name):
    L = cfg.L
    blk = cfg.S // L

    def body(q_ref, k_ref, v_ref, o_ref):
        qv, kv, vv = q_ref[...], k_ref[...], v_ref[...]
        outs = []
        for h in range(NA_HEADS):
            hs = slice(h * HEAD_DIM, (h + 1) * HEAD_DIM)
            s = _nt(qv[:, hs], kv[:, hs])
            e = jnp.exp(s - jnp.max(s, axis=-1, keepdims=True))
            p = e * (1.0 / jnp.sum(e, axis=-1, keepdims=True))
            outs.append(_nn(p.astype(bf16), vv[:, hs]))
        o_ref[...] = jnp.concatenate(outs, axis=-1).astype(bf16)

    spec = pl.BlockSpec((L, NA_WIDTH), lambda i: (blk, 0))
    return pl.pallas_call(
        body, name=name, grid=(1,), in_specs=[spec, spec, spec],
        out_specs=pl.BlockSpec((L, NA_WIDTH), lambda i: (0, 0)), out_shape=S_((L, NA_WIDTH), bf16),
        compiler_params=_cp(VMEM_MID, ("arbitrary",)),
    )(q, k, v)


def _ctx_attn_bwd(cfg, do, q, k, v, name):
    L = cfg.L
    blk = cfg.S // L

    def body(do_ref, q_ref, k_ref, v_ref, dq_ref, dk_ref, dv_ref):
        dov, qv, kv, vv = do_ref[...], q_ref[...], k_ref[...], v_ref[...]
        dqs, dks, dvs = [], [], []
        for h in range(NA_HEADS):
            hs = slice(h * HEAD_DIM, (h + 1) * HEAD_DIM)
            qh, kh, doh = qv[:, hs], kv[:, hs], dov[:, hs]
            s = _nt(qh, kh)
            e = jnp.exp(s - jnp.max(s, axis=-1, keepdims=True))
            p = e * (1.0 / jnp.sum(e, axis=-1, keepdims=True))
            dp = _nt(doh, vv[:, hs])
            ds = (p * (dp - jnp.sum(p * dp, axis=-1, keepdims=True))).astype(bf16)
            dqs.append(_nn(ds, kh))
            dks.append(_tn(ds, qh))
            dvs.append(_tn(p.astype(bf16), doh))
        dq_ref[...] = jnp.concatenate(dqs, axis=-1)
        dk_ref[...] = jnp.concatenate(dks, axis=-1)
        dv_ref[...] = jnp.concatenate(dvs, axis=-1)

    spec = pl.BlockSpec((L, NA_WIDTH), lambda i: (blk, 0))
    ospec = pl.BlockSpec((L, NA_WIDTH), lambda i: (0, 0))
    return pl.pallas_call(
        body, name=name, grid=(1,), in_specs=[spec, spec, spec, spec],
        out_specs=[ospec, ospec, ospec], out_shape=[S_((L, NA_WIDTH), f32)] * 3,
        compiler_params=_cp(VMEM_MID, ("arbitrary",)),
    )(do, q, k, v)


def _pool_centered(u, band, inv):
    return _split_sum(_nn, band, u) * inv - u


def _split_sum(mm, band, t):
    hi = t.astype(bf16)
    lo = (t - hi.astype(f32)).astype(bf16)
    s = mm(band, jnp.concatenate([hi, lo], axis=1))
    n = t.shape[1]
    return s[:, :n] + s[:, n:]


def _pool_mix(u_ref, band_ref, inv_ref, w_ref, ps_ref):
    C = POOL_CH
    outs = []
    for g in range(POOL_GROUPS):
        d = _pool_centered(u_ref[:, g * C:(g + 1) * C], band_ref[0, g], inv_ref[0, g])
        outs.append(_nn(d.astype(bf16), w_ref[g].astype(bf16)) * ps_ref[:, g * C:(g + 1) * C])
    return jnp.concatenate(outs, axis=-1).astype(bf16)


def _pool_bwd(cfg, dmix, u, band, inv, w_pool, pool_scale, with_ctx, name):
    TM = cfg.TM
    nt = cfg.ntiles(with_ctx)
    C = POOL_CH

    def body(dy_ref, u_ref, band_ref, inv_ref, w_ref, ps_ref, du_ref, dw_ref, dps_ref):
        @pl.when(pl.program_id(0) == 0)
        def _():
            dw_ref[...] = jnp.zeros_like(dw_ref)
            dps_ref[...] = jnp.zeros_like(dps_ref)

        dus, dpss = [], []
        for g in range(POOL_GROUPS):
            gs = slice(g * C, (g + 1) * C)
            band_g, inv_g = band_ref[0, g], inv_ref[0, g]
            db = _pool_centered(u_ref[:, gs], band_g, inv_g).astype(bf16)
            wb = w_ref[g].astype(bf16)
            dy = dy_ref[:, gs].astype(f32)
            dpss.append(_rsum(dy * _nn(db, wb)))
            dys = (dy * ps_ref[:, gs]).astype(bf16)
            dw_ref[g] += _tn(db, dys)
            dd = _nt(dys, wb)
            dus.append(_split_sum(_tn, band_g, dd * inv_g) - dd)
        du_ref[...] = jnp.concatenate(dus, axis=-1)
        dps_ref[...] += jnp.concatenate(dpss, axis=-1)

    typ4 = lambda i: (jnp.minimum(i // cfg.nxt, 1), 0, 0, 0)
    return pl.pallas_call(
        body, name=name, grid=(nt,),
        in_specs=[pl.BlockSpec((TM, POOL_WIDTH), lambda i: (i, 1)), pl.BlockSpec((TM, POOL_WIDTH), lambda i: (i, 0)),
                  pl.BlockSpec((1, POOL_GROUPS, TM, TM), typ4), pl.BlockSpec((1, POOL_GROUPS, TM, 1), typ4),
                  pl.BlockSpec((POOL_GROUPS, C, C), lambda i: (0, 0, 0)), pl.BlockSpec((1, POOL_WIDTH), lambda i: (0, 0))],
        out_specs=[pl.BlockSpec((TM, POOL_WIDTH), lambda i: (i, 0)), pl.BlockSpec((POOL_GROUPS, C, C), lambda i: (0, 0, 0)),
                   pl.BlockSpec((1, POOL_WIDTH), lambda i: (0, 0))],
        out_shape=[S_((nt * TM, POOL_WIDTH), f32), S_((POOL_GROUPS, C, C), f32), S_((1, POOL_WIDTH), f32)],
        compiler_params=_cp(VMEM_MID, ("arbitrary",)),
    )(dmix, u, band, inv, w_pool, pool_scale)


def _tmpost_fwd(cfg, na_x, na_c, u, band, inv, w_pool, pool_scale, w_out, xs, mods, gvec, name):
    TM, D = cfg.TM, cfg.D
    with_ctx = na_c is not None
    nt = cfg.ntiles(with_ctx)
    R = nt * TM

    def body(*refs):
        if with_ctx:
            nax_ref, nac_ref = refs[:2]
            na = jnp.where(pl.program_id(0) < cfg.nxt, nax_ref[...], nac_ref[...])
        else:
            na = refs[0][...]
        (u_ref, band_ref, inv_ref, wp_ref, ps_ref, w_ref, xs_ref, mods_ref, g_ref,
         out_ref, opre_ref, mix_ref) = refs[2 if with_ctx else 1:]
        pool_v = _pool_mix(u_ref, band_ref, inv_ref, wp_ref, ps_ref)
        mix_ref[:, 0:NA_WIDTH] = na
        mix_ref[:, NA_WIDTH:] = pool_v
        o = _nn(na, w_ref[0:NA_WIDTH, :]) + _nn(pool_v, w_ref[NA_WIDTH:, :])
        opre_ref[...] = o
        ohat, _ = _rms_hat(o)
        out_ref[...] = xs_ref[...] + mods_ref[0][5:6] * (ohat * g_ref[3:4])

    rt = lambda c: pl.BlockSpec((TM, c), lambda i: (i, 0))
    na_specs = [pl.BlockSpec((TM, NA_WIDTH), lambda i: (jnp.minimum(i, cfg.nxt - 1), 0))]
    na_args = [na_x]
    if with_ctx:
        na_specs.append(pl.BlockSpec((TM, NA_WIDTH), lambda i: (0, 0)))
        na_args.append(na_c)
    typ4 = lambda i: (jnp.minimum(i // cfg.nxt, 1), 0, 0, 0)
    pool_specs = [rt(POOL_WIDTH), pl.BlockSpec((1, POOL_GROUPS, TM, TM), typ4), pl.BlockSpec((1, POOL_GROUPS, TM, 1), typ4),
                  pl.BlockSpec((POOL_GROUPS, POOL_CH, POOL_CH), lambda i: (0, 0, 0)), pl.BlockSpec((1, POOL_WIDTH), lambda i: (0, 0))]
    return pl.pallas_call(
        body, name=name, grid=(nt,),
        in_specs=na_specs + pool_specs + [pl.BlockSpec((MIX_WIDTH, D), lambda i: (0, 0)), rt(D),
                                          pl.BlockSpec((1, N_MOD, D), _typ(cfg)), pl.BlockSpec((6, D), lambda i: (0, 0))],
        out_specs=[rt(D), rt(D), rt(MIX_WIDTH)],
        out_shape=[S_((R, D), f32), S_((R, D), f32), S_((R, MIX_WIDTH), bf16)],
        compiler_params=_cp(VMEM_MID, ("arbitrary",)),
    )(*na_args, u, band, inv, w_pool, pool_scale, w_out, xs, mods, gvec)


def _tmpost_bwd(cfg, dout, opre, w_out, mods, gvec, with_ctx, name):
    TM, D = cfg.TM, cfg.D
    nt = cfg.ntiles(with_ctx)
    R = nt * TM
    ntyp = 2 if with_ctx else 1

    def body(do_ref, opre_ref, w_ref, mods_ref, g_ref, dop_ref, dmix_ref, dm_ref, dg_ref):
        i = pl.program_id(0)

        @pl.when(i == 0)
        def _():
            dg_ref[...] = jnp.zeros_like(dg_ref)

        @pl.when((i == 0) | (i == cfg.nxt))
        def _():
            dm_ref[...] = jnp.zeros_like(dm_ref)

        do = do_ref[...]
        g3 = g_ref[3:4]
        ohat, rinv = _rms_hat(opre_ref[...])
        dm_ref[0] += _rsum(do * (ohat * g3))
        dr = mods_ref[0][5:6] * do
        dg_ref[...] += _rsum(dr * ohat)
        dob = _rms_bwd(dr * g3, ohat, rinv).astype(bf16)
        dop_ref[...] = dob
        dmix_ref[...] = _nt(dob, w_ref[...]).astype(bf16)

    rt = lambda c: pl.BlockSpec((TM, c), lambda i: (i, 0))
    return pl.pallas_call(
        body, name=name, grid=(nt,),
        in_specs=[rt(D), rt(D), pl.BlockSpec((MIX_WIDTH, D), lambda i: (0, 0)),
                  pl.BlockSpec((1, N_MOD, D), _typ(cfg)), pl.BlockSpec((6, D), lambda i: (0, 0))],
        out_specs=[rt(D), rt(MIX_WIDTH), pl.BlockSpec((1, 1, D), _typ(cfg)), pl.BlockSpec((1, D), lambda i: (0, 0))],
        out_shape=[S_((R, D), bf16), S_((R, MIX_WIDTH), bf16), S_((ntyp, 1, D), f32), S_((1, D), f32)],
        compiler_params=_cp(VMEM_MID, ("arbitrary",)),
    )(dout, opre, w_out, mods, gvec)


def _modvec_fwd(cvecs, w_mod, b_shard, name):
    nl, D, n = w_mod.shape
    tn = n // 3 if (n % 3 == 0 and (n // 3) % LANES == 0) else n

    def body(c_ref, w_ref, b_ref, o_ref, s_ref):
        cv = c_ref[...]
        sv = cv * _sigmoid(cv)
        s_ref[...] = sv
        o_ref[...] = _nn(sv.astype(bf16), w_ref[...].astype(bf16)) + b_ref[...]

    return pl.pallas_call(
        body, name=name, grid=(nl, n // tn),
        in_specs=[pl.BlockSpec((16, D), lambda l, j: (0, 0)), pl.BlockSpec((None, D, tn), lambda l, j: (l, 0, j)),
                  pl.BlockSpec((None, 1, tn), lambda l, j: (l, 0, j))],
        out_specs=[pl.BlockSpec((None, 16, tn), lambda l, j: (l, 0, j)), pl.BlockSpec((16, D), lambda l, j: (0, 0))],
        out_shape=[S_((nl, 16, n), f32), S_((16, D), f32)],
        compiler_params=_cp(VMEM_MID, ("arbitrary", "arbitrary")),
    )(cvecs, w_mod, b_shard)


def _modvec_bwd(s_t, dm, w_mod, name):
    nl, D, n = w_mod.shape
    tn = n // 3 if (n % 3 == 0 and (n // 3) % LANES == 0) else n

    def body(s_ref, dm_ref, w_ref, gw_ref, gc_ref):
        @pl.when(pl.program_id(1) == 0)
        def _():
            gc_ref[...] = jnp.zeros_like(gc_ref)
        dmv = dm_ref[...]
        gw_ref[...] = jnp.dot(s_ref[...], dmv, preferred_element_type=f32, precision=lax.Precision.HIGHEST)
        gc_ref[...] += _nt(dmv[8:16].astype(bf16), w_ref[...].astype(bf16))

    return pl.pallas_call(
        body, name=name, grid=(nl, n // tn),
        in_specs=[pl.BlockSpec((D, 16), lambda l, j: (0, 0)), pl.BlockSpec((None, 16, tn), lambda l, j: (l, 0, j)),
                  pl.BlockSpec((None, D, tn), lambda l, j: (l, 0, j))],
        out_specs=[pl.BlockSpec((None, D, tn), lambda l, j: (l, 0, j)), pl.BlockSpec((None, 8, D), lambda l, j: (l, 0, 0))],
        out_shape=[S_((nl, D, n), f32), S_((nl, 8, D), f32)],
        compiler_params=_cp(VMEM_MID, ("arbitrary", "arbitrary")),
    )(s_t, dm, w_mod)


def _as2d(a):
    n = a.size
    if a.ndim >= 2 and a.shape[-1] % LANES == 0:
        return a.reshape(-1, a.shape[-1])
    if n % LANES == 0:
        return a.reshape(-1, LANES)
    return a.reshape(-1, a.shape[-1]) if a.ndim >= 2 else a.reshape(1, n)


def _row_tile(r, c, budget_elems=512 * 1024):
    if r * c <= budget_elems or r % 8 != 0:
        return r
    t = r
    while t * c > budget_elems and t % 16 == 0:
        t //= 2
    return t


def _div_tile(r, c, budget_elems, mult=16):
    best = None
    for t in range(mult, r + 1, mult):
        if r % t == 0 and t * c <= budget_elems:
            best = t
    return best if best is not None else r


def _chip_index():
    return 2 * lax.axis_index("x") + lax.axis_index("y")


def _cast_into_place(shards, lead, axis, name):
    r, c = shards.shape[-2:]
    tr = _div_tile(r, c, 768 * 1024)
    nr = r // tr
    out_map = (lambda i: (i, _chip_index())) if axis == 1 else (lambda i: (_chip_index() * nr + i, 0))
    full2 = (r, c * N_CHIPS) if axis == 1 else (r * N_CHIPS, c)

    def body(a_ref, o_ref):
        o_ref[...] = a_ref[...].astype(bf16)

    return pl.pallas_call(
        body, name=name, grid=(nr,),
        in_specs=[pl.BlockSpec((None,) * len(lead) + (tr, c), lambda i: tuple(lead) + (i, 0))],
        out_specs=pl.BlockSpec((tr, c), out_map),
        out_shape=S_(full2, bf16), compiler_params=_cp(VMEM_MID, ("arbitrary",)),
    )(shards)


def _sum_devices8(own, land, axis, into, lead, name):
    _, rh, cs = land.shape
    tr = _div_tile(rh, cs, 400 * 1024)
    nr = rh // tr
    core = lambda: lax.axis_index("c")
    if axis == 1:
        own_map = lambda i: (core() * nr + i, _chip_index())
    else:
        own_map = lambda i: (_chip_index() * 2 * nr + core() * nr + i, 0)
    nl = len(lead)

    def land_spec(j):
        return pl.BlockSpec((None, tr, cs), lambda i: ((2 * _chip_index() + core() + j) % N_DEV, i, 0))

    def body(own_ref, *rest):
        acc = own_ref[...]
        for p_ref in rest[:N_DEV - 1]:
            acc = acc + p_ref[...].astype(f32)
        rest[-1][...] = acc

    return pl.pallas_call(
        body, name=name, grid=(nr,),
        in_specs=[pl.BlockSpec((tr, cs), own_map)] + [land_spec(j) for j in range(1, N_DEV)] + [ANY],
        out_specs=pl.BlockSpec((None,) * nl + (tr, cs), lambda i: tuple(lead) + (core() * nr + i, 0)),
        out_shape=S_(into.shape, f32), input_output_aliases={N_DEV: 0},
        compiler_params=_cp(VMEM_MID, ("arbitrary",)),
    )(own, *([land] * (N_DEV - 1)), into)


def _adamw(w, g, m, v, name, emit_grad=False):
    shape = w.shape
    w2, g2, m2, v2 = _as2d(w), _as2d(g), _as2d(m), _as2d(v)
    r, c = w2.shape
    tr = _row_tile(r, c, 256 * 1024)
    c1 = 1.0 - ADAM_B1 ** ADAM_STEP
    c2 = 1.0 - ADAM_B2 ** ADAM_STEP
    n_out = 4 if emit_grad else 3

    def body(w_ref, g_ref, m_ref, v_ref, d_ref, mo_ref, vo_ref, *go_ref):
        gv = g_ref[...]
        mn = ADAM_B1 * m_ref[...] + (1.0 - ADAM_B1) * gv
        vn = ADAM_B2 * v_ref[...] + (1.0 - ADAM_B2) * (gv * gv)
        mo_ref[...] = mn
        vo_ref[...] = vn
        d_ref[...] = -ADAM_LR * ((mn / c1) / (jnp.sqrt(vn / c2) + ADAM_EPS) + ADAM_WD * w_ref[...])
        if emit_grad:
            go_ref[0][...] = gv

    spec = pl.BlockSpec((tr, c), lambda i: (i, 0))
    outs = pl.pallas_call(body, name=name, grid=(r // tr,), in_specs=[spec] * 4, out_specs=[spec] * n_out,
                          out_shape=[S_((r, c), f32)] * n_out, compiler_params=_cp(VMEM_MID, ("arbitrary",)))(w2, g2, m2, v2)
    return tuple(o.reshape(shape) for o in outs)


def _sum_devices(gathered, name):
    _, r, c = gathered.shape

    def body(a_ref, o_ref):
        acc = a_ref[0]
        for j in range(1, N_DEV):
            acc = acc + a_ref[j]
        o_ref[...] = acc

    tr = _row_tile(r, c, 64 * 1024)
    return pl.pallas_call(
        body, name=name, grid=(r // tr,),
        in_specs=[pl.BlockSpec((N_DEV, tr, c), lambda i: (0, i, 0))], out_specs=pl.BlockSpec((tr, c), lambda i: (i, 0)),
        out_shape=S_((r, c), f32), compiler_params=_cp(VMEM_MID, ("arbitrary",)))(gathered)


def _all_gather_small(block, name):
    m_per, n = block.shape

    def body(x_ref, out_ref, send_sems, recv_sems, local_sem):
        x, y, c = _mesh_pos()
        me, sibling = (x, y, c), (x, y, 1 - c)
        chips = [(1 - x, y), (x, 1 - y), (1 - x, 1 - y)]

        def rows(px, py, pc):
            return out_ref.at[pl.ds((4 * px + 2 * py + pc) * m_per, m_per), :]

        def copy(k, blk, to, src=None):
            return pltpu.make_async_remote_copy(
                src_ref=rows(*blk) if src is None else src, dst_ref=rows(*blk),
                send_sem=send_sems.at[k], recv_sem=recv_sems.at[k], device_id=to, device_id_type=MESH)

        mine = pltpu.make_async_copy(x_ref, rows(*me), local_sem)
        mine.start()
        first = [copy(0, me, sibling, src=x_ref)]
        first += [copy(1 + j, me, (*chip, c), src=x_ref) for j, chip in enumerate(chips)]
        for cp in first:
            cp.start()
        passed = [copy(4 + j, (*chip, c), sibling) for j, chip in enumerate(chips)]
        for j, chip in enumerate(chips):
            copy(1 + j, (*chip, c), me).wait_recv()
            passed[j].start()
        copy(0, sibling, me).wait_recv()
        for j, chip in enumerate(chips):
            copy(4 + j, (*chip, 1 - c), me).wait_recv()
        for cp in first + passed:
            cp.wait_send()
        mine.wait()

    return pl.pallas_call(
        body, name=name, out_shape=S_((N_DEV * m_per, n), block.dtype),
        in_specs=[pl.BlockSpec(memory_space=pltpu.VMEM)], out_specs=pl.BlockSpec(memory_space=pltpu.VMEM),
        scratch_shapes=[pltpu.SemaphoreType.DMA((7,)), pltpu.SemaphoreType.DMA((7,)), pltpu.SemaphoreType.DMA],
        compiler_params=_cp(VMEM_MID),
    )(block)


def _pack_rows(arrays):
    flat = jnp.concatenate([a.reshape(-1) for a in arrays])
    pad = (-flat.size) % (8 * LANES)
    return jnp.pad(flat, (0, pad)).reshape(-1, LANES)


def _unpack_rows(packed, shapes):
    flat = packed.reshape(-1)
    out, off = [], 0
    for s in shapes:
        n = int(np.prod(s))
        out.append(flat[off:off + n].reshape(s))
        off += n
    return out


W_AXIS = {"gu": 1, "dn": 0, "wi": 1, "wo": 0}
SMALL_NAMES = ("dmods", "dg", "drpb", "dwp", "dps")


def _half_merge(bufs, name):
    nt = len(bufs)

    def body(*refs):
        outs = refs[nt:2 * nt]
        send_sems, recv_sems = refs[2 * nt:]
        x, y, c = _mesh_pos()

        def half(ref, h):
            rh = ref.shape[-2] // 2
            return ref.at[(slice(None),) * (len(ref.shape) - 2) + (pl.ds(h * rh, rh), slice(None))]

        cps = []
        for t in range(nt):
            cp = pltpu.make_async_remote_copy(
                src_ref=half(outs[t], c), dst_ref=half(outs[t], c), send_sem=send_sems.at[t], recv_sem=recv_sems.at[t],
                device_id=(x, y, 1 - c), device_id_type=MESH)
            cp.start()
            cps.append(cp)
        for t in range(nt):
            pltpu.make_async_remote_copy(
                src_ref=half(outs[t], 1 - c), dst_ref=half(outs[t], 1 - c), send_sem=send_sems.at[t], recv_sem=recv_sems.at[t],
                device_id=(x, y, 1 - c), device_id_type=MESH).wait_recv()
        for cp in cps:
            cp.wait_send()

    return pl.pallas_call(
        body, name=name, in_specs=[ANY] * nt, out_specs=[ANY] * nt, out_shape=[S_(b.shape, f32) for b in bufs],
        input_output_aliases={t: t for t in range(nt)},
        scratch_shapes=[pltpu.SemaphoreType.DMA((nt,)), pltpu.SemaphoreType.DMA((nt,))],
        compiler_params=_cp(VMEM_MID),
    )(*bufs)


def _local_step(cfg, x_lat, x_ctx, target, mods, norm_g, W, G, na_rpb, w_pool, pool_scale):
    S, L, T, D, F = cfg.S, cfg.L, cfg.T, cfg.D, cfg.F
    depth = norm_g.shape[0]
    cos, sin = _rope_tables(S, L)
    band, inv = _pool_tables(cfg.TM, L)
    flip, sel = _rpb_reduce_tables()

    assert depth == 2, "the carrier schedules below are written for two layers"
    fwd_carry = {"ffn_fwd_0_0": [("wi", 0), ("wo", 0), ("gu", 0, 1), ("dn", 0, 1)],
                 "na_fwd_0": [("gu", 1, 0), ("dn", 1, 0)],
                 "ffn_fwd_0_1": [("wi", 1), ("wo", 1), ("gu", 1, 1), ("dn", 1, 1)]}
    bwd_carry = {"na_bwd_1": [("gu", 1, 1), ("dn", 1, 1)], "ffn_bwd_1_0": [("wi", 1), ("wo", 1)],
                 "ffn_bwd_0_1": [("gu", 1, 0), ("dn", 1, 0)], "na_bwd_0": [("gu", 0, 1), ("dn", 0, 1)],
                 "ffn_bwd_0_0": [("wi", 0), ("wo", 0)], "wgrad_dn_0_0": [("gu", 0, 0)]}
    last_scatter = [("dn", 0, 0)]
    tag = lambda key: "_".join(str(p) for p in key)
    g_f32, g_b16 = {}, {}

    def gather_on(name):
        keys = fwd_carry.get(name)
        return None if keys is None else _gather_comm([W[k_] for k_ in keys], [W_AXIS[k_[0]] for k_ in keys])

    def gathered(name, res):
        if name in fwd_carry:
            W.update(zip(fwd_carry[name], res))

    def scatter_on(name):
        keys = bwd_carry.get(name)
        return None if keys is None else _scatter_comm([g_b16[k_] for k_ in keys], [W_AXIS[k_[0]] for k_ in keys])

    def scattered(keys, lands):
        for key, land in zip(keys, lands):
            G[key[0]] = _sum_devices8(g_f32[key], land, W_AXIS[key[0]], G[key[0]], key[1:], f"sum8_{tag(key)}")

    small_landed = []

    def wgrad(key, a, b, rows, other_comm=None):
        name = f"wgrad_{tag(key)}"
        if other_comm is not None:
            assert name not in bwd_carry
            (g_f32[key], g_b16[key]), res = _wgrad(a, b, rows, name, other_comm)
            small_landed.extend(res)
            return
        (g_f32[key], g_b16[key]), lands = _wgrad(a, b, rows, name, scatter_on(name))
        scattered(bwd_carry.get(name, ()), lands)

    saved = []
    xs, xs_ctx = x_lat, x_ctx
    for l in range(depth):
        last = l == depth - 1
        wc = not last
        gvec = norm_g[l]
        ps = pool_scale[l].reshape(1, POOL_WIDTH)
        bexp = _expand_rpb(na_rpb[l], f"bias_expand_{l}")
        name = f"ffn_fwd_{l}_0"
        (xs1, hb1, z1, y1), res = _ffn_fwd(cfg, xs, mods[l], gvec, W["gu", l, 0], W["dn", l, 0], 0, 0, True, name,
                                           gather_on(name), xs_ctx=xs_ctx)
        gathered(name, res)
        hb2, q, k, v, u = _tmpre_fwd(cfg, xs1, mods[l], gvec, W["wi", l], cos, sin, f"tmpre_fwd_{l}")
        name = f"na_fwd_{l}"
        (na_x,), res = _na_fwd(cfg, q, k, v, bexp, name, gather_on(name))
        gathered(name, res)
        na_c = _ctx_attn_fwd(cfg, q, k, v, f"ctx_attn_fwd_{l}") if wc else None
        xs2, opre, mix = _tmpost_fwd(cfg, na_x, na_c, u, band, inv, w_pool[l], ps, W["wo", l], xs1, mods[l], gvec,
                                     f"tmpost_fwd_{l}")
        name = f"ffn_fwd_{l}_1"
        outs, res = _ffn_fwd(cfg, xs2, mods[l], gvec, W["gu", l, 1], W["dn", l, 1], 6, 4, wc, name, gather_on(name),
                             loss_target=target if last else None)
        xs3, hb3, z3, y3 = outs[:4]
        gathered(name, res)
        saved.append(dict(xs=xs, xs_ctx=xs_ctx, xs1=xs1, xs2=xs2, hb1=hb1, z1=z1, y1=y1, hb2=hb2, q=q, k=k, v=v, u=u, mix=mix,
                          opre=opre, hb3=hb3, z3=z3, y3=y3, bexp=bexp, ps=ps, gvec=gvec))
        xs, xs_ctx = xs3, None

    dxs, loss_blk = xs, outs[4]

    small = [None] * depth
    for l in reversed(range(depth)):
        last = l == depth - 1
        wc = not last
        sv = saved[l]
        gvec = sv["gvec"]
        rows_b = cfg.T if wc else cfg.S
        name = f"ffn_bwd_{l}_1"
        (dxs2, dz, dyb, ab, dm678, dg45), lands = _ffn_bwd(cfg, dxs, sv["xs2"], sv["z3"], sv["y3"], mods[l], gvec,
                                                           W["gu", l, 1], W["dn", l, 1], 6, 4, wc, name, scatter_on(name))
        scattered(bwd_carry.get(name, ()), lands)
        wgrad(("gu", l, 1), sv["hb3"], dz, rows_b)
        wgrad(("dn", l, 1), ab, dyb, rows_b)
        dop, dmix, dm5, dg3 = _tmpost_bwd(cfg, dxs2, sv["opre"], W["wo", l], mods[l], gvec, wc, f"tmpost_bwd_{l}")
        wgrad(("wo", l), sv["mix"], dop, rows_b)
        du, dwp, dps = _pool_bwd(cfg, dmix, sv["u"], band, inv, w_pool[l], sv["ps"], wc, f"pool_bwd_{l}")
        name = f"na_bwd_{l}"
        (dq, dk, dv, dkc, dvc, dbexp), lands = _na_bwd(cfg, dmix, sv["q"], sv["k"], sv["v"], sv["bexp"], name, scatter_on(name))
        scattered(bwd_carry.get(name, ()), lands)
        drpb = _rpb_reduce(dbexp, flip, sel, f"rpb_reduce_{l}")
        if wc:
            dqc, dkc2, dvc2 = _ctx_attn_bwd(cfg, dmix, sv["q"], sv["k"], sv["v"], f"ctx_attn_bwd_{l}")
            ctx_terms = ([dqc], [dkc, dkc2], [dvc, dvc2])
        else:
            ctx_terms = ([], [dkc], [dvc])
        dxs1, dproj, dm34, dg2 = _tmpre_bwd(cfg, (dq, dk, dv, du), ctx_terms, wc, cos, sin, W["wi", l], sv["xs1"], mods[l], gvec,
                                            dxs2, wc, f"tmpre_bwd_{l}")
        wgrad(("wi", l), sv["hb2"], dproj, cfg.T)
        name = f"ffn_bwd_{l}_0"
        (dxs, dz, dyb, ab, dm012, dg01), lands = _ffn_bwd(cfg, dxs1, sv["xs"], sv["z1"], sv["y1"], mods[l], gvec,
                                                          W["gu", l, 0], W["dn", l, 0], 0, 0, True, name, scatter_on(name),
                                                          xs_ctx=sv["xs_ctx"])
        scattered(bwd_carry.get(name, ()), lands)
        if not wc:
            zero = lambda a: jnp.concatenate([a, jnp.zeros_like(a)], axis=0)
            dm5, dm678 = zero(dm5), zero(dm678)
        dmods = jnp.concatenate([dm012, dm34, dm5, dm678], axis=1)
        dgs = jnp.concatenate([dg01, dg2, dg3, dg45], axis=0)
        small[l] = dict(dmods=dmods, dg=dgs, drpb=drpb, dwp=dwp, dps=dps)
        small_gather = None
        if l == 0:
            parts = [jnp.stack([small[j][n_] for j in range(depth)]) for n_ in SMALL_NAMES]
            packed = _pack_rows(parts)
            small_gather = _allgather_comm(packed)
        wgrad(("gu", l, 0), sv["hb1"], dz, cfg.T, small_gather)
        wgrad(("dn", l, 0), ab, dyb, cfg.T)
    lands = _comm_only(_scatter_comm([g_b16[k_] for k_ in last_scatter], [W_AXIS[k_[0]] for k_ in last_scatter]), "scatter_last")
    scattered(last_scatter, lands)
    kinds = ("gu", "dn", "wi", "wo")
    merged = _half_merge([G[k_] for k_ in kinds], "merge_halves")
    return loss_blk, dxs, dict(zip(kinds, merged)), (packed, [p.shape for p in parts], small_landed[0])


def kernel(x, c, ctx, c_ctx, w_mod, b_mod, norm_g, w_ffn_gate_up, w_ffn_down, w_in, w_out, na_rpb, w_pool, pool_scale, loss_target, m_c_ctx, m_w_mod, m_b_mod, m_norm_g, m_w_ffn_gate_up, m_w_ffn_down, m_w_in, m_w_out, m_na_rpb, m_w_pool, m_pool_scale, v_c_ctx, v_w_mod, v_b_mod, v_norm_g, v_w_ffn_gate_up, v_w_ffn_down, v_w_in, v_w_out, v_na_rpb, v_w_pool, v_pool_scale):
    S, D = x.shape[1], x.shape[2]
    L = ctx.shape[1]
    depth = w_mod.shape[0]
    F = w_ffn_down.shape[2] * N_CHIPS
    nmod = w_mod.shape[2]
    gsh = norm_g.shape[2]
    cfg = _Cfg(S, L, D, F)
    mx, my, mc = _mesh_pos()
    chip = 2 * mx + my
    dev = 4 * mx + 2 * my + mc

    W = {}
    for l in range(depth):
        for i in range(2):
            W["gu", l, i] = _cast_into_place(w_ffn_gate_up, (l, i), W_AXIS["gu"], f"cast_gu_{l}_{i}")
            W["dn", l, i] = _cast_into_place(w_ffn_down, (l, i), W_AXIS["dn"], f"cast_dn_{l}_{i}")
        W["wi", l] = _cast_into_place(w_in, (l,), W_AXIS["wi"], f"cast_wi_{l}")
        W["wo", l] = _cast_into_place(w_out, (l,), W_AXIS["wo"], f"cast_wo_{l}")
    first = [("gu", 0, 0), ("dn", 0, 0)]
    W.update(zip(first, _comm_only(_gather_comm([W[k_] for k_ in first], [W_AXIS[k_[0]] for k_ in first]), "gather_first")))
    G = {"gu": lax.empty(w_ffn_gate_up.shape, f32), "dn": lax.empty(w_ffn_down.shape, f32),
         "wi": lax.empty(w_in.shape, f32), "wo": lax.empty(w_out.shape, f32)}

    cg_packed = _pack_rows([c, norm_g])
    cg_all = _all_gather_small(cg_packed, "gather_c_norm_g").reshape(N_DEV, -1)
    c_all = cg_all[:, :D]
    ng = cg_all[:, D:D + norm_g.size].reshape(N_DEV, depth, 6, gsh)
    norm_g_all = jnp.concatenate([ng[2 * j] for j in range(N_CHIPS)], axis=-1)
    cvecs = jnp.concatenate([c_all, c_ctx[None], jnp.zeros((7, D), f32)], axis=0)
    b_shard = lax.dynamic_slice_in_dim(b_mod, chip * nmod, nmod, axis=1).reshape(depth, 1, nmod)
    m_part, silu_c = _modvec_fwd(cvecs, w_mod, b_shard, "modvec_fwd")
    m_all = _all_gather_small(m_part.reshape(depth * 16, nmod), "gather_mod").reshape(N_DEV, depth, 16, nmod)
    m_full = jnp.concatenate([m_all[2 * j] for j in range(N_CHIPS)], axis=-1)
    m_mine = lax.dynamic_index_in_dim(m_full, dev, axis=1, keepdims=False)
    mods = jnp.stack([m_mine, m_full[:, 8]], axis=1).reshape(depth, 2, N_MOD, D)

    loss_blk, dx_lat, wgrads, small = _local_step(cfg, x[0], ctx[0], loss_target[0], mods, norm_g_all, W, G,
                                                  na_rpb, w_pool, pool_scale)
    loss = lax.psum(loss_blk[0, 0], ("x", "y", "c"))
    grad_x = dx_lat[None]

    g_gu, g_dn, g_wi, g_wo = wgrads["gu"], wgrads["dn"], wgrads["wi"], wgrads["wo"]
    packed, shapes, landed = small
    gathered = lax.dynamic_update_index_in_dim(landed, packed, dev, 0)
    total = _unpack_rows(_sum_devices(gathered, "sum_small"), shapes)
    dmods_sum, dg_sum, drpb_sum, dwp_sum, dps_sum = total
    dmods_each = jnp.stack([_unpack_rows(gathered[j], shapes[:1])[0] for j in range(N_DEV)])
    dm_rows = jnp.concatenate([jnp.transpose(dmods_each[:, :, 0], (1, 0, 2, 3)).reshape(depth, N_DEV, N_MOD * D),
                               dmods_sum[:, 1].reshape(depth, 1, N_MOD * D),
                               jnp.zeros((depth, 7, N_MOD * D), f32)], axis=1)
    dm_shard = lax.dynamic_slice_in_dim(dm_rows, chip * nmod, nmod, axis=2)
    grad_w_mod, gc_part = _modvec_bwd(silu_c.T, dm_shard, w_mod, "modvec_bwd")
    gc_all = _all_gather_small(gc_part.reshape(depth * 8, D), "gather_gc").reshape(N_DEV, depth, 8, D)
    grad_b_mod, grad_c_ctx = _small_finish(dm_rows, gc_all, c_ctx)
    grad_norm_g = lax.dynamic_slice_in_dim(dg_sum, chip * gsh, gsh, axis=2)
    grad_na_rpb = drpb_sum[:, :, :2 * NA_KH - 1, :2 * NA_KW - 1]
    grad_w_pool = dwp_sum
    grad_pool_scale = dps_sum.reshape(depth, POOL_WIDTH)

    grads = [grad_c_ctx, grad_w_mod, grad_b_mod, grad_norm_g, g_gu, g_dn, g_wi, g_wo, grad_na_rpb, grad_w_pool, grad_pool_scale]
    ws = [c_ctx, w_mod, b_mod, norm_g, w_ffn_gate_up, w_ffn_down, w_in, w_out, na_rpb, w_pool, pool_scale]
    ms = [m_c_ctx, m_w_mod, m_b_mod, m_norm_g, m_w_ffn_gate_up, m_w_ffn_down, m_w_in, m_w_out, m_na_rpb, m_w_pool, m_pool_scale]
    vs = [v_c_ctx, v_w_mod, v_b_mod, v_norm_g, v_w_ffn_gate_up, v_w_ffn_down, v_w_in, v_w_out, v_na_rpb, v_w_pool, v_pool_scale]
    tags = ["c_ctx", "w_mod", "b_mod", "norm_g", "gate_up", "down", "w_in", "w_out", "na_rpb", "w_pool", "pool_scale"]
    merged = ("gate_up", "down", "w_in", "w_out")
    upd = [_adamw(w_, g_, m_, v_, f"adamw_{t}", emit_grad=t in merged) for w_, g_, m_, v_, t in zip(ws, grads, ms, vs, tags)]
    grads = [u_[3] if t in merged else g_ for g_, u_, t in zip(grads, upd, tags)]
    return (loss, grad_x, *grads, *[u_[0] for u_ in upd], *[u_[1] for u_ in upd], *[u_[2] for u_ in upd])


def _small_finish(dm_rows, gc_all, c_ctx):
    depth, _, n = dm_rows.shape
    D = c_ctx.shape[0]

    def body(dm_ref, gc_ref, c_ref, gb_ref, gcx_ref):
        acc = dm_ref[:, 0]
        for j in range(1, N_DEV + 1):
            acc = acc + dm_ref[:, j]
        gb_ref[...] = acc
        t = jnp.zeros((1, D), f32)
        for l in range(depth):
            for j in range(N_CHIPS):
                t = t + gc_ref[2 * j, l, 0:1, :]
        cv = c_ref[...]
        sg = _sigmoid(cv)
        gcx_ref[...] = t * (sg * (1.0 + cv * (1.0 - sg)))

    gb, gcx = pl.pallas_call(
        body, name="small_finish",
        out_shape=[S_((depth, n), f32), S_((1, D), f32)],
        compiler_params=_cp(VMEM_MID),
    )(dm_rows, gc_all, c_ctx.reshape(1, D))
    return gb, gcx.reshape(D)
```

```python
import functools

import numpy as np
import jax
import jax.numpy as jnp
from jax import lax
from jax.experimental import pallas as pl
from jax.experimental.pallas import tpu as pltpu

f32, bf16 = jnp.float32, jnp.bfloat16

GRID_W = 64
N_MOD = 9
NA_HEADS = 8
HEAD_DIM = 64
NA_WIDTH = NA_HEADS * HEAD_DIM
NA_KH = 8
NA_KW = 16
POOL_GROUPS = 4
POOL_CH = 128
POOL_WIDTH = POOL_GROUPS * POOL_CH
POOL_WINDOWS = (2, 4, 8, 16)
IN_WIDTH = 3 * NA_WIDTH + POOL_WIDTH
MIX_WIDTH = NA_WIDTH + POOL_WIDTH
ROPE_THETA = 10000.0
ROPE_PAIRS = HEAD_DIM // 4
RMS_EPS = 1e-6
NEG_INF = -1e30
ADAM_LR, ADAM_B1, ADAM_B2, ADAM_EPS, ADAM_WD, ADAM_STEP = 0.001, 0.9, 0.999, 1e-08, 0.01, 10

N_DEV = 8
N_CHIPS = 4
LANES = 128
MIB = 1024 * 1024
VMEM_BIG = 52 * MIB
VMEM_MID = 40 * MIB
WGRAD_TN = 1408
WGRAD_TK = 2816
WGRAD_SLACK = 6 * MIB
ELEMWISE_BLOCK = 256 * 1024
MESH = pl.DeviceIdType.MESH
ANY = pl.BlockSpec(memory_space=pl.ANY)
S_ = jax.ShapeDtypeStruct


def _cp(vmem=VMEM_MID, sem=None):
    return pltpu.CompilerParams(vmem_limit_bytes=vmem, dimension_semantics=sem)


def _sigmoid(x):
    return 0.5 * jnp.tanh(0.5 * x) + 0.5


def _rms_hat(x):
    rinv = lax.rsqrt(jnp.mean(x * x, axis=-1, keepdims=True) + RMS_EPS)
    return x * rinv, rinv


def _rms_bwd(dxhat, xhat, rinv):
    return rinv * (dxhat - xhat * jnp.mean(dxhat * xhat, axis=-1, keepdims=True))


def _rsum(a):
    return jnp.sum(a, axis=0, keepdims=True)


def _nt(a, b):
    return lax.dot_general(a, b, (((1,), (1,)), ((), ())), preferred_element_type=f32)


def _tn(a, b):
    return lax.dot_general(a, b, (((0,), (0,)), ((), ())), preferred_element_type=f32)


def _nn(a, b):
    return jnp.dot(a, b, preferred_element_type=f32)


def _swap16(x):
    lane = lax.broadcasted_iota(jnp.int32, x.shape, 1)
    n = x.shape[1]
    return jnp.where((lane % 32) < 16, pltpu.roll(x, n - 16, 1), pltpu.roll(x, 16, 1))


def _rope_tables(s_len, l_len):
    t = np.arange(s_len)
    inv = ROPE_THETA ** (-np.arange(ROPE_PAIRS, dtype=np.float32) / ROPE_PAIRS)
    ang_r = (t // GRID_W).astype(np.float32)[:, None] * inv
    ang_c = (t % GRID_W).astype(np.float32)[:, None] * inv
    cos = np.concatenate([np.cos(ang_r), np.cos(ang_r), np.cos(ang_c), np.cos(ang_c)], axis=-1)
    sin = np.concatenate([-np.sin(ang_r), np.sin(ang_r), -np.sin(ang_c), np.sin(ang_c)], axis=-1)
    cos = np.concatenate([cos, np.ones((l_len, HEAD_DIM), np.float32)], axis=0)
    sin = np.concatenate([sin, np.zeros((l_len, HEAD_DIM), np.float32)], axis=0)
    return (jnp.asarray(np.tile(cos, (1, 2)), f32), jnp.asarray(np.tile(sin, (1, 2)), f32))


def _pool_tables(tm, l_len):
    band = np.zeros((2, POOL_GROUPS, tm, tm), np.float32)
    inv = np.zeros((2, POOL_GROUPS, tm, 1), np.float32)
    for typ, length in ((0, GRID_W), (1, l_len)):
        for g, w in enumerate(POOL_WINDOWS):
            for t in range(tm):
                base, p = (t // length) * length, t % length
                lo = min(max(p - w // 2, 0), length)
                hi = min(max(p - w // 2 + w, 0), length)
                band[typ, g, t, base + lo:base + hi] = 1.0
                inv[typ, g, t, 0] = 1.0 / (hi - lo)
    return jnp.asarray(band, bf16), jnp.asarray(inv, f32)


NA_QR = 4
NA_WR = NA_KH + NA_QR - 1
NA_TYPES = 3
NA_SEL_ROWS = 136
NA_WPAD = 768


def _rpb_index_tables():
    j = np.arange(GRID_W)
    col_start = np.clip(j - NA_KW // 2, 0, GRID_W - NA_KW)
    valid = (j[None, :] >= col_start[:, None]) & (j[None, :] < col_start[:, None] + NA_KW)
    dc = np.clip(j[None, :] - j[:, None] + NA_KW - 1, 0, 2 * NA_KW - 2)
    i = np.arange(NA_QR)[:, None]
    kk = np.arange(NA_WR)[None, :]
    off = np.stack([np.zeros_like(i), i, np.full_like(i, NA_QR - 1)])
    d = np.stack([kk - i + NA_KH - 1, kk - i + NA_KH - 1 - NA_QR, kk - i])
    row_ok = (kk[None] >= off) & (kk[None] < off + NA_KH)
    assert (d[row_ok] >= 0).all() and (d[row_ok] <= 2 * NA_KH - 2).all()
    return valid, dc, d, row_ok


def _expand_rpb(rpb, name):
    _, _, d, row_ok = _rpb_index_tables()
    heads, nd, ne = rpb.shape
    w = GRID_W
    v = jnp.pad(rpb, ((0, 0), (0, 0), (w - NA_KW, 2 * w - (w - NA_KW) - ne)))
    x = jnp.broadcast_to(v[:, :, None, :], (heads, nd, w, 2 * w)).reshape(heads, nd, 2 * w * w)
    t = x[:, :, :w * (2 * w - 1)].reshape(heads, nd, w, 2 * w - 1)[..., w - 1:]

    def body(t_ref, o_ref):
        q = lax.broadcasted_iota(jnp.int32, (w, w), 0)
        c = lax.broadcasted_iota(jnp.int32, (w, w), 1)
        c0 = jnp.clip(q - NA_KW // 2, 0, w - NA_KW)
        in_cols = (c >= c0) & (c < c0 + NA_KW)
        outside = jnp.full((w, w), NEG_INF, f32)
        blocks = [jnp.where(in_cols, t_ref[dd], NEG_INF) for dd in range(nd)]
        for typ in range(NA_TYPES):
            for i in range(NA_QR):
                row = [blocks[d[typ, i, kk]] if row_ok[typ, i, kk] else outside for kk in range(NA_WR)]
                o_ref[typ, i * w:(i + 1) * w, :] = jnp.concatenate(row, axis=1)

    return pl.pallas_call(
        body, name=name, grid=(heads,),
        in_specs=[pl.BlockSpec((None, nd, w, w), lambda h: (h, 0, 0, 0))],
        out_specs=pl.BlockSpec((NA_TYPES, None, NA_QR * w, NA_WR * w), lambda h: (0, h, 0, 0)),
        out_shape=S_((NA_TYPES, heads, NA_QR * w, NA_WR * w), f32),
        compiler_params=_cp(VMEM_MID, ("arbitrary",)),
    )(t)


def _rpb_reduce_tables():
    _, _, d, row_ok = _rpb_index_tables()
    flip = np.eye(GRID_W, dtype=np.float32)[::-1].copy()
    sel = np.zeros((16, NA_SEL_ROWS), np.float32)
    flat_d, flat_ok = d.reshape(-1), row_ok.reshape(-1)
    for n in range(flat_d.size):
        if flat_ok[n]:
            sel[flat_d[n], n] = 1.0
    return jnp.asarray(flip), jnp.asarray(sel)


class _Cfg:
    def __init__(self, s_len, l_len, d, f):
        self.S, self.L, self.D, self.F = s_len, l_len, d, f
        self.T = s_len + l_len
        self.TM = 256 if l_len % 256 == 0 else 128
        assert l_len == self.TM, "context length must equal the row tile"
        assert s_len % self.TM == 0 and s_len % GRID_W == 0
        self.nxt = s_len // self.TM
        self.ntt = self.T // self.TM
        self.rows = s_len // GRID_W
        assert self.rows >= 2 * NA_KH
        assert f % (2 * LANES) == 0
        self.FC = f

    def ntiles(self, with_ctx):
        return self.ntt if with_ctx else self.nxt


def _typ(cfg):
    return lambda i: (jnp.minimum(i // cfg.nxt, 1), 0, 0)


def _mesh_pos():
    return lax.axis_index("x"), lax.axis_index("y"), lax.axis_index("c")


class _Comm:
    def __init__(self, ins, outs, alias, nsem, start, finish):
        self.ins, self.outs, self.alias, self.nsem, self.start, self.finish = ins, outs, alias, nsem, start, finish


def _call(body, args, comm=None, *, grid, in_specs, out_specs, out_shape, scratch_shapes=(), **kw):
    if comm is None:
        return pl.pallas_call(body, grid=grid, in_specs=list(in_specs), out_specs=list(out_specs), out_shape=list(out_shape),
                              scratch_shapes=list(scratch_shapes), **kw)(*args), ()
    n_in, n_out, n_sc = len(in_specs), len(out_specs), len(scratch_shapes)
    ci, co = len(comm.ins), len(comm.outs)

    def carrier(*refs):
        bounds = np.cumsum([0, n_in, ci, n_out, co, n_sc])
        ins, cins, outs, couts, scr = (refs[a:b] for a, b in zip(bounds[:-1], bounds[1:]))
        send, recv = refs[bounds[-1]], refs[bounds[-1] + 1]
        first = functools.reduce(jnp.logical_and, [pl.program_id(a) == 0 for a in range(len(grid))])
        last = functools.reduce(jnp.logical_and, [pl.program_id(a) == g - 1 for a, g in enumerate(grid)])

        @pl.when(first)
        def _():
            comm.start(cins, couts, send, recv)

        body(*ins, *outs, *scr)

        @pl.when(last)
        def _():
            comm.finish(cins, couts, send, recv)

    res = pl.pallas_call(
        carrier, grid=grid, in_specs=list(in_specs) + [ANY] * ci, out_specs=list(out_specs) + [ANY] * co,
        out_shape=list(out_shape) + list(comm.outs),
        input_output_aliases={n_in + a: n_out + b for a, b in comm.alias.items()},
        scratch_shapes=list(scratch_shapes) + [pltpu.SemaphoreType.DMA((comm.nsem,)), pltpu.SemaphoreType.DMA((comm.nsem,))],
        **kw)(*args, *comm.ins)
    return res[:n_out], res[n_out:]


def _comm_only(comm, name):
    ci, co = len(comm.ins), len(comm.outs)

    def body(*refs):
        cins, couts = refs[:ci], refs[ci:ci + co]
        send, recv = refs[ci + co], refs[ci + co + 1]
        comm.start(cins, couts, send, recv)
        comm.finish(cins, couts, send, recv)

    return pl.pallas_call(
        body, name=name, in_specs=[ANY] * ci, out_specs=[ANY] * co, out_shape=list(comm.outs),
        input_output_aliases=dict(comm.alias),
        scratch_shapes=[pltpu.SemaphoreType.DMA((comm.nsem,)), pltpu.SemaphoreType.DMA((comm.nsem,))],
        compiler_params=_cp(VMEM_MID),
    )(*comm.ins)


def _half_view(ref, axis, kk, h):
    r, c = ref.shape
    if axis == 1:
        n = c // N_CHIPS
        return ref.at[pl.ds(h * (r // 2), r // 2), pl.ds(pl.multiple_of(kk * n, LANES), n)]
    n = r // N_CHIPS
    return ref.at[pl.ds(pl.multiple_of(kk * n + h * (n // 2), 8), n // 2), :]


def _other_chips(x, y):
    return [(1 - x, y), (x, 1 - y), (1 - x, 1 - y)]


def _gather_comm(arrs, axes):
    n = len(arrs)

    def copy(ref, view, sems, k, to):
        send, recv = sems
        return pltpu.make_async_remote_copy(src_ref=view, dst_ref=view, send_sem=send.at[k], recv_sem=recv.at[k],
                                            device_id=to, device_id_type=MESH)

    def start(cins, bufs, send, recv):
        x, y, c = _mesh_pos()
        for t in range(n):
            own = _half_view(bufs[t], axes[t], 2 * x + y, c)
            for j, chip in enumerate(_other_chips(x, y)):
                copy(bufs[t], own, (send, recv), 6 * t + j, (*chip, c)).start()

    def finish(cins, bufs, send, recv):
        x, y, c = _mesh_pos()
        sibling = (x, y, 1 - c)
        chips = _other_chips(x, y)
        for t in range(n):
            for j, chip in enumerate(chips):
                landed = _half_view(bufs[t], axes[t], 2 * chip[0] + chip[1], c)
                copy(bufs[t], landed, (send, recv), 6 * t + j, (*chip, c)).wait_recv()
                copy(bufs[t], landed, (send, recv), 6 * t + 3 + j, sibling).start()
        for t in range(n):
            own = _half_view(bufs[t], axes[t], 2 * x + y, c)
            for j, chip in enumerate(chips):
                kj = 2 * chip[0] + chip[1]
                copy(bufs[t], _half_view(bufs[t], axes[t], kj, 1 - c), (send, recv), 6 * t + 3 + j, sibling).wait_recv()
                copy(bufs[t], own, (send, recv), 6 * t + j, (*chip, c)).wait_send()
                copy(bufs[t], _half_view(bufs[t], axes[t], kj, c), (send, recv), 6 * t + 3 + j, sibling).wait_send()

    return _Comm(list(arrs), [S_(a.shape, a.dtype) for a in arrs], {t: t for t in range(n)}, 6 * n, start, finish)


def _scatter_comm(parts, axes):
    n = len(parts)
    peers = [(fx, fy, fc) for fx in (0, 1) for fy in (0, 1) for fc in (0, 1)][1:]

    def half_shape(a, axis):
        r, c = a.shape
        return (r // 2, c // N_CHIPS) if axis == 1 else (r // N_CHIPS // 2, c)

    def start(srcs, lands, send, recv):
        x, y, c = _mesh_pos()
        me = 4 * x + 2 * y + c
        for t in range(n):
            for r_, (fx, fy, fc) in enumerate(peers):
                dx, dy, dc = (1 - x if fx else x), (1 - y if fy else y), (1 - c if fc else c)
                pltpu.make_async_remote_copy(
                    src_ref=_half_view(srcs[t], axes[t], 2 * dx + dy, dc), dst_ref=lands[t].at[me],
                    send_sem=send.at[7 * t + r_], recv_sem=recv.at[7 * t + r_],
                    device_id=(dx, dy, dc), device_id_type=MESH).start()

    def finish(srcs, lands, send, recv):
        x, y, c = _mesh_pos()
        for t in range(n):
            mine = _half_view(srcs[t], axes[t], 2 * x + y, c)
            for r_, (fx, fy, fc) in enumerate(peers):
                sx, sy, sc = (1 - x if fx else x), (1 - y if fy else y), (1 - c if fc else c)
                cp = pltpu.make_async_remote_copy(
                    src_ref=mine, dst_ref=lands[t].at[4 * sx + 2 * sy + sc],
                    send_sem=send.at[7 * t + r_], recv_sem=recv.at[7 * t + r_],
                    device_id=(sx, sy, sc), device_id_type=MESH)
                cp.wait_recv()
                cp.wait_send()

    return _Comm(list(parts), [S_((N_DEV,) + half_shape(a, ax), a.dtype) for a, ax in zip(parts, axes)], {}, 7 * n, start, finish)


def _allgather_comm(block):
    peers = [(fx, fy, fc) for fx in (0, 1) for fy in (0, 1) for fc in (0, 1)][1:]

    def ends(x, y, c):
        for r_, (fx, fy, fc) in enumerate(peers):
            yield r_, ((1 - x if fx else x), (1 - y if fy else y), (1 - c if fc else c))

    def start(srcs, lands, send, recv):
        x, y, c = _mesh_pos()
        for r_, peer in ends(x, y, c):
            pltpu.make_async_remote_copy(src_ref=srcs[0], dst_ref=lands[0].at[4 * x + 2 * y + c], send_sem=send.at[r_],
                                         recv_sem=recv.at[r_], device_id=peer, device_id_type=MESH).start()

    def finish(srcs, lands, send, recv):
        x, y, c = _mesh_pos()
        for r_, (px, py, pc) in ends(x, y, c):
            cp = pltpu.make_async_remote_copy(src_ref=srcs[0], dst_ref=lands[0].at[4 * px + 2 * py + pc], send_sem=send.at[r_],
                                              recv_sem=recv.at[r_], device_id=(px, py, pc), device_id_type=MESH)
            cp.wait_recv()
            cp.wait_send()

    return _Comm([block], [S_((N_DEV,) + block.shape, block.dtype)], {}, len(peers), start, finish)


def _ffn_fwd(cfg, xs, mods, gvec, wgu, wd, mi, gi, with_ctx, name, comm=None, xs_ctx=None, loss_target=None):
    TM, D, F, FC = cfg.TM, cfg.D, cfg.F, cfg.FC
    nt = cfg.ntiles(with_ctx)
    R = nt * TM
    split, head = xs_ctx is not None, loss_target is not None

    def body(*refs):
        it = iter(refs)
        xs_ref = next(it)
        xc_ref = next(it) if split else None
        mods_ref, g_ref, wgu_hbm, wd_hbm = next(it), next(it), next(it), next(it)
        t_ref = next(it) if head else None
        out_ref, hb_ref, z_ref, y_ref = next(it), next(it), next(it), next(it)
        loss_ref = next(it) if head else None
        wgu_v, wd_v, sem = next(it), next(it), next(it)
        i = pl.program_id(0)

        @pl.when(i == 0)
        def _():
            c0 = pltpu.make_async_copy(wgu_hbm, wgu_v, sem.at[0])
            c1 = pltpu.make_async_copy(wd_hbm, wd_v, sem.at[1])
            c0.start(); c1.start(); c0.wait(); c1.wait()
            if head:
                loss_ref[...] = jnp.zeros_like(loss_ref)
        x = xs_ref[...]
        if split:
            x = jnp.where(i < cfg.nxt, x, xc_ref[...])
        m = mods_ref[0]
        sh, sc, gt = m[mi:mi + 1], m[mi + 1:mi + 2], m[mi + 2:mi + 3]
        xhat, _ = _rms_hat(x)
        h = (xhat * g_ref[gi:gi + 1]) * (1.0 + sc) + sh
        hb = h.astype(bf16)
        hb_ref[...] = hb
        y = jnp.zeros((TM, D), f32)
        for ch in range(F // FC):
            zg = _nn(hb, wgu_v[:, ch * FC:(ch + 1) * FC])
            zu = _nn(hb, wgu_v[:, F + ch * FC:F + (ch + 1) * FC])
            z_ref[:, ch * FC:(ch + 1) * FC] = zg.astype(bf16)
            z_ref[:, F + ch * FC:F + (ch + 1) * FC] = zu.astype(bf16)
            a = (zg * _sigmoid(zg)) * zu
            y = y + _nn(a.astype(bf16), wd_v[ch * FC:(ch + 1) * FC, :])
        y_ref[...] = y
        yhat, _ = _rms_hat(y)
        out = x + 0.5 * gt * (yhat * g_ref[gi + 1:gi + 2])
        if head:
            e = out - t_ref[...]
            out_ref[...] = e * (1.0 / D)
            loss_ref[...] += jnp.sum(jnp.mean(e * e, axis=-1, keepdims=True), axis=0, keepdims=True) * 0.5
        else:
            out_ref[...] = out

    rt = lambda c: pl.BlockSpec((TM, c), lambda i: (i, 0))
    lat = pl.BlockSpec((TM, D), lambda i: (jnp.minimum(i, cfg.nxt - 1), 0))
    x_specs, x_args = ([lat, pl.BlockSpec((TM, D), lambda i: (0, 0))], [xs, xs_ctx]) if split else ([rt(D)], [xs])
    t_specs, t_args = ([rt(D)], [loss_target]) if head else ([], [])
    l_specs, l_shape = ([pl.BlockSpec((8, LANES), lambda i: (0, 0))], [S_((8, LANES), f32)]) if head else ([], [])
    return _call(
        body, (*x_args, mods, gvec, wgu, wd, *t_args), comm, name=name, grid=(nt,),
        in_specs=x_specs + [pl.BlockSpec((1, N_MOD, D), _typ(cfg)), pl.BlockSpec((6, D), lambda i: (0, 0)), ANY, ANY] + t_specs,
        out_specs=[rt(D), rt(D), rt(2 * F), rt(D)] + l_specs,
        out_shape=[S_((R, D), f32), S_((R, D), bf16), S_((R, 2 * F), bf16), S_((R, D), f32)] + l_shape,
        scratch_shapes=[pltpu.VMEM((D, 2 * F), bf16), pltpu.VMEM((F, D), bf16), pltpu.SemaphoreType.DMA((2,))],
        compiler_params=_cp(VMEM_BIG, ("arbitrary",)),
    )


def _ffn_bwd(cfg, dout, xs, z, y, mods, gvec, wgu, wd, mi, gi, with_ctx, name, comm=None, xs_ctx=None):
    TM, D, F, FC = cfg.TM, cfg.D, cfg.F, cfg.FC
    nt = cfg.ntiles(with_ctx)
    R = nt * TM
    ntyp = 2 if with_ctx else 1
    split = xs_ctx is not None

    def body(*refs):
        it = iter(refs)
        do_ref, xs_ref = next(it), next(it)
        xc_ref = next(it) if split else None
        z_ref, y_ref, mods_ref, g_ref, wgu_hbm, wd_hbm = (next(it) for _ in range(6))
        dx_ref, dz_ref, dy_ref, a_ref, dm_ref, dg_ref, wgu_v, wd_v, sem = (next(it) for _ in range(9))
        i = pl.program_id(0)

        @pl.when(i == 0)
        def _():
            c0 = pltpu.make_async_copy(wgu_hbm, wgu_v, sem.at[0])
            c1 = pltpu.make_async_copy(wd_hbm, wd_v, sem.at[1])
            c0.start(); c1.start(); c0.wait(); c1.wait()
            dg_ref[...] = jnp.zeros_like(dg_ref)

        @pl.when((i == 0) | (i == cfg.nxt))
        def _():
            dm_ref[...] = jnp.zeros_like(dm_ref)

        do = do_ref[...]
        x = xs_ref[...]
        if split:
            x = jnp.where(i < cfg.nxt, x, xc_ref[...])
        m = mods_ref[0]
        sc, gt = m[mi + 1:mi + 2], m[mi + 2:mi + 3]
        g_pre, g_post = g_ref[gi:gi + 1], g_ref[gi + 1:gi + 2]
        xhat, rinv0 = _rms_hat(x)
        n0 = xhat * g_pre
        yhat, rinv1 = _rms_hat(y_ref[...])
        d_gt = _rsum(0.5 * do * (yhat * g_post))
        dr = (0.5 * gt) * do
        dg_post = _rsum(dr * yhat)
        dy = _rms_bwd(dr * g_post, yhat, rinv1)
        dyb = dy.astype(bf16)
        dy_ref[...] = dyb
        dh = jnp.zeros((TM, D), f32)
        for ch in range(F // FC):
            zg = z_ref[:, ch * FC:(ch + 1) * FC].astype(f32)
            zu = z_ref[:, F + ch * FC:F + (ch + 1) * FC].astype(f32)
            sg = _sigmoid(zg)
            silu = zg * sg
            a_ref[:, ch * FC:(ch + 1) * FC] = (silu * zu).astype(bf16)
            da = _nt(dyb, wd_v[ch * FC:(ch + 1) * FC, :])
            dzu = (da * silu).astype(bf16)
            dzg = (da * zu * (sg * (1.0 + zg * (1.0 - sg)))).astype(bf16)
            dz_ref[:, ch * FC:(ch + 1) * FC] = dzg
            dz_ref[:, F + ch * FC:F + (ch + 1) * FC] = dzu
            dh = dh + _nt(dzg, wgu_v[:, ch * FC:(ch + 1) * FC]) + _nt(dzu, wgu_v[:, F + ch * FC:F + (ch + 1) * FC])
        d_sh = _rsum(dh)
        d_sc = _rsum(dh * n0)
        dn = dh * (1.0 + sc)
        dg_pre = _rsum(dn * xhat)
        dx = do + _rms_bwd(dn * g_pre, xhat, rinv0)
        if split:
            @pl.when(i < cfg.nxt)
            def _():
                dx_ref[...] = dx
        else:
            dx_ref[...] = dx
        dm_ref[0] += jnp.concatenate([d_sh, d_sc, d_gt], axis=0)
        dg_ref[...] += jnp.concatenate([dg_pre, dg_post], axis=0)

    rt = lambda c: pl.BlockSpec((TM, c), lambda i: (i, 0))
    lat = pl.BlockSpec((TM, D), lambda i: (jnp.minimum(i, cfg.nxt - 1), 0))
    x_specs, x_args = ([lat, pl.BlockSpec((TM, D), lambda i: (0, 0))], [xs, xs_ctx]) if split else ([rt(D)], [xs])
    return _call(
        body, (dout, *x_args, z, y, mods, gvec, wgu, wd), comm, name=name, grid=(nt,),
        in_specs=[rt(D)] + x_specs + [rt(2 * F), rt(D), pl.BlockSpec((1, N_MOD, D), _typ(cfg)),
                                       pl.BlockSpec((6, D), lambda i: (0, 0)), ANY, ANY],
        out_specs=[lat if split else rt(D), rt(2 * F), rt(D), rt(F), pl.BlockSpec((1, 3, D), _typ(cfg)),
                   pl.BlockSpec((2, D), lambda i: (0, 0))],
        out_shape=[S_((cfg.S if split else R, D), f32), S_((R, 2 * F), bf16), S_((R, D), bf16), S_((R, F), bf16),
                   S_((ntyp, 3, D), f32), S_((2, D), f32)],
        scratch_shapes=[pltpu.VMEM((D, 2 * F), bf16), pltpu.VMEM((F, D), bf16), pltpu.SemaphoreType.DMA((2,))],
        compiler_params=_cp(VMEM_BIG, ("arbitrary",)),
    )


def _wgrad(a, b, k_rows, name, comm=None):
    M, N = a.shape[1], b.shape[1]
    tn = _div_tile(N, 1, WGRAD_TN, LANES) if N > WGRAD_TN // 2 else N
    tn = N // 2 if tn == N and N % (2 * LANES) == 0 else tn
    room = VMEM_BIG - WGRAD_SLACK - 2 * M * tn * 6
    tk = _div_tile(k_rows, 1, min(WGRAD_TK, room // (4 * (M + tn))), LANES)
    nk = k_rows // tk

    def body(a_ref, b_ref, o_ref, ob_ref):
        k = pl.program_id(1)

        @pl.when(k == 0)
        def _():
            o_ref[...] = jnp.zeros_like(o_ref)
        o_ref[...] += _tn(a_ref[...], b_ref[...])

        @pl.when(k == nk - 1)
        def _():
            ob_ref[...] = o_ref[...].astype(bf16)

    ospec = pl.BlockSpec((M, tn), lambda n, k: (0, n))
    return _call(
        body, (a, b), comm, name=name, grid=(N // tn, nk),
        in_specs=[pl.BlockSpec((tk, M), lambda n, k: (k, 0)), pl.BlockSpec((tk, tn), lambda n, k: (k, n))],
        out_specs=[ospec, ospec], out_shape=[S_((M, N), f32), S_((M, N), bf16)],
        compiler_params=_cp(VMEM_BIG, ("arbitrary", "arbitrary")),
    )


def _tmpre_fwd(cfg, xs, mods, gvec, w_in, cos, sin, name):
    TM, D = cfg.TM, cfg.D
    nt, R = cfg.ntt, cfg.T
    W = NA_WIDTH

    def body(xs_ref, mods_ref, g_ref, w_ref, cos_ref, sin_ref, hb_ref, q_ref, k_ref, v_ref, u_ref):
        x = xs_ref[...]
        m = mods_ref[0]
        xhat, _ = _rms_hat(x)
        hb = ((xhat * g_ref[2:3]) * (1.0 + m[4:5]) + m[3:4]).astype(bf16)
        hb_ref[...] = hb
        p = _nn(hb, w_ref[...])
        cs = jnp.tile(cos_ref[...], (1, W // LANES))
        sn = jnp.tile(sin_ref[...], (1, W // LANES))
        q = p[:, 0:W]
        k = p[:, W:2 * W]
        q_ref[...] = ((q * cs + _swap16(q) * sn) * (HEAD_DIM ** -0.5)).astype(bf16)
        k_ref[...] = (k * cs + _swap16(k) * sn).astype(bf16)
        v_ref[...] = p[:, 2 * W:3 * W].astype(bf16)
        u_ref[...] = p[:, 3 * W:]

    rt = lambda c: pl.BlockSpec((TM, c), lambda i: (i, 0))
    return pl.pallas_call(
        body, name=name, grid=(nt,),
        in_specs=[rt(D), pl.BlockSpec((1, N_MOD, D), _typ(cfg)), pl.BlockSpec((6, D), lambda i: (0, 0)),
                  pl.BlockSpec((D, IN_WIDTH), lambda i: (0, 0)), rt(LANES), rt(LANES)],
        out_specs=[rt(D), rt(W), rt(W), rt(W), rt(POOL_WIDTH)],
        out_shape=[S_((R, D), bf16), S_((R, W), bf16), S_((R, W), bf16), S_((R, W), bf16), S_((R, POOL_WIDTH), f32)],
        compiler_params=_cp(VMEM_MID, ("arbitrary",)),
    )(xs, mods, gvec, w_in, cos, sin)


def _tmpre_bwd(cfg, lat, ctx_terms, du_has_ctx, cos, sin, w_in, xs, mods, gvec, dres, res_with_ctx, name):
    TM, D = cfg.TM, cfg.D
    nt, R = cfg.ntt, cfg.T
    nres = cfg.ntiles(res_with_ctx)
    W = NA_WIDTH
    n_ctx = [len(t) for t in ctx_terms]
    flat_ctx = [a for t in ctx_terms for a in t]
    n_asm = 4 + len(flat_ctx) + 2

    def assemble(refs, o_ref):
        dq_ref, dk_ref, dv_ref, du_ref = refs[:4]
        ctx_refs = refs[4:4 + len(flat_ctx)]
        cos_ref, sin_ref = refs[4 + len(flat_ctx):]
        is_ctx = pl.program_id(0) >= cfg.nxt
        vals, off = [], 0
        for lat_ref, n in zip((dq_ref, dk_ref, dv_ref), n_ctx):
            cv = jnp.zeros((TM, W), f32)
            for r_ in ctx_refs[off:off + n]:
                cv = cv + r_[...]
            off += n
            vals.append(jnp.where(is_ctx, cv, lat_ref[...]))
        du_ = du_ref[...] if du_has_ctx else jnp.where(is_ctx, 0.0, du_ref[...])
        cs = jnp.tile(cos_ref[...], (1, W // LANES))
        sn = jnp.tile(sin_ref[...], (1, W // LANES))
        dq_ = vals[0] * (HEAD_DIM ** -0.5)
        dk_ = vals[1]
        o_ref[:, 0:W] = (dq_ * cs + _swap16(dq_ * sn)).astype(bf16)
        o_ref[:, W:2 * W] = (dk_ * cs + _swap16(dk_ * sn)).astype(bf16)
        o_ref[:, 2 * W:3 * W] = vals[2].astype(bf16)
        o_ref[:, 3 * W:] = du_.astype(bf16)

    def body(*refs):
        w_ref, xs_ref, mods_ref, g_ref, dres_ref, dx_ref, dp_ref, dm_ref, dg_ref = refs[n_asm:]
        i = pl.program_id(0)

        @pl.when(i == 0)
        def _():
            dg_ref[...] = jnp.zeros_like(dg_ref)

        @pl.when((i == 0) | (i == cfg.nxt))
        def _():
            dm_ref[...] = jnp.zeros_like(dm_ref)

        assemble(refs[:n_asm], dp_ref)
        dh = _nt(dp_ref[...], w_ref[...])
        x = xs_ref[...]
        m = mods_ref[0]
        g2 = g_ref[2:3]
        xhat, rinv = _rms_hat(x)
        d_sh = _rsum(dh)
        d_sc = _rsum(dh * (xhat * g2))
        dn = dh * (1.0 + m[4:5])
        dg_ref[...] += _rsum(dn * xhat)
        dx = _rms_bwd(dn * g2, xhat, rinv)
        res = dres_ref[...]
        if nres < nt:
            res = jnp.where(i < nres, res, 0.0)
        dx_ref[...] = res + dx
        dm_ref[0] += jnp.concatenate([d_sh, d_sc], axis=0)

    rt = lambda c: pl.BlockSpec((TM, c), lambda i: (i, 0))
    lat_spec = pl.BlockSpec((TM, W), lambda i: (jnp.minimum(i, cfg.nxt - 1), 0))
    du_spec = rt(POOL_WIDTH) if du_has_ctx else lat_spec
    asm_specs = ([lat_spec, lat_spec, lat_spec, du_spec] + [pl.BlockSpec((TM, W), lambda i: (0, 0))] * len(flat_ctx)
                 + [rt(LANES), rt(LANES)])
    return pl.pallas_call(
        body, name=name, grid=(nt,),
        in_specs=asm_specs + [pl.BlockSpec((D, IN_WIDTH), lambda i: (0, 0)), rt(D),
                              pl.BlockSpec((1, N_MOD, D), _typ(cfg)), pl.BlockSpec((6, D), lambda i: (0, 0)),
                              pl.BlockSpec((TM, D), lambda i: (jnp.minimum(i, nres - 1), 0))],
        out_specs=[rt(D), rt(IN_WIDTH), pl.BlockSpec((1, 2, D), _typ(cfg)), pl.BlockSpec((1, D), lambda i: (0, 0))],
        out_shape=[S_((R, D), f32), S_((R, IN_WIDTH), bf16), S_((2, 2, D), f32), S_((1, D), f32)],
        compiler_params=_cp(VMEM_MID, ("arbitrary",)),
    )(*lat, *flat_ctx, cos, sin, w_in, xs, mods, gvec, dres)


def _na_block(cfg, b):
    return jnp.clip(NA_QR * b - NA_KH // 2, 0, cfg.rows - NA_WR)


def _na_load_bias(b, nb, b_hbm, b_v, sem):
    for typ, at in ((0, 0), (1, 1), (2, nb - 1)):
        @pl.when(b == at)
        def _(typ=typ):
            cp = pltpu.make_async_copy(b_hbm.at[typ], b_v, sem)
            cp.start()
            cp.wait()


def _na_probs(qh, klh, kch, bias):
    s_loc = _nt(qh, klh) + bias
    s_ctx = _nt(qh, kch)
    mx = jnp.maximum(jnp.max(s_loc, axis=-1, keepdims=True), jnp.max(s_ctx, axis=-1, keepdims=True))
    e_loc = jnp.exp(s_loc - mx)
    e_ctx = jnp.exp(s_ctx - mx)
    inv = 1.0 / (jnp.sum(e_loc, axis=-1, keepdims=True) + jnp.sum(e_ctx, axis=-1, keepdims=True))
    return e_loc * inv, e_ctx * inv


def _na_fwd(cfg, q, k, v, bexp, name, comm=None):
    S, L, T = cfg.S, cfg.L, cfg.T
    NQ, NW = NA_QR * GRID_W, NA_WR * GRID_W
    nb = cfg.rows // NA_QR

    def body(q_ref, k_hbm, v_hbm, b_hbm, o_ref, k_v, v_v, b_v, sem):
        b = pl.program_id(0)

        @pl.when(b == 0)
        def _():
            cs = [pltpu.make_async_copy(k_hbm, k_v, sem.at[0]), pltpu.make_async_copy(v_hbm, v_v, sem.at[1])]
            for c_ in cs:
                c_.start()
            for c_ in cs:
                c_.wait()

        _na_load_bias(b, nb, b_hbm, b_v, sem.at[2])
        st = pl.multiple_of(_na_block(cfg, b) * GRID_W, GRID_W)
        first = lax.broadcasted_iota(jnp.int32, (NQ, LANES), 1) < HEAD_DIM
        for hp in range(NA_HEADS // 2):
            ls = slice(hp * LANES, (hp + 1) * LANES)
            q2 = q_ref[:, ls]
            kl, vl = k_v[pl.ds(st, NW), ls], v_v[pl.ds(st, NW), ls]
            kc, vc = k_v[S:T, ls], v_v[S:T, ls]
            o2 = []
            for hh in range(2):
                qm = jnp.where(first if hh == 0 else ~first, q2, jnp.zeros_like(q2))
                p_loc, p_ctx = _na_probs(qm, kl, kc, b_v[2 * hp + hh])
                o2.append(_nn(p_loc.astype(bf16), vl) + _nn(p_ctx.astype(bf16), vc))
            o_ref[:, ls] = jnp.where(first, o2[0], o2[1]).astype(bf16)

    return _call(
        body, (q, k, v, bexp), comm, name=name, grid=(nb,),
        in_specs=[pl.BlockSpec((NQ, NA_WIDTH), lambda b: (b, 0)), ANY, ANY, ANY],
        out_specs=[pl.BlockSpec((NQ, NA_WIDTH), lambda b: (b, 0))],
        out_shape=[S_((S, NA_WIDTH), bf16)],
        scratch_shapes=[pltpu.VMEM((T, NA_WIDTH), bf16), pltpu.VMEM((T, NA_WIDTH), bf16),
                        pltpu.VMEM((NA_HEADS, NQ, NW), f32), pltpu.SemaphoreType.DMA((3,))],
        compiler_params=_cp(VMEM_MID, ("arbitrary",)),
    )


def _na_bwd(cfg, do, q, k, v, bexp, name, comm=None):
    S, L, T, rows = cfg.S, cfg.L, cfg.T, cfg.rows
    NQ, NW = NA_QR * GRID_W, NA_WR * GRID_W
    NSLOT = 2 * NA_KH
    nb = rows // NA_QR
    bmax = (rows - NA_WR) // NA_QR
    steps = 2 * nb - bmax
    W = NA_WIDTH
    assert nb >= 3 and bmax >= 1 and rows - NA_QR * bmax <= NSLOT

    def out_group(g):
        return jnp.where(g >= nb, g - nb + bmax, jnp.clip(g - 1, 0, bmax - 1))

    def body(do_ref, q_ref, k_hbm, v_hbm, b_hbm, dq_ref, dk_ref, dv_ref, dkc_ref, dvc_ref, db_hbm,
             k_v, v_v, b_v, db_v, ak, av, akc, avc, sem):
        g = pl.program_id(0)

        @pl.when(g == 0)
        def _():
            cs = [pltpu.make_async_copy(k_hbm, k_v, sem.at[0]), pltpu.make_async_copy(v_hbm, v_v, sem.at[1])]
            for c_ in cs:
                c_.start()
            db_v[...] = jnp.zeros_like(db_v)
            ak[...] = jnp.zeros_like(ak)
            av[...] = jnp.zeros_like(av)
            akc[...] = jnp.zeros_like(akc)
            avc[...] = jnp.zeros_like(avc)
            for c_ in cs:
                c_.wait()

        for typ, at in ((0, 1), (1, nb - 1)):
            @pl.when(g == at)
            def _(typ=typ):
                cp = pltpu.make_async_copy(db_v, db_hbm.at[typ], sem.at[2])
                cp.start()
                cp.wait()
                db_v[...] = jnp.zeros_like(db_v)

        @pl.when(g < nb)
        def _():
            _na_load_bias(g, nb, b_hbm, b_v, sem.at[2])
            ws = _na_block(cfg, g)
            st = pl.multiple_of(ws * GRID_W, GRID_W)
            first = lax.broadcasted_iota(jnp.int32, (NQ, LANES), 1) < HEAD_DIM
            for hp in range(NA_HEADS // 2):
                ls = slice(hp * LANES, (hp + 1) * LANES)
                q2, do2 = q_ref[:, ls], do_ref[:, ls]
                kl, vl = k_v[pl.ds(st, NW), ls], v_v[pl.ds(st, NW), ls]
                kc, vc = k_v[S:T, ls], v_v[S:T, ls]
                dq2 = []
                dk2 = jnp.zeros((NW, LANES), f32)
                dv2 = jnp.zeros((NW, LANES), f32)
                dkc2 = jnp.zeros((L, LANES), f32)
                dvc2 = jnp.zeros((L, LANES), f32)
                for hh in range(2):
                    keep = first if hh == 0 else ~first
                    qm = jnp.where(keep, q2, jnp.zeros_like(q2))
                    dom = jnp.where(keep, do2, jnp.zeros_like(do2))
                    p_loc, p_ctx = _na_probs(qm, kl, kc, b_v[2 * hp + hh])
                    dp_loc = _nt(dom, vl)
                    dp_ctx = _nt(dom, vc)
                    delta = jnp.sum(p_loc * dp_loc, axis=-1, keepdims=True) + jnp.sum(p_ctx * dp_ctx, axis=-1, keepdims=True)
                    ds_loc = p_loc * (dp_loc - delta)
                    ds_ctx = p_ctx * (dp_ctx - delta)
                    db_v[2 * hp + hh, :, 0:NW] += ds_loc
                    dsl, dsc = ds_loc.astype(bf16), ds_ctx.astype(bf16)
                    dq2.append(_nn(dsl, kl) + _nn(dsc, kc))
                    dk2 = dk2 + _tn(dsl, qm)
                    dv2 = dv2 + _tn(p_loc.astype(bf16), dom)
                    dkc2 = dkc2 + _tn(dsc, qm)
                    dvc2 = dvc2 + _tn(p_ctx.astype(bf16), dom)
                dq_ref[:, ls] = jnp.where(first, dq2[0], dq2[1])
                akc[:, ls] += dkc2
                avc[:, ls] += dvc2
                for kk in range(NA_WR):
                    slot = (ws + kk) % NSLOT
                    ak[slot, :, ls] += dk2[kk * GRID_W:(kk + 1) * GRID_W, :]
                    av[slot, :, ls] += dv2[kk * GRID_W:(kk + 1) * GRID_W, :]

        @pl.when(((g >= 1) & (g <= bmax)) | (g >= nb))
        def _():
            base = NA_QR * (out_group(g) % (NSLOT // NA_QR))
            for t in range(NA_QR):
                dk_ref[t * GRID_W:(t + 1) * GRID_W, :] = ak[base + t]
                dv_ref[t * GRID_W:(t + 1) * GRID_W, :] = av[base + t]
                ak[base + t] = jnp.zeros((GRID_W, W), f32)
                av[base + t] = jnp.zeros((GRID_W, W), f32)

        @pl.when(g == nb - 1)
        def _():
            cp = pltpu.make_async_copy(db_v, db_hbm.at[2], sem.at[2])
            cp.start()
            cp.wait()

        @pl.when(g == steps - 1)
        def _():
            dkc_ref[...] = akc[...]
            dvc_ref[...] = avc[...]

    qmap = lambda g: (jnp.minimum(g, nb - 1), 0)
    kmap = lambda g: (out_group(g), 0)
    full = lambda g: (0, 0)
    return _call(
        body, (do, q, k, v, bexp), comm, name=name, grid=(steps,),
        in_specs=[pl.BlockSpec((NQ, W), qmap), pl.BlockSpec((NQ, W), qmap), ANY, ANY, ANY],
        out_specs=[pl.BlockSpec((NQ, W), qmap), pl.BlockSpec((NQ, W), kmap), pl.BlockSpec((NQ, W), kmap),
                   pl.BlockSpec((L, W), full), pl.BlockSpec((L, W), full), ANY],
        out_shape=[S_((S, W), f32), S_((S, W), f32), S_((S, W), f32), S_((L, W), f32), S_((L, W), f32),
                   S_((NA_TYPES, NA_HEADS, NQ, NA_WPAD), f32)],
        scratch_shapes=[pltpu.VMEM((T, W), bf16), pltpu.VMEM((T, W), bf16),
                        pltpu.VMEM((NA_HEADS, NQ, NW), f32), pltpu.VMEM((NA_HEADS, NQ, NA_WPAD), f32),
                        pltpu.VMEM((NSLOT, GRID_W, W), f32), pltpu.VMEM((NSLOT, GRID_W, W), f32),
                        pltpu.VMEM((L, W), f32), pltpu.VMEM((L, W), f32), pltpu.SemaphoreType.DMA((3,))],
        compiler_params=_cp(VMEM_BIG, ("arbitrary",)),
    )


def _rpb_reduce(dbias, flip, sel, name):
    nq, w = NA_QR * GRID_W, GRID_W

    def diag_body(x_ref, j_ref, o_ref):
        rows = []
        for i in range(NA_QR):
            xr = jnp.dot(j_ref[...], x_ref[i * w:(i + 1) * w, :], preferred_element_type=f32, precision=lax.Precision.HIGHEST)
            rows.append(jnp.sum(pltpu.roll(xr, 0, 1, stride=1, stride_axis=0), axis=0, keepdims=True))
        o_ref[...] = jnp.concatenate(rows + [jnp.zeros((8 - NA_QR, NA_WPAD), f32)], axis=0)

    diag = pl.pallas_call(
        diag_body, name=name + "_diag", grid=(NA_TYPES, NA_HEADS),
        in_specs=[pl.BlockSpec((None, None, nq, NA_WPAD), lambda t, h: (t, h, 0, 0)), pl.BlockSpec((w, w), lambda t, h: (0, 0))],
        out_specs=pl.BlockSpec((None, None, 8, NA_WPAD), lambda t, h: (t, h, 0, 0)),
        out_shape=S_((NA_TYPES, NA_HEADS, 8, NA_WPAD), f32),
        compiler_params=_cp(VMEM_MID, ("arbitrary", "arbitrary")),
    )(dbias, flip)
    lo = w - NA_KW
    y = diag[:, :, :NA_QR, lo:lo + NA_WR * w].reshape(NA_TYPES, NA_HEADS, NA_QR, NA_WR, w)
    y = jnp.transpose(y, (1, 0, 2, 3, 4)).reshape(NA_HEADS, NA_TYPES * NA_QR * NA_WR, w)
    y = jnp.pad(y, ((0, 0), (0, NA_SEL_ROWS - y.shape[1]), (0, LANES - w)))

    def body(y_ref, sel_ref, o_ref):
        o_ref[...] = jnp.dot(sel_ref[...], y_ref[...], preferred_element_type=f32, precision=lax.Precision.HIGHEST)

    return pl.pallas_call(
        body, name=name, grid=(NA_HEADS,),
        in_specs=[pl.BlockSpec((None, NA_SEL_ROWS, LANES), lambda h: (h, 0, 0)), pl.BlockSpec((16, NA_SEL_ROWS), lambda h: (0, 0))],
        out_specs=pl.BlockSpec((None, 16, LANES), lambda h: (h, 0, 0)),
        out_shape=S_((NA_HEADS, 16, LANES), f32),
        compiler_params=_cp(VMEM_MID, ("arbitrary",)),
    )(y, sel)


def _ctx_attn_fwd(cfg, q, k, v, name):
    L = cfg.L
    blk = cfg.S // L

    def body(q_ref, k_ref, v_ref, o_ref):
        qv, kv, vv = q_ref[...], k_ref[...], v_ref[...]
        outs = []
        for h in range(NA_HEADS):
            hs = slice(h * HEAD_DIM, (h + 1) * HEAD_DIM)
            s = _nt(qv[:, hs], kv[:, hs])
            e = jnp.exp(s - jnp.max(s, axis=-1, keepdims=True))
            p = e * (1.0 / jnp.sum(e, axis=-1, keepdims=True))
            outs.append(_nn(p.astype(bf16), vv[:, hs]))
        o_ref[...] = jnp.concatenate(outs, axis=-1).astype(bf16)

    spec = pl.BlockSpec((L, NA_WIDTH), lambda i: (blk, 0))
    return pl.pallas_call(
        body, name=name, grid=(1,), in_specs=[spec, spec, spec],
        out_specs=pl.BlockSpec((L, NA_WIDTH), lambda i: (0, 0)), out_shape=S_((L, NA_WIDTH), bf16),
        compiler_params=_cp(VMEM_MID, ("arbitrary",)),
    )(q, k, v)


def _ctx_attn_bwd(cfg, do, q, k, v, name):
    L = cfg.L
    blk = cfg.S // L

    def body(do_ref, q_ref, k_ref, v_ref, dq_ref, dk_ref, dv_ref):
        dov, qv, kv, vv = do_ref[...], q_ref[...], k_ref[...], v_ref[...]
        dqs, dks, dvs = [], [], []
        for h in range(NA_HEADS):
            hs = slice(h * HEAD_DIM, (h + 1) * HEAD_DIM)
            qh, kh, doh = qv[:, hs], kv[:, hs], dov[:, hs]
            s = _nt(qh, kh)
            e = jnp.exp(s - jnp.max(s, axis=-1, keepdims=True))
            p = e * (1.0 / jnp.sum(e, axis=-1, keepdims=True))
            dp = _nt(doh, vv[:, hs])
            ds = (p * (dp - jnp.sum(p * dp, axis=-1, keepdims=True))).astype(bf16)
            dqs.append(_nn(ds, kh))
            dks.append(_tn(ds, qh))
            dvs.append(_tn(p.astype(bf16), doh))
        dq_ref[...] = jnp.concatenate(dqs, axis=-1)
        dk_ref[...] = jnp.concatenate(dks, axis=-1)
        dv_ref[...] = jnp.concatenate(dvs, axis=-1)

    spec = pl.BlockSpec((L, NA_WIDTH), lambda i: (blk, 0))
    ospec = pl.BlockSpec((L, NA_WIDTH), lambda i: (0, 0))
    return pl.pallas_call(
        body, name=name, grid=(1,), in_specs=[spec, spec, spec, spec],
        out_specs=[ospec, ospec, ospec], out_shape=[S_((L, NA_WIDTH), f32)] * 3,
        compiler_params=_cp(VMEM_MID, ("arbitrary",)),
    )(do, q, k, v)


def _pool_centered(u, band, inv):
    return _split_sum(_nn, band, u) * inv - u


def _split_sum(mm, band, t):
    hi = t.astype(bf16)
    lo = (t - hi.astype(f32)).astype(bf16)
    s = mm(band, jnp.concatenate([hi, lo], axis=1))
    n = t.shape[1]
    return s[:, :n] + s[:, n:]


def _pool_mix(u_ref, band_ref, inv_ref, w_ref, ps_ref):
    C = POOL_CH
    outs = []
    for g in range(POOL_GROUPS):
        d = _pool_centered(u_ref[:, g * C:(g + 1) * C], band_ref[0, g], inv_ref[0, g])
        outs.append(_nn(d.astype(bf16), w_ref[g].astype(bf16)) * ps_ref[:, g * C:(g + 1) * C])
    return jnp.concatenate(outs, axis=-1).astype(bf16)


def _pool_bwd(cfg, dmix, u, band, inv, w_pool, pool_scale, with_ctx, name):
    TM = cfg.TM
    nt = cfg.ntiles(with_ctx)
    C = POOL_CH

    def body(dy_ref, u_ref, band_ref, inv_ref, w_ref, ps_ref, du_ref, dw_ref, dps_ref):
        @pl.when(pl.program_id(0) == 0)
        def _():
            dw_ref[...] = jnp.zeros_like(dw_ref)
            dps_ref[...] = jnp.zeros_like(dps_ref)

        dus, dpss = [], []
        for g in range(POOL_GROUPS):
            gs = slice(g * C, (g + 1) * C)
            band_g, inv_g = band_ref[0, g], inv_ref[0, g]
            db = _pool_centered(u_ref[:, gs], band_g, inv_g).astype(bf16)
            wb = w_ref[g].astype(bf16)
            dy = dy_ref[:, gs].astype(f32)
            dpss.append(_rsum(dy * _nn(db, wb)))
            dys = (dy * ps_ref[:, gs]).astype(bf16)
            dw_ref[g] += _tn(db, dys)
            dd = _nt(dys, wb)
            dus.append(_split_sum(_tn, band_g, dd * inv_g) - dd)
        du_ref[...] = jnp.concatenate(dus, axis=-1)
        dps_ref[...] += jnp.concatenate(dpss, axis=-1)

    typ4 = lambda i: (jnp.minimum(i // cfg.nxt, 1), 0, 0, 0)
    return pl.pallas_call(
        body, name=name, grid=(nt,),
        in_specs=[pl.BlockSpec((TM, POOL_WIDTH), lambda i: (i, 1)), pl.BlockSpec((TM, POOL_WIDTH), lambda i: (i, 0)),
                  pl.BlockSpec((1, POOL_GROUPS, TM, TM), typ4), pl.BlockSpec((1, POOL_GROUPS, TM, 1), typ4),
                  pl.BlockSpec((POOL_GROUPS, C, C), lambda i: (0, 0, 0)), pl.BlockSpec((1, POOL_WIDTH), lambda i: (0, 0))],
        out_specs=[pl.BlockSpec((TM, POOL_WIDTH), lambda i: (i, 0)), pl.BlockSpec((POOL_GROUPS, C, C), lambda i: (0, 0, 0)),
                   pl.BlockSpec((1, POOL_WIDTH), lambda i: (0, 0))],
        out_shape=[S_((nt * TM, POOL_WIDTH), f32), S_((POOL_GROUPS, C, C), f32), S_((1, POOL_WIDTH), f32)],
        compiler_params=_cp(VMEM_MID, ("arbitrary",)),
    )(dmix, u, band, inv, w_pool, pool_scale)


def _tmpost_fwd(cfg, na_x, na_c, u, band, inv, w_pool, pool_scale, w_out, xs, mods, gvec, name):
    TM, D = cfg.TM, cfg.D
    with_ctx = na_c is not None
    nt = cfg.ntiles(with_ctx)
    R = nt * TM

    def body(*refs):
        if with_ctx:
            nax_ref, nac_ref = refs[:2]
            na = jnp.where(pl.program_id(0) < cfg.nxt, nax_ref[...], nac_ref[...])
        else:
            na = refs[0][...]
        (u_ref, band_ref, inv_ref, wp_ref, ps_ref, w_ref, xs_ref, mods_ref, g_ref,
         out_ref, opre_ref, mix_ref) = refs[2 if with_ctx else 1:]
        pool_v = _pool_mix(u_ref, band_ref, inv_ref, wp_ref, ps_ref)
        mix_ref[:, 0:NA_WIDTH] = na
        mix_ref[:, NA_WIDTH:] = pool_v
        o = _nn(na, w_ref[0:NA_WIDTH, :]) + _nn(pool_v, w_ref[NA_WIDTH:, :])
        opre_ref[...] = o
        ohat, _ = _rms_hat(o)
        out_ref[...] = xs_ref[...] + mods_ref[0][5:6] * (ohat * g_ref[3:4])

    rt = lambda c: pl.BlockSpec((TM, c), lambda i: (i, 0))
    na_specs = [pl.BlockSpec((TM, NA_WIDTH), lambda i: (jnp.minimum(i, cfg.nxt - 1), 0))]
    na_args = [na_x]
    if with_ctx:
        na_specs.append(pl.BlockSpec((TM, NA_WIDTH), lambda i: (0, 0)))
        na_args.append(na_c)
    typ4 = lambda i: (jnp.minimum(i // cfg.nxt, 1), 0, 0, 0)
    pool_specs = [rt(POOL_WIDTH), pl.BlockSpec((1, POOL_GROUPS, TM, TM), typ4), pl.BlockSpec((1, POOL_GROUPS, TM, 1), typ4),
                  pl.BlockSpec((POOL_GROUPS, POOL_CH, POOL_CH), lambda i: (0, 0, 0)), pl.BlockSpec((1, POOL_WIDTH), lambda i: (0, 0))]
    return pl.pallas_call(
        body, name=name, grid=(nt,),
        in_specs=na_specs + pool_specs + [pl.BlockSpec((MIX_WIDTH, D), lambda i: (0, 0)), rt(D),
                                          pl.BlockSpec((1, N_MOD, D), _typ(cfg)), pl.BlockSpec((6, D), lambda i: (0, 0))],
        out_specs=[rt(D), rt(D), rt(MIX_WIDTH)],
        out_shape=[S_((R, D), f32), S_((R, D), f32), S_((R, MIX_WIDTH), bf16)],
        compiler_params=_cp(VMEM_MID, ("arbitrary",)),
    )(*na_args, u, band, inv, w_pool, pool_scale, w_out, xs, mods, gvec)


def _tmpost_bwd(cfg, dout, opre, w_out, mods, gvec, with_ctx, name):
    TM, D = cfg.TM, cfg.D
    nt = cfg.ntiles(with_ctx)
    R = nt * TM
    ntyp = 2 if with_ctx else 1

    def body(do_ref, opre_ref, w_ref, mods_ref, g_ref, dop_ref, dmix_ref, dm_ref, dg_ref):
        i = pl.program_id(0)

        @pl.when(i == 0)
        def _():
            dg_ref[...] = jnp.zeros_like(dg_ref)

        @pl.when((i == 0) | (i == cfg.nxt))
        def _():
            dm_ref[...] = jnp.zeros_like(dm_ref)

        do = do_ref[...]
        g3 = g_ref[3:4]
        ohat, rinv = _rms_hat(opre_ref[...])
        dm_ref[0] += _rsum(do * (ohat * g3))
        dr = mods_ref[0][5:6] * do
        dg_ref[...] += _rsum(dr * ohat)
        dob = _rms_bwd(dr * g3, ohat, rinv).astype(bf16)
        dop_ref[...] = dob
        dmix_ref[...] = _nt(dob, w_ref[...]).astype(bf16)

    rt = lambda c: pl.BlockSpec((TM, c), lambda i: (i, 0))
    return pl.pallas_call(
        body, name=name, grid=(nt,),
        in_specs=[rt(D), rt(D), pl.BlockSpec((MIX_WIDTH, D), lambda i: (0, 0)),
                  pl.BlockSpec((1, N_MOD, D), _typ(cfg)), pl.BlockSpec((6, D), lambda i: (0, 0))],
        out_specs=[rt(D), rt(MIX_WIDTH), pl.BlockSpec((1, 1, D), _typ(cfg)), pl.BlockSpec((1, D), lambda i: (0, 0))],
        out_shape=[S_((R, D), bf16), S_((R, MIX_WIDTH), bf16), S_((ntyp, 1, D), f32), S_((1, D), f32)],
        compiler_params=_cp(VMEM_MID, ("arbitrary",)),
    )(dout, opre, w_out, mods, gvec)


def _modvec_fwd(cvecs, w_mod, b_shard, name):
    nl, D, n = w_mod.shape
    tn = n // 3 if (n % 3 == 0 and (n // 3) % LANES == 0) else n

    def body(c_ref, w_ref, b_ref, o_ref, s_ref):
        cv = c_ref[...]
        sv = cv * _sigmoid(cv)
        s_ref[...] = sv
        o_ref[...] = _nn(sv.astype(bf16), w_ref[...].astype(bf16)) + b_ref[...]

    return pl.pallas_call(
        body, name=name, grid=(nl, n // tn),
        in_specs=[pl.BlockSpec((16, D), lambda l, j: (0, 0)), pl.BlockSpec((None, D, tn), lambda l, j: (l, 0, j)),
                  pl.BlockSpec((None, 1, tn), lambda l, j: (l, 0, j))],
        out_specs=[pl.BlockSpec((None, 16, tn), lambda l, j: (l, 0, j)), pl.BlockSpec((16, D), lambda l, j: (0, 0))],
        out_shape=[S_((nl, 16, n), f32), S_((16, D), f32)],
        compiler_params=_cp(VMEM_MID, ("arbitrary", "arbitrary")),
    )(cvecs, w_mod, b_shard)


def _modvec_bwd(s_t, dm, w_mod, name):
    nl, D, n = w_mod.shape
    tn = n // 3 if (n % 3 == 0 and (n // 3) % LANES == 0) else n

    def body(s_ref, dm_ref, w_ref, gw_ref, gc_ref):
        @pl.when(pl.program_id(1) == 0)
        def _():
            gc_ref[...] = jnp.zeros_like(gc_ref)
        dmv = dm_ref[...]
        gw_ref[...] = jnp.dot(s_ref[...], dmv, preferred_element_type=f32, precision=lax.Precision.HIGHEST)
        gc_ref[...] += _nt(dmv[8:16].astype(bf16), w_ref[...].astype(bf16))

    return pl.pallas_call(
        body, name=name, grid=(nl, n // tn),
        in_specs=[pl.BlockSpec((D, 16), lambda l, j: (0, 0)), pl.BlockSpec((None, 16, tn), lambda l, j: (l, 0, j)),
                  pl.BlockSpec((None, D, tn), lambda l, j: (l, 0, j))],
        out_specs=[pl.BlockSpec((None, D, tn), lambda l, j: (l, 0, j)), pl.BlockSpec((None, 8, D), lambda l, j: (l, 0, 0))],
        out_shape=[S_((nl, D, n), f32), S_((nl, 8, D), f32)],
        compiler_params=_cp(VMEM_MID, ("arbitrary", "arbitrary")),
    )(s_t, dm, w_mod)


def _as2d(a):
    n = a.size
    if a.ndim >= 2 and a.shape[-1] % LANES == 0:
        return a.reshape(-1, a.shape[-1])
    if n % LANES == 0:
        return a.reshape(-1, LANES)
    return a.reshape(-1, a.shape[-1]) if a.ndim >= 2 else a.reshape(1, n)


def _row_tile(r, c, budget_elems):
    if r * c <= budget_elems or r % 8 != 0:
        return r
    t = r
    while t * c > budget_elems and t % 16 == 0:
        t //= 2
    return t


def _div_tile(r, c, budget_elems, mult=16):
    best = None
    for t in range(mult, r + 1, mult):
        if r % t == 0 and t * c <= budget_elems:
            best = t
    return best if best is not None else r


def _chip_index():
    return 2 * lax.axis_index("x") + lax.axis_index("y")


def _cast_into_place(shards, lead, axis, name):
    r, c = shards.shape[-2:]
    tr = _div_tile(r, c, 3 * ELEMWISE_BLOCK)
    nr = r // tr
    out_map = (lambda i: (i, _chip_index())) if axis == 1 else (lambda i: (_chip_index() * nr + i, 0))
    full2 = (r, c * N_CHIPS) if axis == 1 else (r * N_CHIPS, c)

    def body(a_ref, o_ref):
        o_ref[...] = a_ref[...].astype(bf16)

    return pl.pallas_call(
        body, name=name, grid=(nr,),
        in_specs=[pl.BlockSpec((None,) * len(lead) + (tr, c), lambda i: tuple(lead) + (i, 0))],
        out_specs=pl.BlockSpec((tr, c), out_map),
        out_shape=S_(full2, bf16), compiler_params=_cp(VMEM_MID, ("arbitrary",)),
    )(shards)


def _sum_devices8(own, land, axis, into, lead, name):
    _, rh, cs = land.shape
    tr = _div_tile(rh, cs, 2 * ELEMWISE_BLOCK)
    nr = rh // tr
    core = lambda: lax.axis_index("c")
    if axis == 1:
        own_map = lambda i: (core() * nr + i, _chip_index())
    else:
        own_map = lambda i: (_chip_index() * 2 * nr + core() * nr + i, 0)
    nl = len(lead)

    def land_spec(j):
        return pl.BlockSpec((None, tr, cs), lambda i: ((2 * _chip_index() + core() + j) % N_DEV, i, 0))

    def body(own_ref, *rest):
        acc = own_ref[...]
        for p_ref in rest[:N_DEV - 1]:
            acc = acc + p_ref[...].astype(f32)
        rest[-1][...] = acc

    return pl.pallas_call(
        body, name=name, grid=(nr,),
        in_specs=[pl.BlockSpec((tr, cs), own_map)] + [land_spec(j) for j in range(1, N_DEV)] + [ANY],
        out_specs=pl.BlockSpec((None,) * nl + (tr, cs), lambda i: tuple(lead) + (core() * nr + i, 0)),
        out_shape=S_(into.shape, f32), input_output_aliases={N_DEV: 0},
        compiler_params=_cp(VMEM_MID, ("arbitrary",)),
    )(own, *([land] * (N_DEV - 1)), into)


def _adamw(w, g, m, v, name, emit_grad=False, comm=None):
    shape = w.shape
    w2, g2, m2, v2 = _as2d(w), _as2d(g), _as2d(m), _as2d(v)
    r, c = w2.shape
    tr = _row_tile(r, c, ELEMWISE_BLOCK)
    c1 = 1.0 - ADAM_B1 ** ADAM_STEP
    c2 = 1.0 - ADAM_B2 ** ADAM_STEP
    n_out = 4 if emit_grad else 3

    def body(w_ref, g_ref, m_ref, v_ref, d_ref, mo_ref, vo_ref, *go_ref):
        gv = g_ref[...]
        mn = ADAM_B1 * m_ref[...] + (1.0 - ADAM_B1) * gv
        vn = ADAM_B2 * v_ref[...] + (1.0 - ADAM_B2) * (gv * gv)
        mo_ref[...] = mn
        vo_ref[...] = vn
        d_ref[...] = -ADAM_LR * ((mn / c1) / (jnp.sqrt(vn / c2) + ADAM_EPS) + ADAM_WD * w_ref[...])
        if emit_grad:
            go_ref[0][...] = gv

    spec = pl.BlockSpec((tr, c), lambda i: (i, 0))
    outs, res = _call(body, (w2, g2, m2, v2), comm, name=name, grid=(r // tr,), in_specs=[spec] * 4, out_specs=[spec] * n_out,
                      out_shape=[S_((r, c), f32)] * n_out, compiler_params=_cp(VMEM_MID, ("arbitrary",)))
    outs = tuple(o.reshape(shape) for o in outs)
    return outs if comm is None else (outs, res)


def _sum_devices(gathered, name):
    _, r, c = gathered.shape

    def body(a_ref, o_ref):
        acc = a_ref[0]
        for j in range(1, N_DEV):
            acc = acc + a_ref[j]
        o_ref[...] = acc

    tr = _row_tile(r, c, ELEMWISE_BLOCK // 4)
    return pl.pallas_call(
        body, name=name, grid=(r // tr,),
        in_specs=[pl.BlockSpec((N_DEV, tr, c), lambda i: (0, i, 0))], out_specs=pl.BlockSpec((tr, c), lambda i: (i, 0)),
        out_shape=S_((r, c), f32), compiler_params=_cp(VMEM_MID, ("arbitrary",)))(gathered)


def _all_gather_small(block, name):
    m_per, n = block.shape

    def body(x_ref, out_ref, send_sems, recv_sems, local_sem):
        x, y, c = _mesh_pos()
        me, sibling = (x, y, c), (x, y, 1 - c)
        chips = [(1 - x, y), (x, 1 - y), (1 - x, 1 - y)]

        def rows(px, py, pc):
            return out_ref.at[pl.ds((4 * px + 2 * py + pc) * m_per, m_per), :]

        def copy(k, blk, to, src=None):
            return pltpu.make_async_remote_copy(
                src_ref=rows(*blk) if src is None else src, dst_ref=rows(*blk),
                send_sem=send_sems.at[k], recv_sem=recv_sems.at[k], device_id=to, device_id_type=MESH)

        mine = pltpu.make_async_copy(x_ref, rows(*me), local_sem)
        mine.start()
        first = [copy(0, me, sibling, src=x_ref)]
        first += [copy(1 + j, me, (*chip, c), src=x_ref) for j, chip in enumerate(chips)]
        for cp in first:
            cp.start()
        passed = [copy(4 + j, (*chip, c), sibling) for j, chip in enumerate(chips)]
        for j, chip in enumerate(chips):
            copy(1 + j, (*chip, c), me).wait_recv()
            passed[j].start()
        copy(0, sibling, me).wait_recv()
        for j, chip in enumerate(chips):
            copy(4 + j, (*chip, 1 - c), me).wait_recv()
        for cp in first + passed:
            cp.wait_send()
        mine.wait()

    return pl.pallas_call(
        body, name=name, out_shape=S_((N_DEV * m_per, n), block.dtype),
        in_specs=[pl.BlockSpec(memory_space=pltpu.VMEM)], out_specs=pl.BlockSpec(memory_space=pltpu.VMEM),
        scratch_shapes=[pltpu.SemaphoreType.DMA((7,)), pltpu.SemaphoreType.DMA((7,)), pltpu.SemaphoreType.DMA],
        compiler_params=_cp(VMEM_MID),
    )(block)


def _pack_rows(arrays):
    flat = jnp.concatenate([a.reshape(-1) for a in arrays])
    pad = (-flat.size) % (8 * LANES)
    return jnp.pad(flat, (0, pad)).reshape(-1, LANES)


def _unpack_rows(packed, shapes):
    flat = packed.reshape(-1)
    out, off = [], 0
    for s in shapes:
        n = int(np.prod(s))
        out.append(flat[off:off + n].reshape(s))
        off += n
    return out


W_AXIS = {"gu": 1, "dn": 0, "wi": 1, "wo": 0}
SMALL_NAMES = ("dmods", "dg", "drpb", "dwp", "dps")


def _half_merge(bufs, name):
    nt = len(bufs)

    def body(*refs):
        outs = refs[nt:2 * nt]
        send_sems, recv_sems = refs[2 * nt:]
        x, y, c = _mesh_pos()

        def half(ref, h):
            rh = ref.shape[-2] // 2
            return ref.at[(slice(None),) * (len(ref.shape) - 2) + (pl.ds(h * rh, rh), slice(None))]

        cps = []
        for t in range(nt):
            cp = pltpu.make_async_remote_copy(
                src_ref=half(outs[t], c), dst_ref=half(outs[t], c), send_sem=send_sems.at[t], recv_sem=recv_sems.at[t],
                device_id=(x, y, 1 - c), device_id_type=MESH)
            cp.start()
            cps.append(cp)
        for t in range(nt):
            pltpu.make_async_remote_copy(
                src_ref=half(outs[t], 1 - c), dst_ref=half(outs[t], 1 - c), send_sem=send_sems.at[t], recv_sem=recv_sems.at[t],
                device_id=(x, y, 1 - c), device_id_type=MESH).wait_recv()
        for cp in cps:
            cp.wait_send()

    return pl.pallas_call(
        body, name=name, in_specs=[ANY] * nt, out_specs=[ANY] * nt, out_shape=[S_(b.shape, f32) for b in bufs],
        input_output_aliases={t: t for t in range(nt)},
        scratch_shapes=[pltpu.SemaphoreType.DMA((nt,)), pltpu.SemaphoreType.DMA((nt,))],
        compiler_params=_cp(VMEM_MID),
    )(*bufs)


def _local_step(cfg, x_lat, x_ctx, target, mods, norm_g, W, G, na_rpb, w_pool, pool_scale):
    S, L, T, D, F = cfg.S, cfg.L, cfg.T, cfg.D, cfg.F
    depth = norm_g.shape[0]
    cos, sin = _rope_tables(S, L)
    band, inv = _pool_tables(cfg.TM, L)
    flip, sel = _rpb_reduce_tables()

    assert depth == 2, "the carrier schedules below are written for two layers"
    fwd_carry = {"ffn_fwd_0_0": [("wi", 0), ("wo", 0), ("gu", 0, 1), ("dn", 0, 1)],
                 "na_fwd_0": [("gu", 1, 0), ("dn", 1, 0)],
                 "ffn_fwd_0_1": [("wi", 1), ("wo", 1), ("gu", 1, 1), ("dn", 1, 1)]}
    bwd_carry = {"na_bwd_1": [("gu", 1, 1), ("dn", 1, 1)], "ffn_bwd_1_0": [("wi", 1), ("wo", 1)],
                 "ffn_bwd_0_1": [("gu", 1, 0), ("dn", 1, 0)], "na_bwd_0": [("gu", 0, 1), ("dn", 0, 1)],
                 "ffn_bwd_0_0": [("wi", 0), ("wo", 0)], "wgrad_dn_0_0": [("gu", 0, 0)]}
    last_scatter = [("dn", 0, 0)]
    tag = lambda key: "_".join(str(p) for p in key)
    g_f32, g_b16 = {}, {}

    def gather_on(name):
        keys = fwd_carry.get(name)
        return None if keys is None else _gather_comm([W[k_] for k_ in keys], [W_AXIS[k_[0]] for k_ in keys])

    def gathered(name, res):
        if name in fwd_carry:
            W.update(zip(fwd_carry[name], res))

    def scatter_on(name):
        keys = bwd_carry.get(name)
        return None if keys is None else _scatter_comm([g_b16[k_] for k_ in keys], [W_AXIS[k_[0]] for k_ in keys])

    def scattered(keys, lands):
        for key, land in zip(keys, lands):
            G[key[0]] = _sum_devices8(g_f32[key], land, W_AXIS[key[0]], G[key[0]], key[1:], f"sum8_{tag(key)}")

    small_landed = []

    def wgrad(key, a, b, rows, other_comm=None):
        name = f"wgrad_{tag(key)}"
        if other_comm is not None:
            assert name not in bwd_carry
            (g_f32[key], g_b16[key]), res = _wgrad(a, b, rows, name, other_comm)
            small_landed.extend(res)
            return
        (g_f32[key], g_b16[key]), lands = _wgrad(a, b, rows, name, scatter_on(name))
        scattered(bwd_carry.get(name, ()), lands)

    saved = []
    xs, xs_ctx = x_lat, x_ctx
    for l in range(depth):
        last = l == depth - 1
        wc = not last
        gvec = norm_g[l]
        ps = pool_scale[l].reshape(1, POOL_WIDTH)
        bexp = _expand_rpb(na_rpb[l], f"bias_expand_{l}")
        name = f"ffn_fwd_{l}_0"
        (xs1, hb1, z1, y1), res = _ffn_fwd(cfg, xs, mods[l], gvec, W["gu", l, 0], W["dn", l, 0], 0, 0, True, name,
                                           gather_on(name), xs_ctx=xs_ctx)
        gathered(name, res)
        hb2, q, k, v, u = _tmpre_fwd(cfg, xs1, mods[l], gvec, W["wi", l], cos, sin, f"tmpre_fwd_{l}")
        name = f"na_fwd_{l}"
        (na_x,), res = _na_fwd(cfg, q, k, v, bexp, name, gather_on(name))
        gathered(name, res)
        na_c = _ctx_attn_fwd(cfg, q, k, v, f"ctx_attn_fwd_{l}") if wc else None
        xs2, opre, mix = _tmpost_fwd(cfg, na_x, na_c, u, band, inv, w_pool[l], ps, W["wo", l], xs1, mods[l], gvec,
                                     f"tmpost_fwd_{l}")
        name = f"ffn_fwd_{l}_1"
        outs, res = _ffn_fwd(cfg, xs2, mods[l], gvec, W["gu", l, 1], W["dn", l, 1], 6, 4, wc, name, gather_on(name),
                             loss_target=target if last else None)
        xs3, hb3, z3, y3 = outs[:4]
        gathered(name, res)
        saved.append(dict(xs=xs, xs_ctx=xs_ctx, xs1=xs1, xs2=xs2, hb1=hb1, z1=z1, y1=y1, hb2=hb2, q=q, k=k, v=v, u=u, mix=mix,
                          opre=opre, hb3=hb3, z3=z3, y3=y3, bexp=bexp, ps=ps, gvec=gvec))
        xs, xs_ctx = xs3, None

    dxs, loss_blk = xs, outs[4]

    small = [None] * depth
    for l in reversed(range(depth)):
        last = l == depth - 1
        wc = not last
        sv = saved[l]
        gvec = sv["gvec"]
        rows_b = cfg.T if wc else cfg.S
        name = f"ffn_bwd_{l}_1"
        (dxs2, dz, dyb, ab, dm678, dg45), lands = _ffn_bwd(cfg, dxs, sv["xs2"], sv["z3"], sv["y3"], mods[l], gvec,
                                                           W["gu", l, 1], W["dn", l, 1], 6, 4, wc, name, scatter_on(name))
        scattered(bwd_carry.get(name, ()), lands)
        wgrad(("gu", l, 1), sv["hb3"], dz, rows_b)
        wgrad(("dn", l, 1), ab, dyb, rows_b)
        dop, dmix, dm5, dg3 = _tmpost_bwd(cfg, dxs2, sv["opre"], W["wo", l], mods[l], gvec, wc, f"tmpost_bwd_{l}")
        wgrad(("wo", l), sv["mix"], dop, rows_b)
        du, dwp, dps = _pool_bwd(cfg, dmix, sv["u"], band, inv, w_pool[l], sv["ps"], wc, f"pool_bwd_{l}")
        name = f"na_bwd_{l}"
        (dq, dk, dv, dkc, dvc, dbexp), lands = _na_bwd(cfg, dmix, sv["q"], sv["k"], sv["v"], sv["bexp"], name, scatter_on(name))
        scattered(bwd_carry.get(name, ()), lands)
        drpb = _rpb_reduce(dbexp, flip, sel, f"rpb_reduce_{l}")
        if wc:
            dqc, dkc2, dvc2 = _ctx_attn_bwd(cfg, dmix, sv["q"], sv["k"], sv["v"], f"ctx_attn_bwd_{l}")
            ctx_terms = ([dqc], [dkc, dkc2], [dvc, dvc2])
        else:
            ctx_terms = ([], [dkc], [dvc])
        dxs1, dproj, dm34, dg2 = _tmpre_bwd(cfg, (dq, dk, dv, du), ctx_terms, wc, cos, sin, W["wi", l], sv["xs1"], mods[l], gvec,
                                            dxs2, wc, f"tmpre_bwd_{l}")
        wgrad(("wi", l), sv["hb2"], dproj, cfg.T)
        name = f"ffn_bwd_{l}_0"
        (dxs, dz, dyb, ab, dm012, dg01), lands = _ffn_bwd(cfg, dxs1, sv["xs"], sv["z1"], sv["y1"], mods[l], gvec,
                                                          W["gu", l, 0], W["dn", l, 0], 0, 0, True, name, scatter_on(name),
                                                          xs_ctx=sv["xs_ctx"])
        scattered(bwd_carry.get(name, ()), lands)
        if not wc:
            zero = lambda a: jnp.concatenate([a, jnp.zeros_like(a)], axis=0)
            dm5, dm678 = zero(dm5), zero(dm678)
        dmods = jnp.concatenate([dm012, dm34, dm5, dm678], axis=1)
        dgs = jnp.concatenate([dg01, dg2, dg3, dg45], axis=0)
        small[l] = dict(dmods=dmods, dg=dgs, drpb=drpb, dwp=dwp, dps=dps)
        small_gather = None
        if l == 0:
            parts = [jnp.stack([small[j][n_] for j in range(depth)]) for n_ in SMALL_NAMES]
            packed = _pack_rows(parts)
            small_gather = _allgather_comm(packed)
        wgrad(("gu", l, 0), sv["hb1"], dz, cfg.T, small_gather)
        wgrad(("dn", l, 0), ab, dyb, cfg.T)
    last_comm = _scatter_comm([g_b16[k_] for k_ in last_scatter], [W_AXIS[k_[0]] for k_ in last_scatter])

    def finish_weight_grads(lands):
        scattered(last_scatter, lands)
        kinds = ("gu", "dn", "wi", "wo")
        return dict(zip(kinds, _half_merge([G[k_] for k_ in kinds], "merge_halves")))

    return loss_blk, dxs, (last_comm, finish_weight_grads), (packed, [p.shape for p in parts], small_landed[0])


def kernel(x, c, ctx, c_ctx, w_mod, b_mod, norm_g, w_ffn_gate_up, w_ffn_down, w_in, w_out, na_rpb, w_pool, pool_scale, loss_target, m_c_ctx, m_w_mod, m_b_mod, m_norm_g, m_w_ffn_gate_up, m_w_ffn_down, m_w_in, m_w_out, m_na_rpb, m_w_pool, m_pool_scale, v_c_ctx, v_w_mod, v_b_mod, v_norm_g, v_w_ffn_gate_up, v_w_ffn_down, v_w_in, v_w_out, v_na_rpb, v_w_pool, v_pool_scale):
    S, D = x.shape[1], x.shape[2]
    L = ctx.shape[1]
    depth = w_mod.shape[0]
    F = w_ffn_down.shape[2] * N_CHIPS
    nmod = w_mod.shape[2]
    gsh = norm_g.shape[2]
    cfg = _Cfg(S, L, D, F)
    mx, my, mc = _mesh_pos()
    chip = 2 * mx + my
    dev = 4 * mx + 2 * my + mc

    W = {}
    for l in range(depth):
        for i in range(2):
            W["gu", l, i] = _cast_into_place(w_ffn_gate_up, (l, i), W_AXIS["gu"], f"cast_gu_{l}_{i}")
            W["dn", l, i] = _cast_into_place(w_ffn_down, (l, i), W_AXIS["dn"], f"cast_dn_{l}_{i}")
        W["wi", l] = _cast_into_place(w_in, (l,), W_AXIS["wi"], f"cast_wi_{l}")
        W["wo", l] = _cast_into_place(w_out, (l,), W_AXIS["wo"], f"cast_wo_{l}")
    first = [("gu", 0, 0), ("dn", 0, 0)]
    W.update(zip(first, _comm_only(_gather_comm([W[k_] for k_ in first], [W_AXIS[k_[0]] for k_ in first]), "gather_first")))
    G = {"gu": lax.empty(w_ffn_gate_up.shape, f32), "dn": lax.empty(w_ffn_down.shape, f32),
         "wi": lax.empty(w_in.shape, f32), "wo": lax.empty(w_out.shape, f32)}

    cg_packed = _pack_rows([c, norm_g])
    cg_all = _all_gather_small(cg_packed, "gather_c_norm_g").reshape(N_DEV, -1)
    c_all = cg_all[:, :D]
    ng = cg_all[:, D:D + norm_g.size].reshape(N_DEV, depth, 6, gsh)
    norm_g_all = jnp.concatenate([ng[2 * j] for j in range(N_CHIPS)], axis=-1)
    cvecs = jnp.concatenate([c_all, c_ctx[None], jnp.zeros((7, D), f32)], axis=0)
    b_shard = lax.dynamic_slice_in_dim(b_mod, chip * nmod, nmod, axis=1).reshape(depth, 1, nmod)
    m_part, silu_c = _modvec_fwd(cvecs, w_mod, b_shard, "modvec_fwd")
    m_all = _all_gather_small(m_part.reshape(depth * 16, nmod), "gather_mod").reshape(N_DEV, depth, 16, nmod)
    m_full = jnp.concatenate([m_all[2 * j] for j in range(N_CHIPS)], axis=-1)
    m_mine = lax.dynamic_index_in_dim(m_full, dev, axis=1, keepdims=False)
    mods = jnp.stack([m_mine, m_full[:, 8]], axis=1).reshape(depth, 2, N_MOD, D)

    loss_blk, dx_lat, (last_comm, finish_weight_grads), small = _local_step(
        cfg, x[0], ctx[0], loss_target[0], mods, norm_g_all, W, G, na_rpb, w_pool, pool_scale)
    loss = lax.psum(loss_blk[0, 0], ("x", "y", "c"))
    grad_x = dx_lat[None]

    packed, shapes, landed = small
    gathered = lax.dynamic_update_index_in_dim(landed, packed, dev, 0)
    total = _unpack_rows(_sum_devices(gathered, "sum_small"), shapes)
    dmods_sum, dg_sum, drpb_sum, dwp_sum, dps_sum = total
    dmods_each = jnp.stack([_unpack_rows(gathered[j], shapes[:1])[0] for j in range(N_DEV)])
    dm_rows = jnp.concatenate([jnp.transpose(dmods_each[:, :, 0], (1, 0, 2, 3)).reshape(depth, N_DEV, N_MOD * D),
                               dmods_sum[:, 1].reshape(depth, 1, N_MOD * D),
                               jnp.zeros((depth, 7, N_MOD * D), f32)], axis=1)
    dm_shard = lax.dynamic_slice_in_dim(dm_rows, chip * nmod, nmod, axis=2)
    grad_w_mod, gc_part = _modvec_bwd(silu_c.T, dm_shard, w_mod, "modvec_bwd")
    gc_all = _all_gather_small(gc_part.reshape(depth * 8, D), "gather_gc").reshape(N_DEV, depth, 8, D)
    grad_b_mod, grad_c_ctx = _small_finish(dm_rows, gc_all, c_ctx)
    grad_norm_g = lax.dynamic_slice_in_dim(dg_sum, chip * gsh, gsh, axis=2)
    grad_na_rpb = drpb_sum[:, :, :2 * NA_KH - 1, :2 * NA_KW - 1]
    grad_w_pool = dwp_sum
    grad_pool_scale = dps_sum.reshape(depth, POOL_WIDTH)

    upd_w_mod, lands = _adamw(w_mod, grad_w_mod, m_w_mod, v_w_mod, "adamw_w_mod", comm=last_comm)
    wgrads = finish_weight_grads(lands)
    g_gu, g_dn, g_wi, g_wo = wgrads["gu"], wgrads["dn"], wgrads["wi"], wgrads["wo"]
    grads = [grad_c_ctx, grad_w_mod, grad_b_mod, grad_norm_g, g_gu, g_dn, g_wi, g_wo, grad_na_rpb, grad_w_pool, grad_pool_scale]
    ws = [c_ctx, w_mod, b_mod, norm_g, w_ffn_gate_up, w_ffn_down, w_in, w_out, na_rpb, w_pool, pool_scale]
    ms = [m_c_ctx, m_w_mod, m_b_mod, m_norm_g, m_w_ffn_gate_up, m_w_ffn_down, m_w_in, m_w_out, m_na_rpb, m_w_pool, m_pool_scale]
    vs = [v_c_ctx, v_w_mod, v_b_mod, v_norm_g, v_w_ffn_gate_up, v_w_ffn_down, v_w_in, v_w_out, v_na_rpb, v_w_pool, v_pool_scale]
    tags = ["c_ctx", "w_mod", "b_mod", "norm_g", "gate_up", "down", "w_in", "w_out", "na_rpb", "w_pool", "pool_scale"]
    merged = ("gate_up", "down", "w_in", "w_out")
    upd = [upd_w_mod if t == "w_mod" else _adamw(w_, g_, m_, v_, f"adamw_{t}", emit_grad=t in merged)
           for w_, g_, m_, v_, t in zip(ws, grads, ms, vs, tags)]
    grads = [u_[3] if t in merged else g_ for g_, u_, t in zip(grads, upd, tags)]
    return (loss, grad_x, *grads, *[u_[0] for u_ in upd], *[u_[1] for u_ in upd], *[u_[2] for u_ in upd])


def _small_finish(dm_rows, gc_all, c_ctx):
    depth, _, n = dm_rows.shape
    D = c_ctx.shape[0]

    def body(dm_ref, gc_ref, c_ref, gb_ref, gcx_ref):
        acc = dm_ref[:, 0]
        for j in range(1, N_DEV + 1):
            acc = acc + dm_ref[:, j]
        gb_ref[...] = acc
        t = jnp.zeros((1, D), f32)
        for l in range(depth):
            for j in range(N_CHIPS):
                t = t + gc_ref[2 * j, l, 0:1, :]
        cv = c_ref[...]
        sg = _sigmoid(cv)
        gcx_ref[...] = t * (sg * (1.0 + cv * (1.0 - sg)))

    gb, gcx = pl.pallas_call(
        body, name="small_finish",
        out_shape=[S_((depth, n), f32), S_((1, D), f32)],
        compiler_params=_cp(VMEM_MID),
    )(dm_rows, gc_all, c_ctx.reshape(1, D))
    return gb, gcx.reshape(D)
```

```python
import functools

import numpy as np
import jax
import jax.numpy as jnp
from jax import lax
from jax.experimental import pallas as pl
from jax.experimental.pallas import tpu as pltpu

f32, bf16 = jnp.float32, jnp.bfloat16

GRID_W = 64
N_MOD = 9
NA_HEADS = 8
HEAD_DIM = 64
NA_WIDTH = NA_HEADS * HEAD_DIM
NA_KH = 8
NA_KW = 16
POOL_GROUPS = 4
POOL_CH = 128
POOL_WIDTH = POOL_GROUPS * POOL_CH
POOL_WINDOWS = (2, 4, 8, 16)
IN_WIDTH = 3 * NA_WIDTH + POOL_WIDTH
MIX_WIDTH = NA_WIDTH + POOL_WIDTH
ROPE_THETA = 10000.0
ROPE_PAIRS = HEAD_DIM // 4
RMS_EPS = 1e-6
NEG_INF = -1e30
ADAM_LR, ADAM_B1, ADAM_B2, ADAM_EPS, ADAM_WD, ADAM_STEP = 0.001, 0.9, 0.999, 1e-08, 0.01, 10

N_DEV = 8
N_CHIPS = 4
LANES = 128
MIB = 1024 * 1024
VMEM_BIG = 52 * MIB
VMEM_MID = 40 * MIB
WGRAD_TN = 1408
WGRAD_TK = 2816
WGRAD_SLACK = 6 * MIB
ELEMWISE_BLOCK = 256 * 1024
MESH = pl.DeviceIdType.MESH
ANY = pl.BlockSpec(memory_space=pl.ANY)
S_ = jax.ShapeDtypeStruct


def _cp(vmem=VMEM_MID, sem=None):
    return pltpu.CompilerParams(vmem_limit_bytes=vmem, dimension_semantics=sem)


def _sigmoid(x):
    return 0.5 * jnp.tanh(0.5 * x) + 0.5


def _rms_hat(x):
    rinv = lax.rsqrt(jnp.mean(x * x, axis=-1, keepdims=True) + RMS_EPS)
    return x * rinv, rinv


def _rms_bwd(dxhat, xhat, rinv):
    return rinv * (dxhat - xhat * jnp.mean(dxhat * xhat, axis=-1, keepdims=True))


def _rsum(a):
    return jnp.sum(a, axis=0, keepdims=True)


def _nt(a, b):
    return lax.dot_general(a, b, (((1,), (1,)), ((), ())), preferred_element_type=f32)


def _tn(a, b):
    return lax.dot_general(a, b, (((0,), (0,)), ((), ())), preferred_element_type=f32)


def _nn(a, b):
    return jnp.dot(a, b, preferred_element_type=f32)


def _swap16(x):
    lane = lax.broadcasted_iota(jnp.int32, x.shape, 1)
    n = x.shape[1]
    return jnp.where((lane % 32) < 16, pltpu.roll(x, n - 16, 1), pltpu.roll(x, 16, 1))


def _rope_tables(s_len, l_len):
    t = np.arange(s_len)
    inv = ROPE_THETA ** (-np.arange(ROPE_PAIRS, dtype=np.float32) / ROPE_PAIRS)
    ang_r = (t // GRID_W).astype(np.float32)[:, None] * inv
    ang_c = (t % GRID_W).astype(np.float32)[:, None] * inv
    cos = np.concatenate([np.cos(ang_r), np.cos(ang_r), np.cos(ang_c), np.cos(ang_c)], axis=-1)
    sin = np.concatenate([-np.sin(ang_r), np.sin(ang_r), -np.sin(ang_c), np.sin(ang_c)], axis=-1)
    cos = np.concatenate([cos, np.ones((l_len, HEAD_DIM), np.float32)], axis=0)
    sin = np.concatenate([sin, np.zeros((l_len, HEAD_DIM), np.float32)], axis=0)
    return (jnp.asarray(np.tile(cos, (1, 2)), f32), jnp.asarray(np.tile(sin, (1, 2)), f32))


def _pool_tables(tm, l_len):
    band = np.zeros((2, POOL_GROUPS, tm, tm), np.float32)
    inv = np.zeros((2, POOL_GROUPS, tm, 1), np.float32)
    for typ, length in ((0, GRID_W), (1, l_len)):
        for g, w in enumerate(POOL_WINDOWS):
            for t in range(tm):
                base, p = (t // length) * length, t % length
                lo = min(max(p - w // 2, 0), length)
                hi = min(max(p - w // 2 + w, 0), length)
                band[typ, g, t, base + lo:base + hi] = 1.0
                inv[typ, g, t, 0] = 1.0 / (hi - lo)
    return jnp.asarray(band, bf16), jnp.asarray(inv, f32)


NA_QR = 4
NA_WR = NA_KH + NA_QR - 1
NA_TYPES = 3
NA_SEL_ROWS = 136
NA_WPAD = 768


def _rpb_index_tables():
    j = np.arange(GRID_W)
    col_start = np.clip(j - NA_KW // 2, 0, GRID_W - NA_KW)
    valid = (j[None, :] >= col_start[:, None]) & (j[None, :] < col_start[:, None] + NA_KW)
    dc = np.clip(j[None, :] - j[:, None] + NA_KW - 1, 0, 2 * NA_KW - 2)
    i = np.arange(NA_QR)[:, None]
    kk = np.arange(NA_WR)[None, :]
    off = np.stack([np.zeros_like(i), i, np.full_like(i, NA_QR - 1)])
    d = np.stack([kk - i + NA_KH - 1, kk - i + NA_KH - 1 - NA_QR, kk - i])
    row_ok = (kk[None] >= off) & (kk[None] < off + NA_KH)
    assert (d[row_ok] >= 0).all() and (d[row_ok] <= 2 * NA_KH - 2).all()
    return valid, dc, d, row_ok


def _expand_rpb(rpb, name):
    _, _, d, row_ok = _rpb_index_tables()
    heads, nd, ne = rpb.shape
    w = GRID_W
    v = jnp.pad(rpb, ((0, 0), (0, 0), (w - NA_KW, 2 * w - (w - NA_KW) - ne)))
    x = jnp.broadcast_to(v[:, :, None, :], (heads, nd, w, 2 * w)).reshape(heads, nd, 2 * w * w)
    t = x[:, :, :w * (2 * w - 1)].reshape(heads, nd, w, 2 * w - 1)[..., w - 1:]

    def body(t_ref, o_ref):
        q = lax.broadcasted_iota(jnp.int32, (w, w), 0)
        c = lax.broadcasted_iota(jnp.int32, (w, w), 1)
        c0 = jnp.clip(q - NA_KW // 2, 0, w - NA_KW)
        in_cols = (c >= c0) & (c < c0 + NA_KW)
        outside = jnp.full((w, w), NEG_INF, f32)
        blocks = [jnp.where(in_cols, t_ref[dd], NEG_INF) for dd in range(nd)]
        for typ in range(NA_TYPES):
            for i in range(NA_QR):
                row = [blocks[d[typ, i, kk]] if row_ok[typ, i, kk] else outside for kk in range(NA_WR)]
                o_ref[typ, i * w:(i + 1) * w, :] = jnp.concatenate(row, axis=1)

    return pl.pallas_call(
        body, name=name, grid=(heads,),
        in_specs=[pl.BlockSpec((None, nd, w, w), lambda h: (h, 0, 0, 0))],
        out_specs=pl.BlockSpec((NA_TYPES, None, NA_QR * w, NA_WR * w), lambda h: (0, h, 0, 0)),
        out_shape=S_((NA_TYPES, heads, NA_QR * w, NA_WR * w), f32),
        compiler_params=_cp(VMEM_MID, ("arbitrary",)),
    )(t)


def _rpb_reduce_tables():
    _, _, d, row_ok = _rpb_index_tables()
    flip = np.eye(GRID_W, dtype=np.float32)[::-1].copy()
    sel = np.zeros((16, NA_SEL_ROWS), np.float32)
    flat_d, flat_ok = d.reshape(-1), row_ok.reshape(-1)
    for n in range(flat_d.size):
        if flat_ok[n]:
            sel[flat_d[n], n] = 1.0
    return jnp.asarray(flip), jnp.asarray(sel)


class _Cfg:
    def __init__(self, s_len, l_len, d, f):
        self.S, self.L, self.D, self.F = s_len, l_len, d, f
        self.T = s_len + l_len
        self.TM = 256 if l_len % 256 == 0 else 128
        assert l_len == self.TM, "context length must equal the row tile"
        assert s_len % self.TM == 0 and s_len % GRID_W == 0
        self.nxt = s_len // self.TM
        self.ntt = self.T // self.TM
        self.rows = s_len // GRID_W
        assert self.rows >= 2 * NA_KH
        assert f % (2 * LANES) == 0
        self.FC = f

    def ntiles(self, with_ctx):
        return self.ntt if with_ctx else self.nxt


def _typ(cfg):
    return lambda i: (jnp.minimum(i // cfg.nxt, 1), 0, 0)


def _mesh_pos():
    return lax.axis_index("x"), lax.axis_index("y"), lax.axis_index("c")


class _Comm:
    def __init__(self, ins, outs, alias, nsem, start, finish):
        self.ins, self.outs, self.alias, self.nsem, self.start, self.finish = ins, outs, alias, nsem, start, finish


def _call(body, args, comm=None, *, grid, in_specs, out_specs, out_shape, scratch_shapes=(), **kw):
    if comm is None:
        return pl.pallas_call(body, grid=grid, in_specs=list(in_specs), out_specs=list(out_specs), out_shape=list(out_shape),
                              scratch_shapes=list(scratch_shapes), **kw)(*args), ()
    n_in, n_out, n_sc = len(in_specs), len(out_specs), len(scratch_shapes)
    ci, co = len(comm.ins), len(comm.outs)

    def carrier(*refs):
        bounds = np.cumsum([0, n_in, ci, n_out, co, n_sc])
        ins, cins, outs, couts, scr = (refs[a:b] for a, b in zip(bounds[:-1], bounds[1:]))
        send, recv = refs[bounds[-1]], refs[bounds[-1] + 1]
        first = functools.reduce(jnp.logical_and, [pl.program_id(a) == 0 for a in range(len(grid))])
        last = functools.reduce(jnp.logical_and, [pl.program_id(a) == g - 1 for a, g in enumerate(grid)])

        @pl.when(first)
        def _():
            comm.start(cins, couts, send, recv)

        body(*ins, *outs, *scr)

        @pl.when(last)
        def _():
            comm.finish(cins, couts, send, recv)

    res = pl.pallas_call(
        carrier, grid=grid, in_specs=list(in_specs) + [ANY] * ci, out_specs=list(out_specs) + [ANY] * co,
        out_shape=list(out_shape) + list(comm.outs),
        input_output_aliases={n_in + a: n_out + b for a, b in comm.alias.items()},
        scratch_shapes=list(scratch_shapes) + [pltpu.SemaphoreType.DMA((comm.nsem,)), pltpu.SemaphoreType.DMA((comm.nsem,))],
        **kw)(*args, *comm.ins)
    return res[:n_out], res[n_out:]


def _comm_only(comm, name):
    ci, co = len(comm.ins), len(comm.outs)

    def body(*refs):
        cins, couts = refs[:ci], refs[ci:ci + co]
        send, recv = refs[ci + co], refs[ci + co + 1]
        comm.start(cins, couts, send, recv)
        comm.finish(cins, couts, send, recv)

    return pl.pallas_call(
        body, name=name, in_specs=[ANY] * ci, out_specs=[ANY] * co, out_shape=list(comm.outs),
        input_output_aliases=dict(comm.alias),
        scratch_shapes=[pltpu.SemaphoreType.DMA((comm.nsem,)), pltpu.SemaphoreType.DMA((comm.nsem,))],
        compiler_params=_cp(VMEM_MID),
    )(*comm.ins)


def _half_view(ref, axis, kk, h):
    r, c = ref.shape
    if axis == 1:
        n = c // N_CHIPS
        return ref.at[pl.ds(h * (r // 2), r // 2), pl.ds(pl.multiple_of(kk * n, LANES), n)]
    n = r // N_CHIPS
    return ref.at[pl.ds(pl.multiple_of(kk * n + h * (n // 2), 8), n // 2), :]


def _other_chips(x, y):
    return [(1 - x, y), (x, 1 - y), (1 - x, 1 - y)]


def _gather_comm(arrs, axes):
    n = len(arrs)

    def copy(ref, view, sems, k, to):
        send, recv = sems
        return pltpu.make_async_remote_copy(src_ref=view, dst_ref=view, send_sem=send.at[k], recv_sem=recv.at[k],
                                            device_id=to, device_id_type=MESH)

    def start(cins, bufs, send, recv):
        x, y, c = _mesh_pos()
        for t in range(n):
            own = _half_view(bufs[t], axes[t], 2 * x + y, c)
            for j, chip in enumerate(_other_chips(x, y)):
                copy(bufs[t], own, (send, recv), 6 * t + j, (*chip, c)).start()

    def finish(cins, bufs, send, recv):
        x, y, c = _mesh_pos()
        sibling = (x, y, 1 - c)
        chips = _other_chips(x, y)
        for t in range(n):
            for j, chip in enumerate(chips):
                landed = _half_view(bufs[t], axes[t], 2 * chip[0] + chip[1], c)
                copy(bufs[t], landed, (send, recv), 6 * t + j, (*chip, c)).wait_recv()
                copy(bufs[t], landed, (send, recv), 6 * t + 3 + j, sibling).start()
        for t in range(n):
            own = _half_view(bufs[t], axes[t], 2 * x + y, c)
            for j, chip in enumerate(chips):
                kj = 2 * chip[0] + chip[1]
                copy(bufs[t], _half_view(bufs[t], axes[t], kj, 1 - c), (send, recv), 6 * t + 3 + j, sibling).wait_recv()
                copy(bufs[t], own, (send, recv), 6 * t + j, (*chip, c)).wait_send()
                copy(bufs[t], _half_view(bufs[t], axes[t], kj, c), (send, recv), 6 * t + 3 + j, sibling).wait_send()

    return _Comm(list(arrs), [S_(a.shape, a.dtype) for a in arrs], {t: t for t in range(n)}, 6 * n, start, finish)


def _scatter_comm(parts, axes):
    n = len(parts)
    peers = [(fx, fy, fc) for fx in (0, 1) for fy in (0, 1) for fc in (0, 1)][1:]

    def half_shape(a, axis):
        r, c = a.shape
        return (r // 2, c // N_CHIPS) if axis == 1 else (r // N_CHIPS // 2, c)

    def start(srcs, lands, send, recv):
        x, y, c = _mesh_pos()
        me = 4 * x + 2 * y + c
        for t in range(n):
            for r_, (fx, fy, fc) in enumerate(peers):
                dx, dy, dc = (1 - x if fx else x), (1 - y if fy else y), (1 - c if fc else c)
                pltpu.make_async_remote_copy(
                    src_ref=_half_view(srcs[t], axes[t], 2 * dx + dy, dc), dst_ref=lands[t].at[me],
                    send_sem=send.at[7 * t + r_], recv_sem=recv.at[7 * t + r_],
                    device_id=(dx, dy, dc), device_id_type=MESH).start()

    def finish(srcs, lands, send, recv):
        x, y, c = _mesh_pos()
        for t in range(n):
            mine = _half_view(srcs[t], axes[t], 2 * x + y, c)
            for r_, (fx, fy, fc) in enumerate(peers):
                sx, sy, sc = (1 - x if fx else x), (1 - y if fy else y), (1 - c if fc else c)
                cp = pltpu.make_async_remote_copy(
                    src_ref=mine, dst_ref=lands[t].at[4 * sx + 2 * sy + sc],
                    send_sem=send.at[7 * t + r_], recv_sem=recv.at[7 * t + r_],
                    device_id=(sx, sy, sc), device_id_type=MESH)
                cp.wait_recv()
                cp.wait_send()

    return _Comm(list(parts), [S_((N_DEV,) + half_shape(a, ax), a.dtype) for a, ax in zip(parts, axes)], {}, 7 * n, start, finish)


def _allgather_comm(block):
    peers = [(fx, fy, fc) for fx in (0, 1) for fy in (0, 1) for fc in (0, 1)][1:]

    def ends(x, y, c):
        for r_, (fx, fy, fc) in enumerate(peers):
            yield r_, ((1 - x if fx else x), (1 - y if fy else y), (1 - c if fc else c))

    def start(srcs, lands, send, recv):
        x, y, c = _mesh_pos()
        for r_, peer in ends(x, y, c):
            pltpu.make_async_remote_copy(src_ref=srcs[0], dst_ref=lands[0].at[4 * x + 2 * y + c], send_sem=send.at[r_],
                                         recv_sem=recv.at[r_], device_id=peer, device_id_type=MESH).start()

    def finish(srcs, lands, send, recv):
        x, y, c = _mesh_pos()
        for r_, (px, py, pc) in ends(x, y, c):
            cp = pltpu.make_async_remote_copy(src_ref=srcs[0], dst_ref=lands[0].at[4 * px + 2 * py + pc], send_sem=send.at[r_],
                                              recv_sem=recv.at[r_], device_id=(px, py, pc), device_id_type=MESH)
            cp.wait_recv()
            cp.wait_send()

    return _Comm([block], [S_((N_DEV,) + block.shape, block.dtype)], {}, len(peers), start, finish)


def _ffn_fwd(cfg, xs, mods, gvec, wgu, wd, mi, gi, with_ctx, name, comm=None, xs_ctx=None, loss_target=None):
    TM, D, F, FC = cfg.TM, cfg.D, cfg.F, cfg.FC
    nt = cfg.ntiles(with_ctx)
    R = nt * TM
    split, head = xs_ctx is not None, loss_target is not None

    def body(*refs):
        it = iter(refs)
        xs_ref = next(it)
        xc_ref = next(it) if split else None
        mods_ref, g_ref, wgu_hbm, wd_hbm = next(it), next(it), next(it), next(it)
        t_ref = next(it) if head else None
        out_ref, hb_ref, z_ref, y_ref = next(it), next(it), next(it), next(it)
        loss_ref = next(it) if head else None
        wgu_v, wd_v, sem = next(it), next(it), next(it)
        i = pl.program_id(0)

        @pl.when(i == 0)
        def _():
            c0 = pltpu.make_async_copy(wgu_hbm, wgu_v, sem.at[0])
            c1 = pltpu.make_async_copy(wd_hbm, wd_v, sem.at[1])
            c0.start(); c1.start(); c0.wait(); c1.wait()
            if head:
                loss_ref[...] = jnp.zeros_like(loss_ref)
        x = xs_ref[...]
        if split:
            x = jnp.where(i < cfg.nxt, x, xc_ref[...])
        m = mods_ref[0]
        sh, sc, gt = m[mi:mi + 1], m[mi + 1:mi + 2], m[mi + 2:mi + 3]
        xhat, _ = _rms_hat(x)
        h = (xhat * g_ref[gi:gi + 1]) * (1.0 + sc) + sh
        hb = h.astype(bf16)
        hb_ref[...] = hb
        y = jnp.zeros((TM, D), f32)
        for ch in range(F // FC):
            zg = _nn(hb, wgu_v[:, ch * FC:(ch + 1) * FC])
            zu = _nn(hb, wgu_v[:, F + ch * FC:F + (ch + 1) * FC])
            z_ref[:, ch * FC:(ch + 1) * FC] = zg.astype(bf16)
            z_ref[:, F + ch * FC:F + (ch + 1) * FC] = zu.astype(bf16)
            a = (zg * _sigmoid(zg)) * zu
            y = y + _nn(a.astype(bf16), wd_v[ch * FC:(ch + 1) * FC, :])
        y_ref[...] = y
        yhat, _ = _rms_hat(y)
        out = x + 0.5 * gt * (yhat * g_ref[gi + 1:gi + 2])
        if head:
            e = out - t_ref[...]
            out_ref[...] = e * (1.0 / D)
            loss_ref[...] += jnp.sum(jnp.mean(e * e, axis=-1, keepdims=True), axis=0, keepdims=True) * 0.5
        else:
            out_ref[...] = out

    rt = lambda c: pl.BlockSpec((TM, c), lambda i: (i, 0))
    lat = pl.BlockSpec((TM, D), lambda i: (jnp.minimum(i, cfg.nxt - 1), 0))
    x_specs, x_args = ([lat, pl.BlockSpec((TM, D), lambda i: (0, 0))], [xs, xs_ctx]) if split else ([rt(D)], [xs])
    t_specs, t_args = ([rt(D)], [loss_target]) if head else ([], [])
    l_specs, l_shape = ([pl.BlockSpec((8, LANES), lambda i: (0, 0))], [S_((8, LANES), f32)]) if head else ([], [])
    return _call(
        body, (*x_args, mods, gvec, wgu, wd, *t_args), comm, name=name, grid=(nt,),
        in_specs=x_specs + [pl.BlockSpec((1, N_MOD, D), _typ(cfg)), pl.BlockSpec((6, D), lambda i: (0, 0)), ANY, ANY] + t_specs,
        out_specs=[rt(D), rt(D), rt(2 * F), rt(D)] + l_specs,
        out_shape=[S_((R, D), f32), S_((R, D), bf16), S_((R, 2 * F), bf16), S_((R, D), f32)] + l_shape,
        scratch_shapes=[pltpu.VMEM((D, 2 * F), bf16), pltpu.VMEM((F, D), bf16), pltpu.SemaphoreType.DMA((2,))],
        compiler_params=_cp(VMEM_BIG, ("arbitrary",)),
    )


def _ffn_bwd(cfg, dout, xs, z, y, mods, gvec, wgu, wd, mi, gi, with_ctx, name, comm=None, xs_ctx=None):
    TM, D, F, FC = cfg.TM, cfg.D, cfg.F, cfg.FC
    nt = cfg.ntiles(with_ctx)
    R = nt * TM
    ntyp = 2 if with_ctx else 1
    split = xs_ctx is not None

    def body(*refs):
        it = iter(refs)
        do_ref, xs_ref = next(it), next(it)
        xc_ref = next(it) if split else None
        z_ref, y_ref, mods_ref, g_ref, wgu_hbm, wd_hbm = (next(it) for _ in range(6))
        dx_ref, dz_ref, dy_ref, a_ref, dm_ref, dg_ref, wgu_v, wd_v, sem = (next(it) for _ in range(9))
        i = pl.program_id(0)

        @pl.when(i == 0)
        def _():
            c0 = pltpu.make_async_copy(wgu_hbm, wgu_v, sem.at[0])
            c1 = pltpu.make_async_copy(wd_hbm, wd_v, sem.at[1])
            c0.start(); c1.start(); c0.wait(); c1.wait()
            dg_ref[...] = jnp.zeros_like(dg_ref)

        @pl.when((i == 0) | (i == cfg.nxt))
        def _():
            dm_ref[...] = jnp.zeros_like(dm_ref)

        do = do_ref[...]
        x = xs_ref[...]
        if split:
            x = jnp.where(i < cfg.nxt, x, xc_ref[...])
        m = mods_ref[0]
        sc, gt = m[mi + 1:mi + 2], m[mi + 2:mi + 3]
        g_pre, g_post = g_ref[gi:gi + 1], g_ref[gi + 1:gi + 2]
        xhat, rinv0 = _rms_hat(x)
        n0 = xhat * g_pre
        yhat, rinv1 = _rms_hat(y_ref[...])
        d_gt = _rsum(0.5 * do * (yhat * g_post))
        dr = (0.5 * gt) * do
        dg_post = _rsum(dr * yhat)
        dy = _rms_bwd(dr * g_post, yhat, rinv1)
        dyb = dy.astype(bf16)
        dy_ref[...] = dyb
        dh = jnp.zeros((TM, D), f32)
        for ch in range(F // FC):
            zg = z_ref[:, ch * FC:(ch + 1) * FC].astype(f32)
            zu = z_ref[:, F + ch * FC:F + (ch + 1) * FC].astype(f32)
            sg = _sigmoid(zg)
            silu = zg * sg
            a_ref[:, ch * FC:(ch + 1) * FC] = (silu * zu).astype(bf16)
            da = _nt(dyb, wd_v[ch * FC:(ch + 1) * FC, :])
            dzu = (da * silu).astype(bf16)
            dzg = (da * zu * (sg * (1.0 + zg * (1.0 - sg)))).astype(bf16)
            dz_ref[:, ch * FC:(ch + 1) * FC] = dzg
            dz_ref[:, F + ch * FC:F + (ch + 1) * FC] = dzu
            dh = dh + _nt(dzg, wgu_v[:, ch * FC:(ch + 1) * FC]) + _nt(dzu, wgu_v[:, F + ch * FC:F + (ch + 1) * FC])
        d_sh = _rsum(dh)
        d_sc = _rsum(dh * n0)
        dn = dh * (1.0 + sc)
        dg_pre = _rsum(dn * xhat)
        dx = do + _rms_bwd(dn * g_pre, xhat, rinv0)
        if split:
            @pl.when(i < cfg.nxt)
            def _():
                dx_ref[...] = dx
        else:
            dx_ref[...] = dx
        dm_ref[0] += jnp.concatenate([d_sh, d_sc, d_gt], axis=0)
        dg_ref[...] += jnp.concatenate([dg_pre, dg_post], axis=0)

    rt = lambda c: pl.BlockSpec((TM, c), lambda i: (i, 0))
    lat = pl.BlockSpec((TM, D), lambda i: (jnp.minimum(i, cfg.nxt - 1), 0))
    x_specs, x_args = ([lat, pl.BlockSpec((TM, D), lambda i: (0, 0))], [xs, xs_ctx]) if split else ([rt(D)], [xs])
    return _call(
        body, (dout, *x_args, z, y, mods, gvec, wgu, wd), comm, name=name, grid=(nt,),
        in_specs=[rt(D)] + x_specs + [rt(2 * F), rt(D), pl.BlockSpec((1, N_MOD, D), _typ(cfg)),
                                       pl.BlockSpec((6, D), lambda i: (0, 0)), ANY, ANY],
        out_specs=[lat if split else rt(D), rt(2 * F), rt(D), rt(F), pl.BlockSpec((1, 3, D), _typ(cfg)),
                   pl.BlockSpec((2, D), lambda i: (0, 0))],
        out_shape=[S_((cfg.S if split else R, D), f32), S_((R, 2 * F), bf16), S_((R, D), bf16), S_((R, F), bf16),
                   S_((ntyp, 3, D), f32), S_((2, D), f32)],
        scratch_shapes=[pltpu.VMEM((D, 2 * F), bf16), pltpu.VMEM((F, D), bf16), pltpu.SemaphoreType.DMA((2,))],
        compiler_params=_cp(VMEM_BIG, ("arbitrary",)),
    )


def _wgrad(a, b, k_rows, name, comm=None):
    M, N = a.shape[1], b.shape[1]
    tn = _div_tile(N, 1, WGRAD_TN, LANES) if N > WGRAD_TN // 2 else N
    tn = N // 2 if tn == N and N % (2 * LANES) == 0 else tn
    room = VMEM_BIG - WGRAD_SLACK - 2 * M * tn * 6
    tk = _div_tile(k_rows, 1, min(WGRAD_TK, room // (4 * (M + tn))), LANES)
    nk = k_rows // tk

    def body(a_ref, b_ref, o_ref, ob_ref):
        k = pl.program_id(1)

        @pl.when(k == 0)
        def _():
            o_ref[...] = jnp.zeros_like(o_ref)
        o_ref[...] += _tn(a_ref[...], b_ref[...])

        @pl.when(k == nk - 1)
        def _():
            ob_ref[...] = o_ref[...].astype(bf16)

    ospec = pl.BlockSpec((M, tn), lambda n, k: (0, n))
    return _call(
        body, (a, b), comm, name=name, grid=(N // tn, nk),
        in_specs=[pl.BlockSpec((tk, M), lambda n, k: (k, 0)), pl.BlockSpec((tk, tn), lambda n, k: (k, n))],
        out_specs=[ospec, ospec], out_shape=[S_((M, N), f32), S_((M, N), bf16)],
        compiler_params=_cp(VMEM_BIG, ("arbitrary", "arbitrary")),
    )


def _tmpre_fwd(cfg, xs, mods, gvec, w_in, cos, sin, name, comm=None):
    TM, D = cfg.TM, cfg.D
    nt, R = cfg.ntt, cfg.T
    W = NA_WIDTH

    def body(xs_ref, mods_ref, g_ref, w_ref, cos_ref, sin_ref, hb_ref, q_ref, k_ref, v_ref, u_ref):
        x = xs_ref[...]
        m = mods_ref[0]
        xhat, _ = _rms_hat(x)
        hb = ((xhat * g_ref[2:3]) * (1.0 + m[4:5]) + m[3:4]).astype(bf16)
        hb_ref[...] = hb
        p = _nn(hb, w_ref[...])
        cs = jnp.tile(cos_ref[...], (1, W // LANES))
        sn = jnp.tile(sin_ref[...], (1, W // LANES))
        q = p[:, 0:W]
        k = p[:, W:2 * W]
        q_ref[...] = ((q * cs + _swap16(q) * sn) * (HEAD_DIM ** -0.5)).astype(bf16)
        k_ref[...] = (k * cs + _swap16(k) * sn).astype(bf16)
        v_ref[...] = p[:, 2 * W:3 * W].astype(bf16)
        u_ref[...] = p[:, 3 * W:]

    rt = lambda c: pl.BlockSpec((TM, c), lambda i: (i, 0))
    return _call(
        body, (xs, mods, gvec, w_in, cos, sin), comm, name=name, grid=(nt,),
        in_specs=[rt(D), pl.BlockSpec((1, N_MOD, D), _typ(cfg)), pl.BlockSpec((6, D), lambda i: (0, 0)),
                  pl.BlockSpec((D, IN_WIDTH), lambda i: (0, 0)), rt(LANES), rt(LANES)],
        out_specs=[rt(D), rt(W), rt(W), rt(W), rt(POOL_WIDTH)],
        out_shape=[S_((R, D), bf16), S_((R, W), bf16), S_((R, W), bf16), S_((R, W), bf16), S_((R, POOL_WIDTH), f32)],
        compiler_params=_cp(VMEM_MID, ("arbitrary",)),
    )


def _tmpre_bwd(cfg, lat, ctx_terms, du_has_ctx, cos, sin, w_in, xs, mods, gvec, dres, res_with_ctx, name):
    TM, D = cfg.TM, cfg.D
    nt, R = cfg.ntt, cfg.T
    nres = cfg.ntiles(res_with_ctx)
    W = NA_WIDTH
    n_ctx = [len(t) for t in ctx_terms]
    flat_ctx = [a for t in ctx_terms for a in t]
    n_asm = 4 + len(flat_ctx) + 2

    def assemble(refs, o_ref):
        dq_ref, dk_ref, dv_ref, du_ref = refs[:4]
        ctx_refs = refs[4:4 + len(flat_ctx)]
        cos_ref, sin_ref = refs[4 + len(flat_ctx):]
        is_ctx = pl.program_id(0) >= cfg.nxt
        vals, off = [], 0
        for lat_ref, n in zip((dq_ref, dk_ref, dv_ref), n_ctx):
            cv = jnp.zeros((TM, W), f32)
            for r_ in ctx_refs[off:off + n]:
                cv = cv + r_[...]
            off += n
            vals.append(jnp.where(is_ctx, cv, lat_ref[...]))
        du_ = du_ref[...] if du_has_ctx else jnp.where(is_ctx, 0.0, du_ref[...])
        cs = jnp.tile(cos_ref[...], (1, W // LANES))
        sn = jnp.tile(sin_ref[...], (1, W // LANES))
        dq_ = vals[0] * (HEAD_DIM ** -0.5)
        dk_ = vals[1]
        o_ref[:, 0:W] = (dq_ * cs + _swap16(dq_ * sn)).astype(bf16)
        o_ref[:, W:2 * W] = (dk_ * cs + _swap16(dk_ * sn)).astype(bf16)
        o_ref[:, 2 * W:3 * W] = vals[2].astype(bf16)
        o_ref[:, 3 * W:] = du_.astype(bf16)

    def body(*refs):
        w_ref, xs_ref, mods_ref, g_ref, dres_ref, dx_ref, dp_ref, dm_ref, dg_ref = refs[n_asm:]
        i = pl.program_id(0)

        @pl.when(i == 0)
        def _():
            dg_ref[...] = jnp.zeros_like(dg_ref)

        @pl.when((i == 0) | (i == cfg.nxt))
        def _():
            dm_ref[...] = jnp.zeros_like(dm_ref)

        assemble(refs[:n_asm], dp_ref)
        dh = _nt(dp_ref[...], w_ref[...])
        x = xs_ref[...]
        m = mods_ref[0]
        g2 = g_ref[2:3]
        xhat, rinv = _rms_hat(x)
        d_sh = _rsum(dh)
        d_sc = _rsum(dh * (xhat * g2))
        dn = dh * (1.0 + m[4:5])
        dg_ref[...] += _rsum(dn * xhat)
        dx = _rms_bwd(dn * g2, xhat, rinv)
        res = dres_ref[...]
        if nres < nt:
            res = jnp.where(i < nres, res, 0.0)
        dx_ref[...] = res + dx
        dm_ref[0] += jnp.concatenate([d_sh, d_sc], axis=0)

    rt = lambda c: pl.BlockSpec((TM, c), lambda i: (i, 0))
    lat_spec = pl.BlockSpec((TM, W), lambda i: (jnp.minimum(i, cfg.nxt - 1), 0))
    du_spec = rt(POOL_WIDTH) if du_has_ctx else lat_spec
    asm_specs = ([lat_spec, lat_spec, lat_spec, du_spec] + [pl.BlockSpec((TM, W), lambda i: (0, 0))] * len(flat_ctx)
                 + [rt(LANES), rt(LANES)])
    return pl.pallas_call(
        body, name=name, grid=(nt,),
        in_specs=asm_specs + [pl.BlockSpec((D, IN_WIDTH), lambda i: (0, 0)), rt(D),
                              pl.BlockSpec((1, N_MOD, D), _typ(cfg)), pl.BlockSpec((6, D), lambda i: (0, 0)),
                              pl.BlockSpec((TM, D), lambda i: (jnp.minimum(i, nres - 1), 0))],
        out_specs=[rt(D), rt(IN_WIDTH), pl.BlockSpec((1, 2, D), _typ(cfg)), pl.BlockSpec((1, D), lambda i: (0, 0))],
        out_shape=[S_((R, D), f32), S_((R, IN_WIDTH), bf16), S_((2, 2, D), f32), S_((1, D), f32)],
        compiler_params=_cp(VMEM_MID, ("arbitrary",)),
    )(*lat, *flat_ctx, cos, sin, w_in, xs, mods, gvec, dres)


def _na_block(cfg, b):
    return jnp.clip(NA_QR * b - NA_KH // 2, 0, cfg.rows - NA_WR)


def _na_load_bias(b, nb, b_hbm, b_v, sem):
    for typ, at in ((0, 0), (1, 1), (2, nb - 1)):
        @pl.when(b == at)
        def _(typ=typ):
            cp = pltpu.make_async_copy(b_hbm.at[typ], b_v, sem)
            cp.start()
            cp.wait()


def _na_probs(qh, klh, kch, bias):
    s_loc = _nt(qh, klh) + bias
    s_ctx = _nt(qh, kch)
    mx = jnp.maximum(jnp.max(s_loc, axis=-1, keepdims=True), jnp.max(s_ctx, axis=-1, keepdims=True))
    e_loc = jnp.exp(s_loc - mx)
    e_ctx = jnp.exp(s_ctx - mx)
    inv = 1.0 / (jnp.sum(e_loc, axis=-1, keepdims=True) + jnp.sum(e_ctx, axis=-1, keepdims=True))
    return e_loc * inv, e_ctx * inv


def _na_fwd(cfg, q, k, v, bexp, name, comm=None):
    S, L, T = cfg.S, cfg.L, cfg.T
    NQ, NW = NA_QR * GRID_W, NA_WR * GRID_W
    nb = cfg.rows // NA_QR

    def body(q_ref, k_hbm, v_hbm, b_hbm, o_ref, k_v, v_v, b_v, sem):
        b = pl.program_id(0)

        @pl.when(b == 0)
        def _():
            cs = [pltpu.make_async_copy(k_hbm, k_v, sem.at[0]), pltpu.make_async_copy(v_hbm, v_v, sem.at[1])]
            for c_ in cs:
                c_.start()
            for c_ in cs:
                c_.wait()

        _na_load_bias(b, nb, b_hbm, b_v, sem.at[2])
        st = pl.multiple_of(_na_block(cfg, b) * GRID_W, GRID_W)
        first = lax.broadcasted_iota(jnp.int32, (NQ, LANES), 1) < HEAD_DIM
        for hp in range(NA_HEADS // 2):
            ls = slice(hp * LANES, (hp + 1) * LANES)
            q2 = q_ref[:, ls]
            kl, vl = k_v[pl.ds(st, NW), ls], v_v[pl.ds(st, NW), ls]
            kc, vc = k_v[S:T, ls], v_v[S:T, ls]
            o2 = []
            for hh in range(2):
                qm = jnp.where(first if hh == 0 else ~first, q2, jnp.zeros_like(q2))
                p_loc, p_ctx = _na_probs(qm, kl, kc, b_v[2 * hp + hh])
                o2.append(_nn(p_loc.astype(bf16), vl) + _nn(p_ctx.astype(bf16), vc))
            o_ref[:, ls] = jnp.where(first, o2[0], o2[1]).astype(bf16)

    return _call(
        body, (q, k, v, bexp), comm, name=name, grid=(nb,),
        in_specs=[pl.BlockSpec((NQ, NA_WIDTH), lambda b: (b, 0)), ANY, ANY, ANY],
        out_specs=[pl.BlockSpec((NQ, NA_WIDTH), lambda b: (b, 0))],
        out_shape=[S_((S, NA_WIDTH), bf16)],
        scratch_shapes=[pltpu.VMEM((T, NA_WIDTH), bf16), pltpu.VMEM((T, NA_WIDTH), bf16),
                        pltpu.VMEM((NA_HEADS, NQ, NW), f32), pltpu.SemaphoreType.DMA((3,))],
        compiler_params=_cp(VMEM_MID, ("arbitrary",)),
    )


def _na_bwd(cfg, do, q, k, v, bexp, name, comm=None):
    S, L, T, rows = cfg.S, cfg.L, cfg.T, cfg.rows
    NQ, NW = NA_QR * GRID_W, NA_WR * GRID_W
    NSLOT = 2 * NA_KH
    nb = rows // NA_QR
    bmax = (rows - NA_WR) // NA_QR
    steps = 2 * nb - bmax
    W = NA_WIDTH
    assert nb >= 3 and bmax >= 1 and rows - NA_QR * bmax <= NSLOT

    def out_group(g):
        return jnp.where(g >= nb, g - nb + bmax, jnp.clip(g - 1, 0, bmax - 1))

    def body(do_ref, q_ref, k_hbm, v_hbm, b_hbm, dq_ref, dk_ref, dv_ref, dkc_ref, dvc_ref, db_hbm,
             k_v, v_v, b_v, db_v, ak, av, akc, avc, sem):
        g = pl.program_id(0)

        @pl.when(g == 0)
        def _():
            cs = [pltpu.make_async_copy(k_hbm, k_v, sem.at[0]), pltpu.make_async_copy(v_hbm, v_v, sem.at[1])]
            for c_ in cs:
                c_.start()
            db_v[...] = jnp.zeros_like(db_v)
            ak[...] = jnp.zeros_like(ak)
            av[...] = jnp.zeros_like(av)
            akc[...] = jnp.zeros_like(akc)
            avc[...] = jnp.zeros_like(avc)
            for c_ in cs:
                c_.wait()

        for typ, at in ((0, 1), (1, nb - 1)):
            @pl.when(g == at)
            def _(typ=typ):
                cp = pltpu.make_async_copy(db_v, db_hbm.at[typ], sem.at[2])
                cp.start()
                cp.wait()
                db_v[...] = jnp.zeros_like(db_v)

        @pl.when(g < nb)
        def _():
            _na_load_bias(g, nb, b_hbm, b_v, sem.at[2])
            ws = _na_block(cfg, g)
            st = pl.multiple_of(ws * GRID_W, GRID_W)
            first = lax.broadcasted_iota(jnp.int32, (NQ, LANES), 1) < HEAD_DIM
            for hp in range(NA_HEADS // 2):
                ls = slice(hp * LANES, (hp + 1) * LANES)
                q2, do2 = q_ref[:, ls], do_ref[:, ls]
                kl, vl = k_v[pl.ds(st, NW), ls], v_v[pl.ds(st, NW), ls]
                kc, vc = k_v[S:T, ls], v_v[S:T, ls]
                dq2 = []
                dk2 = jnp.zeros((NW, LANES), f32)
                dv2 = jnp.zeros((NW, LANES), f32)
                dkc2 = jnp.zeros((L, LANES), f32)
                dvc2 = jnp.zeros((L, LANES), f32)
                for hh in range(2):
                    keep = first if hh == 0 else ~first
                    qm = jnp.where(keep, q2, jnp.zeros_like(q2))
                    dom = jnp.where(keep, do2, jnp.zeros_like(do2))
                    p_loc, p_ctx = _na_probs(qm, kl, kc, b_v[2 * hp + hh])
                    dp_loc = _nt(dom, vl)
                    dp_ctx = _nt(dom, vc)
                    delta = jnp.sum(p_loc * dp_loc, axis=-1, keepdims=True) + jnp.sum(p_ctx * dp_ctx, axis=-1, keepdims=True)
                    ds_loc = p_loc * (dp_loc - delta)
                    ds_ctx = p_ctx * (dp_ctx - delta)
                    db_v[2 * hp + hh, :, 0:NW] += ds_loc
                    dsl, dsc = ds_loc.astype(bf16), ds_ctx.astype(bf16)
                    dq2.append(_nn(dsl, kl) + _nn(dsc, kc))
                    dk2 = dk2 + _tn(dsl, qm)
                    dv2 = dv2 + _tn(p_loc.astype(bf16), dom)
                    dkc2 = dkc2 + _tn(dsc, qm)
                    dvc2 = dvc2 + _tn(p_ctx.astype(bf16), dom)
                dq_ref[:, ls] = jnp.where(first, dq2[0], dq2[1])
                akc[:, ls] += dkc2
                avc[:, ls] += dvc2
                for kk in range(NA_WR):
                    slot = (ws + kk) % NSLOT
                    ak[slot, :, ls] += dk2[kk * GRID_W:(kk + 1) * GRID_W, :]
                    av[slot, :, ls] += dv2[kk * GRID_W:(kk + 1) * GRID_W, :]

        @pl.when(((g >= 1) & (g <= bmax)) | (g >= nb))
        def _():
            base = NA_QR * (out_group(g) % (NSLOT // NA_QR))
            for t in range(NA_QR):
                dk_ref[t * GRID_W:(t + 1) * GRID_W, :] = ak[base + t]
                dv_ref[t * GRID_W:(t + 1) * GRID_W, :] = av[base + t]
                ak[base + t] = jnp.zeros((GRID_W, W), f32)
                av[base + t] = jnp.zeros((GRID_W, W), f32)

        @pl.when(g == nb - 1)
        def _():
            cp = pltpu.make_async_copy(db_v, db_hbm.at[2], sem.at[2])
            cp.start()
            cp.wait()

        @pl.when(g == steps - 1)
        def _():
            dkc_ref[...] = akc[...]
            dvc_ref[...] = avc[...]

    qmap = lambda g: (jnp.minimum(g, nb - 1), 0)
    kmap = lambda g: (out_group(g), 0)
    full = lambda g: (0, 0)
    return _call(
        body, (do, q, k, v, bexp), comm, name=name, grid=(steps,),
        in_specs=[pl.BlockSpec((NQ, W), qmap), pl.BlockSpec((NQ, W), qmap), ANY, ANY, ANY],
        out_specs=[pl.BlockSpec((NQ, W), qmap), pl.BlockSpec((NQ, W), kmap), pl.BlockSpec((NQ, W), kmap),
                   pl.BlockSpec((L, W), full), pl.BlockSpec((L, W), full), ANY],
        out_shape=[S_((S, W), f32), S_((S, W), f32), S_((S, W), f32), S_((L, W), f32), S_((L, W), f32),
                   S_((NA_TYPES, NA_HEADS, NQ, NA_WPAD), f32)],
        scratch_shapes=[pltpu.VMEM((T, W), bf16), pltpu.VMEM((T, W), bf16),
                        pltpu.VMEM((NA_HEADS, NQ, NW), f32), pltpu.VMEM((NA_HEADS, NQ, NA_WPAD), f32),
                        pltpu.VMEM((NSLOT, GRID_W, W), f32), pltpu.VMEM((NSLOT, GRID_W, W), f32),
                        pltpu.VMEM((L, W), f32), pltpu.VMEM((L, W), f32), pltpu.SemaphoreType.DMA((3,))],
        compiler_params=_cp(VMEM_BIG, ("arbitrary",)),
    )


def _rpb_reduce(dbias, flip, sel, name):
    nq, w = NA_QR * GRID_W, GRID_W

    def diag_body(x_ref, j_ref, o_ref):
        rows = []
        for i in range(NA_QR):
            xr = jnp.dot(j_ref[...], x_ref[i * w:(i + 1) * w, :], preferred_element_type=f32, precision=lax.Precision.HIGHEST)
            rows.append(jnp.sum(pltpu.roll(xr, 0, 1, stride=1, stride_axis=0), axis=0, keepdims=True))
        o_ref[...] = jnp.concatenate(rows + [jnp.zeros((8 - NA_QR, NA_WPAD), f32)], axis=0)

    diag = pl.pallas_call(
        diag_body, name=name + "_diag", grid=(NA_TYPES, NA_HEADS),
        in_specs=[pl.BlockSpec((None, None, nq, NA_WPAD), lambda t, h: (t, h, 0, 0)), pl.BlockSpec((w, w), lambda t, h: (0, 0))],
        out_specs=pl.BlockSpec((None, None, 8, NA_WPAD), lambda t, h: (t, h, 0, 0)),
        out_shape=S_((NA_TYPES, NA_HEADS, 8, NA_WPAD), f32),
        compiler_params=_cp(VMEM_MID, ("arbitrary", "arbitrary")),
    )(dbias, flip)
    lo = w - NA_KW
    y = diag[:, :, :NA_QR, lo:lo + NA_WR * w].reshape(NA_TYPES, NA_HEADS, NA_QR, NA_WR, w)
    y = jnp.transpose(y, (1, 0, 2, 3, 4)).reshape(NA_HEADS, NA_TYPES * NA_QR * NA_WR, w)
    y = jnp.pad(y, ((0, 0), (0, NA_SEL_ROWS - y.shape[1]), (0, LANES - w)))

    def body(y_ref, sel_ref, o_ref):
        o_ref[...] = jnp.dot(sel_ref[...], y_ref[...], preferred_element_type=f32, precision=lax.Precision.HIGHEST)

    return pl.pallas_call(
        body, name=name, grid=(NA_HEADS,),
        in_specs=[pl.BlockSpec((None, NA_SEL_ROWS, LANES), lambda h: (h, 0, 0)), pl.BlockSpec((16, NA_SEL_ROWS), lambda h: (0, 0))],
        out_specs=pl.BlockSpec((None, 16, LANES), lambda h: (h, 0, 0)),
        out_shape=S_((NA_HEADS, 16, LANES), f32),
        compiler_params=_cp(VMEM_MID, ("arbitrary",)),
    )(y, sel)


def _ctx_attn_fwd(cfg, q, k, v, name):
    L = cfg.L
    blk = cfg.S // L

    def body(q_ref, k_ref, v_ref, o_ref):
        qv, kv, vv = q_ref[...], k_ref[...], v_ref[...]
        outs = []
        for h in range(NA_HEADS):
            hs = slice(h * HEAD_DIM, (h + 1) * HEAD_DIM)
            s = _nt(qv[:, hs], kv[:, hs])
            e = jnp.exp(s - jnp.max(s, axis=-1, keepdims=True))
            p = e * (1.0 / jnp.sum(e, axis=-1, keepdims=True))
            outs.append(_nn(p.astype(bf16), vv[:, hs]))
        o_ref[...] = jnp.concatenate(outs, axis=-1).astype(bf16)

    spec = pl.BlockSpec((L, NA_WIDTH), lambda i: (blk, 0))
    return pl.pallas_call(
        body, name=name, grid=(1,), in_specs=[spec, spec, spec],
        out_specs=pl.BlockSpec((L, NA_WIDTH), lambda i: (0, 0)), out_shape=S_((L, NA_WIDTH), bf16),
        compiler_params=_cp(VMEM_MID, ("arbitrary",)),
    )(q, k, v)


def _ctx_attn_bwd(cfg, do, q, k, v, name):
    L = cfg.L
    blk = cfg.S // L

    def body(do_ref, q_ref, k_ref, v_ref, dq_ref, dk_ref, dv_ref):
        dov, qv, kv, vv = do_ref[...], q_ref[...], k_ref[...], v_ref[...]
        dqs, dks, dvs = [], [], []
        for h in range(NA_HEADS):
            hs = slice(h * HEAD_DIM, (h + 1) * HEAD_DIM)
            qh, kh, doh = qv[:, hs], kv[:, hs], dov[:, hs]
            s = _nt(qh, kh)
            e = jnp.exp(s - jnp.max(s, axis=-1, keepdims=True))
            p = e * (1.0 / jnp.sum(e, axis=-1, keepdims=True))
            dp = _nt(doh, vv[:, hs])
            ds = (p * (dp - jnp.sum(p * dp, axis=-1, keepdims=True))).astype(bf16)
            dqs.append(_nn(ds, kh))
            dks.append(_tn(ds, qh))
            dvs.append(_tn(p.astype(bf16), doh))
        dq_ref[...] = jnp.concatenate(dqs, axis=-1)
        dk_ref[...] = jnp.concatenate(dks, axis=-1)
        dv_ref[...] = jnp.concatenate(dvs, axis=-1)

    spec = pl.BlockSpec((L, NA_WIDTH), lambda i: (blk, 0))
    ospec = pl.BlockSpec((L, NA_WIDTH), lambda i: (0, 0))
    return pl.pallas_call(
        body, name=name, grid=(1,), in_specs=[spec, spec, spec, spec],
        out_specs=[ospec, ospec, ospec], out_shape=[S_((L, NA_WIDTH), f32)] * 3,
        compiler_params=_cp(VMEM_MID, ("arbitrary",)),
    )(do, q, k, v)


def _pool_centered(u, band, inv):
    return _split_sum(_nn, band, u) * inv - u


def _split_sum(mm, band, t):
    hi = t.astype(bf16)
    lo = (t - hi.astype(f32)).astype(bf16)
    s = mm(band, jnp.concatenate([hi, lo], axis=1))
    n = t.shape[1]
    return s[:, :n] + s[:, n:]


def _pool_mix(u_ref, band_ref, inv_ref, w_ref, ps_ref):
    C = POOL_CH
    outs = []
    for g in range(POOL_GROUPS):
        d = _pool_centered(u_ref[:, g * C:(g + 1) * C], band_ref[0, g], inv_ref[0, g])
        outs.append(_nn(d.astype(bf16), w_ref[g].astype(bf16)) * ps_ref[:, g * C:(g + 1) * C])
    return jnp.concatenate(outs, axis=-1).astype(bf16)


def _pool_bwd(cfg, dmix, u, band, inv, w_pool, pool_scale, with_ctx, name):
    TM = cfg.TM
    nt = cfg.ntiles(with_ctx)
    C = POOL_CH

    def body(dy_ref, u_ref, band_ref, inv_ref, w_ref, ps_ref, du_ref, dw_ref, dps_ref):
        @pl.when(pl.program_id(0) == 0)
        def _():
            dw_ref[...] = jnp.zeros_like(dw_ref)
            dps_ref[...] = jnp.zeros_like(dps_ref)

        dus, dpss = [], []
        for g in range(POOL_GROUPS):
            gs = slice(g * C, (g + 1) * C)
            band_g, inv_g = band_ref[0, g], inv_ref[0, g]
            db = _pool_centered(u_ref[:, gs], band_g, inv_g).astype(bf16)
            wb = w_ref[g].astype(bf16)
            dy = dy_ref[:, gs].astype(f32)
            dpss.append(_rsum(dy * _nn(db, wb)))
            dys = (dy * ps_ref[:, gs]).astype(bf16)
            dw_ref[g] += _tn(db, dys)
            dd = _nt(dys, wb)
            dus.append(_split_sum(_tn, band_g, dd * inv_g) - dd)
        du_ref[...] = jnp.concatenate(dus, axis=-1)
        dps_ref[...] += jnp.concatenate(dpss, axis=-1)

    typ4 = lambda i: (jnp.minimum(i // cfg.nxt, 1), 0, 0, 0)
    return pl.pallas_call(
        body, name=name, grid=(nt,),
        in_specs=[pl.BlockSpec((TM, POOL_WIDTH), lambda i: (i, 1)), pl.BlockSpec((TM, POOL_WIDTH), lambda i: (i, 0)),
                  pl.BlockSpec((1, POOL_GROUPS, TM, TM), typ4), pl.BlockSpec((1, POOL_GROUPS, TM, 1), typ4),
                  pl.BlockSpec((POOL_GROUPS, C, C), lambda i: (0, 0, 0)), pl.BlockSpec((1, POOL_WIDTH), lambda i: (0, 0))],
        out_specs=[pl.BlockSpec((TM, POOL_WIDTH), lambda i: (i, 0)), pl.BlockSpec((POOL_GROUPS, C, C), lambda i: (0, 0, 0)),
                   pl.BlockSpec((1, POOL_WIDTH), lambda i: (0, 0))],
        out_shape=[S_((nt * TM, POOL_WIDTH), f32), S_((POOL_GROUPS, C, C), f32), S_((1, POOL_WIDTH), f32)],
        compiler_params=_cp(VMEM_MID, ("arbitrary",)),
    )(dmix, u, band, inv, w_pool, pool_scale)


def _tmpost_fwd(cfg, na_x, na_c, u, band, inv, w_pool, pool_scale, w_out, xs, mods, gvec, name, comm=None):
    TM, D = cfg.TM, cfg.D
    with_ctx = na_c is not None
    nt = cfg.ntiles(with_ctx)
    R = nt * TM

    def body(*refs):
        if with_ctx:
            nax_ref, nac_ref = refs[:2]
            na = jnp.where(pl.program_id(0) < cfg.nxt, nax_ref[...], nac_ref[...])
        else:
            na = refs[0][...]
        (u_ref, band_ref, inv_ref, wp_ref, ps_ref, w_ref, xs_ref, mods_ref, g_ref,
         out_ref, opre_ref, mix_ref) = refs[2 if with_ctx else 1:]
        pool_v = _pool_mix(u_ref, band_ref, inv_ref, wp_ref, ps_ref)
        mix_ref[:, 0:NA_WIDTH] = na
        mix_ref[:, NA_WIDTH:] = pool_v
        o = _nn(na, w_ref[0:NA_WIDTH, :]) + _nn(pool_v, w_ref[NA_WIDTH:, :])
        opre_ref[...] = o
        ohat, _ = _rms_hat(o)
        out_ref[...] = xs_ref[...] + mods_ref[0][5:6] * (ohat * g_ref[3:4])

    rt = lambda c: pl.BlockSpec((TM, c), lambda i: (i, 0))
    na_specs = [pl.BlockSpec((TM, NA_WIDTH), lambda i: (jnp.minimum(i, cfg.nxt - 1), 0))]
    na_args = [na_x]
    if with_ctx:
        na_specs.append(pl.BlockSpec((TM, NA_WIDTH), lambda i: (0, 0)))
        na_args.append(na_c)
    typ4 = lambda i: (jnp.minimum(i // cfg.nxt, 1), 0, 0, 0)
    pool_specs = [rt(POOL_WIDTH), pl.BlockSpec((1, POOL_GROUPS, TM, TM), typ4), pl.BlockSpec((1, POOL_GROUPS, TM, 1), typ4),
                  pl.BlockSpec((POOL_GROUPS, POOL_CH, POOL_CH), lambda i: (0, 0, 0)), pl.BlockSpec((1, POOL_WIDTH), lambda i: (0, 0))]
    return _call(
        body, (*na_args, u, band, inv, w_pool, pool_scale, w_out, xs, mods, gvec), comm, name=name, grid=(nt,),
        in_specs=na_specs + pool_specs + [pl.BlockSpec((MIX_WIDTH, D), lambda i: (0, 0)), rt(D),
                                          pl.BlockSpec((1, N_MOD, D), _typ(cfg)), pl.BlockSpec((6, D), lambda i: (0, 0))],
        out_specs=[rt(D), rt(D), rt(MIX_WIDTH)],
        out_shape=[S_((R, D), f32), S_((R, D), f32), S_((R, MIX_WIDTH), bf16)],
        compiler_params=_cp(VMEM_MID, ("arbitrary",)),
    )


def _tmpost_bwd(cfg, dout, opre, w_out, mods, gvec, with_ctx, name):
    TM, D = cfg.TM, cfg.D
    nt = cfg.ntiles(with_ctx)
    R = nt * TM
    ntyp = 2 if with_ctx else 1

    def body(do_ref, opre_ref, w_ref, mods_ref, g_ref, dop_ref, dmix_ref, dm_ref, dg_ref):
        i = pl.program_id(0)

        @pl.when(i == 0)
        def _():
            dg_ref[...] = jnp.zeros_like(dg_ref)

        @pl.when((i == 0) | (i == cfg.nxt))
        def _():
            dm_ref[...] = jnp.zeros_like(dm_ref)

        do = do_ref[...]
        g3 = g_ref[3:4]
        ohat, rinv = _rms_hat(opre_ref[...])
        dm_ref[0] += _rsum(do * (ohat * g3))
        dr = mods_ref[0][5:6] * do
        dg_ref[...] += _rsum(dr * ohat)
        dob = _rms_bwd(dr * g3, ohat, rinv).astype(bf16)
        dop_ref[...] = dob
        dmix_ref[...] = _nt(dob, w_ref[...]).astype(bf16)

    rt = lambda c: pl.BlockSpec((TM, c), lambda i: (i, 0))
    return pl.pallas_call(
        body, name=name, grid=(nt,),
        in_specs=[rt(D), rt(D), pl.BlockSpec((MIX_WIDTH, D), lambda i: (0, 0)),
                  pl.BlockSpec((1, N_MOD, D), _typ(cfg)), pl.BlockSpec((6, D), lambda i: (0, 0))],
        out_specs=[rt(D), rt(MIX_WIDTH), pl.BlockSpec((1, 1, D), _typ(cfg)), pl.BlockSpec((1, D), lambda i: (0, 0))],
        out_shape=[S_((R, D), bf16), S_((R, MIX_WIDTH), bf16), S_((ntyp, 1, D), f32), S_((1, D), f32)],
        compiler_params=_cp(VMEM_MID, ("arbitrary",)),
    )(dout, opre, w_out, mods, gvec)


def _modvec_fwd(cvecs, w_mod, b_shard, name):
    nl, D, n = w_mod.shape
    tn = n // 3 if (n % 3 == 0 and (n // 3) % LANES == 0) else n

    def body(c_ref, w_ref, b_ref, o_ref, s_ref):
        cv = c_ref[...]
        sv = cv * _sigmoid(cv)
        s_ref[...] = sv
        o_ref[...] = _nn(sv.astype(bf16), w_ref[...].astype(bf16)) + b_ref[...]

    return pl.pallas_call(
        body, name=name, grid=(nl, n // tn),
        in_specs=[pl.BlockSpec((16, D), lambda l, j: (0, 0)), pl.BlockSpec((None, D, tn), lambda l, j: (l, 0, j)),
                  pl.BlockSpec((None, 1, tn), lambda l, j: (l, 0, j))],
        out_specs=[pl.BlockSpec((None, 16, tn), lambda l, j: (l, 0, j)), pl.BlockSpec((16, D), lambda l, j: (0, 0))],
        out_shape=[S_((nl, 16, n), f32), S_((16, D), f32)],
        compiler_params=_cp(VMEM_MID, ("arbitrary", "arbitrary")),
    )(cvecs, w_mod, b_shard)


def _modvec_bwd(s_t, dm, w_mod, name):
    nl, D, n = w_mod.shape
    tn = n // 3 if (n % 3 == 0 and (n // 3) % LANES == 0) else n

    def body(s_ref, dm_ref, w_ref, gw_ref, gc_ref):
        @pl.when(pl.program_id(1) == 0)
        def _():
            gc_ref[...] = jnp.zeros_like(gc_ref)
        dmv = dm_ref[...]
        gw_ref[...] = jnp.dot(s_ref[...], dmv, preferred_element_type=f32, precision=lax.Precision.HIGHEST)
        gc_ref[...] += _nt(dmv[8:16].astype(bf16), w_ref[...].astype(bf16))

    return pl.pallas_call(
        body, name=name, grid=(nl, n // tn),
        in_specs=[pl.BlockSpec((D, 16), lambda l, j: (0, 0)), pl.BlockSpec((None, 16, tn), lambda l, j: (l, 0, j)),
                  pl.BlockSpec((None, D, tn), lambda l, j: (l, 0, j))],
        out_specs=[pl.BlockSpec((None, D, tn), lambda l, j: (l, 0, j)), pl.BlockSpec((None, 8, D), lambda l, j: (l, 0, 0))],
        out_shape=[S_((nl, D, n), f32), S_((nl, 8, D), f32)],
        compiler_params=_cp(VMEM_MID, ("arbitrary", "arbitrary")),
    )(s_t, dm, w_mod)


def _as2d(a):
    n = a.size
    if a.ndim >= 2 and a.shape[-1] % LANES == 0:
        return a.reshape(-1, a.shape[-1])
    if n % LANES == 0:
        return a.reshape(-1, LANES)
    return a.reshape(-1, a.shape[-1]) if a.ndim >= 2 else a.reshape(1, n)


def _row_tile(r, c, budget_elems):
    if r * c <= budget_elems or r % 8 != 0:
        return r
    t = r
    while t * c > budget_elems and t % 16 == 0:
        t //= 2
    return t


def _div_tile(r, c, budget_elems, mult=16):
    best = None
    for t in range(mult, r + 1, mult):
        if r % t == 0 and t * c <= budget_elems:
            best = t
    return best if best is not None else r


def _chip_index():
    return 2 * lax.axis_index("x") + lax.axis_index("y")


def _cast_into_place(shards, lead, axis, name):
    r, c = shards.shape[-2:]
    tr = _div_tile(r, c, 3 * ELEMWISE_BLOCK)
    nr = r // tr
    out_map = (lambda i: (i, _chip_index())) if axis == 1 else (lambda i: (_chip_index() * nr + i, 0))
    full2 = (r, c * N_CHIPS) if axis == 1 else (r * N_CHIPS, c)

    def body(a_ref, o_ref):
        o_ref[...] = a_ref[...].astype(bf16)

    return pl.pallas_call(
        body, name=name, grid=(nr,),
        in_specs=[pl.BlockSpec((None,) * len(lead) + (tr, c), lambda i: tuple(lead) + (i, 0))],
        out_specs=pl.BlockSpec((tr, c), out_map),
        out_shape=S_(full2, bf16), compiler_params=_cp(VMEM_MID, ("arbitrary",)),
    )(shards)


def _sum_devices8(own, land, axis, into, lead, name):
    _, rh, cs = land.shape
    tr = _div_tile(rh, cs, 2 * ELEMWISE_BLOCK)
    nr = rh // tr
    core = lambda: lax.axis_index("c")
    if axis == 1:
        own_map = lambda i: (core() * nr + i, _chip_index())
    else:
        own_map = lambda i: (_chip_index() * 2 * nr + core() * nr + i, 0)
    nl = len(lead)

    def land_spec(j):
        return pl.BlockSpec((None, tr, cs), lambda i: ((2 * _chip_index() + core() + j) % N_DEV, i, 0))

    def body(own_ref, *rest):
        acc = own_ref[...]
        for p_ref in rest[:N_DEV - 1]:
            acc = acc + p_ref[...].astype(f32)
        rest[-1][...] = acc

    return pl.pallas_call(
        body, name=name, grid=(nr,),
        in_specs=[pl.BlockSpec((tr, cs), own_map)] + [land_spec(j) for j in range(1, N_DEV)] + [ANY],
        out_specs=pl.BlockSpec((None,) * nl + (tr, cs), lambda i: tuple(lead) + (core() * nr + i, 0)),
        out_shape=S_(into.shape, f32), input_output_aliases={N_DEV: 0},
        compiler_params=_cp(VMEM_MID, ("arbitrary",)),
    )(own, *([land] * (N_DEV - 1)), into)


def _adamw(w, g, m, v, name, emit_grad=False, comm=None):
    shape = w.shape
    w2, g2, m2, v2 = _as2d(w), _as2d(g), _as2d(m), _as2d(v)
    r, c = w2.shape
    tr = _row_tile(r, c, ELEMWISE_BLOCK)
    c1 = 1.0 - ADAM_B1 ** ADAM_STEP
    c2 = 1.0 - ADAM_B2 ** ADAM_STEP
    n_out = 4 if emit_grad else 3

    def body(w_ref, g_ref, m_ref, v_ref, d_ref, mo_ref, vo_ref, *go_ref):
        gv = g_ref[...]
        mn = ADAM_B1 * m_ref[...] + (1.0 - ADAM_B1) * gv
        vn = ADAM_B2 * v_ref[...] + (1.0 - ADAM_B2) * (gv * gv)
        mo_ref[...] = mn
        vo_ref[...] = vn
        d_ref[...] = -ADAM_LR * ((mn / c1) / (jnp.sqrt(vn / c2) + ADAM_EPS) + ADAM_WD * w_ref[...])
        if emit_grad:
            go_ref[0][...] = gv

    spec = pl.BlockSpec((tr, c), lambda i: (i, 0))
    outs, res = _call(body, (w2, g2, m2, v2), comm, name=name, grid=(r // tr,), in_specs=[spec] * 4, out_specs=[spec] * n_out,
                      out_shape=[S_((r, c), f32)] * n_out, compiler_params=_cp(VMEM_MID, ("arbitrary",)))
    outs = tuple(o.reshape(shape) for o in outs)
    return outs if comm is None else (outs, res)


def _sum_devices(gathered, name):
    _, r, c = gathered.shape

    def body(a_ref, o_ref):
        acc = a_ref[0]
        for j in range(1, N_DEV):
            acc = acc + a_ref[j]
        o_ref[...] = acc

    tr = _row_tile(r, c, ELEMWISE_BLOCK // 4)
    return pl.pallas_call(
        body, name=name, grid=(r // tr,),
        in_specs=[pl.BlockSpec((N_DEV, tr, c), lambda i: (0, i, 0))], out_specs=pl.BlockSpec((tr, c), lambda i: (i, 0)),
        out_shape=S_((r, c), f32), compiler_params=_cp(VMEM_MID, ("arbitrary",)))(gathered)


def _all_gather_small(block, name):
    m_per, n = block.shape

    def body(x_ref, out_ref, send_sems, recv_sems, local_sem):
        x, y, c = _mesh_pos()
        me, sibling = (x, y, c), (x, y, 1 - c)
        chips = [(1 - x, y), (x, 1 - y), (1 - x, 1 - y)]

        def rows(px, py, pc):
            return out_ref.at[pl.ds((4 * px + 2 * py + pc) * m_per, m_per), :]

        def copy(k, blk, to, src=None):
            return pltpu.make_async_remote_copy(
                src_ref=rows(*blk) if src is None else src, dst_ref=rows(*blk),
                send_sem=send_sems.at[k], recv_sem=recv_sems.at[k], device_id=to, device_id_type=MESH)

        mine = pltpu.make_async_copy(x_ref, rows(*me), local_sem)
        mine.start()
        first = [copy(0, me, sibling, src=x_ref)]
        first += [copy(1 + j, me, (*chip, c), src=x_ref) for j, chip in enumerate(chips)]
        for cp in first:
            cp.start()
        passed = [copy(4 + j, (*chip, c), sibling) for j, chip in enumerate(chips)]
        for j, chip in enumerate(chips):
            copy(1 + j, (*chip, c), me).wait_recv()
            passed[j].start()
        copy(0, sibling, me).wait_recv()
        for j, chip in enumerate(chips):
            copy(4 + j, (*chip, 1 - c), me).wait_recv()
        for cp in first + passed:
            cp.wait_send()
        mine.wait()

    return pl.pallas_call(
        body, name=name, out_shape=S_((N_DEV * m_per, n), block.dtype),
        in_specs=[pl.BlockSpec(memory_space=pltpu.VMEM)], out_specs=pl.BlockSpec(memory_space=pltpu.VMEM),
        scratch_shapes=[pltpu.SemaphoreType.DMA((7,)), pltpu.SemaphoreType.DMA((7,)), pltpu.SemaphoreType.DMA],
        compiler_params=_cp(VMEM_MID),
    )(block)


def _pack_rows(arrays):
    flat = jnp.concatenate([a.reshape(-1) for a in arrays])
    pad = (-flat.size) % (8 * LANES)
    return jnp.pad(flat, (0, pad)).reshape(-1, LANES)


def _unpack_rows(packed, shapes):
    flat = packed.reshape(-1)
    out, off = [], 0
    for s in shapes:
        n = int(np.prod(s))
        out.append(flat[off:off + n].reshape(s))
        off += n
    return out


W_AXIS = {"gu": 1, "dn": 0, "wi": 1, "wo": 0}
SMALL_NAMES = ("dmods", "dg", "drpb", "dwp", "dps")


def _half_merge(bufs, name):
    nt = len(bufs)

    def body(*refs):
        outs = refs[nt:2 * nt]
        send_sems, recv_sems = refs[2 * nt:]
        x, y, c = _mesh_pos()

        def half(ref, h):
            rh = ref.shape[-2] // 2
            return ref.at[(slice(None),) * (len(ref.shape) - 2) + (pl.ds(h * rh, rh), slice(None))]

        cps = []
        for t in range(nt):
            cp = pltpu.make_async_remote_copy(
                src_ref=half(outs[t], c), dst_ref=half(outs[t], c), send_sem=send_sems.at[t], recv_sem=recv_sems.at[t],
                device_id=(x, y, 1 - c), device_id_type=MESH)
            cp.start()
            cps.append(cp)
        for t in range(nt):
            pltpu.make_async_remote_copy(
                src_ref=half(outs[t], 1 - c), dst_ref=half(outs[t], 1 - c), send_sem=send_sems.at[t], recv_sem=recv_sems.at[t],
                device_id=(x, y, 1 - c), device_id_type=MESH).wait_recv()
        for cp in cps:
            cp.wait_send()

    return pl.pallas_call(
        body, name=name, in_specs=[ANY] * nt, out_specs=[ANY] * nt, out_shape=[S_(b.shape, f32) for b in bufs],
        input_output_aliases={t: t for t in range(nt)},
        scratch_shapes=[pltpu.SemaphoreType.DMA((nt,)), pltpu.SemaphoreType.DMA((nt,))],
        compiler_params=_cp(VMEM_MID),
    )(*bufs)


def _local_step(cfg, x_lat, x_ctx, target, mods, norm_g, W, G, na_rpb, w_pool, pool_scale):
    S, L, T, D, F = cfg.S, cfg.L, cfg.T, cfg.D, cfg.F
    depth = norm_g.shape[0]
    cos, sin = _rope_tables(S, L)
    band, inv = _pool_tables(cfg.TM, L)
    flip, sel = _rpb_reduce_tables()

    assert depth == 2, "the carrier schedules below are written for two layers"
    fwd_carry = {"ffn_fwd_0_0": [("wi", 0), ("wo", 0), ("gu", 0, 1)], "tmpre_fwd_0": [("dn", 0, 1)],
                 "na_fwd_0": [("gu", 1, 0), ("dn", 1, 0)], "tmpost_fwd_0": [("wi", 1), ("wo", 1)],
                 "ffn_fwd_0_1": [("gu", 1, 1), ("dn", 1, 1)]}
    bwd_carry = {"na_bwd_1": [("gu", 1, 1), ("dn", 1, 1)], "ffn_bwd_1_0": [("wi", 1), ("wo", 1)],
                 "ffn_bwd_0_1": [("gu", 1, 0), ("dn", 1, 0)], "na_bwd_0": [("gu", 0, 1), ("dn", 0, 1)],
                 "ffn_bwd_0_0": [("wi", 0), ("wo", 0)], "wgrad_dn_0_0": [("gu", 0, 0)]}
    last_scatter = [("dn", 0, 0)]
    tag = lambda key: "_".join(str(p) for p in key)
    g_f32, g_b16 = {}, {}

    def gather_on(name):
        keys = fwd_carry.get(name)
        return None if keys is None else _gather_comm([W[k_] for k_ in keys], [W_AXIS[k_[0]] for k_ in keys])

    def gathered(name, res):
        if name in fwd_carry:
            W.update(zip(fwd_carry[name], res))

    def scatter_on(name):
        keys = bwd_carry.get(name)
        return None if keys is None else _scatter_comm([g_b16[k_] for k_ in keys], [W_AXIS[k_[0]] for k_ in keys])

    def scattered(keys, lands):
        for key, land in zip(keys, lands):
            G[key[0]] = _sum_devices8(g_f32[key], land, W_AXIS[key[0]], G[key[0]], key[1:], f"sum8_{tag(key)}")

    small_landed = []

    def wgrad(key, a, b, rows, other_comm=None):
        name = f"wgrad_{tag(key)}"
        if other_comm is not None:
            assert name not in bwd_carry
            (g_f32[key], g_b16[key]), res = _wgrad(a, b, rows, name, other_comm)
            small_landed.extend(res)
            return
        (g_f32[key], g_b16[key]), lands = _wgrad(a, b, rows, name, scatter_on(name))
        scattered(bwd_carry.get(name, ()), lands)

    saved = []
    xs, xs_ctx = x_lat, x_ctx
    for l in range(depth):
        last = l == depth - 1
        wc = not last
        gvec = norm_g[l]
        ps = pool_scale[l].reshape(1, POOL_WIDTH)
        bexp = _expand_rpb(na_rpb[l], f"bias_expand_{l}")
        name = f"ffn_fwd_{l}_0"
        (xs1, hb1, z1, y1), res = _ffn_fwd(cfg, xs, mods[l], gvec, W["gu", l, 0], W["dn", l, 0], 0, 0, True, name,
                                           gather_on(name), xs_ctx=xs_ctx)
        gathered(name, res)
        name = f"tmpre_fwd_{l}"
        (hb2, q, k, v, u), res = _tmpre_fwd(cfg, xs1, mods[l], gvec, W["wi", l], cos, sin, name, gather_on(name))
        gathered(name, res)
        name = f"na_fwd_{l}"
        (na_x,), res = _na_fwd(cfg, q, k, v, bexp, name, gather_on(name))
        gathered(name, res)
        na_c = _ctx_attn_fwd(cfg, q, k, v, f"ctx_attn_fwd_{l}") if wc else None
        name = f"tmpost_fwd_{l}"
        (xs2, opre, mix), res = _tmpost_fwd(cfg, na_x, na_c, u, band, inv, w_pool[l], ps, W["wo", l], xs1, mods[l], gvec,
                                            name, gather_on(name))
        gathered(name, res)
        name = f"ffn_fwd_{l}_1"
        outs, res = _ffn_fwd(cfg, xs2, mods[l], gvec, W["gu", l, 1], W["dn", l, 1], 6, 4, wc, name, gather_on(name),
                             loss_target=target if last else None)
        xs3, hb3, z3, y3 = outs[:4]
        gathered(name, res)
        saved.append(dict(xs=xs, xs_ctx=xs_ctx, xs1=xs1, xs2=xs2, hb1=hb1, z1=z1, y1=y1, hb2=hb2, q=q, k=k, v=v, u=u, mix=mix,
                          opre=opre, hb3=hb3, z3=z3, y3=y3, bexp=bexp, ps=ps, gvec=gvec))
        xs, xs_ctx = xs3, None

    dxs, loss_blk = xs, outs[4]

    small = [None] * depth
    for l in reversed(range(depth)):
        last = l == depth - 1
        wc = not last
        sv = saved[l]
        gvec = sv["gvec"]
        rows_b = cfg.T if wc else cfg.S
        name = f"ffn_bwd_{l}_1"
        (dxs2, dz, dyb, ab, dm678, dg45), lands = _ffn_bwd(cfg, dxs, sv["xs2"], sv["z3"], sv["y3"], mods[l], gvec,
                                                           W["gu", l, 1], W["dn", l, 1], 6, 4, wc, name, scatter_on(name))
        scattered(bwd_carry.get(name, ()), lands)
        wgrad(("gu", l, 1), sv["hb3"], dz, rows_b)
        wgrad(("dn", l, 1), ab, dyb, rows_b)
        dop, dmix, dm5, dg3 = _tmpost_bwd(cfg, dxs2, sv["opre"], W["wo", l], mods[l], gvec, wc, f"tmpost_bwd_{l}")
        wgrad(("wo", l), sv["mix"], dop, rows_b)
        du, dwp, dps = _pool_bwd(cfg, dmix, sv["u"], band, inv, w_pool[l], sv["ps"], wc, f"pool_bwd_{l}")
        name = f"na_bwd_{l}"
        (dq, dk, dv, dkc, dvc, dbexp), lands = _na_bwd(cfg, dmix, sv["q"], sv["k"], sv["v"], sv["bexp"], name, scatter_on(name))
        scattered(bwd_carry.get(name, ()), lands)
        drpb = _rpb_reduce(dbexp, flip, sel, f"rpb_reduce_{l}")
        if wc:
            dqc, dkc2, dvc2 = _ctx_attn_bwd(cfg, dmix, sv["q"], sv["k"], sv["v"], f"ctx_attn_bwd_{l}")
            ctx_terms = ([dqc], [dkc, dkc2], [dvc, dvc2])
        else:
            ctx_terms = ([], [dkc], [dvc])
        dxs1, dproj, dm34, dg2 = _tmpre_bwd(cfg, (dq, dk, dv, du), ctx_terms, wc, cos, sin, W["wi", l], sv["xs1"], mods[l], gvec,
                                            dxs2, wc, f"tmpre_bwd_{l}")
        wgrad(("wi", l), sv["hb2"], dproj, cfg.T)
        name = f"ffn_bwd_{l}_0"
        (dxs, dz, dyb, ab, dm012, dg01), lands = _ffn_bwd(cfg, dxs1, sv["xs"], sv["z1"], sv["y1"], mods[l], gvec,
                                                          W["gu", l, 0], W["dn", l, 0], 0, 0, True, name, scatter_on(name),
                                                          xs_ctx=sv["xs_ctx"])
        scattered(bwd_carry.get(name, ()), lands)
        if not wc:
            zero = lambda a: jnp.concatenate([a, jnp.zeros_like(a)], axis=0)
            dm5, dm678 = zero(dm5), zero(dm678)
        dmods = jnp.concatenate([dm012, dm34, dm5, dm678], axis=1)
        dgs = jnp.concatenate([dg01, dg2, dg3, dg45], axis=0)
        small[l] = dict(dmods=dmods, dg=dgs, drpb=drpb, dwp=dwp, dps=dps)
        small_gather = None
        if l == 0:
            parts = [jnp.stack([small[j][n_] for j in range(depth)]) for n_ in SMALL_NAMES]
            packed = _pack_rows(parts)
            small_gather = _allgather_comm(packed)
        wgrad(("gu", l, 0), sv["hb1"], dz, cfg.T, small_gather)
        wgrad(("dn", l, 0), ab, dyb, cfg.T)
    last_comm = _scatter_comm([g_b16[k_] for k_ in last_scatter], [W_AXIS[k_[0]] for k_ in last_scatter])

    def finish_weight_grads(lands):
        scattered(last_scatter, lands)
        kinds = ("gu", "dn", "wi", "wo")
        return dict(zip(kinds, _half_merge([G[k_] for k_ in kinds], "merge_halves")))

    return loss_blk, dxs, (last_comm, finish_weight_grads), (packed, [p.shape for p in parts], small_landed[0])


def kernel(x, c, ctx, c_ctx, w_mod, b_mod, norm_g, w_ffn_gate_up, w_ffn_down, w_in, w_out, na_rpb, w_pool, pool_scale, loss_target, m_c_ctx, m_w_mod, m_b_mod, m_norm_g, m_w_ffn_gate_up, m_w_ffn_down, m_w_in, m_w_out, m_na_rpb, m_w_pool, m_pool_scale, v_c_ctx, v_w_mod, v_b_mod, v_norm_g, v_w_ffn_gate_up, v_w_ffn_down, v_w_in, v_w_out, v_na_rpb, v_w_pool, v_pool_scale):
    S, D = x.shape[1], x.shape[2]
    L = ctx.shape[1]
    depth = w_mod.shape[0]
    F = w_ffn_down.shape[2] * N_CHIPS
    nmod = w_mod.shape[2]
    gsh = norm_g.shape[2]
    cfg = _Cfg(S, L, D, F)
    mx, my, mc = _mesh_pos()
    chip = 2 * mx + my
    dev = 4 * mx + 2 * my + mc

    W = {}
    for l in range(depth):
        for i in range(2):
            W["gu", l, i] = _cast_into_place(w_ffn_gate_up, (l, i), W_AXIS["gu"], f"cast_gu_{l}_{i}")
            W["dn", l, i] = _cast_into_place(w_ffn_down, (l, i), W_AXIS["dn"], f"cast_dn_{l}_{i}")
        W["wi", l] = _cast_into_place(w_in, (l,), W_AXIS["wi"], f"cast_wi_{l}")
        W["wo", l] = _cast_into_place(w_out, (l,), W_AXIS["wo"], f"cast_wo_{l}")
    first = [("gu", 0, 0), ("dn", 0, 0)]
    W.update(zip(first, _comm_only(_gather_comm([W[k_] for k_ in first], [W_AXIS[k_[0]] for k_ in first]), "gather_first")))
    G = {"gu": lax.empty(w_ffn_gate_up.shape, f32), "dn": lax.empty(w_ffn_down.shape, f32),
         "wi": lax.empty(w_in.shape, f32), "wo": lax.empty(w_out.shape, f32)}

    cg_packed = _pack_rows([c, norm_g])
    cg_all = _all_gather_small(cg_packed, "gather_c_norm_g").reshape(N_DEV, -1)
    c_all = cg_all[:, :D]
    ng = cg_all[:, D:D + norm_g.size].reshape(N_DEV, depth, 6, gsh)
    norm_g_all = jnp.concatenate([ng[2 * j] for j in range(N_CHIPS)], axis=-1)
    cvecs = jnp.concatenate([c_all, c_ctx[None], jnp.zeros((7, D), f32)], axis=0)
    b_shard = lax.dynamic_slice_in_dim(b_mod, chip * nmod, nmod, axis=1).reshape(depth, 1, nmod)
    m_part, silu_c = _modvec_fwd(cvecs, w_mod, b_shard, "modvec_fwd")
    m_all = _all_gather_small(m_part.reshape(depth * 16, nmod), "gather_mod").reshape(N_DEV, depth, 16, nmod)
    m_full = jnp.concatenate([m_all[2 * j] for j in range(N_CHIPS)], axis=-1)
    m_mine = lax.dynamic_index_in_dim(m_full, dev, axis=1, keepdims=False)
    mods = jnp.stack([m_mine, m_full[:, 8]], axis=1).reshape(depth, 2, N_MOD, D)

    loss_blk, dx_lat, (last_comm, finish_weight_grads), small = _local_step(
        cfg, x[0], ctx[0], loss_target[0], mods, norm_g_all, W, G, na_rpb, w_pool, pool_scale)
    loss = lax.psum(loss_blk[0, 0], ("x", "y", "c"))
    grad_x = dx_lat[None]

    packed, shapes, landed = small
    gathered = lax.dynamic_update_index_in_dim(landed, packed, dev, 0)
    total = _unpack_rows(_sum_devices(gathered, "sum_small"), shapes)
    dmods_sum, dg_sum, drpb_sum, dwp_sum, dps_sum = total
    dmods_each = jnp.stack([_unpack_rows(gathered[j], shapes[:1])[0] for j in range(N_DEV)])
    dm_rows = jnp.concatenate([jnp.transpose(dmods_each[:, :, 0], (1, 0, 2, 3)).reshape(depth, N_DEV, N_MOD * D),
                               dmods_sum[:, 1].reshape(depth, 1, N_MOD * D),
                               jnp.zeros((depth, 7, N_MOD * D), f32)], axis=1)
    dm_shard = lax.dynamic_slice_in_dim(dm_rows, chip * nmod, nmod, axis=2)
    grad_w_mod, gc_part = _modvec_bwd(silu_c.T, dm_shard, w_mod, "modvec_bwd")
    gc_all = _all_gather_small(gc_part.reshape(depth * 8, D), "gather_gc").reshape(N_DEV, depth, 8, D)
    grad_b_mod, grad_c_ctx = _small_finish(dm_rows, gc_all, c_ctx)
    grad_norm_g = lax.dynamic_slice_in_dim(dg_sum, chip * gsh, gsh, axis=2)
    grad_na_rpb = drpb_sum[:, :, :2 * NA_KH - 1, :2 * NA_KW - 1]
    grad_w_pool = dwp_sum
    grad_pool_scale = dps_sum.reshape(depth, POOL_WIDTH)

    upd_w_mod, lands = _adamw(w_mod, grad_w_mod, m_w_mod, v_w_mod, "adamw_w_mod", comm=last_comm)
    wgrads = finish_weight_grads(lands)
    g_gu, g_dn, g_wi, g_wo = wgrads["gu"], wgrads["dn"], wgrads["wi"], wgrads["wo"]
    grads = [grad_c_ctx, grad_w_mod, grad_b_mod, grad_norm_g, g_gu, g_dn, g_wi, g_wo, grad_na_rpb, grad_w_pool, grad_pool_scale]
    ws = [c_ctx, w_mod, b_mod, norm_g, w_ffn_gate_up, w_ffn_down, w_in, w_out, na_rpb, w_pool, pool_scale]
    ms = [m_c_ctx, m_w_mod, m_b_mod, m_norm_g, m_w_ffn_gate_up, m_w_ffn_down, m_w_in, m_w_out, m_na_rpb, m_w_pool, m_pool_scale]
    vs = [v_c_ctx, v_w_mod, v_b_mod, v_norm_g, v_w_ffn_gate_up, v_w_ffn_down, v_w_in, v_w_out, v_na_rpb, v_w_pool, v_pool_scale]
    tags = ["c_ctx", "w_mod", "b_mod", "norm_g", "gate_up", "down", "w_in", "w_out", "na_rpb", "w_pool", "pool_scale"]
    merged = ("gate_up", "down", "w_in", "w_out")
    upd = [upd_w_mod if t == "w_mod" else _adamw(w_, g_, m_, v_, f"adamw_{t}", emit_grad=t in merged)
           for w_, g_, m_, v_, t in zip(ws, grads, ms, vs, tags)]
    grads = [u_[3] if t in merged else g_ for g_, u_, t in zip(grads, upd, tags)]
    return (loss, grad_x, *grads, *[u_[0] for u_ in upd], *[u_[1] for u_ in upd], *[u_[2] for u_ in upd])


def _small_finish(dm_rows, gc_all, c_ctx):
    depth, _, n = dm_rows.shape
    D = c_ctx.shape[0]

    def body(dm_ref, gc_ref, c_ref, gb_ref, gcx_ref):
        acc = dm_ref[:, 0]
        for j in range(1, N_DEV + 1):
            acc = acc + dm_ref[:, j]
        gb_ref[...] = acc
        t = jnp.zeros((1, D), f32)
        for l in range(depth):
            for j in range(N_CHIPS):
                t = t + gc_ref[2 * j, l, 0:1, :]
        cv = c_ref[...]
        sg = _sigmoid(cv)
        gcx_ref[...] = t * (sg * (1.0 + cv * (1.0 - sg)))

    gb, gcx = pl.pallas_call(
        body, name="small_finish",
        out_shape=[S_((depth, n), f32), S_((1, D), f32)],
        compiler_params=_cp(VMEM_MID),
    )(dm_rows, gc_all, c_ctx.reshape(1, D))
    return gb, gcx.reshape(D)
```

```python
import functools

import numpy as np
import jax
import jax.numpy as jnp
from jax import lax
from jax.experimental import pallas as pl
from jax.experimental.pallas import tpu as pltpu

f32, bf16 = jnp.float32, jnp.bfloat16

GRID_W = 64
N_MOD = 9
NA_HEADS = 8
HEAD_DIM = 64
NA_WIDTH = NA_HEADS * HEAD_DIM
NA_KH = 8
NA_KW = 16
POOL_GROUPS = 4
POOL_CH = 128
POOL_WIDTH = POOL_GROUPS * POOL_CH
POOL_WINDOWS = (2, 4, 8, 16)
IN_WIDTH = 3 * NA_WIDTH + POOL_WIDTH
MIX_WIDTH = NA_WIDTH + POOL_WIDTH
ROPE_THETA = 10000.0
ROPE_PAIRS = HEAD_DIM // 4
RMS_EPS = 1e-6
NEG_INF = -1e30
ADAM_LR, ADAM_B1, ADAM_B2, ADAM_EPS, ADAM_WD, ADAM_STEP = 0.001, 0.9, 0.999, 1e-08, 0.01, 10

N_DEV = 8
N_CHIPS = 4
LANES = 128
MIB = 1024 * 1024
VMEM_BIG = 52 * MIB
VMEM_MID = 40 * MIB
WGRAD_TN = 1408
WGRAD_TK = 2816
WGRAD_SLACK = 6 * MIB
ELEMWISE_BLOCK = 256 * 1024
MESH = pl.DeviceIdType.MESH
ANY = pl.BlockSpec(memory_space=pl.ANY)
S_ = jax.ShapeDtypeStruct


def _cp(vmem=VMEM_MID, sem=None):
    return pltpu.CompilerParams(vmem_limit_bytes=vmem, dimension_semantics=sem)


def _sigmoid(x):
    return 0.5 * jnp.tanh(0.5 * x) + 0.5


def _rms_hat(x):
    rinv = lax.rsqrt(jnp.mean(x * x, axis=-1, keepdims=True) + RMS_EPS)
    return x * rinv, rinv


def _rms_bwd(dxhat, xhat, rinv):
    return rinv * (dxhat - xhat * jnp.mean(dxhat * xhat, axis=-1, keepdims=True))


def _rsum(a):
    return jnp.sum(a, axis=0, keepdims=True)


def _nt(a, b):
    return lax.dot_general(a, b, (((1,), (1,)), ((), ())), preferred_element_type=f32)


def _tn(a, b):
    return lax.dot_general(a, b, (((0,), (0,)), ((), ())), preferred_element_type=f32)


def _nn(a, b):
    return jnp.dot(a, b, preferred_element_type=f32)


def _swap16(x):
    lane = lax.broadcasted_iota(jnp.int32, x.shape, 1)
    n = x.shape[1]
    return jnp.where((lane % 32) < 16, pltpu.roll(x, n - 16, 1), pltpu.roll(x, 16, 1))


def _rope_tables(s_len, l_len):
    t = np.arange(s_len)
    inv = ROPE_THETA ** (-np.arange(ROPE_PAIRS, dtype=np.float32) / ROPE_PAIRS)
    ang_r = (t // GRID_W).astype(np.float32)[:, None] * inv
    ang_c = (t % GRID_W).astype(np.float32)[:, None] * inv
    cos = np.concatenate([np.cos(ang_r), np.cos(ang_r), np.cos(ang_c), np.cos(ang_c)], axis=-1)
    sin = np.concatenate([-np.sin(ang_r), np.sin(ang_r), -np.sin(ang_c), np.sin(ang_c)], axis=-1)
    cos = np.concatenate([cos, np.ones((l_len, HEAD_DIM), np.float32)], axis=0)
    sin = np.concatenate([sin, np.zeros((l_len, HEAD_DIM), np.float32)], axis=0)
    return (jnp.asarray(np.tile(cos, (1, 2)), f32), jnp.asarray(np.tile(sin, (1, 2)), f32))


def _pool_tables(tm, l_len):
    band = np.zeros((2, POOL_GROUPS, tm, tm), np.float32)
    inv = np.zeros((2, POOL_GROUPS, tm, 1), np.float32)
    for typ, length in ((0, GRID_W), (1, l_len)):
        for g, w in enumerate(POOL_WINDOWS):
            for t in range(tm):
                base, p = (t // length) * length, t % length
                lo = min(max(p - w // 2, 0), length)
                hi = min(max(p - w // 2 + w, 0), length)
                band[typ, g, t, base + lo:base + hi] = 1.0
                inv[typ, g, t, 0] = 1.0 / (hi - lo)
    return jnp.asarray(band, bf16), jnp.asarray(inv, f32)


NA_QR = 4
NA_WR = NA_KH + NA_QR - 1
NA_TYPES = 3
NA_SEL_ROWS = 136
NA_WPAD = 768


def _rpb_index_tables():
    j = np.arange(GRID_W)
    col_start = np.clip(j - NA_KW // 2, 0, GRID_W - NA_KW)
    valid = (j[None, :] >= col_start[:, None]) & (j[None, :] < col_start[:, None] + NA_KW)
    dc = np.clip(j[None, :] - j[:, None] + NA_KW - 1, 0, 2 * NA_KW - 2)
    i = np.arange(NA_QR)[:, None]
    kk = np.arange(NA_WR)[None, :]
    off = np.stack([np.zeros_like(i), i, np.full_like(i, NA_QR - 1)])
    d = np.stack([kk - i + NA_KH - 1, kk - i + NA_KH - 1 - NA_QR, kk - i])
    row_ok = (kk[None] >= off) & (kk[None] < off + NA_KH)
    assert (d[row_ok] >= 0).all() and (d[row_ok] <= 2 * NA_KH - 2).all()
    return valid, dc, d, row_ok


def _expand_rpb(rpb, name):
    _, _, d, row_ok = _rpb_index_tables()
    heads, nd, ne = rpb.shape
    w = GRID_W
    v = jnp.pad(rpb, ((0, 0), (0, 0), (w - NA_KW, 2 * w - (w - NA_KW) - ne)))
    x = jnp.broadcast_to(v[:, :, None, :], (heads, nd, w, 2 * w)).reshape(heads, nd, 2 * w * w)
    t = x[:, :, :w * (2 * w - 1)].reshape(heads, nd, w, 2 * w - 1)[..., w - 1:]

    def body(t_ref, o_ref):
        q = lax.broadcasted_iota(jnp.int32, (w, w), 0)
        c = lax.broadcasted_iota(jnp.int32, (w, w), 1)
        c0 = jnp.clip(q - NA_KW // 2, 0, w - NA_KW)
        in_cols = (c >= c0) & (c < c0 + NA_KW)
        outside = jnp.full((w, w), NEG_INF, f32)
        blocks = [jnp.where(in_cols, t_ref[dd], NEG_INF) for dd in range(nd)]
        for typ in range(NA_TYPES):
            for i in range(NA_QR):
                row = [blocks[d[typ, i, kk]] if row_ok[typ, i, kk] else outside for kk in range(NA_WR)]
                o_ref[typ, i * w:(i + 1) * w, :] = jnp.concatenate(row, axis=1)

    return pl.pallas_call(
        body, name=name, grid=(heads,),
        in_specs=[pl.BlockSpec((None, nd, w, w), lambda h: (h, 0, 0, 0))],
        out_specs=pl.BlockSpec((NA_TYPES, None, NA_QR * w, NA_WR * w), lambda h: (0, h, 0, 0)),
        out_shape=S_((NA_TYPES, heads, NA_QR * w, NA_WR * w), f32),
        compiler_params=_cp(VMEM_MID, ("arbitrary",)),
    )(t)


def _rpb_reduce_tables():
    _, _, d, row_ok = _rpb_index_tables()
    flip = np.eye(GRID_W, dtype=np.float32)[::-1].copy()
    sel = np.zeros((16, NA_SEL_ROWS), np.float32)
    flat_d, flat_ok = d.reshape(-1), row_ok.reshape(-1)
    for n in range(flat_d.size):
        if flat_ok[n]:
            sel[flat_d[n], n] = 1.0
    return jnp.asarray(flip), jnp.asarray(sel)


class _Cfg:
    def __init__(self, s_len, l_len, d, f):
        self.S, self.L, self.D, self.F = s_len, l_len, d, f
        self.T = s_len + l_len
        self.TM = 256 if l_len % 256 == 0 else 128
        assert l_len == self.TM, "context length must equal the row tile"
        assert s_len % self.TM == 0 and s_len % GRID_W == 0
        self.nxt = s_len // self.TM
        self.ntt = self.T // self.TM
        self.rows = s_len // GRID_W
        assert self.rows >= 2 * NA_KH
        assert f % (2 * LANES) == 0
        self.FC = f

    def ntiles(self, with_ctx):
        return self.ntt if with_ctx else self.nxt


def _typ(cfg):
    return lambda i: (jnp.minimum(i // cfg.nxt, 1), 0, 0)


def _mesh_pos():
    return lax.axis_index("x"), lax.axis_index("y"), lax.axis_index("c")


class _Comm:
    def __init__(self, ins, outs, alias, nsem, start, finish):
        self.ins, self.outs, self.alias, self.nsem, self.start, self.finish = ins, outs, alias, nsem, start, finish


def _call(body, args, comm=None, *, grid, in_specs, out_specs, out_shape, scratch_shapes=(), **kw):
    if comm is None:
        return pl.pallas_call(body, grid=grid, in_specs=list(in_specs), out_specs=list(out_specs), out_shape=list(out_shape),
                              scratch_shapes=list(scratch_shapes), **kw)(*args), ()
    n_in, n_out, n_sc = len(in_specs), len(out_specs), len(scratch_shapes)
    ci, co = len(comm.ins), len(comm.outs)

    def carrier(*refs):
        bounds = np.cumsum([0, n_in, ci, n_out, co, n_sc])
        ins, cins, outs, couts, scr = (refs[a:b] for a, b in zip(bounds[:-1], bounds[1:]))
        send, recv = refs[bounds[-1]], refs[bounds[-1] + 1]
        first = functools.reduce(jnp.logical_and, [pl.program_id(a) == 0 for a in range(len(grid))])
        last = functools.reduce(jnp.logical_and, [pl.program_id(a) == g - 1 for a, g in enumerate(grid)])

        @pl.when(first)
        def _():
            comm.start(cins, couts, send, recv)

        body(*ins, *outs, *scr)

        @pl.when(last)
        def _():
            comm.finish(cins, couts, send, recv)

    res = pl.pallas_call(
        carrier, grid=grid, in_specs=list(in_specs) + [ANY] * ci, out_specs=list(out_specs) + [ANY] * co,
        out_shape=list(out_shape) + list(comm.outs),
        input_output_aliases={n_in + a: n_out + b for a, b in comm.alias.items()},
        scratch_shapes=list(scratch_shapes) + [pltpu.SemaphoreType.DMA((comm.nsem,)), pltpu.SemaphoreType.DMA((comm.nsem,))],
        **kw)(*args, *comm.ins)
    return res[:n_out], res[n_out:]


def _comm_only(comm, name):
    ci, co = len(comm.ins), len(comm.outs)

    def body(*refs):
        cins, couts = refs[:ci], refs[ci:ci + co]
        send, recv = refs[ci + co], refs[ci + co + 1]
        comm.start(cins, couts, send, recv)
        comm.finish(cins, couts, send, recv)

    return pl.pallas_call(
        body, name=name, in_specs=[ANY] * ci, out_specs=[ANY] * co, out_shape=list(comm.outs),
        input_output_aliases=dict(comm.alias),
        scratch_shapes=[pltpu.SemaphoreType.DMA((comm.nsem,)), pltpu.SemaphoreType.DMA((comm.nsem,))],
        compiler_params=_cp(VMEM_MID),
    )(*comm.ins)


def _half_view(ref, axis, kk, h):
    r, c = ref.shape
    if axis == 1:
        n = c // N_CHIPS
        return ref.at[pl.ds(h * (r // 2), r // 2), pl.ds(pl.multiple_of(kk * n, LANES), n)]
    n = r // N_CHIPS
    return ref.at[pl.ds(pl.multiple_of(kk * n + h * (n // 2), 8), n // 2), :]


def _other_chips(x, y):
    return [(1 - x, y), (x, 1 - y), (1 - x, 1 - y)]


def _gather_comm(arrs, axes):
    n = len(arrs)

    def copy(ref, view, sems, k, to):
        send, recv = sems
        return pltpu.make_async_remote_copy(src_ref=view, dst_ref=view, send_sem=send.at[k], recv_sem=recv.at[k],
                                            device_id=to, device_id_type=MESH)

    def start(cins, bufs, send, recv):
        x, y, c = _mesh_pos()
        for t in range(n):
            own = _half_view(bufs[t], axes[t], 2 * x + y, c)
            for j, chip in enumerate(_other_chips(x, y)):
                copy(bufs[t], own, (send, recv), 6 * t + j, (*chip, c)).start()

    def finish(cins, bufs, send, recv):
        x, y, c = _mesh_pos()
        sibling = (x, y, 1 - c)
        chips = _other_chips(x, y)
        for t in range(n):
            for j, chip in enumerate(chips):
                landed = _half_view(bufs[t], axes[t], 2 * chip[0] + chip[1], c)
                copy(bufs[t], landed, (send, recv), 6 * t + j, (*chip, c)).wait_recv()
                copy(bufs[t], landed, (send, recv), 6 * t + 3 + j, sibling).start()
        for t in range(n):
            own = _half_view(bufs[t], axes[t], 2 * x + y, c)
            for j, chip in enumerate(chips):
                kj = 2 * chip[0] + chip[1]
                copy(bufs[t], _half_view(bufs[t], axes[t], kj, 1 - c), (send, recv), 6 * t + 3 + j, sibling).wait_recv()
                copy(bufs[t], own, (send, recv), 6 * t + j, (*chip, c)).wait_send()
                copy(bufs[t], _half_view(bufs[t], axes[t], kj, c), (send, recv), 6 * t + 3 + j, sibling).wait_send()

    return _Comm(list(arrs), [S_(a.shape, a.dtype) for a in arrs], {t: t for t in range(n)}, 6 * n, start, finish)


def _scatter_comm(parts, axes):
    n = len(parts)
    peers = [(fx, fy, fc) for fx in (0, 1) for fy in (0, 1) for fc in (0, 1)][1:]

    def half_shape(a, axis):
        r, c = a.shape
        return (r // 2, c // N_CHIPS) if axis == 1 else (r // N_CHIPS // 2, c)

    def start(srcs, lands, send, recv):
        x, y, c = _mesh_pos()
        me = 4 * x + 2 * y + c
        for t in range(n):
            for r_, (fx, fy, fc) in enumerate(peers):
                dx, dy, dc = (1 - x if fx else x), (1 - y if fy else y), (1 - c if fc else c)
                pltpu.make_async_remote_copy(
                    src_ref=_half_view(srcs[t], axes[t], 2 * dx + dy, dc), dst_ref=lands[t].at[me],
                    send_sem=send.at[7 * t + r_], recv_sem=recv.at[7 * t + r_],
                    device_id=(dx, dy, dc), device_id_type=MESH).start()

    def finish(srcs, lands, send, recv):
        x, y, c = _mesh_pos()
        for t in range(n):
            mine = _half_view(srcs[t], axes[t], 2 * x + y, c)
            for r_, (fx, fy, fc) in enumerate(peers):
                sx, sy, sc = (1 - x if fx else x), (1 - y if fy else y), (1 - c if fc else c)
                cp = pltpu.make_async_remote_copy(
                    src_ref=mine, dst_ref=lands[t].at[4 * sx + 2 * sy + sc],
                    send_sem=send.at[7 * t + r_], recv_sem=recv.at[7 * t + r_],
                    device_id=(sx, sy, sc), device_id_type=MESH)
                cp.wait_recv()
                cp.wait_send()

    return _Comm(list(parts), [S_((N_DEV,) + half_shape(a, ax), a.dtype) for a, ax in zip(parts, axes)], {}, 7 * n, start, finish)


def _allgather_comm(block):
    peers = [(fx, fy, fc) for fx in (0, 1) for fy in (0, 1) for fc in (0, 1)][1:]

    def ends(x, y, c):
        for r_, (fx, fy, fc) in enumerate(peers):
            yield r_, ((1 - x if fx else x), (1 - y if fy else y), (1 - c if fc else c))

    def start(srcs, lands, send, recv):
        x, y, c = _mesh_pos()
        for r_, peer in ends(x, y, c):
            pltpu.make_async_remote_copy(src_ref=srcs[0], dst_ref=lands[0].at[4 * x + 2 * y + c], send_sem=send.at[r_],
                                         recv_sem=recv.at[r_], device_id=peer, device_id_type=MESH).start()

    def finish(srcs, lands, send, recv):
        x, y, c = _mesh_pos()
        for r_, (px, py, pc) in ends(x, y, c):
            cp = pltpu.make_async_remote_copy(src_ref=srcs[0], dst_ref=lands[0].at[4 * px + 2 * py + pc], send_sem=send.at[r_],
                                              recv_sem=recv.at[r_], device_id=(px, py, pc), device_id_type=MESH)
            cp.wait_recv()
            cp.wait_send()

    return _Comm([block], [S_((N_DEV,) + block.shape, block.dtype)], {}, len(peers), start, finish)


def _ffn_fwd(cfg, xs, mods, gvec, wgu, wd, mi, gi, with_ctx, name, comm=None, xs_ctx=None, loss_target=None):
    TM, D, F, FC = cfg.TM, cfg.D, cfg.F, cfg.FC
    nt = cfg.ntiles(with_ctx)
    R = nt * TM
    split, head = xs_ctx is not None, loss_target is not None

    def body(*refs):
        it = iter(refs)
        xs_ref = next(it)
        xc_ref = next(it) if split else None
        mods_ref, g_ref, wgu_hbm, wd_hbm = next(it), next(it), next(it), next(it)
        t_ref = next(it) if head else None
        out_ref, hb_ref, z_ref, y_ref = next(it), next(it), next(it), next(it)
        loss_ref = next(it) if head else None
        wgu_v, wd_v, sem = next(it), next(it), next(it)
        i = pl.program_id(0)

        @pl.when(i == 0)
        def _():
            c0 = pltpu.make_async_copy(wgu_hbm, wgu_v, sem.at[0])
            c1 = pltpu.make_async_copy(wd_hbm, wd_v, sem.at[1])
            c0.start(); c1.start(); c0.wait(); c1.wait()
            if head:
                loss_ref[...] = jnp.zeros_like(loss_ref)
        x = xs_ref[...]
        if split:
            x = jnp.where(i < cfg.nxt, x, xc_ref[...])
        m = mods_ref[0]
        sh, sc, gt = m[mi:mi + 1], m[mi + 1:mi + 2], m[mi + 2:mi + 3]
        xhat, _ = _rms_hat(x)
        h = (xhat * g_ref[gi:gi + 1]) * (1.0 + sc) + sh
        hb = h.astype(bf16)
        hb_ref[...] = hb
        y = jnp.zeros((TM, D), f32)
        for ch in range(F // FC):
            zg = _nn(hb, wgu_v[:, ch * FC:(ch + 1) * FC])
            zu = _nn(hb, wgu_v[:, F + ch * FC:F + (ch + 1) * FC])
            z_ref[:, ch * FC:(ch + 1) * FC] = zg.astype(bf16)
            z_ref[:, F + ch * FC:F + (ch + 1) * FC] = zu.astype(bf16)
            a = (zg * _sigmoid(zg)) * zu
            y = y + _nn(a.astype(bf16), wd_v[ch * FC:(ch + 1) * FC, :])
        y_ref[...] = y
        yhat, _ = _rms_hat(y)
        out = x + 0.5 * gt * (yhat * g_ref[gi + 1:gi + 2])
        if head:
            e = out - t_ref[...]
            out_ref[...] = e * (1.0 / D)
            loss_ref[...] += jnp.sum(jnp.mean(e * e, axis=-1, keepdims=True), axis=0, keepdims=True) * 0.5
        else:
            out_ref[...] = out

    rt = lambda c: pl.BlockSpec((TM, c), lambda i: (i, 0))
    lat = pl.BlockSpec((TM, D), lambda i: (jnp.minimum(i, cfg.nxt - 1), 0))
    x_specs, x_args = ([lat, pl.BlockSpec((TM, D), lambda i: (0, 0))], [xs, xs_ctx]) if split else ([rt(D)], [xs])
    t_specs, t_args = ([rt(D)], [loss_target]) if head else ([], [])
    l_specs, l_shape = ([pl.BlockSpec((8, LANES), lambda i: (0, 0))], [S_((8, LANES), f32)]) if head else ([], [])
    return _call(
        body, (*x_args, mods, gvec, wgu, wd, *t_args), comm, name=name, grid=(nt,),
        in_specs=x_specs + [pl.BlockSpec((1, N_MOD, D), _typ(cfg)), pl.BlockSpec((6, D), lambda i: (0, 0)), ANY, ANY] + t_specs,
        out_specs=[rt(D), rt(D), rt(2 * F), rt(D)] + l_specs,
        out_shape=[S_((R, D), f32), S_((R, D), bf16), S_((R, 2 * F), bf16), S_((R, D), f32)] + l_shape,
        scratch_shapes=[pltpu.VMEM((D, 2 * F), bf16), pltpu.VMEM((F, D), bf16), pltpu.SemaphoreType.DMA((2,))],
        compiler_params=_cp(VMEM_BIG, ("arbitrary",)),
    )


def _ffn_bwd(cfg, dout, xs, z, y, mods, gvec, wgu, wd, mi, gi, with_ctx, name, comm=None, xs_ctx=None):
    TM, D, F, FC = cfg.TM, cfg.D, cfg.F, cfg.FC
    nt = cfg.ntiles(with_ctx)
    R = nt * TM
    ntyp = 2 if with_ctx else 1
    split = xs_ctx is not None

    def body(*refs):
        it = iter(refs)
        do_ref, xs_ref = next(it), next(it)
        xc_ref = next(it) if split else None
        z_ref, y_ref, mods_ref, g_ref, wgu_hbm, wd_hbm = (next(it) for _ in range(6))
        dx_ref, dz_ref, dy_ref, a_ref, dm_ref, dg_ref, wgu_v, wd_v, sem = (next(it) for _ in range(9))
        i = pl.program_id(0)

        @pl.when(i == 0)
        def _():
            c0 = pltpu.make_async_copy(wgu_hbm, wgu_v, sem.at[0])
            c1 = pltpu.make_async_copy(wd_hbm, wd_v, sem.at[1])
            c0.start(); c1.start(); c0.wait(); c1.wait()
            dg_ref[...] = jnp.zeros_like(dg_ref)

        @pl.when((i == 0) | (i == cfg.nxt))
        def _():
            dm_ref[...] = jnp.zeros_like(dm_ref)

        do = do_ref[...]
        x = xs_ref[...]
        if split:
            x = jnp.where(i < cfg.nxt, x, xc_ref[...])
        m = mods_ref[0]
        sc, gt = m[mi + 1:mi + 2], m[mi + 2:mi + 3]
        g_pre, g_post = g_ref[gi:gi + 1], g_ref[gi + 1:gi + 2]
        xhat, rinv0 = _rms_hat(x)
        n0 = xhat * g_pre
        yhat, rinv1 = _rms_hat(y_ref[...])
        d_gt = _rsum(0.5 * do * (yhat * g_post))
        dr = (0.5 * gt) * do
        dg_post = _rsum(dr * yhat)
        dy = _rms_bwd(dr * g_post, yhat, rinv1)
        dyb = dy.astype(bf16)
        dy_ref[...] = dyb
        dh = jnp.zeros((TM, D), f32)
        for ch in range(F // FC):
            zg = z_ref[:, ch * FC:(ch + 1) * FC].astype(f32)
            zu = z_ref[:, F + ch * FC:F + (ch + 1) * FC].astype(f32)
            sg = _sigmoid(zg)
            silu = zg * sg
            a_ref[:, ch * FC:(ch + 1) * FC] = (silu * zu).astype(bf16)
            da = _nt(dyb, wd_v[ch * FC:(ch + 1) * FC, :])
            dzu = (da * silu).astype(bf16)
            dzg = (da * zu * (sg * (1.0 + zg * (1.0 - sg)))).astype(bf16)
            dz_ref[:, ch * FC:(ch + 1) * FC] = dzg
            dz_ref[:, F + ch * FC:F + (ch + 1) * FC] = dzu
            dh = dh + _nt(dzg, wgu_v[:, ch * FC:(ch + 1) * FC]) + _nt(dzu, wgu_v[:, F + ch * FC:F + (ch + 1) * FC])
        d_sh = _rsum(dh)
        d_sc = _rsum(dh * n0)
        dn = dh * (1.0 + sc)
        dg_pre = _rsum(dn * xhat)
        dx = do + _rms_bwd(dn * g_pre, xhat, rinv0)
        if split:
            @pl.when(i < cfg.nxt)
            def _():
                dx_ref[...] = dx
        else:
            dx_ref[...] = dx
        dm_ref[0] += jnp.concatenate([d_sh, d_sc, d_gt], axis=0)
        dg_ref[...] += jnp.concatenate([dg_pre, dg_post], axis=0)

    rt = lambda c: pl.BlockSpec((TM, c), lambda i: (i, 0))
    lat = pl.BlockSpec((TM, D), lambda i: (jnp.minimum(i, cfg.nxt - 1), 0))
    x_specs, x_args = ([lat, pl.BlockSpec((TM, D), lambda i: (0, 0))], [xs, xs_ctx]) if split else ([rt(D)], [xs])
    return _call(
        body, (dout, *x_args, z, y, mods, gvec, wgu, wd), comm, name=name, grid=(nt,),
        in_specs=[rt(D)] + x_specs + [rt(2 * F), rt(D), pl.BlockSpec((1, N_MOD, D), _typ(cfg)),
                                       pl.BlockSpec((6, D), lambda i: (0, 0)), ANY, ANY],
        out_specs=[lat if split else rt(D), rt(2 * F), rt(D), rt(F), pl.BlockSpec((1, 3, D), _typ(cfg)),
                   pl.BlockSpec((2, D), lambda i: (0, 0))],
        out_shape=[S_((cfg.S if split else R, D), f32), S_((R, 2 * F), bf16), S_((R, D), bf16), S_((R, F), bf16),
                   S_((ntyp, 3, D), f32), S_((2, D), f32)],
        scratch_shapes=[pltpu.VMEM((D, 2 * F), bf16), pltpu.VMEM((F, D), bf16), pltpu.SemaphoreType.DMA((2,))],
        compiler_params=_cp(VMEM_BIG, ("arbitrary",)),
    )


def _wgrad(a, b, k_rows, name, comm=None):
    M, N = a.shape[1], b.shape[1]
    tn = _div_tile(N, 1, WGRAD_TN, LANES) if N > WGRAD_TN // 2 else N
    tn = N // 2 if tn == N and N % (2 * LANES) == 0 else tn
    room = VMEM_BIG - WGRAD_SLACK - 2 * M * tn * 6
    tk = _div_tile(k_rows, 1, min(WGRAD_TK, room // (4 * (M + tn))), LANES)
    nk = k_rows // tk

    def body(a_ref, b_ref, o_ref, ob_ref):
        k = pl.program_id(1)

        @pl.when(k == 0)
        def _():
            o_ref[...] = jnp.zeros_like(o_ref)
        o_ref[...] += _tn(a_ref[...], b_ref[...])

        @pl.when(k == nk - 1)
        def _():
            ob_ref[...] = o_ref[...].astype(bf16)

    ospec = pl.BlockSpec((M, tn), lambda n, k: (0, n))
    return _call(
        body, (a, b), comm, name=name, grid=(N // tn, nk),
        in_specs=[pl.BlockSpec((tk, M), lambda n, k: (k, 0)), pl.BlockSpec((tk, tn), lambda n, k: (k, n))],
        out_specs=[ospec, ospec], out_shape=[S_((M, N), f32), S_((M, N), bf16)],
        compiler_params=_cp(VMEM_BIG, ("arbitrary", "arbitrary")),
    )


def _tmpre_fwd(cfg, xs, mods, gvec, w_in, cos, sin, name):
    TM, D = cfg.TM, cfg.D
    nt, R = cfg.ntt, cfg.T
    W = NA_WIDTH

    def body(xs_ref, mods_ref, g_ref, w_ref, cos_ref, sin_ref, hb_ref, q_ref, k_ref, v_ref, u_ref):
        x = xs_ref[...]
        m = mods_ref[0]
        xhat, _ = _rms_hat(x)
        hb = ((xhat * g_ref[2:3]) * (1.0 + m[4:5]) + m[3:4]).astype(bf16)
        hb_ref[...] = hb
        p = _nn(hb, w_ref[...])
        cs = jnp.tile(cos_ref[...], (1, W // LANES))
        sn = jnp.tile(sin_ref[...], (1, W // LANES))
        q = p[:, 0:W]
        k = p[:, W:2 * W]
        q_ref[...] = ((q * cs + _swap16(q) * sn) * (HEAD_DIM ** -0.5)).astype(bf16)
        k_ref[...] = (k * cs + _swap16(k) * sn).astype(bf16)
        v_ref[...] = p[:, 2 * W:3 * W].astype(bf16)
        u_ref[...] = p[:, 3 * W:]

    rt = lambda c: pl.BlockSpec((TM, c), lambda i: (i, 0))
    return pl.pallas_call(
        body, name=name, grid=(nt,),
        in_specs=[rt(D), pl.BlockSpec((1, N_MOD, D), _typ(cfg)), pl.BlockSpec((6, D), lambda i: (0, 0)),
                  pl.BlockSpec((D, IN_WIDTH), lambda i: (0, 0)), rt(LANES), rt(LANES)],
        out_specs=[rt(D), rt(W), rt(W), rt(W), rt(POOL_WIDTH)],
        out_shape=[S_((R, D), bf16), S_((R, W), bf16), S_((R, W), bf16), S_((R, W), bf16), S_((R, POOL_WIDTH), f32)],
        compiler_params=_cp(VMEM_MID, ("arbitrary",)),
    )(xs, mods, gvec, w_in, cos, sin)


def _tmpre_bwd(cfg, lat, ctx_terms, du_has_ctx, cos, sin, w_in, xs, mods, gvec, dres, res_with_ctx, name):
    TM, D = cfg.TM, cfg.D
    nt, R = cfg.ntt, cfg.T
    nres = cfg.ntiles(res_with_ctx)
    W = NA_WIDTH
    n_ctx = [len(t) for t in ctx_terms]
    flat_ctx = [a for t in ctx_terms for a in t]
    n_asm = 4 + len(flat_ctx) + 2

    def assemble(refs, o_ref):
        dq_ref, dk_ref, dv_ref, du_ref = refs[:4]
        ctx_refs = refs[4:4 + len(flat_ctx)]
        cos_ref, sin_ref = refs[4 + len(flat_ctx):]
        is_ctx = pl.program_id(0) >= cfg.nxt
        vals, off = [], 0
        for lat_ref, n in zip((dq_ref, dk_ref, dv_ref), n_ctx):
            cv = jnp.zeros((TM, W), f32)
            for r_ in ctx_refs[off:off + n]:
                cv = cv + r_[...]
            off += n
            vals.append(jnp.where(is_ctx, cv, lat_ref[...]))
        du_ = du_ref[...] if du_has_ctx else jnp.where(is_ctx, 0.0, du_ref[...])
        cs = jnp.tile(cos_ref[...], (1, W // LANES))
        sn = jnp.tile(sin_ref[...], (1, W // LANES))
        dq_ = vals[0] * (HEAD_DIM ** -0.5)
        dk_ = vals[1]
        o_ref[:, 0:W] = (dq_ * cs + _swap16(dq_ * sn)).astype(bf16)
        o_ref[:, W:2 * W] = (dk_ * cs + _swap16(dk_ * sn)).astype(bf16)
        o_ref[:, 2 * W:3 * W] = vals[2].astype(bf16)
        o_ref[:, 3 * W:] = du_.astype(bf16)

    def body(*refs):
        w_ref, xs_ref, mods_ref, g_ref, dres_ref, dx_ref, dp_ref, dm_ref, dg_ref = refs[n_asm:]
        i = pl.program_id(0)

        @pl.when(i == 0)
        def _():
            dg_ref[...] = jnp.zeros_like(dg_ref)

        @pl.when((i == 0) | (i == cfg.nxt))
        def _():
            dm_ref[...] = jnp.zeros_like(dm_ref)

        assemble(refs[:n_asm], dp_ref)
        dh = _nt(dp_ref[...], w_ref[...])
        x = xs_ref[...]
        m = mods_ref[0]
        g2 = g_ref[2:3]
        xhat, rinv = _rms_hat(x)
        d_sh = _rsum(dh)
        d_sc = _rsum(dh * (xhat * g2))
        dn = dh * (1.0 + m[4:5])
        dg_ref[...] += _rsum(dn * xhat)
        dx = _rms_bwd(dn * g2, xhat, rinv)
        res = dres_ref[...]
        if nres < nt:
            res = jnp.where(i < nres, res, 0.0)
        dx_ref[...] = res + dx
        dm_ref[0] += jnp.concatenate([d_sh, d_sc], axis=0)

    rt = lambda c: pl.BlockSpec((TM, c), lambda i: (i, 0))
    lat_spec = pl.BlockSpec((TM, W), lambda i: (jnp.minimum(i, cfg.nxt - 1), 0))
    du_spec = rt(POOL_WIDTH) if du_has_ctx else lat_spec
    asm_specs = ([lat_spec, lat_spec, lat_spec, du_spec] + [pl.BlockSpec((TM, W), lambda i: (0, 0))] * len(flat_ctx)
                 + [rt(LANES), rt(LANES)])
    return pl.pallas_call(
        body, name=name, grid=(nt,),
        in_specs=asm_specs + [pl.BlockSpec((D, IN_WIDTH), lambda i: (0, 0)), rt(D),
                              pl.BlockSpec((1, N_MOD, D), _typ(cfg)), pl.BlockSpec((6, D), lambda i: (0, 0)),
                              pl.BlockSpec((TM, D), lambda i: (jnp.minimum(i, nres - 1), 0))],
        out_specs=[rt(D), rt(IN_WIDTH), pl.BlockSpec((1, 2, D), _typ(cfg)), pl.BlockSpec((1, D), lambda i: (0, 0))],
        out_shape=[S_((R, D), f32), S_((R, IN_WIDTH), bf16), S_((2, 2, D), f32), S_((1, D), f32)],
        compiler_params=_cp(VMEM_MID, ("arbitrary",)),
    )(*lat, *flat_ctx, cos, sin, w_in, xs, mods, gvec, dres)


def _na_block(cfg, b):
    return jnp.clip(NA_QR * b - NA_KH // 2, 0, cfg.rows - NA_WR)


NA_KV_CHUNKS = 4
NA_KV_SEMS = 2 * (NA_KV_CHUNKS + 1)


def _na_load_kv(step, cfg, k_hbm, v_hbm, k_v, v_v, sem):
    rows_per = cfg.rows // NA_KV_CHUNKS
    pieces = [(j * rows_per * GRID_W, rows_per * GRID_W) for j in range(NA_KV_CHUNKS)] + [(cfg.S, cfg.L)]
    copies = [(pltpu.make_async_copy(k_hbm.at[pl.ds(r0, n)], k_v.at[pl.ds(r0, n)], sem.at[2 * j]),
               pltpu.make_async_copy(v_hbm.at[pl.ds(r0, n)], v_v.at[pl.ds(r0, n)], sem.at[2 * j + 1]))
              for j, (r0, n) in enumerate(pieces)]

    @pl.when(step == 0)
    def _():
        for ck, cv in copies:
            ck.start()
            cv.start()

    for j, (ck, cv) in enumerate(copies):
        row0 = 0 if j == NA_KV_CHUNKS else j * rows_per
        first_block = 0 if row0 < NA_WR else (row0 - NA_WR + NA_KH // 2) // NA_QR + 1

        @pl.when(step == first_block)
        def _(ck=ck, cv=cv):
            ck.wait()
            cv.wait()


def _na_load_bias(b, nb, b_hbm, b_v, sem):
    for typ, at in ((0, 0), (1, 1), (2, nb - 1)):
        @pl.when(b == at)
        def _(typ=typ):
            cp = pltpu.make_async_copy(b_hbm.at[typ], b_v, sem)
            cp.start()
            cp.wait()


def _na_probs(qh, klh, kch, bias):
    s_loc = _nt(qh, klh) + bias
    s_ctx = _nt(qh, kch)
    mx = jnp.maximum(jnp.max(s_loc, axis=-1, keepdims=True), jnp.max(s_ctx, axis=-1, keepdims=True))
    e_loc = jnp.exp(s_loc - mx)
    e_ctx = jnp.exp(s_ctx - mx)
    inv = 1.0 / (jnp.sum(e_loc, axis=-1, keepdims=True) + jnp.sum(e_ctx, axis=-1, keepdims=True))
    return e_loc * inv, e_ctx * inv


def _na_fwd(cfg, q, k, v, bexp, name, comm=None):
    S, L, T = cfg.S, cfg.L, cfg.T
    NQ, NW = NA_QR * GRID_W, NA_WR * GRID_W
    nb = cfg.rows // NA_QR

    def body(q_ref, k_hbm, v_hbm, b_hbm, o_ref, k_v, v_v, b_v, sem):
        b = pl.program_id(0)
        _na_load_kv(b, cfg, k_hbm, v_hbm, k_v, v_v, sem)
        _na_load_bias(b, nb, b_hbm, b_v, sem.at[NA_KV_SEMS])
        st = pl.multiple_of(_na_block(cfg, b) * GRID_W, GRID_W)
        first = lax.broadcasted_iota(jnp.int32, (NQ, LANES), 1) < HEAD_DIM
        for hp in range(NA_HEADS // 2):
            ls = slice(hp * LANES, (hp + 1) * LANES)
            q2 = q_ref[:, ls]
            kl, vl = k_v[pl.ds(st, NW), ls], v_v[pl.ds(st, NW), ls]
            kc, vc = k_v[S:T, ls], v_v[S:T, ls]
            o2 = []
            for hh in range(2):
                qm = jnp.where(first if hh == 0 else ~first, q2, jnp.zeros_like(q2))
                p_loc, p_ctx = _na_probs(qm, kl, kc, b_v[2 * hp + hh])
                o2.append(_nn(p_loc.astype(bf16), vl) + _nn(p_ctx.astype(bf16), vc))
            o_ref[:, ls] = jnp.where(first, o2[0], o2[1]).astype(bf16)

    return _call(
        body, (q, k, v, bexp), comm, name=name, grid=(nb,),
        in_specs=[pl.BlockSpec((NQ, NA_WIDTH), lambda b: (b, 0)), ANY, ANY, ANY],
        out_specs=[pl.BlockSpec((NQ, NA_WIDTH), lambda b: (b, 0))],
        out_shape=[S_((S, NA_WIDTH), bf16)],
        scratch_shapes=[pltpu.VMEM((T, NA_WIDTH), bf16), pltpu.VMEM((T, NA_WIDTH), bf16),
                        pltpu.VMEM((NA_HEADS, NQ, NW), f32), pltpu.SemaphoreType.DMA((NA_KV_SEMS + 1,))],
        compiler_params=_cp(VMEM_MID, ("arbitrary",)),
    )


def _na_bwd(cfg, do, q, k, v, bexp, name, comm=None):
    S, L, T, rows = cfg.S, cfg.L, cfg.T, cfg.rows
    NQ, NW = NA_QR * GRID_W, NA_WR * GRID_W
    NSLOT = 2 * NA_KH
    nb = rows // NA_QR
    bmax = (rows - NA_WR) // NA_QR
    steps = 2 * nb - bmax
    W = NA_WIDTH
    assert nb >= 3 and bmax >= 1 and rows - NA_QR * bmax <= NSLOT

    def out_group(g):
        return jnp.where(g >= nb, g - nb + bmax, jnp.clip(g - 1, 0, bmax - 1))

    def body(do_ref, q_ref, k_hbm, v_hbm, b_hbm, dq_ref, dk_ref, dv_ref, dkc_ref, dvc_ref, db_hbm,
             k_v, v_v, b_v, db_v, ak, av, akc, avc, sem):
        g = pl.program_id(0)

        _na_load_kv(g, cfg, k_hbm, v_hbm, k_v, v_v, sem)

        @pl.when(g == 0)
        def _():
            db_v[...] = jnp.zeros_like(db_v)
            ak[...] = jnp.zeros_like(ak)
            av[...] = jnp.zeros_like(av)
            akc[...] = jnp.zeros_like(akc)
            avc[...] = jnp.zeros_like(avc)

        for typ, at in ((0, 1), (1, nb - 1)):
            @pl.when(g == at)
            def _(typ=typ):
                cp = pltpu.make_async_copy(db_v, db_hbm.at[typ], sem.at[NA_KV_SEMS])
                cp.start()
                cp.wait()
                db_v[...] = jnp.zeros_like(db_v)

        @pl.when(g < nb)
        def _():
            _na_load_bias(g, nb, b_hbm, b_v, sem.at[NA_KV_SEMS])
            ws = _na_block(cfg, g)
            st = pl.multiple_of(ws * GRID_W, GRID_W)
            first = lax.broadcasted_iota(jnp.int32, (NQ, LANES), 1) < HEAD_DIM
            for hp in range(NA_HEADS // 2):
                ls = slice(hp * LANES, (hp + 1) * LANES)
                q2, do2 = q_ref[:, ls], do_ref[:, ls]
                kl, vl = k_v[pl.ds(st, NW), ls], v_v[pl.ds(st, NW), ls]
                kc, vc = k_v[S:T, ls], v_v[S:T, ls]
                dq2 = []
                dk2 = jnp.zeros((NW, LANES), f32)
                dv2 = jnp.zeros((NW, LANES), f32)
                dkc2 = jnp.zeros((L, LANES), f32)
                dvc2 = jnp.zeros((L, LANES), f32)
                for hh in range(2):
                    keep = first if hh == 0 else ~first
                    qm = jnp.where(keep, q2, jnp.zeros_like(q2))
                    dom = jnp.where(keep, do2, jnp.zeros_like(do2))
                    p_loc, p_ctx = _na_probs(qm, kl, kc, b_v[2 * hp + hh])
                    dp_loc = _nt(dom, vl)
                    dp_ctx = _nt(dom, vc)
                    delta = jnp.sum(p_loc * dp_loc, axis=-1, keepdims=True) + jnp.sum(p_ctx * dp_ctx, axis=-1, keepdims=True)
                    ds_loc = p_loc * (dp_loc - delta)
                    ds_ctx = p_ctx * (dp_ctx - delta)
                    db_v[2 * hp + hh, :, 0:NW] += ds_loc
                    dsl, dsc = ds_loc.astype(bf16), ds_ctx.astype(bf16)
                    dq2.append(_nn(dsl, kl) + _nn(dsc, kc))
                    dk2 = dk2 + _tn(dsl, qm)
                    dv2 = dv2 + _tn(p_loc.astype(bf16), dom)
                    dkc2 = dkc2 + _tn(dsc, qm)
                    dvc2 = dvc2 + _tn(p_ctx.astype(bf16), dom)
                dq_ref[:, ls] = jnp.where(first, dq2[0], dq2[1])
                akc[:, ls] += dkc2
                avc[:, ls] += dvc2
                for kk in range(NA_WR):
                    slot = (ws + kk) % NSLOT
                    ak[slot, :, ls] += dk2[kk * GRID_W:(kk + 1) * GRID_W, :]
                    av[slot, :, ls] += dv2[kk * GRID_W:(kk + 1) * GRID_W, :]

        @pl.when(((g >= 1) & (g <= bmax)) | (g >= nb))
        def _():
            base = NA_QR * (out_group(g) % (NSLOT // NA_QR))
            for t in range(NA_QR):
                dk_ref[t * GRID_W:(t + 1) * GRID_W, :] = ak[base + t]
                dv_ref[t * GRID_W:(t + 1) * GRID_W, :] = av[base + t]
                ak[base + t] = jnp.zeros((GRID_W, W), f32)
                av[base + t] = jnp.zeros((GRID_W, W), f32)

        @pl.when(g == nb - 1)
        def _():
            cp = pltpu.make_async_copy(db_v, db_hbm.at[2], sem.at[NA_KV_SEMS])
            cp.start()
            cp.wait()

        @pl.when(g == steps - 1)
        def _():
            dkc_ref[...] = akc[...]
            dvc_ref[...] = avc[...]

    qmap = lambda g: (jnp.minimum(g, nb - 1), 0)
    kmap = lambda g: (out_group(g), 0)
    full = lambda g: (0, 0)
    return _call(
        body, (do, q, k, v, bexp), comm, name=name, grid=(steps,),
        in_specs=[pl.BlockSpec((NQ, W), qmap), pl.BlockSpec((NQ, W), qmap), ANY, ANY, ANY],
        out_specs=[pl.BlockSpec((NQ, W), qmap), pl.BlockSpec((NQ, W), kmap), pl.BlockSpec((NQ, W), kmap),
                   pl.BlockSpec((L, W), full), pl.BlockSpec((L, W), full), ANY],
        out_shape=[S_((S, W), f32), S_((S, W), f32), S_((S, W), f32), S_((L, W), f32), S_((L, W), f32),
                   S_((NA_TYPES, NA_HEADS, NQ, NA_WPAD), f32)],
        scratch_shapes=[pltpu.VMEM((T, W), bf16), pltpu.VMEM((T, W), bf16),
                        pltpu.VMEM((NA_HEADS, NQ, NW), f32), pltpu.VMEM((NA_HEADS, NQ, NA_WPAD), f32),
                        pltpu.VMEM((NSLOT, GRID_W, W), f32), pltpu.VMEM((NSLOT, GRID_W, W), f32),
                        pltpu.VMEM((L, W), f32), pltpu.VMEM((L, W), f32), pltpu.SemaphoreType.DMA((NA_KV_SEMS + 1,))],
        compiler_params=_cp(VMEM_BIG, ("arbitrary",)),
    )


def _rpb_reduce(dbias, flip, sel, name):
    nq, w = NA_QR * GRID_W, GRID_W

    def diag_body(x_ref, j_ref, o_ref):
        rows = []
        for i in range(NA_QR):
            xr = jnp.dot(j_ref[...], x_ref[i * w:(i + 1) * w, :], preferred_element_type=f32, precision=lax.Precision.HIGHEST)
            rows.append(jnp.sum(pltpu.roll(xr, 0, 1, stride=1, stride_axis=0), axis=0, keepdims=True))
        o_ref[...] = jnp.concatenate(rows + [jnp.zeros((8 - NA_QR, NA_WPAD), f32)], axis=0)

    diag = pl.pallas_call(
        diag_body, name=name + "_diag", grid=(NA_TYPES, NA_HEADS),
        in_specs=[pl.BlockSpec((None, None, nq, NA_WPAD), lambda t, h: (t, h, 0, 0)), pl.BlockSpec((w, w), lambda t, h: (0, 0))],
        out_specs=pl.BlockSpec((None, None, 8, NA_WPAD), lambda t, h: (t, h, 0, 0)),
        out_shape=S_((NA_TYPES, NA_HEADS, 8, NA_WPAD), f32),
        compiler_params=_cp(VMEM_MID, ("arbitrary", "arbitrary")),
    )(dbias, flip)
    lo = w - NA_KW
    y = diag[:, :, :NA_QR, lo:lo + NA_WR * w].reshape(NA_TYPES, NA_HEADS, NA_QR, NA_WR, w)
    y = jnp.transpose(y, (1, 0, 2, 3, 4)).reshape(NA_HEADS, NA_TYPES * NA_QR * NA_WR, w)
    y = jnp.pad(y, ((0, 0), (0, NA_SEL_ROWS - y.shape[1]), (0, LANES - w)))

    def body(y_ref, sel_ref, o_ref):
        o_ref[...] = jnp.dot(sel_ref[...], y_ref[...], preferred_element_type=f32, precision=lax.Precision.HIGHEST)

    return pl.pallas_call(
        body, name=name, grid=(NA_HEADS,),
        in_specs=[pl.BlockSpec((None, NA_SEL_ROWS, LANES), lambda h: (h, 0, 0)), pl.BlockSpec((16, NA_SEL_ROWS), lambda h: (0, 0))],
        out_specs=pl.BlockSpec((None, 16, LANES), lambda h: (h, 0, 0)),
        out_shape=S_((NA_HEADS, 16, LANES), f32),
        compiler_params=_cp(VMEM_MID, ("arbitrary",)),
    )(y, sel)


def _ctx_attn_fwd(cfg, q, k, v, name):
    L = cfg.L
    blk = cfg.S // L

    def body(q_ref, k_ref, v_ref, o_ref):
        qv, kv, vv = q_ref[...], k_ref[...], v_ref[...]
        outs = []
        for h in range(NA_HEADS):
            hs = slice(h * HEAD_DIM, (h + 1) * HEAD_DIM)
            s = _nt(qv[:, hs], kv[:, hs])
            e = jnp.exp(s - jnp.max(s, axis=-1, keepdims=True))
            p = e * (1.0 / jnp.sum(e, axis=-1, keepdims=True))
            outs.append(_nn(p.astype(bf16), vv[:, hs]))
        o_ref[...] = jnp.concatenate(outs, axis=-1).astype(bf16)

    spec = pl.BlockSpec((L, NA_WIDTH), lambda i: (blk, 0))
    return pl.pallas_call(
        body, name=name, grid=(1,), in_specs=[spec, spec, spec],
        out_specs=pl.BlockSpec((L, NA_WIDTH), lambda i: (0, 0)), out_shape=S_((L, NA_WIDTH), bf16),
        compiler_params=_cp(VMEM_MID, ("arbitrary",)),
    )(q, k, v)


def _ctx_attn_bwd(cfg, do, q, k, v, name):
    L = cfg.L
    blk = cfg.S // L

    def body(do_ref, q_ref, k_ref, v_ref, dq_ref, dk_ref, dv_ref):
        dov, qv, kv, vv = do_ref[...], q_ref[...], k_ref[...], v_ref[...]
        dqs, dks, dvs = [], [], []
        for h in range(NA_HEADS):
            hs = slice(h * HEAD_DIM, (h + 1) * HEAD_DIM)
            qh, kh, doh = qv[:, hs], kv[:, hs], dov[:, hs]
            s = _nt(qh, kh)
            e = jnp.exp(s - jnp.max(s, axis=-1, keepdims=True))
            p = e * (1.0 / jnp.sum(e, axis=-1, keepdims=True))
            dp = _nt(doh, vv[:, hs])
            ds = (p * (dp - jnp.sum(p * dp, axis=-1, keepdims=True))).astype(bf16)
            dqs.append(_nn(ds, kh))
            dks.append(_tn(ds, qh))
            dvs.append(_tn(p.astype(bf16), doh))
        dq_ref[...] = jnp.concatenate(dqs, axis=-1)
        dk_ref[...] = jnp.concatenate(dks, axis=-1)
        dv_ref[...] = jnp.concatenate(dvs, axis=-1)

    spec = pl.BlockSpec((L, NA_WIDTH), lambda i: (blk, 0))
    ospec = pl.BlockSpec((L, NA_WIDTH), lambda i: (0, 0))
    return pl.pallas_call(
        body, name=name, grid=(1,), in_specs=[spec, spec, spec, spec],
        out_specs=[ospec, ospec, ospec], out_shape=[S_((L, NA_WIDTH), f32)] * 3,
        compiler_params=_cp(VMEM_MID, ("arbitrary",)),
    )(do, q, k, v)


def _pool_centered(u, band, inv):
    return _split_sum(_nn, band, u) * inv - u


def _split_sum(mm, band, t):
    hi = t.astype(bf16)
    lo = (t - hi.astype(f32)).astype(bf16)
    s = mm(band, jnp.concatenate([hi, lo], axis=1))
    n = t.shape[1]
    return s[:, :n] + s[:, n:]


def _pool_mix(u_ref, band_ref, inv_ref, w_ref, ps_ref):
    C = POOL_CH
    outs = []
    for g in range(POOL_GROUPS):
        d = _pool_centered(u_ref[:, g * C:(g + 1) * C], band_ref[0, g], inv_ref[0, g])
        outs.append(_nn(d.astype(bf16), w_ref[g].astype(bf16)) * ps_ref[:, g * C:(g + 1) * C])
    return jnp.concatenate(outs, axis=-1).astype(bf16)


def _pool_bwd(cfg, dmix, u, band, inv, w_pool, pool_scale, with_ctx, name):
    TM = cfg.TM
    nt = cfg.ntiles(with_ctx)
    C = POOL_CH

    def body(dy_ref, u_ref, band_ref, inv_ref, w_ref, ps_ref, du_ref, dw_ref, dps_ref):
        @pl.when(pl.program_id(0) == 0)
        def _():
            dw_ref[...] = jnp.zeros_like(dw_ref)
            dps_ref[...] = jnp.zeros_like(dps_ref)

        dus, dpss = [], []
        for g in range(POOL_GROUPS):
            gs = slice(g * C, (g + 1) * C)
            band_g, inv_g = band_ref[0, g], inv_ref[0, g]
            db = _pool_centered(u_ref[:, gs], band_g, inv_g).astype(bf16)
            wb = w_ref[g].astype(bf16)
            dy = dy_ref[:, gs].astype(f32)
            dpss.append(_rsum(dy * _nn(db, wb)))
            dys = (dy * ps_ref[:, gs]).astype(bf16)
            dw_ref[g] += _tn(db, dys)
            dd = _nt(dys, wb)
            dus.append(_split_sum(_tn, band_g, dd * inv_g) - dd)
        du_ref[...] = jnp.concatenate(dus, axis=-1)
        dps_ref[...] += jnp.concatenate(dpss, axis=-1)

    typ4 = lambda i: (jnp.minimum(i // cfg.nxt, 1), 0, 0, 0)
    return pl.pallas_call(
        body, name=name, grid=(nt,),
        in_specs=[pl.BlockSpec((TM, POOL_WIDTH), lambda i: (i, 1)), pl.BlockSpec((TM, POOL_WIDTH), lambda i: (i, 0)),
                  pl.BlockSpec((1, POOL_GROUPS, TM, TM), typ4), pl.BlockSpec((1, POOL_GROUPS, TM, 1), typ4),
                  pl.BlockSpec((POOL_GROUPS, C, C), lambda i: (0, 0, 0)), pl.BlockSpec((1, POOL_WIDTH), lambda i: (0, 0))],
        out_specs=[pl.BlockSpec((TM, POOL_WIDTH), lambda i: (i, 0)), pl.BlockSpec((POOL_GROUPS, C, C), lambda i: (0, 0, 0)),
                   pl.BlockSpec((1, POOL_WIDTH), lambda i: (0, 0))],
        out_shape=[S_((nt * TM, POOL_WIDTH), f32), S_((POOL_GROUPS, C, C), f32), S_((1, POOL_WIDTH), f32)],
        compiler_params=_cp(VMEM_MID, ("arbitrary",)),
    )(dmix, u, band, inv, w_pool, pool_scale)


def _tmpost_fwd(cfg, na_x, na_c, u, band, inv, w_pool, pool_scale, w_out, xs, mods, gvec, name):
    TM, D = cfg.TM, cfg.D
    with_ctx = na_c is not None
    nt = cfg.ntiles(with_ctx)
    R = nt * TM

    def body(*refs):
        if with_ctx:
            nax_ref, nac_ref = refs[:2]
            na = jnp.where(pl.program_id(0) < cfg.nxt, nax_ref[...], nac_ref[...])
        else:
            na = refs[0][...]
        (u_ref, band_ref, inv_ref, wp_ref, ps_ref, w_ref, xs_ref, mods_ref, g_ref,
         out_ref, opre_ref, mix_ref) = refs[2 if with_ctx else 1:]
        pool_v = _pool_mix(u_ref, band_ref, inv_ref, wp_ref, ps_ref)
        mix_ref[:, 0:NA_WIDTH] = na
        mix_ref[:, NA_WIDTH:] = pool_v
        o = _nn(na, w_ref[0:NA_WIDTH, :]) + _nn(pool_v, w_ref[NA_WIDTH:, :])
        opre_ref[...] = o
        ohat, _ = _rms_hat(o)
        out_ref[...] = xs_ref[...] + mods_ref[0][5:6] * (ohat * g_ref[3:4])

    rt = lambda c: pl.BlockSpec((TM, c), lambda i: (i, 0))
    na_specs = [pl.BlockSpec((TM, NA_WIDTH), lambda i: (jnp.minimum(i, cfg.nxt - 1), 0))]
    na_args = [na_x]
    if with_ctx:
        na_specs.append(pl.BlockSpec((TM, NA_WIDTH), lambda i: (0, 0)))
        na_args.append(na_c)
    typ4 = lambda i: (jnp.minimum(i // cfg.nxt, 1), 0, 0, 0)
    pool_specs = [rt(POOL_WIDTH), pl.BlockSpec((1, POOL_GROUPS, TM, TM), typ4), pl.BlockSpec((1, POOL_GROUPS, TM, 1), typ4),
                  pl.BlockSpec((POOL_GROUPS, POOL_CH, POOL_CH), lambda i: (0, 0, 0)), pl.BlockSpec((1, POOL_WIDTH), lambda i: (0, 0))]
    return pl.pallas_call(
        body, name=name, grid=(nt,),
        in_specs=na_specs + pool_specs + [pl.BlockSpec((MIX_WIDTH, D), lambda i: (0, 0)), rt(D),
                                          pl.BlockSpec((1, N_MOD, D), _typ(cfg)), pl.BlockSpec((6, D), lambda i: (0, 0))],
        out_specs=[rt(D), rt(D), rt(MIX_WIDTH)],
        out_shape=[S_((R, D), f32), S_((R, D), f32), S_((R, MIX_WIDTH), bf16)],
        compiler_params=_cp(VMEM_MID, ("arbitrary",)),
    )(*na_args, u, band, inv, w_pool, pool_scale, w_out, xs, mods, gvec)


def _tmpost_bwd(cfg, dout, opre, w_out, mods, gvec, with_ctx, name):
    TM, D = cfg.TM, cfg.D
    nt = cfg.ntiles(with_ctx)
    R = nt * TM
    ntyp = 2 if with_ctx else 1

    def body(do_ref, opre_ref, w_ref, mods_ref, g_ref, dop_ref, dmix_ref, dm_ref, dg_ref):
        i = pl.program_id(0)

        @pl.when(i == 0)
        def _():
            dg_ref[...] = jnp.zeros_like(dg_ref)

        @pl.when((i == 0) | (i == cfg.nxt))
        def _():
            dm_ref[...] = jnp.zeros_like(dm_ref)

        do = do_ref[...]
        g3 = g_ref[3:4]
        ohat, rinv = _rms_hat(opre_ref[...])
        dm_ref[0] += _rsum(do * (ohat * g3))
        dr = mods_ref[0][5:6] * do
        dg_ref[...] += _rsum(dr * ohat)
        dob = _rms_bwd(dr * g3, ohat, rinv).astype(bf16)
        dop_ref[...] = dob
        dmix_ref[...] = _nt(dob, w_ref[...]).astype(bf16)

    rt = lambda c: pl.BlockSpec((TM, c), lambda i: (i, 0))
    return pl.pallas_call(
        body, name=name, grid=(nt,),
        in_specs=[rt(D), rt(D), pl.BlockSpec((MIX_WIDTH, D), lambda i: (0, 0)),
                  pl.BlockSpec((1, N_MOD, D), _typ(cfg)), pl.BlockSpec((6, D), lambda i: (0, 0))],
        out_specs=[rt(D), rt(MIX_WIDTH), pl.BlockSpec((1, 1, D), _typ(cfg)), pl.BlockSpec((1, D), lambda i: (0, 0))],
        out_shape=[S_((R, D), bf16), S_((R, MIX_WIDTH), bf16), S_((ntyp, 1, D), f32), S_((1, D), f32)],
        compiler_params=_cp(VMEM_MID, ("arbitrary",)),
    )(dout, opre, w_out, mods, gvec)


def _modvec_fwd(cvecs, w_mod, b_shard, name):
    nl, D, n = w_mod.shape
    tn = n // 3 if (n % 3 == 0 and (n // 3) % LANES == 0) else n

    def body(c_ref, w_ref, b_ref, o_ref, s_ref):
        cv = c_ref[...]
        sv = cv * _sigmoid(cv)
        s_ref[...] = sv
        o_ref[...] = _nn(sv.astype(bf16), w_ref[...].astype(bf16)) + b_ref[...]

    return pl.pallas_call(
        body, name=name, grid=(nl, n // tn),
        in_specs=[pl.BlockSpec((16, D), lambda l, j: (0, 0)), pl.BlockSpec((None, D, tn), lambda l, j: (l, 0, j)),
                  pl.BlockSpec((None, 1, tn), lambda l, j: (l, 0, j))],
        out_specs=[pl.BlockSpec((None, 16, tn), lambda l, j: (l, 0, j)), pl.BlockSpec((16, D), lambda l, j: (0, 0))],
        out_shape=[S_((nl, 16, n), f32), S_((16, D), f32)],
        compiler_params=_cp(VMEM_MID, ("arbitrary", "arbitrary")),
    )(cvecs, w_mod, b_shard)


def _modvec_bwd(s_t, dm, w_mod, name):
    nl, D, n = w_mod.shape
    tn = n // 3 if (n % 3 == 0 and (n // 3) % LANES == 0) else n

    def body(s_ref, dm_ref, w_ref, gw_ref, gc_ref):
        @pl.when(pl.program_id(1) == 0)
        def _():
            gc_ref[...] = jnp.zeros_like(gc_ref)
        dmv = dm_ref[...]
        gw_ref[...] = jnp.dot(s_ref[...], dmv, preferred_element_type=f32, precision=lax.Precision.HIGHEST)
        gc_ref[...] += _nt(dmv[8:16].astype(bf16), w_ref[...].astype(bf16))

    return pl.pallas_call(
        body, name=name, grid=(nl, n // tn),
        in_specs=[pl.BlockSpec((D, 16), lambda l, j: (0, 0)), pl.BlockSpec((None, 16, tn), lambda l, j: (l, 0, j)),
                  pl.BlockSpec((None, D, tn), lambda l, j: (l, 0, j))],
        out_specs=[pl.BlockSpec((None, D, tn), lambda l, j: (l, 0, j)), pl.BlockSpec((None, 8, D), lambda l, j: (l, 0, 0))],
        out_shape=[S_((nl, D, n), f32), S_((nl, 8, D), f32)],
        compiler_params=_cp(VMEM_MID, ("arbitrary", "arbitrary")),
    )(s_t, dm, w_mod)


def _as2d(a):
    n = a.size
    if a.ndim >= 2 and a.shape[-1] % LANES == 0:
        return a.reshape(-1, a.shape[-1])
    if n % LANES == 0:
        return a.reshape(-1, LANES)
    return a.reshape(-1, a.shape[-1]) if a.ndim >= 2 else a.reshape(1, n)


def _row_tile(r, c, budget_elems):
    if r * c <= budget_elems or r % 8 != 0:
        return r
    t = r
    while t * c > budget_elems and t % 16 == 0:
        t //= 2
    return t


def _div_tile(r, c, budget_elems, mult=16):
    best = None
    for t in range(mult, r + 1, mult):
        if r % t == 0 and t * c <= budget_elems:
            best = t
    return best if best is not None else r


def _chip_index():
    return 2 * lax.axis_index("x") + lax.axis_index("y")


def _cast_into_place(shards, lead, axis, name):
    r, c = shards.shape[-2:]
    tr = _div_tile(r, c, 3 * ELEMWISE_BLOCK)
    nr = r // tr
    out_map = (lambda i: (i, _chip_index())) if axis == 1 else (lambda i: (_chip_index() * nr + i, 0))
    full2 = (r, c * N_CHIPS) if axis == 1 else (r * N_CHIPS, c)

    def body(a_ref, o_ref):
        o_ref[...] = a_ref[...].astype(bf16)

    return pl.pallas_call(
        body, name=name, grid=(nr,),
        in_specs=[pl.BlockSpec((None,) * len(lead) + (tr, c), lambda i: tuple(lead) + (i, 0))],
        out_specs=pl.BlockSpec((tr, c), out_map),
        out_shape=S_(full2, bf16), compiler_params=_cp(VMEM_MID, ("arbitrary",)),
    )(shards)


def _sum_devices8(own, land, axis, into, lead, name):
    _, rh, cs = land.shape
    tr = _div_tile(rh, cs, 2 * ELEMWISE_BLOCK)
    nr = rh // tr
    core = lambda: lax.axis_index("c")
    if axis == 1:
        own_map = lambda i: (core() * nr + i, _chip_index())
    else:
        own_map = lambda i: (_chip_index() * 2 * nr + core() * nr + i, 0)
    nl = len(lead)

    def land_spec(j):
        return pl.BlockSpec((None, tr, cs), lambda i: ((2 * _chip_index() + core() + j) % N_DEV, i, 0))

    def body(own_ref, *rest):
        acc = own_ref[...]
        for p_ref in rest[:N_DEV - 1]:
            acc = acc + p_ref[...].astype(f32)
        rest[-1][...] = acc

    return pl.pallas_call(
        body, name=name, grid=(nr,),
        in_specs=[pl.BlockSpec((tr, cs), own_map)] + [land_spec(j) for j in range(1, N_DEV)] + [ANY],
        out_specs=pl.BlockSpec((None,) * nl + (tr, cs), lambda i: tuple(lead) + (core() * nr + i, 0)),
        out_shape=S_(into.shape, f32), input_output_aliases={N_DEV: 0},
        compiler_params=_cp(VMEM_MID, ("arbitrary",)),
    )(own, *([land] * (N_DEV - 1)), into)


def _adamw(w, g, m, v, name, emit_grad=False, comm=None):
    shape = w.shape
    w2, g2, m2, v2 = _as2d(w), _as2d(g), _as2d(m), _as2d(v)
    r, c = w2.shape
    tr = _row_tile(r, c, ELEMWISE_BLOCK)
    c1 = 1.0 - ADAM_B1 ** ADAM_STEP
    c2 = 1.0 - ADAM_B2 ** ADAM_STEP
    n_out = 4 if emit_grad else 3

    def body(w_ref, g_ref, m_ref, v_ref, d_ref, mo_ref, vo_ref, *go_ref):
        gv = g_ref[...]
        mn = ADAM_B1 * m_ref[...] + (1.0 - ADAM_B1) * gv
        vn = ADAM_B2 * v_ref[...] + (1.0 - ADAM_B2) * (gv * gv)
        mo_ref[...] = mn
        vo_ref[...] = vn
        d_ref[...] = -ADAM_LR * ((mn / c1) / (jnp.sqrt(vn / c2) + ADAM_EPS) + ADAM_WD * w_ref[...])
        if emit_grad:
            go_ref[0][...] = gv

    spec = pl.BlockSpec((tr, c), lambda i: (i, 0))
    outs, res = _call(body, (w2, g2, m2, v2), comm, name=name, grid=(r // tr,), in_specs=[spec] * 4, out_specs=[spec] * n_out,
                      out_shape=[S_((r, c), f32)] * n_out, compiler_params=_cp(VMEM_MID, ("arbitrary",)))
    outs = tuple(o.reshape(shape) for o in outs)
    return outs if comm is None else (outs, res)


def _sum_devices(gathered, name):
    _, r, c = gathered.shape

    def body(a_ref, o_ref):
        acc = a_ref[0]
        for j in range(1, N_DEV):
            acc = acc + a_ref[j]
        o_ref[...] = acc

    tr = _row_tile(r, c, ELEMWISE_BLOCK // 4)
    return pl.pallas_call(
        body, name=name, grid=(r // tr,),
        in_specs=[pl.BlockSpec((N_DEV, tr, c), lambda i: (0, i, 0))], out_specs=pl.BlockSpec((tr, c), lambda i: (i, 0)),
        out_shape=S_((r, c), f32), compiler_params=_cp(VMEM_MID, ("arbitrary",)))(gathered)


def _all_gather_small(block, name):
    m_per, n = block.shape

    def body(x_ref, out_ref, send_sems, recv_sems, local_sem):
        x, y, c = _mesh_pos()
        me, sibling = (x, y, c), (x, y, 1 - c)
        chips = [(1 - x, y), (x, 1 - y), (1 - x, 1 - y)]

        def rows(px, py, pc):
            return out_ref.at[pl.ds((4 * px + 2 * py + pc) * m_per, m_per), :]

        def copy(k, blk, to, src=None):
            return pltpu.make_async_remote_copy(
                src_ref=rows(*blk) if src is None else src, dst_ref=rows(*blk),
                send_sem=send_sems.at[k], recv_sem=recv_sems.at[k], device_id=to, device_id_type=MESH)

        mine = pltpu.make_async_copy(x_ref, rows(*me), local_sem)
        mine.start()
        first = [copy(0, me, sibling, src=x_ref)]
        first += [copy(1 + j, me, (*chip, c), src=x_ref) for j, chip in enumerate(chips)]
        for cp in first:
            cp.start()
        passed = [copy(4 + j, (*chip, c), sibling) for j, chip in enumerate(chips)]
        for j, chip in enumerate(chips):
            copy(1 + j, (*chip, c), me).wait_recv()
            passed[j].start()
        copy(0, sibling, me).wait_recv()
        for j, chip in enumerate(chips):
            copy(4 + j, (*chip, 1 - c), me).wait_recv()
        for cp in first + passed:
            cp.wait_send()
        mine.wait()

    return pl.pallas_call(
        body, name=name, out_shape=S_((N_DEV * m_per, n), block.dtype),
        in_specs=[pl.BlockSpec(memory_space=pltpu.VMEM)], out_specs=pl.BlockSpec(memory_space=pltpu.VMEM),
        scratch_shapes=[pltpu.SemaphoreType.DMA((7,)), pltpu.SemaphoreType.DMA((7,)), pltpu.SemaphoreType.DMA],
        compiler_params=_cp(VMEM_MID),
    )(block)


def _pack_rows(arrays):
    flat = jnp.concatenate([a.reshape(-1) for a in arrays])
    pad = (-flat.size) % (8 * LANES)
    return jnp.pad(flat, (0, pad)).reshape(-1, LANES)


def _unpack_rows(packed, shapes):
    flat = packed.reshape(-1)
    out, off = [], 0
    for s in shapes:
        n = int(np.prod(s))
        out.append(flat[off:off + n].reshape(s))
        off += n
    return out


W_AXIS = {"gu": 1, "dn": 0, "wi": 1, "wo": 0}
SMALL_NAMES = ("dmods", "dg", "drpb", "dwp", "dps")


def _half_merge(bufs, name):
    nt = len(bufs)

    def body(*refs):
        outs = refs[nt:2 * nt]
        send_sems, recv_sems = refs[2 * nt:]
        x, y, c = _mesh_pos()

        def half(ref, h):
            rh = ref.shape[-2] // 2
            return ref.at[(slice(None),) * (len(ref.shape) - 2) + (pl.ds(h * rh, rh), slice(None))]

        cps = []
        for t in range(nt):
            cp = pltpu.make_async_remote_copy(
                src_ref=half(outs[t], c), dst_ref=half(outs[t], c), send_sem=send_sems.at[t], recv_sem=recv_sems.at[t],
                device_id=(x, y, 1 - c), device_id_type=MESH)
            cp.start()
            cps.append(cp)
        for t in range(nt):
            pltpu.make_async_remote_copy(
                src_ref=half(outs[t], 1 - c), dst_ref=half(outs[t], 1 - c), send_sem=send_sems.at[t], recv_sem=recv_sems.at[t],
                device_id=(x, y, 1 - c), device_id_type=MESH).wait_recv()
        for cp in cps:
            cp.wait_send()

    return pl.pallas_call(
        body, name=name, in_specs=[ANY] * nt, out_specs=[ANY] * nt, out_shape=[S_(b.shape, f32) for b in bufs],
        input_output_aliases={t: t for t in range(nt)},
        scratch_shapes=[pltpu.SemaphoreType.DMA((nt,)), pltpu.SemaphoreType.DMA((nt,))],
        compiler_params=_cp(VMEM_MID),
    )(*bufs)


def _local_step(cfg, x_lat, x_ctx, target, mods, norm_g, W, G, na_rpb, w_pool, pool_scale):
    S, L, T, D, F = cfg.S, cfg.L, cfg.T, cfg.D, cfg.F
    depth = norm_g.shape[0]
    cos, sin = _rope_tables(S, L)
    band, inv = _pool_tables(cfg.TM, L)
    flip, sel = _rpb_reduce_tables()

    assert depth == 2, "the carrier schedules below are written for two layers"
    fwd_carry = {"ffn_fwd_0_0": [("wi", 0), ("wo", 0), ("gu", 0, 1), ("dn", 0, 1)],
                 "na_fwd_0": [("gu", 1, 0), ("dn", 1, 0)],
                 "ffn_fwd_0_1": [("wi", 1), ("wo", 1), ("gu", 1, 1), ("dn", 1, 1)]}
    bwd_carry = {"na_bwd_1": [("gu", 1, 1), ("dn", 1, 1)], "ffn_bwd_1_0": [("wi", 1), ("wo", 1)],
                 "ffn_bwd_0_1": [("gu", 1, 0), ("dn", 1, 0)], "na_bwd_0": [("gu", 0, 1), ("dn", 0, 1)],
                 "ffn_bwd_0_0": [("wi", 0), ("wo", 0)], "wgrad_dn_0_0": [("gu", 0, 0)]}
    last_scatter = [("dn", 0, 0)]
    tag = lambda key: "_".join(str(p) for p in key)
    g_f32, g_b16 = {}, {}

    def gather_on(name):
        keys = fwd_carry.get(name)
        return None if keys is None else _gather_comm([W[k_] for k_ in keys], [W_AXIS[k_[0]] for k_ in keys])

    def gathered(name, res):
        if name in fwd_carry:
            W.update(zip(fwd_carry[name], res))

    def scatter_on(name):
        keys = bwd_carry.get(name)
        return None if keys is None else _scatter_comm([g_b16[k_] for k_ in keys], [W_AXIS[k_[0]] for k_ in keys])

    def scattered(keys, lands):
        for key, land in zip(keys, lands):
            G[key[0]] = _sum_devices8(g_f32[key], land, W_AXIS[key[0]], G[key[0]], key[1:], f"sum8_{tag(key)}")

    small_landed = []

    def wgrad(key, a, b, rows, other_comm=None):
        name = f"wgrad_{tag(key)}"
        if other_comm is not None:
            assert name not in bwd_carry
            (g_f32[key], g_b16[key]), res = _wgrad(a, b, rows, name, other_comm)
            small_landed.extend(res)
            return
        (g_f32[key], g_b16[key]), lands = _wgrad(a, b, rows, name, scatter_on(name))
        scattered(bwd_carry.get(name, ()), lands)

    saved = []
    xs, xs_ctx = x_lat, x_ctx
    for l in range(depth):
        last = l == depth - 1
        wc = not last
        gvec = norm_g[l]
        ps = pool_scale[l].reshape(1, POOL_WIDTH)
        bexp = _expand_rpb(na_rpb[l], f"bias_expand_{l}")
        name = f"ffn_fwd_{l}_0"
        (xs1, hb1, z1, y1), res = _ffn_fwd(cfg, xs, mods[l], gvec, W["gu", l, 0], W["dn", l, 0], 0, 0, True, name,
                                           gather_on(name), xs_ctx=xs_ctx)
        gathered(name, res)
        hb2, q, k, v, u = _tmpre_fwd(cfg, xs1, mods[l], gvec, W["wi", l], cos, sin, f"tmpre_fwd_{l}")
        name = f"na_fwd_{l}"
        (na_x,), res = _na_fwd(cfg, q, k, v, bexp, name, gather_on(name))
        gathered(name, res)
        na_c = _ctx_attn_fwd(cfg, q, k, v, f"ctx_attn_fwd_{l}") if wc else None
        xs2, opre, mix = _tmpost_fwd(cfg, na_x, na_c, u, band, inv, w_pool[l], ps, W["wo", l], xs1, mods[l], gvec,
                                     f"tmpost_fwd_{l}")
        name = f"ffn_fwd_{l}_1"
        outs, res = _ffn_fwd(cfg, xs2, mods[l], gvec, W["gu", l, 1], W["dn", l, 1], 6, 4, wc, name, gather_on(name),
                             loss_target=target if last else None)
        xs3, hb3, z3, y3 = outs[:4]
        gathered(name, res)
        saved.append(dict(xs=xs, xs_ctx=xs_ctx, xs1=xs1, xs2=xs2, hb1=hb1, z1=z1, y1=y1, hb2=hb2, q=q, k=k, v=v, u=u, mix=mix,
                          opre=opre, hb3=hb3, z3=z3, y3=y3, bexp=bexp, ps=ps, gvec=gvec))
        xs, xs_ctx = xs3, None

    dxs, loss_blk = xs, outs[4]

    small = [None] * depth
    for l in reversed(range(depth)):
        last = l == depth - 1
        wc = not last
        sv = saved[l]
        gvec = sv["gvec"]
        rows_b = cfg.T if wc else cfg.S
        name = f"ffn_bwd_{l}_1"
        (dxs2, dz, dyb, ab, dm678, dg45), lands = _ffn_bwd(cfg, dxs, sv["xs2"], sv["z3"], sv["y3"], mods[l], gvec,
                                                           W["gu", l, 1], W["dn", l, 1], 6, 4, wc, name, scatter_on(name))
        scattered(bwd_carry.get(name, ()), lands)
        wgrad(("gu", l, 1), sv["hb3"], dz, rows_b)
        wgrad(("dn", l, 1), ab, dyb, rows_b)
        dop, dmix, dm5, dg3 = _tmpost_bwd(cfg, dxs2, sv["opre"], W["wo", l], mods[l], gvec, wc, f"tmpost_bwd_{l}")
        wgrad(("wo", l), sv["mix"], dop, rows_b)
        du, dwp, dps = _pool_bwd(cfg, dmix, sv["u"], band, inv, w_pool[l], sv["ps"], wc, f"pool_bwd_{l}")
        name = f"na_bwd_{l}"
        (dq, dk, dv, dkc, dvc, dbexp), lands = _na_bwd(cfg, dmix, sv["q"], sv["k"], sv["v"], sv["bexp"], name, scatter_on(name))
        scattered(bwd_carry.get(name, ()), lands)
        drpb = _rpb_reduce(dbexp, flip, sel, f"rpb_reduce_{l}")
        if wc:
            dqc, dkc2, dvc2 = _ctx_attn_bwd(cfg, dmix, sv["q"], sv["k"], sv["v"], f"ctx_attn_bwd_{l}")
            ctx_terms = ([dqc], [dkc, dkc2], [dvc, dvc2])
        else:
            ctx_terms = ([], [dkc], [dvc])
        dxs1, dproj, dm34, dg2 = _tmpre_bwd(cfg, (dq, dk, dv, du), ctx_terms, wc, cos, sin, W["wi", l], sv["xs1"], mods[l], gvec,
                                            dxs2, wc, f"tmpre_bwd_{l}")
        wgrad(("wi", l), sv["hb2"], dproj, cfg.T)
        name = f"ffn_bwd_{l}_0"
        (dxs, dz, dyb, ab, dm012, dg01), lands = _ffn_bwd(cfg, dxs1, sv["xs"], sv["z1"], sv["y1"], mods[l], gvec,
                                                          W["gu", l, 0], W["dn", l, 0], 0, 0, True, name, scatter_on(name),
                                                          xs_ctx=sv["xs_ctx"])
        scattered(bwd_carry.get(name, ()), lands)
        if not wc:
            zero = lambda a: jnp.concatenate([a, jnp.zeros_like(a)], axis=0)
            dm5, dm678 = zero(dm5), zero(dm678)
        dmods = jnp.concatenate([dm012, dm34, dm5, dm678], axis=1)
        dgs = jnp.concatenate([dg01, dg2, dg3, dg45], axis=0)
        small[l] = dict(dmods=dmods, dg=dgs, drpb=drpb, dwp=dwp, dps=dps)
        small_gather = None
        if l == 0:
            parts = [jnp.stack([small[j][n_] for j in range(depth)]) for n_ in SMALL_NAMES]
            packed = _pack_rows(parts)
            small_gather = _allgather_comm(packed)
        wgrad(("gu", l, 0), sv["hb1"], dz, cfg.T, small_gather)
        wgrad(("dn", l, 0), ab, dyb, cfg.T)
    last_comm = _scatter_comm([g_b16[k_] for k_ in last_scatter], [W_AXIS[k_[0]] for k_ in last_scatter])

    def finish_weight_grads(lands):
        scattered(last_scatter, lands)
        kinds = ("gu", "dn", "wi", "wo")
        return dict(zip(kinds, _half_merge([G[k_] for k_ in kinds], "merge_halves")))

    return loss_blk, dxs, (last_comm, finish_weight_grads), (packed, [p.shape for p in parts], small_landed[0])


def kernel(x, c, ctx, c_ctx, w_mod, b_mod, norm_g, w_ffn_gate_up, w_ffn_down, w_in, w_out, na_rpb, w_pool, pool_scale, loss_target, m_c_ctx, m_w_mod, m_b_mod, m_norm_g, m_w_ffn_gate_up, m_w_ffn_down, m_w_in, m_w_out, m_na_rpb, m_w_pool, m_pool_scale, v_c_ctx, v_w_mod, v_b_mod, v_norm_g, v_w_ffn_gate_up, v_w_ffn_down, v_w_in, v_w_out, v_na_rpb, v_w_pool, v_pool_scale):
    S, D = x.shape[1], x.shape[2]
    L = ctx.shape[1]
    depth = w_mod.shape[0]
    F = w_ffn_down.shape[2] * N_CHIPS
    nmod = w_mod.shape[2]
    gsh = norm_g.shape[2]
    cfg = _Cfg(S, L, D, F)
    mx, my, mc = _mesh_pos()
    chip = 2 * mx + my
    dev = 4 * mx + 2 * my + mc

    W = {}
    for l in range(depth):
        for i in range(2):
            W["gu", l, i] = _cast_into_place(w_ffn_gate_up, (l, i), W_AXIS["gu"], f"cast_gu_{l}_{i}")
            W["dn", l, i] = _cast_into_place(w_ffn_down, (l, i), W_AXIS["dn"], f"cast_dn_{l}_{i}")
        W["wi", l] = _cast_into_place(w_in, (l,), W_AXIS["wi"], f"cast_wi_{l}")
        W["wo", l] = _cast_into_place(w_out, (l,), W_AXIS["wo"], f"cast_wo_{l}")
    first = [("gu", 0, 0), ("dn", 0, 0)]
    W.update(zip(first, _comm_only(_gather_comm([W[k_] for k_ in first], [W_AXIS[k_[0]] for k_ in first]), "gather_first")))
    G = {"gu": lax.empty(w_ffn_gate_up.shape, f32), "dn": lax.empty(w_ffn_down.shape, f32),
         "wi": lax.empty(w_in.shape, f32), "wo": lax.empty(w_out.shape, f32)}

    cg_packed = _pack_rows([c, norm_g])
    cg_all = _all_gather_small(cg_packed, "gather_c_norm_g").reshape(N_DEV, -1)
    c_all = cg_all[:, :D]
    ng = cg_all[:, D:D + norm_g.size].reshape(N_DEV, depth, 6, gsh)
    norm_g_all = jnp.concatenate([ng[2 * j] for j in range(N_CHIPS)], axis=-1)
    cvecs = jnp.concatenate([c_all, c_ctx[None], jnp.zeros((7, D), f32)], axis=0)
    b_shard = lax.dynamic_slice_in_dim(b_mod, chip * nmod, nmod, axis=1).reshape(depth, 1, nmod)
    m_part, silu_c = _modvec_fwd(cvecs, w_mod, b_shard, "modvec_fwd")
    m_all = _all_gather_small(m_part.reshape(depth * 16, nmod), "gather_mod").reshape(N_DEV, depth, 16, nmod)
    m_full = jnp.concatenate([m_all[2 * j] for j in range(N_CHIPS)], axis=-1)
    m_mine = lax.dynamic_index_in_dim(m_full, dev, axis=1, keepdims=False)
    mods = jnp.stack([m_mine, m_full[:, 8]], axis=1).reshape(depth, 2, N_MOD, D)

    loss_blk, dx_lat, (last_comm, finish_weight_grads), small = _local_step(
        cfg, x[0], ctx[0], loss_target[0], mods, norm_g_all, W, G, na_rpb, w_pool, pool_scale)
    loss = lax.psum(loss_blk[0, 0], ("x", "y", "c"))
    grad_x = dx_lat[None]

    packed, shapes, landed = small
    gathered = lax.dynamic_update_index_in_dim(landed, packed, dev, 0)
    total = _unpack_rows(_sum_devices(gathered, "sum_small"), shapes)
    dmods_sum, dg_sum, drpb_sum, dwp_sum, dps_sum = total
    dmods_each = jnp.stack([_unpack_rows(gathered[j], shapes[:1])[0] for j in range(N_DEV)])
    dm_rows = jnp.concatenate([jnp.transpose(dmods_each[:, :, 0], (1, 0, 2, 3)).reshape(depth, N_DEV, N_MOD * D),
                               dmods_sum[:, 1].reshape(depth, 1, N_MOD * D),
                               jnp.zeros((depth, 7, N_MOD * D), f32)], axis=1)
    dm_shard = lax.dynamic_slice_in_dim(dm_rows, chip * nmod, nmod, axis=2)
    grad_w_mod, gc_part = _modvec_bwd(silu_c.T, dm_shard, w_mod, "modvec_bwd")
    gc_all = _all_gather_small(gc_part.reshape(depth * 8, D), "gather_gc").reshape(N_DEV, depth, 8, D)
    grad_b_mod, grad_c_ctx = _small_finish(dm_rows, gc_all, c_ctx)
    grad_norm_g = lax.dynamic_slice_in_dim(dg_sum, chip * gsh, gsh, axis=2)
    grad_na_rpb = drpb_sum[:, :, :2 * NA_KH - 1, :2 * NA_KW - 1]
    grad_w_pool = dwp_sum
    grad_pool_scale = dps_sum.reshape(depth, POOL_WIDTH)

    upd_w_mod, lands = _adamw(w_mod, grad_w_mod, m_w_mod, v_w_mod, "adamw_w_mod", comm=last_comm)
    wgrads = finish_weight_grads(lands)
    g_gu, g_dn, g_wi, g_wo = wgrads["gu"], wgrads["dn"], wgrads["wi"], wgrads["wo"]
    grads = [grad_c_ctx, grad_w_mod, grad_b_mod, grad_norm_g, g_gu, g_dn, g_wi, g_wo, grad_na_rpb, grad_w_pool, grad_pool_scale]
    ws = [c_ctx, w_mod, b_mod, norm_g, w_ffn_gate_up, w_ffn_down, w_in, w_out, na_rpb, w_pool, pool_scale]
    ms = [m_c_ctx, m_w_mod, m_b_mod, m_norm_g, m_w_ffn_gate_up, m_w_ffn_down, m_w_in, m_w_out, m_na_rpb, m_w_pool, m_pool_scale]
    vs = [v_c_ctx, v_w_mod, v_b_mod, v_norm_g, v_w_ffn_gate_up, v_w_ffn_down, v_w_in, v_w_out, v_na_rpb, v_w_pool, v_pool_scale]
    tags = ["c_ctx", "w_mod", "b_mod", "norm_g", "gate_up", "down", "w_in", "w_out", "na_rpb", "w_pool", "pool_scale"]
    merged = ("gate_up", "down", "w_in", "w_out")
    upd = [upd_w_mod if t == "w_mod" else _adamw(w_, g_, m_, v_, f"adamw_{t}", emit_grad=t in merged)
           for w_, g_, m_, v_, t in zip(ws, grads, ms, vs, tags)]
    grads = [u_[3] if t in merged else g_ for g_, u_, t in zip(grads, upd, tags)]
    return (loss, grad_x, *grads, *[u_[0] for u_ in upd], *[u_[1] for u_ in upd], *[u_[2] for u_ in upd])


def _small_finish(dm_rows, gc_all, c_ctx):
    depth, _, n = dm_rows.shape
    D = c_ctx.shape[0]

    def body(dm_ref, gc_ref, c_ref, gb_ref, gcx_ref):
        acc = dm_ref[:, 0]
        for j in range(1, N_DEV + 1):
            acc = acc + dm_ref[:, j]
        gb_ref[...] = acc
        t = jnp.zeros((1, D), f32)
        for l in range(depth):
            for j in range(N_CHIPS):
                t = t + gc_ref[2 * j, l, 0:1, :]
        cv = c_ref[...]
        sg = _sigmoid(cv)
        gcx_ref[...] = t * (sg * (1.0 + cv * (1.0 - sg)))

    gb, gcx = pl.pallas_call(
        body, name="small_finish",
        out_shape=[S_((depth, n), f32), S_((1, D), f32)],
        compiler_params=_cp(VMEM_MID),
    )(dm_rows, gc_all, c_ctx.reshape(1, D))
    return gb, gcx.reshape(D)
```

```python
import functools

import numpy as np
import jax
import jax.numpy as jnp
from jax import lax
from jax.experimental import pallas as pl
from jax.experimental.pallas import tpu as pltpu

f32, bf16 = jnp.float32, jnp.bfloat16

GRID_W = 64
N_MOD = 9
NA_HEADS = 8
HEAD_DIM = 64
NA_WIDTH = NA_HEADS * HEAD_DIM
NA_KH = 8
NA_KW = 16
POOL_GROUPS = 4
POOL_CH = 128
POOL_WIDTH = POOL_GROUPS * POOL_CH
POOL_WINDOWS = (2, 4, 8, 16)
IN_WIDTH = 3 * NA_WIDTH + POOL_WIDTH
MIX_WIDTH = NA_WIDTH + POOL_WIDTH
ROPE_THETA = 10000.0
ROPE_PAIRS = HEAD_DIM // 4
RMS_EPS = 1e-6
NEG_INF = -1e30
ADAM_LR, ADAM_B1, ADAM_B2, ADAM_EPS, ADAM_WD, ADAM_STEP = 0.001, 0.9, 0.999, 1e-08, 0.01, 10

N_DEV = 8
N_CHIPS = 4
LANES = 128
MIB = 1024 * 1024
VMEM_BIG = 52 * MIB
VMEM_MID = 40 * MIB
WGRAD_TN = 1408
WGRAD_TK = 2816
WGRAD_SLACK = 6 * MIB
ELEMWISE_BLOCK = 256 * 1024
MESH = pl.DeviceIdType.MESH
ANY = pl.BlockSpec(memory_space=pl.ANY)
S_ = jax.ShapeDtypeStruct


def _cp(vmem=VMEM_MID, sem=None):
    return pltpu.CompilerParams(vmem_limit_bytes=vmem, dimension_semantics=sem)


def _sigmoid(x):
    return 0.5 * jnp.tanh(0.5 * x) + 0.5


def _rms_hat(x):
    rinv = lax.rsqrt(jnp.mean(x * x, axis=-1, keepdims=True) + RMS_EPS)
    return x * rinv, rinv


def _rms_bwd(dxhat, xhat, rinv):
    return rinv * (dxhat - xhat * jnp.mean(dxhat * xhat, axis=-1, keepdims=True))


def _rsum(a):
    return jnp.sum(a, axis=0, keepdims=True)


def _nt(a, b):
    return lax.dot_general(a, b, (((1,), (1,)), ((), ())), preferred_element_type=f32)


def _tn(a, b):
    return lax.dot_general(a, b, (((0,), (0,)), ((), ())), preferred_element_type=f32)


def _nn(a, b):
    return jnp.dot(a, b, preferred_element_type=f32)


def _swap16(x):
    lane = lax.broadcasted_iota(jnp.int32, x.shape, 1)
    n = x.shape[1]
    return jnp.where((lane % 32) < 16, pltpu.roll(x, n - 16, 1), pltpu.roll(x, 16, 1))


def _rope_tables(s_len, l_len):
    t = np.arange(s_len)
    inv = ROPE_THETA ** (-np.arange(ROPE_PAIRS, dtype=np.float32) / ROPE_PAIRS)
    ang_r = (t // GRID_W).astype(np.float32)[:, None] * inv
    ang_c = (t % GRID_W).astype(np.float32)[:, None] * inv
    cos = np.concatenate([np.cos(ang_r), np.cos(ang_r), np.cos(ang_c), np.cos(ang_c)], axis=-1)
    sin = np.concatenate([-np.sin(ang_r), np.sin(ang_r), -np.sin(ang_c), np.sin(ang_c)], axis=-1)
    cos = np.concatenate([cos, np.ones((l_len, HEAD_DIM), np.float32)], axis=0)
    sin = np.concatenate([sin, np.zeros((l_len, HEAD_DIM), np.float32)], axis=0)
    return (jnp.asarray(np.tile(cos, (1, 2)), f32), jnp.asarray(np.tile(sin, (1, 2)), f32))


def _pool_tables(tm, l_len):
    band = np.zeros((2, POOL_GROUPS, tm, tm), np.float32)
    inv = np.zeros((2, POOL_GROUPS, tm, 1), np.float32)
    for typ, length in ((0, GRID_W), (1, l_len)):
        for g, w in enumerate(POOL_WINDOWS):
            for t in range(tm):
                base, p = (t // length) * length, t % length
                lo = min(max(p - w // 2, 0), length)
                hi = min(max(p - w // 2 + w, 0), length)
                band[typ, g, t, base + lo:base + hi] = 1.0
                inv[typ, g, t, 0] = 1.0 / (hi - lo)
    return jnp.asarray(band, bf16), jnp.asarray(inv, f32)


NA_QR = 4
NA_WR = NA_KH + NA_QR - 1
NA_TYPES = 3
NA_SEL_ROWS = 136
NA_WPAD = 768


def _rpb_index_tables():
    j = np.arange(GRID_W)
    col_start = np.clip(j - NA_KW // 2, 0, GRID_W - NA_KW)
    valid = (j[None, :] >= col_start[:, None]) & (j[None, :] < col_start[:, None] + NA_KW)
    dc = np.clip(j[None, :] - j[:, None] + NA_KW - 1, 0, 2 * NA_KW - 2)
    i = np.arange(NA_QR)[:, None]
    kk = np.arange(NA_WR)[None, :]
    off = np.stack([np.zeros_like(i), i, np.full_like(i, NA_QR - 1)])
    d = np.stack([kk - i + NA_KH - 1, kk - i + NA_KH - 1 - NA_QR, kk - i])
    row_ok = (kk[None] >= off) & (kk[None] < off + NA_KH)
    assert (d[row_ok] >= 0).all() and (d[row_ok] <= 2 * NA_KH - 2).all()
    return valid, dc, d, row_ok


def _expand_rpb(rpb, name):
    _, _, d, row_ok = _rpb_index_tables()
    heads, nd, ne = rpb.shape
    w = GRID_W
    v = jnp.pad(rpb, ((0, 0), (0, 0), (w - NA_KW, 2 * w - (w - NA_KW) - ne)))
    x = jnp.broadcast_to(v[:, :, None, :], (heads, nd, w, 2 * w)).reshape(heads, nd, 2 * w * w)
    t = x[:, :, :w * (2 * w - 1)].reshape(heads, nd, w, 2 * w - 1)[..., w - 1:]

    def body(t_ref, o_ref):
        q = lax.broadcasted_iota(jnp.int32, (w, w), 0)
        c = lax.broadcasted_iota(jnp.int32, (w, w), 1)
        c0 = jnp.clip(q - NA_KW // 2, 0, w - NA_KW)
        in_cols = (c >= c0) & (c < c0 + NA_KW)
        outside = jnp.full((w, w), NEG_INF, f32)
        blocks = [jnp.where(in_cols, t_ref[dd], NEG_INF) for dd in range(nd)]
        for typ in range(NA_TYPES):
            for i in range(NA_QR):
                row = [blocks[d[typ, i, kk]] if row_ok[typ, i, kk] else outside for kk in range(NA_WR)]
                o_ref[typ, i * w:(i + 1) * w, :] = jnp.concatenate(row, axis=1)

    return pl.pallas_call(
        body, name=name, grid=(heads,),
        in_specs=[pl.BlockSpec((None, nd, w, w), lambda h: (h, 0, 0, 0))],
        out_specs=pl.BlockSpec((NA_TYPES, None, NA_QR * w, NA_WR * w), lambda h: (0, h, 0, 0)),
        out_shape=S_((NA_TYPES, heads, NA_QR * w, NA_WR * w), f32),
        compiler_params=_cp(VMEM_MID, ("arbitrary",)),
    )(t)


def _rpb_reduce_tables():
    _, _, d, row_ok = _rpb_index_tables()
    flip = np.eye(GRID_W, dtype=np.float32)[::-1].copy()
    sel = np.zeros((16, NA_SEL_ROWS), np.float32)
    flat_d, flat_ok = d.reshape(-1), row_ok.reshape(-1)
    for n in range(flat_d.size):
        if flat_ok[n]:
            sel[flat_d[n], n] = 1.0
    return jnp.asarray(flip), jnp.asarray(sel)


class _Cfg:
    def __init__(self, s_len, l_len, d, f):
        self.S, self.L, self.D, self.F = s_len, l_len, d, f
        self.T = s_len + l_len
        self.TM = 256 if l_len % 256 == 0 else 128
        assert l_len == self.TM, "context length must equal the row tile"
        assert s_len % self.TM == 0 and s_len % GRID_W == 0
        self.nxt = s_len // self.TM
        self.ntt = self.T // self.TM
        self.rows = s_len // GRID_W
        assert self.rows >= 2 * NA_KH
        assert f % (2 * LANES) == 0

    def ntiles(self, with_ctx):
        return self.ntt if with_ctx else self.nxt


def _typ(cfg):
    return lambda i: (jnp.minimum(i // cfg.nxt, 1), 0, 0)


def _mesh_pos():
    return lax.axis_index("x"), lax.axis_index("y"), lax.axis_index("c")


class _Comm:
    def __init__(self, ins, outs, alias, nsem, start, finish):
        self.ins, self.outs, self.alias, self.nsem, self.start, self.finish = ins, outs, alias, nsem, start, finish


def _call(body, args, comm=None, *, grid, in_specs, out_specs, out_shape, scratch_shapes=(), **kw):
    if comm is None:
        return pl.pallas_call(body, grid=grid, in_specs=list(in_specs), out_specs=list(out_specs), out_shape=list(out_shape),
                              scratch_shapes=list(scratch_shapes), **kw)(*args), ()
    n_in, n_out, n_sc = len(in_specs), len(out_specs), len(scratch_shapes)
    ci, co = len(comm.ins), len(comm.outs)

    def carrier(*refs):
        bounds = np.cumsum([0, n_in, ci, n_out, co, n_sc])
        ins, cins, outs, couts, scr = (refs[a:b] for a, b in zip(bounds[:-1], bounds[1:]))
        send, recv = refs[bounds[-1]], refs[bounds[-1] + 1]
        first = functools.reduce(jnp.logical_and, [pl.program_id(a) == 0 for a in range(len(grid))])
        last = functools.reduce(jnp.logical_and, [pl.program_id(a) == g - 1 for a, g in enumerate(grid)])

        @pl.when(first)
        def _():
            comm.start(cins, couts, send, recv)

        body(*ins, *outs, *scr)

        @pl.when(last)
        def _():
            comm.finish(cins, couts, send, recv)

    res = pl.pallas_call(
        carrier, grid=grid, in_specs=list(in_specs) + [ANY] * ci, out_specs=list(out_specs) + [ANY] * co,
        out_shape=list(out_shape) + list(comm.outs),
        input_output_aliases={n_in + a: n_out + b for a, b in comm.alias.items()},
        scratch_shapes=list(scratch_shapes) + [pltpu.SemaphoreType.DMA((comm.nsem,)), pltpu.SemaphoreType.DMA((comm.nsem,))],
        **kw)(*args, *comm.ins)
    return res[:n_out], res[n_out:]


def _comm_only(comm, name):
    ci, co = len(comm.ins), len(comm.outs)

    def body(*refs):
        cins, couts = refs[:ci], refs[ci:ci + co]
        send, recv = refs[ci + co], refs[ci + co + 1]
        comm.start(cins, couts, send, recv)
        comm.finish(cins, couts, send, recv)

    return pl.pallas_call(
        body, name=name, in_specs=[ANY] * ci, out_specs=[ANY] * co, out_shape=list(comm.outs),
        input_output_aliases=dict(comm.alias),
        scratch_shapes=[pltpu.SemaphoreType.DMA((comm.nsem,)), pltpu.SemaphoreType.DMA((comm.nsem,))],
        compiler_params=_cp(VMEM_MID),
    )(*comm.ins)


def _half_view(ref, axis, kk, h):
    r, c = ref.shape
    if axis == 1:
        n = c // N_CHIPS
        return ref.at[pl.ds(h * (r // 2), r // 2), pl.ds(pl.multiple_of(kk * n, LANES), n)]
    n = r // N_CHIPS
    return ref.at[pl.ds(pl.multiple_of(kk * n + h * (n // 2), 8), n // 2), :]


def _other_chips(x, y):
    return [(1 - x, y), (x, 1 - y), (1 - x, 1 - y)]


def _gather_comm(arrs, axes):
    n = len(arrs)

    def copy(ref, view, sems, k, to):
        send, recv = sems
        return pltpu.make_async_remote_copy(src_ref=view, dst_ref=view, send_sem=send.at[k], recv_sem=recv.at[k],
                                            device_id=to, device_id_type=MESH)

    def start(cins, bufs, send, recv):
        x, y, c = _mesh_pos()
        for t in range(n):
            own = _half_view(bufs[t], axes[t], 2 * x + y, c)
            for j, chip in enumerate(_other_chips(x, y)):
                copy(bufs[t], own, (send, recv), 6 * t + j, (*chip, c)).start()

    def finish(cins, bufs, send, recv):
        x, y, c = _mesh_pos()
        sibling = (x, y, 1 - c)
        chips = _other_chips(x, y)
        for t in range(n):
            for j, chip in enumerate(chips):
                landed = _half_view(bufs[t], axes[t], 2 * chip[0] + chip[1], c)
                copy(bufs[t], landed, (send, recv), 6 * t + j, (*chip, c)).wait_recv()
                copy(bufs[t], landed, (send, recv), 6 * t + 3 + j, sibling).start()
        for t in range(n):
            own = _half_view(bufs[t], axes[t], 2 * x + y, c)
            for j, chip in enumerate(chips):
                kj = 2 * chip[0] + chip[1]
                copy(bufs[t], _half_view(bufs[t], axes[t], kj, 1 - c), (send, recv), 6 * t + 3 + j, sibling).wait_recv()
                copy(bufs[t], own, (send, recv), 6 * t + j, (*chip, c)).wait_send()
                copy(bufs[t], _half_view(bufs[t], axes[t], kj, c), (send, recv), 6 * t + 3 + j, sibling).wait_send()

    return _Comm(list(arrs), [S_(a.shape, a.dtype) for a in arrs], {t: t for t in range(n)}, 6 * n, start, finish)


def _scatter_comm(parts, axes):
    n = len(parts)
    peers = [(fx, fy, fc) for fx in (0, 1) for fy in (0, 1) for fc in (0, 1)][1:]

    def half_shape(a, axis):
        r, c = a.shape
        return (r // 2, c // N_CHIPS) if axis == 1 else (r // N_CHIPS // 2, c)

    def start(srcs, lands, send, recv):
        x, y, c = _mesh_pos()
        me = 4 * x + 2 * y + c
        for t in range(n):
            for r_, (fx, fy, fc) in enumerate(peers):
                dx, dy, dc = (1 - x if fx else x), (1 - y if fy else y), (1 - c if fc else c)
                pltpu.make_async_remote_copy(
                    src_ref=_half_view(srcs[t], axes[t], 2 * dx + dy, dc), dst_ref=lands[t].at[me],
                    send_sem=send.at[7 * t + r_], recv_sem=recv.at[7 * t + r_],
                    device_id=(dx, dy, dc), device_id_type=MESH).start()

    def finish(srcs, lands, send, recv):
        x, y, c = _mesh_pos()
        for t in range(n):
            mine = _half_view(srcs[t], axes[t], 2 * x + y, c)
            for r_, (fx, fy, fc) in enumerate(peers):
                sx, sy, sc = (1 - x if fx else x), (1 - y if fy else y), (1 - c if fc else c)
                cp = pltpu.make_async_remote_copy(
                    src_ref=mine, dst_ref=lands[t].at[4 * sx + 2 * sy + sc],
                    send_sem=send.at[7 * t + r_], recv_sem=recv.at[7 * t + r_],
                    device_id=(sx, sy, sc), device_id_type=MESH)
                cp.wait_recv()
                cp.wait_send()

    return _Comm(list(parts), [S_((N_DEV,) + half_shape(a, ax), a.dtype) for a, ax in zip(parts, axes)], {}, 7 * n, start, finish)


def _allgather_comm(block):
    peers = [(fx, fy, fc) for fx in (0, 1) for fy in (0, 1) for fc in (0, 1)][1:]

    def ends(x, y, c):
        for r_, (fx, fy, fc) in enumerate(peers):
            yield r_, ((1 - x if fx else x), (1 - y if fy else y), (1 - c if fc else c))

    def start(srcs, lands, send, recv):
        x, y, c = _mesh_pos()
        for r_, peer in ends(x, y, c):
            pltpu.make_async_remote_copy(src_ref=srcs[0], dst_ref=lands[0].at[4 * x + 2 * y + c], send_sem=send.at[r_],
                                         recv_sem=recv.at[r_], device_id=peer, device_id_type=MESH).start()

    def finish(srcs, lands, send, recv):
        x, y, c = _mesh_pos()
        for r_, (px, py, pc) in ends(x, y, c):
            cp = pltpu.make_async_remote_copy(src_ref=srcs[0], dst_ref=lands[0].at[4 * px + 2 * py + pc], send_sem=send.at[r_],
                                              recv_sem=recv.at[r_], device_id=(px, py, pc), device_id_type=MESH)
            cp.wait_recv()
            cp.wait_send()

    return _Comm([block], [S_((N_DEV,) + block.shape, block.dtype)], {}, len(peers), start, finish)


def _ffn_fwd(cfg, xs, mods, gvec, wgu, wd, mi, gi, with_ctx, name, comm=None, xs_ctx=None, loss_target=None):
    TM, D, F = cfg.TM, cfg.D, cfg.F
    nt = cfg.ntiles(with_ctx)
    R = nt * TM
    split, head = xs_ctx is not None, loss_target is not None

    def body(*refs):
        it = iter(refs)
        xs_ref = next(it)
        xc_ref = next(it) if split else None
        mods_ref, g_ref, wgu_hbm, wd_hbm = next(it), next(it), next(it), next(it)
        t_ref = next(it) if head else None
        out_ref, hb_ref, z_ref, y_ref = next(it), next(it), next(it), next(it)
        loss_ref = next(it) if head else None
        wgu_v, wd_v, sem = next(it), next(it), next(it)
        i = pl.program_id(0)

        @pl.when(i == 0)
        def _():
            c0 = pltpu.make_async_copy(wgu_hbm, wgu_v, sem.at[0])
            c1 = pltpu.make_async_copy(wd_hbm, wd_v, sem.at[1])
            c0.start(); c1.start(); c0.wait(); c1.wait()
            if head:
                loss_ref[...] = jnp.zeros_like(loss_ref)
        x = xs_ref[...]
        if split:
            x = jnp.where(i < cfg.nxt, x, xc_ref[...])
        m = mods_ref[0]
        sh, sc, gt = m[mi:mi + 1], m[mi + 1:mi + 2], m[mi + 2:mi + 3]
        xhat, _ = _rms_hat(x)
        h = (xhat * g_ref[gi:gi + 1]) * (1.0 + sc) + sh
        hb = h.astype(bf16)
        hb_ref[...] = hb
        z = _nn(hb, wgu_v[...])
        z_ref[...] = z.astype(bf16)
        zg, zu = z[:, :F], z[:, F:]
        y = _nn(((zg * _sigmoid(zg)) * zu).astype(bf16), wd_v[...])
        y_ref[...] = y
        yhat, _ = _rms_hat(y)
        out = x + 0.5 * gt * (yhat * g_ref[gi + 1:gi + 2])
        if head:
            e = out - t_ref[...]
            out_ref[...] = e * (1.0 / D)
            loss_ref[...] += jnp.sum(jnp.mean(e * e, axis=-1, keepdims=True), axis=0, keepdims=True) * 0.5
        else:
            out_ref[...] = out

    rt = lambda c: pl.BlockSpec((TM, c), lambda i: (i, 0))
    lat = pl.BlockSpec((TM, D), lambda i: (jnp.minimum(i, cfg.nxt - 1), 0))
    x_specs, x_args = ([lat, pl.BlockSpec((TM, D), lambda i: (0, 0))], [xs, xs_ctx]) if split else ([rt(D)], [xs])
    t_specs, t_args = ([rt(D)], [loss_target]) if head else ([], [])
    l_specs, l_shape = ([pl.BlockSpec((8, LANES), lambda i: (0, 0))], [S_((8, LANES), f32)]) if head else ([], [])
    return _call(
        body, (*x_args, mods, gvec, wgu, wd, *t_args), comm, name=name, grid=(nt,),
        in_specs=x_specs + [pl.BlockSpec((1, N_MOD, D), _typ(cfg)), pl.BlockSpec((6, D), lambda i: (0, 0)), ANY, ANY] + t_specs,
        out_specs=[rt(D), rt(D), rt(2 * F), rt(D)] + l_specs,
        out_shape=[S_((R, D), f32), S_((R, D), bf16), S_((R, 2 * F), bf16), S_((R, D), f32)] + l_shape,
        scratch_shapes=[pltpu.VMEM((D, 2 * F), bf16), pltpu.VMEM((F, D), bf16), pltpu.SemaphoreType.DMA((2,))],
        compiler_params=_cp(VMEM_BIG, ("arbitrary",)),
    )


def _ffn_bwd(cfg, dout, xs, z, y, mods, gvec, wgu, wd, mi, gi, with_ctx, name, comm=None, xs_ctx=None):
    TM, D, F = cfg.TM, cfg.D, cfg.F
    nt = cfg.ntiles(with_ctx)
    R = nt * TM
    ntyp = 2 if with_ctx else 1
    split = xs_ctx is not None

    def body(*refs):
        it = iter(refs)
        do_ref, xs_ref = next(it), next(it)
        xc_ref = next(it) if split else None
        z_ref, y_ref, mods_ref, g_ref, wgu_hbm, wd_hbm = (next(it) for _ in range(6))
        dx_ref, dz_ref, dy_ref, a_ref, dm_ref, dg_ref, wgu_v, wd_v, sem = (next(it) for _ in range(9))
        i = pl.program_id(0)

        @pl.when(i == 0)
        def _():
            c0 = pltpu.make_async_copy(wgu_hbm, wgu_v, sem.at[0])
            c1 = pltpu.make_async_copy(wd_hbm, wd_v, sem.at[1])
            c0.start(); c1.start(); c0.wait(); c1.wait()
            dg_ref[...] = jnp.zeros_like(dg_ref)

        @pl.when((i == 0) | (i == cfg.nxt))
        def _():
            dm_ref[...] = jnp.zeros_like(dm_ref)

        do = do_ref[...]
        x = xs_ref[...]
        if split:
            x = jnp.where(i < cfg.nxt, x, xc_ref[...])
        m = mods_ref[0]
        sc, gt = m[mi + 1:mi + 2], m[mi + 2:mi + 3]
        g_pre, g_post = g_ref[gi:gi + 1], g_ref[gi + 1:gi + 2]
        xhat, rinv0 = _rms_hat(x)
        n0 = xhat * g_pre
        yhat, rinv1 = _rms_hat(y_ref[...])
        d_gt = _rsum(0.5 * do * (yhat * g_post))
        dr = (0.5 * gt) * do
        dg_post = _rsum(dr * yhat)
        dy = _rms_bwd(dr * g_post, yhat, rinv1)
        dyb = dy.astype(bf16)
        dy_ref[...] = dyb
        zg = z_ref[:, :F].astype(f32)
        zu = z_ref[:, F:].astype(f32)
        sg = _sigmoid(zg)
        silu = zg * sg
        a_ref[...] = (silu * zu).astype(bf16)
        da = _nt(dyb, wd_v[...])
        dz_ref[:, :F] = (da * zu * (sg * (1.0 + zg * (1.0 - sg)))).astype(bf16)
        dz_ref[:, F:] = (da * silu).astype(bf16)
        dh = _nt(dz_ref[...], wgu_v[...])
        d_sh = _rsum(dh)
        d_sc = _rsum(dh * n0)
        dn = dh * (1.0 + sc)
        dg_pre = _rsum(dn * xhat)
        dx = do + _rms_bwd(dn * g_pre, xhat, rinv0)
        if split:
            @pl.when(i < cfg.nxt)
            def _():
                dx_ref[...] = dx
        else:
            dx_ref[...] = dx
        dm_ref[0] += jnp.concatenate([d_sh, d_sc, d_gt], axis=0)
        dg_ref[...] += jnp.concatenate([dg_pre, dg_post], axis=0)

    rt = lambda c: pl.BlockSpec((TM, c), lambda i: (i, 0))
    lat = pl.BlockSpec((TM, D), lambda i: (jnp.minimum(i, cfg.nxt - 1), 0))
    x_specs, x_args = ([lat, pl.BlockSpec((TM, D), lambda i: (0, 0))], [xs, xs_ctx]) if split else ([rt(D)], [xs])
    return _call(
        body, (dout, *x_args, z, y, mods, gvec, wgu, wd), comm, name=name, grid=(nt,),
        in_specs=[rt(D)] + x_specs + [rt(2 * F), rt(D), pl.BlockSpec((1, N_MOD, D), _typ(cfg)),
                                       pl.BlockSpec((6, D), lambda i: (0, 0)), ANY, ANY],
        out_specs=[lat if split else rt(D), rt(2 * F), rt(D), rt(F), pl.BlockSpec((1, 3, D), _typ(cfg)),
                   pl.BlockSpec((2, D), lambda i: (0, 0))],
        out_shape=[S_((cfg.S if split else R, D), f32), S_((R, 2 * F), bf16), S_((R, D), bf16), S_((R, F), bf16),
                   S_((ntyp, 3, D), f32), S_((2, D), f32)],
        scratch_shapes=[pltpu.VMEM((D, 2 * F), bf16), pltpu.VMEM((F, D), bf16), pltpu.SemaphoreType.DMA((2,))],
        compiler_params=_cp(VMEM_BIG, ("arbitrary",)),
    )


def _wgrad(a, b, k_rows, name, comm=None):
    M, N = a.shape[1], b.shape[1]
    tn = _div_tile(N, 1, WGRAD_TN, LANES) if N > WGRAD_TN // 2 else N
    tn = N // 2 if tn == N and N % (2 * LANES) == 0 else tn
    room = VMEM_BIG - WGRAD_SLACK - 2 * M * tn * 6
    tk = _div_tile(k_rows, 1, min(WGRAD_TK, room // (4 * (M + tn))), LANES)
    nk = k_rows // tk

    def body(a_ref, b_ref, o_ref, ob_ref):
        k = pl.program_id(1)

        @pl.when(k == 0)
        def _():
            o_ref[...] = jnp.zeros_like(o_ref)
        o_ref[...] += _tn(a_ref[...], b_ref[...])

        @pl.when(k == nk - 1)
        def _():
            ob_ref[...] = o_ref[...].astype(bf16)

    ospec = pl.BlockSpec((M, tn), lambda n, k: (0, n))
    return _call(
        body, (a, b), comm, name=name, grid=(N // tn, nk),
        in_specs=[pl.BlockSpec((tk, M), lambda n, k: (k, 0)), pl.BlockSpec((tk, tn), lambda n, k: (k, n))],
        out_specs=[ospec, ospec], out_shape=[S_((M, N), f32), S_((M, N), bf16)],
        compiler_params=_cp(VMEM_BIG, ("arbitrary", "arbitrary")),
    )


def _tmpre_fwd(cfg, xs, mods, gvec, w_in, cos, sin, name):
    TM, D = cfg.TM, cfg.D
    nt, R = cfg.ntt, cfg.T
    W = NA_WIDTH

    def body(xs_ref, mods_ref, g_ref, w_ref, cos_ref, sin_ref, hb_ref, q_ref, k_ref, v_ref, u_ref):
        x = xs_ref[...]
        m = mods_ref[0]
        xhat, _ = _rms_hat(x)
        hb = ((xhat * g_ref[2:3]) * (1.0 + m[4:5]) + m[3:4]).astype(bf16)
        hb_ref[...] = hb
        p = _nn(hb, w_ref[...])
        cs = jnp.tile(cos_ref[...], (1, W // LANES))
        sn = jnp.tile(sin_ref[...], (1, W // LANES))
        q = p[:, 0:W]
        k = p[:, W:2 * W]
        q_ref[...] = ((q * cs + _swap16(q) * sn) * (HEAD_DIM ** -0.5)).astype(bf16)
        k_ref[...] = (k * cs + _swap16(k) * sn).astype(bf16)
        v_ref[...] = p[:, 2 * W:3 * W].astype(bf16)
        u_ref[...] = p[:, 3 * W:]

    rt = lambda c: pl.BlockSpec((TM, c), lambda i: (i, 0))
    return pl.pallas_call(
        body, name=name, grid=(nt,),
        in_specs=[rt(D), pl.BlockSpec((1, N_MOD, D), _typ(cfg)), pl.BlockSpec((6, D), lambda i: (0, 0)),
                  pl.BlockSpec((D, IN_WIDTH), lambda i: (0, 0)), rt(LANES), rt(LANES)],
        out_specs=[rt(D), rt(W), rt(W), rt(W), rt(POOL_WIDTH)],
        out_shape=[S_((R, D), bf16), S_((R, W), bf16), S_((R, W), bf16), S_((R, W), bf16), S_((R, POOL_WIDTH), f32)],
        compiler_params=_cp(VMEM_MID, ("arbitrary",)),
    )(xs, mods, gvec, w_in, cos, sin)


def _tmpre_bwd(cfg, lat, ctx_terms, du_has_ctx, cos, sin, w_in, xs, mods, gvec, dres, res_with_ctx, name):
    TM, D = cfg.TM, cfg.D
    nt, R = cfg.ntt, cfg.T
    nres = cfg.ntiles(res_with_ctx)
    W = NA_WIDTH
    n_ctx = [len(t) for t in ctx_terms]
    flat_ctx = [a for t in ctx_terms for a in t]
    n_asm = 4 + len(flat_ctx) + 2

    def assemble(refs, o_ref):
        dq_ref, dk_ref, dv_ref, du_ref = refs[:4]
        ctx_refs = refs[4:4 + len(flat_ctx)]
        cos_ref, sin_ref = refs[4 + len(flat_ctx):]
        is_ctx = pl.program_id(0) >= cfg.nxt
        vals, off = [], 0
        for lat_ref, n in zip((dq_ref, dk_ref, dv_ref), n_ctx):
            cv = jnp.zeros((TM, W), f32)
            for r_ in ctx_refs[off:off + n]:
                cv = cv + r_[...]
            off += n
            vals.append(jnp.where(is_ctx, cv, lat_ref[...]))
        du_ = du_ref[...] if du_has_ctx else jnp.where(is_ctx, 0.0, du_ref[...])
        cs = jnp.tile(cos_ref[...], (1, W // LANES))
        sn = jnp.tile(sin_ref[...], (1, W // LANES))
        dq_ = vals[0] * (HEAD_DIM ** -0.5)
        dk_ = vals[1]
        o_ref[:, 0:W] = (dq_ * cs + _swap16(dq_ * sn)).astype(bf16)
        o_ref[:, W:2 * W] = (dk_ * cs + _swap16(dk_ * sn)).astype(bf16)
        o_ref[:, 2 * W:3 * W] = vals[2].astype(bf16)
        o_ref[:, 3 * W:] = du_.astype(bf16)

    def body(*refs):
        w_ref, xs_ref, mods_ref, g_ref, dres_ref, dx_ref, dp_ref, dm_ref, dg_ref = refs[n_asm:]
        i = pl.program_id(0)

        @pl.when(i == 0)
        def _():
            dg_ref[...] = jnp.zeros_like(dg_ref)

        @pl.when((i == 0) | (i == cfg.nxt))
        def _():
            dm_ref[...] = jnp.zeros_like(dm_ref)

        assemble(refs[:n_asm], dp_ref)
        dh = _nt(dp_ref[...], w_ref[...])
        x = xs_ref[...]
        m = mods_ref[0]
        g2 = g_ref[2:3]
        xhat, rinv = _rms_hat(x)
        d_sh = _rsum(dh)
        d_sc = _rsum(dh * (xhat * g2))
        dn = dh * (1.0 + m[4:5])
        dg_ref[...] += _rsum(dn * xhat)
        dx = _rms_bwd(dn * g2, xhat, rinv)
        res = dres_ref[...]
        if nres < nt:
            res = jnp.where(i < nres, res, 0.0)
        dx_ref[...] = res + dx
        dm_ref[0] += jnp.concatenate([d_sh, d_sc], axis=0)

    rt = lambda c: pl.BlockSpec((TM, c), lambda i: (i, 0))
    lat_spec = pl.BlockSpec((TM, W), lambda i: (jnp.minimum(i, cfg.nxt - 1), 0))
    du_spec = rt(POOL_WIDTH) if du_has_ctx else lat_spec
    asm_specs = ([lat_spec, lat_spec, lat_spec, du_spec] + [pl.BlockSpec((TM, W), lambda i: (0, 0))] * len(flat_ctx)
                 + [rt(LANES), rt(LANES)])
    return pl.pallas_call(
        body, name=name, grid=(nt,),
        in_specs=asm_specs + [pl.BlockSpec((D, IN_WIDTH), lambda i: (0, 0)), rt(D),
                              pl.BlockSpec((1, N_MOD, D), _typ(cfg)), pl.BlockSpec((6, D), lambda i: (0, 0)),
                              pl.BlockSpec((TM, D), lambda i: (jnp.minimum(i, nres - 1), 0))],
        out_specs=[rt(D), rt(IN_WIDTH), pl.BlockSpec((1, 2, D), _typ(cfg)), pl.BlockSpec((1, D), lambda i: (0, 0))],
        out_shape=[S_((R, D), f32), S_((R, IN_WIDTH), bf16), S_((2, 2, D), f32), S_((1, D), f32)],
        compiler_params=_cp(VMEM_MID, ("arbitrary",)),
    )(*lat, *flat_ctx, cos, sin, w_in, xs, mods, gvec, dres)


def _na_block(cfg, b):
    return jnp.clip(NA_QR * b - NA_KH // 2, 0, cfg.rows - NA_WR)


def _na_load_bias(b, nb, b_hbm, b_v, sem):
    for typ, at in ((0, 0), (1, 1), (2, nb - 1)):
        @pl.when(b == at)
        def _(typ=typ):
            cp = pltpu.make_async_copy(b_hbm.at[typ], b_v, sem)
            cp.start()
            cp.wait()


def _na_probs(qh, klh, kch, bias):
    s_loc = _nt(qh, klh) + bias
    s_ctx = _nt(qh, kch)
    mx = jnp.maximum(jnp.max(s_loc, axis=-1, keepdims=True), jnp.max(s_ctx, axis=-1, keepdims=True))
    e_loc = jnp.exp(s_loc - mx)
    e_ctx = jnp.exp(s_ctx - mx)
    inv = 1.0 / (jnp.sum(e_loc, axis=-1, keepdims=True) + jnp.sum(e_ctx, axis=-1, keepdims=True))
    return e_loc * inv, e_ctx * inv


def _na_fwd(cfg, q, k, v, bexp, name, comm=None):
    S, L, T = cfg.S, cfg.L, cfg.T
    NQ, NW = NA_QR * GRID_W, NA_WR * GRID_W
    nb = cfg.rows // NA_QR

    def body(q_ref, k_hbm, v_hbm, b_hbm, o_ref, k_v, v_v, b_v, sem):
        b = pl.program_id(0)

        @pl.when(b == 0)
        def _():
            cs = [pltpu.make_async_copy(k_hbm, k_v, sem.at[0]), pltpu.make_async_copy(v_hbm, v_v, sem.at[1])]
            for c_ in cs:
                c_.start()
            for c_ in cs:
                c_.wait()

        _na_load_bias(b, nb, b_hbm, b_v, sem.at[2])
        st = pl.multiple_of(_na_block(cfg, b) * GRID_W, GRID_W)
        first = lax.broadcasted_iota(jnp.int32, (NQ, LANES), 1) < HEAD_DIM
        for hp in range(NA_HEADS // 2):
            ls = slice(hp * LANES, (hp + 1) * LANES)
            q2 = q_ref[:, ls]
            kl, vl = k_v[pl.ds(st, NW), ls], v_v[pl.ds(st, NW), ls]
            kc, vc = k_v[S:T, ls], v_v[S:T, ls]
            o2 = []
            for hh in range(2):
                qm = jnp.where(first if hh == 0 else ~first, q2, jnp.zeros_like(q2))
                p_loc, p_ctx = _na_probs(qm, kl, kc, b_v[2 * hp + hh])
                o2.append(_nn(p_loc.astype(bf16), vl) + _nn(p_ctx.astype(bf16), vc))
            o_ref[:, ls] = jnp.where(first, o2[0], o2[1]).astype(bf16)

    return _call(
        body, (q, k, v, bexp), comm, name=name, grid=(nb,),
        in_specs=[pl.BlockSpec((NQ, NA_WIDTH), lambda b: (b, 0)), ANY, ANY, ANY],
        out_specs=[pl.BlockSpec((NQ, NA_WIDTH), lambda b: (b, 0))],
        out_shape=[S_((S, NA_WIDTH), bf16)],
        scratch_shapes=[pltpu.VMEM((T, NA_WIDTH), bf16), pltpu.VMEM((T, NA_WIDTH), bf16),
                        pltpu.VMEM((NA_HEADS, NQ, NW), f32), pltpu.SemaphoreType.DMA((3,))],
        compiler_params=_cp(VMEM_MID, ("arbitrary",)),
    )


def _na_bwd(cfg, do, q, k, v, bexp, name, comm=None):
    S, L, T, rows = cfg.S, cfg.L, cfg.T, cfg.rows
    NQ, NW = NA_QR * GRID_W, NA_WR * GRID_W
    NSLOT = 2 * NA_KH
    nb = rows // NA_QR
    bmax = (rows - NA_WR) // NA_QR
    steps = 2 * nb - bmax
    W = NA_WIDTH
    assert nb >= 3 and bmax >= 1 and rows - NA_QR * bmax <= NSLOT

    def out_group(g):
        return jnp.where(g >= nb, g - nb + bmax, jnp.clip(g - 1, 0, bmax - 1))

    def body(do_ref, q_ref, k_hbm, v_hbm, b_hbm, dq_ref, dk_ref, dv_ref, dkc_ref, dvc_ref, db_hbm,
             k_v, v_v, b_v, db_v, ak, av, akc, avc, sem):
        g = pl.program_id(0)

        @pl.when(g == 0)
        def _():
            cs = [pltpu.make_async_copy(k_hbm, k_v, sem.at[0]), pltpu.make_async_copy(v_hbm, v_v, sem.at[1])]
            for c_ in cs:
                c_.start()
            db_v[...] = jnp.zeros_like(db_v)
            ak[...] = jnp.zeros_like(ak)
            av[...] = jnp.zeros_like(av)
            akc[...] = jnp.zeros_like(akc)
            avc[...] = jnp.zeros_like(avc)
            for c_ in cs:
                c_.wait()

        for typ, at in ((0, 1), (1, nb - 1)):
            @pl.when(g == at)
            def _(typ=typ):
                cp = pltpu.make_async_copy(db_v, db_hbm.at[typ], sem.at[2])
                cp.start()
                cp.wait()
                db_v[...] = jnp.zeros_like(db_v)

        @pl.when(g < nb)
        def _():
            _na_load_bias(g, nb, b_hbm, b_v, sem.at[2])
            ws = _na_block(cfg, g)
            st = pl.multiple_of(ws * GRID_W, GRID_W)
            first = lax.broadcasted_iota(jnp.int32, (NQ, LANES), 1) < HEAD_DIM
            for hp in range(NA_HEADS // 2):
                ls = slice(hp * LANES, (hp + 1) * LANES)
                q2, do2 = q_ref[:, ls], do_ref[:, ls]
                kl, vl = k_v[pl.ds(st, NW), ls], v_v[pl.ds(st, NW), ls]
                kc, vc = k_v[S:T, ls], v_v[S:T, ls]
                dq2 = []
                dk2 = jnp.zeros((NW, LANES), f32)
                dv2 = jnp.zeros((NW, LANES), f32)
                dkc2 = jnp.zeros((L, LANES), f32)
                dvc2 = jnp.zeros((L, LANES), f32)
                for hh in range(2):
                    keep = first if hh == 0 else ~first
                    qm = jnp.where(keep, q2, jnp.zeros_like(q2))
                    dom = jnp.where(keep, do2, jnp.zeros_like(do2))
                    p_loc, p_ctx = _na_probs(qm, kl, kc, b_v[2 * hp + hh])
                    dp_loc = _nt(dom, vl)
                    dp_ctx = _nt(dom, vc)
                    delta = jnp.sum(p_loc * dp_loc, axis=-1, keepdims=True) + jnp.sum(p_ctx * dp_ctx, axis=-1, keepdims=True)
                    ds_loc = p_loc * (dp_loc - delta)
                    ds_ctx = p_ctx * (dp_ctx - delta)
                    db_v[2 * hp + hh, :, 0:NW] += ds_loc
                    dsl, dsc = ds_loc.astype(bf16), ds_ctx.astype(bf16)
                    dq2.append(_nn(dsl, kl) + _nn(dsc, kc))
                    dk2 = dk2 + _tn(dsl, qm)
                    dv2 = dv2 + _tn(p_loc.astype(bf16), dom)
                    dkc2 = dkc2 + _tn(dsc, qm)
                    dvc2 = dvc2 + _tn(p_ctx.astype(bf16), dom)
                dq_ref[:, ls] = jnp.where(first, dq2[0], dq2[1])
                akc[:, ls] += dkc2
                avc[:, ls] += dvc2
                for kk in range(NA_WR):
                    slot = (ws + kk) % NSLOT
                    ak[slot, :, ls] += dk2[kk * GRID_W:(kk + 1) * GRID_W, :]
                    av[slot, :, ls] += dv2[kk * GRID_W:(kk + 1) * GRID_W, :]

        @pl.when(((g >= 1) & (g <= bmax)) | (g >= nb))
        def _():
            base = NA_QR * (out_group(g) % (NSLOT // NA_QR))
            for t in range(NA_QR):
                dk_ref[t * GRID_W:(t + 1) * GRID_W, :] = ak[base + t]
                dv_ref[t * GRID_W:(t + 1) * GRID_W, :] = av[base + t]
                ak[base + t] = jnp.zeros((GRID_W, W), f32)
                av[base + t] = jnp.zeros((GRID_W, W), f32)

        @pl.when(g == nb - 1)
        def _():
            cp = pltpu.make_async_copy(db_v, db_hbm.at[2], sem.at[2])
            cp.start()
            cp.wait()

        @pl.when(g == steps - 1)
        def _():
            dkc_ref[...] = akc[...]
            dvc_ref[...] = avc[...]

    qmap = lambda g: (jnp.minimum(g, nb - 1), 0)
    kmap = lambda g: (out_group(g), 0)
    full = lambda g: (0, 0)
    return _call(
        body, (do, q, k, v, bexp), comm, name=name, grid=(steps,),
        in_specs=[pl.BlockSpec((NQ, W), qmap), pl.BlockSpec((NQ, W), qmap), ANY, ANY, ANY],
        out_specs=[pl.BlockSpec((NQ, W), qmap), pl.BlockSpec((NQ, W), kmap), pl.BlockSpec((NQ, W), kmap),
                   pl.BlockSpec((L, W), full), pl.BlockSpec((L, W), full), ANY],
        out_shape=[S_((S, W), f32), S_((S, W), f32), S_((S, W), f32), S_((L, W), f32), S_((L, W), f32),
                   S_((NA_TYPES, NA_HEADS, NQ, NA_WPAD), f32)],
        scratch_shapes=[pltpu.VMEM((T, W), bf16), pltpu.VMEM((T, W), bf16),
                        pltpu.VMEM((NA_HEADS, NQ, NW), f32), pltpu.VMEM((NA_HEADS, NQ, NA_WPAD), f32),
                        pltpu.VMEM((NSLOT, GRID_W, W), f32), pltpu.VMEM((NSLOT, GRID_W, W), f32),
                        pltpu.VMEM((L, W), f32), pltpu.VMEM((L, W), f32), pltpu.SemaphoreType.DMA((3,))],
        compiler_params=_cp(VMEM_BIG, ("arbitrary",)),
    )


def _rpb_reduce(dbias, flip, sel, name):
    nq, w = NA_QR * GRID_W, GRID_W

    def diag_body(x_ref, j_ref, o_ref):
        rows = []
        for i in range(NA_QR):
            xr = jnp.dot(j_ref[...], x_ref[i * w:(i + 1) * w, :], preferred_element_type=f32, precision=lax.Precision.HIGHEST)
            rows.append(jnp.sum(pltpu.roll(xr, 0, 1, stride=1, stride_axis=0), axis=0, keepdims=True))
        o_ref[...] = jnp.concatenate(rows + [jnp.zeros((8 - NA_QR, NA_WPAD), f32)], axis=0)

    diag = pl.pallas_call(
        diag_body, name=name + "_diag", grid=(NA_TYPES, NA_HEADS),
        in_specs=[pl.BlockSpec((None, None, nq, NA_WPAD), lambda t, h: (t, h, 0, 0)), pl.BlockSpec((w, w), lambda t, h: (0, 0))],
        out_specs=pl.BlockSpec((None, None, 8, NA_WPAD), lambda t, h: (t, h, 0, 0)),
        out_shape=S_((NA_TYPES, NA_HEADS, 8, NA_WPAD), f32),
        compiler_params=_cp(VMEM_MID, ("arbitrary", "arbitrary")),
    )(dbias, flip)
    lo = w - NA_KW
    y = diag[:, :, :NA_QR, lo:lo + NA_WR * w].reshape(NA_TYPES, NA_HEADS, NA_QR, NA_WR, w)
    y = jnp.transpose(y, (1, 0, 2, 3, 4)).reshape(NA_HEADS, NA_TYPES * NA_QR * NA_WR, w)
    y = jnp.pad(y, ((0, 0), (0, NA_SEL_ROWS - y.shape[1]), (0, LANES - w)))

    def body(y_ref, sel_ref, o_ref):
        o_ref[...] = jnp.dot(sel_ref[...], y_ref[...], preferred_element_type=f32, precision=lax.Precision.HIGHEST)

    return pl.pallas_call(
        body, name=name, grid=(NA_HEADS,),
        in_specs=[pl.BlockSpec((None, NA_SEL_ROWS, LANES), lambda h: (h, 0, 0)), pl.BlockSpec((16, NA_SEL_ROWS), lambda h: (0, 0))],
        out_specs=pl.BlockSpec((None, 16, LANES), lambda h: (h, 0, 0)),
        out_shape=S_((NA_HEADS, 16, LANES), f32),
        compiler_params=_cp(VMEM_MID, ("arbitrary",)),
    )(y, sel)


def _ctx_attn_fwd(cfg, q, k, v, name):
    L = cfg.L
    blk = cfg.S // L

    def body(q_ref, k_ref, v_ref, o_ref):
        qv, kv, vv = q_ref[...], k_ref[...], v_ref[...]
        outs = []
        for h in range(NA_HEADS):
            hs = slice(h * HEAD_DIM, (h + 1) * HEAD_DIM)
            s = _nt(qv[:, hs], kv[:, hs])
            e = jnp.exp(s - jnp.max(s, axis=-1, keepdims=True))
            p = e * (1.0 / jnp.sum(e, axis=-1, keepdims=True))
            outs.append(_nn(p.astype(bf16), vv[:, hs]))
        o_ref[...] = jnp.concatenate(outs, axis=-1).astype(bf16)

    spec = pl.BlockSpec((L, NA_WIDTH), lambda i: (blk, 0))
    return pl.pallas_call(
        body, name=name, grid=(1,), in_specs=[spec, spec, spec],
        out_specs=pl.BlockSpec((L, NA_WIDTH), lambda i: (0, 0)), out_shape=S_((L, NA_WIDTH), bf16),
        compiler_params=_cp(VMEM_MID, ("arbitrary",)),
    )(q, k, v)


def _ctx_attn_bwd(cfg, do, q, k, v, name):
    L = cfg.L
    blk = cfg.S // L

    def body(do_ref, q_ref, k_ref, v_ref, dq_ref, dk_ref, dv_ref):
        dov, qv, kv, vv = do_ref[...], q_ref[...], k_ref[...], v_ref[...]
        dqs, dks, dvs = [], [], []
        for h in range(NA_HEADS):
            hs = slice(h * HEAD_DIM, (h + 1) * HEAD_DIM)
            qh, kh, doh = qv[:, hs], kv[:, hs], dov[:, hs]
            s = _nt(qh, kh)
            e = jnp.exp(s - jnp.max(s, axis=-1, keepdims=True))
            p = e * (1.0 / jnp.sum(e, axis=-1, keepdims=True))
            dp = _nt(doh, vv[:, hs])
            ds = (p * (dp - jnp.sum(p * dp, axis=-1, keepdims=True))).astype(bf16)
            dqs.append(_nn(ds, kh))
            dks.append(_tn(ds, qh))
            dvs.append(_tn(p.astype(bf16), doh))
        dq_ref[...] = jnp.concatenate(dqs, axis=-1)
        dk_ref[...] = jnp.concatenate(dks, axis=-1)
        dv_ref[...] = jnp.concatenate(dvs, axis=-1)

    spec = pl.BlockSpec((L, NA_WIDTH), lambda i: (blk, 0))
    ospec = pl.BlockSpec((L, NA_WIDTH), lambda i: (0, 0))
    return pl.pallas_call(
        body, name=name, grid=(1,), in_specs=[spec, spec, spec, spec],
        out_specs=[ospec, ospec, ospec], out_shape=[S_((L, NA_WIDTH), f32)] * 3,
        compiler_params=_cp(VMEM_MID, ("arbitrary",)),
    )(do, q, k, v)


def _pool_centered(u, band, inv):
    return _split_sum(_nn, band, u) * inv - u


def _split_sum(mm, band, t):
    hi = t.astype(bf16)
    lo = (t - hi.astype(f32)).astype(bf16)
    s = mm(band, jnp.concatenate([hi, lo], axis=1))
    n = t.shape[1]
    return s[:, :n] + s[:, n:]


def _pool_mix(u_ref, band_ref, inv_ref, w_ref, ps_ref):
    C = POOL_CH
    outs = []
    for g in range(POOL_GROUPS):
        d = _pool_centered(u_ref[:, g * C:(g + 1) * C], band_ref[0, g], inv_ref[0, g])
        outs.append(_nn(d.astype(bf16), w_ref[g].astype(bf16)) * ps_ref[:, g * C:(g + 1) * C])
    return jnp.concatenate(outs, axis=-1).astype(bf16)


def _pool_bwd(cfg, dmix, u, band, inv, w_pool, pool_scale, with_ctx, name):
    TM = cfg.TM
    nt = cfg.ntiles(with_ctx)
    C = POOL_CH

    def body(dy_ref, u_ref, band_ref, inv_ref, w_ref, ps_ref, du_ref, dw_ref, dps_ref):
        @pl.when(pl.program_id(0) == 0)
        def _():
            dw_ref[...] = jnp.zeros_like(dw_ref)
            dps_ref[...] = jnp.zeros_like(dps_ref)

        dus, dpss = [], []
        for g in range(POOL_GROUPS):
            gs = slice(g * C, (g + 1) * C)
            band_g, inv_g = band_ref[0, g], inv_ref[0, g]
            db = _pool_centered(u_ref[:, gs], band_g, inv_g).astype(bf16)
            wb = w_ref[g].astype(bf16)
            dy = dy_ref[:, gs].astype(f32)
            dpss.append(_rsum(dy * _nn(db, wb)))
            dys = (dy * ps_ref[:, gs]).astype(bf16)
            dw_ref[g] += _tn(db, dys)
            dd = _nt(dys, wb)
            dus.append(_split_sum(_tn, band_g, dd * inv_g) - dd)
        du_ref[...] = jnp.concatenate(dus, axis=-1)
        dps_ref[...] += jnp.concatenate(dpss, axis=-1)

    typ4 = lambda i: (jnp.minimum(i // cfg.nxt, 1), 0, 0, 0)
    return pl.pallas_call(
        body, name=name, grid=(nt,),
        in_specs=[pl.BlockSpec((TM, POOL_WIDTH), lambda i: (i, 1)), pl.BlockSpec((TM, POOL_WIDTH), lambda i: (i, 0)),
                  pl.BlockSpec((1, POOL_GROUPS, TM, TM), typ4), pl.BlockSpec((1, POOL_GROUPS, TM, 1), typ4),
                  pl.BlockSpec((POOL_GROUPS, C, C), lambda i: (0, 0, 0)), pl.BlockSpec((1, POOL_WIDTH), lambda i: (0, 0))],
        out_specs=[pl.BlockSpec((TM, POOL_WIDTH), lambda i: (i, 0)), pl.BlockSpec((POOL_GROUPS, C, C), lambda i: (0, 0, 0)),
                   pl.BlockSpec((1, POOL_WIDTH), lambda i: (0, 0))],
        out_shape=[S_((nt * TM, POOL_WIDTH), f32), S_((POOL_GROUPS, C, C), f32), S_((1, POOL_WIDTH), f32)],
        compiler_params=_cp(VMEM_MID, ("arbitrary",)),
    )(dmix, u, band, inv, w_pool, pool_scale)


def _tmpost_fwd(cfg, na_x, na_c, u, band, inv, w_pool, pool_scale, w_out, xs, mods, gvec, name):
    TM, D = cfg.TM, cfg.D
    with_ctx = na_c is not None
    nt = cfg.ntiles(with_ctx)
    R = nt * TM

    def body(*refs):
        if with_ctx:
            nax_ref, nac_ref = refs[:2]
            na = jnp.where(pl.program_id(0) < cfg.nxt, nax_ref[...], nac_ref[...])
        else:
            na = refs[0][...]
        (u_ref, band_ref, inv_ref, wp_ref, ps_ref, w_ref, xs_ref, mods_ref, g_ref,
         out_ref, opre_ref, mix_ref) = refs[2 if with_ctx else 1:]
        pool_v = _pool_mix(u_ref, band_ref, inv_ref, wp_ref, ps_ref)
        mix_ref[:, 0:NA_WIDTH] = na
        mix_ref[:, NA_WIDTH:] = pool_v
        o = _nn(na, w_ref[0:NA_WIDTH, :]) + _nn(pool_v, w_ref[NA_WIDTH:, :])
        opre_ref[...] = o
        ohat, _ = _rms_hat(o)
        out_ref[...] = xs_ref[...] + mods_ref[0][5:6] * (ohat * g_ref[3:4])

    rt = lambda c: pl.BlockSpec((TM, c), lambda i: (i, 0))
    na_specs = [pl.BlockSpec((TM, NA_WIDTH), lambda i: (jnp.minimum(i, cfg.nxt - 1), 0))]
    na_args = [na_x]
    if with_ctx:
        na_specs.append(pl.BlockSpec((TM, NA_WIDTH), lambda i: (0, 0)))
        na_args.append(na_c)
    typ4 = lambda i: (jnp.minimum(i // cfg.nxt, 1), 0, 0, 0)
    pool_specs = [rt(POOL_WIDTH), pl.BlockSpec((1, POOL_GROUPS, TM, TM), typ4), pl.BlockSpec((1, POOL_GROUPS, TM, 1), typ4),
                  pl.BlockSpec((POOL_GROUPS, POOL_CH, POOL_CH), lambda i: (0, 0, 0)), pl.BlockSpec((1, POOL_WIDTH), lambda i: (0, 0))]
    return pl.pallas_call(
        body, name=name, grid=(nt,),
        in_specs=na_specs + pool_specs + [pl.BlockSpec((MIX_WIDTH, D), lambda i: (0, 0)), rt(D),
                                          pl.BlockSpec((1, N_MOD, D), _typ(cfg)), pl.BlockSpec((6, D), lambda i: (0, 0))],
        out_specs=[rt(D), rt(D), rt(MIX_WIDTH)],
        out_shape=[S_((R, D), f32), S_((R, D), f32), S_((R, MIX_WIDTH), bf16)],
        compiler_params=_cp(VMEM_MID, ("arbitrary",)),
    )(*na_args, u, band, inv, w_pool, pool_scale, w_out, xs, mods, gvec)


def _tmpost_bwd(cfg, dout, opre, w_out, mods, gvec, with_ctx, name):
    TM, D = cfg.TM, cfg.D
    nt = cfg.ntiles(with_ctx)
    R = nt * TM
    ntyp = 2 if with_ctx else 1

    def body(do_ref, opre_ref, w_ref, mods_ref, g_ref, dop_ref, dmix_ref, dm_ref, dg_ref):
        i = pl.program_id(0)

        @pl.when(i == 0)
        def _():
            dg_ref[...] = jnp.zeros_like(dg_ref)

        @pl.when((i == 0) | (i == cfg.nxt))
        def _():
            dm_ref[...] = jnp.zeros_like(dm_ref)

        do = do_ref[...]
        g3 = g_ref[3:4]
        ohat, rinv = _rms_hat(opre_ref[...])
        dm_ref[0] += _rsum(do * (ohat * g3))
        dr = mods_ref[0][5:6] * do
        dg_ref[...] += _rsum(dr * ohat)
        dob = _rms_bwd(dr * g3, ohat, rinv).astype(bf16)
        dop_ref[...] = dob
        dmix_ref[...] = _nt(dob, w_ref[...]).astype(bf16)

    rt = lambda c: pl.BlockSpec((TM, c), lambda i: (i, 0))
    return pl.pallas_call(
        body, name=name, grid=(nt,),
        in_specs=[rt(D), rt(D), pl.BlockSpec((MIX_WIDTH, D), lambda i: (0, 0)),
                  pl.BlockSpec((1, N_MOD, D), _typ(cfg)), pl.BlockSpec((6, D), lambda i: (0, 0))],
        out_specs=[rt(D), rt(MIX_WIDTH), pl.BlockSpec((1, 1, D), _typ(cfg)), pl.BlockSpec((1, D), lambda i: (0, 0))],
        out_shape=[S_((R, D), bf16), S_((R, MIX_WIDTH), bf16), S_((ntyp, 1, D), f32), S_((1, D), f32)],
        compiler_params=_cp(VMEM_MID, ("arbitrary",)),
    )(dout, opre, w_out, mods, gvec)


def _modvec_fwd(cvecs, w_mod, b_shard, name):
    nl, D, n = w_mod.shape
    tn = n // 3 if (n % 3 == 0 and (n // 3) % LANES == 0) else n

    def body(c_ref, w_ref, b_ref, o_ref, s_ref):
        cv = c_ref[...]
        sv = cv * _sigmoid(cv)
        s_ref[...] = sv
        o_ref[...] = _nn(sv.astype(bf16), w_ref[...].astype(bf16)) + b_ref[...]

    return pl.pallas_call(
        body, name=name, grid=(nl, n // tn),
        in_specs=[pl.BlockSpec((16, D), lambda l, j: (0, 0)), pl.BlockSpec((None, D, tn), lambda l, j: (l, 0, j)),
                  pl.BlockSpec((None, 1, tn), lambda l, j: (l, 0, j))],
        out_specs=[pl.BlockSpec((None, 16, tn), lambda l, j: (l, 0, j)), pl.BlockSpec((16, D), lambda l, j: (0, 0))],
        out_shape=[S_((nl, 16, n), f32), S_((16, D), f32)],
        compiler_params=_cp(VMEM_MID, ("arbitrary", "arbitrary")),
    )(cvecs, w_mod, b_shard)


def _modvec_bwd(s_t, dm, w_mod, name):
    nl, D, n = w_mod.shape
    tn = n // 3 if (n % 3 == 0 and (n // 3) % LANES == 0) else n

    def body(s_ref, dm_ref, w_ref, gw_ref, gc_ref):
        @pl.when(pl.program_id(1) == 0)
        def _():
            gc_ref[...] = jnp.zeros_like(gc_ref)
        dmv = dm_ref[...]
        gw_ref[...] = jnp.dot(s_ref[...], dmv, preferred_element_type=f32, precision=lax.Precision.HIGHEST)
        gc_ref[...] += _nt(dmv[8:16].astype(bf16), w_ref[...].astype(bf16))

    return pl.pallas_call(
        body, name=name, grid=(nl, n // tn),
        in_specs=[pl.BlockSpec((D, 16), lambda l, j: (0, 0)), pl.BlockSpec((None, 16, tn), lambda l, j: (l, 0, j)),
                  pl.BlockSpec((None, D, tn), lambda l, j: (l, 0, j))],
        out_specs=[pl.BlockSpec((None, D, tn), lambda l, j: (l, 0, j)), pl.BlockSpec((None, 8, D), lambda l, j: (l, 0, 0))],
        out_shape=[S_((nl, D, n), f32), S_((nl, 8, D), f32)],
        compiler_params=_cp(VMEM_MID, ("arbitrary", "arbitrary")),
    )(s_t, dm, w_mod)


def _as2d(a):
    n = a.size
    if a.ndim >= 2 and a.shape[-1] % LANES == 0:
        return a.reshape(-1, a.shape[-1])
    if n % LANES == 0:
        return a.reshape(-1, LANES)
    return a.reshape(-1, a.shape[-1]) if a.ndim >= 2 else a.reshape(1, n)


def _row_tile(r, c, budget_elems):
    if r * c <= budget_elems or r % 8 != 0:
        return r
    t = r
    while t * c > budget_elems and t % 16 == 0:
        t //= 2
    return t


def _div_tile(r, c, budget_elems, mult=16):
    best = None
    for t in range(mult, r + 1, mult):
        if r % t == 0 and t * c <= budget_elems:
            best = t
    return best if best is not None else r


def _chip_index():
    return 2 * lax.axis_index("x") + lax.axis_index("y")


def _cast_into_place(shards, lead, axis, name):
    r, c = shards.shape[-2:]
    tr = _div_tile(r, c, 3 * ELEMWISE_BLOCK)
    nr = r // tr
    out_map = (lambda i: (i, _chip_index())) if axis == 1 else (lambda i: (_chip_index() * nr + i, 0))
    full2 = (r, c * N_CHIPS) if axis == 1 else (r * N_CHIPS, c)

    def body(a_ref, o_ref):
        o_ref[...] = a_ref[...].astype(bf16)

    return pl.pallas_call(
        body, name=name, grid=(nr,),
        in_specs=[pl.BlockSpec((None,) * len(lead) + (tr, c), lambda i: tuple(lead) + (i, 0))],
        out_specs=pl.BlockSpec((tr, c), out_map),
        out_shape=S_(full2, bf16), compiler_params=_cp(VMEM_MID, ("arbitrary",)),
    )(shards)


def _sum_devices8(own, land, axis, into, lead, name):
    _, rh, cs = land.shape
    tr = _div_tile(rh, cs, 2 * ELEMWISE_BLOCK)
    nr = rh // tr
    core = lambda: lax.axis_index("c")
    if axis == 1:
        own_map = lambda i: (core() * nr + i, _chip_index())
    else:
        own_map = lambda i: (_chip_index() * 2 * nr + core() * nr + i, 0)
    nl = len(lead)

    def land_spec(j):
        return pl.BlockSpec((None, tr, cs), lambda i: ((2 * _chip_index() + core() + j) % N_DEV, i, 0))

    def body(own_ref, *rest):
        acc = own_ref[...]
        for p_ref in rest[:N_DEV - 1]:
            acc = acc + p_ref[...].astype(f32)
        rest[-1][...] = acc

    return pl.pallas_call(
        body, name=name, grid=(nr,),
        in_specs=[pl.BlockSpec((tr, cs), own_map)] + [land_spec(j) for j in range(1, N_DEV)] + [ANY],
        out_specs=pl.BlockSpec((None,) * nl + (tr, cs), lambda i: tuple(lead) + (core() * nr + i, 0)),
        out_shape=S_(into.shape, f32), input_output_aliases={N_DEV: 0},
        compiler_params=_cp(VMEM_MID, ("arbitrary",)),
    )(own, *([land] * (N_DEV - 1)), into)


def _adamw(w, g, m, v, name, emit_grad=False, comm=None):
    shape = w.shape
    w2, g2, m2, v2 = _as2d(w), _as2d(g), _as2d(m), _as2d(v)
    r, c = w2.shape
    tr = _row_tile(r, c, ELEMWISE_BLOCK)
    c1 = 1.0 - ADAM_B1 ** ADAM_STEP
    c2 = 1.0 - ADAM_B2 ** ADAM_STEP
    n_out = 4 if emit_grad else 3

    def body(w_ref, g_ref, m_ref, v_ref, d_ref, mo_ref, vo_ref, *go_ref):
        gv = g_ref[...]
        mn = ADAM_B1 * m_ref[...] + (1.0 - ADAM_B1) * gv
        vn = ADAM_B2 * v_ref[...] + (1.0 - ADAM_B2) * (gv * gv)
        mo_ref[...] = mn
        vo_ref[...] = vn
        d_ref[...] = -ADAM_LR * ((mn / c1) / (jnp.sqrt(vn / c2) + ADAM_EPS) + ADAM_WD * w_ref[...])
        if emit_grad:
            go_ref[0][...] = gv

    spec = pl.BlockSpec((tr, c), lambda i: (i, 0))
    outs, res = _call(body, (w2, g2, m2, v2), comm, name=name, grid=(r // tr,), in_specs=[spec] * 4, out_specs=[spec] * n_out,
                      out_shape=[S_((r, c), f32)] * n_out, compiler_params=_cp(VMEM_MID, ("arbitrary",)))
    outs = tuple(o.reshape(shape) for o in outs)
    return outs if comm is None else (outs, res)


def _sum_devices(gathered, name):
    _, r, c = gathered.shape

    def body(a_ref, o_ref):
        acc = a_ref[0]
        for j in range(1, N_DEV):
            acc = acc + a_ref[j]
        o_ref[...] = acc

    tr = _row_tile(r, c, ELEMWISE_BLOCK // 4)
    return pl.pallas_call(
        body, name=name, grid=(r // tr,),
        in_specs=[pl.BlockSpec((N_DEV, tr, c), lambda i: (0, i, 0))], out_specs=pl.BlockSpec((tr, c), lambda i: (i, 0)),
        out_shape=S_((r, c), f32), compiler_params=_cp(VMEM_MID, ("arbitrary",)))(gathered)


def _all_gather_small(block, name):
    m_per, n = block.shape

    def body(x_ref, out_ref, send_sems, recv_sems, local_sem):
        x, y, c = _mesh_pos()
        me, sibling = (x, y, c), (x, y, 1 - c)
        chips = [(1 - x, y), (x, 1 - y), (1 - x, 1 - y)]

        def rows(px, py, pc):
            return out_ref.at[pl.ds((4 * px + 2 * py + pc) * m_per, m_per), :]

        def copy(k, blk, to, src=None):
            return pltpu.make_async_remote_copy(
                src_ref=rows(*blk) if src is None else src, dst_ref=rows(*blk),
                send_sem=send_sems.at[k], recv_sem=recv_sems.at[k], device_id=to, device_id_type=MESH)

        mine = pltpu.make_async_copy(x_ref, rows(*me), local_sem)
        mine.start()
        first = [copy(0, me, sibling, src=x_ref)]
        first += [copy(1 + j, me, (*chip, c), src=x_ref) for j, chip in enumerate(chips)]
        for cp in first:
            cp.start()
        passed = [copy(4 + j, (*chip, c), sibling) for j, chip in enumerate(chips)]
        for j, chip in enumerate(chips):
            copy(1 + j, (*chip, c), me).wait_recv()
            passed[j].start()
        copy(0, sibling, me).wait_recv()
        for j, chip in enumerate(chips):
            copy(4 + j, (*chip, 1 - c), me).wait_recv()
        for cp in first + passed:
            cp.wait_send()
        mine.wait()

    return pl.pallas_call(
        body, name=name, out_shape=S_((N_DEV * m_per, n), block.dtype),
        in_specs=[pl.BlockSpec(memory_space=pltpu.VMEM)], out_specs=pl.BlockSpec(memory_space=pltpu.VMEM),
        scratch_shapes=[pltpu.SemaphoreType.DMA((7,)), pltpu.SemaphoreType.DMA((7,)), pltpu.SemaphoreType.DMA],
        compiler_params=_cp(VMEM_MID),
    )(block)


def _pack_rows(arrays):
    flat = jnp.concatenate([a.reshape(-1) for a in arrays])
    pad = (-flat.size) % (8 * LANES)
    return jnp.pad(flat, (0, pad)).reshape(-1, LANES)


def _unpack_rows(packed, shapes):
    flat = packed.reshape(-1)
    out, off = [], 0
    for s in shapes:
        n = int(np.prod(s))
        out.append(flat[off:off + n].reshape(s))
        off += n
    return out


W_AXIS = {"gu": 1, "dn": 0, "wi": 1, "wo": 0}
SMALL_NAMES = ("dmods", "dg", "drpb", "dwp", "dps")


def _half_merge(bufs, name):
    nt = len(bufs)

    def body(*refs):
        outs = refs[nt:2 * nt]
        send_sems, recv_sems = refs[2 * nt:]
        x, y, c = _mesh_pos()

        def half(ref, h):
            rh = ref.shape[-2] // 2
            return ref.at[(slice(None),) * (len(ref.shape) - 2) + (pl.ds(h * rh, rh), slice(None))]

        cps = []
        for t in range(nt):
            cp = pltpu.make_async_remote_copy(
                src_ref=half(outs[t], c), dst_ref=half(outs[t], c), send_sem=send_sems.at[t], recv_sem=recv_sems.at[t],
                device_id=(x, y, 1 - c), device_id_type=MESH)
            cp.start()
            cps.append(cp)
        for t in range(nt):
            pltpu.make_async_remote_copy(
                src_ref=half(outs[t], 1 - c), dst_ref=half(outs[t], 1 - c), send_sem=send_sems.at[t], recv_sem=recv_sems.at[t],
                device_id=(x, y, 1 - c), device_id_type=MESH).wait_recv()
        for cp in cps:
            cp.wait_send()

    return pl.pallas_call(
        body, name=name, in_specs=[ANY] * nt, out_specs=[ANY] * nt, out_shape=[S_(b.shape, f32) for b in bufs],
        input_output_aliases={t: t for t in range(nt)},
        scratch_shapes=[pltpu.SemaphoreType.DMA((nt,)), pltpu.SemaphoreType.DMA((nt,))],
        compiler_params=_cp(VMEM_MID),
    )(*bufs)


def _local_step(cfg, x_lat, x_ctx, target, mods, norm_g, W, G, na_rpb, w_pool, pool_scale):
    S, L, T, D, F = cfg.S, cfg.L, cfg.T, cfg.D, cfg.F
    depth = norm_g.shape[0]
    cos, sin = _rope_tables(S, L)
    band, inv = _pool_tables(cfg.TM, L)
    flip, sel = _rpb_reduce_tables()

    assert depth == 2, "the carrier schedules below are written for two layers"
    fwd_carry = {"ffn_fwd_0_0": [("wi", 0), ("wo", 0), ("gu", 0, 1), ("dn", 0, 1)],
                 "na_fwd_0": [("gu", 1, 0), ("dn", 1, 0)],
                 "ffn_fwd_0_1": [("wi", 1), ("wo", 1), ("gu", 1, 1), ("dn", 1, 1)]}
    bwd_carry = {"na_bwd_1": [("gu", 1, 1), ("dn", 1, 1)], "ffn_bwd_1_0": [("wi", 1), ("wo", 1)],
                 "ffn_bwd_0_1": [("gu", 1, 0), ("dn", 1, 0)], "na_bwd_0": [("gu", 0, 1), ("dn", 0, 1)],
                 "ffn_bwd_0_0": [("wi", 0), ("wo", 0)], "wgrad_dn_0_0": [("gu", 0, 0)]}
    last_scatter = [("dn", 0, 0)]
    tag = lambda key: "_".join(str(p) for p in key)
    g_f32, g_b16 = {}, {}

    def gather_on(name):
        keys = fwd_carry.get(name)
        return None if keys is None else _gather_comm([W[k_] for k_ in keys], [W_AXIS[k_[0]] for k_ in keys])

    def gathered(name, res):
        if name in fwd_carry:
            W.update(zip(fwd_carry[name], res))

    def scatter_on(name):
        keys = bwd_carry.get(name)
        return None if keys is None else _scatter_comm([g_b16[k_] for k_ in keys], [W_AXIS[k_[0]] for k_ in keys])

    def scattered(keys, lands):
        for key, land in zip(keys, lands):
            G[key[0]] = _sum_devices8(g_f32[key], land, W_AXIS[key[0]], G[key[0]], key[1:], f"sum8_{tag(key)}")

    small_landed = []

    def wgrad(key, a, b, rows, other_comm=None):
        name = f"wgrad_{tag(key)}"
        if other_comm is not None:
            assert name not in bwd_carry
            (g_f32[key], g_b16[key]), res = _wgrad(a, b, rows, name, other_comm)
            small_landed.extend(res)
            return
        (g_f32[key], g_b16[key]), lands = _wgrad(a, b, rows, name, scatter_on(name))
        scattered(bwd_carry.get(name, ()), lands)

    saved = []
    xs, xs_ctx = x_lat, x_ctx
    for l in range(depth):
        last = l == depth - 1
        wc = not last
        gvec = norm_g[l]
        ps = pool_scale[l].reshape(1, POOL_WIDTH)
        bexp = _expand_rpb(na_rpb[l], f"bias_expand_{l}")
        name = f"ffn_fwd_{l}_0"
        (xs1, hb1, z1, y1), res = _ffn_fwd(cfg, xs, mods[l], gvec, W["gu", l, 0], W["dn", l, 0], 0, 0, True, name,
                                           gather_on(name), xs_ctx=xs_ctx)
        gathered(name, res)
        hb2, q, k, v, u = _tmpre_fwd(cfg, xs1, mods[l], gvec, W["wi", l], cos, sin, f"tmpre_fwd_{l}")
        name = f"na_fwd_{l}"
        (na_x,), res = _na_fwd(cfg, q, k, v, bexp, name, gather_on(name))
        gathered(name, res)
        na_c = _ctx_attn_fwd(cfg, q, k, v, f"ctx_attn_fwd_{l}") if wc else None
        xs2, opre, mix = _tmpost_fwd(cfg, na_x, na_c, u, band, inv, w_pool[l], ps, W["wo", l], xs1, mods[l], gvec,
                                     f"tmpost_fwd_{l}")
        name = f"ffn_fwd_{l}_1"
        outs, res = _ffn_fwd(cfg, xs2, mods[l], gvec, W["gu", l, 1], W["dn", l, 1], 6, 4, wc, name, gather_on(name),
                             loss_target=target if last else None)
        xs3, hb3, z3, y3 = outs[:4]
        gathered(name, res)
        saved.append(dict(xs=xs, xs_ctx=xs_ctx, xs1=xs1, xs2=xs2, hb1=hb1, z1=z1, y1=y1, hb2=hb2, q=q, k=k, v=v, u=u, mix=mix,
                          opre=opre, hb3=hb3, z3=z3, y3=y3, bexp=bexp, ps=ps, gvec=gvec))
        xs, xs_ctx = xs3, None

    dxs, loss_blk = xs, outs[4]

    small = [None] * depth
    for l in reversed(range(depth)):
        last = l == depth - 1
        wc = not last
        sv = saved[l]
        gvec = sv["gvec"]
        rows_b = cfg.T if wc else cfg.S
        name = f"ffn_bwd_{l}_1"
        (dxs2, dz, dyb, ab, dm678, dg45), lands = _ffn_bwd(cfg, dxs, sv["xs2"], sv["z3"], sv["y3"], mods[l], gvec,
                                                           W["gu", l, 1], W["dn", l, 1], 6, 4, wc, name, scatter_on(name))
        scattered(bwd_carry.get(name, ()), lands)
        wgrad(("gu", l, 1), sv["hb3"], dz, rows_b)
        wgrad(("dn", l, 1), ab, dyb, rows_b)
        dop, dmix, dm5, dg3 = _tmpost_bwd(cfg, dxs2, sv["opre"], W["wo", l], mods[l], gvec, wc, f"tmpost_bwd_{l}")
        wgrad(("wo", l), sv["mix"], dop, rows_b)
        du, dwp, dps = _pool_bwd(cfg, dmix, sv["u"], band, inv, w_pool[l], sv["ps"], wc, f"pool_bwd_{l}")
        name = f"na_bwd_{l}"
        (dq, dk, dv, dkc, dvc, dbexp), lands = _na_bwd(cfg, dmix, sv["q"], sv["k"], sv["v"], sv["bexp"], name, scatter_on(name))
        scattered(bwd_carry.get(name, ()), lands)
        drpb = _rpb_reduce(dbexp, flip, sel, f"rpb_reduce_{l}")
        if wc:
            dqc, dkc2, dvc2 = _ctx_attn_bwd(cfg, dmix, sv["q"], sv["k"], sv["v"], f"ctx_attn_bwd_{l}")
            ctx_terms = ([dqc], [dkc, dkc2], [dvc, dvc2])
        else:
            ctx_terms = ([], [dkc], [dvc])
        dxs1, dproj, dm34, dg2 = _tmpre_bwd(cfg, (dq, dk, dv, du), ctx_terms, wc, cos, sin, W["wi", l], sv["xs1"], mods[l], gvec,
                                            dxs2, wc, f"tmpre_bwd_{l}")
        wgrad(("wi", l), sv["hb2"], dproj, cfg.T)
        name = f"ffn_bwd_{l}_0"
        (dxs, dz, dyb, ab, dm012, dg01), lands = _ffn_bwd(cfg, dxs1, sv["xs"], sv["z1"], sv["y1"], mods[l], gvec,
                                                          W["gu", l, 0], W["dn", l, 0], 0, 0, True, name, scatter_on(name),
                                                          xs_ctx=sv["xs_ctx"])
        scattered(bwd_carry.get(name, ()), lands)
        if not wc:
            zero = lambda a: jnp.concatenate([a, jnp.zeros_like(a)], axis=0)
            dm5, dm678 = zero(dm5), zero(dm678)
        dmods = jnp.concatenate([dm012, dm34, dm5, dm678], axis=1)
        dgs = jnp.concatenate([dg01, dg2, dg3, dg45], axis=0)
        small[l] = dict(dmods=dmods, dg=dgs, drpb=drpb, dwp=dwp, dps=dps)
        small_gather = None
        if l == 0:
            parts = [jnp.stack([small[j][n_] for j in range(depth)]) for n_ in SMALL_NAMES]
            packed = _pack_rows(parts)
            small_gather = _allgather_comm(packed)
        wgrad(("gu", l, 0), sv["hb1"], dz, cfg.T, small_gather)
        wgrad(("dn", l, 0), ab, dyb, cfg.T)
    last_comm = _scatter_comm([g_b16[k_] for k_ in last_scatter], [W_AXIS[k_[0]] for k_ in last_scatter])

    def finish_weight_grads(lands):
        scattered(last_scatter, lands)
        kinds = ("gu", "dn", "wi", "wo")
        return dict(zip(kinds, _half_merge([G[k_] for k_ in kinds], "merge_halves")))

    return loss_blk, dxs, (last_comm, finish_weight_grads), (packed, [p.shape for p in parts], small_landed[0])


def kernel(x, c, ctx, c_ctx, w_mod, b_mod, norm_g, w_ffn_gate_up, w_ffn_down, w_in, w_out, na_rpb, w_pool, pool_scale, loss_target, m_c_ctx, m_w_mod, m_b_mod, m_norm_g, m_w_ffn_gate_up, m_w_ffn_down, m_w_in, m_w_out, m_na_rpb, m_w_pool, m_pool_scale, v_c_ctx, v_w_mod, v_b_mod, v_norm_g, v_w_ffn_gate_up, v_w_ffn_down, v_w_in, v_w_out, v_na_rpb, v_w_pool, v_pool_scale):
    S, D = x.shape[1], x.shape[2]
    L = ctx.shape[1]
    depth = w_mod.shape[0]
    F = w_ffn_down.shape[2] * N_CHIPS
    nmod = w_mod.shape[2]
    gsh = norm_g.shape[2]
    cfg = _Cfg(S, L, D, F)
    mx, my, mc = _mesh_pos()
    chip = 2 * mx + my
    dev = 4 * mx + 2 * my + mc

    W = {}
    for l in range(depth):
        for i in range(2):
            W["gu", l, i] = _cast_into_place(w_ffn_gate_up, (l, i), W_AXIS["gu"], f"cast_gu_{l}_{i}")
            W["dn", l, i] = _cast_into_place(w_ffn_down, (l, i), W_AXIS["dn"], f"cast_dn_{l}_{i}")
        W["wi", l] = _cast_into_place(w_in, (l,), W_AXIS["wi"], f"cast_wi_{l}")
        W["wo", l] = _cast_into_place(w_out, (l,), W_AXIS["wo"], f"cast_wo_{l}")
    first = [("gu", 0, 0), ("dn", 0, 0)]
    W.update(zip(first, _comm_only(_gather_comm([W[k_] for k_ in first], [W_AXIS[k_[0]] for k_ in first]), "gather_first")))
    G = {"gu": lax.empty(w_ffn_gate_up.shape, f32), "dn": lax.empty(w_ffn_down.shape, f32),
         "wi": lax.empty(w_in.shape, f32), "wo": lax.empty(w_out.shape, f32)}

    cg_packed = _pack_rows([c, norm_g])
    cg_all = _all_gather_small(cg_packed, "gather_c_norm_g").reshape(N_DEV, -1)
    c_all = cg_all[:, :D]
    ng = cg_all[:, D:D + norm_g.size].reshape(N_DEV, depth, 6, gsh)
    norm_g_all = jnp.concatenate([ng[2 * j] for j in range(N_CHIPS)], axis=-1)
    cvecs = jnp.concatenate([c_all, c_ctx[None], jnp.zeros((7, D), f32)], axis=0)
    b_shard = lax.dynamic_slice_in_dim(b_mod, chip * nmod, nmod, axis=1).reshape(depth, 1, nmod)
    m_part, silu_c = _modvec_fwd(cvecs, w_mod, b_shard, "modvec_fwd")
    m_all = _all_gather_small(m_part.reshape(depth * 16, nmod), "gather_mod").reshape(N_DEV, depth, 16, nmod)
    m_full = jnp.concatenate([m_all[2 * j] for j in range(N_CHIPS)], axis=-1)
    m_mine = lax.dynamic_index_in_dim(m_full, dev, axis=1, keepdims=False)
    mods = jnp.stack([m_mine, m_full[:, 8]], axis=1).reshape(depth, 2, N_MOD, D)

    loss_blk, dx_lat, (last_comm, finish_weight_grads), small = _local_step(
        cfg, x[0], ctx[0], loss_target[0], mods, norm_g_all, W, G, na_rpb, w_pool, pool_scale)
    loss = lax.psum(loss_blk[0, 0], ("x", "y", "c"))
    grad_x = dx_lat[None]

    packed, shapes, landed = small
    gathered = lax.dynamic_update_index_in_dim(landed, packed, dev, 0)
    total = _unpack_rows(_sum_devices(gathered, "sum_small"), shapes)
    dmods_sum, dg_sum, drpb_sum, dwp_sum, dps_sum = total
    dmods_each = jnp.stack([_unpack_rows(gathered[j], shapes[:1])[0] for j in range(N_DEV)])
    dm_rows = jnp.concatenate([jnp.transpose(dmods_each[:, :, 0], (1, 0, 2, 3)).reshape(depth, N_DEV, N_MOD * D),
                               dmods_sum[:, 1].reshape(depth, 1, N_MOD * D),
                               jnp.zeros((depth, 7, N_MOD * D), f32)], axis=1)
    dm_shard = lax.dynamic_slice_in_dim(dm_rows, chip * nmod, nmod, axis=2)
    grad_w_mod, gc_part = _modvec_bwd(silu_c.T, dm_shard, w_mod, "modvec_bwd")
    gc_all = _all_gather_small(gc_part.reshape(depth * 8, D), "gather_gc").reshape(N_DEV, depth, 8, D)
    grad_b_mod, grad_c_ctx = _small_finish(dm_rows, gc_all, c_ctx)
    grad_norm_g = lax.dynamic_slice_in_dim(dg_sum, chip * gsh, gsh, axis=2)
    grad_na_rpb = drpb_sum[:, :, :2 * NA_KH - 1, :2 * NA_KW - 1]
    grad_w_pool = dwp_sum
    grad_pool_scale = dps_sum.reshape(depth, POOL_WIDTH)

    upd_w_mod, lands = _adamw(w_mod, grad_w_mod, m_w_mod, v_w_mod, "adamw_w_mod", comm=last_comm)
    wgrads = finish_weight_grads(lands)
    g_gu, g_dn, g_wi, g_wo = wgrads["gu"], wgrads["dn"], wgrads["wi"], wgrads["wo"]
    grads = [grad_c_ctx, grad_w_mod, grad_b_mod, grad_norm_g, g_gu, g_dn, g_wi, g_wo, grad_na_rpb, grad_w_pool, grad_pool_scale]
    ws = [c_ctx, w_mod, b_mod, norm_g, w_ffn_gate_up, w_ffn_down, w_in, w_out, na_rpb, w_pool, pool_scale]
    ms = [m_c_ctx, m_w_mod, m_b_mod, m_norm_g, m_w_ffn_gate_up, m_w_ffn_down, m_w_in, m_w_out, m_na_rpb, m_w_pool, m_pool_scale]
    vs = [v_c_ctx, v_w_mod, v_b_mod, v_norm_g, v_w_ffn_gate_up, v_w_ffn_down, v_w_in, v_w_out, v_na_rpb, v_w_pool, v_pool_scale]
    tags = ["c_ctx", "w_mod", "b_mod", "norm_g", "gate_up", "down", "w_in", "w_out", "na_rpb", "w_pool", "pool_scale"]
    merged = ("gate_up", "down", "w_in", "w_out")
    upd = [upd_w_mod if t == "w_mod" else _adamw(w_, g_, m_, v_, f"adamw_{t}", emit_grad=t in merged)
           for w_, g_, m_, v_, t in zip(ws, grads, ms, vs, tags)]
    grads = [u_[3] if t in merged else g_ for g_, u_, t in zip(grads, upd, tags)]
    return (loss, grad_x, *grads, *[u_[0] for u_ in upd], *[u_[1] for u_ in upd], *[u_[2] for u_ in upd])


def _small_finish(dm_rows, gc_all, c_ctx):
    depth, _, n = dm_rows.shape
    D = c_ctx.shape[0]

    def body(dm_ref, gc_ref, c_ref, gb_ref, gcx_ref):
        acc = dm_ref[:, 0]
        for j in range(1, N_DEV + 1):
            acc = acc + dm_ref[:, j]
        gb_ref[...] = acc
        t = jnp.zeros((1, D), f32)
        for l in range(depth):
            for j in range(N_CHIPS):
                t = t + gc_ref[2 * j, l, 0:1, :]
        cv = c_ref[...]
        sg = _sigmoid(cv)
        gcx_ref[...] = t * (sg * (1.0 + cv * (1.0 - sg)))

    gb, gcx = pl.pallas_call(
        body, name="small_finish",
        out_shape=[S_((depth, n), f32), S_((1, D), f32)],
        compiler_params=_cp(VMEM_MID),
    )(dm_rows, gc_all, c_ctx.reshape(1, D))
    return gb, gcx.reshape(D)
```

```python
import functools

import numpy as np
import jax
import jax.numpy as jnp
from jax import lax
from jax.experimental import pallas as pl
from jax.experimental.pallas import tpu as pltpu

f32, bf16 = jnp.float32, jnp.bfloat16

GRID_W = 64
N_MOD = 9
NA_HEADS = 8
HEAD_DIM = 64
NA_WIDTH = NA_HEADS * HEAD_DIM
NA_KH = 8
NA_KW = 16
POOL_GROUPS = 4
POOL_CH = 128
POOL_WIDTH = POOL_GROUPS * POOL_CH
POOL_WINDOWS = (2, 4, 8, 16)
IN_WIDTH = 3 * NA_WIDTH + POOL_WIDTH
MIX_WIDTH = NA_WIDTH + POOL_WIDTH
ROPE_THETA = 10000.0
ROPE_PAIRS = HEAD_DIM // 4
RMS_EPS = 1e-6
NEG_INF = -1e30
ADAM_LR, ADAM_B1, ADAM_B2, ADAM_EPS, ADAM_WD, ADAM_STEP = 0.001, 0.9, 0.999, 1e-08, 0.01, 10

N_DEV = 8
N_CHIPS = 4
LANES = 128
MIB = 1024 * 1024
VMEM_BIG = 52 * MIB
VMEM_MID = 40 * MIB
WGRAD_TN = 1408
WGRAD_TK = 2816
WGRAD_SLACK = 6 * MIB
ELEMWISE_BLOCK = 256 * 1024
MESH = pl.DeviceIdType.MESH
ANY = pl.BlockSpec(memory_space=pl.ANY)
S_ = jax.ShapeDtypeStruct


def _cp(vmem=VMEM_MID, sem=None):
    return pltpu.CompilerParams(vmem_limit_bytes=vmem, dimension_semantics=sem)


def _sigmoid(x):
    return 0.5 * jnp.tanh(0.5 * x) + 0.5


def _rms_hat(x):
    rinv = lax.rsqrt(jnp.mean(x * x, axis=-1, keepdims=True) + RMS_EPS)
    return x * rinv, rinv


def _rms_bwd(dxhat, xhat, rinv):
    return rinv * (dxhat - xhat * jnp.mean(dxhat * xhat, axis=-1, keepdims=True))


def _rsum(a):
    return jnp.sum(a, axis=0, keepdims=True)


def _nt(a, b):
    return lax.dot_general(a, b, (((1,), (1,)), ((), ())), preferred_element_type=f32)


def _tn(a, b):
    return lax.dot_general(a, b, (((0,), (0,)), ((), ())), preferred_element_type=f32)


def _nn(a, b):
    return jnp.dot(a, b, preferred_element_type=f32)


def _swap16(x):
    lane = lax.broadcasted_iota(jnp.int32, x.shape, 1)
    n = x.shape[1]
    return jnp.where((lane % 32) < 16, pltpu.roll(x, n - 16, 1), pltpu.roll(x, 16, 1))


def _rope_tables(s_len, l_len):
    t = np.arange(s_len)
    inv = ROPE_THETA ** (-np.arange(ROPE_PAIRS, dtype=np.float32) / ROPE_PAIRS)
    ang_r = (t // GRID_W).astype(np.float32)[:, None] * inv
    ang_c = (t % GRID_W).astype(np.float32)[:, None] * inv
    cos = np.concatenate([np.cos(ang_r), np.cos(ang_r), np.cos(ang_c), np.cos(ang_c)], axis=-1)
    sin = np.concatenate([-np.sin(ang_r), np.sin(ang_r), -np.sin(ang_c), np.sin(ang_c)], axis=-1)
    cos = np.concatenate([cos, np.ones((l_len, HEAD_DIM), np.float32)], axis=0)
    sin = np.concatenate([sin, np.zeros((l_len, HEAD_DIM), np.float32)], axis=0)
    return (jnp.asarray(np.tile(cos, (1, 2)), f32), jnp.asarray(np.tile(sin, (1, 2)), f32))


def _pool_tables(tm, l_len):
    band = np.zeros((2, POOL_GROUPS, tm, tm), np.float32)
    inv = np.zeros((2, POOL_GROUPS, tm, 1), np.float32)
    for typ, length in ((0, GRID_W), (1, l_len)):
        for g, w in enumerate(POOL_WINDOWS):
            for t in range(tm):
                base, p = (t // length) * length, t % length
                lo = min(max(p - w // 2, 0), length)
                hi = min(max(p - w // 2 + w, 0), length)
                band[typ, g, t, base + lo:base + hi] = 1.0
                inv[typ, g, t, 0] = 1.0 / (hi - lo)
    return jnp.asarray(band, bf16), jnp.asarray(inv, f32)


NA_QR = 4
NA_WR = NA_KH + NA_QR - 1
NA_TYPES = 3
NA_SEL_ROWS = 136
NA_WPAD = 768


def _rpb_index_tables():
    j = np.arange(GRID_W)
    col_start = np.clip(j - NA_KW // 2, 0, GRID_W - NA_KW)
    valid = (j[None, :] >= col_start[:, None]) & (j[None, :] < col_start[:, None] + NA_KW)
    dc = np.clip(j[None, :] - j[:, None] + NA_KW - 1, 0, 2 * NA_KW - 2)
    i = np.arange(NA_QR)[:, None]
    kk = np.arange(NA_WR)[None, :]
    off = np.stack([np.zeros_like(i), i, np.full_like(i, NA_QR - 1)])
    d = np.stack([kk - i + NA_KH - 1, kk - i + NA_KH - 1 - NA_QR, kk - i])
    row_ok = (kk[None] >= off) & (kk[None] < off + NA_KH)
    assert (d[row_ok] >= 0).all() and (d[row_ok] <= 2 * NA_KH - 2).all()
    return valid, dc, d, row_ok


def _expand_rpb(rpb, name):
    _, _, d, row_ok = _rpb_index_tables()
    heads, nd, ne = rpb.shape
    w = GRID_W
    v = jnp.pad(rpb, ((0, 0), (0, 0), (w - NA_KW, 2 * w - (w - NA_KW) - ne)))
    x = jnp.broadcast_to(v[:, :, None, :], (heads, nd, w, 2 * w)).reshape(heads, nd, 2 * w * w)
    t = x[:, :, :w * (2 * w - 1)].reshape(heads, nd, w, 2 * w - 1)[..., w - 1:]

    def body(t_ref, o_ref):
        q = lax.broadcasted_iota(jnp.int32, (w, w), 0)
        c = lax.broadcasted_iota(jnp.int32, (w, w), 1)
        c0 = jnp.clip(q - NA_KW // 2, 0, w - NA_KW)
        in_cols = (c >= c0) & (c < c0 + NA_KW)
        outside = jnp.full((w, w), NEG_INF, f32)
        blocks = [jnp.where(in_cols, t_ref[dd], NEG_INF) for dd in range(nd)]
        for typ in range(NA_TYPES):
            for i in range(NA_QR):
                row = [blocks[d[typ, i, kk]] if row_ok[typ, i, kk] else outside for kk in range(NA_WR)]
                o_ref[typ, i * w:(i + 1) * w, :] = jnp.concatenate(row, axis=1)

    return pl.pallas_call(
        body, name=name, grid=(heads,),
        in_specs=[pl.BlockSpec((None, nd, w, w), lambda h: (h, 0, 0, 0))],
        out_specs=pl.BlockSpec((NA_TYPES, None, NA_QR * w, NA_WR * w), lambda h: (0, h, 0, 0)),
        out_shape=S_((NA_TYPES, heads, NA_QR * w, NA_WR * w), f32),
        compiler_params=_cp(VMEM_MID, ("arbitrary",)),
    )(t)


def _rpb_reduce_tables():
    _, _, d, row_ok = _rpb_index_tables()
    flip = np.eye(GRID_W, dtype=np.float32)[::-1].copy()
    sel = np.zeros((16, NA_SEL_ROWS), np.float32)
    flat_d, flat_ok = d.reshape(-1), row_ok.reshape(-1)
    for n in range(flat_d.size):
        if flat_ok[n]:
            sel[flat_d[n], n] = 1.0
    return jnp.asarray(flip), jnp.asarray(sel)


class _Cfg:
    def __init__(self, s_len, l_len, d, f):
        self.S, self.L, self.D, self.F = s_len, l_len, d, f
        self.T = s_len + l_len
        self.TM = 256 if l_len % 256 == 0 else 128
        assert l_len == self.TM, "context length must equal the row tile"
        assert s_len % self.TM == 0 and s_len % GRID_W == 0
        self.nxt = s_len // self.TM
        self.ntt = self.T // self.TM
        self.rows = s_len // GRID_W
        assert self.rows >= 2 * NA_KH
        assert f % (2 * LANES) == 0
        self.FC = f

    def ntiles(self, with_ctx):
        return self.ntt if with_ctx else self.nxt


def _typ(cfg):
    return lambda i: (jnp.minimum(i // cfg.nxt, 1), 0, 0)


def _mesh_pos():
    return lax.axis_index("x"), lax.axis_index("y"), lax.axis_index("c")


class _Comm:
    def __init__(self, ins, outs, alias, nsem, start, finish):
        self.ins, self.outs, self.alias, self.nsem, self.start, self.finish = ins, outs, alias, nsem, start, finish


def _call(body, args, comm=None, *, grid, in_specs, out_specs, out_shape, scratch_shapes=(), **kw):
    if comm is None:
        return pl.pallas_call(body, grid=grid, in_specs=list(in_specs), out_specs=list(out_specs), out_shape=list(out_shape),
                              scratch_shapes=list(scratch_shapes), **kw)(*args), ()
    n_in, n_out, n_sc = len(in_specs), len(out_specs), len(scratch_shapes)
    ci, co = len(comm.ins), len(comm.outs)

    def carrier(*refs):
        bounds = np.cumsum([0, n_in, ci, n_out, co, n_sc])
        ins, cins, outs, couts, scr = (refs[a:b] for a, b in zip(bounds[:-1], bounds[1:]))
        send, recv = refs[bounds[-1]], refs[bounds[-1] + 1]
        first = functools.reduce(jnp.logical_and, [pl.program_id(a) == 0 for a in range(len(grid))])
        last = functools.reduce(jnp.logical_and, [pl.program_id(a) == g - 1 for a, g in enumerate(grid)])

        @pl.when(first)
        def _():
            comm.start(cins, couts, send, recv)

        body(*ins, *outs, *scr)

        @pl.when(last)
        def _():
            comm.finish(cins, couts, send, recv)

    res = pl.pallas_call(
        carrier, grid=grid, in_specs=list(in_specs) + [ANY] * ci, out_specs=list(out_specs) + [ANY] * co,
        out_shape=list(out_shape) + list(comm.outs),
        input_output_aliases={n_in + a: n_out + b for a, b in comm.alias.items()},
        scratch_shapes=list(scratch_shapes) + [pltpu.SemaphoreType.DMA((comm.nsem,)), pltpu.SemaphoreType.DMA((comm.nsem,))],
        **kw)(*args, *comm.ins)
    return res[:n_out], res[n_out:]


def _comm_only(comm, name):
    ci, co = len(comm.ins), len(comm.outs)

    def body(*refs):
        cins, couts = refs[:ci], refs[ci:ci + co]
        send, recv = refs[ci + co], refs[ci + co + 1]
        comm.start(cins, couts, send, recv)
        comm.finish(cins, couts, send, recv)

    return pl.pallas_call(
        body, name=name, in_specs=[ANY] * ci, out_specs=[ANY] * co, out_shape=list(comm.outs),
        input_output_aliases=dict(comm.alias),
        scratch_shapes=[pltpu.SemaphoreType.DMA((comm.nsem,)), pltpu.SemaphoreType.DMA((comm.nsem,))],
        compiler_params=_cp(VMEM_MID),
    )(*comm.ins)


def _half_view(ref, axis, kk, h):
    r, c = ref.shape
    if axis == 1:
        n = c // N_CHIPS
        return ref.at[pl.ds(h * (r // 2), r // 2), pl.ds(pl.multiple_of(kk * n, LANES), n)]
    n = r // N_CHIPS
    return ref.at[pl.ds(pl.multiple_of(kk * n + h * (n // 2), 8), n // 2), :]


def _other_chips(x, y):
    return [(1 - x, y), (x, 1 - y), (1 - x, 1 - y)]


def _gather_comm(arrs, axes):
    n = len(arrs)

    def copy(ref, view, sems, k, to):
        send, recv = sems
        return pltpu.make_async_remote_copy(src_ref=view, dst_ref=view, send_sem=send.at[k], recv_sem=recv.at[k],
                                            device_id=to, device_id_type=MESH)

    def start(cins, bufs, send, recv):
        x, y, c = _mesh_pos()
        for t in range(n):
            own = _half_view(bufs[t], axes[t], 2 * x + y, c)
            for j, chip in enumerate(_other_chips(x, y)):
                copy(bufs[t], own, (send, recv), 6 * t + j, (*chip, c)).start()

    def finish(cins, bufs, send, recv):
        x, y, c = _mesh_pos()
        sibling = (x, y, 1 - c)
        chips = _other_chips(x, y)
        for t in range(n):
            for j, chip in enumerate(chips):
                landed = _half_view(bufs[t], axes[t], 2 * chip[0] + chip[1], c)
                copy(bufs[t], landed, (send, recv), 6 * t + j, (*chip, c)).wait_recv()
                copy(bufs[t], landed, (send, recv), 6 * t + 3 + j, sibling).start()
        for t in range(n):
            own = _half_view(bufs[t], axes[t], 2 * x + y, c)
            for j, chip in enumerate(chips):
                kj = 2 * chip[0] + chip[1]
                copy(bufs[t], _half_view(bufs[t], axes[t], kj, 1 - c), (send, recv), 6 * t + 3 + j, sibling).wait_recv()
                copy(bufs[t], own, (send, recv), 6 * t + j, (*chip, c)).wait_send()
                copy(bufs[t], _half_view(bufs[t], axes[t], kj, c), (send, recv), 6 * t + 3 + j, sibling).wait_send()

    return _Comm(list(arrs), [S_(a.shape, a.dtype) for a in arrs], {t: t for t in range(n)}, 6 * n, start, finish)


def _scatter_comm(parts, axes):
    n = len(parts)
    peers = [(fx, fy, fc) for fx in (0, 1) for fy in (0, 1) for fc in (0, 1)][1:]

    def half_shape(a, axis):
        r, c = a.shape
        return (r // 2, c // N_CHIPS) if axis == 1 else (r // N_CHIPS // 2, c)

    def start(srcs, lands, send, recv):
        x, y, c = _mesh_pos()
        me = 4 * x + 2 * y + c
        for t in range(n):
            for r_, (fx, fy, fc) in enumerate(peers):
                dx, dy, dc = (1 - x if fx else x), (1 - y if fy else y), (1 - c if fc else c)
                pltpu.make_async_remote_copy(
                    src_ref=_half_view(srcs[t], axes[t], 2 * dx + dy, dc), dst_ref=lands[t].at[me],
                    send_sem=send.at[7 * t + r_], recv_sem=recv.at[7 * t + r_],
                    device_id=(dx, dy, dc), device_id_type=MESH).start()

    def finish(srcs, lands, send, recv):
        x, y, c = _mesh_pos()
        for t in range(n):
            mine = _half_view(srcs[t], axes[t], 2 * x + y, c)
            for r_, (fx, fy, fc) in enumerate(peers):
                sx, sy, sc = (1 - x if fx else x), (1 - y if fy else y), (1 - c if fc else c)
                cp = pltpu.make_async_remote_copy(
                    src_ref=mine, dst_ref=lands[t].at[4 * sx + 2 * sy + sc],
                    send_sem=send.at[7 * t + r_], recv_sem=recv.at[7 * t + r_],
                    device_id=(sx, sy, sc), device_id_type=MESH)
                cp.wait_recv()
                cp.wait_send()

    return _Comm(list(parts), [S_((N_DEV,) + half_shape(a, ax), a.dtype) for a, ax in zip(parts, axes)], {}, 7 * n, start, finish)


def _allgather_comm(block):
    peers = [(fx, fy, fc) for fx in (0, 1) for fy in (0, 1) for fc in (0, 1)][1:]

    def ends(x, y, c):
        for r_, (fx, fy, fc) in enumerate(peers):
            yield r_, ((1 - x if fx else x), (1 - y if fy else y), (1 - c if fc else c))

    def start(srcs, lands, send, recv):
        x, y, c = _mesh_pos()
        for r_, peer in ends(x, y, c):
            pltpu.make_async_remote_copy(src_ref=srcs[0], dst_ref=lands[0].at[4 * x + 2 * y + c], send_sem=send.at[r_],
                                         recv_sem=recv.at[r_], device_id=peer, device_id_type=MESH).start()

    def finish(srcs, lands, send, recv):
        x, y, c = _mesh_pos()
        for r_, (px, py, pc) in ends(x, y, c):
            cp = pltpu.make_async_remote_copy(src_ref=srcs[0], dst_ref=lands[0].at[4 * px + 2 * py + pc], send_sem=send.at[r_],
                                              recv_sem=recv.at[r_], device_id=(px, py, pc), device_id_type=MESH)
            cp.wait_recv()
            cp.wait_send()

    return _Comm([block], [S_((N_DEV,) + block.shape, block.dtype)], {}, len(peers), start, finish)


def _ffn_fwd(cfg, xs, mods, gvec, wgu, wd, mi, gi, with_ctx, name, comm=None, xs_ctx=None, loss_target=None):
    TM, D, F, FC = cfg.TM, cfg.D, cfg.F, cfg.FC
    nt = cfg.ntiles(with_ctx)
    R = nt * TM
    split, head = xs_ctx is not None, loss_target is not None

    def body(*refs):
        it = iter(refs)
        xs_ref = next(it)
        xc_ref = next(it) if split else None
        mods_ref, g_ref, wgu_hbm, wd_hbm = next(it), next(it), next(it), next(it)
        t_ref = next(it) if head else None
        out_ref, hb_ref, z_ref, y_ref = next(it), next(it), next(it), next(it)
        loss_ref = next(it) if head else None
        wgu_v, wd_v, sem = next(it), next(it), next(it)
        i = pl.program_id(0)

        @pl.when(i == 0)
        def _():
            c0 = pltpu.make_async_copy(wgu_hbm, wgu_v, sem.at[0])
            c1 = pltpu.make_async_copy(wd_hbm, wd_v, sem.at[1])
            c0.start(); c1.start(); c0.wait(); c1.wait()
            if head:
                loss_ref[...] = jnp.zeros_like(loss_ref)
        x = xs_ref[...]
        if split:
            x = jnp.where(i < cfg.nxt, x, xc_ref[...])
        m = mods_ref[0]
        sh, sc, gt = m[mi:mi + 1], m[mi + 1:mi + 2], m[mi + 2:mi + 3]
        xhat, _ = _rms_hat(x)
        h = (xhat * g_ref[gi:gi + 1]) * (1.0 + sc) + sh
        hb = h.astype(bf16)
        hb_ref[...] = hb
        y = jnp.zeros((TM, D), f32)
        for ch in range(F // FC):
            zg = _nn(hb, wgu_v[:, ch * FC:(ch + 1) * FC])
            zu = _nn(hb, wgu_v[:, F + ch * FC:F + (ch + 1) * FC])
            z_ref[:, ch * FC:(ch + 1) * FC] = zg.astype(bf16)
            z_ref[:, F + ch * FC:F + (ch + 1) * FC] = zu.astype(bf16)
            a = (zg * _sigmoid(zg)) * zu
            y = y + _nn(a.astype(bf16), wd_v[ch * FC:(ch + 1) * FC, :])
        y_ref[...] = y
        yhat, _ = _rms_hat(y)
        out = x + 0.5 * gt * (yhat * g_ref[gi + 1:gi + 2])
        if head:
            e = out - t_ref[...]
            out_ref[...] = e * (1.0 / D)
            loss_ref[...] += jnp.sum(jnp.mean(e * e, axis=-1, keepdims=True), axis=0, keepdims=True) * 0.5
        else:
            out_ref[...] = out

    rt = lambda c: pl.BlockSpec((TM, c), lambda i: (i, 0))
    lat = pl.BlockSpec((TM, D), lambda i: (jnp.minimum(i, cfg.nxt - 1), 0))
    x_specs, x_args = ([lat, pl.BlockSpec((TM, D), lambda i: (0, 0))], [xs, xs_ctx]) if split else ([rt(D)], [xs])
    t_specs, t_args = ([rt(D)], [loss_target]) if head else ([], [])
    l_specs, l_shape = ([pl.BlockSpec((8, LANES), lambda i: (0, 0))], [S_((8, LANES), f32)]) if head else ([], [])
    return _call(
        body, (*x_args, mods, gvec, wgu, wd, *t_args), comm, name=name, grid=(nt,),
        in_specs=x_specs + [pl.BlockSpec((1, N_MOD, D), _typ(cfg)), pl.BlockSpec((6, D), lambda i: (0, 0)), ANY, ANY] + t_specs,
        out_specs=[rt(D), rt(D), rt(2 * F), rt(D)] + l_specs,
        out_shape=[S_((R, D), f32), S_((R, D), bf16), S_((R, 2 * F), bf16), S_((R, D), f32)] + l_shape,
        scratch_shapes=[pltpu.VMEM((D, 2 * F), bf16), pltpu.VMEM((F, D), bf16), pltpu.SemaphoreType.DMA((2,))],
        compiler_params=_cp(VMEM_BIG, ("arbitrary",)),
    )


def _ffn_bwd(cfg, dout, xs, z, y, mods, gvec, wgu, wd, mi, gi, with_ctx, name, comm=None, xs_ctx=None):
    TM, D, F, FC = cfg.TM, cfg.D, cfg.F, cfg.FC
    nt = cfg.ntiles(with_ctx)
    R = nt * TM
    ntyp = 2 if with_ctx else 1
    split = xs_ctx is not None

    def body(*refs):
        it = iter(refs)
        do_ref, xs_ref = next(it), next(it)
        xc_ref = next(it) if split else None
        z_ref, y_ref, mods_ref, g_ref, wgu_hbm, wd_hbm = (next(it) for _ in range(6))
        dx_ref, dz_ref, dy_ref, a_ref, dm_ref, dg_ref, wgu_v, wd_v, sem = (next(it) for _ in range(9))
        i = pl.program_id(0)

        @pl.when(i == 0)
        def _():
            c0 = pltpu.make_async_copy(wgu_hbm, wgu_v, sem.at[0])
            c1 = pltpu.make_async_copy(wd_hbm, wd_v, sem.at[1])
            c0.start(); c1.start(); c0.wait(); c1.wait()
            dg_ref[...] = jnp.zeros_like(dg_ref)

        @pl.when((i == 0) | (i == cfg.nxt))
        def _():
            dm_ref[...] = jnp.zeros_like(dm_ref)

        do = do_ref[...]
        x = xs_ref[...]
        if split:
            x = jnp.where(i < cfg.nxt, x, xc_ref[...])
        m = mods_ref[0]
        sc, gt = m[mi + 1:mi + 2], m[mi + 2:mi + 3]
        g_pre, g_post = g_ref[gi:gi + 1], g_ref[gi + 1:gi + 2]
        xhat, rinv0 = _rms_hat(x)
        n0 = xhat * g_pre
        yhat, rinv1 = _rms_hat(y_ref[...])
        d_gt = _rsum(0.5 * do * (yhat * g_post))
        dr = (0.5 * gt) * do
        dg_post = _rsum(dr * yhat)
        dy = _rms_bwd(dr * g_post, yhat, rinv1)
        dyb = dy.astype(bf16)
        dy_ref[...] = dyb
        dh = jnp.zeros((TM, D), f32)
        for ch in range(F // FC):
            zg = z_ref[:, ch * FC:(ch + 1) * FC].astype(f32)
            zu = z_ref[:, F + ch * FC:F + (ch + 1) * FC].astype(f32)
            sg = _sigmoid(zg)
            silu = zg * sg
            a_ref[:, ch * FC:(ch + 1) * FC] = (silu * zu).astype(bf16)
            da = _nt(dyb, wd_v[ch * FC:(ch + 1) * FC, :])
            dzu = (da * silu).astype(bf16)
            dzg = (da * zu * (sg * (1.0 + zg * (1.0 - sg)))).astype(bf16)
            dz_ref[:, ch * FC:(ch + 1) * FC] = dzg
            dz_ref[:, F + ch * FC:F + (ch + 1) * FC] = dzu
            dh = dh + _nt(dzg, wgu_v[:, ch * FC:(ch + 1) * FC]) + _nt(dzu, wgu_v[:, F + ch * FC:F + (ch + 1) * FC])
        d_sh = _rsum(dh)
        d_sc = _rsum(dh * n0)
        dn = dh * (1.0 + sc)
        dg_pre = _rsum(dn * xhat)
        dx = do + _rms_bwd(dn * g_pre, xhat, rinv0)
        if split:
            @pl.when(i < cfg.nxt)
            def _():
                dx_ref[...] = dx
        else:
            dx_ref[...] = dx
        dm_ref[0] += jnp.concatenate([d_sh, d_sc, d_gt], axis=0)
        dg_ref[...] += jnp.concatenate([dg_pre, dg_post], axis=0)

    rt = lambda c: pl.BlockSpec((TM, c), lambda i: (i, 0))
    lat = pl.BlockSpec((TM, D), lambda i: (jnp.minimum(i, cfg.nxt - 1), 0))
    x_specs, x_args = ([lat, pl.BlockSpec((TM, D), lambda i: (0, 0))], [xs, xs_ctx]) if split else ([rt(D)], [xs])
    return _call(
        body, (dout, *x_args, z, y, mods, gvec, wgu, wd), comm, name=name, grid=(nt,),
        in_specs=[rt(D)] + x_specs + [rt(2 * F), rt(D), pl.BlockSpec((1, N_MOD, D), _typ(cfg)),
                                       pl.BlockSpec((6, D), lambda i: (0, 0)), ANY, ANY],
        out_specs=[lat if split else rt(D), rt(2 * F), rt(D), rt(F), pl.BlockSpec((1, 3, D), _typ(cfg)),
                   pl.BlockSpec((2, D), lambda i: (0, 0))],
        out_shape=[S_((cfg.S if split else R, D), f32), S_((R, 2 * F), bf16), S_((R, D), bf16), S_((R, F), bf16),
                   S_((ntyp, 3, D), f32), S_((2, D), f32)],
        scratch_shapes=[pltpu.VMEM((D, 2 * F), bf16), pltpu.VMEM((F, D), bf16), pltpu.SemaphoreType.DMA((2,))],
        compiler_params=_cp(VMEM_BIG, ("arbitrary",)),
    )


def _wgrad(a, b, k_rows, name, comm=None):
    M, N = a.shape[1], b.shape[1]
    tn = _div_tile(N, 1, WGRAD_TN, LANES) if N > WGRAD_TN // 2 else N
    tn = N // 2 if tn == N and N % (2 * LANES) == 0 else tn
    room = VMEM_BIG - WGRAD_SLACK - 2 * M * tn * 6
    tk = _div_tile(k_rows, 1, min(WGRAD_TK, room // (4 * (M + tn))), LANES)
    nk = k_rows // tk

    def body(a_ref, b_ref, o_ref, ob_ref):
        k = pl.program_id(1)

        @pl.when(k == 0)
        def _():
            o_ref[...] = jnp.zeros_like(o_ref)
        o_ref[...] += _tn(a_ref[...], b_ref[...])

        @pl.when(k == nk - 1)
        def _():
            ob_ref[...] = o_ref[...].astype(bf16)

    ospec = pl.BlockSpec((M, tn), lambda n, k: (0, n))
    return _call(
        body, (a, b), comm, name=name, grid=(N // tn, nk),
        in_specs=[pl.BlockSpec((tk, M), lambda n, k: (k, 0)), pl.BlockSpec((tk, tn), lambda n, k: (k, n))],
        out_specs=[ospec, ospec], out_shape=[S_((M, N), f32), S_((M, N), bf16)],
        compiler_params=_cp(VMEM_BIG, ("arbitrary", "arbitrary")),
    )


def _tmpre_fwd(cfg, xs, mods, gvec, w_in, cos, sin, name):
    TM, D = cfg.TM, cfg.D
    nt, R = cfg.ntt, cfg.T
    W = NA_WIDTH

    def body(xs_ref, mods_ref, g_ref, w_ref, cos_ref, sin_ref, hb_ref, q_ref, k_ref, v_ref, u_ref):
        x = xs_ref[...]
        m = mods_ref[0]
        xhat, _ = _rms_hat(x)
        hb = ((xhat * g_ref[2:3]) * (1.0 + m[4:5]) + m[3:4]).astype(bf16)
        hb_ref[...] = hb
        p = _nn(hb, w_ref[...])
        cs = jnp.tile(cos_ref[...], (1, W // LANES))
        sn = jnp.tile(sin_ref[...], (1, W // LANES))
        q = p[:, 0:W]
        k = p[:, W:2 * W]
        q_ref[...] = ((q * cs + _swap16(q) * sn) * (HEAD_DIM ** -0.5)).astype(bf16)
        k_ref[...] = (k * cs + _swap16(k) * sn).astype(bf16)
        v_ref[...] = p[:, 2 * W:3 * W].astype(bf16)
        u_ref[...] = p[:, 3 * W:]

    rt = lambda c: pl.BlockSpec((TM, c), lambda i: (i, 0))
    return pl.pallas_call(
        body, name=name, grid=(nt,),
        in_specs=[rt(D), pl.BlockSpec((1, N_MOD, D), _typ(cfg)), pl.BlockSpec((6, D), lambda i: (0, 0)),
                  pl.BlockSpec((D, IN_WIDTH), lambda i: (0, 0)), rt(LANES), rt(LANES)],
        out_specs=[rt(D), rt(W), rt(W), rt(W), rt(POOL_WIDTH)],
        out_shape=[S_((R, D), bf16), S_((R, W), bf16), S_((R, W), bf16), S_((R, W), bf16), S_((R, POOL_WIDTH), f32)],
        compiler_params=_cp(VMEM_MID, ("arbitrary",)),
    )(xs, mods, gvec, w_in, cos, sin)


def _tmpre_bwd(cfg, lat, ctx_terms, du_has_ctx, cos, sin, w_in, xs, mods, gvec, dres, res_with_ctx, name):
    TM, D = cfg.TM, cfg.D
    nt, R = cfg.ntt, cfg.T
    nres = cfg.ntiles(res_with_ctx)
    W = NA_WIDTH
    n_ctx = [len(t) for t in ctx_terms]
    flat_ctx = [a for t in ctx_terms for a in t]
    n_asm = 4 + len(flat_ctx) + 2

    def assemble(refs, o_ref):
        dq_ref, dk_ref, dv_ref, du_ref = refs[:4]
        ctx_refs = refs[4:4 + len(flat_ctx)]
        cos_ref, sin_ref = refs[4 + len(flat_ctx):]
        is_ctx = pl.program_id(0) >= cfg.nxt
        vals, off = [], 0
        for lat_ref, n in zip((dq_ref, dk_ref, dv_ref), n_ctx):
            cv = jnp.zeros((TM, W), f32)
            for r_ in ctx_refs[off:off + n]:
                cv = cv + r_[...]
            off += n
            vals.append(jnp.where(is_ctx, cv, lat_ref[...]))
        du_ = du_ref[...] if du_has_ctx else jnp.where(is_ctx, 0.0, du_ref[...])
        cs = jnp.tile(cos_ref[...], (1, W // LANES))
        sn = jnp.tile(sin_ref[...], (1, W // LANES))
        dq_ = vals[0] * (HEAD_DIM ** -0.5)
        dk_ = vals[1]
        o_ref[:, 0:W] = (dq_ * cs + _swap16(dq_ * sn)).astype(bf16)
        o_ref[:, W:2 * W] = (dk_ * cs + _swap16(dk_ * sn)).astype(bf16)
        o_ref[:, 2 * W:3 * W] = vals[2].astype(bf16)
        o_ref[:, 3 * W:] = du_.astype(bf16)

    def body(*refs):
        w_ref, xs_ref, mods_ref, g_ref, dres_ref, dx_ref, dp_ref, dm_ref, dg_ref = refs[n_asm:]
        i = pl.program_id(0)

        @pl.when(i == 0)
        def _():
            dg_ref[...] = jnp.zeros_like(dg_ref)

        @pl.when((i == 0) | (i == cfg.nxt))
        def _():
            dm_ref[...] = jnp.zeros_like(dm_ref)

        assemble(refs[:n_asm], dp_ref)
        dh = _nt(dp_ref[...], w_ref[...])
        x = xs_ref[...]
        m = mods_ref[0]
        g2 = g_ref[2:3]
        xhat, rinv = _rms_hat(x)
        d_sh = _rsum(dh)
        d_sc = _rsum(dh * (xhat * g2))
        dn = dh * (1.0 + m[4:5])
        dg_ref[...] += _rsum(dn * xhat)
        dx = _rms_bwd(dn * g2, xhat, rinv)
        res = dres_ref[...]
        if nres < nt:
            res = jnp.where(i < nres, res, 0.0)
        dx_ref[...] = res + dx
        dm_ref[0] += jnp.concatenate([d_sh, d_sc], axis=0)

    rt = lambda c: pl.BlockSpec((TM, c), lambda i: (i, 0))
    lat_spec = pl.BlockSpec((TM, W), lambda i: (jnp.minimum(i, cfg.nxt - 1), 0))
    du_spec = rt(POOL_WIDTH) if du_has_ctx else lat_spec
    asm_specs = ([lat_spec, lat_spec, lat_spec, du_spec] + [pl.BlockSpec((TM, W), lambda i: (0, 0))] * len(flat_ctx)
                 + [rt(LANES), rt(LANES)])
    return pl.pallas_call(
        body, name=name, grid=(nt,),
        in_specs=asm_specs + [pl.BlockSpec((D, IN_WIDTH), lambda i: (0, 0)), rt(D),
                              pl.BlockSpec((1, N_MOD, D), _typ(cfg)), pl.BlockSpec((6, D), lambda i: (0, 0)),
                              pl.BlockSpec((TM, D), lambda i: (jnp.minimum(i, nres - 1), 0))],
        out_specs=[rt(D), rt(IN_WIDTH), pl.BlockSpec((1, 2, D), _typ(cfg)), pl.BlockSpec((1, D), lambda i: (0, 0))],
        out_shape=[S_((R, D), f32), S_((R, IN_WIDTH), bf16), S_((2, 2, D), f32), S_((1, D), f32)],
        compiler_params=_cp(VMEM_MID, ("arbitrary",)),
    )(*lat, *flat_ctx, cos, sin, w_in, xs, mods, gvec, dres)


def _na_block(cfg, b):
    return jnp.clip(NA_QR * b - NA_KH // 2, 0, cfg.rows - NA_WR)


def _na_load_bias(b, nb, b_hbm, b_v, sem):
    for typ, at in ((0, 0), (1, 1), (2, nb - 1)):
        @pl.when(b == at)
        def _(typ=typ):
            cp = pltpu.make_async_copy(b_hbm.at[typ], b_v, sem)
            cp.start()
            cp.wait()


def _na_exps(qh, klh, kch, bias):
    s_loc = _nt(qh, klh) + bias
    s_ctx = _nt(qh, kch)
    mx = jnp.maximum(jnp.max(s_loc, axis=-1, keepdims=True), jnp.max(s_ctx, axis=-1, keepdims=True))
    e_loc = jnp.exp(s_loc - mx)
    e_ctx = jnp.exp(s_ctx - mx)
    inv = 1.0 / (jnp.sum(e_loc, axis=-1, keepdims=True) + jnp.sum(e_ctx, axis=-1, keepdims=True))
    return e_loc, e_ctx, inv


def _na_probs(qh, klh, kch, bias):
    e_loc, e_ctx, inv = _na_exps(qh, klh, kch, bias)
    return e_loc * inv, e_ctx * inv


def _na_fwd(cfg, q, k, v, bexp, name, comm=None):
    S, L, T = cfg.S, cfg.L, cfg.T
    NQ, NW = NA_QR * GRID_W, NA_WR * GRID_W
    nb = cfg.rows // NA_QR

    def body(q_ref, k_hbm, v_hbm, b_hbm, o_ref, k_v, v_v, b_v, sem):
        b = pl.program_id(0)

        @pl.when(b == 0)
        def _():
            cs = [pltpu.make_async_copy(k_hbm, k_v, sem.at[0]), pltpu.make_async_copy(v_hbm, v_v, sem.at[1])]
            for c_ in cs:
                c_.start()
            for c_ in cs:
                c_.wait()

        _na_load_bias(b, nb, b_hbm, b_v, sem.at[2])
        st = pl.multiple_of(_na_block(cfg, b) * GRID_W, GRID_W)
        first = lax.broadcasted_iota(jnp.int32, (NQ, LANES), 1) < HEAD_DIM
        for hp in range(NA_HEADS // 2):
            ls = slice(hp * LANES, (hp + 1) * LANES)
            q2 = q_ref[:, ls]
            kl, vl = k_v[pl.ds(st, NW), ls], v_v[pl.ds(st, NW), ls]
            kc, vc = k_v[S:T, ls], v_v[S:T, ls]
            o2 = []
            for hh in range(2):
                qm = jnp.where(first if hh == 0 else ~first, q2, jnp.zeros_like(q2))
                e_loc, e_ctx, inv = _na_exps(qm, kl, kc, b_v[2 * hp + hh])
                o2.append((_nn(e_loc.astype(bf16), vl) + _nn(e_ctx.astype(bf16), vc)) * inv)
            o_ref[:, ls] = jnp.where(first, o2[0], o2[1]).astype(bf16)

    return _call(
        body, (q, k, v, bexp), comm, name=name, grid=(nb,),
        in_specs=[pl.BlockSpec((NQ, NA_WIDTH), lambda b: (b, 0)), ANY, ANY, ANY],
        out_specs=[pl.BlockSpec((NQ, NA_WIDTH), lambda b: (b, 0))],
        out_shape=[S_((S, NA_WIDTH), bf16)],
        scratch_shapes=[pltpu.VMEM((T, NA_WIDTH), bf16), pltpu.VMEM((T, NA_WIDTH), bf16),
                        pltpu.VMEM((NA_HEADS, NQ, NW), f32), pltpu.SemaphoreType.DMA((3,))],
        compiler_params=_cp(VMEM_MID, ("arbitrary",)),
    )


def _na_bwd(cfg, do, q, k, v, bexp, name, comm=None):
    S, L, T, rows = cfg.S, cfg.L, cfg.T, cfg.rows
    NQ, NW = NA_QR * GRID_W, NA_WR * GRID_W
    NSLOT = 2 * NA_KH
    nb = rows // NA_QR
    bmax = (rows - NA_WR) // NA_QR
    steps = 2 * nb - bmax
    W = NA_WIDTH
    assert nb >= 3 and bmax >= 1 and rows - NA_QR * bmax <= NSLOT

    def out_group(g):
        return jnp.where(g >= nb, g - nb + bmax, jnp.clip(g - 1, 0, bmax - 1))

    def body(do_ref, q_ref, k_hbm, v_hbm, b_hbm, dq_ref, dk_ref, dv_ref, dkc_ref, dvc_ref, db_hbm,
             k_v, v_v, b_v, db_v, ak, av, akc, avc, sem):
        g = pl.program_id(0)

        @pl.when(g == 0)
        def _():
            cs = [pltpu.make_async_copy(k_hbm, k_v, sem.at[0]), pltpu.make_async_copy(v_hbm, v_v, sem.at[1])]
            for c_ in cs:
                c_.start()
            db_v[...] = jnp.zeros_like(db_v)
            ak[...] = jnp.zeros_like(ak)
            av[...] = jnp.zeros_like(av)
            akc[...] = jnp.zeros_like(akc)
            avc[...] = jnp.zeros_like(avc)
            for c_ in cs:
                c_.wait()

        for typ, at in ((0, 1), (1, nb - 1)):
            @pl.when(g == at)
            def _(typ=typ):
                cp = pltpu.make_async_copy(db_v, db_hbm.at[typ], sem.at[2])
                cp.start()
                cp.wait()
                db_v[...] = jnp.zeros_like(db_v)

        @pl.when(g < nb)
        def _():
            _na_load_bias(g, nb, b_hbm, b_v, sem.at[2])
            ws = _na_block(cfg, g)
            st = pl.multiple_of(ws * GRID_W, GRID_W)
            first = lax.broadcasted_iota(jnp.int32, (NQ, LANES), 1) < HEAD_DIM
            for hp in range(NA_HEADS // 2):
                ls = slice(hp * LANES, (hp + 1) * LANES)
                q2, do2 = q_ref[:, ls], do_ref[:, ls]
                kl, vl = k_v[pl.ds(st, NW), ls], v_v[pl.ds(st, NW), ls]
                kc, vc = k_v[S:T, ls], v_v[S:T, ls]
                dq2 = []
                dk2 = jnp.zeros((NW, LANES), f32)
                dv2 = jnp.zeros((NW, LANES), f32)
                dkc2 = jnp.zeros((L, LANES), f32)
                dvc2 = jnp.zeros((L, LANES), f32)
                for hh in range(2):
                    keep = first if hh == 0 else ~first
                    qm = jnp.where(keep, q2, jnp.zeros_like(q2))
                    dom = jnp.where(keep, do2, jnp.zeros_like(do2))
                    p_loc, p_ctx = _na_probs(qm, kl, kc, b_v[2 * hp + hh])
                    dp_loc = _nt(dom, vl)
                    dp_ctx = _nt(dom, vc)
                    delta = jnp.sum(p_loc * dp_loc, axis=-1, keepdims=True) + jnp.sum(p_ctx * dp_ctx, axis=-1, keepdims=True)
                    ds_loc = p_loc * (dp_loc - delta)
                    ds_ctx = p_ctx * (dp_ctx - delta)
                    db_v[2 * hp + hh, :, 0:NW] += ds_loc
                    dsl, dsc = ds_loc.astype(bf16), ds_ctx.astype(bf16)
                    dq2.append(_nn(dsl, kl) + _nn(dsc, kc))
                    dk2 = dk2 + _tn(dsl, qm)
                    dv2 = dv2 + _tn(p_loc.astype(bf16), dom)
                    dkc2 = dkc2 + _tn(dsc, qm)
                    dvc2 = dvc2 + _tn(p_ctx.astype(bf16), dom)
                dq_ref[:, ls] = jnp.where(first, dq2[0], dq2[1])
                akc[:, ls] += dkc2
                avc[:, ls] += dvc2
                for kk in range(NA_WR):
                    slot = (ws + kk) % NSLOT
                    ak[slot, :, ls] += dk2[kk * GRID_W:(kk + 1) * GRID_W, :]
                    av[slot, :, ls] += dv2[kk * GRID_W:(kk + 1) * GRID_W, :]

        @pl.when(((g >= 1) & (g <= bmax)) | (g >= nb))
        def _():
            base = NA_QR * (out_group(g) % (NSLOT // NA_QR))
            for t in range(NA_QR):
                dk_ref[t * GRID_W:(t + 1) * GRID_W, :] = ak[base + t]
                dv_ref[t * GRID_W:(t + 1) * GRID_W, :] = av[base + t]
                ak[base + t] = jnp.zeros((GRID_W, W), f32)
                av[base + t] = jnp.zeros((GRID_W, W), f32)

        @pl.when(g == nb - 1)
        def _():
            cp = pltpu.make_async_copy(db_v, db_hbm.at[2], sem.at[2])
            cp.start()
            cp.wait()

        @pl.when(g == steps - 1)
        def _():
            dkc_ref[...] = akc[...]
            dvc_ref[...] = avc[...]

    qmap = lambda g: (jnp.minimum(g, nb - 1), 0)
    kmap = lambda g: (out_group(g), 0)
    full = lambda g: (0, 0)
    return _call(
        body, (do, q, k, v, bexp), comm, name=name, grid=(steps,),
        in_specs=[pl.BlockSpec((NQ, W), qmap), pl.BlockSpec((NQ, W), qmap), ANY, ANY, ANY],
        out_specs=[pl.BlockSpec((NQ, W), qmap), pl.BlockSpec((NQ, W), kmap), pl.BlockSpec((NQ, W), kmap),
                   pl.BlockSpec((L, W), full), pl.BlockSpec((L, W), full), ANY],
        out_shape=[S_((S, W), f32), S_((S, W), f32), S_((S, W), f32), S_((L, W), f32), S_((L, W), f32),
                   S_((NA_TYPES, NA_HEADS, NQ, NA_WPAD), f32)],
        scratch_shapes=[pltpu.VMEM((T, W), bf16), pltpu.VMEM((T, W), bf16),
                        pltpu.VMEM((NA_HEADS, NQ, NW), f32), pltpu.VMEM((NA_HEADS, NQ, NA_WPAD), f32),
                        pltpu.VMEM((NSLOT, GRID_W, W), f32), pltpu.VMEM((NSLOT, GRID_W, W), f32),
                        pltpu.VMEM((L, W), f32), pltpu.VMEM((L, W), f32), pltpu.SemaphoreType.DMA((3,))],
        compiler_params=_cp(VMEM_BIG, ("arbitrary",)),
    )


def _rpb_reduce(dbias, flip, sel, name):
    nq, w = NA_QR * GRID_W, GRID_W

    def diag_body(x_ref, j_ref, o_ref):
        rows = []
        for i in range(NA_QR):
            xr = jnp.dot(j_ref[...], x_ref[i * w:(i + 1) * w, :], preferred_element_type=f32, precision=lax.Precision.HIGHEST)
            rows.append(jnp.sum(pltpu.roll(xr, 0, 1, stride=1, stride_axis=0), axis=0, keepdims=True))
        o_ref[...] = jnp.concatenate(rows + [jnp.zeros((8 - NA_QR, NA_WPAD), f32)], axis=0)

    diag = pl.pallas_call(
        diag_body, name=name + "_diag", grid=(NA_TYPES, NA_HEADS),
        in_specs=[pl.BlockSpec((None, None, nq, NA_WPAD), lambda t, h: (t, h, 0, 0)), pl.BlockSpec((w, w), lambda t, h: (0, 0))],
        out_specs=pl.BlockSpec((None, None, 8, NA_WPAD), lambda t, h: (t, h, 0, 0)),
        out_shape=S_((NA_TYPES, NA_HEADS, 8, NA_WPAD), f32),
        compiler_params=_cp(VMEM_MID, ("arbitrary", "arbitrary")),
    )(dbias, flip)
    lo = w - NA_KW
    y = diag[:, :, :NA_QR, lo:lo + NA_WR * w].reshape(NA_TYPES, NA_HEADS, NA_QR, NA_WR, w)
    y = jnp.transpose(y, (1, 0, 2, 3, 4)).reshape(NA_HEADS, NA_TYPES * NA_QR * NA_WR, w)
    y = jnp.pad(y, ((0, 0), (0, NA_SEL_ROWS - y.shape[1]), (0, LANES - w)))

    def body(y_ref, sel_ref, o_ref):
        o_ref[...] = jnp.dot(sel_ref[...], y_ref[...], preferred_element_type=f32, precision=lax.Precision.HIGHEST)

    return pl.pallas_call(
        body, name=name, grid=(NA_HEADS,),
        in_specs=[pl.BlockSpec((None, NA_SEL_ROWS, LANES), lambda h: (h, 0, 0)), pl.BlockSpec((16, NA_SEL_ROWS), lambda h: (0, 0))],
        out_specs=pl.BlockSpec((None, 16, LANES), lambda h: (h, 0, 0)),
        out_shape=S_((NA_HEADS, 16, LANES), f32),
        compiler_params=_cp(VMEM_MID, ("arbitrary",)),
    )(y, sel)


def _ctx_attn_fwd(cfg, q, k, v, name):
    L = cfg.L
    blk = cfg.S // L

    def body(q_ref, k_ref, v_ref, o_ref):
        qv, kv, vv = q_ref[...], k_ref[...], v_ref[...]
        outs = []
        for h in range(NA_HEADS):
            hs = slice(h * HEAD_DIM, (h + 1) * HEAD_DIM)
            s = _nt(qv[:, hs], kv[:, hs])
            e = jnp.exp(s - jnp.max(s, axis=-1, keepdims=True))
            p = e * (1.0 / jnp.sum(e, axis=-1, keepdims=True))
            outs.append(_nn(p.astype(bf16), vv[:, hs]))
        o_ref[...] = jnp.concatenate(outs, axis=-1).astype(bf16)

    spec = pl.BlockSpec((L, NA_WIDTH), lambda i: (blk, 0))
    return pl.pallas_call(
        body, name=name, grid=(1,), in_specs=[spec, spec, spec],
        out_specs=pl.BlockSpec((L, NA_WIDTH), lambda i: (0, 0)), out_shape=S_((L, NA_WIDTH), bf16),
        compiler_params=_cp(VMEM_MID, ("arbitrary",)),
    )(q, k, v)


def _ctx_attn_bwd(cfg, do, q, k, v, name):
    L = cfg.L
    blk = cfg.S // L

    def body(do_ref, q_ref, k_ref, v_ref, dq_ref, dk_ref, dv_ref):
        dov, qv, kv, vv = do_ref[...], q_ref[...], k_ref[...], v_ref[...]
        dqs, dks, dvs = [], [], []
        for h in range(NA_HEADS):
            hs = slice(h * HEAD_DIM, (h + 1) * HEAD_DIM)
            qh, kh, doh = qv[:, hs], kv[:, hs], dov[:, hs]
            s = _nt(qh, kh)
            e = jnp.exp(s - jnp.max(s, axis=-1, keepdims=True))
            p = e * (1.0 / jnp.sum(e, axis=-1, keepdims=True))
            dp = _nt(doh, vv[:, hs])
            ds = (p * (dp - jnp.sum(p * dp, axis=-1, keepdims=True))).astype(bf16)
            dqs.append(_nn(ds, kh))
            dks.append(_tn(ds, qh))
            dvs.append(_tn(p.astype(bf16), doh))
        dq_ref[...] = jnp.concatenate(dqs, axis=-1)
        dk_ref[...] = jnp.concatenate(dks, axis=-1)
        dv_ref[...] = jnp.concatenate(dvs, axis=-1)

    spec = pl.BlockSpec((L, NA_WIDTH), lambda i: (blk, 0))
    ospec = pl.BlockSpec((L, NA_WIDTH), lambda i: (0, 0))
    return pl.pallas_call(
        body, name=name, grid=(1,), in_specs=[spec, spec, spec, spec],
        out_specs=[ospec, ospec, ospec], out_shape=[S_((L, NA_WIDTH), f32)] * 3,
        compiler_params=_cp(VMEM_MID, ("arbitrary",)),
    )(do, q, k, v)


def _pool_centered(u, band, inv):
    return _split_sum(_nn, band, u) * inv - u


def _split_sum(mm, band, t):
    hi = t.astype(bf16)
    lo = (t - hi.astype(f32)).astype(bf16)
    s = mm(band, jnp.concatenate([hi, lo], axis=1))
    n = t.shape[1]
    return s[:, :n] + s[:, n:]


def _pool_mix(u_ref, band_ref, inv_ref, w_ref, ps_ref):
    C = POOL_CH
    outs = []
    for g in range(POOL_GROUPS):
        d = _pool_centered(u_ref[:, g * C:(g + 1) * C], band_ref[0, g], inv_ref[0, g])
        outs.append(_nn(d.astype(bf16), w_ref[g].astype(bf16)) * ps_ref[:, g * C:(g + 1) * C])
    return jnp.concatenate(outs, axis=-1).astype(bf16)


def _pool_bwd(cfg, dmix, u, band, inv, w_pool, pool_scale, with_ctx, name):
    TM = cfg.TM
    nt = cfg.ntiles(with_ctx)
    C = POOL_CH

    def body(dy_ref, u_ref, band_ref, inv_ref, w_ref, ps_ref, du_ref, dw_ref, dps_ref):
        @pl.when(pl.program_id(0) == 0)
        def _():
            dw_ref[...] = jnp.zeros_like(dw_ref)
            dps_ref[...] = jnp.zeros_like(dps_ref)

        dus, dpss = [], []
        for g in range(POOL_GROUPS):
            gs = slice(g * C, (g + 1) * C)
            band_g, inv_g = band_ref[0, g], inv_ref[0, g]
            db = _pool_centered(u_ref[:, gs], band_g, inv_g).astype(bf16)
            wb = w_ref[g].astype(bf16)
            dy = dy_ref[:, gs].astype(f32)
            dpss.append(_rsum(dy * _nn(db, wb)))
            dys = (dy * ps_ref[:, gs]).astype(bf16)
            dw_ref[g] += _tn(db, dys)
            dd = _nt(dys, wb)
            dus.append(_split_sum(_tn, band_g, dd * inv_g) - dd)
        du_ref[...] = jnp.concatenate(dus, axis=-1)
        dps_ref[...] += jnp.concatenate(dpss, axis=-1)

    typ4 = lambda i: (jnp.minimum(i // cfg.nxt, 1), 0, 0, 0)
    return pl.pallas_call(
        body, name=name, grid=(nt,),
        in_specs=[pl.BlockSpec((TM, POOL_WIDTH), lambda i: (i, 1)), pl.BlockSpec((TM, POOL_WIDTH), lambda i: (i, 0)),
                  pl.BlockSpec((1, POOL_GROUPS, TM, TM), typ4), pl.BlockSpec((1, POOL_GROUPS, TM, 1), typ4),
                  pl.BlockSpec((POOL_GROUPS, C, C), lambda i: (0, 0, 0)), pl.BlockSpec((1, POOL_WIDTH), lambda i: (0, 0))],
        out_specs=[pl.BlockSpec((TM, POOL_WIDTH), lambda i: (i, 0)), pl.BlockSpec((POOL_GROUPS, C, C), lambda i: (0, 0, 0)),
                   pl.BlockSpec((1, POOL_WIDTH), lambda i: (0, 0))],
        out_shape=[S_((nt * TM, POOL_WIDTH), f32), S_((POOL_GROUPS, C, C), f32), S_((1, POOL_WIDTH), f32)],
        compiler_params=_cp(VMEM_MID, ("arbitrary",)),
    )(dmix, u, band, inv, w_pool, pool_scale)


def _tmpost_fwd(cfg, na_x, na_c, u, band, inv, w_pool, pool_scale, w_out, xs, mods, gvec, name):
    TM, D = cfg.TM, cfg.D
    with_ctx = na_c is not None
    nt = cfg.ntiles(with_ctx)
    R = nt * TM

    def body(*refs):
        if with_ctx:
            nax_ref, nac_ref = refs[:2]
            na = jnp.where(pl.program_id(0) < cfg.nxt, nax_ref[...], nac_ref[...])
        else:
            na = refs[0][...]
        (u_ref, band_ref, inv_ref, wp_ref, ps_ref, w_ref, xs_ref, mods_ref, g_ref,
         out_ref, opre_ref, mix_ref) = refs[2 if with_ctx else 1:]
        pool_v = _pool_mix(u_ref, band_ref, inv_ref, wp_ref, ps_ref)
        mix_ref[:, 0:NA_WIDTH] = na
        mix_ref[:, NA_WIDTH:] = pool_v
        o = _nn(na, w_ref[0:NA_WIDTH, :]) + _nn(pool_v, w_ref[NA_WIDTH:, :])
        opre_ref[...] = o
        ohat, _ = _rms_hat(o)
        out_ref[...] = xs_ref[...] + mods_ref[0][5:6] * (ohat * g_ref[3:4])

    rt = lambda c: pl.BlockSpec((TM, c), lambda i: (i, 0))
    na_specs = [pl.BlockSpec((TM, NA_WIDTH), lambda i: (jnp.minimum(i, cfg.nxt - 1), 0))]
    na_args = [na_x]
    if with_ctx:
        na_specs.append(pl.BlockSpec((TM, NA_WIDTH), lambda i: (0, 0)))
        na_args.append(na_c)
    typ4 = lambda i: (jnp.minimum(i // cfg.nxt, 1), 0, 0, 0)
    pool_specs = [rt(POOL_WIDTH), pl.BlockSpec((1, POOL_GROUPS, TM, TM), typ4), pl.BlockSpec((1, POOL_GROUPS, TM, 1), typ4),
                  pl.BlockSpec((POOL_GROUPS, POOL_CH, POOL_CH), lambda i: (0, 0, 0)), pl.BlockSpec((1, POOL_WIDTH), lambda i: (0, 0))]
    return pl.pallas_call(
        body, name=name, grid=(nt,),
        in_specs=na_specs + pool_specs + [pl.BlockSpec((MIX_WIDTH, D), lambda i: (0, 0)), rt(D),
                                          pl.BlockSpec((1, N_MOD, D), _typ(cfg)), pl.BlockSpec((6, D), lambda i: (0, 0))],
        out_specs=[rt(D), rt(D), rt(MIX_WIDTH)],
        out_shape=[S_((R, D), f32), S_((R, D), f32), S_((R, MIX_WIDTH), bf16)],
        compiler_params=_cp(VMEM_MID, ("arbitrary",)),
    )(*na_args, u, band, inv, w_pool, pool_scale, w_out, xs, mods, gvec)


def _tmpost_bwd(cfg, dout, opre, w_out, mods, gvec, with_ctx, name):
    TM, D = cfg.TM, cfg.D
    nt = cfg.ntiles(with_ctx)
    R = nt * TM
    ntyp = 2 if with_ctx else 1

    def body(do_ref, opre_ref, w_ref, mods_ref, g_ref, dop_ref, dmix_ref, dm_ref, dg_ref):
        i = pl.program_id(0)

        @pl.when(i == 0)
        def _():
            dg_ref[...] = jnp.zeros_like(dg_ref)

        @pl.when((i == 0) | (i == cfg.nxt))
        def _():
            dm_ref[...] = jnp.zeros_like(dm_ref)

        do = do_ref[...]
        g3 = g_ref[3:4]
        ohat, rinv = _rms_hat(opre_ref[...])
        dm_ref[0] += _rsum(do * (ohat * g3))
        dr = mods_ref[0][5:6] * do
        dg_ref[...] += _rsum(dr * ohat)
        dob = _rms_bwd(dr * g3, ohat, rinv).astype(bf16)
        dop_ref[...] = dob
        dmix_ref[...] = _nt(dob, w_ref[...]).astype(bf16)

    rt = lambda c: pl.BlockSpec((TM, c), lambda i: (i, 0))
    return pl.pallas_call(
        body, name=name, grid=(nt,),
        in_specs=[rt(D), rt(D), pl.BlockSpec((MIX_WIDTH, D), lambda i: (0, 0)),
                  pl.BlockSpec((1, N_MOD, D), _typ(cfg)), pl.BlockSpec((6, D), lambda i: (0, 0))],
        out_specs=[rt(D), rt(MIX_WIDTH), pl.BlockSpec((1, 1, D), _typ(cfg)), pl.BlockSpec((1, D), lambda i: (0, 0))],
        out_shape=[S_((R, D), bf16), S_((R, MIX_WIDTH), bf16), S_((ntyp, 1, D), f32), S_((1, D), f32)],
        compiler_params=_cp(VMEM_MID, ("arbitrary",)),
    )(dout, opre, w_out, mods, gvec)


def _modvec_fwd(cvecs, w_mod, b_shard, name):
    nl, D, n = w_mod.shape
    tn = n // 3 if (n % 3 == 0 and (n // 3) % LANES == 0) else n

    def body(c_ref, w_ref, b_ref, o_ref, s_ref):
        cv = c_ref[...]
        sv = cv * _sigmoid(cv)
        s_ref[...] = sv
        o_ref[...] = _nn(sv.astype(bf16), w_ref[...].astype(bf16)) + b_ref[...]

    return pl.pallas_call(
        body, name=name, grid=(nl, n // tn),
        in_specs=[pl.BlockSpec((16, D), lambda l, j: (0, 0)), pl.BlockSpec((None, D, tn), lambda l, j: (l, 0, j)),
                  pl.BlockSpec((None, 1, tn), lambda l, j: (l, 0, j))],
        out_specs=[pl.BlockSpec((None, 16, tn), lambda l, j: (l, 0, j)), pl.BlockSpec((16, D), lambda l, j: (0, 0))],
        out_shape=[S_((nl, 16, n), f32), S_((16, D), f32)],
        compiler_params=_cp(VMEM_MID, ("arbitrary", "arbitrary")),
    )(cvecs, w_mod, b_shard)


def _modvec_bwd(s_t, dm, w_mod, name):
    nl, D, n = w_mod.shape
    tn = n // 3 if (n % 3 == 0 and (n // 3) % LANES == 0) else n

    def body(s_ref, dm_ref, w_ref, gw_ref, gc_ref):
        @pl.when(pl.program_id(1) == 0)
        def _():
            gc_ref[...] = jnp.zeros_like(gc_ref)
        dmv = dm_ref[...]
        gw_ref[...] = jnp.dot(s_ref[...], dmv, preferred_element_type=f32, precision=lax.Precision.HIGHEST)
        gc_ref[...] += _nt(dmv[8:16].astype(bf16), w_ref[...].astype(bf16))

    return pl.pallas_call(
        body, name=name, grid=(nl, n // tn),
        in_specs=[pl.BlockSpec((D, 16), lambda l, j: (0, 0)), pl.BlockSpec((None, 16, tn), lambda l, j: (l, 0, j)),
                  pl.BlockSpec((None, D, tn), lambda l, j: (l, 0, j))],
        out_specs=[pl.BlockSpec((None, D, tn), lambda l, j: (l, 0, j)), pl.BlockSpec((None, 8, D), lambda l, j: (l, 0, 0))],
        out_shape=[S_((nl, D, n), f32), S_((nl, 8, D), f32)],
        compiler_params=_cp(VMEM_MID, ("arbitrary", "arbitrary")),
    )(s_t, dm, w_mod)


def _as2d(a):
    n = a.size
    if a.ndim >= 2 and a.shape[-1] % LANES == 0:
        return a.reshape(-1, a.shape[-1])
    if n % LANES == 0:
        return a.reshape(-1, LANES)
    return a.reshape(-1, a.shape[-1]) if a.ndim >= 2 else a.reshape(1, n)


def _row_tile(r, c, budget_elems):
    if r * c <= budget_elems or r % 8 != 0:
        return r
    t = r
    while t * c > budget_elems and t % 16 == 0:
        t //= 2
    return t


def _div_tile(r, c, budget_elems, mult=16):
    best = None
    for t in range(mult, r + 1, mult):
        if r % t == 0 and t * c <= budget_elems:
            best = t
    return best if best is not None else r


def _chip_index():
    return 2 * lax.axis_index("x") + lax.axis_index("y")


def _cast_into_place(shards, lead, axis, name):
    r, c = shards.shape[-2:]
    tr = _div_tile(r, c, 3 * ELEMWISE_BLOCK)
    nr = r // tr
    out_map = (lambda i: (i, _chip_index())) if axis == 1 else (lambda i: (_chip_index() * nr + i, 0))
    full2 = (r, c * N_CHIPS) if axis == 1 else (r * N_CHIPS, c)

    def body(a_ref, o_ref):
        o_ref[...] = a_ref[...].astype(bf16)

    return pl.pallas_call(
        body, name=name, grid=(nr,),
        in_specs=[pl.BlockSpec((None,) * len(lead) + (tr, c), lambda i: tuple(lead) + (i, 0))],
        out_specs=pl.BlockSpec((tr, c), out_map),
        out_shape=S_(full2, bf16), compiler_params=_cp(VMEM_MID, ("arbitrary",)),
    )(shards)


def _sum_devices8(own, land, axis, into, lead, name):
    _, rh, cs = land.shape
    tr = _div_tile(rh, cs, 2 * ELEMWISE_BLOCK)
    nr = rh // tr
    core = lambda: lax.axis_index("c")
    if axis == 1:
        own_map = lambda i: (core() * nr + i, _chip_index())
    else:
        own_map = lambda i: (_chip_index() * 2 * nr + core() * nr + i, 0)
    nl = len(lead)

    def land_spec(j):
        return pl.BlockSpec((None, tr, cs), lambda i: ((2 * _chip_index() + core() + j) % N_DEV, i, 0))

    def body(own_ref, *rest):
        acc = own_ref[...]
        for p_ref in rest[:N_DEV - 1]:
            acc = acc + p_ref[...].astype(f32)
        rest[-1][...] = acc

    return pl.pallas_call(
        body, name=name, grid=(nr,),
        in_specs=[pl.BlockSpec((tr, cs), own_map)] + [land_spec(j) for j in range(1, N_DEV)] + [ANY],
        out_specs=pl.BlockSpec((None,) * nl + (tr, cs), lambda i: tuple(lead) + (core() * nr + i, 0)),
        out_shape=S_(into.shape, f32), input_output_aliases={N_DEV: 0},
        compiler_params=_cp(VMEM_MID, ("arbitrary",)),
    )(own, *([land] * (N_DEV - 1)), into)


def _adamw(w, g, m, v, name, emit_grad=False, comm=None):
    shape = w.shape
    w2, g2, m2, v2 = _as2d(w), _as2d(g), _as2d(m), _as2d(v)
    r, c = w2.shape
    tr = _row_tile(r, c, ELEMWISE_BLOCK)
    c1 = 1.0 - ADAM_B1 ** ADAM_STEP
    c2 = 1.0 - ADAM_B2 ** ADAM_STEP
    n_out = 4 if emit_grad else 3

    def body(w_ref, g_ref, m_ref, v_ref, d_ref, mo_ref, vo_ref, *go_ref):
        gv = g_ref[...]
        mn = ADAM_B1 * m_ref[...] + (1.0 - ADAM_B1) * gv
        vn = ADAM_B2 * v_ref[...] + (1.0 - ADAM_B2) * (gv * gv)
        mo_ref[...] = mn
        vo_ref[...] = vn
        d_ref[...] = -ADAM_LR * ((mn / c1) / (jnp.sqrt(vn / c2) + ADAM_EPS) + ADAM_WD * w_ref[...])
        if emit_grad:
            go_ref[0][...] = gv

    spec = pl.BlockSpec((tr, c), lambda i: (i, 0))
    outs, res = _call(body, (w2, g2, m2, v2), comm, name=name, grid=(r // tr,), in_specs=[spec] * 4, out_specs=[spec] * n_out,
                      out_shape=[S_((r, c), f32)] * n_out, compiler_params=_cp(VMEM_MID, ("arbitrary",)))
    outs = tuple(o.reshape(shape) for o in outs)
    return outs if comm is None else (outs, res)


def _sum_devices(gathered, name):
    _, r, c = gathered.shape

    def body(a_ref, o_ref):
        acc = a_ref[0]
        for j in range(1, N_DEV):
            acc = acc + a_ref[j]
        o_ref[...] = acc

    tr = _row_tile(r, c, ELEMWISE_BLOCK // 4)
    return pl.pallas_call(
        body, name=name, grid=(r // tr,),
        in_specs=[pl.BlockSpec((N_DEV, tr, c), lambda i: (0, i, 0))], out_specs=pl.BlockSpec((tr, c), lambda i: (i, 0)),
        out_shape=S_((r, c), f32), compiler_params=_cp(VMEM_MID, ("arbitrary",)))(gathered)


def _all_gather_small(block, name):
    m_per, n = block.shape

    def body(x_ref, out_ref, send_sems, recv_sems, local_sem):
        x, y, c = _mesh_pos()
        me, sibling = (x, y, c), (x, y, 1 - c)
        chips = [(1 - x, y), (x, 1 - y), (1 - x, 1 - y)]

        def rows(px, py, pc):
            return out_ref.at[pl.ds((4 * px + 2 * py + pc) * m_per, m_per), :]

        def copy(k, blk, to, src=None):
            return pltpu.make_async_remote_copy(
                src_ref=rows(*blk) if src is None else src, dst_ref=rows(*blk),
                send_sem=send_sems.at[k], recv_sem=recv_sems.at[k], device_id=to, device_id_type=MESH)

        mine = pltpu.make_async_copy(x_ref, rows(*me), local_sem)
        mine.start()
        first = [copy(0, me, sibling, src=x_ref)]
        first += [copy(1 + j, me, (*chip, c), src=x_ref) for j, chip in enumerate(chips)]
        for cp in first:
            cp.start()
        passed = [copy(4 + j, (*chip, c), sibling) for j, chip in enumerate(chips)]
        for j, chip in enumerate(chips):
            copy(1 + j, (*chip, c), me).wait_recv()
            passed[j].start()
        copy(0, sibling, me).wait_recv()
        for j, chip in enumerate(chips):
            copy(4 + j, (*chip, 1 - c), me).wait_recv()
        for cp in first + passed:
            cp.wait_send()
        mine.wait()

    return pl.pallas_call(
        body, name=name, out_shape=S_((N_DEV * m_per, n), block.dtype),
        in_specs=[pl.BlockSpec(memory_space=pltpu.VMEM)], out_specs=pl.BlockSpec(memory_space=pltpu.VMEM),
        scratch_shapes=[pltpu.SemaphoreType.DMA((7,)), pltpu.SemaphoreType.DMA((7,)), pltpu.SemaphoreType.DMA],
        compiler_params=_cp(VMEM_MID),
    )(block)


def _pack_rows(arrays):
    flat = jnp.concatenate([a.reshape(-1) for a in arrays])
    pad = (-flat.size) % (8 * LANES)
    return jnp.pad(flat, (0, pad)).reshape(-1, LANES)


def _unpack_rows(packed, shapes):
    flat = packed.reshape(-1)
    out, off = [], 0
    for s in shapes:
        n = int(np.prod(s))
        out.append(flat[off:off + n].reshape(s))
        off += n
    return out


W_AXIS = {"gu": 1, "dn": 0, "wi": 1, "wo": 0}
SMALL_NAMES = ("dmods", "dg", "drpb", "dwp", "dps")


def _half_merge(bufs, name):
    nt = len(bufs)

    def body(*refs):
        outs = refs[nt:2 * nt]
        send_sems, recv_sems = refs[2 * nt:]
        x, y, c = _mesh_pos()

        def half(ref, h):
            rh = ref.shape[-2] // 2
            return ref.at[(slice(None),) * (len(ref.shape) - 2) + (pl.ds(h * rh, rh), slice(None))]

        cps = []
        for t in range(nt):
            cp = pltpu.make_async_remote_copy(
                src_ref=half(outs[t], c), dst_ref=half(outs[t], c), send_sem=send_sems.at[t], recv_sem=recv_sems.at[t],
                device_id=(x, y, 1 - c), device_id_type=MESH)
            cp.start()
            cps.append(cp)
        for t in range(nt):
            pltpu.make_async_remote_copy(
                src_ref=half(outs[t], 1 - c), dst_ref=half(outs[t], 1 - c), send_sem=send_sems.at[t], recv_sem=recv_sems.at[t],
                device_id=(x, y, 1 - c), device_id_type=MESH).wait_recv()
        for cp in cps:
            cp.wait_send()

    return pl.pallas_call(
        body, name=name, in_specs=[ANY] * nt, out_specs=[ANY] * nt, out_shape=[S_(b.shape, f32) for b in bufs],
        input_output_aliases={t: t for t in range(nt)},
        scratch_shapes=[pltpu.SemaphoreType.DMA((nt,)), pltpu.SemaphoreType.DMA((nt,))],
        compiler_params=_cp(VMEM_MID),
    )(*bufs)


def _local_step(cfg, x_lat, x_ctx, target, mods, norm_g, W, G, na_rpb, w_pool, pool_scale):
    S, L, T, D, F = cfg.S, cfg.L, cfg.T, cfg.D, cfg.F
    depth = norm_g.shape[0]
    cos, sin = _rope_tables(S, L)
    band, inv = _pool_tables(cfg.TM, L)
    flip, sel = _rpb_reduce_tables()

    assert depth == 2, "the carrier schedules below are written for two layers"
    fwd_carry = {"ffn_fwd_0_0": [("wi", 0), ("wo", 0), ("gu", 0, 1), ("dn", 0, 1)],
                 "na_fwd_0": [("gu", 1, 0), ("dn", 1, 0)],
                 "ffn_fwd_0_1": [("wi", 1), ("wo", 1), ("gu", 1, 1), ("dn", 1, 1)]}
    bwd_carry = {"na_bwd_1": [("gu", 1, 1), ("dn", 1, 1)], "ffn_bwd_1_0": [("wi", 1), ("wo", 1)],
                 "ffn_bwd_0_1": [("gu", 1, 0), ("dn", 1, 0)], "na_bwd_0": [("gu", 0, 1), ("dn", 0, 1)],
                 "ffn_bwd_0_0": [("wi", 0), ("wo", 0)], "wgrad_dn_0_0": [("gu", 0, 0)]}
    last_scatter = [("dn", 0, 0)]
    tag = lambda key: "_".join(str(p) for p in key)
    g_f32, g_b16 = {}, {}

    def gather_on(name):
        keys = fwd_carry.get(name)
        return None if keys is None else _gather_comm([W[k_] for k_ in keys], [W_AXIS[k_[0]] for k_ in keys])

    def gathered(name, res):
        if name in fwd_carry:
            W.update(zip(fwd_carry[name], res))

    def scatter_on(name):
        keys = bwd_carry.get(name)
        return None if keys is None else _scatter_comm([g_b16[k_] for k_ in keys], [W_AXIS[k_[0]] for k_ in keys])

    def scattered(keys, lands):
        for key, land in zip(keys, lands):
            G[key[0]] = _sum_devices8(g_f32[key], land, W_AXIS[key[0]], G[key[0]], key[1:], f"sum8_{tag(key)}")

    small_landed = []

    def wgrad(key, a, b, rows, other_comm=None):
        name = f"wgrad_{tag(key)}"
        if other_comm is not None:
            assert name not in bwd_carry
            (g_f32[key], g_b16[key]), res = _wgrad(a, b, rows, name, other_comm)
            small_landed.extend(res)
            return
        (g_f32[key], g_b16[key]), lands = _wgrad(a, b, rows, name, scatter_on(name))
        scattered(bwd_carry.get(name, ()), lands)

    saved = []
    xs, xs_ctx = x_lat, x_ctx
    for l in range(depth):
        last = l == depth - 1
        wc = not last
        gvec = norm_g[l]
        ps = pool_scale[l].reshape(1, POOL_WIDTH)
        bexp = _expand_rpb(na_rpb[l], f"bias_expand_{l}")
        name = f"ffn_fwd_{l}_0"
        (xs1, hb1, z1, y1), res = _ffn_fwd(cfg, xs, mods[l], gvec, W["gu", l, 0], W["dn", l, 0], 0, 0, True, name,
                                           gather_on(name), xs_ctx=xs_ctx)
        gathered(name, res)
        hb2, q, k, v, u = _tmpre_fwd(cfg, xs1, mods[l], gvec, W["wi", l], cos, sin, f"tmpre_fwd_{l}")
        name = f"na_fwd_{l}"
        (na_x,), res = _na_fwd(cfg, q, k, v, bexp, name, gather_on(name))
        gathered(name, res)
        na_c = _ctx_attn_fwd(cfg, q, k, v, f"ctx_attn_fwd_{l}") if wc else None
        xs2, opre, mix = _tmpost_fwd(cfg, na_x, na_c, u, band, inv, w_pool[l], ps, W["wo", l], xs1, mods[l], gvec,
                                     f"tmpost_fwd_{l}")
        name = f"ffn_fwd_{l}_1"
        outs, res = _ffn_fwd(cfg, xs2, mods[l], gvec, W["gu", l, 1], W["dn", l, 1], 6, 4, wc, name, gather_on(name),
                             loss_target=target if last else None)
        xs3, hb3, z3, y3 = outs[:4]
        gathered(name, res)
        saved.append(dict(xs=xs, xs_ctx=xs_ctx, xs1=xs1, xs2=xs2, hb1=hb1, z1=z1, y1=y1, hb2=hb2, q=q, k=k, v=v, u=u, mix=mix,
                          opre=opre, hb3=hb3, z3=z3, y3=y3, bexp=bexp, ps=ps, gvec=gvec))
        xs, xs_ctx = xs3, None

    dxs, loss_blk = xs, outs[4]

    small = [None] * depth
    for l in reversed(range(depth)):
        last = l == depth - 1
        wc = not last
        sv = saved[l]
        gvec = sv["gvec"]
        rows_b = cfg.T if wc else cfg.S
        name = f"ffn_bwd_{l}_1"
        (dxs2, dz, dyb, ab, dm678, dg45), lands = _ffn_bwd(cfg, dxs, sv["xs2"], sv["z3"], sv["y3"], mods[l], gvec,
                                                           W["gu", l, 1], W["dn", l, 1], 6, 4, wc, name, scatter_on(name))
        scattered(bwd_carry.get(name, ()), lands)
        wgrad(("gu", l, 1), sv["hb3"], dz, rows_b)
        wgrad(("dn", l, 1), ab, dyb, rows_b)
        dop, dmix, dm5, dg3 = _tmpost_bwd(cfg, dxs2, sv["opre"], W["wo", l], mods[l], gvec, wc, f"tmpost_bwd_{l}")
        wgrad(("wo", l), sv["mix"], dop, rows_b)
        du, dwp, dps = _pool_bwd(cfg, dmix, sv["u"], band, inv, w_pool[l], sv["ps"], wc, f"pool_bwd_{l}")
        name = f"na_bwd_{l}"
        (dq, dk, dv, dkc, dvc, dbexp), lands = _na_bwd(cfg, dmix, sv["q"], sv["k"], sv["v"], sv["bexp"], name, scatter_on(name))
        scattered(bwd_carry.get(name, ()), lands)
        drpb = _rpb_reduce(dbexp, flip, sel, f"rpb_reduce_{l}")
        if wc:
            dqc, dkc2, dvc2 = _ctx_attn_bwd(cfg, dmix, sv["q"], sv["k"], sv["v"], f"ctx_attn_bwd_{l}")
            ctx_terms = ([dqc], [dkc, dkc2], [dvc, dvc2])
        else:
            ctx_terms = ([], [dkc], [dvc])
        dxs1, dproj, dm34, dg2 = _tmpre_bwd(cfg, (dq, dk, dv, du), ctx_terms, wc, cos, sin, W["wi", l], sv["xs1"], mods[l], gvec,
                                            dxs2, wc, f"tmpre_bwd_{l}")
        wgrad(("wi", l), sv["hb2"], dproj, cfg.T)
        name = f"ffn_bwd_{l}_0"
        (dxs, dz, dyb, ab, dm012, dg01), lands = _ffn_bwd(cfg, dxs1, sv["xs"], sv["z1"], sv["y1"], mods[l], gvec,
                                                          W["gu", l, 0], W["dn", l, 0], 0, 0, True, name, scatter_on(name),
                                                          xs_ctx=sv["xs_ctx"])
        scattered(bwd_carry.get(name, ()), lands)
        if not wc:
            zero = lambda a: jnp.concatenate([a, jnp.zeros_like(a)], axis=0)
            dm5, dm678 = zero(dm5), zero(dm678)
        dmods = jnp.concatenate([dm012, dm34, dm5, dm678], axis=1)
        dgs = jnp.concatenate([dg01, dg2, dg3, dg45], axis=0)
        small[l] = dict(dmods=dmods, dg=dgs, drpb=drpb, dwp=dwp, dps=dps)
        small_gather = None
        if l == 0:
            parts = [jnp.stack([small[j][n_] for j in range(depth)]) for n_ in SMALL_NAMES]
            packed = _pack_rows(parts)
            small_gather = _allgather_comm(packed)
        wgrad(("gu", l, 0), sv["hb1"], dz, cfg.T, small_gather)
        wgrad(("dn", l, 0), ab, dyb, cfg.T)
    last_comm = _scatter_comm([g_b16[k_] for k_ in last_scatter], [W_AXIS[k_[0]] for k_ in last_scatter])

    def finish_weight_grads(lands):
        scattered(last_scatter, lands)
        kinds = ("gu", "dn", "wi", "wo")
        return dict(zip(kinds, _half_merge([G[k_] for k_ in kinds], "merge_halves")))

    return loss_blk, dxs, (last_comm, finish_weight_grads), (packed, [p.shape for p in parts], small_landed[0])


def kernel(x, c, ctx, c_ctx, w_mod, b_mod, norm_g, w_ffn_gate_up, w_ffn_down, w_in, w_out, na_rpb, w_pool, pool_scale, loss_target, m_c_ctx, m_w_mod, m_b_mod, m_norm_g, m_w_ffn_gate_up, m_w_ffn_down, m_w_in, m_w_out, m_na_rpb, m_w_pool, m_pool_scale, v_c_ctx, v_w_mod, v_b_mod, v_norm_g, v_w_ffn_gate_up, v_w_ffn_down, v_w_in, v_w_out, v_na_rpb, v_w_pool, v_pool_scale):
    S, D = x.shape[1], x.shape[2]
    L = ctx.shape[1]
    depth = w_mod.shape[0]
    F = w_ffn_down.shape[2] * N_CHIPS
    nmod = w_mod.shape[2]
    gsh = norm_g.shape[2]
    cfg = _Cfg(S, L, D, F)
    mx, my, mc = _mesh_pos()
    chip = 2 * mx + my
    dev = 4 * mx + 2 * my + mc

    W = {}
    for l in range(depth):
        for i in range(2):
            W["gu", l, i] = _cast_into_place(w_ffn_gate_up, (l, i), W_AXIS["gu"], f"cast_gu_{l}_{i}")
            W["dn", l, i] = _cast_into_place(w_ffn_down, (l, i), W_AXIS["dn"], f"cast_dn_{l}_{i}")
        W["wi", l] = _cast_into_place(w_in, (l,), W_AXIS["wi"], f"cast_wi_{l}")
        W["wo", l] = _cast_into_place(w_out, (l,), W_AXIS["wo"], f"cast_wo_{l}")
    first = [("gu", 0, 0), ("dn", 0, 0)]
    W.update(zip(first, _comm_only(_gather_comm([W[k_] for k_ in first], [W_AXIS[k_[0]] for k_ in first]), "gather_first")))
    G = {"gu": lax.empty(w_ffn_gate_up.shape, f32), "dn": lax.empty(w_ffn_down.shape, f32),
         "wi": lax.empty(w_in.shape, f32), "wo": lax.empty(w_out.shape, f32)}

    cg_packed = _pack_rows([c, norm_g])
    cg_all = _all_gather_small(cg_packed, "gather_c_norm_g").reshape(N_DEV, -1)
    c_all = cg_all[:, :D]
    ng = cg_all[:, D:D + norm_g.size].reshape(N_DEV, depth, 6, gsh)
    norm_g_all = jnp.concatenate([ng[2 * j] for j in range(N_CHIPS)], axis=-1)
    cvecs = jnp.concatenate([c_all, c_ctx[None], jnp.zeros((7, D), f32)], axis=0)
    b_shard = lax.dynamic_slice_in_dim(b_mod, chip * nmod, nmod, axis=1).reshape(depth, 1, nmod)
    m_part, silu_c = _modvec_fwd(cvecs, w_mod, b_shard, "modvec_fwd")
    m_all = _all_gather_small(m_part.reshape(depth * 16, nmod), "gather_mod").reshape(N_DEV, depth, 16, nmod)
    m_full = jnp.concatenate([m_all[2 * j] for j in range(N_CHIPS)], axis=-1)
    m_mine = lax.dynamic_index_in_dim(m_full, dev, axis=1, keepdims=False)
    mods = jnp.stack([m_mine, m_full[:, 8]], axis=1).reshape(depth, 2, N_MOD, D)

    loss_blk, dx_lat, (last_comm, finish_weight_grads), small = _local_step(
        cfg, x[0], ctx[0], loss_target[0], mods, norm_g_all, W, G, na_rpb, w_pool, pool_scale)
    loss = lax.psum(loss_blk[0, 0], ("x", "y", "c"))
    grad_x = dx_lat[None]

    packed, shapes, landed = small
    gathered = lax.dynamic_update_index_in_dim(landed, packed, dev, 0)
    total = _unpack_rows(_sum_devices(gathered, "sum_small"), shapes)
    dmods_sum, dg_sum, drpb_sum, dwp_sum, dps_sum = total
    dmods_each = jnp.stack([_unpack_rows(gathered[j], shapes[:1])[0] for j in range(N_DEV)])
    dm_rows = jnp.concatenate([jnp.transpose(dmods_each[:, :, 0], (1, 0, 2, 3)).reshape(depth, N_DEV, N_MOD * D),
                               dmods_sum[:, 1].reshape(depth, 1, N_MOD * D),
                               jnp.zeros((depth, 7, N_MOD * D), f32)], axis=1)
    dm_shard = lax.dynamic_slice_in_dim(dm_rows, chip * nmod, nmod, axis=2)
    grad_w_mod, gc_part = _modvec_bwd(silu_c.T, dm_shard, w_mod, "modvec_bwd")
    gc_all = _all_gather_small(gc_part.reshape(depth * 8, D), "gather_gc").reshape(N_DEV, depth, 8, D)
    grad_b_mod, grad_c_ctx = _small_finish(dm_rows, gc_all, c_ctx)
    grad_norm_g = lax.dynamic_slice_in_dim(dg_sum, chip * gsh, gsh, axis=2)
    grad_na_rpb = drpb_sum[:, :, :2 * NA_KH - 1, :2 * NA_KW - 1]
    grad_w_pool = dwp_sum
    grad_pool_scale = dps_sum.reshape(depth, POOL_WIDTH)

    upd_w_mod, lands = _adamw(w_mod, grad_w_mod, m_w_mod, v_w_mod, "adamw_w_mod", comm=last_comm)
    wgrads = finish_weight_grads(lands)
    g_gu, g_dn, g_wi, g_wo = wgrads["gu"], wgrads["dn"], wgrads["wi"], wgrads["wo"]
    grads = [grad_c_ctx, grad_w_mod, grad_b_mod, grad_norm_g, g_gu, g_dn, g_wi, g_wo, grad_na_rpb, grad_w_pool, grad_pool_scale]
    ws = [c_ctx, w_mod, b_mod, norm_g, w_ffn_gate_up, w_ffn_down, w_in, w_out, na_rpb, w_pool, pool_scale]
    ms = [m_c_ctx, m_w_mod, m_b_mod, m_norm_g, m_w_ffn_gate_up, m_w_ffn_down, m_w_in, m_w_out, m_na_rpb, m_w_pool, m_pool_scale]
    vs = [v_c_ctx, v_w_mod, v_b_mod, v_norm_g, v_w_ffn_gate_up, v_w_ffn_down, v_w_in, v_w_out, v_na_rpb, v_w_pool, v_pool_scale]
    tags = ["c_ctx", "w_mod", "b_mod", "norm_g", "gate_up", "down", "w_in", "w_out", "na_rpb", "w_pool", "pool_scale"]
    merged = ("gate_up", "down", "w_in", "w_out")
    upd = [upd_w_mod if t == "w_mod" else _adamw(w_, g_, m_, v_, f"adamw_{t}", emit_grad=t in merged)
           for w_, g_, m_, v_, t in zip(ws, grads, ms, vs, tags)]
    grads = [u_[3] if t in merged else g_ for g_, u_, t in zip(grads, upd, tags)]
    return (loss, grad_x, *grads, *[u_[0] for u_ in upd], *[u_[1] for u_ in upd], *[u_[2] for u_ in upd])


def _small_finish(dm_rows, gc_all, c_ctx):
    depth, _, n = dm_rows.shape
    D = c_ctx.shape[0]

    def body(dm_ref, gc_ref, c_ref, gb_ref, gcx_ref):
        acc = dm_ref[:, 0]
        for j in range(1, N_DEV + 1):
            acc = acc + dm_ref[:, j]
        gb_ref[...] = acc
        t = jnp.zeros((1, D), f32)
        for l in range(depth):
            for j in range(N_CHIPS):
                t = t + gc_ref[2 * j, l, 0:1, :]
        cv = c_ref[...]
        sg = _sigmoid(cv)
        gcx_ref[...] = t * (sg * (1.0 + cv * (1.0 - sg)))

    gb, gcx = pl.pallas_call(
        body, name="small_finish",
        out_shape=[S_((depth, n), f32), S_((1, D), f32)],
        compiler_params=_cp(VMEM_MID),
    )(dm_rows, gc_all, c_ctx.reshape(1, D))
    return gb, gcx.reshape(D)
```

```python
import functools

import numpy as np
import jax
import jax.numpy as jnp
from jax import lax
from jax.experimental import pallas as pl
from jax.experimental.pallas import tpu as pltpu

f32, bf16 = jnp.float32, jnp.bfloat16

GRID_W = 64
N_MOD = 9
NA_HEADS = 8
HEAD_DIM = 64
NA_WIDTH = NA_HEADS * HEAD_DIM
NA_KH = 8
NA_KW = 16
POOL_GROUPS = 4
POOL_CH = 128
POOL_WIDTH = POOL_GROUPS * POOL_CH
POOL_WINDOWS = (2, 4, 8, 16)
IN_WIDTH = 3 * NA_WIDTH + POOL_WIDTH
MIX_WIDTH = NA_WIDTH + POOL_WIDTH
ROPE_THETA = 10000.0
ROPE_PAIRS = HEAD_DIM // 4
RMS_EPS = 1e-6
NEG_INF = -1e30
ADAM_LR, ADAM_B1, ADAM_B2, ADAM_EPS, ADAM_WD, ADAM_STEP = 0.001, 0.9, 0.999, 1e-08, 0.01, 10

N_DEV = 8
N_CHIPS = 4
LANES = 128
MIB = 1024 * 1024
VMEM_BIG = 52 * MIB
VMEM_MID = 40 * MIB
WGRAD_TN = 1408
WGRAD_TK = 2816
WGRAD_SLACK = 6 * MIB
ELEMWISE_BLOCK = 256 * 1024
MESH = pl.DeviceIdType.MESH
ANY = pl.BlockSpec(memory_space=pl.ANY)
S_ = jax.ShapeDtypeStruct


def _cp(vmem=VMEM_MID, sem=None):
    return pltpu.CompilerParams(vmem_limit_bytes=vmem, dimension_semantics=sem)


def _sigmoid(x):
    return 0.5 * jnp.tanh(0.5 * x) + 0.5


def _rms_hat(x):
    rinv = lax.rsqrt(jnp.mean(x * x, axis=-1, keepdims=True) + RMS_EPS)
    return x * rinv, rinv


def _rms_bwd(dxhat, xhat, rinv):
    return rinv * (dxhat - xhat * jnp.mean(dxhat * xhat, axis=-1, keepdims=True))


def _rsum(a):
    return jnp.sum(a, axis=0, keepdims=True)


def _nt(a, b):
    return lax.dot_general(a, b, (((1,), (1,)), ((), ())), preferred_element_type=f32)


def _tn(a, b):
    return lax.dot_general(a, b, (((0,), (0,)), ((), ())), preferred_element_type=f32)


def _nn(a, b):
    return jnp.dot(a, b, preferred_element_type=f32)


def _swap16(x):
    lane = lax.broadcasted_iota(jnp.int32, x.shape, 1)
    n = x.shape[1]
    return jnp.where((lane % 32) < 16, pltpu.roll(x, n - 16, 1), pltpu.roll(x, 16, 1))


def _rope_tables(s_len, l_len):
    t = np.arange(s_len)
    inv = ROPE_THETA ** (-np.arange(ROPE_PAIRS, dtype=np.float32) / ROPE_PAIRS)
    ang_r = (t // GRID_W).astype(np.float32)[:, None] * inv
    ang_c = (t % GRID_W).astype(np.float32)[:, None] * inv
    cos = np.concatenate([np.cos(ang_r), np.cos(ang_r), np.cos(ang_c), np.cos(ang_c)], axis=-1)
    sin = np.concatenate([-np.sin(ang_r), np.sin(ang_r), -np.sin(ang_c), np.sin(ang_c)], axis=-1)
    cos = np.concatenate([cos, np.ones((l_len, HEAD_DIM), np.float32)], axis=0)
    sin = np.concatenate([sin, np.zeros((l_len, HEAD_DIM), np.float32)], axis=0)
    return (jnp.asarray(np.tile(cos, (1, 2)), f32), jnp.asarray(np.tile(sin, (1, 2)), f32))


def _pool_tables(tm, l_len):
    band = np.zeros((2, POOL_GROUPS, tm, tm), np.float32)
    inv = np.zeros((2, POOL_GROUPS, tm, 1), np.float32)
    for typ, length in ((0, GRID_W), (1, l_len)):
        for g, w in enumerate(POOL_WINDOWS):
            for t in range(tm):
                base, p = (t // length) * length, t % length
                lo = min(max(p - w // 2, 0), length)
                hi = min(max(p - w // 2 + w, 0), length)
                band[typ, g, t, base + lo:base + hi] = 1.0
                inv[typ, g, t, 0] = 1.0 / (hi - lo)
    return jnp.asarray(band, bf16), jnp.asarray(inv, f32)


NA_QR = 4
NA_WR = NA_KH + NA_QR - 1
NA_TYPES = 3
NA_SEL_ROWS = 136
NA_WPAD = 768


def _rpb_index_tables():
    j = np.arange(GRID_W)
    col_start = np.clip(j - NA_KW // 2, 0, GRID_W - NA_KW)
    valid = (j[None, :] >= col_start[:, None]) & (j[None, :] < col_start[:, None] + NA_KW)
    dc = np.clip(j[None, :] - j[:, None] + NA_KW - 1, 0, 2 * NA_KW - 2)
    i = np.arange(NA_QR)[:, None]
    kk = np.arange(NA_WR)[None, :]
    off = np.stack([np.zeros_like(i), i, np.full_like(i, NA_QR - 1)])
    d = np.stack([kk - i + NA_KH - 1, kk - i + NA_KH - 1 - NA_QR, kk - i])
    row_ok = (kk[None] >= off) & (kk[None] < off + NA_KH)
    assert (d[row_ok] >= 0).all() and (d[row_ok] <= 2 * NA_KH - 2).all()
    return valid, dc, d, row_ok


def _expand_rpb(rpb, name):
    _, _, d, row_ok = _rpb_index_tables()
    heads, nd, ne = rpb.shape
    w = GRID_W
    v = jnp.pad(rpb, ((0, 0), (0, 0), (w - NA_KW, 2 * w - (w - NA_KW) - ne)))
    x = jnp.broadcast_to(v[:, :, None, :], (heads, nd, w, 2 * w)).reshape(heads, nd, 2 * w * w)
    t = x[:, :, :w * (2 * w - 1)].reshape(heads, nd, w, 2 * w - 1)[..., w - 1:]

    def body(t_ref, o_ref):
        q = lax.broadcasted_iota(jnp.int32, (w, w), 0)
        c = lax.broadcasted_iota(jnp.int32, (w, w), 1)
        c0 = jnp.clip(q - NA_KW // 2, 0, w - NA_KW)
        in_cols = (c >= c0) & (c < c0 + NA_KW)
        outside = jnp.full((w, w), NEG_INF, f32)
        blocks = [jnp.where(in_cols, t_ref[dd], NEG_INF) for dd in range(nd)]
        for typ in range(NA_TYPES):
            for i in range(NA_QR):
                row = [blocks[d[typ, i, kk]] if row_ok[typ, i, kk] else outside for kk in range(NA_WR)]
                o_ref[typ, i * w:(i + 1) * w, :] = jnp.concatenate(row, axis=1)

    return pl.pallas_call(
        body, name=name, grid=(heads,),
        in_specs=[pl.BlockSpec((None, nd, w, w), lambda h: (h, 0, 0, 0))],
        out_specs=pl.BlockSpec((NA_TYPES, None, NA_QR * w, NA_WR * w), lambda h: (0, h, 0, 0)),
        out_shape=S_((NA_TYPES, heads, NA_QR * w, NA_WR * w), f32),
        compiler_params=_cp(VMEM_MID, ("arbitrary",)),
    )(t)


def _rpb_reduce_tables():
    _, _, d, row_ok = _rpb_index_tables()
    flip = np.eye(GRID_W, dtype=np.float32)[::-1].copy()
    sel = np.zeros((16, NA_SEL_ROWS), np.float32)
    flat_d, flat_ok = d.reshape(-1), row_ok.reshape(-1)
    for n in range(flat_d.size):
        if flat_ok[n]:
            sel[flat_d[n], n] = 1.0
    return jnp.asarray(flip), jnp.asarray(sel)


class _Cfg:
    def __init__(self, s_len, l_len, d, f):
        self.S, self.L, self.D, self.F = s_len, l_len, d, f
        self.T = s_len + l_len
        self.TM = 256 if l_len % 256 == 0 else 128
        assert l_len == self.TM, "context length must equal the row tile"
        assert s_len % self.TM == 0 and s_len % GRID_W == 0
        self.nxt = s_len // self.TM
        self.ntt = self.T // self.TM
        self.rows = s_len // GRID_W
        assert self.rows >= 2 * NA_KH
        assert f % (2 * LANES) == 0
        self.FC = f

    def ntiles(self, with_ctx):
        return self.ntt if with_ctx else self.nxt


def _typ(cfg):
    return lambda i: (jnp.minimum(i // cfg.nxt, 1), 0, 0)


def _mesh_pos():
    return lax.axis_index("x"), lax.axis_index("y"), lax.axis_index("c")


class _Comm:
    def __init__(self, ins, outs, alias, nsem, start, finish):
        self.ins, self.outs, self.alias, self.nsem, self.start, self.finish = ins, outs, alias, nsem, start, finish


def _call(body, args, comm=None, *, grid, in_specs, out_specs, out_shape, scratch_shapes=(), **kw):
    if comm is None:
        return pl.pallas_call(body, grid=grid, in_specs=list(in_specs), out_specs=list(out_specs), out_shape=list(out_shape),
                              scratch_shapes=list(scratch_shapes), **kw)(*args), ()
    n_in, n_out, n_sc = len(in_specs), len(out_specs), len(scratch_shapes)
    ci, co = len(comm.ins), len(comm.outs)

    def carrier(*refs):
        bounds = np.cumsum([0, n_in, ci, n_out, co, n_sc])
        ins, cins, outs, couts, scr = (refs[a:b] for a, b in zip(bounds[:-1], bounds[1:]))
        send, recv = refs[bounds[-1]], refs[bounds[-1] + 1]
        first = functools.reduce(jnp.logical_and, [pl.program_id(a) == 0 for a in range(len(grid))])
        last = functools.reduce(jnp.logical_and, [pl.program_id(a) == g - 1 for a, g in enumerate(grid)])

        @pl.when(first)
        def _():
            comm.start(cins, couts, send, recv)

        body(*ins, *outs, *scr)

        @pl.when(last)
        def _():
            comm.finish(cins, couts, send, recv)

    res = pl.pallas_call(
        carrier, grid=grid, in_specs=list(in_specs) + [ANY] * ci, out_specs=list(out_specs) + [ANY] * co,
        out_shape=list(out_shape) + list(comm.outs),
        input_output_aliases={n_in + a: n_out + b for a, b in comm.alias.items()},
        scratch_shapes=list(scratch_shapes) + [pltpu.SemaphoreType.DMA((comm.nsem,)), pltpu.SemaphoreType.DMA((comm.nsem,))],
        **kw)(*args, *comm.ins)
    return res[:n_out], res[n_out:]


def _comm_only(comm, name):
    ci, co = len(comm.ins), len(comm.outs)

    def body(*refs):
        cins, couts = refs[:ci], refs[ci:ci + co]
        send, recv = refs[ci + co], refs[ci + co + 1]
        comm.start(cins, couts, send, recv)
        comm.finish(cins, couts, send, recv)

    return pl.pallas_call(
        body, name=name, in_specs=[ANY] * ci, out_specs=[ANY] * co, out_shape=list(comm.outs),
        input_output_aliases=dict(comm.alias),
        scratch_shapes=[pltpu.SemaphoreType.DMA((comm.nsem,)), pltpu.SemaphoreType.DMA((comm.nsem,))],
        compiler_params=_cp(VMEM_MID),
    )(*comm.ins)


def _half_view(ref, axis, kk, h):
    r, c = ref.shape
    if axis == 1:
        n = c // N_CHIPS
        return ref.at[pl.ds(h * (r // 2), r // 2), pl.ds(pl.multiple_of(kk * n, LANES), n)]
    n = r // N_CHIPS
    return ref.at[pl.ds(pl.multiple_of(kk * n + h * (n // 2), 8), n // 2), :]


def _other_chips(x, y):
    return [(1 - x, y), (x, 1 - y), (1 - x, 1 - y)]


def _gather_comm(arrs, axes):
    n = len(arrs)

    def copy(ref, view, sems, k, to):
        send, recv = sems
        return pltpu.make_async_remote_copy(src_ref=view, dst_ref=view, send_sem=send.at[k], recv_sem=recv.at[k],
                                            device_id=to, device_id_type=MESH)

    def start(cins, bufs, send, recv):
        x, y, c = _mesh_pos()
        for t in range(n):
            own = _half_view(bufs[t], axes[t], 2 * x + y, c)
            for j, chip in enumerate(_other_chips(x, y)):
                copy(bufs[t], own, (send, recv), 6 * t + j, (*chip, c)).start()

    def finish(cins, bufs, send, recv):
        x, y, c = _mesh_pos()
        sibling = (x, y, 1 - c)
        chips = _other_chips(x, y)
        for t in range(n):
            for j, chip in enumerate(chips):
                landed = _half_view(bufs[t], axes[t], 2 * chip[0] + chip[1], c)
                copy(bufs[t], landed, (send, recv), 6 * t + j, (*chip, c)).wait_recv()
                copy(bufs[t], landed, (send, recv), 6 * t + 3 + j, sibling).start()
        for t in range(n):
            own = _half_view(bufs[t], axes[t], 2 * x + y, c)
            for j, chip in enumerate(chips):
                kj = 2 * chip[0] + chip[1]
                copy(bufs[t], _half_view(bufs[t], axes[t], kj, 1 - c), (send, recv), 6 * t + 3 + j, sibling).wait_recv()
                copy(bufs[t], own, (send, recv), 6 * t + j, (*chip, c)).wait_send()
                copy(bufs[t], _half_view(bufs[t], axes[t], kj, c), (send, recv), 6 * t + 3 + j, sibling).wait_send()

    return _Comm(list(arrs), [S_(a.shape, a.dtype) for a in arrs], {t: t for t in range(n)}, 6 * n, start, finish)


def _scatter_comm(parts, axes):
    n = len(parts)
    peers = [(fx, fy, fc) for fx in (0, 1) for fy in (0, 1) for fc in (0, 1)][1:]

    def half_shape(a, axis):
        r, c = a.shape
        return (r // 2, c // N_CHIPS) if axis == 1 else (r // N_CHIPS // 2, c)

    def start(srcs, lands, send, recv):
        x, y, c = _mesh_pos()
        me = 4 * x + 2 * y + c
        for t in range(n):
            for r_, (fx, fy, fc) in enumerate(peers):
                dx, dy, dc = (1 - x if fx else x), (1 - y if fy else y), (1 - c if fc else c)
                pltpu.make_async_remote_copy(
                    src_ref=_half_view(srcs[t], axes[t], 2 * dx + dy, dc), dst_ref=lands[t].at[me],
                    send_sem=send.at[7 * t + r_], recv_sem=recv.at[7 * t + r_],
                    device_id=(dx, dy, dc), device_id_type=MESH).start()

    def finish(srcs, lands, send, recv):
        x, y, c = _mesh_pos()
        for t in range(n):
            mine = _half_view(srcs[t], axes[t], 2 * x + y, c)
            for r_, (fx, fy, fc) in enumerate(peers):
                sx, sy, sc = (1 - x if fx else x), (1 - y if fy else y), (1 - c if fc else c)
                cp = pltpu.make_async_remote_copy(
                    src_ref=mine, dst_ref=lands[t].at[4 * sx + 2 * sy + sc],
                    send_sem=send.at[7 * t + r_], recv_sem=recv.at[7 * t + r_],
                    device_id=(sx, sy, sc), device_id_type=MESH)
                cp.wait_recv()
                cp.wait_send()

    return _Comm(list(parts), [S_((N_DEV,) + half_shape(a, ax), a.dtype) for a, ax in zip(parts, axes)], {}, 7 * n, start, finish)


def _allgather_comm(block):
    peers = [(fx, fy, fc) for fx in (0, 1) for fy in (0, 1) for fc in (0, 1)][1:]

    def ends(x, y, c):
        for r_, (fx, fy, fc) in enumerate(peers):
            yield r_, ((1 - x if fx else x), (1 - y if fy else y), (1 - c if fc else c))

    def start(srcs, lands, send, recv):
        x, y, c = _mesh_pos()
        for r_, peer in ends(x, y, c):
            pltpu.make_async_remote_copy(src_ref=srcs[0], dst_ref=lands[0].at[4 * x + 2 * y + c], send_sem=send.at[r_],
                                         recv_sem=recv.at[r_], device_id=peer, device_id_type=MESH).start()

    def finish(srcs, lands, send, recv):
        x, y, c = _mesh_pos()
        for r_, (px, py, pc) in ends(x, y, c):
            cp = pltpu.make_async_remote_copy(src_ref=srcs[0], dst_ref=lands[0].at[4 * px + 2 * py + pc], send_sem=send.at[r_],
                                              recv_sem=recv.at[r_], device_id=(px, py, pc), device_id_type=MESH)
            cp.wait_recv()
            cp.wait_send()

    return _Comm([block], [S_((N_DEV,) + block.shape, block.dtype)], {}, len(peers), start, finish)


def _ffn_fwd(cfg, xs, mods, gvec, wgu, wd, mi, gi, with_ctx, name, comm=None, xs_ctx=None, loss_target=None):
    TM, D, F, FC = cfg.TM, cfg.D, cfg.F, cfg.FC
    nt = cfg.ntiles(with_ctx)
    R = nt * TM
    split, head = xs_ctx is not None, loss_target is not None

    def body(*refs):
        it = iter(refs)
        xs_ref = next(it)
        xc_ref = next(it) if split else None
        mods_ref, g_ref, wgu_hbm, wd_hbm = next(it), next(it), next(it), next(it)
        t_ref = next(it) if head else None
        out_ref, hb_ref, z_ref, y_ref = next(it), next(it), next(it), next(it)
        loss_ref = next(it) if head else None
        wgu_v, wd_v, sem = next(it), next(it), next(it)
        i = pl.program_id(0)

        @pl.when(i == 0)
        def _():
            c0 = pltpu.make_async_copy(wgu_hbm, wgu_v, sem.at[0])
            c1 = pltpu.make_async_copy(wd_hbm, wd_v, sem.at[1])
            c0.start(); c1.start(); c0.wait(); c1.wait()
            if head:
                loss_ref[...] = jnp.zeros_like(loss_ref)
        x = xs_ref[...]
        if split:
            x = jnp.where(i < cfg.nxt, x, xc_ref[...])
        m = mods_ref[0]
        sh, sc, gt = m[mi:mi + 1], m[mi + 1:mi + 2], m[mi + 2:mi + 3]
        xhat, _ = _rms_hat(x)
        h = (xhat * g_ref[gi:gi + 1]) * (1.0 + sc) + sh
        hb = h.astype(bf16)
        hb_ref[...] = hb
        y = jnp.zeros((TM, D), f32)
        for ch in range(F // FC):
            zg = _nn(hb, wgu_v[:, ch * FC:(ch + 1) * FC])
            zu = _nn(hb, wgu_v[:, F + ch * FC:F + (ch + 1) * FC])
            z_ref[:, ch * FC:(ch + 1) * FC] = zg.astype(bf16)
            z_ref[:, F + ch * FC:F + (ch + 1) * FC] = zu.astype(bf16)
            a = (zg * _sigmoid(zg)) * zu
            y = y + _nn(a.astype(bf16), wd_v[ch * FC:(ch + 1) * FC, :])
        y_ref[...] = y
        yhat, _ = _rms_hat(y)
        out = x + 0.5 * gt * (yhat * g_ref[gi + 1:gi + 2])
        if head:
            e = out - t_ref[...]
            out_ref[...] = e * (1.0 / D)
            loss_ref[...] += jnp.sum(jnp.mean(e * e, axis=-1, keepdims=True), axis=0, keepdims=True) * 0.5
        else:
            out_ref[...] = out

    rt = lambda c: pl.BlockSpec((TM, c), lambda i: (i, 0))
    lat = pl.BlockSpec((TM, D), lambda i: (jnp.minimum(i, cfg.nxt - 1), 0))
    x_specs, x_args = ([lat, pl.BlockSpec((TM, D), lambda i: (0, 0))], [xs, xs_ctx]) if split else ([rt(D)], [xs])
    t_specs, t_args = ([rt(D)], [loss_target]) if head else ([], [])
    l_specs, l_shape = ([pl.BlockSpec((8, LANES), lambda i: (0, 0))], [S_((8, LANES), f32)]) if head else ([], [])
    return _call(
        body, (*x_args, mods, gvec, wgu, wd, *t_args), comm, name=name, grid=(nt,),
        in_specs=x_specs + [pl.BlockSpec((1, N_MOD, D), _typ(cfg)), pl.BlockSpec((6, D), lambda i: (0, 0)), ANY, ANY] + t_specs,
        out_specs=[rt(D), rt(D), rt(2 * F), rt(D)] + l_specs,
        out_shape=[S_((R, D), f32), S_((R, D), bf16), S_((R, 2 * F), bf16), S_((R, D), f32)] + l_shape,
        scratch_shapes=[pltpu.VMEM((D, 2 * F), bf16), pltpu.VMEM((F, D), bf16), pltpu.SemaphoreType.DMA((2,))],
        compiler_params=_cp(VMEM_BIG, ("arbitrary",)),
    )


def _ffn_bwd(cfg, dout, xs, z, y, mods, gvec, wgu, wd, mi, gi, with_ctx, name, comm=None, xs_ctx=None):
    TM, D, F, FC = cfg.TM, cfg.D, cfg.F, cfg.FC
    nt = cfg.ntiles(with_ctx)
    R = nt * TM
    ntyp = 2 if with_ctx else 1
    split = xs_ctx is not None

    def body(*refs):
        it = iter(refs)
        do_ref, xs_ref = next(it), next(it)
        xc_ref = next(it) if split else None
        z_ref, y_ref, mods_ref, g_ref, wgu_hbm, wd_hbm = (next(it) for _ in range(6))
        dx_ref, dz_ref, dy_ref, a_ref, dm_ref, dg_ref, wgu_v, wd_v, sem = (next(it) for _ in range(9))
        i = pl.program_id(0)

        @pl.when(i == 0)
        def _():
            c0 = pltpu.make_async_copy(wgu_hbm, wgu_v, sem.at[0])
            c1 = pltpu.make_async_copy(wd_hbm, wd_v, sem.at[1])
            c0.start(); c1.start(); c0.wait(); c1.wait()
            dg_ref[...] = jnp.zeros_like(dg_ref)

        @pl.when((i == 0) | (i == cfg.nxt))
        def _():
            dm_ref[...] = jnp.zeros_like(dm_ref)

        do = do_ref[...]
        x = xs_ref[...]
        if split:
            x = jnp.where(i < cfg.nxt, x, xc_ref[...])
        m = mods_ref[0]
        sc, gt = m[mi + 1:mi + 2], m[mi + 2:mi + 3]
        g_pre, g_post = g_ref[gi:gi + 1], g_ref[gi + 1:gi + 2]
        xhat, rinv0 = _rms_hat(x)
        n0 = xhat * g_pre
        yhat, rinv1 = _rms_hat(y_ref[...])
        d_gt = _rsum(0.5 * do * (yhat * g_post))
        dr = (0.5 * gt) * do
        dg_post = _rsum(dr * yhat)
        dy = _rms_bwd(dr * g_post, yhat, rinv1)
        dyb = dy.astype(bf16)
        dy_ref[...] = dyb
        dh = jnp.zeros((TM, D), f32)
        for ch in range(F // FC):
            zg = z_ref[:, ch * FC:(ch + 1) * FC].astype(f32)
            zu = z_ref[:, F + ch * FC:F + (ch + 1) * FC].astype(f32)
            sg = _sigmoid(zg)
            silu = zg * sg
            a_ref[:, ch * FC:(ch + 1) * FC] = (silu * zu).astype(bf16)
            da = _nt(dyb, wd_v[ch * FC:(ch + 1) * FC, :])
            dzu = (da * silu).astype(bf16)
            dzg = (da * zu * (sg * (1.0 + zg * (1.0 - sg)))).astype(bf16)
            dz_ref[:, ch * FC:(ch + 1) * FC] = dzg
            dz_ref[:, F + ch * FC:F + (ch + 1) * FC] = dzu
            dh = dh + _nt(dzg, wgu_v[:, ch * FC:(ch + 1) * FC]) + _nt(dzu, wgu_v[:, F + ch * FC:F + (ch + 1) * FC])
        d_sh = _rsum(dh)
        d_sc = _rsum(dh * n0)
        dn = dh * (1.0 + sc)
        dg_pre = _rsum(dn * xhat)
        dx = do + _rms_bwd(dn * g_pre, xhat, rinv0)
        if split:
            @pl.when(i < cfg.nxt)
            def _():
                dx_ref[...] = dx
        else:
            dx_ref[...] = dx
        dm_ref[0] += jnp.concatenate([d_sh, d_sc, d_gt], axis=0)
        dg_ref[...] += jnp.concatenate([dg_pre, dg_post], axis=0)

    rt = lambda c: pl.BlockSpec((TM, c), lambda i: (i, 0))
    lat = pl.BlockSpec((TM, D), lambda i: (jnp.minimum(i, cfg.nxt - 1), 0))
    x_specs, x_args = ([lat, pl.BlockSpec((TM, D), lambda i: (0, 0))], [xs, xs_ctx]) if split else ([rt(D)], [xs])
    return _call(
        body, (dout, *x_args, z, y, mods, gvec, wgu, wd), comm, name=name, grid=(nt,),
        in_specs=[rt(D)] + x_specs + [rt(2 * F), rt(D), pl.BlockSpec((1, N_MOD, D), _typ(cfg)),
                                       pl.BlockSpec((6, D), lambda i: (0, 0)), ANY, ANY],
        out_specs=[lat if split else rt(D), rt(2 * F), rt(D), rt(F), pl.BlockSpec((1, 3, D), _typ(cfg)),
                   pl.BlockSpec((2, D), lambda i: (0, 0))],
        out_shape=[S_((cfg.S if split else R, D), f32), S_((R, 2 * F), bf16), S_((R, D), bf16), S_((R, F), bf16),
                   S_((ntyp, 3, D), f32), S_((2, D), f32)],
        scratch_shapes=[pltpu.VMEM((D, 2 * F), bf16), pltpu.VMEM((F, D), bf16), pltpu.SemaphoreType.DMA((2,))],
        compiler_params=_cp(VMEM_BIG, ("arbitrary",)),
    )


def _wgrad(a, b, k_rows, name, comm=None):
    M, N = a.shape[1], b.shape[1]
    tn = _div_tile(N, 1, WGRAD_TN, LANES) if N > WGRAD_TN // 2 else N
    tn = N // 2 if tn == N and N % (2 * LANES) == 0 else tn
    room = VMEM_BIG - WGRAD_SLACK - 2 * M * tn * 6
    tk = _div_tile(k_rows, 1, min(WGRAD_TK, room // (4 * (M + tn))), LANES)
    nk = k_rows // tk

    def body(a_ref, b_ref, o_ref, ob_ref):
        k = pl.program_id(1)

        @pl.when(k == 0)
        def _():
            o_ref[...] = jnp.zeros_like(o_ref)
        o_ref[...] += _tn(a_ref[...], b_ref[...])

        @pl.when(k == nk - 1)
        def _():
            ob_ref[...] = o_ref[...].astype(bf16)

    ospec = pl.BlockSpec((M, tn), lambda n, k: (0, n))
    return _call(
        body, (a, b), comm, name=name, grid=(N // tn, nk),
        in_specs=[pl.BlockSpec((tk, M), lambda n, k: (k, 0)), pl.BlockSpec((tk, tn), lambda n, k: (k, n))],
        out_specs=[ospec, ospec], out_shape=[S_((M, N), f32), S_((M, N), bf16)],
        compiler_params=_cp(VMEM_BIG, ("arbitrary", "arbitrary")),
    )


def _tmpre_fwd(cfg, xs, mods, gvec, w_in, cos, sin, name):
    TM, D = cfg.TM, cfg.D
    nt, R = cfg.ntt, cfg.T
    W = NA_WIDTH

    def body(xs_ref, mods_ref, g_ref, w_ref, cos_ref, sin_ref, hb_ref, q_ref, k_ref, v_ref, u_ref):
        x = xs_ref[...]
        m = mods_ref[0]
        xhat, _ = _rms_hat(x)
        hb = ((xhat * g_ref[2:3]) * (1.0 + m[4:5]) + m[3:4]).astype(bf16)
        hb_ref[...] = hb
        p = _nn(hb, w_ref[...])
        cs = jnp.tile(cos_ref[...], (1, W // LANES))
        sn = jnp.tile(sin_ref[...], (1, W // LANES))
        q = p[:, 0:W]
        k = p[:, W:2 * W]
        q_ref[...] = ((q * cs + _swap16(q) * sn) * (HEAD_DIM ** -0.5)).astype(bf16)
        k_ref[...] = (k * cs + _swap16(k) * sn).astype(bf16)
        v_ref[...] = p[:, 2 * W:3 * W].astype(bf16)
        u_ref[...] = p[:, 3 * W:]

    rt = lambda c: pl.BlockSpec((TM, c), lambda i: (i, 0))
    return pl.pallas_call(
        body, name=name, grid=(nt,),
        in_specs=[rt(D), pl.BlockSpec((1, N_MOD, D), _typ(cfg)), pl.BlockSpec((6, D), lambda i: (0, 0)),
                  pl.BlockSpec((D, IN_WIDTH), lambda i: (0, 0)), rt(LANES), rt(LANES)],
        out_specs=[rt(D), rt(W), rt(W), rt(W), rt(POOL_WIDTH)],
        out_shape=[S_((R, D), bf16), S_((R, W), bf16), S_((R, W), bf16), S_((R, W), bf16), S_((R, POOL_WIDTH), f32)],
        compiler_params=_cp(VMEM_MID, ("arbitrary",)),
    )(xs, mods, gvec, w_in, cos, sin)


def _tmpre_bwd(cfg, lat, ctx_terms, du_has_ctx, cos, sin, w_in, xs, mods, gvec, dres, res_with_ctx, name):
    TM, D = cfg.TM, cfg.D
    nt, R = cfg.ntt, cfg.T
    nres = cfg.ntiles(res_with_ctx)
    W = NA_WIDTH
    n_ctx = [len(t) for t in ctx_terms]
    flat_ctx = [a for t in ctx_terms for a in t]
    n_asm = 4 + len(flat_ctx) + 2

    def assemble(refs, o_ref):
        dq_ref, dk_ref, dv_ref, du_ref = refs[:4]
        ctx_refs = refs[4:4 + len(flat_ctx)]
        cos_ref, sin_ref = refs[4 + len(flat_ctx):]
        is_ctx = pl.program_id(0) >= cfg.nxt
        vals, off = [], 0
        for lat_ref, n in zip((dq_ref, dk_ref, dv_ref), n_ctx):
            cv = jnp.zeros((TM, W), f32)
            for r_ in ctx_refs[off:off + n]:
                cv = cv + r_[...]
            off += n
            vals.append(jnp.where(is_ctx, cv, lat_ref[...]))
        du_ = du_ref[...] if du_has_ctx else jnp.where(is_ctx, 0.0, du_ref[...])
        cs = jnp.tile(cos_ref[...], (1, W // LANES))
        sn = jnp.tile(sin_ref[...], (1, W // LANES))
        dq_ = vals[0] * (HEAD_DIM ** -0.5)
        dk_ = vals[1]
        o_ref[:, 0:W] = (dq_ * cs + _swap16(dq_ * sn)).astype(bf16)
        o_ref[:, W:2 * W] = (dk_ * cs + _swap16(dk_ * sn)).astype(bf16)
        o_ref[:, 2 * W:3 * W] = vals[2].astype(bf16)
        o_ref[:, 3 * W:] = du_.astype(bf16)

    def body(*refs):
        w_ref, xs_ref, mods_ref, g_ref, dres_ref, dx_ref, dp_ref, dm_ref, dg_ref = refs[n_asm:]
        i = pl.program_id(0)

        @pl.when(i == 0)
        def _():
            dg_ref[...] = jnp.zeros_like(dg_ref)

        @pl.when((i == 0) | (i == cfg.nxt))
        def _():
            dm_ref[...] = jnp.zeros_like(dm_ref)

        assemble(refs[:n_asm], dp_ref)
        dh = _nt(dp_ref[...], w_ref[...])
        x = xs_ref[...]
        m = mods_ref[0]
        g2 = g_ref[2:3]
        xhat, rinv = _rms_hat(x)
        d_sh = _rsum(dh)
        d_sc = _rsum(dh * (xhat * g2))
        dn = dh * (1.0 + m[4:5])
        dg_ref[...] += _rsum(dn * xhat)
        dx = _rms_bwd(dn * g2, xhat, rinv)
        res = dres_ref[...]
        if nres < nt:
            res = jnp.where(i < nres, res, 0.0)
        dx_ref[...] = res + dx
        dm_ref[0] += jnp.concatenate([d_sh, d_sc], axis=0)

    rt = lambda c: pl.BlockSpec((TM, c), lambda i: (i, 0))
    lat_spec = pl.BlockSpec((TM, W), lambda i: (jnp.minimum(i, cfg.nxt - 1), 0))
    du_spec = rt(POOL_WIDTH) if du_has_ctx else lat_spec
    asm_specs = ([lat_spec, lat_spec, lat_spec, du_spec] + [pl.BlockSpec((TM, W), lambda i: (0, 0))] * len(flat_ctx)
                 + [rt(LANES), rt(LANES)])
    return pl.pallas_call(
        body, name=name, grid=(nt,),
        in_specs=asm_specs + [pl.BlockSpec((D, IN_WIDTH), lambda i: (0, 0)), rt(D),
                              pl.BlockSpec((1, N_MOD, D), _typ(cfg)), pl.BlockSpec((6, D), lambda i: (0, 0)),
                              pl.BlockSpec((TM, D), lambda i: (jnp.minimum(i, nres - 1), 0))],
        out_specs=[rt(D), rt(IN_WIDTH), pl.BlockSpec((1, 2, D), _typ(cfg)), pl.BlockSpec((1, D), lambda i: (0, 0))],
        out_shape=[S_((R, D), f32), S_((R, IN_WIDTH), bf16), S_((2, 2, D), f32), S_((1, D), f32)],
        compiler_params=_cp(VMEM_MID, ("arbitrary",)),
    )(*lat, *flat_ctx, cos, sin, w_in, xs, mods, gvec, dres)


def _na_block(cfg, b):
    return jnp.clip(NA_QR * b - NA_KH // 2, 0, cfg.rows - NA_WR)


def _na_load_bias(b, nb, b_hbm, b_v, sem):
    for typ, at in ((0, 0), (1, 1), (2, nb - 1)):
        @pl.when(b == at)
        def _(typ=typ):
            cp = pltpu.make_async_copy(b_hbm.at[typ], b_v, sem)
            cp.start()
            cp.wait()


def _na_exps(qh, klh, kch, bias):
    s_loc = _nt(qh, klh) + bias
    s_ctx = _nt(qh, kch)
    mx = jnp.maximum(jnp.max(s_loc, axis=-1, keepdims=True), jnp.max(s_ctx, axis=-1, keepdims=True))
    e_loc = jnp.exp(s_loc - mx)
    e_ctx = jnp.exp(s_ctx - mx)
    inv = 1.0 / (jnp.sum(e_loc, axis=-1, keepdims=True) + jnp.sum(e_ctx, axis=-1, keepdims=True))
    return e_loc, e_ctx, inv


def _na_probs(qh, klh, kch, bias):
    e_loc, e_ctx, inv = _na_exps(qh, klh, kch, bias)
    return e_loc * inv, e_ctx * inv


def _na_fwd(cfg, q, k, v, bexp, name, comm=None):
    S, L, T = cfg.S, cfg.L, cfg.T
    NQ, NW = NA_QR * GRID_W, NA_WR * GRID_W
    nb = cfg.rows // NA_QR

    def body(q_ref, k_hbm, v_hbm, b_hbm, o_ref, k_v, v_v, b_v, sem):
        b = pl.program_id(0)

        @pl.when(b == 0)
        def _():
            cs = [pltpu.make_async_copy(k_hbm, k_v, sem.at[0]), pltpu.make_async_copy(v_hbm, v_v, sem.at[1])]
            for c_ in cs:
                c_.start()
            for c_ in cs:
                c_.wait()

        _na_load_bias(b, nb, b_hbm, b_v, sem.at[2])
        st = pl.multiple_of(_na_block(cfg, b) * GRID_W, GRID_W)
        first = lax.broadcasted_iota(jnp.int32, (NQ, LANES), 1) < HEAD_DIM
        for hp in range(NA_HEADS // 2):
            ls = slice(hp * LANES, (hp + 1) * LANES)
            q2 = q_ref[:, ls]
            kl, vl = k_v[pl.ds(st, NW), ls], v_v[pl.ds(st, NW), ls]
            kc, vc = k_v[S:T, ls], v_v[S:T, ls]
            o2 = []
            for hh in range(2):
                qm = jnp.where(first if hh == 0 else ~first, q2, jnp.zeros_like(q2))
                e_loc, e_ctx, inv = _na_exps(qm, kl, kc, b_v[2 * hp + hh])
                o2.append((_nn(e_loc.astype(bf16), vl) + _nn(e_ctx.astype(bf16), vc)) * inv)
            o_ref[:, ls] = jnp.where(first, o2[0], o2[1]).astype(bf16)

    return _call(
        body, (q, k, v, bexp), comm, name=name, grid=(nb,),
        in_specs=[pl.BlockSpec((NQ, NA_WIDTH), lambda b: (b, 0)), ANY, ANY, ANY],
        out_specs=[pl.BlockSpec((NQ, NA_WIDTH), lambda b: (b, 0))],
        out_shape=[S_((S, NA_WIDTH), bf16)],
        scratch_shapes=[pltpu.VMEM((T, NA_WIDTH), bf16), pltpu.VMEM((T, NA_WIDTH), bf16),
                        pltpu.VMEM((NA_HEADS, NQ, NW), f32), pltpu.SemaphoreType.DMA((3,))],
        compiler_params=_cp(VMEM_MID, ("arbitrary",)),
    )


def _na_bwd(cfg, do, o, q, k, v, bexp, name, comm=None):
    S, L, T, rows = cfg.S, cfg.L, cfg.T, cfg.rows
    NQ, NW = NA_QR * GRID_W, NA_WR * GRID_W
    NSLOT = 2 * NA_KH
    nb = rows // NA_QR
    bmax = (rows - NA_WR) // NA_QR
    steps = 2 * nb - bmax
    W = NA_WIDTH
    assert nb >= 3 and bmax >= 1 and rows - NA_QR * bmax <= NSLOT

    def out_group(g):
        return jnp.where(g >= nb, g - nb + bmax, jnp.clip(g - 1, 0, bmax - 1))

    def body(do_ref, o_ref, q_ref, k_hbm, v_hbm, b_hbm, dq_ref, dk_ref, dv_ref, dkc_ref, dvc_ref, db_hbm,
             k_v, v_v, b_v, db_v, ak, av, akc, avc, sem):
        g = pl.program_id(0)

        @pl.when(g == 0)
        def _():
            cs = [pltpu.make_async_copy(k_hbm, k_v, sem.at[0]), pltpu.make_async_copy(v_hbm, v_v, sem.at[1])]
            for c_ in cs:
                c_.start()
            db_v[...] = jnp.zeros_like(db_v)
            ak[...] = jnp.zeros_like(ak)
            av[...] = jnp.zeros_like(av)
            akc[...] = jnp.zeros_like(akc)
            avc[...] = jnp.zeros_like(avc)
            for c_ in cs:
                c_.wait()

        for typ, at in ((0, 1), (1, nb - 1)):
            @pl.when(g == at)
            def _(typ=typ):
                cp = pltpu.make_async_copy(db_v, db_hbm.at[typ], sem.at[2])
                cp.start()
                cp.wait()
                db_v[...] = jnp.zeros_like(db_v)

        @pl.when(g < nb)
        def _():
            _na_load_bias(g, nb, b_hbm, b_v, sem.at[2])
            ws = _na_block(cfg, g)
            st = pl.multiple_of(ws * GRID_W, GRID_W)
            first = lax.broadcasted_iota(jnp.int32, (NQ, LANES), 1) < HEAD_DIM
            for hp in range(NA_HEADS // 2):
                ls = slice(hp * LANES, (hp + 1) * LANES)
                q2, do2 = q_ref[:, ls], do_ref[:, ls]
                o2 = o_ref[:, ls].astype(f32)
                kl, vl = k_v[pl.ds(st, NW), ls], v_v[pl.ds(st, NW), ls]
                kc, vc = k_v[S:T, ls], v_v[S:T, ls]
                dq2 = []
                dk2 = jnp.zeros((NW, LANES), f32)
                dv2 = jnp.zeros((NW, LANES), f32)
                dkc2 = jnp.zeros((L, LANES), f32)
                dvc2 = jnp.zeros((L, LANES), f32)
                for hh in range(2):
                    keep = first if hh == 0 else ~first
                    qm = jnp.where(keep, q2, jnp.zeros_like(q2))
                    dom = jnp.where(keep, do2, jnp.zeros_like(do2))
                    p_loc, p_ctx = _na_probs(qm, kl, kc, b_v[2 * hp + hh])
                    dp_loc = _nt(dom, vl)
                    dp_ctx = _nt(dom, vc)
                    delta = jnp.sum(dom.astype(f32) * o2, axis=-1, keepdims=True)
                    ds_loc = p_loc * (dp_loc - delta)
                    ds_ctx = p_ctx * (dp_ctx - delta)
                    db_v[2 * hp + hh, :, 0:NW] += ds_loc
                    dsl, dsc = ds_loc.astype(bf16), ds_ctx.astype(bf16)
                    dq2.append(_nn(dsl, kl) + _nn(dsc, kc))
                    dk2 = dk2 + _tn(dsl, qm)
                    dv2 = dv2 + _tn(p_loc.astype(bf16), dom)
                    dkc2 = dkc2 + _tn(dsc, qm)
                    dvc2 = dvc2 + _tn(p_ctx.astype(bf16), dom)
                dq_ref[:, ls] = jnp.where(first, dq2[0], dq2[1])
                akc[:, ls] += dkc2
                avc[:, ls] += dvc2
                for kk in range(NA_WR):
                    slot = (ws + kk) % NSLOT
                    ak[slot, :, ls] += dk2[kk * GRID_W:(kk + 1) * GRID_W, :]
                    av[slot, :, ls] += dv2[kk * GRID_W:(kk + 1) * GRID_W, :]

        @pl.when(((g >= 1) & (g <= bmax)) | (g >= nb))
        def _():
            base = NA_QR * (out_group(g) % (NSLOT // NA_QR))
            for t in range(NA_QR):
                dk_ref[t * GRID_W:(t + 1) * GRID_W, :] = ak[base + t]
                dv_ref[t * GRID_W:(t + 1) * GRID_W, :] = av[base + t]
                ak[base + t] = jnp.zeros((GRID_W, W), f32)
                av[base + t] = jnp.zeros((GRID_W, W), f32)

        @pl.when(g == nb - 1)
        def _():
            cp = pltpu.make_async_copy(db_v, db_hbm.at[2], sem.at[2])
            cp.start()
            cp.wait()

        @pl.when(g == steps - 1)
        def _():
            dkc_ref[...] = akc[...]
            dvc_ref[...] = avc[...]

    qmap = lambda g: (jnp.minimum(g, nb - 1), 0)
    kmap = lambda g: (out_group(g), 0)
    full = lambda g: (0, 0)
    return _call(
        body, (do, o, q, k, v, bexp), comm, name=name, grid=(steps,),
        in_specs=[pl.BlockSpec((NQ, W), qmap), pl.BlockSpec((NQ, W), qmap), pl.BlockSpec((NQ, W), qmap), ANY, ANY, ANY],
        out_specs=[pl.BlockSpec((NQ, W), qmap), pl.BlockSpec((NQ, W), kmap), pl.BlockSpec((NQ, W), kmap),
                   pl.BlockSpec((L, W), full), pl.BlockSpec((L, W), full), ANY],
        out_shape=[S_((S, W), f32), S_((S, W), f32), S_((S, W), f32), S_((L, W), f32), S_((L, W), f32),
                   S_((NA_TYPES, NA_HEADS, NQ, NA_WPAD), f32)],
        scratch_shapes=[pltpu.VMEM((T, W), bf16), pltpu.VMEM((T, W), bf16),
                        pltpu.VMEM((NA_HEADS, NQ, NW), f32), pltpu.VMEM((NA_HEADS, NQ, NA_WPAD), f32),
                        pltpu.VMEM((NSLOT, GRID_W, W), f32), pltpu.VMEM((NSLOT, GRID_W, W), f32),
                        pltpu.VMEM((L, W), f32), pltpu.VMEM((L, W), f32), pltpu.SemaphoreType.DMA((3,))],
        compiler_params=_cp(VMEM_BIG, ("arbitrary",)),
    )


def _rpb_reduce(dbias, flip, sel, name):
    nq, w = NA_QR * GRID_W, GRID_W

    def diag_body(x_ref, j_ref, o_ref):
        rows = []
        for i in range(NA_QR):
            xr = jnp.dot(j_ref[...], x_ref[i * w:(i + 1) * w, :], preferred_element_type=f32, precision=lax.Precision.HIGHEST)
            rows.append(jnp.sum(pltpu.roll(xr, 0, 1, stride=1, stride_axis=0), axis=0, keepdims=True))
        o_ref[...] = jnp.concatenate(rows + [jnp.zeros((8 - NA_QR, NA_WPAD), f32)], axis=0)

    diag = pl.pallas_call(
        diag_body, name=name + "_diag", grid=(NA_TYPES, NA_HEADS),
        in_specs=[pl.BlockSpec((None, None, nq, NA_WPAD), lambda t, h: (t, h, 0, 0)), pl.BlockSpec((w, w), lambda t, h: (0, 0))],
        out_specs=pl.BlockSpec((None, None, 8, NA_WPAD), lambda t, h: (t, h, 0, 0)),
        out_shape=S_((NA_TYPES, NA_HEADS, 8, NA_WPAD), f32),
        compiler_params=_cp(VMEM_MID, ("arbitrary", "arbitrary")),
    )(dbias, flip)
    lo = w - NA_KW
    y = diag[:, :, :NA_QR, lo:lo + NA_WR * w].reshape(NA_TYPES, NA_HEADS, NA_QR, NA_WR, w)
    y = jnp.transpose(y, (1, 0, 2, 3, 4)).reshape(NA_HEADS, NA_TYPES * NA_QR * NA_WR, w)
    y = jnp.pad(y, ((0, 0), (0, NA_SEL_ROWS - y.shape[1]), (0, LANES - w)))

    def body(y_ref, sel_ref, o_ref):
        o_ref[...] = jnp.dot(sel_ref[...], y_ref[...], preferred_element_type=f32, precision=lax.Precision.HIGHEST)

    return pl.pallas_call(
        body, name=name, grid=(NA_HEADS,),
        in_specs=[pl.BlockSpec((None, NA_SEL_ROWS, LANES), lambda h: (h, 0, 0)), pl.BlockSpec((16, NA_SEL_ROWS), lambda h: (0, 0))],
        out_specs=pl.BlockSpec((None, 16, LANES), lambda h: (h, 0, 0)),
        out_shape=S_((NA_HEADS, 16, LANES), f32),
        compiler_params=_cp(VMEM_MID, ("arbitrary",)),
    )(y, sel)


def _ctx_attn_fwd(cfg, q, k, v, name):
    L = cfg.L
    blk = cfg.S // L

    def body(q_ref, k_ref, v_ref, o_ref):
        qv, kv, vv = q_ref[...], k_ref[...], v_ref[...]
        outs = []
        for h in range(NA_HEADS):
            hs = slice(h * HEAD_DIM, (h + 1) * HEAD_DIM)
            s = _nt(qv[:, hs], kv[:, hs])
            e = jnp.exp(s - jnp.max(s, axis=-1, keepdims=True))
            p = e * (1.0 / jnp.sum(e, axis=-1, keepdims=True))
            outs.append(_nn(p.astype(bf16), vv[:, hs]))
        o_ref[...] = jnp.concatenate(outs, axis=-1).astype(bf16)

    spec = pl.BlockSpec((L, NA_WIDTH), lambda i: (blk, 0))
    return pl.pallas_call(
        body, name=name, grid=(1,), in_specs=[spec, spec, spec],
        out_specs=pl.BlockSpec((L, NA_WIDTH), lambda i: (0, 0)), out_shape=S_((L, NA_WIDTH), bf16),
        compiler_params=_cp(VMEM_MID, ("arbitrary",)),
    )(q, k, v)


def _ctx_attn_bwd(cfg, do, q, k, v, name):
    L = cfg.L
    blk = cfg.S // L

    def body(do_ref, q_ref, k_ref, v_ref, dq_ref, dk_ref, dv_ref):
        dov, qv, kv, vv = do_ref[...], q_ref[...], k_ref[...], v_ref[...]
        dqs, dks, dvs = [], [], []
        for h in range(NA_HEADS):
            hs = slice(h * HEAD_DIM, (h + 1) * HEAD_DIM)
            qh, kh, doh = qv[:, hs], kv[:, hs], dov[:, hs]
            s = _nt(qh, kh)
            e = jnp.exp(s - jnp.max(s, axis=-1, keepdims=True))
            p = e * (1.0 / jnp.sum(e, axis=-1, keepdims=True))
            dp = _nt(doh, vv[:, hs])
            ds = (p * (dp - jnp.sum(p * dp, axis=-1, keepdims=True))).astype(bf16)
            dqs.append(_nn(ds, kh))
            dks.append(_tn(ds, qh))
            dvs.append(_tn(p.astype(bf16), doh))
        dq_ref[...] = jnp.concatenate(dqs, axis=-1)
        dk_ref[...] = jnp.concatenate(dks, axis=-1)
        dv_ref[...] = jnp.concatenate(dvs, axis=-1)

    spec = pl.BlockSpec((L, NA_WIDTH), lambda i: (blk, 0))
    ospec = pl.BlockSpec((L, NA_WIDTH), lambda i: (0, 0))
    return pl.pallas_call(
        body, name=name, grid=(1,), in_specs=[spec, spec, spec, spec],
        out_specs=[ospec, ospec, ospec], out_shape=[S_((L, NA_WIDTH), f32)] * 3,
        compiler_params=_cp(VMEM_MID, ("arbitrary",)),
    )(do, q, k, v)


def _pool_centered(u, band, inv):
    return _split_sum(_nn, band, u) * inv - u


def _split_sum(mm, band, t):
    hi = t.astype(bf16)
    lo = (t - hi.astype(f32)).astype(bf16)
    s = mm(band, jnp.concatenate([hi, lo], axis=1))
    n = t.shape[1]
    return s[:, :n] + s[:, n:]


def _pool_mix(u_ref, band_ref, inv_ref, w_ref, ps_ref):
    C = POOL_CH
    outs = []
    for g in range(POOL_GROUPS):
        d = _pool_centered(u_ref[:, g * C:(g + 1) * C], band_ref[0, g], inv_ref[0, g])
        outs.append(_nn(d.astype(bf16), w_ref[g].astype(bf16)) * ps_ref[:, g * C:(g + 1) * C])
    return jnp.concatenate(outs, axis=-1).astype(bf16)


def _pool_bwd(cfg, dmix, u, band, inv, w_pool, pool_scale, with_ctx, name):
    TM = cfg.TM
    nt = cfg.ntiles(with_ctx)
    C = POOL_CH

    def body(dy_ref, u_ref, band_ref, inv_ref, w_ref, ps_ref, du_ref, dw_ref, dps_ref):
        @pl.when(pl.program_id(0) == 0)
        def _():
            dw_ref[...] = jnp.zeros_like(dw_ref)
            dps_ref[...] = jnp.zeros_like(dps_ref)

        dus, dpss = [], []
        for g in range(POOL_GROUPS):
            gs = slice(g * C, (g + 1) * C)
            band_g, inv_g = band_ref[0, g], inv_ref[0, g]
            db = _pool_centered(u_ref[:, gs], band_g, inv_g).astype(bf16)
            wb = w_ref[g].astype(bf16)
            dy = dy_ref[:, gs].astype(f32)
            dpss.append(_rsum(dy * _nn(db, wb)))
            dys = (dy * ps_ref[:, gs]).astype(bf16)
            dw_ref[g] += _tn(db, dys)
            dd = _nt(dys, wb)
            dus.append(_split_sum(_tn, band_g, dd * inv_g) - dd)
        du_ref[...] = jnp.concatenate(dus, axis=-1)
        dps_ref[...] += jnp.concatenate(dpss, axis=-1)

    typ4 = lambda i: (jnp.minimum(i // cfg.nxt, 1), 0, 0, 0)
    return pl.pallas_call(
        body, name=name, grid=(nt,),
        in_specs=[pl.BlockSpec((TM, POOL_WIDTH), lambda i: (i, 1)), pl.BlockSpec((TM, POOL_WIDTH), lambda i: (i, 0)),
                  pl.BlockSpec((1, POOL_GROUPS, TM, TM), typ4), pl.BlockSpec((1, POOL_GROUPS, TM, 1), typ4),
                  pl.BlockSpec((POOL_GROUPS, C, C), lambda i: (0, 0, 0)), pl.BlockSpec((1, POOL_WIDTH), lambda i: (0, 0))],
        out_specs=[pl.BlockSpec((TM, POOL_WIDTH), lambda i: (i, 0)), pl.BlockSpec((POOL_GROUPS, C, C), lambda i: (0, 0, 0)),
                   pl.BlockSpec((1, POOL_WIDTH), lambda i: (0, 0))],
        out_shape=[S_((nt * TM, POOL_WIDTH), f32), S_((POOL_GROUPS, C, C), f32), S_((1, POOL_WIDTH), f32)],
        compiler_params=_cp(VMEM_MID, ("arbitrary",)),
    )(dmix, u, band, inv, w_pool, pool_scale)


def _tmpost_fwd(cfg, na_x, na_c, u, band, inv, w_pool, pool_scale, w_out, xs, mods, gvec, name):
    TM, D = cfg.TM, cfg.D
    with_ctx = na_c is not None
    nt = cfg.ntiles(with_ctx)
    R = nt * TM

    def body(*refs):
        if with_ctx:
            nax_ref, nac_ref = refs[:2]
            na = jnp.where(pl.program_id(0) < cfg.nxt, nax_ref[...], nac_ref[...])
        else:
            na = refs[0][...]
        (u_ref, band_ref, inv_ref, wp_ref, ps_ref, w_ref, xs_ref, mods_ref, g_ref,
         out_ref, opre_ref, mix_ref) = refs[2 if with_ctx else 1:]
        pool_v = _pool_mix(u_ref, band_ref, inv_ref, wp_ref, ps_ref)
        mix_ref[:, 0:NA_WIDTH] = na
        mix_ref[:, NA_WIDTH:] = pool_v
        o = _nn(na, w_ref[0:NA_WIDTH, :]) + _nn(pool_v, w_ref[NA_WIDTH:, :])
        opre_ref[...] = o
        ohat, _ = _rms_hat(o)
        out_ref[...] = xs_ref[...] + mods_ref[0][5:6] * (ohat * g_ref[3:4])

    rt = lambda c: pl.BlockSpec((TM, c), lambda i: (i, 0))
    na_specs = [pl.BlockSpec((TM, NA_WIDTH), lambda i: (jnp.minimum(i, cfg.nxt - 1), 0))]
    na_args = [na_x]
    if with_ctx:
        na_specs.append(pl.BlockSpec((TM, NA_WIDTH), lambda i: (0, 0)))
        na_args.append(na_c)
    typ4 = lambda i: (jnp.minimum(i // cfg.nxt, 1), 0, 0, 0)
    pool_specs = [rt(POOL_WIDTH), pl.BlockSpec((1, POOL_GROUPS, TM, TM), typ4), pl.BlockSpec((1, POOL_GROUPS, TM, 1), typ4),
                  pl.BlockSpec((POOL_GROUPS, POOL_CH, POOL_CH), lambda i: (0, 0, 0)), pl.BlockSpec((1, POOL_WIDTH), lambda i: (0, 0))]
    return pl.pallas_call(
        body, name=name, grid=(nt,),
        in_specs=na_specs + pool_specs + [pl.BlockSpec((MIX_WIDTH, D), lambda i: (0, 0)), rt(D),
                                          pl.BlockSpec((1, N_MOD, D), _typ(cfg)), pl.BlockSpec((6, D), lambda i: (0, 0))],
        out_specs=[rt(D), rt(D), rt(MIX_WIDTH)],
        out_shape=[S_((R, D), f32), S_((R, D), f32), S_((R, MIX_WIDTH), bf16)],
        compiler_params=_cp(VMEM_MID, ("arbitrary",)),
    )(*na_args, u, band, inv, w_pool, pool_scale, w_out, xs, mods, gvec)


def _tmpost_bwd(cfg, dout, opre, w_out, mods, gvec, with_ctx, name):
    TM, D = cfg.TM, cfg.D
    nt = cfg.ntiles(with_ctx)
    R = nt * TM
    ntyp = 2 if with_ctx else 1

    def body(do_ref, opre_ref, w_ref, mods_ref, g_ref, dop_ref, dmix_ref, dm_ref, dg_ref):
        i = pl.program_id(0)

        @pl.when(i == 0)
        def _():
            dg_ref[...] = jnp.zeros_like(dg_ref)

        @pl.when((i == 0) | (i == cfg.nxt))
        def _():
            dm_ref[...] = jnp.zeros_like(dm_ref)

        do = do_ref[...]
        g3 = g_ref[3:4]
        ohat, rinv = _rms_hat(opre_ref[...])
        dm_ref[0] += _rsum(do * (ohat * g3))
        dr = mods_ref[0][5:6] * do
        dg_ref[...] += _rsum(dr * ohat)
        dob = _rms_bwd(dr * g3, ohat, rinv).astype(bf16)
        dop_ref[...] = dob
        dmix_ref[...] = _nt(dob, w_ref[...]).astype(bf16)

    rt = lambda c: pl.BlockSpec((TM, c), lambda i: (i, 0))
    return pl.pallas_call(
        body, name=name, grid=(nt,),
        in_specs=[rt(D), rt(D), pl.BlockSpec((MIX_WIDTH, D), lambda i: (0, 0)),
                  pl.BlockSpec((1, N_MOD, D), _typ(cfg)), pl.BlockSpec((6, D), lambda i: (0, 0))],
        out_specs=[rt(D), rt(MIX_WIDTH), pl.BlockSpec((1, 1, D), _typ(cfg)), pl.BlockSpec((1, D), lambda i: (0, 0))],
        out_shape=[S_((R, D), bf16), S_((R, MIX_WIDTH), bf16), S_((ntyp, 1, D), f32), S_((1, D), f32)],
        compiler_params=_cp(VMEM_MID, ("arbitrary",)),
    )(dout, opre, w_out, mods, gvec)


def _modvec_fwd(cvecs, w_mod, b_shard, name):
    nl, D, n = w_mod.shape
    tn = n // 3 if (n % 3 == 0 and (n // 3) % LANES == 0) else n

    def body(c_ref, w_ref, b_ref, o_ref, s_ref):
        cv = c_ref[...]
        sv = cv * _sigmoid(cv)
        s_ref[...] = sv
        o_ref[...] = _nn(sv.astype(bf16), w_ref[...].astype(bf16)) + b_ref[...]

    return pl.pallas_call(
        body, name=name, grid=(nl, n // tn),
        in_specs=[pl.BlockSpec((16, D), lambda l, j: (0, 0)), pl.BlockSpec((None, D, tn), lambda l, j: (l, 0, j)),
                  pl.BlockSpec((None, 1, tn), lambda l, j: (l, 0, j))],
        out_specs=[pl.BlockSpec((None, 16, tn), lambda l, j: (l, 0, j)), pl.BlockSpec((16, D), lambda l, j: (0, 0))],
        out_shape=[S_((nl, 16, n), f32), S_((16, D), f32)],
        compiler_params=_cp(VMEM_MID, ("arbitrary", "arbitrary")),
    )(cvecs, w_mod, b_shard)


def _modvec_bwd(s_t, dm, w_mod, name):
    nl, D, n = w_mod.shape
    tn = n // 3 if (n % 3 == 0 and (n // 3) % LANES == 0) else n

    def body(s_ref, dm_ref, w_ref, gw_ref, gc_ref):
        @pl.when(pl.program_id(1) == 0)
        def _():
            gc_ref[...] = jnp.zeros_like(gc_ref)
        dmv = dm_ref[...]
        gw_ref[...] = jnp.dot(s_ref[...], dmv, preferred_element_type=f32, precision=lax.Precision.HIGHEST)
        gc_ref[...] += _nt(dmv[8:16].astype(bf16), w_ref[...].astype(bf16))

    return pl.pallas_call(
        body, name=name, grid=(nl, n // tn),
        in_specs=[pl.BlockSpec((D, 16), lambda l, j: (0, 0)), pl.BlockSpec((None, 16, tn), lambda l, j: (l, 0, j)),
                  pl.BlockSpec((None, D, tn), lambda l, j: (l, 0, j))],
        out_specs=[pl.BlockSpec((None, D, tn), lambda l, j: (l, 0, j)), pl.BlockSpec((None, 8, D), lambda l, j: (l, 0, 0))],
        out_shape=[S_((nl, D, n), f32), S_((nl, 8, D), f32)],
        compiler_params=_cp(VMEM_MID, ("arbitrary", "arbitrary")),
    )(s_t, dm, w_mod)


def _as2d(a):
    n = a.size
    if a.ndim >= 2 and a.shape[-1] % LANES == 0:
        return a.reshape(-1, a.shape[-1])
    if n % LANES == 0:
        return a.reshape(-1, LANES)
    return a.reshape(-1, a.shape[-1]) if a.ndim >= 2 else a.reshape(1, n)


def _row_tile(r, c, budget_elems):
    if r * c <= budget_elems or r % 8 != 0:
        return r
    t = r
    while t * c > budget_elems and t % 16 == 0:
        t //= 2
    return t


def _div_tile(r, c, budget_elems, mult=16):
    best = None
    for t in range(mult, r + 1, mult):
        if r % t == 0 and t * c <= budget_elems:
            best = t
    return best if best is not None else r


def _chip_index():
    return 2 * lax.axis_index("x") + lax.axis_index("y")


def _cast_into_place(shards, lead, axis, name):
    r, c = shards.shape[-2:]
    tr = _div_tile(r, c, 3 * ELEMWISE_BLOCK)
    nr = r // tr
    out_map = (lambda i: (i, _chip_index())) if axis == 1 else (lambda i: (_chip_index() * nr + i, 0))
    full2 = (r, c * N_CHIPS) if axis == 1 else (r * N_CHIPS, c)

    def body(a_ref, o_ref):
        o_ref[...] = a_ref[...].astype(bf16)

    return pl.pallas_call(
        body, name=name, grid=(nr,),
        in_specs=[pl.BlockSpec((None,) * len(lead) + (tr, c), lambda i: tuple(lead) + (i, 0))],
        out_specs=pl.BlockSpec((tr, c), out_map),
        out_shape=S_(full2, bf16), compiler_params=_cp(VMEM_MID, ("arbitrary",)),
    )(shards)


def _sum_devices8(own, land, axis, into, lead, name):
    _, rh, cs = land.shape
    tr = _div_tile(rh, cs, 2 * ELEMWISE_BLOCK)
    nr = rh // tr
    core = lambda: lax.axis_index("c")
    if axis == 1:
        own_map = lambda i: (core() * nr + i, _chip_index())
    else:
        own_map = lambda i: (_chip_index() * 2 * nr + core() * nr + i, 0)
    nl = len(lead)

    def land_spec(j):
        return pl.BlockSpec((None, tr, cs), lambda i: ((2 * _chip_index() + core() + j) % N_DEV, i, 0))

    def body(own_ref, *rest):
        acc = own_ref[...]
        for p_ref in rest[:N_DEV - 1]:
            acc = acc + p_ref[...].astype(f32)
        rest[-1][...] = acc

    return pl.pallas_call(
        body, name=name, grid=(nr,),
        in_specs=[pl.BlockSpec((tr, cs), own_map)] + [land_spec(j) for j in range(1, N_DEV)] + [ANY],
        out_specs=pl.BlockSpec((None,) * nl + (tr, cs), lambda i: tuple(lead) + (core() * nr + i, 0)),
        out_shape=S_(into.shape, f32), input_output_aliases={N_DEV: 0},
        compiler_params=_cp(VMEM_MID, ("arbitrary",)),
    )(own, *([land] * (N_DEV - 1)), into)


def _adamw(w, g, m, v, name, emit_grad=False, comm=None):
    shape = w.shape
    w2, g2, m2, v2 = _as2d(w), _as2d(g), _as2d(m), _as2d(v)
    r, c = w2.shape
    tr = _row_tile(r, c, ELEMWISE_BLOCK)
    c1 = 1.0 - ADAM_B1 ** ADAM_STEP
    c2 = 1.0 - ADAM_B2 ** ADAM_STEP
    n_out = 4 if emit_grad else 3

    def body(w_ref, g_ref, m_ref, v_ref, d_ref, mo_ref, vo_ref, *go_ref):
        gv = g_ref[...]
        mn = ADAM_B1 * m_ref[...] + (1.0 - ADAM_B1) * gv
        vn = ADAM_B2 * v_ref[...] + (1.0 - ADAM_B2) * (gv * gv)
        mo_ref[...] = mn
        vo_ref[...] = vn
        d_ref[...] = -ADAM_LR * ((mn / c1) / (jnp.sqrt(vn / c2) + ADAM_EPS) + ADAM_WD * w_ref[...])
        if emit_grad:
            go_ref[0][...] = gv

    spec = pl.BlockSpec((tr, c), lambda i: (i, 0))
    outs, res = _call(body, (w2, g2, m2, v2), comm, name=name, grid=(r // tr,), in_specs=[spec] * 4, out_specs=[spec] * n_out,
                      out_shape=[S_((r, c), f32)] * n_out, compiler_params=_cp(VMEM_MID, ("arbitrary",)))
    outs = tuple(o.reshape(shape) for o in outs)
    return outs if comm is None else (outs, res)


def _sum_devices(gathered, name):
    _, r, c = gathered.shape

    def body(a_ref, o_ref):
        acc = a_ref[0]
        for j in range(1, N_DEV):
            acc = acc + a_ref[j]
        o_ref[...] = acc

    tr = _row_tile(r, c, ELEMWISE_BLOCK // 4)
    return pl.pallas_call(
        body, name=name, grid=(r // tr,),
        in_specs=[pl.BlockSpec((N_DEV, tr, c), lambda i: (0, i, 0))], out_specs=pl.BlockSpec((tr, c), lambda i: (i, 0)),
        out_shape=S_((r, c), f32), compiler_params=_cp(VMEM_MID, ("arbitrary",)))(gathered)


def _all_gather_small(block, name):
    m_per, n = block.shape

    def body(x_ref, out_ref, send_sems, recv_sems, local_sem):
        x, y, c = _mesh_pos()
        me, sibling = (x, y, c), (x, y, 1 - c)
        chips = [(1 - x, y), (x, 1 - y), (1 - x, 1 - y)]

        def rows(px, py, pc):
            return out_ref.at[pl.ds((4 * px + 2 * py + pc) * m_per, m_per), :]

        def copy(k, blk, to, src=None):
            return pltpu.make_async_remote_copy(
                src_ref=rows(*blk) if src is None else src, dst_ref=rows(*blk),
                send_sem=send_sems.at[k], recv_sem=recv_sems.at[k], device_id=to, device_id_type=MESH)

        mine = pltpu.make_async_copy(x_ref, rows(*me), local_sem)
        mine.start()
        first = [copy(0, me, sibling, src=x_ref)]
        first += [copy(1 + j, me, (*chip, c), src=x_ref) for j, chip in enumerate(chips)]
        for cp in first:
            cp.start()
        passed = [copy(4 + j, (*chip, c), sibling) for j, chip in enumerate(chips)]
        for j, chip in enumerate(chips):
            copy(1 + j, (*chip, c), me).wait_recv()
            passed[j].start()
        copy(0, sibling, me).wait_recv()
        for j, chip in enumerate(chips):
            copy(4 + j, (*chip, 1 - c), me).wait_recv()
        for cp in first + passed:
            cp.wait_send()
        mine.wait()

    return pl.pallas_call(
        body, name=name, out_shape=S_((N_DEV * m_per, n), block.dtype),
        in_specs=[pl.BlockSpec(memory_space=pltpu.VMEM)], out_specs=pl.BlockSpec(memory_space=pltpu.VMEM),
        scratch_shapes=[pltpu.SemaphoreType.DMA((7,)), pltpu.SemaphoreType.DMA((7,)), pltpu.SemaphoreType.DMA],
        compiler_params=_cp(VMEM_MID),
    )(block)


def _pack_rows(arrays):
    flat = jnp.concatenate([a.reshape(-1) for a in arrays])
    pad = (-flat.size) % (8 * LANES)
    return jnp.pad(flat, (0, pad)).reshape(-1, LANES)


def _unpack_rows(packed, shapes):
    flat = packed.reshape(-1)
    out, off = [], 0
    for s in shapes:
        n = int(np.prod(s))
        out.append(flat[off:off + n].reshape(s))
        off += n
    return out


W_AXIS = {"gu": 1, "dn": 0, "wi": 1, "wo": 0}
SMALL_NAMES = ("dmods", "dg", "drpb", "dwp", "dps")


def _half_merge(bufs, name):
    nt = len(bufs)

    def body(*refs):
        outs = refs[nt:2 * nt]
        send_sems, recv_sems = refs[2 * nt:]
        x, y, c = _mesh_pos()

        def half(ref, h):
            rh = ref.shape[-2] // 2
            return ref.at[(slice(None),) * (len(ref.shape) - 2) + (pl.ds(h * rh, rh), slice(None))]

        cps = []
        for t in range(nt):
            cp = pltpu.make_async_remote_copy(
                src_ref=half(outs[t], c), dst_ref=half(outs[t], c), send_sem=send_sems.at[t], recv_sem=recv_sems.at[t],
                device_id=(x, y, 1 - c), device_id_type=MESH)
            cp.start()
            cps.append(cp)
        for t in range(nt):
            pltpu.make_async_remote_copy(
                src_ref=half(outs[t], 1 - c), dst_ref=half(outs[t], 1 - c), send_sem=send_sems.at[t], recv_sem=recv_sems.at[t],
                device_id=(x, y, 1 - c), device_id_type=MESH).wait_recv()
        for cp in cps:
            cp.wait_send()

    return pl.pallas_call(
        body, name=name, in_specs=[ANY] * nt, out_specs=[ANY] * nt, out_shape=[S_(b.shape, f32) for b in bufs],
        input_output_aliases={t: t for t in range(nt)},
        scratch_shapes=[pltpu.SemaphoreType.DMA((nt,)), pltpu.SemaphoreType.DMA((nt,))],
        compiler_params=_cp(VMEM_MID),
    )(*bufs)


def _local_step(cfg, x_lat, x_ctx, target, mods, norm_g, W, G, na_rpb, w_pool, pool_scale):
    S, L, T, D, F = cfg.S, cfg.L, cfg.T, cfg.D, cfg.F
    depth = norm_g.shape[0]
    cos, sin = _rope_tables(S, L)
    band, inv = _pool_tables(cfg.TM, L)
    flip, sel = _rpb_reduce_tables()

    assert depth == 2, "the carrier schedules below are written for two layers"
    fwd_carry = {"ffn_fwd_0_0": [("wi", 0), ("wo", 0), ("gu", 0, 1), ("dn", 0, 1)],
                 "na_fwd_0": [("gu", 1, 0), ("dn", 1, 0)],
                 "ffn_fwd_0_1": [("wi", 1), ("wo", 1), ("gu", 1, 1), ("dn", 1, 1)]}
    bwd_carry = {"na_bwd_1": [("gu", 1, 1), ("dn", 1, 1)], "ffn_bwd_1_0": [("wi", 1), ("wo", 1)],
                 "ffn_bwd_0_1": [("gu", 1, 0), ("dn", 1, 0)], "na_bwd_0": [("gu", 0, 1), ("dn", 0, 1)],
                 "ffn_bwd_0_0": [("wi", 0), ("wo", 0)], "wgrad_dn_0_0": [("gu", 0, 0)]}
    last_scatter = [("dn", 0, 0)]
    tag = lambda key: "_".join(str(p) for p in key)
    g_f32, g_b16 = {}, {}

    def gather_on(name):
        keys = fwd_carry.get(name)
        return None if keys is None else _gather_comm([W[k_] for k_ in keys], [W_AXIS[k_[0]] for k_ in keys])

    def gathered(name, res):
        if name in fwd_carry:
            W.update(zip(fwd_carry[name], res))

    def scatter_on(name):
        keys = bwd_carry.get(name)
        return None if keys is None else _scatter_comm([g_b16[k_] for k_ in keys], [W_AXIS[k_[0]] for k_ in keys])

    def scattered(keys, lands):
        for key, land in zip(keys, lands):
            G[key[0]] = _sum_devices8(g_f32[key], land, W_AXIS[key[0]], G[key[0]], key[1:], f"sum8_{tag(key)}")

    small_landed = []

    def wgrad(key, a, b, rows, other_comm=None):
        name = f"wgrad_{tag(key)}"
        if other_comm is not None:
            assert name not in bwd_carry
            (g_f32[key], g_b16[key]), res = _wgrad(a, b, rows, name, other_comm)
            small_landed.extend(res)
            return
        (g_f32[key], g_b16[key]), lands = _wgrad(a, b, rows, name, scatter_on(name))
        scattered(bwd_carry.get(name, ()), lands)

    saved = []
    xs, xs_ctx = x_lat, x_ctx
    for l in range(depth):
        last = l == depth - 1
        wc = not last
        gvec = norm_g[l]
        ps = pool_scale[l].reshape(1, POOL_WIDTH)
        bexp = _expand_rpb(na_rpb[l], f"bias_expand_{l}")
        name = f"ffn_fwd_{l}_0"
        (xs1, hb1, z1, y1), res = _ffn_fwd(cfg, xs, mods[l], gvec, W["gu", l, 0], W["dn", l, 0], 0, 0, True, name,
                                           gather_on(name), xs_ctx=xs_ctx)
        gathered(name, res)
        hb2, q, k, v, u = _tmpre_fwd(cfg, xs1, mods[l], gvec, W["wi", l], cos, sin, f"tmpre_fwd_{l}")
        name = f"na_fwd_{l}"
        (na_x,), res = _na_fwd(cfg, q, k, v, bexp, name, gather_on(name))
        gathered(name, res)
        na_c = _ctx_attn_fwd(cfg, q, k, v, f"ctx_attn_fwd_{l}") if wc else None
        xs2, opre, mix = _tmpost_fwd(cfg, na_x, na_c, u, band, inv, w_pool[l], ps, W["wo", l], xs1, mods[l], gvec,
                                     f"tmpost_fwd_{l}")
        name = f"ffn_fwd_{l}_1"
        outs, res = _ffn_fwd(cfg, xs2, mods[l], gvec, W["gu", l, 1], W["dn", l, 1], 6, 4, wc, name, gather_on(name),
                             loss_target=target if last else None)
        xs3, hb3, z3, y3 = outs[:4]
        gathered(name, res)
        saved.append(dict(xs=xs, xs_ctx=xs_ctx, xs1=xs1, xs2=xs2, hb1=hb1, z1=z1, y1=y1, hb2=hb2, q=q, k=k, v=v, u=u, mix=mix,
                          opre=opre, hb3=hb3, z3=z3, y3=y3, bexp=bexp, ps=ps, gvec=gvec))
        xs, xs_ctx = xs3, None

    dxs, loss_blk = xs, outs[4]

    small = [None] * depth
    for l in reversed(range(depth)):
        last = l == depth - 1
        wc = not last
        sv = saved[l]
        gvec = sv["gvec"]
        rows_b = cfg.T if wc else cfg.S
        name = f"ffn_bwd_{l}_1"
        (dxs2, dz, dyb, ab, dm678, dg45), lands = _ffn_bwd(cfg, dxs, sv["xs2"], sv["z3"], sv["y3"], mods[l], gvec,
                                                           W["gu", l, 1], W["dn", l, 1], 6, 4, wc, name, scatter_on(name))
        scattered(bwd_carry.get(name, ()), lands)
        wgrad(("gu", l, 1), sv["hb3"], dz, rows_b)
        wgrad(("dn", l, 1), ab, dyb, rows_b)
        dop, dmix, dm5, dg3 = _tmpost_bwd(cfg, dxs2, sv["opre"], W["wo", l], mods[l], gvec, wc, f"tmpost_bwd_{l}")
        wgrad(("wo", l), sv["mix"], dop, rows_b)
        du, dwp, dps = _pool_bwd(cfg, dmix, sv["u"], band, inv, w_pool[l], sv["ps"], wc, f"pool_bwd_{l}")
        name = f"na_bwd_{l}"
        (dq, dk, dv, dkc, dvc, dbexp), lands = _na_bwd(cfg, dmix, sv["mix"], sv["q"], sv["k"], sv["v"], sv["bexp"], name,
                                                       scatter_on(name))
        scattered(bwd_carry.get(name, ()), lands)
        drpb = _rpb_reduce(dbexp, flip, sel, f"rpb_reduce_{l}")
        if wc:
            dqc, dkc2, dvc2 = _ctx_attn_bwd(cfg, dmix, sv["q"], sv["k"], sv["v"], f"ctx_attn_bwd_{l}")
            ctx_terms = ([dqc], [dkc, dkc2], [dvc, dvc2])
        else:
            ctx_terms = ([], [dkc], [dvc])
        dxs1, dproj, dm34, dg2 = _tmpre_bwd(cfg, (dq, dk, dv, du), ctx_terms, wc, cos, sin, W["wi", l], sv["xs1"], mods[l], gvec,
                                            dxs2, wc, f"tmpre_bwd_{l}")
        wgrad(("wi", l), sv["hb2"], dproj, cfg.T)
        name = f"ffn_bwd_{l}_0"
        (dxs, dz, dyb, ab, dm012, dg01), lands = _ffn_bwd(cfg, dxs1, sv["xs"], sv["z1"], sv["y1"], mods[l], gvec,
                                                          W["gu", l, 0], W["dn", l, 0], 0, 0, True, name, scatter_on(name),
                                                          xs_ctx=sv["xs_ctx"])
        scattered(bwd_carry.get(name, ()), lands)
        if not wc:
            zero = lambda a: jnp.concatenate([a, jnp.zeros_like(a)], axis=0)
            dm5, dm678 = zero(dm5), zero(dm678)
        dmods = jnp.concatenate([dm012, dm34, dm5, dm678], axis=1)
        dgs = jnp.concatenate([dg01, dg2, dg3, dg45], axis=0)
        small[l] = dict(dmods=dmods, dg=dgs, drpb=drpb, dwp=dwp, dps=dps)
        small_gather = None
        if l == 0:
            parts = [jnp.stack([small[j][n_] for j in range(depth)]) for n_ in SMALL_NAMES]
            packed = _pack_rows(parts)
            small_gather = _allgather_comm(packed)
        wgrad(("gu", l, 0), sv["hb1"], dz, cfg.T, small_gather)
        wgrad(("dn", l, 0), ab, dyb, cfg.T)
    last_comm = _scatter_comm([g_b16[k_] for k_ in last_scatter], [W_AXIS[k_[0]] for k_ in last_scatter])

    def finish_weight_grads(lands):
        scattered(last_scatter, lands)
        kinds = ("gu", "dn", "wi", "wo")
        return dict(zip(kinds, _half_merge([G[k_] for k_ in kinds], "merge_halves")))

    return loss_blk, dxs, (last_comm, finish_weight_grads), (packed, [p.shape for p in parts], small_landed[0])


def kernel(x, c, ctx, c_ctx, w_mod, b_mod, norm_g, w_ffn_gate_up, w_ffn_down, w_in, w_out, na_rpb, w_pool, pool_scale, loss_target, m_c_ctx, m_w_mod, m_b_mod, m_norm_g, m_w_ffn_gate_up, m_w_ffn_down, m_w_in, m_w_out, m_na_rpb, m_w_pool, m_pool_scale, v_c_ctx, v_w_mod, v_b_mod, v_norm_g, v_w_ffn_gate_up, v_w_ffn_down, v_w_in, v_w_out, v_na_rpb, v_w_pool, v_pool_scale):
    S, D = x.shape[1], x.shape[2]
    L = ctx.shape[1]
    depth = w_mod.shape[0]
    F = w_ffn_down.shape[2] * N_CHIPS
    nmod = w_mod.shape[2]
    gsh = norm_g.shape[2]
    cfg = _Cfg(S, L, D, F)
    mx, my, mc = _mesh_pos()
    chip = 2 * mx + my
    dev = 4 * mx + 2 * my + mc

    W = {}
    for l in range(depth):
        for i in range(2):
            W["gu", l, i] = _cast_into_place(w_ffn_gate_up, (l, i), W_AXIS["gu"], f"cast_gu_{l}_{i}")
            W["dn", l, i] = _cast_into_place(w_ffn_down, (l, i), W_AXIS["dn"], f"cast_dn_{l}_{i}")
        W["wi", l] = _cast_into_place(w_in, (l,), W_AXIS["wi"], f"cast_wi_{l}")
        W["wo", l] = _cast_into_place(w_out, (l,), W_AXIS["wo"], f"cast_wo_{l}")
    first = [("gu", 0, 0), ("dn", 0, 0)]
    W.update(zip(first, _comm_only(_gather_comm([W[k_] for k_ in first], [W_AXIS[k_[0]] for k_ in first]), "gather_first")))
    G = {"gu": lax.empty(w_ffn_gate_up.shape, f32), "dn": lax.empty(w_ffn_down.shape, f32),
         "wi": lax.empty(w_in.shape, f32), "wo": lax.empty(w_out.shape, f32)}

    cg_packed = _pack_rows([c, norm_g])
    cg_all = _all_gather_small(cg_packed, "gather_c_norm_g").reshape(N_DEV, -1)
    c_all = cg_all[:, :D]
    ng = cg_all[:, D:D + norm_g.size].reshape(N_DEV, depth, 6, gsh)
    norm_g_all = jnp.concatenate([ng[2 * j] for j in range(N_CHIPS)], axis=-1)
    cvecs = jnp.concatenate([c_all, c_ctx[None], jnp.zeros((7, D), f32)], axis=0)
    b_shard = lax.dynamic_slice_in_dim(b_mod, chip * nmod, nmod, axis=1).reshape(depth, 1, nmod)
    m_part, silu_c = _modvec_fwd(cvecs, w_mod, b_shard, "modvec_fwd")
    m_all = _all_gather_small(m_part.reshape(depth * 16, nmod), "gather_mod").reshape(N_DEV, depth, 16, nmod)
    m_full = jnp.concatenate([m_all[2 * j] for j in range(N_CHIPS)], axis=-1)
    m_mine = lax.dynamic_index_in_dim(m_full, dev, axis=1, keepdims=False)
    mods = jnp.stack([m_mine, m_full[:, 8]], axis=1).reshape(depth, 2, N_MOD, D)

    loss_blk, dx_lat, (last_comm, finish_weight_grads), small = _local_step(
        cfg, x[0], ctx[0], loss_target[0], mods, norm_g_all, W, G, na_rpb, w_pool, pool_scale)
    loss = lax.psum(loss_blk[0, 0], ("x", "y", "c"))
    grad_x = dx_lat[None]

    packed, shapes, landed = small
    gathered = lax.dynamic_update_index_in_dim(landed, packed, dev, 0)
    total = _unpack_rows(_sum_devices(gathered, "sum_small"), shapes)
    dmods_sum, dg_sum, drpb_sum, dwp_sum, dps_sum = total
    dmods_each = jnp.stack([_unpack_rows(gathered[j], shapes[:1])[0] for j in range(N_DEV)])
    dm_rows = jnp.concatenate([jnp.transpose(dmods_each[:, :, 0], (1, 0, 2, 3)).reshape(depth, N_DEV, N_MOD * D),
                               dmods_sum[:, 1].reshape(depth, 1, N_MOD * D),
                               jnp.zeros((depth, 7, N_MOD * D), f32)], axis=1)
    dm_shard = lax.dynamic_slice_in_dim(dm_rows, chip * nmod, nmod, axis=2)
    grad_w_mod, gc_part = _modvec_bwd(silu_c.T, dm_shard, w_mod, "modvec_bwd")
    gc_all = _all_gather_small(gc_part.reshape(depth * 8, D), "gather_gc").reshape(N_DEV, depth, 8, D)
    grad_b_mod, grad_c_ctx = _small_finish(dm_rows, gc_all, c_ctx)
    grad_norm_g = lax.dynamic_slice_in_dim(dg_sum, chip * gsh, gsh, axis=2)
    grad_na_rpb = drpb_sum[:, :, :2 * NA_KH - 1, :2 * NA_KW - 1]
    grad_w_pool = dwp_sum
    grad_pool_scale = dps_sum.reshape(depth, POOL_WIDTH)

    upd_w_mod, lands = _adamw(w_mod, grad_w_mod, m_w_mod, v_w_mod, "adamw_w_mod", comm=last_comm)
    wgrads = finish_weight_grads(lands)
    g_gu, g_dn, g_wi, g_wo = wgrads["gu"], wgrads["dn"], wgrads["wi"], wgrads["wo"]
    grads = [grad_c_ctx, grad_w_mod, grad_b_mod, grad_norm_g, g_gu, g_dn, g_wi, g_wo, grad_na_rpb, grad_w_pool, grad_pool_scale]
    ws = [c_ctx, w_mod, b_mod, norm_g, w_ffn_gate_up, w_ffn_down, w_in, w_out, na_rpb, w_pool, pool_scale]
    ms = [m_c_ctx, m_w_mod, m_b_mod, m_norm_g, m_w_ffn_gate_up, m_w_ffn_down, m_w_in, m_w_out, m_na_rpb, m_w_pool, m_pool_scale]
    vs = [v_c_ctx, v_w_mod, v_b_mod, v_norm_g, v_w_ffn_gate_up, v_w_ffn_down, v_w_in, v_w_out, v_na_rpb, v_w_pool, v_pool_scale]
    tags = ["c_ctx", "w_mod", "b_mod", "norm_g", "gate_up", "down", "w_in", "w_out", "na_rpb", "w_pool", "pool_scale"]
    merged = ("gate_up", "down", "w_in", "w_out")
    upd = [upd_w_mod if t == "w_mod" else _adamw(w_, g_, m_, v_, f"adamw_{t}", emit_grad=t in merged)
           for w_, g_, m_, v_, t in zip(ws, grads, ms, vs, tags)]
    grads = [u_[3] if t in merged else g_ for g_, u_, t in zip(grads, upd, tags)]
    return (loss, grad_x, *grads, *[u_[0] for u_ in upd], *[u_[1] for u_ in upd], *[u_[2] for u_ in upd])


def _small_finish(dm_rows, gc_all, c_ctx):
    depth, _, n = dm_rows.shape
    D = c_ctx.shape[0]

    def body(dm_ref, gc_ref, c_ref, gb_ref, gcx_ref):
        acc = dm_ref[:, 0]
        for j in range(1, N_DEV + 1):
            acc = acc + dm_ref[:, j]
        gb_ref[...] = acc
        t = jnp.zeros((1, D), f32)
        for l in range(depth):
            for j in range(N_CHIPS):
                t = t + gc_ref[2 * j, l, 0:1, :]
        cv = c_ref[...]
        sg = _sigmoid(cv)
        gcx_ref[...] = t * (sg * (1.0 + cv * (1.0 - sg)))

    gb, gcx = pl.pallas_call(
        body, name="small_finish",
        out_shape=[S_((depth, n), f32), S_((1, D), f32)],
        compiler_params=_cp(VMEM_MID),
    )(dm_rows, gc_all, c_ctx.reshape(1, D))
    return gb, gcx.reshape(D)
```

```python
import functools

import numpy as np
import jax
import jax.numpy as jnp
from jax import lax
from jax.experimental import pallas as pl
from jax.experimental.pallas import tpu as pltpu

f32, bf16 = jnp.float32, jnp.bfloat16

GRID_W = 64
N_MOD = 9
NA_HEADS = 8
HEAD_DIM = 64
NA_WIDTH = NA_HEADS * HEAD_DIM
NA_KH = 8
NA_KW = 16
POOL_GROUPS = 4
POOL_CH = 128
POOL_WIDTH = POOL_GROUPS * POOL_CH
POOL_WINDOWS = (2, 4, 8, 16)
IN_WIDTH = 3 * NA_WIDTH + POOL_WIDTH
MIX_WIDTH = NA_WIDTH + POOL_WIDTH
ROPE_THETA = 10000.0
ROPE_PAIRS = HEAD_DIM // 4
RMS_EPS = 1e-6
NEG_INF = -1e30
ADAM_LR, ADAM_B1, ADAM_B2, ADAM_EPS, ADAM_WD, ADAM_STEP = 0.001, 0.9, 0.999, 1e-08, 0.01, 10

N_DEV = 8
N_CHIPS = 4
LANES = 128
MIB = 1024 * 1024
VMEM_BIG = 52 * MIB
VMEM_MID = 40 * MIB
WGRAD_TN = 1408
WGRAD_TK = 2816
WGRAD_SLACK = 6 * MIB
ELEMWISE_BLOCK = 256 * 1024
MESH = pl.DeviceIdType.MESH
ANY = pl.BlockSpec(memory_space=pl.ANY)
S_ = jax.ShapeDtypeStruct


def _cp(vmem=VMEM_MID, sem=None):
    return pltpu.CompilerParams(vmem_limit_bytes=vmem, dimension_semantics=sem)


def _sigmoid(x):
    return 0.5 * jnp.tanh(0.5 * x) + 0.5


def _rms_hat(x):
    rinv = lax.rsqrt(jnp.mean(x * x, axis=-1, keepdims=True) + RMS_EPS)
    return x * rinv, rinv


def _rms_bwd(dxhat, xhat, rinv):
    return rinv * (dxhat - xhat * jnp.mean(dxhat * xhat, axis=-1, keepdims=True))


def _rsum(a):
    return jnp.sum(a, axis=0, keepdims=True)


def _nt(a, b):
    return lax.dot_general(a, b, (((1,), (1,)), ((), ())), preferred_element_type=f32)


def _tn(a, b):
    return lax.dot_general(a, b, (((0,), (0,)), ((), ())), preferred_element_type=f32)


def _nn(a, b):
    return jnp.dot(a, b, preferred_element_type=f32)


def _swap16(x):
    lane = lax.broadcasted_iota(jnp.int32, x.shape, 1)
    n = x.shape[1]
    return jnp.where((lane % 32) < 16, pltpu.roll(x, n - 16, 1), pltpu.roll(x, 16, 1))


def _rope_tables(s_len, l_len):
    t = np.arange(s_len)
    inv = ROPE_THETA ** (-np.arange(ROPE_PAIRS, dtype=np.float32) / ROPE_PAIRS)
    ang_r = (t // GRID_W).astype(np.float32)[:, None] * inv
    ang_c = (t % GRID_W).astype(np.float32)[:, None] * inv
    cos = np.concatenate([np.cos(ang_r), np.cos(ang_r), np.cos(ang_c), np.cos(ang_c)], axis=-1)
    sin = np.concatenate([-np.sin(ang_r), np.sin(ang_r), -np.sin(ang_c), np.sin(ang_c)], axis=-1)
    cos = np.concatenate([cos, np.ones((l_len, HEAD_DIM), np.float32)], axis=0)
    sin = np.concatenate([sin, np.zeros((l_len, HEAD_DIM), np.float32)], axis=0)
    return (jnp.asarray(np.tile(cos, (1, 2)), f32), jnp.asarray(np.tile(sin, (1, 2)), f32))


def _pool_tables(tm, l_len):
    band = np.zeros((2, POOL_GROUPS, tm, tm), np.float32)
    inv = np.zeros((2, POOL_GROUPS, tm, 1), np.float32)
    for typ, length in ((0, GRID_W), (1, l_len)):
        for g, w in enumerate(POOL_WINDOWS):
            for t in range(tm):
                base, p = (t // length) * length, t % length
                lo = min(max(p - w // 2, 0), length)
                hi = min(max(p - w // 2 + w, 0), length)
                band[typ, g, t, base + lo:base + hi] = 1.0
                inv[typ, g, t, 0] = 1.0 / (hi - lo)
    return jnp.asarray(band, bf16), jnp.asarray(inv, f32)


NA_QR = 4
NA_WR = NA_KH + NA_QR - 1
NA_TYPES = 3
NA_SEL_ROWS = 136
NA_WPAD = 768


def _rpb_index_tables():
    j = np.arange(GRID_W)
    col_start = np.clip(j - NA_KW // 2, 0, GRID_W - NA_KW)
    valid = (j[None, :] >= col_start[:, None]) & (j[None, :] < col_start[:, None] + NA_KW)
    dc = np.clip(j[None, :] - j[:, None] + NA_KW - 1, 0, 2 * NA_KW - 2)
    i = np.arange(NA_QR)[:, None]
    kk = np.arange(NA_WR)[None, :]
    off = np.stack([np.zeros_like(i), i, np.full_like(i, NA_QR - 1)])
    d = np.stack([kk - i + NA_KH - 1, kk - i + NA_KH - 1 - NA_QR, kk - i])
    row_ok = (kk[None] >= off) & (kk[None] < off + NA_KH)
    assert (d[row_ok] >= 0).all() and (d[row_ok] <= 2 * NA_KH - 2).all()
    return valid, dc, d, row_ok


def _expand_rpb(rpb, name):
    _, _, d, row_ok = _rpb_index_tables()
    heads, nd, ne = rpb.shape
    w = GRID_W
    v = jnp.pad(rpb, ((0, 0), (0, 0), (w - NA_KW, 2 * w - (w - NA_KW) - ne)))
    x = jnp.broadcast_to(v[:, :, None, :], (heads, nd, w, 2 * w)).reshape(heads, nd, 2 * w * w)
    t = x[:, :, :w * (2 * w - 1)].reshape(heads, nd, w, 2 * w - 1)[..., w - 1:]

    def body(t_ref, o_ref):
        q = lax.broadcasted_iota(jnp.int32, (w, w), 0)
        c = lax.broadcasted_iota(jnp.int32, (w, w), 1)
        c0 = jnp.clip(q - NA_KW // 2, 0, w - NA_KW)
        in_cols = (c >= c0) & (c < c0 + NA_KW)
        outside = jnp.full((w, w), NEG_INF, f32)
        blocks = [jnp.where(in_cols, t_ref[dd], NEG_INF) for dd in range(nd)]
        for typ in range(NA_TYPES):
            for i in range(NA_QR):
                row = [blocks[d[typ, i, kk]] if row_ok[typ, i, kk] else outside for kk in range(NA_WR)]
                o_ref[typ, i * w:(i + 1) * w, :] = jnp.concatenate(row, axis=1)

    return pl.pallas_call(
        body, name=name, grid=(heads,),
        in_specs=[pl.BlockSpec((None, nd, w, w), lambda h: (h, 0, 0, 0))],
        out_specs=pl.BlockSpec((NA_TYPES, None, NA_QR * w, NA_WR * w), lambda h: (0, h, 0, 0)),
        out_shape=S_((NA_TYPES, heads, NA_QR * w, NA_WR * w), f32),
        compiler_params=_cp(VMEM_MID, ("arbitrary",)),
    )(t)


def _rpb_reduce_tables():
    _, _, d, row_ok = _rpb_index_tables()
    flip = np.eye(GRID_W, dtype=np.float32)[::-1].copy()
    sel = np.zeros((16, NA_SEL_ROWS), np.float32)
    flat_d, flat_ok = d.reshape(-1), row_ok.reshape(-1)
    for n in range(flat_d.size):
        if flat_ok[n]:
            sel[flat_d[n], n] = 1.0
    return jnp.asarray(flip), jnp.asarray(sel)


class _Cfg:
    def __init__(self, s_len, l_len, d, f):
        self.S, self.L, self.D, self.F = s_len, l_len, d, f
        self.T = s_len + l_len
        self.TM = 256 if l_len % 256 == 0 else 128
        assert l_len == self.TM, "context length must equal the row tile"
        assert s_len % self.TM == 0 and s_len % GRID_W == 0
        self.nxt = s_len // self.TM
        self.ntt = self.T // self.TM
        self.rows = s_len // GRID_W
        assert self.rows >= 2 * NA_KH
        assert f % (2 * LANES) == 0
        self.FC = f

    def ntiles(self, with_ctx):
        return self.ntt if with_ctx else self.nxt


def _typ(cfg):
    return lambda i: (jnp.minimum(i // cfg.nxt, 1), 0, 0)


def _mesh_pos():
    return lax.axis_index("x"), lax.axis_index("y"), lax.axis_index("c")


class _Comm:
    def __init__(self, ins, outs, alias, nsem, start, finish):
        self.ins, self.outs, self.alias, self.nsem, self.start, self.finish = ins, outs, alias, nsem, start, finish


def _call(body, args, comm=None, *, grid, in_specs, out_specs, out_shape, scratch_shapes=(), **kw):
    if comm is None:
        return pl.pallas_call(body, grid=grid, in_specs=list(in_specs), out_specs=list(out_specs), out_shape=list(out_shape),
                              scratch_shapes=list(scratch_shapes), **kw)(*args), ()
    n_in, n_out, n_sc = len(in_specs), len(out_specs), len(scratch_shapes)
    ci, co = len(comm.ins), len(comm.outs)

    def carrier(*refs):
        bounds = np.cumsum([0, n_in, ci, n_out, co, n_sc])
        ins, cins, outs, couts, scr = (refs[a:b] for a, b in zip(bounds[:-1], bounds[1:]))
        send, recv = refs[bounds[-1]], refs[bounds[-1] + 1]
        first = functools.reduce(jnp.logical_and, [pl.program_id(a) == 0 for a in range(len(grid))])
        last = functools.reduce(jnp.logical_and, [pl.program_id(a) == g - 1 for a, g in enumerate(grid)])

        @pl.when(first)
        def _():
            comm.start(cins, couts, send, recv)

        body(*ins, *outs, *scr)

        @pl.when(last)
        def _():
            comm.finish(cins, couts, send, recv)

    res = pl.pallas_call(
        carrier, grid=grid, in_specs=list(in_specs) + [ANY] * ci, out_specs=list(out_specs) + [ANY] * co,
        out_shape=list(out_shape) + list(comm.outs),
        input_output_aliases={n_in + a: n_out + b for a, b in comm.alias.items()},
        scratch_shapes=list(scratch_shapes) + [pltpu.SemaphoreType.DMA((comm.nsem,)), pltpu.SemaphoreType.DMA((comm.nsem,))],
        **kw)(*args, *comm.ins)
    return res[:n_out], res[n_out:]


def _comm_only(comm, name):
    ci, co = len(comm.ins), len(comm.outs)

    def body(*refs):
        cins, couts = refs[:ci], refs[ci:ci + co]
        send, recv = refs[ci + co], refs[ci + co + 1]
        comm.start(cins, couts, send, recv)
        comm.finish(cins, couts, send, recv)

    return pl.pallas_call(
        body, name=name, in_specs=[ANY] * ci, out_specs=[ANY] * co, out_shape=list(comm.outs),
        input_output_aliases=dict(comm.alias),
        scratch_shapes=[pltpu.SemaphoreType.DMA((comm.nsem,)), pltpu.SemaphoreType.DMA((comm.nsem,))],
        compiler_params=_cp(VMEM_MID),
    )(*comm.ins)


def _half_view(ref, axis, kk, h):
    r, c = ref.shape
    if axis == 1:
        n = c // N_CHIPS
        return ref.at[pl.ds(h * (r // 2), r // 2), pl.ds(pl.multiple_of(kk * n, LANES), n)]
    n = r // N_CHIPS
    return ref.at[pl.ds(pl.multiple_of(kk * n + h * (n // 2), 8), n // 2), :]


def _other_chips(x, y):
    return [(1 - x, y), (x, 1 - y), (1 - x, 1 - y)]


def _gather_comm(arrs, axes):
    n = len(arrs)

    def copy(ref, view, sems, k, to):
        send, recv = sems
        return pltpu.make_async_remote_copy(src_ref=view, dst_ref=view, send_sem=send.at[k], recv_sem=recv.at[k],
                                            device_id=to, device_id_type=MESH)

    def start(cins, bufs, send, recv):
        x, y, c = _mesh_pos()
        for t in range(n):
            own = _half_view(bufs[t], axes[t], 2 * x + y, c)
            for j, chip in enumerate(_other_chips(x, y)):
                copy(bufs[t], own, (send, recv), 6 * t + j, (*chip, c)).start()

    def finish(cins, bufs, send, recv):
        x, y, c = _mesh_pos()
        sibling = (x, y, 1 - c)
        chips = _other_chips(x, y)
        for t in range(n):
            for j, chip in enumerate(chips):
                landed = _half_view(bufs[t], axes[t], 2 * chip[0] + chip[1], c)
                copy(bufs[t], landed, (send, recv), 6 * t + j, (*chip, c)).wait_recv()
                copy(bufs[t], landed, (send, recv), 6 * t + 3 + j, sibling).start()
        for t in range(n):
            own = _half_view(bufs[t], axes[t], 2 * x + y, c)
            for j, chip in enumerate(chips):
                kj = 2 * chip[0] + chip[1]
                copy(bufs[t], _half_view(bufs[t], axes[t], kj, 1 - c), (send, recv), 6 * t + 3 + j, sibling).wait_recv()
                copy(bufs[t], own, (send, recv), 6 * t + j, (*chip, c)).wait_send()
                copy(bufs[t], _half_view(bufs[t], axes[t], kj, c), (send, recv), 6 * t + 3 + j, sibling).wait_send()

    return _Comm(list(arrs), [S_(a.shape, a.dtype) for a in arrs], {t: t for t in range(n)}, 6 * n, start, finish)


def _scatter_comm(parts, axes):
    n = len(parts)
    peers = [(fx, fy, fc) for fx in (0, 1) for fy in (0, 1) for fc in (0, 1)][1:]

    def half_shape(a, axis):
        r, c = a.shape
        return (r // 2, c // N_CHIPS) if axis == 1 else (r // N_CHIPS // 2, c)

    def start(srcs, lands, send, recv):
        x, y, c = _mesh_pos()
        me = 4 * x + 2 * y + c
        for t in range(n):
            for r_, (fx, fy, fc) in enumerate(peers):
                dx, dy, dc = (1 - x if fx else x), (1 - y if fy else y), (1 - c if fc else c)
                pltpu.make_async_remote_copy(
                    src_ref=_half_view(srcs[t], axes[t], 2 * dx + dy, dc), dst_ref=lands[t].at[me],
                    send_sem=send.at[7 * t + r_], recv_sem=recv.at[7 * t + r_],
                    device_id=(dx, dy, dc), device_id_type=MESH).start()

    def finish(srcs, lands, send, recv):
        x, y, c = _mesh_pos()
        for t in range(n):
            mine = _half_view(srcs[t], axes[t], 2 * x + y, c)
            for r_, (fx, fy, fc) in enumerate(peers):
                sx, sy, sc = (1 - x if fx else x), (1 - y if fy else y), (1 - c if fc else c)
                cp = pltpu.make_async_remote_copy(
                    src_ref=mine, dst_ref=lands[t].at[4 * sx + 2 * sy + sc],
                    send_sem=send.at[7 * t + r_], recv_sem=recv.at[7 * t + r_],
                    device_id=(sx, sy, sc), device_id_type=MESH)
                cp.wait_recv()
                cp.wait_send()

    return _Comm(list(parts), [S_((N_DEV,) + half_shape(a, ax), a.dtype) for a, ax in zip(parts, axes)], {}, 7 * n, start, finish)


def _allgather_comm(block):
    peers = [(fx, fy, fc) for fx in (0, 1) for fy in (0, 1) for fc in (0, 1)][1:]

    def ends(x, y, c):
        for r_, (fx, fy, fc) in enumerate(peers):
            yield r_, ((1 - x if fx else x), (1 - y if fy else y), (1 - c if fc else c))

    def start(srcs, lands, send, recv):
        x, y, c = _mesh_pos()
        for r_, peer in ends(x, y, c):
            pltpu.make_async_remote_copy(src_ref=srcs[0], dst_ref=lands[0].at[4 * x + 2 * y + c], send_sem=send.at[r_],
                                         recv_sem=recv.at[r_], device_id=peer, device_id_type=MESH).start()

    def finish(srcs, lands, send, recv):
        x, y, c = _mesh_pos()
        for r_, (px, py, pc) in ends(x, y, c):
            cp = pltpu.make_async_remote_copy(src_ref=srcs[0], dst_ref=lands[0].at[4 * px + 2 * py + pc], send_sem=send.at[r_],
                                              recv_sem=recv.at[r_], device_id=(px, py, pc), device_id_type=MESH)
            cp.wait_recv()
            cp.wait_send()

    return _Comm([block], [S_((N_DEV,) + block.shape, block.dtype)], {}, len(peers), start, finish)


def _ffn_fwd(cfg, xs, mods, gvec, wgu, wd, mi, gi, with_ctx, name, comm=None, xs_ctx=None, loss_target=None):
    TM, D, F, FC = cfg.TM, cfg.D, cfg.F, cfg.FC
    nt = cfg.ntiles(with_ctx)
    R = nt * TM
    split, head = xs_ctx is not None, loss_target is not None

    def body(*refs):
        it = iter(refs)
        xs_ref = next(it)
        xc_ref = next(it) if split else None
        mods_ref, g_ref, wgu_hbm, wd_hbm = next(it), next(it), next(it), next(it)
        t_ref = next(it) if head else None
        out_ref, hb_ref, z_ref, y_ref = next(it), next(it), next(it), next(it)
        loss_ref = next(it) if head else None
        wgu_v, wd_v, sem = next(it), next(it), next(it)
        i = pl.program_id(0)

        @pl.when(i == 0)
        def _():
            c0 = pltpu.make_async_copy(wgu_hbm, wgu_v, sem.at[0])
            c1 = pltpu.make_async_copy(wd_hbm, wd_v, sem.at[1])
            c0.start(); c1.start(); c0.wait(); c1.wait()
            if head:
                loss_ref[...] = jnp.zeros_like(loss_ref)
        x = xs_ref[...]
        if split:
            x = jnp.where(i < cfg.nxt, x, xc_ref[...])
        m = mods_ref[0]
        sh, sc, gt = m[mi:mi + 1], m[mi + 1:mi + 2], m[mi + 2:mi + 3]
        xhat, _ = _rms_hat(x)
        h = (xhat * g_ref[gi:gi + 1]) * (1.0 + sc) + sh
        hb = h.astype(bf16)
        hb_ref[...] = hb
        y = jnp.zeros((TM, D), f32)
        for ch in range(F // FC):
            zg = _nn(hb, wgu_v[:, ch * FC:(ch + 1) * FC])
            zu = _nn(hb, wgu_v[:, F + ch * FC:F + (ch + 1) * FC])
            z_ref[:, ch * FC:(ch + 1) * FC] = zg.astype(bf16)
            z_ref[:, F + ch * FC:F + (ch + 1) * FC] = zu.astype(bf16)
            a = (zg * _sigmoid(zg)) * zu
            y = y + _nn(a.astype(bf16), wd_v[ch * FC:(ch + 1) * FC, :])
        y_ref[...] = y
        yhat, _ = _rms_hat(y)
        out = x + 0.5 * gt * (yhat * g_ref[gi + 1:gi + 2])
        if head:
            e = out - t_ref[...]
            out_ref[...] = e * (1.0 / D)
            loss_ref[...] += jnp.sum(jnp.mean(e * e, axis=-1, keepdims=True), axis=0, keepdims=True) * 0.5
        else:
            out_ref[...] = out

    rt = lambda c: pl.BlockSpec((TM, c), lambda i: (i, 0))
    lat = pl.BlockSpec((TM, D), lambda i: (jnp.minimum(i, cfg.nxt - 1), 0))
    x_specs, x_args = ([lat, pl.BlockSpec((TM, D), lambda i: (0, 0))], [xs, xs_ctx]) if split else ([rt(D)], [xs])
    t_specs, t_args = ([rt(D)], [loss_target]) if head else ([], [])
    l_specs, l_shape = ([pl.BlockSpec((8, LANES), lambda i: (0, 0))], [S_((8, LANES), f32)]) if head else ([], [])
    return _call(
        body, (*x_args, mods, gvec, wgu, wd, *t_args), comm, name=name, grid=(nt,),
        in_specs=x_specs + [pl.BlockSpec((1, N_MOD, D), _typ(cfg)), pl.BlockSpec((6, D), lambda i: (0, 0)), ANY, ANY] + t_specs,
        out_specs=[rt(D), rt(D), rt(2 * F), rt(D)] + l_specs,
        out_shape=[S_((R, D), f32), S_((R, D), bf16), S_((R, 2 * F), bf16), S_((R, D), f32)] + l_shape,
        scratch_shapes=[pltpu.VMEM((D, 2 * F), bf16), pltpu.VMEM((F, D), bf16), pltpu.SemaphoreType.DMA((2,))],
        compiler_params=_cp(VMEM_BIG, ("arbitrary",)),
    )


def _ffn_bwd(cfg, dout, xs, z, y, mods, gvec, wgu, wd, mi, gi, with_ctx, name, comm=None, xs_ctx=None):
    TM, D, F, FC = cfg.TM, cfg.D, cfg.F, cfg.FC
    nt = cfg.ntiles(with_ctx)
    R = nt * TM
    ntyp = 2 if with_ctx else 1
    split = xs_ctx is not None

    def body(*refs):
        it = iter(refs)
        do_ref, xs_ref = next(it), next(it)
        xc_ref = next(it) if split else None
        z_ref, y_ref, mods_ref, g_ref, wgu_hbm, wd_hbm = (next(it) for _ in range(6))
        dx_ref, dz_ref, dy_ref, a_ref, dm_ref, dg_ref, wgu_v, wd_v, sem = (next(it) for _ in range(9))
        i = pl.program_id(0)

        @pl.when(i == 0)
        def _():
            c0 = pltpu.make_async_copy(wgu_hbm, wgu_v, sem.at[0])
            c1 = pltpu.make_async_copy(wd_hbm, wd_v, sem.at[1])
            c0.start(); c1.start(); c0.wait(); c1.wait()
            dg_ref[...] = jnp.zeros_like(dg_ref)

        @pl.when((i == 0) | (i == cfg.nxt))
        def _():
            dm_ref[...] = jnp.zeros_like(dm_ref)

        do = do_ref[...]
        x = xs_ref[...]
        if split:
            x = jnp.where(i < cfg.nxt, x, xc_ref[...])
        m = mods_ref[0]
        sc, gt = m[mi + 1:mi + 2], m[mi + 2:mi + 3]
        g_pre, g_post = g_ref[gi:gi + 1], g_ref[gi + 1:gi + 2]
        xhat, rinv0 = _rms_hat(x)
        n0 = xhat * g_pre
        yhat, rinv1 = _rms_hat(y_ref[...])
        d_gt = _rsum(0.5 * do * (yhat * g_post))
        dr = (0.5 * gt) * do
        dg_post = _rsum(dr * yhat)
        dy = _rms_bwd(dr * g_post, yhat, rinv1)
        dyb = dy.astype(bf16)
        dy_ref[...] = dyb
        dh = jnp.zeros((TM, D), f32)
        for ch in range(F // FC):
            zg = z_ref[:, ch * FC:(ch + 1) * FC].astype(f32)
            zu = z_ref[:, F + ch * FC:F + (ch + 1) * FC].astype(f32)
            sg = _sigmoid(zg)
            silu = zg * sg
            a_ref[:, ch * FC:(ch + 1) * FC] = (silu * zu).astype(bf16)
            da = _nt(dyb, wd_v[ch * FC:(ch + 1) * FC, :])
            dzu = (da * silu).astype(bf16)
            dzg = (da * zu * (sg * (1.0 + zg * (1.0 - sg)))).astype(bf16)
            dz_ref[:, ch * FC:(ch + 1) * FC] = dzg
            dz_ref[:, F + ch * FC:F + (ch + 1) * FC] = dzu
            dh = dh + _nt(dzg, wgu_v[:, ch * FC:(ch + 1) * FC]) + _nt(dzu, wgu_v[:, F + ch * FC:F + (ch + 1) * FC])
        d_sh = _rsum(dh)
        d_sc = _rsum(dh * n0)
        dn = dh * (1.0 + sc)
        dg_pre = _rsum(dn * xhat)
        dx = do + _rms_bwd(dn * g_pre, xhat, rinv0)
        if split:
            @pl.when(i < cfg.nxt)
            def _():
                dx_ref[...] = dx
        else:
            dx_ref[...] = dx
        dm_ref[0] += jnp.concatenate([d_sh, d_sc, d_gt], axis=0)
        dg_ref[...] += jnp.concatenate([dg_pre, dg_post], axis=0)

    rt = lambda c: pl.BlockSpec((TM, c), lambda i: (i, 0))
    lat = pl.BlockSpec((TM, D), lambda i: (jnp.minimum(i, cfg.nxt - 1), 0))
    x_specs, x_args = ([lat, pl.BlockSpec((TM, D), lambda i: (0, 0))], [xs, xs_ctx]) if split else ([rt(D)], [xs])
    return _call(
        body, (dout, *x_args, z, y, mods, gvec, wgu, wd), comm, name=name, grid=(nt,),
        in_specs=[rt(D)] + x_specs + [rt(2 * F), rt(D), pl.BlockSpec((1, N_MOD, D), _typ(cfg)),
                                       pl.BlockSpec((6, D), lambda i: (0, 0)), ANY, ANY],
        out_specs=[lat if split else rt(D), rt(2 * F), rt(D), rt(F), pl.BlockSpec((1, 3, D), _typ(cfg)),
                   pl.BlockSpec((2, D), lambda i: (0, 0))],
        out_shape=[S_((cfg.S if split else R, D), f32), S_((R, 2 * F), bf16), S_((R, D), bf16), S_((R, F), bf16),
                   S_((ntyp, 3, D), f32), S_((2, D), f32)],
        scratch_shapes=[pltpu.VMEM((D, 2 * F), bf16), pltpu.VMEM((F, D), bf16), pltpu.SemaphoreType.DMA((2,))],
        compiler_params=_cp(VMEM_BIG, ("arbitrary",)),
    )


def _wgrad(a, b, k_rows, name, comm=None):
    M, N = a.shape[1], b.shape[1]
    tn = _div_tile(N, 1, WGRAD_TN, LANES) if N > WGRAD_TN // 2 else N
    tn = N // 2 if tn == N and N % (2 * LANES) == 0 else tn
    room = VMEM_BIG - WGRAD_SLACK - 2 * M * tn * 6
    tk = _div_tile(k_rows, 1, min(WGRAD_TK, room // (4 * (M + tn))), LANES)
    nk = k_rows // tk

    def body(a_ref, b_ref, o_ref, ob_ref):
        k = pl.program_id(1)

        @pl.when(k == 0)
        def _():
            o_ref[...] = jnp.zeros_like(o_ref)
        o_ref[...] += _tn(a_ref[...], b_ref[...])

        @pl.when(k == nk - 1)
        def _():
            ob_ref[...] = o_ref[...].astype(bf16)

    ospec = pl.BlockSpec((M, tn), lambda n, k: (0, n))
    return _call(
        body, (a, b), comm, name=name, grid=(N // tn, nk),
        in_specs=[pl.BlockSpec((tk, M), lambda n, k: (k, 0)), pl.BlockSpec((tk, tn), lambda n, k: (k, n))],
        out_specs=[ospec, ospec], out_shape=[S_((M, N), f32), S_((M, N), bf16)],
        compiler_params=_cp(VMEM_BIG, ("arbitrary", "arbitrary")),
    )


def _tmpre_fwd(cfg, xs, mods, gvec, w_in, cos, sin, name):
    TM, D = cfg.TM, cfg.D
    nt, R = cfg.ntt, cfg.T
    W = NA_WIDTH

    def body(xs_ref, mods_ref, g_ref, w_ref, cos_ref, sin_ref, hb_ref, q_ref, k_ref, v_ref, u_ref):
        x = xs_ref[...]
        m = mods_ref[0]
        xhat, _ = _rms_hat(x)
        hb = ((xhat * g_ref[2:3]) * (1.0 + m[4:5]) + m[3:4]).astype(bf16)
        hb_ref[...] = hb
        p = _nn(hb, w_ref[...])
        cs = jnp.tile(cos_ref[...], (1, W // LANES))
        sn = jnp.tile(sin_ref[...], (1, W // LANES))
        q = p[:, 0:W]
        k = p[:, W:2 * W]
        q_ref[...] = ((q * cs + _swap16(q) * sn) * (HEAD_DIM ** -0.5)).astype(bf16)
        k_ref[...] = (k * cs + _swap16(k) * sn).astype(bf16)
        v_ref[...] = p[:, 2 * W:3 * W].astype(bf16)
        u_ref[...] = p[:, 3 * W:]

    rt = lambda c: pl.BlockSpec((TM, c), lambda i: (i, 0))
    return pl.pallas_call(
        body, name=name, grid=(nt,),
        in_specs=[rt(D), pl.BlockSpec((1, N_MOD, D), _typ(cfg)), pl.BlockSpec((6, D), lambda i: (0, 0)),
                  pl.BlockSpec((D, IN_WIDTH), lambda i: (0, 0)), rt(LANES), rt(LANES)],
        out_specs=[rt(D), rt(W), rt(W), rt(W), rt(POOL_WIDTH)],
        out_shape=[S_((R, D), bf16), S_((R, W), bf16), S_((R, W), bf16), S_((R, W), bf16), S_((R, POOL_WIDTH), f32)],
        compiler_params=_cp(VMEM_MID, ("arbitrary",)),
    )(xs, mods, gvec, w_in, cos, sin)


def _tmpre_bwd(cfg, lat, ctx_terms, du_has_ctx, cos, sin, w_in, xs, mods, gvec, dres, res_with_ctx, name):
    TM, D = cfg.TM, cfg.D
    nt, R = cfg.ntt, cfg.T
    nres = cfg.ntiles(res_with_ctx)
    W = NA_WIDTH
    n_ctx = [len(t) for t in ctx_terms]
    flat_ctx = [a for t in ctx_terms for a in t]
    n_asm = 4 + len(flat_ctx) + 2

    def assemble(refs, o_ref):
        dq_ref, dk_ref, dv_ref, du_ref = refs[:4]
        ctx_refs = refs[4:4 + len(flat_ctx)]
        cos_ref, sin_ref = refs[4 + len(flat_ctx):]
        is_ctx = pl.program_id(0) >= cfg.nxt
        vals, off = [], 0
        for lat_ref, n in zip((dq_ref, dk_ref, dv_ref), n_ctx):
            cv = jnp.zeros((TM, W), f32)
            for r_ in ctx_refs[off:off + n]:
                cv = cv + r_[...]
            off += n
            vals.append(jnp.where(is_ctx, cv, lat_ref[...]))
        du_ = du_ref[...] if du_has_ctx else jnp.where(is_ctx, 0.0, du_ref[...])
        cs = jnp.tile(cos_ref[...], (1, W // LANES))
        sn = jnp.tile(sin_ref[...], (1, W // LANES))
        dq_ = vals[0] * (HEAD_DIM ** -0.5)
        dk_ = vals[1]
        o_ref[:, 0:W] = (dq_ * cs + _swap16(dq_ * sn)).astype(bf16)
        o_ref[:, W:2 * W] = (dk_ * cs + _swap16(dk_ * sn)).astype(bf16)
        o_ref[:, 2 * W:3 * W] = vals[2].astype(bf16)
        o_ref[:, 3 * W:] = du_.astype(bf16)

    def body(*refs):
        w_ref, xs_ref, mods_ref, g_ref, dres_ref, dx_ref, dp_ref, dm_ref, dg_ref = refs[n_asm:]
        i = pl.program_id(0)

        @pl.when(i == 0)
        def _():
            dg_ref[...] = jnp.zeros_like(dg_ref)

        @pl.when((i == 0) | (i == cfg.nxt))
        def _():
            dm_ref[...] = jnp.zeros_like(dm_ref)

        assemble(refs[:n_asm], dp_ref)
        dh = _nt(dp_ref[...], w_ref[...])
        x = xs_ref[...]
        m = mods_ref[0]
        g2 = g_ref[2:3]
        xhat, rinv = _rms_hat(x)
        d_sh = _rsum(dh)
        d_sc = _rsum(dh * (xhat * g2))
        dn = dh * (1.0 + m[4:5])
        dg_ref[...] += _rsum(dn * xhat)
        dx = _rms_bwd(dn * g2, xhat, rinv)
        res = dres_ref[...]
        if nres < nt:
            res = jnp.where(i < nres, res, 0.0)
        dx_ref[...] = res + dx
        dm_ref[0] += jnp.concatenate([d_sh, d_sc], axis=0)

    rt = lambda c: pl.BlockSpec((TM, c), lambda i: (i, 0))
    lat_spec = pl.BlockSpec((TM, W), lambda i: (jnp.minimum(i, cfg.nxt - 1), 0))
    du_spec = rt(POOL_WIDTH) if du_has_ctx else lat_spec
    asm_specs = ([lat_spec, lat_spec, lat_spec, du_spec] + [pl.BlockSpec((TM, W), lambda i: (0, 0))] * len(flat_ctx)
                 + [rt(LANES), rt(LANES)])
    return pl.pallas_call(
        body, name=name, grid=(nt,),
        in_specs=asm_specs + [pl.BlockSpec((D, IN_WIDTH), lambda i: (0, 0)), rt(D),
                              pl.BlockSpec((1, N_MOD, D), _typ(cfg)), pl.BlockSpec((6, D), lambda i: (0, 0)),
                              pl.BlockSpec((TM, D), lambda i: (jnp.minimum(i, nres - 1), 0))],
        out_specs=[rt(D), rt(IN_WIDTH), pl.BlockSpec((1, 2, D), _typ(cfg)), pl.BlockSpec((1, D), lambda i: (0, 0))],
        out_shape=[S_((R, D), f32), S_((R, IN_WIDTH), bf16), S_((2, 2, D), f32), S_((1, D), f32)],
        compiler_params=_cp(VMEM_MID, ("arbitrary",)),
    )(*lat, *flat_ctx, cos, sin, w_in, xs, mods, gvec, dres)


def _na_block(cfg, b):
    return jnp.clip(NA_QR * b - NA_KH // 2, 0, cfg.rows - NA_WR)


def _na_load_bias(b, nb, b_hbm, b_v, sem):
    for typ, at in ((0, 0), (1, 1), (2, nb - 1)):
        @pl.when(b == at)
        def _(typ=typ):
            cp = pltpu.make_async_copy(b_hbm.at[typ], b_v, sem)
            cp.start()
            cp.wait()


def _na_exps(qh, klh, kch, bias):
    s_loc = _nt(qh, klh) + bias
    s_ctx = _nt(qh, kch)
    mx = jnp.maximum(jnp.max(s_loc, axis=-1, keepdims=True), jnp.max(s_ctx, axis=-1, keepdims=True))
    e_loc = jnp.exp(s_loc - mx)
    e_ctx = jnp.exp(s_ctx - mx)
    inv = 1.0 / (jnp.sum(e_loc, axis=-1, keepdims=True) + jnp.sum(e_ctx, axis=-1, keepdims=True))
    return e_loc, e_ctx, inv, mx - jnp.log(inv)


def _na_probs(qh, klh, kch, bias, lse):
    return jnp.exp(_nt(qh, klh) + bias - lse), jnp.exp(_nt(qh, kch) - lse)


def _na_fwd(cfg, q, k, v, bexp, name, comm=None):
    S, L, T = cfg.S, cfg.L, cfg.T
    NQ, NW = NA_QR * GRID_W, NA_WR * GRID_W
    nb = cfg.rows // NA_QR

    def body(q_ref, k_hbm, v_hbm, b_hbm, o_ref, lse_ref, k_v, v_v, b_v, sem):
        b = pl.program_id(0)

        @pl.when(b == 0)
        def _():
            cs = [pltpu.make_async_copy(k_hbm, k_v, sem.at[0]), pltpu.make_async_copy(v_hbm, v_v, sem.at[1])]
            for c_ in cs:
                c_.start()
            for c_ in cs:
                c_.wait()

        _na_load_bias(b, nb, b_hbm, b_v, sem.at[2])
        st = pl.multiple_of(_na_block(cfg, b) * GRID_W, GRID_W)
        first = lax.broadcasted_iota(jnp.int32, (NQ, LANES), 1) < HEAD_DIM
        for hp in range(NA_HEADS // 2):
            ls = slice(hp * LANES, (hp + 1) * LANES)
            q2 = q_ref[:, ls]
            kl, vl = k_v[pl.ds(st, NW), ls], v_v[pl.ds(st, NW), ls]
            kc, vc = k_v[S:T, ls], v_v[S:T, ls]
            o2, lse2 = [], []
            for hh in range(2):
                qm = jnp.where(first if hh == 0 else ~first, q2, jnp.zeros_like(q2))
                e_loc, e_ctx, inv, lse = _na_exps(qm, kl, kc, b_v[2 * hp + hh])
                o2.append((_nn(e_loc.astype(bf16), vl) + _nn(e_ctx.astype(bf16), vc)) * inv)
                lse2.append(lse)
            o_ref[:, ls] = jnp.where(first, o2[0], o2[1]).astype(bf16)
            lse_ref[:, ls] = jnp.where(first, lse2[0], lse2[1])

    return _call(
        body, (q, k, v, bexp), comm, name=name, grid=(nb,),
        in_specs=[pl.BlockSpec((NQ, NA_WIDTH), lambda b: (b, 0)), ANY, ANY, ANY],
        out_specs=[pl.BlockSpec((NQ, NA_WIDTH), lambda b: (b, 0)), pl.BlockSpec((NQ, NA_WIDTH), lambda b: (b, 0))],
        out_shape=[S_((S, NA_WIDTH), bf16), S_((S, NA_WIDTH), f32)],
        scratch_shapes=[pltpu.VMEM((T, NA_WIDTH), bf16), pltpu.VMEM((T, NA_WIDTH), bf16),
                        pltpu.VMEM((NA_HEADS, NQ, NW), f32), pltpu.SemaphoreType.DMA((3,))],
        compiler_params=_cp(VMEM_MID, ("arbitrary",)),
    )


def _na_bwd(cfg, do, o, lse, q, k, v, bexp, name, comm=None):
    S, L, T, rows = cfg.S, cfg.L, cfg.T, cfg.rows
    NQ, NW = NA_QR * GRID_W, NA_WR * GRID_W
    NSLOT = 2 * NA_KH
    nb = rows // NA_QR
    bmax = (rows - NA_WR) // NA_QR
    steps = 2 * nb - bmax
    W = NA_WIDTH
    assert nb >= 3 and bmax >= 1 and rows - NA_QR * bmax <= NSLOT

    def out_group(g):
        return jnp.where(g >= nb, g - nb + bmax, jnp.clip(g - 1, 0, bmax - 1))

    def body(do_ref, o_ref, lse_ref, q_ref, k_hbm, v_hbm, b_hbm, dq_ref, dk_ref, dv_ref, dkc_ref, dvc_ref, db_hbm,
             k_v, v_v, b_v, db_v, ak, av, akc, avc, sem):
        g = pl.program_id(0)

        @pl.when(g == 0)
        def _():
            cs = [pltpu.make_async_copy(k_hbm, k_v, sem.at[0]), pltpu.make_async_copy(v_hbm, v_v, sem.at[1])]
            for c_ in cs:
                c_.start()
            db_v[...] = jnp.zeros_like(db_v)
            ak[...] = jnp.zeros_like(ak)
            av[...] = jnp.zeros_like(av)
            akc[...] = jnp.zeros_like(akc)
            avc[...] = jnp.zeros_like(avc)
            for c_ in cs:
                c_.wait()

        for typ, at in ((0, 1), (1, nb - 1)):
            @pl.when(g == at)
            def _(typ=typ):
                cp = pltpu.make_async_copy(db_v, db_hbm.at[typ], sem.at[2])
                cp.start()
                cp.wait()
                db_v[...] = jnp.zeros_like(db_v)

        @pl.when(g < nb)
        def _():
            _na_load_bias(g, nb, b_hbm, b_v, sem.at[2])
            ws = _na_block(cfg, g)
            st = pl.multiple_of(ws * GRID_W, GRID_W)
            first = lax.broadcasted_iota(jnp.int32, (NQ, LANES), 1) < HEAD_DIM
            for hp in range(NA_HEADS // 2):
                ls = slice(hp * LANES, (hp + 1) * LANES)
                q2, do2 = q_ref[:, ls], do_ref[:, ls]
                o2 = o_ref[:, ls].astype(f32)
                lse2 = lse_ref[:, ls]
                kl, vl = k_v[pl.ds(st, NW), ls], v_v[pl.ds(st, NW), ls]
                kc, vc = k_v[S:T, ls], v_v[S:T, ls]
                dq2 = []
                dk2 = jnp.zeros((NW, LANES), f32)
                dv2 = jnp.zeros((NW, LANES), f32)
                dkc2 = jnp.zeros((L, LANES), f32)
                dvc2 = jnp.zeros((L, LANES), f32)
                for hh in range(2):
                    keep = first if hh == 0 else ~first
                    qm = jnp.where(keep, q2, jnp.zeros_like(q2))
                    dom = jnp.where(keep, do2, jnp.zeros_like(do2))
                    p_loc, p_ctx = _na_probs(qm, kl, kc, b_v[2 * hp + hh], lse2[:, hh * HEAD_DIM:hh * HEAD_DIM + 1])
                    dp_loc = _nt(dom, vl)
                    dp_ctx = _nt(dom, vc)
                    delta = jnp.sum(dom.astype(f32) * o2, axis=-1, keepdims=True)
                    ds_loc = p_loc * (dp_loc - delta)
                    ds_ctx = p_ctx * (dp_ctx - delta)
                    db_v[2 * hp + hh, :, 0:NW] += ds_loc
                    dsl, dsc = ds_loc.astype(bf16), ds_ctx.astype(bf16)
                    dq2.append(_nn(dsl, kl) + _nn(dsc, kc))
                    dk2 = dk2 + _tn(dsl, qm)
                    dv2 = dv2 + _tn(p_loc.astype(bf16), dom)
                    dkc2 = dkc2 + _tn(dsc, qm)
                    dvc2 = dvc2 + _tn(p_ctx.astype(bf16), dom)
                dq_ref[:, ls] = jnp.where(first, dq2[0], dq2[1])
                akc[:, ls] += dkc2
                avc[:, ls] += dvc2
                for kk in range(NA_WR):
                    slot = (ws + kk) % NSLOT
                    ak[slot, :, ls] += dk2[kk * GRID_W:(kk + 1) * GRID_W, :]
                    av[slot, :, ls] += dv2[kk * GRID_W:(kk + 1) * GRID_W, :]

        @pl.when(((g >= 1) & (g <= bmax)) | (g >= nb))
        def _():
            base = NA_QR * (out_group(g) % (NSLOT // NA_QR))
            for t in range(NA_QR):
                dk_ref[t * GRID_W:(t + 1) * GRID_W, :] = ak[base + t]
                dv_ref[t * GRID_W:(t + 1) * GRID_W, :] = av[base + t]
                ak[base + t] = jnp.zeros((GRID_W, W), f32)
                av[base + t] = jnp.zeros((GRID_W, W), f32)

        @pl.when(g == nb - 1)
        def _():
            cp = pltpu.make_async_copy(db_v, db_hbm.at[2], sem.at[2])
            cp.start()
            cp.wait()

        @pl.when(g == steps - 1)
        def _():
            dkc_ref[...] = akc[...]
            dvc_ref[...] = avc[...]

    qmap = lambda g: (jnp.minimum(g, nb - 1), 0)
    kmap = lambda g: (out_group(g), 0)
    full = lambda g: (0, 0)
    return _call(
        body, (do, o, lse, q, k, v, bexp), comm, name=name, grid=(steps,),
        in_specs=[pl.BlockSpec((NQ, W), qmap)] * 4 + [ANY, ANY, ANY],
        out_specs=[pl.BlockSpec((NQ, W), qmap), pl.BlockSpec((NQ, W), kmap), pl.BlockSpec((NQ, W), kmap),
                   pl.BlockSpec((L, W), full), pl.BlockSpec((L, W), full), ANY],
        out_shape=[S_((S, W), f32), S_((S, W), f32), S_((S, W), f32), S_((L, W), f32), S_((L, W), f32),
                   S_((NA_TYPES, NA_HEADS, NQ, NA_WPAD), f32)],
        scratch_shapes=[pltpu.VMEM((T, W), bf16), pltpu.VMEM((T, W), bf16),
                        pltpu.VMEM((NA_HEADS, NQ, NW), f32), pltpu.VMEM((NA_HEADS, NQ, NA_WPAD), f32),
                        pltpu.VMEM((NSLOT, GRID_W, W), f32), pltpu.VMEM((NSLOT, GRID_W, W), f32),
                        pltpu.VMEM((L, W), f32), pltpu.VMEM((L, W), f32), pltpu.SemaphoreType.DMA((3,))],
        compiler_params=_cp(VMEM_BIG, ("arbitrary",)),
    )


def _rpb_reduce(dbias, flip, sel, name):
    nq, w = NA_QR * GRID_W, GRID_W

    def diag_body(x_ref, j_ref, o_ref):
        rows = []
        for i in range(NA_QR):
            xr = jnp.dot(j_ref[...], x_ref[i * w:(i + 1) * w, :], preferred_element_type=f32, precision=lax.Precision.HIGHEST)
            rows.append(jnp.sum(pltpu.roll(xr, 0, 1, stride=1, stride_axis=0), axis=0, keepdims=True))
        o_ref[...] = jnp.concatenate(rows + [jnp.zeros((8 - NA_QR, NA_WPAD), f32)], axis=0)

    diag = pl.pallas_call(
        diag_body, name=name + "_diag", grid=(NA_TYPES, NA_HEADS),
        in_specs=[pl.BlockSpec((None, None, nq, NA_WPAD), lambda t, h: (t, h, 0, 0)), pl.BlockSpec((w, w), lambda t, h: (0, 0))],
        out_specs=pl.BlockSpec((None, None, 8, NA_WPAD), lambda t, h: (t, h, 0, 0)),
        out_shape=S_((NA_TYPES, NA_HEADS, 8, NA_WPAD), f32),
        compiler_params=_cp(VMEM_MID, ("arbitrary", "arbitrary")),
    )(dbias, flip)
    lo = w - NA_KW
    y = diag[:, :, :NA_QR, lo:lo + NA_WR * w].reshape(NA_TYPES, NA_HEADS, NA_QR, NA_WR, w)
    y = jnp.transpose(y, (1, 0, 2, 3, 4)).reshape(NA_HEADS, NA_TYPES * NA_QR * NA_WR, w)
    y = jnp.pad(y, ((0, 0), (0, NA_SEL_ROWS - y.shape[1]), (0, LANES - w)))

    def body(y_ref, sel_ref, o_ref):
        o_ref[...] = jnp.dot(sel_ref[...], y_ref[...], preferred_element_type=f32, precision=lax.Precision.HIGHEST)

    return pl.pallas_call(
        body, name=name, grid=(NA_HEADS,),
        in_specs=[pl.BlockSpec((None, NA_SEL_ROWS, LANES), lambda h: (h, 0, 0)), pl.BlockSpec((16, NA_SEL_ROWS), lambda h: (0, 0))],
        out_specs=pl.BlockSpec((None, 16, LANES), lambda h: (h, 0, 0)),
        out_shape=S_((NA_HEADS, 16, LANES), f32),
        compiler_params=_cp(VMEM_MID, ("arbitrary",)),
    )(y, sel)


def _ctx_attn_fwd(cfg, q, k, v, name):
    L = cfg.L
    blk = cfg.S // L

    def body(q_ref, k_ref, v_ref, o_ref):
        qv, kv, vv = q_ref[...], k_ref[...], v_ref[...]
        outs = []
        for h in range(NA_HEADS):
            hs = slice(h * HEAD_DIM, (h + 1) * HEAD_DIM)
            s = _nt(qv[:, hs], kv[:, hs])
            e = jnp.exp(s - jnp.max(s, axis=-1, keepdims=True))
            p = e * (1.0 / jnp.sum(e, axis=-1, keepdims=True))
            outs.append(_nn(p.astype(bf16), vv[:, hs]))
        o_ref[...] = jnp.concatenate(outs, axis=-1).astype(bf16)

    spec = pl.BlockSpec((L, NA_WIDTH), lambda i: (blk, 0))
    return pl.pallas_call(
        body, name=name, grid=(1,), in_specs=[spec, spec, spec],
        out_specs=pl.BlockSpec((L, NA_WIDTH), lambda i: (0, 0)), out_shape=S_((L, NA_WIDTH), bf16),
        compiler_params=_cp(VMEM_MID, ("arbitrary",)),
    )(q, k, v)


def _ctx_attn_bwd(cfg, do, q, k, v, name):
    L = cfg.L
    blk = cfg.S // L

    def body(do_ref, q_ref, k_ref, v_ref, dq_ref, dk_ref, dv_ref):
        dov, qv, kv, vv = do_ref[...], q_ref[...], k_ref[...], v_ref[...]
        dqs, dks, dvs = [], [], []
        for h in range(NA_HEADS):
            hs = slice(h * HEAD_DIM, (h + 1) * HEAD_DIM)
            qh, kh, doh = qv[:, hs], kv[:, hs], dov[:, hs]
            s = _nt(qh, kh)
            e = jnp.exp(s - jnp.max(s, axis=-1, keepdims=True))
            p = e * (1.0 / jnp.sum(e, axis=-1, keepdims=True))
            dp = _nt(doh, vv[:, hs])
            ds = (p * (dp - jnp.sum(p * dp, axis=-1, keepdims=True))).astype(bf16)
            dqs.append(_nn(ds, kh))
            dks.append(_tn(ds, qh))
            dvs.append(_tn(p.astype(bf16), doh))
        dq_ref[...] = jnp.concatenate(dqs, axis=-1)
        dk_ref[...] = jnp.concatenate(dks, axis=-1)
        dv_ref[...] = jnp.concatenate(dvs, axis=-1)

    spec = pl.BlockSpec((L, NA_WIDTH), lambda i: (blk, 0))
    ospec = pl.BlockSpec((L, NA_WIDTH), lambda i: (0, 0))
    return pl.pallas_call(
        body, name=name, grid=(1,), in_specs=[spec, spec, spec, spec],
        out_specs=[ospec, ospec, ospec], out_shape=[S_((L, NA_WIDTH), f32)] * 3,
        compiler_params=_cp(VMEM_MID, ("arbitrary",)),
    )(do, q, k, v)


def _pool_centered(u, band, inv):
    return _split_sum(_nn, band, u) * inv - u


def _split_sum(mm, band, t):
    hi = t.astype(bf16)
    lo = (t - hi.astype(f32)).astype(bf16)
    s = mm(band, jnp.concatenate([hi, lo], axis=1))
    n = t.shape[1]
    return s[:, :n] + s[:, n:]


def _pool_mix(u_ref, band_ref, inv_ref, w_ref, ps_ref):
    C = POOL_CH
    outs = []
    for g in range(POOL_GROUPS):
        d = _pool_centered(u_ref[:, g * C:(g + 1) * C], band_ref[0, g], inv_ref[0, g])
        outs.append(_nn(d.astype(bf16), w_ref[g].astype(bf16)) * ps_ref[:, g * C:(g + 1) * C])
    return jnp.concatenate(outs, axis=-1).astype(bf16)


def _pool_bwd(cfg, dmix, u, band, inv, w_pool, pool_scale, with_ctx, name):
    TM = cfg.TM
    nt = cfg.ntiles(with_ctx)
    C = POOL_CH

    def body(dy_ref, u_ref, band_ref, inv_ref, w_ref, ps_ref, du_ref, dw_ref, dps_ref):
        @pl.when(pl.program_id(0) == 0)
        def _():
            dw_ref[...] = jnp.zeros_like(dw_ref)
            dps_ref[...] = jnp.zeros_like(dps_ref)

        dus, dpss = [], []
        for g in range(POOL_GROUPS):
            gs = slice(g * C, (g + 1) * C)
            band_g, inv_g = band_ref[0, g], inv_ref[0, g]
            db = _pool_centered(u_ref[:, gs], band_g, inv_g).astype(bf16)
            wb = w_ref[g].astype(bf16)
            dy = dy_ref[:, gs].astype(f32)
            dpss.append(_rsum(dy * _nn(db, wb)))
            dys = (dy * ps_ref[:, gs]).astype(bf16)
            dw_ref[g] += _tn(db, dys)
            dd = _nt(dys, wb)
            dus.append(_split_sum(_tn, band_g, dd * inv_g) - dd)
        du_ref[...] = jnp.concatenate(dus, axis=-1)
        dps_ref[...] += jnp.concatenate(dpss, axis=-1)

    typ4 = lambda i: (jnp.minimum(i // cfg.nxt, 1), 0, 0, 0)
    return pl.pallas_call(
        body, name=name, grid=(nt,),
        in_specs=[pl.BlockSpec((TM, POOL_WIDTH), lambda i: (i, 1)), pl.BlockSpec((TM, POOL_WIDTH), lambda i: (i, 0)),
                  pl.BlockSpec((1, POOL_GROUPS, TM, TM), typ4), pl.BlockSpec((1, POOL_GROUPS, TM, 1), typ4),
                  pl.BlockSpec((POOL_GROUPS, C, C), lambda i: (0, 0, 0)), pl.BlockSpec((1, POOL_WIDTH), lambda i: (0, 0))],
        out_specs=[pl.BlockSpec((TM, POOL_WIDTH), lambda i: (i, 0)), pl.BlockSpec((POOL_GROUPS, C, C), lambda i: (0, 0, 0)),
                   pl.BlockSpec((1, POOL_WIDTH), lambda i: (0, 0))],
        out_shape=[S_((nt * TM, POOL_WIDTH), f32), S_((POOL_GROUPS, C, C), f32), S_((1, POOL_WIDTH), f32)],
        compiler_params=_cp(VMEM_MID, ("arbitrary",)),
    )(dmix, u, band, inv, w_pool, pool_scale)


def _tmpost_fwd(cfg, na_x, na_c, u, band, inv, w_pool, pool_scale, w_out, xs, mods, gvec, name):
    TM, D = cfg.TM, cfg.D
    with_ctx = na_c is not None
    nt = cfg.ntiles(with_ctx)
    R = nt * TM

    def body(*refs):
        if with_ctx:
            nax_ref, nac_ref = refs[:2]
            na = jnp.where(pl.program_id(0) < cfg.nxt, nax_ref[...], nac_ref[...])
        else:
            na = refs[0][...]
        (u_ref, band_ref, inv_ref, wp_ref, ps_ref, w_ref, xs_ref, mods_ref, g_ref,
         out_ref, opre_ref, mix_ref) = refs[2 if with_ctx else 1:]
        pool_v = _pool_mix(u_ref, band_ref, inv_ref, wp_ref, ps_ref)
        mix_ref[:, 0:NA_WIDTH] = na
        mix_ref[:, NA_WIDTH:] = pool_v
        o = _nn(na, w_ref[0:NA_WIDTH, :]) + _nn(pool_v, w_ref[NA_WIDTH:, :])
        opre_ref[...] = o
        ohat, _ = _rms_hat(o)
        out_ref[...] = xs_ref[...] + mods_ref[0][5:6] * (ohat * g_ref[3:4])

    rt = lambda c: pl.BlockSpec((TM, c), lambda i: (i, 0))
    na_specs = [pl.BlockSpec((TM, NA_WIDTH), lambda i: (jnp.minimum(i, cfg.nxt - 1), 0))]
    na_args = [na_x]
    if with_ctx:
        na_specs.append(pl.BlockSpec((TM, NA_WIDTH), lambda i: (0, 0)))
        na_args.append(na_c)
    typ4 = lambda i: (jnp.minimum(i // cfg.nxt, 1), 0, 0, 0)
    pool_specs = [rt(POOL_WIDTH), pl.BlockSpec((1, POOL_GROUPS, TM, TM), typ4), pl.BlockSpec((1, POOL_GROUPS, TM, 1), typ4),
                  pl.BlockSpec((POOL_GROUPS, POOL_CH, POOL_CH), lambda i: (0, 0, 0)), pl.BlockSpec((1, POOL_WIDTH), lambda i: (0, 0))]
    return pl.pallas_call(
        body, name=name, grid=(nt,),
        in_specs=na_specs + pool_specs + [pl.BlockSpec((MIX_WIDTH, D), lambda i: (0, 0)), rt(D),
                                          pl.BlockSpec((1, N_MOD, D), _typ(cfg)), pl.BlockSpec((6, D), lambda i: (0, 0))],
        out_specs=[rt(D), rt(D), rt(MIX_WIDTH)],
        out_shape=[S_((R, D), f32), S_((R, D), f32), S_((R, MIX_WIDTH), bf16)],
        compiler_params=_cp(VMEM_MID, ("arbitrary",)),
    )(*na_args, u, band, inv, w_pool, pool_scale, w_out, xs, mods, gvec)


def _tmpost_bwd(cfg, dout, opre, w_out, mods, gvec, with_ctx, name):
    TM, D = cfg.TM, cfg.D
    nt = cfg.ntiles(with_ctx)
    R = nt * TM
    ntyp = 2 if with_ctx else 1

    def body(do_ref, opre_ref, w_ref, mods_ref, g_ref, dop_ref, dmix_ref, dm_ref, dg_ref):
        i = pl.program_id(0)

        @pl.when(i == 0)
        def _():
            dg_ref[...] = jnp.zeros_like(dg_ref)

        @pl.when((i == 0) | (i == cfg.nxt))
        def _():
            dm_ref[...] = jnp.zeros_like(dm_ref)

        do = do_ref[...]
        g3 = g_ref[3:4]
        ohat, rinv = _rms_hat(opre_ref[...])
        dm_ref[0] += _rsum(do * (ohat * g3))
        dr = mods_ref[0][5:6] * do
        dg_ref[...] += _rsum(dr * ohat)
        dob = _rms_bwd(dr * g3, ohat, rinv).astype(bf16)
        dop_ref[...] = dob
        dmix_ref[...] = _nt(dob, w_ref[...]).astype(bf16)

    rt = lambda c: pl.BlockSpec((TM, c), lambda i: (i, 0))
    return pl.pallas_call(
        body, name=name, grid=(nt,),
        in_specs=[rt(D), rt(D), pl.BlockSpec((MIX_WIDTH, D), lambda i: (0, 0)),
                  pl.BlockSpec((1, N_MOD, D), _typ(cfg)), pl.BlockSpec((6, D), lambda i: (0, 0))],
        out_specs=[rt(D), rt(MIX_WIDTH), pl.BlockSpec((1, 1, D), _typ(cfg)), pl.BlockSpec((1, D), lambda i: (0, 0))],
        out_shape=[S_((R, D), bf16), S_((R, MIX_WIDTH), bf16), S_((ntyp, 1, D), f32), S_((1, D), f32)],
        compiler_params=_cp(VMEM_MID, ("arbitrary",)),
    )(dout, opre, w_out, mods, gvec)


def _modvec_fwd(cvecs, w_mod, b_shard, name):
    nl, D, n = w_mod.shape
    tn = n // 3 if (n % 3 == 0 and (n // 3) % LANES == 0) else n

    def body(c_ref, w_ref, b_ref, o_ref, s_ref):
        cv = c_ref[...]
        sv = cv * _sigmoid(cv)
        s_ref[...] = sv
        o_ref[...] = _nn(sv.astype(bf16), w_ref[...].astype(bf16)) + b_ref[...]

    return pl.pallas_call(
        body, name=name, grid=(nl, n // tn),
        in_specs=[pl.BlockSpec((16, D), lambda l, j: (0, 0)), pl.BlockSpec((None, D, tn), lambda l, j: (l, 0, j)),
                  pl.BlockSpec((None, 1, tn), lambda l, j: (l, 0, j))],
        out_specs=[pl.BlockSpec((None, 16, tn), lambda l, j: (l, 0, j)), pl.BlockSpec((16, D), lambda l, j: (0, 0))],
        out_shape=[S_((nl, 16, n), f32), S_((16, D), f32)],
        compiler_params=_cp(VMEM_MID, ("arbitrary", "arbitrary")),
    )(cvecs, w_mod, b_shard)


def _modvec_bwd(s_t, dm, w_mod, name):
    nl, D, n = w_mod.shape
    tn = n // 3 if (n % 3 == 0 and (n // 3) % LANES == 0) else n

    def body(s_ref, dm_ref, w_ref, gw_ref, gc_ref):
        @pl.when(pl.program_id(1) == 0)
        def _():
            gc_ref[...] = jnp.zeros_like(gc_ref)
        dmv = dm_ref[...]
        gw_ref[...] = jnp.dot(s_ref[...], dmv, preferred_element_type=f32, precision=lax.Precision.HIGHEST)
        gc_ref[...] += _nt(dmv[8:16].astype(bf16), w_ref[...].astype(bf16))

    return pl.pallas_call(
        body, name=name, grid=(nl, n // tn),
        in_specs=[pl.BlockSpec((D, 16), lambda l, j: (0, 0)), pl.BlockSpec((None, 16, tn), lambda l, j: (l, 0, j)),
                  pl.BlockSpec((None, D, tn), lambda l, j: (l, 0, j))],
        out_specs=[pl.BlockSpec((None, D, tn), lambda l, j: (l, 0, j)), pl.BlockSpec((None, 8, D), lambda l, j: (l, 0, 0))],
        out_shape=[S_((nl, D, n), f32), S_((nl, 8, D), f32)],
        compiler_params=_cp(VMEM_MID, ("arbitrary", "arbitrary")),
    )(s_t, dm, w_mod)


def _as2d(a):
    n = a.size
    if a.ndim >= 2 and a.shape[-1] % LANES == 0:
        return a.reshape(-1, a.shape[-1])
    if n % LANES == 0:
        return a.reshape(-1, LANES)
    return a.reshape(-1, a.shape[-1]) if a.ndim >= 2 else a.reshape(1, n)


def _row_tile(r, c, budget_elems):
    if r * c <= budget_elems or r % 8 != 0:
        return r
    t = r
    while t * c > budget_elems and t % 16 == 0:
        t //= 2
    return t


def _div_tile(r, c, budget_elems, mult=16):
    best = None
    for t in range(mult, r + 1, mult):
        if r % t == 0 and t * c <= budget_elems:
            best = t
    return best if best is not None else r


def _chip_index():
    return 2 * lax.axis_index("x") + lax.axis_index("y")


def _cast_into_place(shards, lead, axis, name):
    r, c = shards.shape[-2:]
    tr = _div_tile(r, c, 3 * ELEMWISE_BLOCK)
    nr = r // tr
    out_map = (lambda i: (i, _chip_index())) if axis == 1 else (lambda i: (_chip_index() * nr + i, 0))
    full2 = (r, c * N_CHIPS) if axis == 1 else (r * N_CHIPS, c)

    def body(a_ref, o_ref):
        o_ref[...] = a_ref[...].astype(bf16)

    return pl.pallas_call(
        body, name=name, grid=(nr,),
        in_specs=[pl.BlockSpec((None,) * len(lead) + (tr, c), lambda i: tuple(lead) + (i, 0))],
        out_specs=pl.BlockSpec((tr, c), out_map),
        out_shape=S_(full2, bf16), compiler_params=_cp(VMEM_MID, ("arbitrary",)),
    )(shards)


def _sum_devices8(own, land, axis, into, lead, name):
    _, rh, cs = land.shape
    tr = _div_tile(rh, cs, 2 * ELEMWISE_BLOCK)
    nr = rh // tr
    core = lambda: lax.axis_index("c")
    if axis == 1:
        own_map = lambda i: (core() * nr + i, _chip_index())
    else:
        own_map = lambda i: (_chip_index() * 2 * nr + core() * nr + i, 0)
    nl = len(lead)

    def land_spec(j):
        return pl.BlockSpec((None, tr, cs), lambda i: ((2 * _chip_index() + core() + j) % N_DEV, i, 0))

    def body(own_ref, *rest):
        acc = own_ref[...]
        for p_ref in rest[:N_DEV - 1]:
            acc = acc + p_ref[...].astype(f32)
        rest[-1][...] = acc

    return pl.pallas_call(
        body, name=name, grid=(nr,),
        in_specs=[pl.BlockSpec((tr, cs), own_map)] + [land_spec(j) for j in range(1, N_DEV)] + [ANY],
        out_specs=pl.BlockSpec((None,) * nl + (tr, cs), lambda i: tuple(lead) + (core() * nr + i, 0)),
        out_shape=S_(into.shape, f32), input_output_aliases={N_DEV: 0},
        compiler_params=_cp(VMEM_MID, ("arbitrary",)),
    )(own, *([land] * (N_DEV - 1)), into)


def _adamw(w, g, m, v, name, emit_grad=False, comm=None):
    shape = w.shape
    w2, g2, m2, v2 = _as2d(w), _as2d(g), _as2d(m), _as2d(v)
    r, c = w2.shape
    tr = _row_tile(r, c, ELEMWISE_BLOCK)
    c1 = 1.0 - ADAM_B1 ** ADAM_STEP
    c2 = 1.0 - ADAM_B2 ** ADAM_STEP
    n_out = 4 if emit_grad else 3

    def body(w_ref, g_ref, m_ref, v_ref, d_ref, mo_ref, vo_ref, *go_ref):
        gv = g_ref[...]
        mn = ADAM_B1 * m_ref[...] + (1.0 - ADAM_B1) * gv
        vn = ADAM_B2 * v_ref[...] + (1.0 - ADAM_B2) * (gv * gv)
        mo_ref[...] = mn
        vo_ref[...] = vn
        d_ref[...] = -ADAM_LR * ((mn / c1) / (jnp.sqrt(vn / c2) + ADAM_EPS) + ADAM_WD * w_ref[...])
        if emit_grad:
            go_ref[0][...] = gv

    spec = pl.BlockSpec((tr, c), lambda i: (i, 0))
    outs, res = _call(body, (w2, g2, m2, v2), comm, name=name, grid=(r // tr,), in_specs=[spec] * 4, out_specs=[spec] * n_out,
                      out_shape=[S_((r, c), f32)] * n_out, compiler_params=_cp(VMEM_MID, ("arbitrary",)))
    outs = tuple(o.reshape(shape) for o in outs)
    return outs if comm is None else (outs, res)


def _sum_devices(gathered, name):
    _, r, c = gathered.shape

    def body(a_ref, o_ref):
        acc = a_ref[0]
        for j in range(1, N_DEV):
            acc = acc + a_ref[j]
        o_ref[...] = acc

    tr = _row_tile(r, c, ELEMWISE_BLOCK // 4)
    return pl.pallas_call(
        body, name=name, grid=(r // tr,),
        in_specs=[pl.BlockSpec((N_DEV, tr, c), lambda i: (0, i, 0))], out_specs=pl.BlockSpec((tr, c), lambda i: (i, 0)),
        out_shape=S_((r, c), f32), compiler_params=_cp(VMEM_MID, ("arbitrary",)))(gathered)


def _all_gather_small(block, name):
    m_per, n = block.shape

    def body(x_ref, out_ref, send_sems, recv_sems, local_sem):
        x, y, c = _mesh_pos()
        me, sibling = (x, y, c), (x, y, 1 - c)
        chips = [(1 - x, y), (x, 1 - y), (1 - x, 1 - y)]

        def rows(px, py, pc):
            return out_ref.at[pl.ds((4 * px + 2 * py + pc) * m_per, m_per), :]

        def copy(k, blk, to, src=None):
            return pltpu.make_async_remote_copy(
                src_ref=rows(*blk) if src is None else src, dst_ref=rows(*blk),
                send_sem=send_sems.at[k], recv_sem=recv_sems.at[k], device_id=to, device_id_type=MESH)

        mine = pltpu.make_async_copy(x_ref, rows(*me), local_sem)
        mine.start()
        first = [copy(0, me, sibling, src=x_ref)]
        first += [copy(1 + j, me, (*chip, c), src=x_ref) for j, chip in enumerate(chips)]
        for cp in first:
            cp.start()
        passed = [copy(4 + j, (*chip, c), sibling) for j, chip in enumerate(chips)]
        for j, chip in enumerate(chips):
            copy(1 + j, (*chip, c), me).wait_recv()
            passed[j].start()
        copy(0, sibling, me).wait_recv()
        for j, chip in enumerate(chips):
            copy(4 + j, (*chip, 1 - c), me).wait_recv()
        for cp in first + passed:
            cp.wait_send()
        mine.wait()

    return pl.pallas_call(
        body, name=name, out_shape=S_((N_DEV * m_per, n), block.dtype),
        in_specs=[pl.BlockSpec(memory_space=pltpu.VMEM)], out_specs=pl.BlockSpec(memory_space=pltpu.VMEM),
        scratch_shapes=[pltpu.SemaphoreType.DMA((7,)), pltpu.SemaphoreType.DMA((7,)), pltpu.SemaphoreType.DMA],
        compiler_params=_cp(VMEM_MID),
    )(block)


def _pack_rows(arrays):
    flat = jnp.concatenate([a.reshape(-1) for a in arrays])
    pad = (-flat.size) % (8 * LANES)
    return jnp.pad(flat, (0, pad)).reshape(-1, LANES)


def _unpack_rows(packed, shapes):
    flat = packed.reshape(-1)
    out, off = [], 0
    for s in shapes:
        n = int(np.prod(s))
        out.append(flat[off:off + n].reshape(s))
        off += n
    return out


W_AXIS = {"gu": 1, "dn": 0, "wi": 1, "wo": 0}
SMALL_NAMES = ("dmods", "dg", "drpb", "dwp", "dps")


def _half_merge(bufs, name):
    nt = len(bufs)

    def body(*refs):
        outs = refs[nt:2 * nt]
        send_sems, recv_sems = refs[2 * nt:]
        x, y, c = _mesh_pos()

        def half(ref, h):
            rh = ref.shape[-2] // 2
            return ref.at[(slice(None),) * (len(ref.shape) - 2) + (pl.ds(h * rh, rh), slice(None))]

        cps = []
        for t in range(nt):
            cp = pltpu.make_async_remote_copy(
                src_ref=half(outs[t], c), dst_ref=half(outs[t], c), send_sem=send_sems.at[t], recv_sem=recv_sems.at[t],
                device_id=(x, y, 1 - c), device_id_type=MESH)
            cp.start()
            cps.append(cp)
        for t in range(nt):
            pltpu.make_async_remote_copy(
                src_ref=half(outs[t], 1 - c), dst_ref=half(outs[t], 1 - c), send_sem=send_sems.at[t], recv_sem=recv_sems.at[t],
                device_id=(x, y, 1 - c), device_id_type=MESH).wait_recv()
        for cp in cps:
            cp.wait_send()

    return pl.pallas_call(
        body, name=name, in_specs=[ANY] * nt, out_specs=[ANY] * nt, out_shape=[S_(b.shape, f32) for b in bufs],
        input_output_aliases={t: t for t in range(nt)},
        scratch_shapes=[pltpu.SemaphoreType.DMA((nt,)), pltpu.SemaphoreType.DMA((nt,))],
        compiler_params=_cp(VMEM_MID),
    )(*bufs)


def _local_step(cfg, x_lat, x_ctx, target, mods, norm_g, W, G, na_rpb, w_pool, pool_scale):
    S, L, T, D, F = cfg.S, cfg.L, cfg.T, cfg.D, cfg.F
    depth = norm_g.shape[0]
    cos, sin = _rope_tables(S, L)
    band, inv = _pool_tables(cfg.TM, L)
    flip, sel = _rpb_reduce_tables()

    assert depth == 2, "the carrier schedules below are written for two layers"
    fwd_carry = {"ffn_fwd_0_0": [("wi", 0), ("wo", 0), ("gu", 0, 1), ("dn", 0, 1)],
                 "na_fwd_0": [("gu", 1, 0), ("dn", 1, 0)],
                 "ffn_fwd_0_1": [("wi", 1), ("wo", 1), ("gu", 1, 1), ("dn", 1, 1)]}
    bwd_carry = {"na_bwd_1": [("gu", 1, 1), ("dn", 1, 1)], "ffn_bwd_1_0": [("wi", 1), ("wo", 1)],
                 "ffn_bwd_0_1": [("gu", 1, 0), ("dn", 1, 0)], "na_bwd_0": [("gu", 0, 1), ("dn", 0, 1)],
                 "ffn_bwd_0_0": [("wi", 0), ("wo", 0)], "wgrad_dn_0_0": [("gu", 0, 0)]}
    last_scatter = [("dn", 0, 0)]
    tag = lambda key: "_".join(str(p) for p in key)
    g_f32, g_b16 = {}, {}

    def gather_on(name):
        keys = fwd_carry.get(name)
        return None if keys is None else _gather_comm([W[k_] for k_ in keys], [W_AXIS[k_[0]] for k_ in keys])

    def gathered(name, res):
        if name in fwd_carry:
            W.update(zip(fwd_carry[name], res))

    def scatter_on(name):
        keys = bwd_carry.get(name)
        return None if keys is None else _scatter_comm([g_b16[k_] for k_ in keys], [W_AXIS[k_[0]] for k_ in keys])

    def scattered(keys, lands):
        for key, land in zip(keys, lands):
            G[key[0]] = _sum_devices8(g_f32[key], land, W_AXIS[key[0]], G[key[0]], key[1:], f"sum8_{tag(key)}")

    small_landed = []

    def wgrad(key, a, b, rows, other_comm=None):
        name = f"wgrad_{tag(key)}"
        if other_comm is not None:
            assert name not in bwd_carry
            (g_f32[key], g_b16[key]), res = _wgrad(a, b, rows, name, other_comm)
            small_landed.extend(res)
            return
        (g_f32[key], g_b16[key]), lands = _wgrad(a, b, rows, name, scatter_on(name))
        scattered(bwd_carry.get(name, ()), lands)

    saved = []
    xs, xs_ctx = x_lat, x_ctx
    for l in range(depth):
        last = l == depth - 1
        wc = not last
        gvec = norm_g[l]
        ps = pool_scale[l].reshape(1, POOL_WIDTH)
        bexp = _expand_rpb(na_rpb[l], f"bias_expand_{l}")
        name = f"ffn_fwd_{l}_0"
        (xs1, hb1, z1, y1), res = _ffn_fwd(cfg, xs, mods[l], gvec, W["gu", l, 0], W["dn", l, 0], 0, 0, True, name,
                                           gather_on(name), xs_ctx=xs_ctx)
        gathered(name, res)
        hb2, q, k, v, u = _tmpre_fwd(cfg, xs1, mods[l], gvec, W["wi", l], cos, sin, f"tmpre_fwd_{l}")
        name = f"na_fwd_{l}"
        (na_x, lse), res = _na_fwd(cfg, q, k, v, bexp, name, gather_on(name))
        gathered(name, res)
        na_c = _ctx_attn_fwd(cfg, q, k, v, f"ctx_attn_fwd_{l}") if wc else None
        xs2, opre, mix = _tmpost_fwd(cfg, na_x, na_c, u, band, inv, w_pool[l], ps, W["wo", l], xs1, mods[l], gvec,
                                     f"tmpost_fwd_{l}")
        name = f"ffn_fwd_{l}_1"
        outs, res = _ffn_fwd(cfg, xs2, mods[l], gvec, W["gu", l, 1], W["dn", l, 1], 6, 4, wc, name, gather_on(name),
                             loss_target=target if last else None)
        xs3, hb3, z3, y3 = outs[:4]
        gathered(name, res)
        saved.append(dict(xs=xs, xs_ctx=xs_ctx, xs1=xs1, xs2=xs2, hb1=hb1, z1=z1, y1=y1, hb2=hb2, q=q, k=k, v=v, u=u, mix=mix,
                          opre=opre, hb3=hb3, z3=z3, y3=y3, bexp=bexp, ps=ps, gvec=gvec, lse=lse))
        xs, xs_ctx = xs3, None

    dxs, loss_blk = xs, outs[4]

    small = [None] * depth
    for l in reversed(range(depth)):
        last = l == depth - 1
        wc = not last
        sv = saved[l]
        gvec = sv["gvec"]
        rows_b = cfg.T if wc else cfg.S
        name = f"ffn_bwd_{l}_1"
        (dxs2, dz, dyb, ab, dm678, dg45), lands = _ffn_bwd(cfg, dxs, sv["xs2"], sv["z3"], sv["y3"], mods[l], gvec,
                                                           W["gu", l, 1], W["dn", l, 1], 6, 4, wc, name, scatter_on(name))
        scattered(bwd_carry.get(name, ()), lands)
        wgrad(("gu", l, 1), sv["hb3"], dz, rows_b)
        wgrad(("dn", l, 1), ab, dyb, rows_b)
        dop, dmix, dm5, dg3 = _tmpost_bwd(cfg, dxs2, sv["opre"], W["wo", l], mods[l], gvec, wc, f"tmpost_bwd_{l}")
        wgrad(("wo", l), sv["mix"], dop, rows_b)
        du, dwp, dps = _pool_bwd(cfg, dmix, sv["u"], band, inv, w_pool[l], sv["ps"], wc, f"pool_bwd_{l}")
        name = f"na_bwd_{l}"
        (dq, dk, dv, dkc, dvc, dbexp), lands = _na_bwd(cfg, dmix, sv["mix"], sv["lse"], sv["q"], sv["k"], sv["v"], sv["bexp"],
                                                       name, scatter_on(name))
        scattered(bwd_carry.get(name, ()), lands)
        drpb = _rpb_reduce(dbexp, flip, sel, f"rpb_reduce_{l}")
        if wc:
            dqc, dkc2, dvc2 = _ctx_attn_bwd(cfg, dmix, sv["q"], sv["k"], sv["v"], f"ctx_attn_bwd_{l}")
            ctx_terms = ([dqc], [dkc, dkc2], [dvc, dvc2])
        else:
            ctx_terms = ([], [dkc], [dvc])
        dxs1, dproj, dm34, dg2 = _tmpre_bwd(cfg, (dq, dk, dv, du), ctx_terms, wc, cos, sin, W["wi", l], sv["xs1"], mods[l], gvec,
                                            dxs2, wc, f"tmpre_bwd_{l}")
        wgrad(("wi", l), sv["hb2"], dproj, cfg.T)
        name = f"ffn_bwd_{l}_0"
        (dxs, dz, dyb, ab, dm012, dg01), lands = _ffn_bwd(cfg, dxs1, sv["xs"], sv["z1"], sv["y1"], mods[l], gvec,
                                                          W["gu", l, 0], W["dn", l, 0], 0, 0, True, name, scatter_on(name),
                                                          xs_ctx=sv["xs_ctx"])
        scattered(bwd_carry.get(name, ()), lands)
        if not wc:
            zero = lambda a: jnp.concatenate([a, jnp.zeros_like(a)], axis=0)
            dm5, dm678 = zero(dm5), zero(dm678)
        dmods = jnp.concatenate([dm012, dm34, dm5, dm678], axis=1)
        dgs = jnp.concatenate([dg01, dg2, dg3, dg45], axis=0)
        small[l] = dict(dmods=dmods, dg=dgs, drpb=drpb, dwp=dwp, dps=dps)
        small_gather = None
        if l == 0:
            parts = [jnp.stack([small[j][n_] for j in range(depth)]) for n_ in SMALL_NAMES]
            packed = _pack_rows(parts)
            small_gather = _allgather_comm(packed)
        wgrad(("gu", l, 0), sv["hb1"], dz, cfg.T, small_gather)
        wgrad(("dn", l, 0), ab, dyb, cfg.T)
    last_comm = _scatter_comm([g_b16[k_] for k_ in last_scatter], [W_AXIS[k_[0]] for k_ in last_scatter])

    def finish_weight_grads(lands):
        scattered(last_scatter, lands)
        kinds = ("gu", "dn", "wi", "wo")
        return dict(zip(kinds, _half_merge([G[k_] for k_ in kinds], "merge_halves")))

    return loss_blk, dxs, (last_comm, finish_weight_grads), (packed, [p.shape for p in parts], small_landed[0])


def kernel(x, c, ctx, c_ctx, w_mod, b_mod, norm_g, w_ffn_gate_up, w_ffn_down, w_in, w_out, na_rpb, w_pool, pool_scale, loss_target, m_c_ctx, m_w_mod, m_b_mod, m_norm_g, m_w_ffn_gate_up, m_w_ffn_down, m_w_in, m_w_out, m_na_rpb, m_w_pool, m_pool_scale, v_c_ctx, v_w_mod, v_b_mod, v_norm_g, v_w_ffn_gate_up, v_w_ffn_down, v_w_in, v_w_out, v_na_rpb, v_w_pool, v_pool_scale):
    S, D = x.shape[1], x.shape[2]
    L = ctx.shape[1]
    depth = w_mod.shape[0]
    F = w_ffn_down.shape[2] * N_CHIPS
    nmod = w_mod.shape[2]
    gsh = norm_g.shape[2]
    cfg = _Cfg(S, L, D, F)
    mx, my, mc = _mesh_pos()
    chip = 2 * mx + my
    dev = 4 * mx + 2 * my + mc

    W = {}
    for l in range(depth):
        for i in range(2):
            W["gu", l, i] = _cast_into_place(w_ffn_gate_up, (l, i), W_AXIS["gu"], f"cast_gu_{l}_{i}")
            W["dn", l, i] = _cast_into_place(w_ffn_down, (l, i), W_AXIS["dn"], f"cast_dn_{l}_{i}")
        W["wi", l] = _cast_into_place(w_in, (l,), W_AXIS["wi"], f"cast_wi_{l}")
        W["wo", l] = _cast_into_place(w_out, (l,), W_AXIS["wo"], f"cast_wo_{l}")
    first = [("gu", 0, 0), ("dn", 0, 0)]
    W.update(zip(first, _comm_only(_gather_comm([W[k_] for k_ in first], [W_AXIS[k_[0]] for k_ in first]), "gather_first")))
    G = {"gu": lax.empty(w_ffn_gate_up.shape, f32), "dn": lax.empty(w_ffn_down.shape, f32),
         "wi": lax.empty(w_in.shape, f32), "wo": lax.empty(w_out.shape, f32)}

    cg_packed = _pack_rows([c, norm_g])
    cg_all = _all_gather_small(cg_packed, "gather_c_norm_g").reshape(N_DEV, -1)
    c_all = cg_all[:, :D]
    ng = cg_all[:, D:D + norm_g.size].reshape(N_DEV, depth, 6, gsh)
    norm_g_all = jnp.concatenate([ng[2 * j] for j in range(N_CHIPS)], axis=-1)
    cvecs = jnp.concatenate([c_all, c_ctx[None], jnp.zeros((7, D), f32)], axis=0)
    b_shard = lax.dynamic_slice_in_dim(b_mod, chip * nmod, nmod, axis=1).reshape(depth, 1, nmod)
    m_part, silu_c = _modvec_fwd(cvecs, w_mod, b_shard, "modvec_fwd")
    m_all = _all_gather_small(m_part.reshape(depth * 16, nmod), "gather_mod").reshape(N_DEV, depth, 16, nmod)
    m_full = jnp.concatenate([m_all[2 * j] for j in range(N_CHIPS)], axis=-1)
    m_mine = lax.dynamic_index_in_dim(m_full, dev, axis=1, keepdims=False)
    mods = jnp.stack([m_mine, m_full[:, 8]], axis=1).reshape(depth, 2, N_MOD, D)

    loss_blk, dx_lat, (last_comm, finish_weight_grads), small = _local_step(
        cfg, x[0], ctx[0], loss_target[0], mods, norm_g_all, W, G, na_rpb, w_pool, pool_scale)
    loss = lax.psum(loss_blk[0, 0], ("x", "y", "c"))
    grad_x = dx_lat[None]

    packed, shapes, landed = small
    gathered = lax.dynamic_update_index_in_dim(landed, packed, dev, 0)
    total = _unpack_rows(_sum_devices(gathered, "sum_small"), shapes)
    dmods_sum, dg_sum, drpb_sum, dwp_sum, dps_sum = total
    dmods_each = jnp.stack([_unpack_rows(gathered[j], shapes[:1])[0] for j in range(N_DEV)])
    dm_rows = jnp.concatenate([jnp.transpose(dmods_each[:, :, 0], (1, 0, 2, 3)).reshape(depth, N_DEV, N_MOD * D),
                               dmods_sum[:, 1].reshape(depth, 1, N_MOD * D),
                               jnp.zeros((depth, 7, N_MOD * D), f32)], axis=1)
    dm_shard = lax.dynamic_slice_in_dim(dm_rows, chip * nmod, nmod, axis=2)
    grad_w_mod, gc_part = _modvec_bwd(silu_c.T, dm_shard, w_mod, "modvec_bwd")
    gc_all = _all_gather_small(gc_part.reshape(depth * 8, D), "gather_gc").reshape(N_DEV, depth, 8, D)
    grad_b_mod, grad_c_ctx = _small_finish(dm_rows, gc_all, c_ctx)
    grad_norm_g = lax.dynamic_slice_in_dim(dg_sum, chip * gsh, gsh, axis=2)
    grad_na_rpb = drpb_sum[:, :, :2 * NA_KH - 1, :2 * NA_KW - 1]
    grad_w_pool = dwp_sum
    grad_pool_scale = dps_sum.reshape(depth, POOL_WIDTH)

    upd_w_mod, lands = _adamw(w_mod, grad_w_mod, m_w_mod, v_w_mod, "adamw_w_mod", comm=last_comm)
    wgrads = finish_weight_grads(lands)
    g_gu, g_dn, g_wi, g_wo = wgrads["gu"], wgrads["dn"], wgrads["wi"], wgrads["wo"]
    grads = [grad_c_ctx, grad_w_mod, grad_b_mod, grad_norm_g, g_gu, g_dn, g_wi, g_wo, grad_na_rpb, grad_w_pool, grad_pool_scale]
    ws = [c_ctx, w_mod, b_mod, norm_g, w_ffn_gate_up, w_ffn_down, w_in, w_out, na_rpb, w_pool, pool_scale]
    ms = [m_c_ctx, m_w_mod, m_b_mod, m_norm_g, m_w_ffn_gate_up, m_w_ffn_down, m_w_in, m_w_out, m_na_rpb, m_w_pool, m_pool_scale]
    vs = [v_c_ctx, v_w_mod, v_b_mod, v_norm_g, v_w_ffn_gate_up, v_w_ffn_down, v_w_in, v_w_out, v_na_rpb, v_w_pool, v_pool_scale]
    tags = ["c_ctx", "w_mod", "b_mod", "norm_g", "gate_up", "down", "w_in", "w_out", "na_rpb", "w_pool", "pool_scale"]
    merged = ("gate_up", "down", "w_in", "w_out")
    upd = [upd_w_mod if t == "w_mod" else _adamw(w_, g_, m_, v_, f"adamw_{t}", emit_grad=t in merged)
           for w_, g_, m_, v_, t in zip(ws, grads, ms, vs, tags)]
    grads = [u_[3] if t in merged else g_ for g_, u_, t in zip(grads, upd, tags)]
    return (loss, grad_x, *grads, *[u_[0] for u_ in upd], *[u_[1] for u_ in upd], *[u_[2] for u_ in upd])


def _small_finish(dm_rows, gc_all, c_ctx):
    depth, _, n = dm_rows.shape
    D = c_ctx.shape[0]

    def body(dm_ref, gc_ref, c_ref, gb_ref, gcx_ref):
        acc = dm_ref[:, 0]
        for j in range(1, N_DEV + 1):
            acc = acc + dm_ref[:, j]
        gb_ref[...] = acc
        t = jnp.zeros((1, D), f32)
        for l in range(depth):
            for j in range(N_CHIPS):
                t = t + gc_ref[2 * j, l, 0:1, :]
        cv = c_ref[...]
        sg = _sigmoid(cv)
        gcx_ref[...] = t * (sg * (1.0 + cv * (1.0 - sg)))

    gb, gcx = pl.pallas_call(
        body, name="small_finish",
        out_shape=[S_((depth, n), f32), S_((1, D), f32)],
        compiler_params=_cp(VMEM_MID),
    )(dm_rows, gc_all, c_ctx.reshape(1, D))
    return gb, gcx.reshape(D)
```
